```python
import jax, jax.numpy as jnp
from jax import lax
import numpy as np


D_MODEL = 1024
BATCH = 8
SEQ = 4096
DEPTH = 4

D_MIX = D_MODEL
D_CONV = D_MIX // 4
D_CONF = D_MIX // 4
D_DN = D_MIX // 2
N_CONV_GROUPS = 4
N_CONF_GROUPS = 4
DN_HEADS = 4
DN_HEAD_DIM = D_DN // DN_HEADS
SHORT_CONV_W = 3
CONF_CONV_W = 31
DN_CONV_W = 4
DN_CHUNK = 64
D_FF = ((8 * D_MODEL + 3 * 256 - 1) // (3 * 256)) * 256
IN_COLS = 3 * D_CONV + 2 * D_CONF + 4 * D_DN + 2 * DN_HEADS
N_MOD = 6
EPS = 1e-6

kernel_name = 'hymba_conv_conformer_gdn_adaln_trunk'


def rmsnorm(x, g):
    xf = x.astype(jnp.float32)
    y = xf * lax.rsqrt(jnp.mean(xf * xf, axis=-1, keepdims=True) + EPS)
    return y.astype(x.dtype) * g


def layernorm(x, g, b):
    xf = x.astype(jnp.float32)
    mu = jnp.mean(xf, axis=-1, keepdims=True)
    var = jnp.mean(jnp.square(xf - mu), axis=-1, keepdims=True)
    return ((xf - mu) * lax.rsqrt(var + 1e-5)).astype(x.dtype) * g + b


def causal_dwconv(x, w):
    K, C = w.shape
    xp = jnp.pad(x, ((0, 0), (K - 1, 0), (0, 0)))
    return lax.conv_general_dilated(xp, w[:, None, :].astype(x.dtype), window_strides=(1,), padding='VALID',
                                    dimension_numbers=('NWC', 'WIO', 'NWC'), feature_group_count=C)


def l2norm(x):
    return x * lax.rsqrt(jnp.sum(x * x, axis=-1, keepdims=True) + EPS)


def chunk_gated_delta_rule(q, k, v, g, beta):
    Bsz, T, H, Dk = q.shape
    C = DN_CHUNK
    N = T // C
    def to_chunks(t):
        return t.reshape(Bsz, N, C, H, -1).transpose(0, 3, 1, 2, 4)
    q = to_chunks(q) * (Dk ** -0.5)
    k = to_chunks(k)
    v = to_chunks(v)
    beta = beta.reshape(Bsz, N, C, H).transpose(0, 3, 1, 2)
    g = jnp.cumsum(g.reshape(Bsz, N, C, H).transpose(0, 3, 1, 2), axis=-1)
    causal = jnp.tril(jnp.ones((C, C), dtype=bool))
    strict = jnp.tril(jnp.ones((C, C), dtype=bool), -1)
    diff = g[..., :, None] - g[..., None, :]
    decay = jnp.where(causal, jnp.exp(jnp.where(causal, diff, 0.0)), 0.0)
    k_beta = k * beta[..., None]
    v_beta = v * beta[..., None]
    Lm = jnp.where(strict, jnp.einsum('bhncd,bhnsd->bhncs', k_beta, k) * decay, 0.0)
    eye = jnp.eye(C, dtype=q.dtype)
    Tm = lax.linalg.triangular_solve(eye + Lm, jnp.broadcast_to(eye, Lm.shape), left_side=True,
                                     lower=True, unit_diagonal=True)
    u = jnp.einsum('bhncs,bhnse->bhnce', Tm, v_beta)
    w = jnp.einsum('bhncs,bhnsd->bhncd', Tm, k_beta * jnp.exp(g)[..., None])
    qk = jnp.where(causal, jnp.einsum('bhncd,bhnsd->bhncs', q, k) * decay, 0.0)

    def step(S, inp):
        q_i, k_i, u_i, w_i, g_i, qk_i = inp
        v_new = u_i - jnp.einsum('bhcd,bhde->bhce', w_i, S)
        o = (jnp.einsum('bhcd,bhde->bhce', q_i * jnp.exp(g_i)[..., None], S)
             + jnp.einsum('bhcs,bhse->bhce', qk_i, v_new))
        g_last = g_i[..., -1]
        S = (S * jnp.exp(g_last)[..., None, None]
             + jnp.einsum('bhcd,bhce->bhde', k_i * jnp.exp(g_last[..., None] - g_i)[..., None], v_new))
        return S, o

    xs = tuple(jnp.moveaxis(t, 2, 0) for t in (q, k, u, w, g, qk))
    S0 = jnp.zeros((Bsz, H, Dk, v.shape[-1]), jnp.float32)
    _, o = lax.scan(step, S0, xs)
    return o.transpose(1, 0, 3, 2, 4).reshape(Bsz, T, H, -1)


def _fwd_setup_inputs(seed: int = 0) -> dict:
    key = jax.random.key(seed)
    ks = jax.random.split(key, 24)
    f32 = jnp.float32
    def nrm(k, shape, scale):
        return jax.random.normal(k, shape, f32) * scale
    L, D = DEPTH, D_MODEL
    dt = jax.random.uniform(ks[13], (L, DN_HEADS), f32, minval=1e-3, maxval=0.1)
    return {
        'x': nrm(ks[0], (BATCH, SEQ, D), 1.0),
        'c': nrm(ks[1], (BATCH, D), 1.0),
        'w_ada': nrm(ks[2], (L, D, N_MOD * D), 0.5 * D ** -0.5),
        'b_ada': nrm(ks[3], (L, N_MOD * D), 0.02),
        'norm_mix_g': 1.0 + nrm(ks[4], (L, D), 0.02),
        'norm_ffn_g': 1.0 + nrm(ks[5], (L, D), 0.02),
        'w_in': nrm(ks[6], (L, D, IN_COLS), D ** -0.5),
        'conv_a_w': nrm(ks[7], (L, SHORT_CONV_W, D_CONV), SHORT_CONV_W ** -0.5),
        'conf_dw_w': nrm(ks[8], (L, CONF_CONV_W, D_CONF), CONF_CONV_W ** -0.5),
        'conf_dw_b': nrm(ks[9], (L, D_CONF), 0.02),
        'conf_ln_g': 1.0 + nrm(ks[10], (L, D_CONF), 0.02),
        'conf_ln_b': nrm(ks[11], (L, D_CONF), 0.02),
        'dn_conv_w': nrm(ks[12], (L, DN_CONV_W, 3 * D_DN), DN_CONV_W ** -0.5),
        'dn_a_log': jnp.log(jax.random.uniform(ks[14], (L, DN_HEADS), f32, minval=1.0, maxval=16.0)),
        'dn_dt_bias': dt + jnp.log(-jnp.expm1(-dt)),
        'dn_norm_g': 1.0 + nrm(ks[15], (L, DN_HEAD_DIM), 0.02),
        'w_out': nrm(ks[16], (L, D_MIX, D), D_MIX ** -0.5),
        'w_ffn_in': nrm(ks[17], (L, D, 2 * D_FF), D ** -0.5),
        'w_ffn_out': nrm(ks[18], (L, D_FF, D), D_FF ** -0.5),
        'final_norm_g': 1.0 + nrm(ks[19], (D,), 0.02),
    }


def _fwd_reference(x, c, w_ada, b_ada, norm_mix_g, norm_ffn_g, w_in, conv_a_w, conf_dw_w, conf_dw_b,
              conf_ln_g, conf_ln_b, dn_conv_w, dn_a_log, dn_dt_bias, dn_norm_g, w_out,
              w_ffn_in, w_ffn_out, final_norm_g):
    Bsz, T, _ = x.shape
    c_act = jax.nn.silu(c)
    sizes = [D_CONV] * 3 + [D_CONF] * 2 + [D_DN] * 4 + [DN_HEADS] * 2
    split_idx = np.cumsum(sizes)[:-1].tolist()
    for l in range(DEPTH):
        mod = (c_act @ w_ada[l] + b_ada[l])[:, None, :]
        sh1, sc1, g1, sh2, sc2, g2 = jnp.split(mod, N_MOD, axis=-1)

        h = rmsnorm(x, norm_mix_g[l]) * (1.0 + sc1) + sh1
        proj = h @ w_in[l]
        (a_b, a_c, a_v, b_a, b_g, c_q, c_k, c_v, c_z, c_alpha, c_beta) = jnp.split(proj, split_idx, axis=-1)

        y_a = a_b * causal_dwconv(a_c * a_v, conv_a_w[l])

        u = b_a * jax.nn.sigmoid(b_g)
        u = causal_dwconv(u, conf_dw_w[l]) + conf_dw_b[l]
        y_b = jax.nn.silu(layernorm(u, conf_ln_g[l], conf_ln_b[l]))

        qkv = jax.nn.silu(causal_dwconv(jnp.concatenate([c_q, c_k, c_v], axis=-1), dn_conv_w[l]))
        q, k, v = jnp.split(qkv.astype(jnp.float32), 3, axis=-1)
        q = l2norm(q.reshape(Bsz, T, DN_HEADS, DN_HEAD_DIM))
        k = l2norm(k.reshape(Bsz, T, DN_HEADS, DN_HEAD_DIM))
        v = v.reshape(Bsz, T, DN_HEADS, DN_HEAD_DIM)
        gdec = -jnp.exp(dn_a_log[l].astype(jnp.float32)) * jax.nn.softplus(
            c_alpha.astype(jnp.float32) + dn_dt_bias[l].astype(jnp.float32))
        beta = jax.nn.sigmoid(c_beta.astype(jnp.float32))
        o = chunk_gated_delta_rule(q, k, v, gdec, beta).astype(x.dtype)
        z = c_z.reshape(Bsz, T, DN_HEADS, DN_HEAD_DIM)
        y_c = (rmsnorm(o, dn_norm_g[l]) * jax.nn.silu(z)).reshape(Bsz, T, D_DN)

        mix = jnp.concatenate([y_a, y_b, y_c], axis=-1) @ w_out[l]
        x = x + g1 * mix

        h = rmsnorm(x, norm_ffn_g[l]) * (1.0 + sc2) + sh2
        gate, up = jnp.split(h @ w_ffn_in[l], 2, axis=-1)
        x = x + g2 * ((jax.nn.silu(gate) * up) @ w_ffn_out[l])
    return rmsnorm(x, final_norm_g)


import jax as _jax
import jax.numpy as _jnp

TWIN_FORMAT = 'train_step'
FWD_PARAMS = ['x', 'c', 'w_ada', 'b_ada', 'norm_mix_g', 'norm_ffn_g', 'w_in', 'conv_a_w', 'conf_dw_w', 'conf_dw_b', 'conf_ln_g', 'conf_ln_b', 'dn_conv_w', 'dn_a_log', 'dn_dt_bias', 'dn_norm_g', 'w_out', 'w_ffn_in', 'w_ffn_out', 'final_norm_g']
TWIN_WEIGHTS = ['w_ada', 'b_ada', 'norm_mix_g', 'norm_ffn_g', 'w_in', 'conv_a_w', 'conf_dw_w', 'conf_dw_b', 'conf_ln_g', 'conf_ln_b', 'dn_conv_w', 'dn_a_log', 'dn_dt_bias', 'dn_norm_g', 'w_out', 'w_ffn_in', 'w_ffn_out', 'final_norm_g']
TWIN_DIFF_INPUT = 'x'
TWIN_INPUTS = ['x', 'c', 'w_ada', 'b_ada', 'norm_mix_g', 'norm_ffn_g', 'w_in', 'conv_a_w', 'conf_dw_w', 'conf_dw_b', 'conf_ln_g', 'conf_ln_b', 'dn_conv_w', 'dn_a_log', 'dn_dt_bias', 'dn_norm_g', 'w_out', 'w_ffn_in', 'w_ffn_out', 'final_norm_g', 'loss_target', 'm_w_ada', 'm_b_ada', 'm_norm_mix_g', 'm_norm_ffn_g', 'm_w_in', 'm_conv_a_w', 'm_conf_dw_w', 'm_conf_dw_b', 'm_conf_ln_g', 'm_conf_ln_b', 'm_dn_conv_w', 'm_dn_a_log', 'm_dn_dt_bias', 'm_dn_norm_g', 'm_w_out', 'm_w_ffn_in', 'm_w_ffn_out', 'm_final_norm_g', 'v_w_ada', 'v_b_ada', 'v_norm_mix_g', 'v_norm_ffn_g', 'v_w_in', 'v_conv_a_w', 'v_conf_dw_w', 'v_conf_dw_b', 'v_conf_ln_g', 'v_conf_ln_b', 'v_dn_conv_w', 'v_dn_a_log', 'v_dn_dt_bias', 'v_dn_norm_g', 'v_w_out', 'v_w_ffn_in', 'v_w_ffn_out', 'v_final_norm_g']
TWIN_OUTPUTS = ['loss', 'grad_x', 'grad_w_ada', 'grad_b_ada', 'grad_norm_mix_g', 'grad_norm_ffn_g', 'grad_w_in', 'grad_conv_a_w', 'grad_conf_dw_w', 'grad_conf_dw_b', 'grad_conf_ln_g', 'grad_conf_ln_b', 'grad_dn_conv_w', 'grad_dn_a_log', 'grad_dn_dt_bias', 'grad_dn_norm_g', 'grad_w_out', 'grad_w_ffn_in', 'grad_w_ffn_out', 'grad_final_norm_g', 'delta_w_ada', 'delta_b_ada', 'delta_norm_mix_g', 'delta_norm_ffn_g', 'delta_w_in', 'delta_conv_a_w', 'delta_conf_dw_w', 'delta_conf_dw_b', 'delta_conf_ln_g', 'delta_conf_ln_b', 'delta_dn_conv_w', 'delta_dn_a_log', 'delta_dn_dt_bias', 'delta_dn_norm_g', 'delta_w_out', 'delta_w_ffn_in', 'delta_w_ffn_out', 'delta_final_norm_g', 'new_m_w_ada', 'new_m_b_ada', 'new_m_norm_mix_g', 'new_m_norm_ffn_g', 'new_m_w_in', 'new_m_conv_a_w', 'new_m_conf_dw_w', 'new_m_conf_dw_b', 'new_m_conf_ln_g', 'new_m_conf_ln_b', 'new_m_dn_conv_w', 'new_m_dn_a_log', 'new_m_dn_dt_bias', 'new_m_dn_norm_g', 'new_m_w_out', 'new_m_w_ffn_in', 'new_m_w_ffn_out', 'new_m_final_norm_g', 'new_v_w_ada', 'new_v_b_ada', 'new_v_norm_mix_g', 'new_v_norm_ffn_g', 'new_v_w_in', 'new_v_conv_a_w', 'new_v_conf_dw_w', 'new_v_conf_dw_b', 'new_v_conf_ln_g', 'new_v_conf_ln_b', 'new_v_dn_conv_w', 'new_v_dn_a_log', 'new_v_dn_dt_bias', 'new_v_dn_norm_g', 'new_v_w_out', 'new_v_w_ffn_in', 'new_v_w_ffn_out', 'new_v_final_norm_g']
TWIN_LEAF_KINDS = {'loss': 'loss', 'grad_x': 'grad_x', 'grad_w_ada': 'grad_w', 'grad_b_ada': 'grad_w', 'grad_norm_mix_g': 'grad_w', 'grad_norm_ffn_g': 'grad_w', 'grad_w_in': 'grad_w', 'grad_conv_a_w': 'grad_w', 'grad_conf_dw_w': 'grad_w', 'grad_conf_dw_b': 'grad_w', 'grad_conf_ln_g': 'grad_w', 'grad_conf_ln_b': 'grad_w', 'grad_dn_conv_w': 'grad_w', 'grad_dn_a_log': 'grad_w', 'grad_dn_dt_bias': 'grad_w', 'grad_dn_norm_g': 'grad_w', 'grad_w_out': 'grad_w', 'grad_w_ffn_in': 'grad_w', 'grad_w_ffn_out': 'grad_w', 'grad_final_norm_g': 'grad_w', 'delta_w_ada': 'delta_w', 'delta_b_ada': 'delta_w', 'delta_norm_mix_g': 'delta_w', 'delta_norm_ffn_g': 'delta_w', 'delta_w_in': 'delta_w', 'delta_conv_a_w': 'delta_w', 'delta_conf_dw_w': 'delta_w', 'delta_conf_dw_b': 'delta_w', 'delta_conf_ln_g': 'delta_w', 'delta_conf_ln_b': 'delta_w', 'delta_dn_conv_w': 'delta_w', 'delta_dn_a_log': 'delta_w', 'delta_dn_dt_bias': 'delta_w', 'delta_dn_norm_g': 'delta_w', 'delta_w_out': 'delta_w', 'delta_w_ffn_in': 'delta_w', 'delta_w_ffn_out': 'delta_w', 'delta_final_norm_g': 'delta_w', 'new_m_w_ada': 'new_m', 'new_m_b_ada': 'new_m', 'new_m_norm_mix_g': 'new_m', 'new_m_norm_ffn_g': 'new_m', 'new_m_w_in': 'new_m', 'new_m_conv_a_w': 'new_m', 'new_m_conf_dw_w': 'new_m', 'new_m_conf_dw_b': 'new_m', 'new_m_conf_ln_g': 'new_m', 'new_m_conf_ln_b': 'new_m', 'new_m_dn_conv_w': 'new_m', 'new_m_dn_a_log': 'new_m', 'new_m_dn_dt_bias': 'new_m', 'new_m_dn_norm_g': 'new_m', 'new_m_w_out': 'new_m', 'new_m_w_ffn_in': 'new_m', 'new_m_w_ffn_out': 'new_m', 'new_m_final_norm_g': 'new_m', 'new_v_w_ada': 'new_v', 'new_v_b_ada': 'new_v', 'new_v_norm_mix_g': 'new_v', 'new_v_norm_ffn_g': 'new_v', 'new_v_w_in': 'new_v', 'new_v_conv_a_w': 'new_v', 'new_v_conf_dw_w': 'new_v', 'new_v_conf_dw_b': 'new_v', 'new_v_conf_ln_g': 'new_v', 'new_v_conf_ln_b': 'new_v', 'new_v_dn_conv_w': 'new_v', 'new_v_dn_a_log': 'new_v', 'new_v_dn_dt_bias': 'new_v', 'new_v_dn_norm_g': 'new_v', 'new_v_w_out': 'new_v', 'new_v_w_ffn_in': 'new_v', 'new_v_w_ffn_out': 'new_v', 'new_v_final_norm_g': 'new_v'}


def _forward(args):
    return _fwd_reference(*[args[k] for k in FWD_PARAMS])


def _output_shape():
    def fwd():
        inp = _fwd_setup_inputs(0)
        return _fwd_reference(*[inp[k] for k in FWD_PARAMS])
    out = _jax.eval_shape(fwd)
    return out.shape, out.dtype

N_MICROBATCH = 1
ADAM_LR = 0.001
ADAM_B1 = 0.9
ADAM_B2 = 0.999
ADAM_EPS = 1e-08
ADAM_WD = 0.01
ADAM_STEP = 10
PER_EXAMPLE_BATCH_AXIS = {'x': 0, 'c': 0, 'loss_target': 0}
SHARED_INPUTS = []
_WEIGHT_DTYPES = {'w_ada': _jnp.float32, 'b_ada': _jnp.float32, 'norm_mix_g': _jnp.float32, 'norm_ffn_g': _jnp.float32, 'w_in': _jnp.float32, 'conv_a_w': _jnp.float32, 'conf_dw_w': _jnp.float32, 'conf_dw_b': _jnp.float32, 'conf_ln_g': _jnp.float32, 'conf_ln_b': _jnp.float32, 'dn_conv_w': _jnp.float32, 'dn_a_log': _jnp.float32, 'dn_dt_bias': _jnp.float32, 'dn_norm_g': _jnp.float32, 'w_out': _jnp.float32, 'w_ffn_in': _jnp.float32, 'w_ffn_out': _jnp.float32, 'final_norm_g': _jnp.float32}
MOMENT_SCALE = {'w_ada': 5.771345e-02, 'b_ada': 9.879317e-02, 'norm_mix_g': 7.121291e-02, 'norm_ffn_g': 5.149051e-02, 'w_in': 4.126328e-02, 'conv_a_w': 7.135890e-02, 'conf_dw_w': 3.357785e-02, 'conf_dw_b': 6.400623e-02, 'conf_ln_g': 3.890024e-02, 'conf_ln_b': 3.562829e-02, 'dn_conv_w': 2.548478e-02, 'dn_a_log': 1.647123e-01, 'dn_dt_bias': 1.501723e-01, 'dn_norm_g': 7.441299e-02, 'w_out': 4.518446e-02, 'w_ffn_in': 2.295249e-02, 'w_ffn_out': 3.743371e-02, 'final_norm_g': 3.205174e+01}


def _to_microbatches(a, axis):
    t = _jnp.moveaxis(a, axis, 0)
    t = t.reshape((N_MICROBATCH, t.shape[0] // N_MICROBATCH) + t.shape[1:])
    return _jnp.moveaxis(t, 1, axis + 1)


def setup_inputs(seed: int = 0) -> dict:
    inp = _fwd_setup_inputs(seed)
    key = _jax.random.fold_in(_jax.random.key(seed), 7919)
    shape, _ = _output_shape()
    out = dict(inp)
    out["loss_target"] = _jax.random.normal(_jax.random.fold_in(key, 0), shape, _jnp.float32)
    for i, name in enumerate(TWIN_WEIGHTS):
        w = inp[name].astype(_jnp.float32)
        if MOMENT_SCALE is None:
            s = _jnp.sqrt(_jnp.mean(_jnp.square(w)) + 1e-30)
        else:
            s = MOMENT_SCALE[name]
        km, kv = _jax.random.split(_jax.random.fold_in(key, i + 1))
        out[name] = w
        out["m_" + name] = s * _jax.random.normal(km, w.shape, _jnp.float32)
        out["v_" + name] = (s * s) * _jax.random.uniform(kv, w.shape, _jnp.float32, 0.5, 1.5)
    if N_MICROBATCH > 1:
        for name, axis in PER_EXAMPLE_BATCH_AXIS.items():
            out[name] = _to_microbatches(out[name], axis)
    return {'x': out['x'], 'c': out['c'], 'w_ada': out['w_ada'], 'b_ada': out['b_ada'], 'norm_mix_g': out['norm_mix_g'], 'norm_ffn_g': out['norm_ffn_g'], 'w_in': out['w_in'], 'conv_a_w': out['conv_a_w'], 'conf_dw_w': out['conf_dw_w'], 'conf_dw_b': out['conf_dw_b'], 'conf_ln_g': out['conf_ln_g'], 'conf_ln_b': out['conf_ln_b'], 'dn_conv_w': out['dn_conv_w'], 'dn_a_log': out['dn_a_log'], 'dn_dt_bias': out['dn_dt_bias'], 'dn_norm_g': out['dn_norm_g'], 'w_out': out['w_out'], 'w_ffn_in': out['w_ffn_in'], 'w_ffn_out': out['w_ffn_out'], 'final_norm_g': out['final_norm_g'], 'loss_target': out['loss_target'], 'm_w_ada': out['m_w_ada'], 'm_b_ada': out['m_b_ada'], 'm_norm_mix_g': out['m_norm_mix_g'], 'm_norm_ffn_g': out['m_norm_ffn_g'], 'm_w_in': out['m_w_in'], 'm_conv_a_w': out['m_conv_a_w'], 'm_conf_dw_w': out['m_conf_dw_w'], 'm_conf_dw_b': out['m_conf_dw_b'], 'm_conf_ln_g': out['m_conf_ln_g'], 'm_conf_ln_b': out['m_conf_ln_b'], 'm_dn_conv_w': out['m_dn_conv_w'], 'm_dn_a_log': out['m_dn_a_log'], 'm_dn_dt_bias': out['m_dn_dt_bias'], 'm_dn_norm_g': out['m_dn_norm_g'], 'm_w_out': out['m_w_out'], 'm_w_ffn_in': out['m_w_ffn_in'], 'm_w_ffn_out': out['m_w_ffn_out'], 'm_final_norm_g': out['m_final_norm_g'], 'v_w_ada': out['v_w_ada'], 'v_b_ada': out['v_b_ada'], 'v_norm_mix_g': out['v_norm_mix_g'], 'v_norm_ffn_g': out['v_norm_ffn_g'], 'v_w_in': out['v_w_in'], 'v_conv_a_w': out['v_conv_a_w'], 'v_conf_dw_w': out['v_conf_dw_w'], 'v_conf_dw_b': out['v_conf_dw_b'], 'v_conf_ln_g': out['v_conf_ln_g'], 'v_conf_ln_b': out['v_conf_ln_b'], 'v_dn_conv_w': out['v_dn_conv_w'], 'v_dn_a_log': out['v_dn_a_log'], 'v_dn_dt_bias': out['v_dn_dt_bias'], 'v_dn_norm_g': out['v_dn_norm_g'], 'v_w_out': out['v_w_out'], 'v_w_ffn_in': out['v_w_ffn_in'], 'v_w_ffn_out': out['v_w_ffn_out'], 'v_final_norm_g': out['v_final_norm_g']}


def _loss(weights, diff, rest, loss_target):
    with _jax.named_scope("forward"):
        args = {**rest, TWIN_DIFF_INPUT: diff, **{k: w.astype(_WEIGHT_DTYPES[k]) for k, w in weights.items()}}
        y = _forward(args)
    with _jax.named_scope("loss_head"):
        err = _jnp.square(y.astype(_jnp.float32) - loss_target)
        return 0.5 * _jnp.sum(_jnp.mean(err, axis=-1)) if err.ndim else 0.5 * err


def _adamw(w, g, m, v):
    m = ADAM_B1 * m + (1.0 - ADAM_B1) * g
    v = ADAM_B2 * v + (1.0 - ADAM_B2) * _jnp.square(g)
    m_hat = m / (1.0 - ADAM_B1 ** ADAM_STEP)
    v_hat = v / (1.0 - ADAM_B2 ** ADAM_STEP)
    delta = -ADAM_LR * (m_hat / (_jnp.sqrt(v_hat) + ADAM_EPS) + ADAM_WD * w)
    return delta, m, v


def reference(x, c, w_ada, b_ada, norm_mix_g, norm_ffn_g, w_in, conv_a_w, conf_dw_w, conf_dw_b, conf_ln_g, conf_ln_b, dn_conv_w, dn_a_log, dn_dt_bias, dn_norm_g, w_out, w_ffn_in, w_ffn_out, final_norm_g, loss_target, m_w_ada, m_b_ada, m_norm_mix_g, m_norm_ffn_g, m_w_in, m_conv_a_w, m_conf_dw_w, m_conf_dw_b, m_conf_ln_g, m_conf_ln_b, m_dn_conv_w, m_dn_a_log, m_dn_dt_bias, m_dn_norm_g, m_w_out, m_w_ffn_in, m_w_ffn_out, m_final_norm_g, v_w_ada, v_b_ada, v_norm_mix_g, v_norm_ffn_g, v_w_in, v_conv_a_w, v_conf_dw_w, v_conf_dw_b, v_conf_ln_g, v_conf_ln_b, v_dn_conv_w, v_dn_a_log, v_dn_dt_bias, v_dn_norm_g, v_w_out, v_w_ffn_in, v_w_ffn_out, v_final_norm_g):
    given = dict(x=x, c=c, w_ada=w_ada, b_ada=b_ada, norm_mix_g=norm_mix_g, norm_ffn_g=norm_ffn_g, w_in=w_in, conv_a_w=conv_a_w, conf_dw_w=conf_dw_w, conf_dw_b=conf_dw_b, conf_ln_g=conf_ln_g, conf_ln_b=conf_ln_b, dn_conv_w=dn_conv_w, dn_a_log=dn_a_log, dn_dt_bias=dn_dt_bias, dn_norm_g=dn_norm_g, w_out=w_out, w_ffn_in=w_ffn_in, w_ffn_out=w_ffn_out, final_norm_g=final_norm_g, loss_target=loss_target, m_w_ada=m_w_ada, m_b_ada=m_b_ada, m_norm_mix_g=m_norm_mix_g, m_norm_ffn_g=m_norm_ffn_g, m_w_in=m_w_in, m_conv_a_w=m_conv_a_w, m_conf_dw_w=m_conf_dw_w, m_conf_dw_b=m_conf_dw_b, m_conf_ln_g=m_conf_ln_g, m_conf_ln_b=m_conf_ln_b, m_dn_conv_w=m_dn_conv_w, m_dn_a_log=m_dn_a_log, m_dn_dt_bias=m_dn_dt_bias, m_dn_norm_g=m_dn_norm_g, m_w_out=m_w_out, m_w_ffn_in=m_w_ffn_in, m_w_ffn_out=m_w_ffn_out, m_final_norm_g=m_final_norm_g, v_w_ada=v_w_ada, v_b_ada=v_b_ada, v_norm_mix_g=v_norm_mix_g, v_norm_ffn_g=v_norm_ffn_g, v_w_in=v_w_in, v_conv_a_w=v_conv_a_w, v_conf_dw_w=v_conf_dw_w, v_conf_dw_b=v_conf_dw_b, v_conf_ln_g=v_conf_ln_g, v_conf_ln_b=v_conf_ln_b, v_dn_conv_w=v_dn_conv_w, v_dn_a_log=v_dn_a_log, v_dn_dt_bias=v_dn_dt_bias, v_dn_norm_g=v_dn_norm_g, v_w_out=v_w_out, v_w_ffn_in=v_w_ffn_in, v_w_ffn_out=v_w_ffn_out, v_final_norm_g=v_final_norm_g)
    weights = {n: given[n] for n in TWIN_WEIGHTS}
    shared = {n: given[n] for n in SHARED_INPUTS}
    per_example = {n: given[n] for n in ['x', 'c']}
    grad_fn = _jax.value_and_grad(_loss, argnums=(0, 1))

    def one_microbatch(ex, loss_target):
        ex = dict(ex)
        diff = ex.pop(TWIN_DIFF_INPUT)
        return grad_fn(weights, diff, {**shared, **ex}, loss_target)

    if N_MICROBATCH == 1:
        loss, (grad_w, grad_x) = one_microbatch(per_example, given["loss_target"])
    else:
        def body(carry, xs):
            loss_sum, grad_sum = carry
            l_k, (gw_k, gx_k) = one_microbatch(xs[0], xs[1])
            with _jax.named_scope("update"):
                return (loss_sum + l_k, _jax.tree.map(_jnp.add, grad_sum, gw_k)), gx_k

        init = (_jnp.zeros((), _jnp.float32), _jax.tree.map(_jnp.zeros_like, weights))
        (loss, grad_w), grad_x = _jax.lax.scan(body, init, (per_example, given["loss_target"]))
    with _jax.named_scope("update"):
        delta_w, new_m, new_v = {}, {}, {}
        for n in TWIN_WEIGHTS:
            delta_w[n], new_m[n], new_v[n] = _adamw(weights[n], grad_w[n], given["m_" + n], given["v_" + n])
    return (loss, grad_x, *[grad_w[n] for n in TWIN_WEIGHTS], *[delta_w[n] for n in TWIN_WEIGHTS],
            *[new_m[n] for n in TWIN_WEIGHTS], *[new_v[n] for n in TWIN_WEIGHTS])
```

```python
import functools

import jax
import jax.numpy as jnp
import numpy as np
from jax import lax
from jax.experimental import pallas as pl
from jax.experimental.pallas import tpu as pltpu

F32 = jnp.float32
BF16 = jnp.bfloat16

N_DEV = 8
D = 1024
DEPTH = 4
D_CONV = 256
D_CONF = 256
D_DN = 512
HEADS = 4
HD = 128
KA, KB, KC = 3, 31, 4
CHUNK = 64
D_FF = 2816
IN_COLS = 3336
IN_PAD = 3456
N_CONVCOL = 2048
EPS = 1e-6
LN_EPS = 1e-5
HALO = 32
VMEM_LIMIT = 56 * 1024 * 1024

C_AB, C_AC, C_AV, C_BA, C_BG, C_Q, C_Z, C_GB = 0, 256, 512, 768, 1024, 1280, 2816, 3328

LR, B1, B2, AEPS, WD, STEP = 0.001, 0.9, 0.999, 1e-08, 0.01, 10


def _dot(a, b, dims, hi):
    if hi:
        return lax.dot_general(a.astype(F32), b.astype(F32), (dims, ((), ())), precision=lax.Precision.HIGHEST,
                               preferred_element_type=F32)
    return lax.dot_general(a.astype(BF16), b.astype(BF16), (dims, ((), ())), preferred_element_type=F32)


@functools.partial(jax.custom_vjp, nondiff_argnums=(2,))
def mm_nn(a, b, hi=False):
    return _dot(a, b, ((1,), (0,)), hi)


@functools.partial(jax.custom_vjp, nondiff_argnums=(2,))
def mm_nt(a, b, hi=False):
    return _dot(a, b, ((1,), (1,)), hi)


@functools.partial(jax.custom_vjp, nondiff_argnums=(2,))
def mm_tn(a, b, hi=False):
    return _dot(a, b, ((0,), (0,)), hi)


mm_nn.defvjp(lambda a, b, hi: (mm_nn(a, b, hi), (a, b)),
             lambda hi, r, g: (mm_nt(g, r[1], hi), mm_tn(r[0], g, hi)))
mm_nt.defvjp(lambda a, b, hi: (mm_nt(a, b, hi), (a, b)),
             lambda hi, r, g: (mm_nn(g, r[1], hi), mm_tn(g, r[0], hi)))
mm_tn.defvjp(lambda a, b, hi: (mm_tn(a, b, hi), (a, b)),
             lambda hi, r, g: (mm_nt(r[1], g, hi), mm_nn(r[0], g, hi)))


def _sigmoid(x):
    return 1.0 / (1.0 + jnp.exp(-x))


def _silu(x):
    return x * _sigmoid(x)


def _softplus(x):
    return jnp.maximum(x, 0.0) + jnp.log(1.0 + jnp.exp(-jnp.abs(x)))


def _iota2(shape, dim):
    return lax.broadcasted_iota(jnp.int32, shape, dim)


def _delta_chunk(qs, ks, vs, gb, Ss):
    C = CHUNK
    row = _iota2((C, C), 0)
    col = _iota2((C, C), 1)
    causal = row >= col
    strict = row > col
    tri = causal.astype(F32)
    eye = (row == col).astype(F32)
    gc_all = mm_nn(tri, gb, True)
    gc_t = gc_all.T
    lane = _iota2((C, 128), 1)
    subl = _iota2((128, C), 0)
    last = (_iota2((C, 1), 0) == C - 1).astype(F32)
    outs, new_S = [], []
    for h in range(HEADS):
        q = qs[h] * (HD ** -0.5)
        k, v, S = ks[h], vs[h], Ss[h]
        gcol = jnp.sum(jnp.where(lane == h, gc_all, 0.0), axis=1, keepdims=True)
        grow = jnp.sum(jnp.where(subl == h, gc_t, 0.0), axis=0, keepdims=True)
        beta = jnp.sum(jnp.where(lane == HEADS + h, gb, 0.0), axis=1, keepdims=True)
        diff = gcol - grow
        decay = jnp.where(causal, jnp.exp(jnp.where(causal, diff, 0.0)), 0.0)
        kb = k * beta
        vb = v * beta
        X = -jnp.where(strict, mm_nt(kb, k) * decay, 0.0)
        T = eye + X
        P = X
        for _ in range(5):
            P = mm_nn(P, P, True)
            T = T + mm_nn(T, P, True)
        eg = jnp.exp(gcol)
        u = mm_nn(T, vb)
        w = mm_nn(T, kb * eg)
        qk = jnp.where(causal, mm_nt(q, k) * decay, 0.0)
        v_new = u - mm_nn(w, S)
        o = mm_nn(q * eg, S) + mm_nn(qk, v_new)
        g_last = jnp.sum(gcol * last, axis=0, keepdims=True)
        S2 = S * jnp.exp(g_last) + mm_tn(k * jnp.exp(g_last - gcol), v_new)
        outs.append(o)
        new_S.append(S2)
    return outs, new_S


def _split_heads(x):
    return [x[:, h * HD:(h + 1) * HD] for h in range(HEADS)]


def _delta_fwd(qkv, gb):
    T = qkv.shape[0]
    n = T // CHUNK

    def body(q_ref, k_ref, v_ref, gb_ref, o_ref, ss_ref, s_scr):
        @pl.when(pl.program_id(0) == 0)
        def _():
            s_scr[...] = jnp.zeros_like(s_scr)

        Ss = [s_scr[h] for h in range(HEADS)]
        for h in range(HEADS):
            ss_ref[0, h] = Ss[h]
        outs, new_S = _delta_chunk(_split_heads(q_ref[...]), _split_heads(k_ref[...]), _split_heads(v_ref[...]),
                                   gb_ref[...], Ss)
        for h in range(HEADS):
            o_ref[:, h * HD:(h + 1) * HD] = outs[h]
            s_scr[h] = new_S[h]

    row = lambda w, j=0: pl.BlockSpec((CHUNK, w), lambda i: (i, j))
    return pl.pallas_call(
        body, name="delta_fwd", grid=(n,),
        in_specs=[row(D_DN, 0), row(D_DN, 1), row(D_DN, 2), row(128)],
        out_specs=[row(D_DN), pl.BlockSpec((1, HEADS, HD, HD), lambda i: (i, 0, 0, 0))],
        out_shape=[jax.ShapeDtypeStruct((T, D_DN), F32), jax.ShapeDtypeStruct((n, HEADS, HD, HD), F32)],
        scratch_shapes=[pltpu.VMEM((HEADS, HD, HD), F32)],
        compiler_params=pltpu.CompilerParams(dimension_semantics=("arbitrary",), vmem_limit_bytes=VMEM_LIMIT),
    )(qkv, qkv, qkv, gb)


def _delta_bwd(qkv, gb, ss, do):
    T = qkv.shape[0]
    n = T // CHUNK

    def body(q_ref, k_ref, v_ref, gb_ref, ss_ref, do_ref, dq_ref, dk_ref, dv_ref, dgb_ref, ds_scr):
        @pl.when(pl.program_id(0) == 0)
        def _():
            ds_scr[...] = jnp.zeros_like(ds_scr)

        Ss = [ss_ref[0, h] for h in range(HEADS)]
        _, vjp = jax.vjp(_delta_chunk, _split_heads(q_ref[...]), _split_heads(k_ref[...]), _split_heads(v_ref[...]),
                         gb_ref[...], Ss)
        dqs, dks, dvs, dgb, dSs = vjp((_split_heads(do_ref[...]), [ds_scr[h] for h in range(HEADS)]))
        for h in range(HEADS):
            dq_ref[:, h * HD:(h + 1) * HD] = dqs[h]
            dk_ref[:, h * HD:(h + 1) * HD] = dks[h]
            dv_ref[:, h * HD:(h + 1) * HD] = dvs[h]
            ds_scr[h] = dSs[h]
        dgb_ref[...] = dgb

    row = lambda w, j=0: pl.BlockSpec((CHUNK, w), lambda i: (n - 1 - i, j))
    return pl.pallas_call(
        body, name="delta_bwd", grid=(n,),
        in_specs=[row(D_DN, 0), row(D_DN, 1), row(D_DN, 2), row(128),
                  pl.BlockSpec((1, HEADS, HD, HD), lambda i: (n - 1 - i, 0, 0, 0)), row(D_DN)],
        out_specs=[row(D_DN), row(D_DN), row(D_DN), row(128)],
        out_shape=[jax.ShapeDtypeStruct((T, D_DN), F32)] * 3 + [jax.ShapeDtypeStruct((T, 128), F32)],
        scratch_shapes=[pltpu.VMEM((HEADS, HD, HD), F32)],
        compiler_params=pltpu.CompilerParams(dimension_semantics=("arbitrary",), vmem_limit_bytes=VMEM_LIMIT),
    )(qkv, qkv, qkv, gb, ss, do)


def _cols(arr, width, index, first_row=0):
    return (arr, width, index, first_row)


def _rowwise(name, fn, tiled, consts, out_tiled, out_acc, tm, rows=None):
    tiled = [t if isinstance(t, tuple) else (t, t.shape[1], 0, 0) for t in tiled]
    T = tiled[0][0].shape[0] if rows is None else rows
    tm = min(tm, T)
    assert T % tm == 0 and all(r % tm == 0 for (_, _, _, r) in tiled)
    n_t, n_c, n_o, n_a = len(tiled), len(consts), len(out_tiled), len(out_acc)

    def body(*refs):
        ins = [r[...] for r in refs[:n_t + n_c]]
        outs = fn(*ins)
        o_refs = refs[n_t + n_c:n_t + n_c + n_o]
        a_refs = refs[n_t + n_c + n_o:]
        for r, val in zip(o_refs, outs[:n_o]):
            r[...] = val.astype(r.dtype)
        if n_a:
            @pl.when(pl.program_id(0) == 0)
            def _():
                for r in a_refs:
                    r[...] = jnp.zeros_like(r)
            for r, val in zip(a_refs, outs[n_o:]):
                r[...] += val

    def const_spec(a):
        nd = a.ndim
        return pl.BlockSpec(a.shape, lambda i: (0,) * nd, pipeline_mode=pl.Buffered(1))

    in_specs = [pl.BlockSpec((tm, w), functools.partial(lambda i, j, r: (i + r, j), j=j, r=r0 // tm))
                for (_, w, j, r0) in tiled]
    in_specs += [const_spec(a) for a in consts]
    out_specs = [pl.BlockSpec((tm, w), lambda i: (i, 0)) for (w, _) in out_tiled]
    out_specs += [pl.BlockSpec(s, lambda i: (0, 0)) for (s, _) in out_acc]
    out_shape = [jax.ShapeDtypeStruct((T, w), dt) for (w, dt) in out_tiled]
    out_shape += [jax.ShapeDtypeStruct(s, dt) for (s, dt) in out_acc]
    return pl.pallas_call(
        body, name=name, grid=(T // tm,), in_specs=in_specs, out_specs=out_specs, out_shape=out_shape,
        compiler_params=pltpu.CompilerParams(dimension_semantics=("arbitrary",), vmem_limit_bytes=VMEM_LIMIT),
    )(*[t[0] for t in tiled], *consts)


def _colsum(x):
    return jnp.sum(x, axis=0, keepdims=True)


def _rms(x):
    r = lax.rsqrt(jnp.mean(x * x, axis=-1, keepdims=True) + EPS)
    return x * r, r


def _rms_bwd(dxn, xn, r):
    return r * (dxn - xn * jnp.mean(dxn * xn, axis=-1, keepdims=True))


def _normproj_fwd(name, x, sh, sc, g, w, tm):
    def fn(x, sh, sc, g, w):
        xn, _ = _rms(x)
        h = (xn * (g * (1.0 + sc)) + sh).astype(BF16)
        return jnp.dot(h, w, preferred_element_type=F32), h

    return _rowwise(name, fn, [x], [sh, sc, g, w], [(w.shape[1], F32), (D, BF16)], [], tm)


def _normproj_bwd(name, x, dpre, dres, sc, g, w, tm):
    def fn(x, dpre, dres, sc, g, w):
        xn, r = _rms(x)
        dh = lax.dot_general(dpre, w, (((1,), (1,)), ((), ())), preferred_element_type=F32)
        da = _colsum(dh * xn)
        dx = _rms_bwd(dh * (g * (1.0 + sc)), xn, r) + dres
        return dx, _colsum(dh), da * g, da * (1.0 + sc)

    vec = ((1, D), F32)
    return _rowwise(name, fn, [x, dpre, dres], [sc, g, w], [(D, F32)], [vec, vec, vec], tm)


def _stage2(a_b, blk, cp, u1c, qp, kp, vp, bb, lg, lb, alog, dtb):
    y_a = a_b * cp
    u1 = u1c + bb
    mu = jnp.mean(u1, axis=-1, keepdims=True)
    uc = u1 - mu
    var = jnp.mean(uc * uc, axis=-1, keepdims=True)
    y_b = _silu(uc * lax.rsqrt(var + LN_EPS) * lg + lb)

    def l2(t):
        t = _silu(t)
        return t * lax.rsqrt(jnp.sum(t * t, axis=-1, keepdims=True) + EPS)

    q = [l2(t) for t in qp]
    k = [l2(t) for t in kp]
    v = _silu(vp)
    lane = _iota2(blk.shape, 1)
    gdec = -jnp.exp(alog) * _softplus(blk + dtb)
    gb = jnp.where(lane < HEADS, gdec, jnp.where(lane < 2 * HEADS, _sigmoid(blk), 0.0))
    return y_a, y_b, q, k, v, gb


def _heads_of(x, base=0):
    return [x[:, base + h * HD:base + (h + 1) * HD] for h in range(HEADS)]


def _stage2_fwd(proj, convout, bb, lg, lb, alog, dtb, tm):
    def fn(a_b, blk, co, bb, lg, lb, alog, dtb):
        y_a, y_b, q, k, v, gb = _stage2(a_b, blk, co[:, 0:256], co[:, 256:512], _heads_of(co, 512), _heads_of(co, 1024),
                                        co[:, 1536:2048], bb, lg, lb, alog, dtb)
        return jnp.concatenate([y_a, y_b], axis=1), jnp.concatenate(q + k + [v], axis=1), gb

    return _rowwise("stage2_fwd", fn, [_cols(proj, 256, 0), _cols(proj, 128, C_GB // 128), convout],
                    [bb, lg, lb, alog, dtb], [(512, BF16), (1536, F32), (128, F32)], [], tm)


def _stage2_bwd(proj, convout, dy_ab, dq, dk, dv, dgb, bb, lg, lb, alog, dtb, tm):
    def fn(a_b, blk, co, dy_ab, dq, dk, dv, dgb, bb, lg, lb, alog, dtb):
        args = (a_b, blk, co[:, 0:256], co[:, 256:512], _heads_of(co, 512), _heads_of(co, 1024), co[:, 1536:2048],
                bb, lg, lb, alog, dtb)
        _, vjp = jax.vjp(_stage2, *args)
        ct = (dy_ab[:, 0:256], dy_ab[:, 256:512], _heads_of(dq), _heads_of(dk), dv, dgb)
        da_b, dblk, dcp, du1c, dqp, dkp, dvp, dbb, dlg, dlb, dalog, ddtb = vjp(ct)
        dco = jnp.concatenate([dcp, du1c] + dqp + dkp + [dvp], axis=1)
        return dco, da_b, dblk, dbb, dlg, dlb, dalog, ddtb

    v256, v128 = ((1, 256), F32), ((1, 128), F32)
    return _rowwise("stage2_bwd", fn,
                    [_cols(proj, 256, 0), _cols(proj, 128, C_GB // 128), convout, dy_ab, dq, dk, dv, dgb],
                    [bb, lg, lb, alog, dtb], [(N_CONVCOL, F32), (256, F32), (128, F32)],
                    [v256, v256, v256, v128, v128], tm)


def _stage3(o, z, dng):
    ys = []
    for oh, zh in zip(o, z):
        on = oh * lax.rsqrt(jnp.mean(oh * oh, axis=-1, keepdims=True) + EPS)
        ys.append(on * dng * _silu(zh))
    return ys


def _outproj_fwd(x, o, proj, y_ab, g1, dng, wout, tm):
    def fn(x, o, z0, z1, z2, z3, y_ab, g1, dng, wout):
        y_c = _stage3(_heads_of(o), [z0, z1, z2, z3], dng)
        ycat = jnp.concatenate([y_ab] + [t.astype(BF16) for t in y_c], axis=1)
        mix = jnp.dot(ycat, wout, preferred_element_type=F32)
        return x + g1 * mix, mix, ycat

    return _rowwise("outproj_fwd", fn, [x, o] + _z_heads(proj) + [y_ab],
                    [g1, dng, wout], [(D, F32), (D, F32), (D, BF16)], [], tm)


def _z_heads(proj):
    return [_cols(proj, HD, C_Z // HD + h) for h in range(HEADS)]


def _outproj_bwd(dx1, mix, o, proj, g1, dng, wout, tm):
    def fn(dx1, mix, o, z0, z1, z2, z3, g1, dng, wout):
        dmix = (dx1 * g1).astype(BF16)
        dycat = lax.dot_general(dmix, wout, (((1,), (1,)), ((), ())), preferred_element_type=F32)
        _, vjp = jax.vjp(_stage3, _heads_of(o), [z0, z1, z2, z3], dng)
        do, dz, ddng = vjp(_heads_of(dycat, 512))
        return (dmix, dycat[:, 0:512], jnp.concatenate(do, axis=1), jnp.concatenate(dz, axis=1),
                _colsum(dx1 * mix), ddng)

    return _rowwise("outproj_bwd", fn, [dx1, mix, o] + _z_heads(proj), [g1, dng, wout],
                    [(D, BF16), (512, F32), (512, F32), (512, F32)], [((1, D), F32), ((1, HD), F32)], tm)


_CONV_BLOCKS = ((0, 256, KA), (256, 512, KB), (512, 2048, KC))
_CONV_STRIP = 256


def _conv_inputs(proj_ref):
    a_c, a_v = proj_ref[:, C_AC:C_AC + 256], proj_ref[:, C_AV:C_AV + 256]
    b_a, b_g = proj_ref[:, C_BA:C_BA + 256], proj_ref[:, C_BG:C_BG + 256]
    return a_c, a_v, b_a, _sigmoid(b_g)


def _conv_fwd(proj, wa, wb, wc, tm):
    T = proj.shape[0]
    tm = min(tm, T)

    def body(proj_ref, wa_ref, wb_ref, wc_ref, out_ref, ext):
        @pl.when(pl.program_id(0) == 0)
        def _():
            ext[0:HALO, :] = jnp.zeros((HALO, N_CONVCOL), F32)

        a_c, a_v, b_a, sg = _conv_inputs(proj_ref)
        ext[HALO:HALO + tm, 0:256] = a_c * a_v
        ext[HALO:HALO + tm, 256:512] = b_a * sg
        ext[HALO:HALO + tm, 512:2048] = proj_ref[:, C_Q:C_Q + 1536]
        for (c0, c1, kw), w_ref in zip(_CONV_BLOCKS, (wa_ref, wb_ref, wc_ref)):
            for s0 in range(c0, c1, _CONV_STRIP):
                acc = jnp.zeros((tm, _CONV_STRIP), F32)
                for k in range(kw):
                    acc += w_ref[k:k + 1, s0 - c0:s0 - c0 + _CONV_STRIP] * ext[pl.ds(HALO - (kw - 1) + k, tm), s0:s0 + _CONV_STRIP]
                out_ref[:, s0:s0 + _CONV_STRIP] = acc
        ext[0:HALO, :] = ext[tm:tm + HALO, :]

    full = lambda a: pl.BlockSpec(a.shape, lambda i: (0, 0))
    return pl.pallas_call(
        body, name="conv_fwd", grid=(T // tm,),
        in_specs=[pl.BlockSpec((tm, IN_PAD), lambda i: (i, 0)), full(wa), full(wb), full(wc)],
        out_specs=pl.BlockSpec((tm, N_CONVCOL), lambda i: (i, 0)),
        out_shape=jax.ShapeDtypeStruct((T, N_CONVCOL), F32),
        scratch_shapes=[pltpu.VMEM((HALO + tm, N_CONVCOL), F32)],
        compiler_params=pltpu.CompilerParams(dimension_semantics=("arbitrary",), vmem_limit_bytes=VMEM_LIMIT),
    )(proj, wa, wb, wc)


def _conv_bwd(proj, dco, da_b, dz, dblk, wa, wb, wc, tm):
    T = proj.shape[0]
    tm = min(tm, T)
    n = T // tm

    def body(proj_ref, dco_ref, dab_ref, dz_ref, dblk_ref, wa_ref, wb_ref, wc_ref,
             dproj_ref, dwa_ref, dwb_ref, dwc_ref, ext, dxs):
        @pl.when(pl.program_id(0) == 0)
        def _():
            ext[tm:tm + HALO, :] = jnp.zeros((HALO, N_CONVCOL), F32)
            dwa_ref[...] = jnp.zeros_like(dwa_ref)
            dwb_ref[...] = jnp.zeros_like(dwb_ref)
            dwc_ref[...] = jnp.zeros_like(dwc_ref)

        ext[0:tm, :] = dco_ref[...]
        a_c, a_v, b_a, sg = _conv_inputs(proj_ref)

        def conv_in(s0):
            if s0 < 256:
                return a_c * a_v
            if s0 < 512:
                return b_a * sg
            return proj_ref[:, C_Q + s0 - 512:C_Q + s0 - 512 + _CONV_STRIP]

        for (c0, c1, kw), w_ref, dw_ref in zip(_CONV_BLOCKS, (wa_ref, wb_ref, wc_ref), (dwa_ref, dwb_ref, dwc_ref)):
            for s0 in range(c0, c1, _CONV_STRIP):
                xin = conv_in(s0)
                acc = jnp.zeros((tm, _CONV_STRIP), F32)
                for k in range(kw):
                    sh = ext[pl.ds(kw - 1 - k, tm), s0:s0 + _CONV_STRIP]
                    acc += w_ref[k:k + 1, s0 - c0:s0 - c0 + _CONV_STRIP] * sh
                    dw_ref[k:k + 1, s0 - c0:s0 - c0 + _CONV_STRIP] += _colsum(sh * xin)
                dxs[:, s0:s0 + _CONV_STRIP] = acc
        ext[tm:tm + HALO, :] = ext[0:HALO, :]

        dp, du0 = dxs[:, 0:256], dxs[:, 256:512]
        dproj_ref[:, C_AB:C_AB + 256] = dab_ref[...].astype(BF16)
        dproj_ref[:, C_AC:C_AC + 256] = (dp * a_v).astype(BF16)
        dproj_ref[:, C_AV:C_AV + 256] = (dp * a_c).astype(BF16)
        dproj_ref[:, C_BA:C_BA + 256] = (du0 * sg).astype(BF16)
        dproj_ref[:, C_BG:C_BG + 256] = (du0 * b_a * sg * (1.0 - sg)).astype(BF16)
        dproj_ref[:, C_Q:C_Q + 1536] = dxs[:, 512:2048].astype(BF16)
        dproj_ref[:, C_Z:C_Z + 512] = dz_ref[...].astype(BF16)
        dproj_ref[:, C_GB:C_GB + 128] = dblk_ref[...].astype(BF16)

    rev = lambda w: pl.BlockSpec((tm, w), lambda i: (n - 1 - i, 0))
    full = lambda a: pl.BlockSpec(a.shape, lambda i: (0, 0))
    return pl.pallas_call(
        body, name="conv_bwd", grid=(n,),
        in_specs=[rev(IN_PAD), rev(N_CONVCOL), rev(256), rev(512), rev(128), full(wa), full(wb), full(wc)],
        out_specs=[rev(IN_PAD), full(wa), full(wb), full(wc)],
        out_shape=[jax.ShapeDtypeStruct((T, IN_PAD), BF16), jax.ShapeDtypeStruct(wa.shape, F32),
                   jax.ShapeDtypeStruct(wb.shape, F32), jax.ShapeDtypeStruct(wc.shape, F32)],
        scratch_shapes=[pltpu.VMEM((tm + HALO, N_CONVCOL), F32), pltpu.VMEM((tm, N_CONVCOL), F32)],
        compiler_params=pltpu.CompilerParams(dimension_semantics=("arbitrary",), vmem_limit_bytes=VMEM_LIMIT),
    )(proj, dco, da_b, dz, dblk, wa, wb, wc)


def _ffn_out_fwd(x1, gu, g2, wfo, tm):
    def fn(x1, gu, g2, wfo):
        s = (_silu(gu[:, :D_FF]) * gu[:, D_FF:]).astype(BF16)
        f = jnp.dot(s, wfo, preferred_element_type=F32)
        return x1 + g2 * f, f

    return _rowwise("ffnout_fwd", fn, [x1, gu], [g2, wfo], [(D, F32), (D, F32)], [], tm)


def _ffn_out_bwd(dx2, gu, f, g2, wfo, tm):
    def fn(dx2, gu, f, g2, wfo):
        gate, up = gu[:, :D_FF], gu[:, D_FF:]
        sg = _sigmoid(gate)
        sl = gate * sg
        df = (dx2 * g2).astype(BF16)
        ds = lax.dot_general(df, wfo, (((1,), (1,)), ((), ())), preferred_element_type=F32)
        dgate = ds * up * (sg * (1.0 + gate * (1.0 - sg)))
        dgu = jnp.concatenate([dgate.astype(BF16), (ds * sl).astype(BF16)], axis=1)
        return dgu, sl * up, df, _colsum(dx2 * f)

    return _rowwise("ffnout_bwd", fn, [dx2, gu, f], [g2, wfo], [(2 * D_FF, BF16), (D_FF, BF16), (D, BF16)],
                    [((1, D), F32)], tm)


def _loss_bwd(x, tgt, gfin, tm):
    def fn(x, tgt, gfin):
        xn, r = _rms(x)
        e = xn * gfin - tgt
        loss = 0.5 * jnp.sum(jnp.mean(e * e, axis=-1, keepdims=True), axis=0, keepdims=True)
        dy = e * (1.0 / D)
        return _rms_bwd(dy * gfin, xn, r), jnp.broadcast_to(loss, (1, 128)), _colsum(dy * xn)

    return _rowwise("loss_bwd", fn, [x, tgt], [gfin], [(D, F32)], [((1, 128), F32), ((1, D), F32)], tm)


def _wgrad(name, a, b, bm=256, bk=512):
    T, M = a.shape
    N = b.shape[1]
    bk = min(bk, T)

    def body(a_ref, b_ref, o_ref):
        @pl.when(pl.program_id(1) == 0)
        def _():
            o_ref[...] = jnp.zeros_like(o_ref)

        o_ref[...] += lax.dot_general(a_ref[...], b_ref[...], (((0,), (0,)), ((), ())), preferred_element_type=F32)

    return pl.pallas_call(
        body, name=name, grid=(M // bm, T // bk),
        in_specs=[pl.BlockSpec((bk, bm), lambda i, k: (k, i)), pl.BlockSpec((bk, N), lambda i, k: (k, 0))],
        out_specs=pl.BlockSpec((bm, N), lambda i, k: (i, 0)),
        out_shape=jax.ShapeDtypeStruct((M, N), F32),
        compiler_params=pltpu.CompilerParams(dimension_semantics=("arbitrary", "arbitrary"), vmem_limit_bytes=VMEM_LIMIT),
    )(a, b)


def _adamw(w, g, m, v):
    m = B1 * m + (1.0 - B1) * g
    v = B2 * v + (1.0 - B2) * (g * g)
    m_hat = m / (1.0 - B1 ** STEP)
    v_hat = v / (1.0 - B2 ** STEP)
    return -LR * (m_hat / (jnp.sqrt(v_hat) + AEPS) + WD * w), m, v


def _adam_call(name, w, g, m, v, tm):
    C = w.shape[1]
    return _rowwise(name, _adamw, [w, g, m, v], [], [(C, F32)] * 3, [], tm)


def _reduce_adam(name, recv, w, m, v, tm):
    _, R, C = recv.shape
    flat = recv.reshape(N_DEV * R, C)

    def fn(*a):
        g = a[0]
        for p in a[1:N_DEV]:
            g = g + p
        return (g,) + _adamw(a[N_DEV], g, a[N_DEV + 1], a[N_DEV + 2])

    return _rowwise(name, fn, [_cols(flat, C, 0, j * R) for j in range(N_DEV)] + [w, m, v], [], [(C, F32)] * 4, [],
                    tm, rows=R)


_OFFSETS = [(dx, dy, dc) for dx in (0, 1) for dy in (0, 1) for dc in (0, 1)][1:]
_MESH = pl.DeviceIdType.MESH


def _coords():
    return lax.axis_index("x"), lax.axis_index("y"), lax.axis_index("c")


def _flip(me, off):
    return tuple((1 - m) if d else m for m, d in zip(me, off))


def _linear(p):
    return 4 * p[0] + 2 * p[1] + p[2]


def _exchange(name, arrs, scatter):
    n = len(arrs)
    k_peers = len(_OFFSETS)

    def body(*refs):
        ins, outs = refs[:n], refs[n:2 * n]
        send, recv, loc = refs[2 * n:]
        me = _coords()
        my = _linear(me)
        local = []
        for a in range(n):
            src = ins[a].at[my] if scatter else ins[a]
            cp = pltpu.make_async_copy(src, outs[a].at[my], loc.at[a])
            cp.start()
            local.append(cp)
        remote = []
        for a in range(n):
            for k, off in enumerate(_OFFSETS):
                peer = _flip(me, off)
                src = ins[a].at[_linear(peer)] if scatter else ins[a]
                cp = pltpu.make_async_remote_copy(src_ref=src, dst_ref=outs[a].at[my], send_sem=send.at[a * k_peers + k],
                                                  recv_sem=recv.at[a * k_peers + k], device_id=peer, device_id_type=_MESH)
                cp.start()
                remote.append(cp)
        for cp in remote:
            cp.wait()
        for cp in local:
            cp.wait()

    anyspec = pl.BlockSpec(memory_space=pl.ANY)
    out_shape = [jax.ShapeDtypeStruct(a.shape if scatter else (N_DEV,) + a.shape, a.dtype) for a in arrs]
    return pl.pallas_call(
        body, name=name, in_specs=[anyspec] * n, out_specs=[anyspec] * n, out_shape=out_shape,
        scratch_shapes=[pltpu.SemaphoreType.DMA((n * k_peers,)), pltpu.SemaphoreType.DMA((n * k_peers,)),
                        pltpu.SemaphoreType.DMA((n,))],
    )(*arrs)


def _small_allgather(name, packed):
    R = packed.shape[0]

    def body(in_ref, all_ref, sum_ref, send, recv):
        me = _coords()
        my = _linear(me)
        all_ref[my] = in_ref[...]
        copies = []
        for k, off in enumerate(_OFFSETS):
            cp = pltpu.make_async_remote_copy(src_ref=in_ref, dst_ref=all_ref.at[my], send_sem=send.at[k], recv_sem=recv.at[k],
                                              device_id=_flip(me, off), device_id_type=_MESH)
            cp.start()
            copies.append(cp)
        for cp in copies:
            cp.wait()
        acc = all_ref[0]
        for j in range(1, N_DEV):
            acc = acc + all_ref[j]
        sum_ref[...] = acc

    vm = pl.BlockSpec(memory_space=pltpu.VMEM)
    return pl.pallas_call(
        body, name=name, in_specs=[vm], out_specs=[vm, vm],
        out_shape=[jax.ShapeDtypeStruct((N_DEV, R, 128), F32), jax.ShapeDtypeStruct((R, 128), F32)],
        scratch_shapes=[pltpu.SemaphoreType.DMA((len(_OFFSETS),)), pltpu.SemaphoreType.DMA((len(_OFFSETS),))],
        compiler_params=pltpu.CompilerParams(vmem_limit_bytes=VMEM_LIMIT),
    )(packed)


def _ada_mod(c, w16, bias):
    nc = w16.shape[2]
    kp = len(_OFFSETS)

    def body(c_ref, w_ref, b_ref, rows_ref, act_ref, cbuf, sbuf, send, recv):
        me = _coords()
        my = _linear(me)
        cbuf[my] = c_ref[...]
        copies = []
        for k, off in enumerate(_OFFSETS):
            cp = pltpu.make_async_remote_copy(src_ref=c_ref, dst_ref=cbuf.at[my], send_sem=send.at[k], recv_sem=recv.at[k],
                                              device_id=_flip(me, off), device_id_type=_MESH)
            cp.start()
            copies.append(cp)
        for cp in copies:
            cp.wait()
        act = _silu(jnp.concatenate([cbuf[j] for j in range(N_DEV)], axis=0))
        act_ref[...] = act
        act16 = act.astype(BF16)
        for l in range(DEPTH):
            ml = jnp.dot(act16, w_ref[l], preferred_element_type=F32) + b_ref[l:l + 1, :]
            for j in range(N_DEV):
                sbuf[j, l:l + 1, :] = ml[j:j + 1, :]
        rows_ref[my] = sbuf[my]
        copies = []
        for k, off in enumerate(_OFFSETS):
            peer = _flip(me, off)
            cp = pltpu.make_async_remote_copy(src_ref=sbuf.at[_linear(peer)], dst_ref=rows_ref.at[my], send_sem=send.at[kp + k],
                                              recv_sem=recv.at[kp + k], device_id=peer, device_id_type=_MESH)
            cp.start()
            copies.append(cp)
        for cp in copies:
            cp.wait()

    vm = pl.BlockSpec(memory_space=pltpu.VMEM)
    return pl.pallas_call(
        body, name="ada_mod", in_specs=[vm, vm, vm], out_specs=[vm, vm],
        out_shape=[jax.ShapeDtypeStruct((N_DEV, DEPTH, nc), F32), jax.ShapeDtypeStruct((N_DEV, D), F32)],
        scratch_shapes=[pltpu.VMEM((N_DEV, 1, D), F32), pltpu.VMEM((N_DEV, DEPTH, nc), F32),
                        pltpu.SemaphoreType.DMA((2 * kp,)), pltpu.SemaphoreType.DMA((2 * kp,))],
        compiler_params=pltpu.CompilerParams(vmem_limit_bytes=VMEM_LIMIT),
    )(c, w16, bias)


def _pack(arrs):
    parts = []
    for a in arrs:
        f = a.reshape(-1).astype(F32)
        parts.append(jnp.pad(f, (0, (-f.shape[0]) % 128)))
    flat = jnp.concatenate(parts)
    flat = jnp.pad(flat, (0, (-flat.shape[0]) % 1024))
    return flat.reshape(-1, 128)


def _unpack(packed, shapes):
    flat = packed.reshape(packed.shape[:-2] + (-1,))
    out, r = [], 0
    for s in shapes:
        n = int(np.prod(s))
        out.append(flat[..., r:r + n].reshape(packed.shape[:-2] + tuple(s)))
        r += -(-n // 128) * 128
    return out


def _pad_rows(w, rows):
    return jnp.pad(w, ((0, 0), (0, rows - w.shape[1]), (0, 0)))


_SMALL = ("b_ada", "norm_mix_g", "norm_ffn_g", "conv_a_w", "conf_dw_w", "conf_dw_b", "conf_ln_g", "conf_ln_b",
          "dn_conv_w", "dn_a_log", "dn_dt_bias", "dn_norm_g", "final_norm_g")
_BIG = ("w_in", "w_out", "w_ffn_in", "w_ffn_out")
_WEIGHTS = ("w_ada", "b_ada", "norm_mix_g", "norm_ffn_g", "w_in", "conv_a_w", "conf_dw_w", "conf_dw_b", "conf_ln_g",
            "conf_ln_b", "dn_conv_w", "dn_a_log", "dn_dt_bias", "dn_norm_g", "w_out", "w_ffn_in", "w_ffn_out",
            "final_norm_g")


def _step(x, c, loss_target, W, M, V):
    T = x.shape[1]
    me = _linear(_coords())
    xs, tgt = x[0], loss_target[0]
    vec = lambda a: a.reshape(1, -1)

    nada = W["w_ada"].shape[2]
    rows, act_all = _ada_mod(c, W["w_ada"].astype(BF16), lax.dynamic_slice(W["b_ada"], (0, me * nada), (DEPTH, nada)))
    mod = rows.transpose(1, 0, 2).reshape(DEPTH, 6, 1, D)

    g_in, g_out, g_fi, g_fo = _exchange("gather_weights", [W[k].astype(BF16) for k in _BIG], scatter=False)
    win = jnp.pad(g_in.transpose(1, 2, 0, 3).reshape(DEPTH, D, IN_COLS), ((0, 0), (0, 0), (0, IN_PAD - IN_COLS)))
    wout = g_out.transpose(1, 0, 2, 3).reshape(DEPTH, D, D)
    wfi = g_fi.transpose(1, 2, 0, 3).reshape(DEPTH, D, 2 * D_FF)
    wfo = g_fo.transpose(1, 0, 2, 3).reshape(DEPTH, D_FF, D)
    conv_names = ("conv_a_w", "conf_dw_w", "dn_conv_w")
    conv_all, _ = _small_allgather("gather_conv_w", _pack([W[k] for k in conv_names]))
    conv_full = [t.transpose(1, 2, 0, 3).reshape(t.shape[1], t.shape[2], -1)
                 for t in _unpack(conv_all, [W[k].shape for k in conv_names])]
    wa, wb, wc = _pad_rows(conv_full[0], 8), _pad_rows(conv_full[1], 32), _pad_rows(conv_full[2], 8)
    lane_pad = lambda a: jnp.pad(a, ((0, 0), (0, 128 - a.shape[1])))
    alog, dtb = lane_pad(W["dn_a_log"]), lane_pad(W["dn_dt_bias"])

    saved = []
    xc = xs
    for l in range(DEPTH):
        sh1, sc1, g1, sh2, sc2, g2 = [mod[l, i] for i in range(6)]
        proj, h = _normproj_fwd("inproj_fwd", xc, sh1, sc1, vec(W["norm_mix_g"][l]), win[l], 512)
        convout = _conv_fwd(proj, wa[l], wb[l], wc[l], 256)
        y_ab, qkv, gb = _stage2_fwd(proj, convout, vec(W["conf_dw_b"][l]), vec(W["conf_ln_g"][l]), vec(W["conf_ln_b"][l]),
                                    alog[l:l + 1], dtb[l:l + 1], 256)
        o, ss = _delta_fwd(qkv, gb)
        x1, mix, ycat = _outproj_fwd(xc, o, proj, y_ab, g1, vec(W["dn_norm_g"][l]), wout[l], 512)
        gu, h2 = _normproj_fwd("ffnin_fwd", x1, sh2, sc2, vec(W["norm_ffn_g"][l]), wfi[l], 256)
        x2, f = _ffn_out_fwd(x1, gu, g2, wfo[l], 256)
        saved.append((xc, proj, h, convout, qkv, gb, o, ss, mix, ycat, x1, gu, h2, f))
        xc = x2

    dx, loss_row, d_gfin = _loss_bwd(xc, tgt, vec(W["final_norm_g"]), 512)
    loss = lax.psum(loss_row[0, 0], ("x", "y", "c"))

    big_out = {k: [None] * DEPTH for k in _BIG}
    dmod, small = [None] * DEPTH, [None] * DEPTH
    for l in reversed(range(DEPTH)):
        xc, proj, h, convout, qkv, gb, o, ss, mix, ycat, x1, gu, h2, f = saved[l]
        sh1, sc1, g1, sh2, sc2, g2 = [mod[l, i] for i in range(6)]
        gm, gf = vec(W["norm_mix_g"][l]), vec(W["norm_ffn_g"][l])
        bb, lg, lb = vec(W["conf_dw_b"][l]), vec(W["conf_ln_g"][l]), vec(W["conf_ln_b"][l])
        dng = vec(W["dn_norm_g"][l])

        dgu, s, df, d_g2 = _ffn_out_bwd(dx, gu, f, g2, wfo[l], 256)
        gw_fo = _wgrad("wgrad_ffn_out", s, df)
        dx1, d_sh2, d_sc2, d_gf = _normproj_bwd("ffnin_bwd", x1, dgu, dx, sc2, gf, wfi[l], 256)
        gw_fi = _wgrad("wgrad_ffn_in", h2, dgu)
        dmix, dy_ab, do, dz, d_g1, d_dng = _outproj_bwd(dx1, mix, o, proj, g1, dng, wout[l], 512)
        gw_out = _wgrad("wgrad_out", ycat, dmix)
        dqkv = _delta_bwd(qkv, gb, ss, do)
        dco, da_b, dblk, d_bb, d_lg, d_lb, d_alog, d_dtb = _stage2_bwd(
            proj, convout, dy_ab, dqkv[0], dqkv[1], dqkv[2], dqkv[3], bb, lg, lb, alog[l:l + 1], dtb[l:l + 1], 256)
        dproj, d_wa, d_wb, d_wc = _conv_bwd(proj, dco, da_b, dz, dblk, wa[l], wb[l], wc[l], 256)
        dx, d_sh1, d_sc1, d_gm = _normproj_bwd("inproj_bwd", xc, dproj, dx1, sc1, gm, win[l], 512)
        gw_in = _wgrad("wgrad_in", h, dproj)

        dmod[l] = jnp.concatenate([d_sh1, d_sc1, d_g1, d_sh2, d_sc2, d_g2], axis=1)
        small[l] = dict(norm_mix_g=d_gm, norm_ffn_g=d_gf, conv_a_w=d_wa[:KA], conf_dw_w=d_wb[:KB], conf_dw_b=d_bb,
                        conf_ln_g=d_lg, conf_ln_b=d_lb, dn_conv_w=d_wc[:KC], dn_a_log=d_alog, dn_dt_bias=d_dtb,
                        dn_norm_g=d_dng)

        parts = [gw_in[:, :IN_COLS].reshape(D, N_DEV, -1).transpose(1, 0, 2), gw_out.reshape(N_DEV, -1, D),
                 gw_fi.reshape(D, N_DEV, -1).transpose(1, 0, 2), gw_fo.reshape(N_DEV, -1, D)]
        recv = _exchange("exchange_grads", parts, scatter=True)
        for k, r in zip(_BIG, recv):
            big_out[k][l] = _reduce_adam("reduce_adam_" + k, r, W[k][l], M[k][l], V[k][l],
                                         256 if r.shape[1] % 256 == 0 else r.shape[1])

    names = ("norm_mix_g", "norm_ffn_g", "conv_a_w", "conf_dw_w", "conf_dw_b", "conf_ln_g", "conf_ln_b", "dn_conv_w",
             "dn_a_log", "dn_dt_bias", "dn_norm_g")
    pieces = [jnp.stack(dmod)] + [jnp.stack([small[l][k] for l in range(DEPTH)]) for k in names] + [d_gfin]
    shapes = [p.shape for p in pieces]
    every, total = _small_allgather("gather_small_grads", _pack(pieces))
    tot = dict(zip(("dmod",) + names + ("final_norm_g",), _unpack(total, shapes)))
    dmod_all = _unpack(every, shapes[:1])[0]

    grads = {}
    grads["b_ada"] = tot["dmod"].reshape(DEPTH, 6 * D)
    for k in ("norm_mix_g", "norm_ffn_g", "conf_dw_b", "conf_ln_g", "conf_ln_b", "dn_norm_g"):
        grads[k] = tot[k].reshape(W[k].shape)
    grads["dn_a_log"] = tot["dn_a_log"].reshape(DEPTH, 128)[:, :HEADS]
    grads["dn_dt_bias"] = tot["dn_dt_bias"].reshape(DEPTH, 128)[:, :HEADS]
    grads["final_norm_g"] = tot["final_norm_g"].reshape(D)
    for k in conv_names:
        nloc = W[k].shape[2]
        grads[k] = lax.dynamic_slice_in_dim(tot[k], me * nloc, nloc, axis=2)

    dm = lax.dynamic_slice_in_dim(dmod_all.reshape(N_DEV, DEPTH, 6 * D), me * nada, nada, axis=2)
    pad16 = lambda a: jnp.pad(a, ((0, 16 - N_DEV), (0, 0))).astype(BF16)
    g_ada = _wgrad("wgrad_ada", pad16(act_all), pad16(dm.reshape(N_DEV, DEPTH * nada)))
    grads["w_ada"] = g_ada.reshape(D, DEPTH, nada).transpose(1, 0, 2)

    delta, new_m, new_v = {}, {}, {}
    r2 = lambda a: a.reshape(DEPTH * D, nada)
    d_, m_, v_ = _adam_call("adam_ada", r2(W["w_ada"]), r2(grads["w_ada"]), r2(M["w_ada"]), r2(V["w_ada"]), 512)
    delta["w_ada"], new_m["w_ada"], new_v["w_ada"] = [t.reshape(W["w_ada"].shape) for t in (d_, m_, v_)]
    sshapes = [W[k].shape for k in _SMALL]
    d_, m_, v_ = _adam_call("adam_small", _pack([W[k] for k in _SMALL]), _pack([grads[k] for k in _SMALL]),
                            _pack([M[k] for k in _SMALL]), _pack([V[k] for k in _SMALL]), 4096)
    for dst, packed in ((delta, d_), (new_m, m_), (new_v, v_)):
        dst.update(zip(_SMALL, _unpack(packed, sshapes)))
    for k in _BIG:
        g_, d_, m_, v_ = [jnp.stack([big_out[k][l][i] for l in range(DEPTH)]) for i in range(4)]
        grads[k], delta[k], new_m[k], new_v[k] = g_, d_, m_, v_

    return (loss, dx[None], *[grads[k] for k in _WEIGHTS], *[delta[k] for k in _WEIGHTS],
            *[new_m[k] for k in _WEIGHTS], *[new_v[k] for k in _WEIGHTS])


def kernel(x, c, w_ada, b_ada, norm_mix_g, norm_ffn_g, w_in, conv_a_w, conf_dw_w, conf_dw_b, conf_ln_g, conf_ln_b, dn_conv_w, dn_a_log, dn_dt_bias, dn_norm_g, w_out, w_ffn_in, w_ffn_out, final_norm_g, loss_target, m_w_ada, m_b_ada, m_norm_mix_g, m_norm_ffn_g, m_w_in, m_conv_a_w, m_conf_dw_w, m_conf_dw_b, m_conf_ln_g, m_conf_ln_b, m_dn_conv_w, m_dn_a_log, m_dn_dt_bias, m_dn_norm_g, m_w_out, m_w_ffn_in, m_w_ffn_out, m_final_norm_g, v_w_ada, v_b_ada, v_norm_mix_g, v_norm_ffn_g, v_w_in, v_conv_a_w, v_conf_dw_w, v_conf_dw_b, v_conf_ln_g, v_conf_ln_b, v_dn_conv_w, v_dn_a_log, v_dn_dt_bias, v_dn_norm_g, v_w_out, v_w_ffn_in, v_w_ffn_out, v_final_norm_g):
    a = dict(locals())
    W = {k: a[k] for k in _WEIGHTS}
    M = {k: a["m_" + k] for k in _WEIGHTS}
    V = {k: a["v_" + k] for k in _WEIGHTS}
    return _step(x, c, loss_target, W, M, V)
```

```python
import functools

import jax
import jax.numpy as jnp
import numpy as np
from jax import lax
from jax.experimental import pallas as pl
from jax.experimental.pallas import tpu as pltpu

F32 = jnp.float32
BF16 = jnp.bfloat16

N_DEV = 8
D = 1024
DEPTH = 4
D_CONV = 256
D_CONF = 256
D_DN = 512
HEADS = 4
HD = 128
KA, KB, KC = 3, 31, 4
CHUNK = 64
D_FF = 2816
IN_COLS = 3336
IN_PAD = 3456
N_CONVCOL = 2048
EPS = 1e-6
LN_EPS = 1e-5
HALO = 32
VMEM_LIMIT = 56 * 1024 * 1024
DELTA_NB = 4

C_AB, C_AC, C_AV, C_BA, C_BG, C_Q, C_Z, C_GB = 0, 256, 512, 768, 1024, 1280, 2816, 3328

LR, B1, B2, AEPS, WD, STEP = 0.001, 0.9, 0.999, 1e-08, 0.01, 10


def _dot(a, b, dims, hi):
    if hi:
        return lax.dot_general(a.astype(F32), b.astype(F32), (dims, ((), ())), precision=lax.Precision.HIGHEST,
                               preferred_element_type=F32)
    return lax.dot_general(a.astype(BF16), b.astype(BF16), (dims, ((), ())), preferred_element_type=F32)


@functools.partial(jax.custom_vjp, nondiff_argnums=(2,))
def mm_nn(a, b, hi=False):
    return _dot(a, b, ((1,), (0,)), hi)


@functools.partial(jax.custom_vjp, nondiff_argnums=(2,))
def mm_nt(a, b, hi=False):
    return _dot(a, b, ((1,), (1,)), hi)


@functools.partial(jax.custom_vjp, nondiff_argnums=(2,))
def mm_tn(a, b, hi=False):
    return _dot(a, b, ((0,), (0,)), hi)


mm_nn.defvjp(lambda a, b, hi: (mm_nn(a, b, hi), (a, b)),
             lambda hi, r, g: (mm_nt(g, r[1], hi), mm_tn(r[0], g, hi)))
mm_nt.defvjp(lambda a, b, hi: (mm_nt(a, b, hi), (a, b)),
             lambda hi, r, g: (mm_nn(g, r[1], hi), mm_tn(g, r[0], hi)))
mm_tn.defvjp(lambda a, b, hi: (mm_tn(a, b, hi), (a, b)),
             lambda hi, r, g: (mm_nt(r[1], g, hi), mm_nn(r[0], g, hi)))


def _sigmoid(x):
    return 1.0 / (1.0 + jnp.exp(-x))


def _silu(x):
    return x * _sigmoid(x)


def _softplus(x):
    return jnp.maximum(x, 0.0) + jnp.log(1.0 + jnp.exp(-jnp.abs(x)))


def _iota2(shape, dim):
    return lax.broadcasted_iota(jnp.int32, shape, dim)


def _dot16(a, b):
    return jnp.dot(a.astype(BF16), b.astype(BF16), preferred_element_type=F32)


def _dot_3pass(a, b):
    ah = a.astype(BF16)
    bh = b.astype(BF16)
    al = (a - ah.astype(F32)).astype(BF16)
    bl = (b - bh.astype(F32)).astype(BF16)
    d = lambda x, y: jnp.dot(x, y, preferred_element_type=F32)
    return d(ah, bh) + (d(ah, bl) + d(al, bh))


@jax.custom_vjp
def _unit_lower_inverses(Xs):
    n = Xs[0].shape[0]
    r, c = _iota2((n, n), 0), _iota2((n, n), 1)
    eye = (r == c).astype(F32)

    def joins(b):
        s = b.bit_length() - 1
        return ((r >> (s + 1)) == (c >> (s + 1))) & (((r >> s) & 1) == 1) & (((c >> s) & 1) == 0)

    Ts = [eye + jnp.where(joins(1), x, 0.0) for x in Xs]
    b = 2
    while b < n:
        m = joins(b)
        Ys = [_dot16(jnp.where(m, x, 0.0), t) for x, t in zip(Xs, Ts)]
        Ts = [t + _dot16(t, y) for t, y in zip(Ts, Ys)]
        b *= 2
    Rs = [(eye - t) + _dot_3pass(x, t) for x, t in zip(Xs, Ts)]
    return [t + _dot16(t, r_) for t, r_ in zip(Ts, Rs)]


def _unit_lower_inverses_fwd(Xs):
    Ts = _unit_lower_inverses(Xs)
    return Ts, Ts


def _unit_lower_inverses_bwd(Ts, gs):
    inner = [mm_nt(g, t) for g, t in zip(gs, Ts)]
    return ([mm_tn(t, i) for t, i in zip(Ts, inner)],)


_unit_lower_inverses.defvjp(_unit_lower_inverses_fwd, _unit_lower_inverses_bwd)

def _delta_chunks(qs, ks, vs, gbs, Ss):
    C = CHUNK
    nb = len(gbs)
    pairs = [(c, h) for c in range(nb) for h in range(HEADS)]
    each = lambda fn, *lists: [fn(*a) for a in zip(*lists)]
    row = _iota2((C, C), 0)
    col = _iota2((C, C), 1)
    causal = row >= col
    strict = row > col
    tri = causal.astype(F32)
    eye = (row == col).astype(F32)
    lane = _iota2((C, 128), 1)
    subl = _iota2((128, C), 0)
    last = (_iota2((C, 1), 0) == C - 1).astype(F32)

    gc_all = [mm_nn(tri, gb, True) for gb in gbs]
    gc_t = [g.T for g in gc_all]
    q = [qs[c][h] * (HD ** -0.5) for c, h in pairs]
    k = [ks[c][h] for c, h in pairs]
    v = [vs[c][h] for c, h in pairs]
    gcol = [jnp.sum(jnp.where(lane == h, gc_all[c], 0.0), axis=1, keepdims=True) for c, h in pairs]
    grow = [jnp.sum(jnp.where(subl == h, gc_t[c], 0.0), axis=0, keepdims=True) for c, h in pairs]
    beta = [jnp.sum(jnp.where(lane == HEADS + h, gbs[c], 0.0), axis=1, keepdims=True) for c, h in pairs]
    decay = each(lambda a, b: jnp.where(causal, jnp.exp(jnp.where(causal, a - b, 0.0)), 0.0), gcol, grow)
    kb = each(lambda a, b: a * b, k, beta)
    vb = each(lambda a, b: a * b, v, beta)
    kk = each(lambda a, b: mm_nt(a, b), kb, k)
    X = each(lambda a, d: -jnp.where(strict, a * d, 0.0), kk, decay)
    T = _unit_lower_inverses(X)
    eg = [jnp.exp(g) for g in gcol]
    u = each(lambda t, a: mm_nn(t, a), T, vb)
    w = each(lambda t, a, e: mm_nn(t, a * e), T, kb, eg)
    qk = each(lambda a, b, d: jnp.where(causal, mm_nt(a, b) * d, 0.0), q, k, decay)
    qg = each(lambda a, e: a * e, q, eg)
    g_last = [jnp.sum(g * last, axis=0, keepdims=True) for g in gcol]
    kd = each(lambda a, gl, g: a * jnp.exp(gl - g), k, g_last, gcol)
    eg_last = [jnp.exp(g) for g in g_last]

    outs = []
    for c in range(nb):
        sl = slice(c * HEADS, (c + 1) * HEADS)
        v_new = each(lambda a, b, S: a - mm_nn(b, S), u[sl], w[sl], Ss)
        oS = each(lambda a, S: mm_nn(a, S), qg[sl], Ss)
        outs.append(each(lambda a, b, n: a + mm_nn(b, n), oS, qk[sl], v_new))
        Ss = each(lambda S, e, a, n: S * e + mm_tn(a, n), Ss, eg_last[sl], kd[sl], v_new)
    return outs, Ss


def _split_chunks(ref, nb):
    return [[ref[c * CHUNK:(c + 1) * CHUNK, h * HD:(h + 1) * HD] for h in range(HEADS)] for c in range(nb)]


def _join_chunks(vals):
    return jnp.concatenate([jnp.concatenate(heads, axis=1) for heads in vals], axis=0)


def _delta_fwd(qkv, gb, nb):
    T = qkv.shape[0]
    nb = min(nb, T // CHUNK)
    rows = nb * CHUNK
    n = T // rows

    def body(q_ref, k_ref, v_ref, gb_ref, o_ref, ss_ref, s_scr):
        @pl.when(pl.program_id(0) == 0)
        def _():
            s_scr[...] = jnp.zeros_like(s_scr)

        Ss = [s_scr[h] for h in range(HEADS)]
        for h in range(HEADS):
            ss_ref[0, h] = Ss[h]
        gbs = [gb_ref[c * CHUNK:(c + 1) * CHUNK, :] for c in range(nb)]
        outs, new_S = _delta_chunks(_split_chunks(q_ref, nb), _split_chunks(k_ref, nb), _split_chunks(v_ref, nb), gbs, Ss)
        o_ref[...] = _join_chunks(outs)
        s_scr[...] = jnp.stack(new_S)

    row = lambda w, j=0: pl.BlockSpec((rows, w), lambda i: (i, j))
    return pl.pallas_call(
        body, name="delta_fwd", grid=(n,),
        in_specs=[row(D_DN, 0), row(D_DN, 1), row(D_DN, 2), row(128)],
        out_specs=[row(D_DN), pl.BlockSpec((1, HEADS, HD, HD), lambda i: (i, 0, 0, 0))],
        out_shape=[jax.ShapeDtypeStruct((T, D_DN), F32), jax.ShapeDtypeStruct((n, HEADS, HD, HD), F32)],
        scratch_shapes=[pltpu.VMEM((HEADS, HD, HD), F32)],
        compiler_params=pltpu.CompilerParams(dimension_semantics=("arbitrary",), vmem_limit_bytes=VMEM_LIMIT),
    )(qkv, qkv, qkv, gb)


def _delta_bwd(qkv, gb, ss, do, nb):
    T = qkv.shape[0]
    nb = min(nb, T // CHUNK)
    rows = nb * CHUNK
    n = T // rows

    def body(q_ref, k_ref, v_ref, gb_ref, ss_ref, do_ref, d_ref, ds_scr):
        @pl.when(pl.program_id(0) == 0)
        def _():
            ds_scr[...] = jnp.zeros_like(ds_scr)

        Ss = [ss_ref[0, h] for h in range(HEADS)]
        gbs = [gb_ref[c * CHUNK:(c + 1) * CHUNK, :] for c in range(nb)]
        _, vjp = jax.vjp(_delta_chunks, _split_chunks(q_ref, nb), _split_chunks(k_ref, nb), _split_chunks(v_ref, nb), gbs, Ss)
        dqs, dks, dvs, dgbs, dSs = vjp((_split_chunks(do_ref, nb), [ds_scr[h] for h in range(HEADS)]))
        d_ref[...] = jnp.concatenate([_join_chunks(dqs), _join_chunks(dks), _join_chunks(dvs),
                                      jnp.concatenate(dgbs, axis=0)], axis=1)
        ds_scr[...] = jnp.stack(dSs)

    row = lambda w, j=0: pl.BlockSpec((rows, w), lambda i: (n - 1 - i, j))
    return pl.pallas_call(
        body, name="delta_bwd", grid=(n,),
        in_specs=[row(D_DN, 0), row(D_DN, 1), row(D_DN, 2), row(128),
                  pl.BlockSpec((1, HEADS, HD, HD), lambda i: (n - 1 - i, 0, 0, 0)), row(D_DN)],
        out_specs=row(3 * D_DN + 128),
        out_shape=jax.ShapeDtypeStruct((T, 3 * D_DN + 128), F32),
        scratch_shapes=[pltpu.VMEM((HEADS, HD, HD), F32)],
        compiler_params=pltpu.CompilerParams(dimension_semantics=("arbitrary",), vmem_limit_bytes=VMEM_LIMIT),
    )(qkv, qkv, qkv, gb, ss, do)


def _cols(arr, width, index, first_row=0):
    return (arr, width, index, first_row)


def _rowwise(name, fn, tiled, consts, out_tiled, out_acc, tm, rows=None):
    tiled = [t if isinstance(t, tuple) else (t, t.shape[1], 0, 0) for t in tiled]
    T = tiled[0][0].shape[0] if rows is None else rows
    tm = min(tm, T)
    assert T % tm == 0 and all(r % tm == 0 for (_, _, _, r) in tiled)
    n_t, n_c, n_o, n_a = len(tiled), len(consts), len(out_tiled), len(out_acc)

    def body(*refs):
        ins = [r[...] for r in refs[:n_t + n_c]]
        outs = fn(*ins)
        o_refs = refs[n_t + n_c:n_t + n_c + n_o]
        a_refs = refs[n_t + n_c + n_o:]
        for r, val in zip(o_refs, outs[:n_o]):
            r[...] = val.astype(r.dtype)
        if n_a:
            @pl.when(pl.program_id(0) == 0)
            def _():
                for r in a_refs:
                    r[...] = jnp.zeros_like(r)
            for r, val in zip(a_refs, outs[n_o:]):
                r[...] += val

    def const_spec(a):
        nd = a.ndim
        return pl.BlockSpec(a.shape, lambda i: (0,) * nd, pipeline_mode=pl.Buffered(1))

    in_specs = [pl.BlockSpec((tm, w), functools.partial(lambda i, j, r: (i + r, j), j=j, r=r0 // tm))
                for (_, w, j, r0) in tiled]
    in_specs += [const_spec(a) for a in consts]
    out_specs = [pl.BlockSpec((tm, w), lambda i: (i, 0)) for (w, _) in out_tiled]
    out_specs += [pl.BlockSpec(s, lambda i: (0, 0)) for (s, _) in out_acc]
    out_shape = [jax.ShapeDtypeStruct((T, w), dt) for (w, dt) in out_tiled]
    out_shape += [jax.ShapeDtypeStruct(s, dt) for (s, dt) in out_acc]
    return pl.pallas_call(
        body, name=name, grid=(T // tm,), in_specs=in_specs, out_specs=out_specs, out_shape=out_shape,
        compiler_params=pltpu.CompilerParams(dimension_semantics=("arbitrary",), vmem_limit_bytes=VMEM_LIMIT),
    )(*[t[0] for t in tiled], *consts)


def _colsum(x):
    return jnp.sum(x, axis=0, keepdims=True)


def _rms(x):
    r = lax.rsqrt(jnp.mean(x * x, axis=-1, keepdims=True) + EPS)
    return x * r, r


def _rms_bwd(dxn, xn, r):
    return r * (dxn - xn * jnp.mean(dxn * xn, axis=-1, keepdims=True))


def _normproj_fwd(name, x, sh, sc, g, w, tm):
    def fn(x, sh, sc, g, w):
        xn, _ = _rms(x)
        h = (xn * (g * (1.0 + sc)) + sh).astype(BF16)
        return jnp.dot(h, w, preferred_element_type=F32), h

    return _rowwise(name, fn, [x], [sh, sc, g, w], [(w.shape[1], F32), (D, BF16)], [], tm)


def _normproj_bwd(name, x, dpre, dres, sc, g, w, tm):
    def fn(x, dpre, dres, sc, g, w):
        xn, r = _rms(x)
        dh = lax.dot_general(dpre, w, (((1,), (1,)), ((), ())), preferred_element_type=F32)
        da = _colsum(dh * xn)
        dx = _rms_bwd(dh * (g * (1.0 + sc)), xn, r) + dres
        return dx, _colsum(dh), da * g, da * (1.0 + sc)

    vec = ((1, D), F32)
    return _rowwise(name, fn, [x, dpre, dres], [sc, g, w], [(D, F32)], [vec, vec, vec], tm)


def _stage2(a_b, blk, cp, u1c, qp, kp, vp, bb, lg, lb, alog, dtb):
    y_a = a_b * cp
    u1 = u1c + bb
    mu = jnp.mean(u1, axis=-1, keepdims=True)
    uc = u1 - mu
    var = jnp.mean(uc * uc, axis=-1, keepdims=True)
    y_b = _silu(uc * lax.rsqrt(var + LN_EPS) * lg + lb)

    def l2(t):
        t = _silu(t)
        return t * lax.rsqrt(jnp.sum(t * t, axis=-1, keepdims=True) + EPS)

    q = [l2(t) for t in qp]
    k = [l2(t) for t in kp]
    v = _silu(vp)
    lane = _iota2(blk.shape, 1)
    gdec = -jnp.exp(alog) * _softplus(blk + dtb)
    gb = jnp.where(lane < HEADS, gdec, jnp.where(lane < 2 * HEADS, _sigmoid(blk), 0.0))
    return y_a, y_b, q, k, v, gb


def _heads_of(x, base=0):
    return [x[:, base + h * HD:base + (h + 1) * HD] for h in range(HEADS)]


def _stage2_fwd(proj, convout, bb, lg, lb, alog, dtb, tm):
    def fn(a_b, blk, co, bb, lg, lb, alog, dtb):
        y_a, y_b, q, k, v, gb = _stage2(a_b, blk, co[:, 0:256], co[:, 256:512], _heads_of(co, 512), _heads_of(co, 1024),
                                        co[:, 1536:2048], bb, lg, lb, alog, dtb)
        return jnp.concatenate([y_a, y_b], axis=1), jnp.concatenate(q + k + [v], axis=1), gb

    return _rowwise("stage2_fwd", fn, [_cols(proj, 256, 0), _cols(proj, 128, C_GB // 128), convout],
                    [bb, lg, lb, alog, dtb], [(512, BF16), (1536, F32), (128, F32)], [], tm)


def _stage2_bwd(proj, convout, dy_ab, dq, dk, dv, dgb, bb, lg, lb, alog, dtb, tm):
    def fn(a_b, blk, co, dy_ab, dq, dk, dv, dgb, bb, lg, lb, alog, dtb):
        args = (a_b, blk, co[:, 0:256], co[:, 256:512], _heads_of(co, 512), _heads_of(co, 1024), co[:, 1536:2048],
                bb, lg, lb, alog, dtb)
        _, vjp = jax.vjp(_stage2, *args)
        ct = (dy_ab[:, 0:256], dy_ab[:, 256:512], _heads_of(dq), _heads_of(dk), dv, dgb)
        da_b, dblk, dcp, du1c, dqp, dkp, dvp, dbb, dlg, dlb, dalog, ddtb = vjp(ct)
        dco = jnp.concatenate([dcp, du1c] + dqp + dkp + [dvp], axis=1)
        return dco, da_b, dblk, dbb, dlg, dlb, dalog, ddtb

    v256, v128 = ((1, 256), F32), ((1, 128), F32)
    return _rowwise("stage2_bwd", fn,
                    [_cols(proj, 256, 0), _cols(proj, 128, C_GB // 128), convout, dy_ab, dq, dk, dv, dgb],
                    [bb, lg, lb, alog, dtb], [(N_CONVCOL, F32), (256, F32), (128, F32)],
                    [v256, v256, v256, v128, v128], tm)


def _stage3(o, z, dng):
    ys = []
    for oh, zh in zip(o, z):
        on = oh * lax.rsqrt(jnp.mean(oh * oh, axis=-1, keepdims=True) + EPS)
        ys.append(on * dng * _silu(zh))
    return ys


def _outproj_fwd(x, o, proj, y_ab, g1, dng, wout, tm):
    def fn(x, o, z0, z1, z2, z3, y_ab, g1, dng, wout):
        y_c = _stage3(_heads_of(o), [z0, z1, z2, z3], dng)
        ycat = jnp.concatenate([y_ab] + [t.astype(BF16) for t in y_c], axis=1)
        mix = jnp.dot(ycat, wout, preferred_element_type=F32)
        return x + g1 * mix, mix, ycat

    return _rowwise("outproj_fwd", fn, [x, o] + _z_heads(proj) + [y_ab],
                    [g1, dng, wout], [(D, F32), (D, F32), (D, BF16)], [], tm)


def _z_heads(proj):
    return [_cols(proj, HD, C_Z // HD + h) for h in range(HEADS)]


def _outproj_bwd(dx1, mix, o, proj, g1, dng, wout, tm):
    def fn(dx1, mix, o, z0, z1, z2, z3, g1, dng, wout):
        dmix = (dx1 * g1).astype(BF16)
        dycat = lax.dot_general(dmix, wout, (((1,), (1,)), ((), ())), preferred_element_type=F32)
        _, vjp = jax.vjp(_stage3, _heads_of(o), [z0, z1, z2, z3], dng)
        do, dz, ddng = vjp(_heads_of(dycat, 512))
        return (dmix, dycat[:, 0:512], jnp.concatenate(do, axis=1), jnp.concatenate(dz, axis=1),
                _colsum(dx1 * mix), ddng)

    return _rowwise("outproj_bwd", fn, [dx1, mix, o] + _z_heads(proj), [g1, dng, wout],
                    [(D, BF16), (512, F32), (512, F32), (512, F32)], [((1, D), F32), ((1, HD), F32)], tm)


_CONV_BLOCKS = ((0, 256, KA), (256, 512, KB), (512, 2048, KC))
_CONV_STRIP = 256


def _conv_inputs(proj_ref):
    a_c, a_v = proj_ref[:, C_AC:C_AC + 256], proj_ref[:, C_AV:C_AV + 256]
    b_a, b_g = proj_ref[:, C_BA:C_BA + 256], proj_ref[:, C_BG:C_BG + 256]
    return a_c, a_v, b_a, _sigmoid(b_g)


def _conv_fwd(proj, wa, wb, wc, tm):
    T = proj.shape[0]
    tm = min(tm, T)

    def body(proj_ref, wa_ref, wb_ref, wc_ref, out_ref, ext):
        @pl.when(pl.program_id(0) == 0)
        def _():
            ext[0:HALO, :] = jnp.zeros((HALO, N_CONVCOL), F32)

        a_c, a_v, b_a, sg = _conv_inputs(proj_ref)
        ext[HALO:HALO + tm, 0:256] = a_c * a_v
        ext[HALO:HALO + tm, 256:512] = b_a * sg
        ext[HALO:HALO + tm, 512:2048] = proj_ref[:, C_Q:C_Q + 1536]
        for (c0, c1, kw), w_ref in zip(_CONV_BLOCKS, (wa_ref, wb_ref, wc_ref)):
            for s0 in range(c0, c1, _CONV_STRIP):
                acc = jnp.zeros((tm, _CONV_STRIP), F32)
                for k in range(kw):
                    acc += w_ref[k:k + 1, s0 - c0:s0 - c0 + _CONV_STRIP] * ext[pl.ds(HALO - (kw - 1) + k, tm), s0:s0 + _CONV_STRIP]
                out_ref[:, s0:s0 + _CONV_STRIP] = acc
        ext[0:HALO, :] = ext[tm:tm + HALO, :]

    full = lambda a: pl.BlockSpec(a.shape, lambda i: (0, 0))
    return pl.pallas_call(
        body, name="conv_fwd", grid=(T // tm,),
        in_specs=[pl.BlockSpec((tm, IN_PAD), lambda i: (i, 0)), full(wa), full(wb), full(wc)],
        out_specs=pl.BlockSpec((tm, N_CONVCOL), lambda i: (i, 0)),
        out_shape=jax.ShapeDtypeStruct((T, N_CONVCOL), F32),
        scratch_shapes=[pltpu.VMEM((HALO + tm, N_CONVCOL), F32)],
        compiler_params=pltpu.CompilerParams(dimension_semantics=("arbitrary",), vmem_limit_bytes=VMEM_LIMIT),
    )(proj, wa, wb, wc)


def _conv_bwd(proj, dco, da_b, dz, dblk, wa, wb, wc, tm):
    T = proj.shape[0]
    tm = min(tm, T)
    n = T // tm

    def body(proj_ref, dco_ref, dab_ref, dz_ref, dblk_ref, wa_ref, wb_ref, wc_ref,
             dproj_ref, dwa_ref, dwb_ref, dwc_ref, ext, dxs):
        @pl.when(pl.program_id(0) == 0)
        def _():
            ext[tm:tm + HALO, :] = jnp.zeros((HALO, N_CONVCOL), F32)
            dwa_ref[...] = jnp.zeros_like(dwa_ref)
            dwb_ref[...] = jnp.zeros_like(dwb_ref)
            dwc_ref[...] = jnp.zeros_like(dwc_ref)

        ext[0:tm, :] = dco_ref[...]
        a_c, a_v, b_a, sg = _conv_inputs(proj_ref)

        def conv_in(s0):
            if s0 < 256:
                return a_c * a_v
            if s0 < 512:
                return b_a * sg
            return proj_ref[:, C_Q + s0 - 512:C_Q + s0 - 512 + _CONV_STRIP]

        for (c0, c1, kw), w_ref, dw_ref in zip(_CONV_BLOCKS, (wa_ref, wb_ref, wc_ref), (dwa_ref, dwb_ref, dwc_ref)):
            for s0 in range(c0, c1, _CONV_STRIP):
                xin = conv_in(s0)
                acc = jnp.zeros((tm, _CONV_STRIP), F32)
                for k in range(kw):
                    sh = ext[pl.ds(kw - 1 - k, tm), s0:s0 + _CONV_STRIP]
                    acc += w_ref[k:k + 1, s0 - c0:s0 - c0 + _CONV_STRIP] * sh
                    dw_ref[k:k + 1, s0 - c0:s0 - c0 + _CONV_STRIP] += _colsum(sh * xin)
                dxs[:, s0:s0 + _CONV_STRIP] = acc
        ext[tm:tm + HALO, :] = ext[0:HALO, :]

        dp, du0 = dxs[:, 0:256], dxs[:, 256:512]
        dproj_ref[:, C_AB:C_AB + 256] = dab_ref[...].astype(BF16)
        dproj_ref[:, C_AC:C_AC + 256] = (dp * a_v).astype(BF16)
        dproj_ref[:, C_AV:C_AV + 256] = (dp * a_c).astype(BF16)
        dproj_ref[:, C_BA:C_BA + 256] = (du0 * sg).astype(BF16)
        dproj_ref[:, C_BG:C_BG + 256] = (du0 * b_a * sg * (1.0 - sg)).astype(BF16)
        dproj_ref[:, C_Q:C_Q + 1536] = dxs[:, 512:2048].astype(BF16)
        dproj_ref[:, C_Z:C_Z + 512] = dz_ref[...].astype(BF16)
        dproj_ref[:, C_GB:C_GB + 128] = dblk_ref[...].astype(BF16)

    rev = lambda w: pl.BlockSpec((tm, w), lambda i: (n - 1 - i, 0))
    full = lambda a: pl.BlockSpec(a.shape, lambda i: (0, 0))
    return pl.pallas_call(
        body, name="conv_bwd", grid=(n,),
        in_specs=[rev(IN_PAD), rev(N_CONVCOL), rev(256), rev(512), rev(128), full(wa), full(wb), full(wc)],
        out_specs=[rev(IN_PAD), full(wa), full(wb), full(wc)],
        out_shape=[jax.ShapeDtypeStruct((T, IN_PAD), BF16), jax.ShapeDtypeStruct(wa.shape, F32),
                   jax.ShapeDtypeStruct(wb.shape, F32), jax.ShapeDtypeStruct(wc.shape, F32)],
        scratch_shapes=[pltpu.VMEM((tm + HALO, N_CONVCOL), F32), pltpu.VMEM((tm, N_CONVCOL), F32)],
        compiler_params=pltpu.CompilerParams(dimension_semantics=("arbitrary",), vmem_limit_bytes=VMEM_LIMIT),
    )(proj, dco, da_b, dz, dblk, wa, wb, wc)


def _ffn_out_fwd(x1, gu, g2, wfo, tm):
    def fn(x1, gu, g2, wfo):
        s = (_silu(gu[:, :D_FF]) * gu[:, D_FF:]).astype(BF16)
        f = jnp.dot(s, wfo, preferred_element_type=F32)
        return x1 + g2 * f, f

    return _rowwise("ffnout_fwd", fn, [x1, gu], [g2, wfo], [(D, F32), (D, F32)], [], tm)


def _ffn_out_bwd(dx2, gu, f, g2, wfo, tm):
    def fn(dx2, gu, f, g2, wfo):
        gate, up = gu[:, :D_FF], gu[:, D_FF:]
        sg = _sigmoid(gate)
        sl = gate * sg
        df = (dx2 * g2).astype(BF16)
        ds = lax.dot_general(df, wfo, (((1,), (1,)), ((), ())), preferred_element_type=F32)
        dgate = ds * up * (sg * (1.0 + gate * (1.0 - sg)))
        dgu = jnp.concatenate([dgate.astype(BF16), (ds * sl).astype(BF16)], axis=1)
        return dgu, sl * up, df, _colsum(dx2 * f)

    return _rowwise("ffnout_bwd", fn, [dx2, gu, f], [g2, wfo], [(2 * D_FF, BF16), (D_FF, BF16), (D, BF16)],
                    [((1, D), F32)], tm)


def _loss_bwd(x, tgt, gfin, tm):
    def fn(x, tgt, gfin):
        xn, r = _rms(x)
        e = xn * gfin - tgt
        loss = 0.5 * jnp.sum(jnp.mean(e * e, axis=-1, keepdims=True), axis=0, keepdims=True)
        dy = e * (1.0 / D)
        return _rms_bwd(dy * gfin, xn, r), jnp.broadcast_to(loss, (1, 128)), _colsum(dy * xn)

    return _rowwise("loss_bwd", fn, [x, tgt], [gfin], [(D, F32)], [((1, 128), F32), ((1, D), F32)], tm)


def _wgrad(name, a, b, bm=256, bk=512):
    T, M = a.shape
    N = b.shape[1]
    bk = min(bk, T)

    def body(a_ref, b_ref, o_ref):
        @pl.when(pl.program_id(1) == 0)
        def _():
            o_ref[...] = jnp.zeros_like(o_ref)

        o_ref[...] += lax.dot_general(a_ref[...], b_ref[...], (((0,), (0,)), ((), ())), preferred_element_type=F32)

    return pl.pallas_call(
        body, name=name, grid=(M // bm, T // bk),
        in_specs=[pl.BlockSpec((bk, bm), lambda i, k: (k, i)), pl.BlockSpec((bk, N), lambda i, k: (k, 0))],
        out_specs=pl.BlockSpec((bm, N), lambda i, k: (i, 0)),
        out_shape=jax.ShapeDtypeStruct((M, N), F32),
        compiler_params=pltpu.CompilerParams(dimension_semantics=("arbitrary", "arbitrary"), vmem_limit_bytes=VMEM_LIMIT),
    )(a, b)


def _adamw(w, g, m, v):
    m = B1 * m + (1.0 - B1) * g
    v = B2 * v + (1.0 - B2) * (g * g)
    m_hat = m / (1.0 - B1 ** STEP)
    v_hat = v / (1.0 - B2 ** STEP)
    return -LR * (m_hat / (jnp.sqrt(v_hat) + AEPS) + WD * w), m, v


def _adam_call(name, w, g, m, v, tm):
    C = w.shape[1]
    return _rowwise(name, _adamw, [w, g, m, v], [], [(C, F32)] * 3, [], tm)


def _reduce_adam(name, recv, w, m, v, tm):
    _, R, C = recv.shape
    flat = recv.reshape(N_DEV * R, C)

    def fn(*a):
        g = a[0]
        for p in a[1:N_DEV]:
            g = g + p
        return (g,) + _adamw(a[N_DEV], g, a[N_DEV + 1], a[N_DEV + 2])

    return _rowwise(name, fn, [_cols(flat, C, 0, j * R) for j in range(N_DEV)] + [w, m, v], [], [(C, F32)] * 4, [],
                    tm, rows=R)


_OFFSETS = [(dx, dy, dc) for dx in (0, 1) for dy in (0, 1) for dc in (0, 1)][1:]
_MESH = pl.DeviceIdType.MESH


def _coords():
    return lax.axis_index("x"), lax.axis_index("y"), lax.axis_index("c")


def _flip(me, off):
    return tuple((1 - m) if d else m for m, d in zip(me, off))


def _linear(p):
    return 4 * p[0] + 2 * p[1] + p[2]


def _exchange(name, arrs, scatter):
    n = len(arrs)
    k_peers = len(_OFFSETS)

    def body(*refs):
        ins, outs = refs[:n], refs[n:2 * n]
        send, recv, loc = refs[2 * n:]
        me = _coords()
        my = _linear(me)
        local = []
        for a in range(n):
            src = ins[a].at[my] if scatter else ins[a]
            cp = pltpu.make_async_copy(src, outs[a].at[my], loc.at[a])
            cp.start()
            local.append(cp)
        remote = []
        for a in range(n):
            for k, off in enumerate(_OFFSETS):
                peer = _flip(me, off)
                src = ins[a].at[_linear(peer)] if scatter else ins[a]
                cp = pltpu.make_async_remote_copy(src_ref=src, dst_ref=outs[a].at[my], send_sem=send.at[a * k_peers + k],
                                                  recv_sem=recv.at[a * k_peers + k], device_id=peer, device_id_type=_MESH)
                cp.start()
                remote.append(cp)
        for cp in remote:
            cp.wait()
        for cp in local:
            cp.wait()

    anyspec = pl.BlockSpec(memory_space=pl.ANY)
    out_shape = [jax.ShapeDtypeStruct(a.shape if scatter else (N_DEV,) + a.shape, a.dtype) for a in arrs]
    return pl.pallas_call(
        body, name=name, in_specs=[anyspec] * n, out_specs=[anyspec] * n, out_shape=out_shape,
        scratch_shapes=[pltpu.SemaphoreType.DMA((n * k_peers,)), pltpu.SemaphoreType.DMA((n * k_peers,)),
                        pltpu.SemaphoreType.DMA((n,))],
    )(*arrs)


def _small_allgather(name, packed):
    R = packed.shape[0]

    def body(in_ref, all_ref, sum_ref, send, recv):
        me = _coords()
        my = _linear(me)
        all_ref[my] = in_ref[...]
        copies = []
        for k, off in enumerate(_OFFSETS):
            cp = pltpu.make_async_remote_copy(src_ref=in_ref, dst_ref=all_ref.at[my], send_sem=send.at[k], recv_sem=recv.at[k],
                                              device_id=_flip(me, off), device_id_type=_MESH)
            cp.start()
            copies.append(cp)
        for cp in copies:
            cp.wait()
        acc = all_ref[0]
        for j in range(1, N_DEV):
            acc = acc + all_ref[j]
        sum_ref[...] = acc

    vm = pl.BlockSpec(memory_space=pltpu.VMEM)
    return pl.pallas_call(
        body, name=name, in_specs=[vm], out_specs=[vm, vm],
        out_shape=[jax.ShapeDtypeStruct((N_DEV, R, 128), F32), jax.ShapeDtypeStruct((R, 128), F32)],
        scratch_shapes=[pltpu.SemaphoreType.DMA((len(_OFFSETS),)), pltpu.SemaphoreType.DMA((len(_OFFSETS),))],
        compiler_params=pltpu.CompilerParams(vmem_limit_bytes=VMEM_LIMIT),
    )(packed)


def _ada_mod(c, w16, bias):
    nc = w16.shape[2]
    kp = len(_OFFSETS)

    def body(c_ref, w_ref, b_ref, rows_ref, act_ref, cbuf, sbuf, send, recv):
        me = _coords()
        my = _linear(me)
        cbuf[my] = c_ref[...]
        copies = []
        for k, off in enumerate(_OFFSETS):
            cp = pltpu.make_async_remote_copy(src_ref=c_ref, dst_ref=cbuf.at[my], send_sem=send.at[k], recv_sem=recv.at[k],
                                              device_id=_flip(me, off), device_id_type=_MESH)
            cp.start()
            copies.append(cp)
        for cp in copies:
            cp.wait()
        act = _silu(jnp.concatenate([cbuf[j] for j in range(N_DEV)], axis=0))
        act_ref[...] = act
        act16 = act.astype(BF16)
        for l in range(DEPTH):
            ml = jnp.dot(act16, w_ref[l], preferred_element_type=F32) + b_ref[l:l + 1, :]
            for j in range(N_DEV):
                sbuf[j, l:l + 1, :] = ml[j:j + 1, :]
        rows_ref[my] = sbuf[my]
        copies = []
        for k, off in enumerate(_OFFSETS):
            peer = _flip(me, off)
            cp = pltpu.make_async_remote_copy(src_ref=sbuf.at[_linear(peer)], dst_ref=rows_ref.at[my], send_sem=send.at[kp + k],
                                              recv_sem=recv.at[kp + k], device_id=peer, device_id_type=_MESH)
            cp.start()
            copies.append(cp)
        for cp in copies:
            cp.wait()

    vm = pl.BlockSpec(memory_space=pltpu.VMEM)
    return pl.pallas_call(
        body, name="ada_mod", in_specs=[vm, vm, vm], out_specs=[vm, vm],
        out_shape=[jax.ShapeDtypeStruct((N_DEV, DEPTH, nc), F32), jax.ShapeDtypeStruct((N_DEV, D), F32)],
        scratch_shapes=[pltpu.VMEM((N_DEV, 1, D), F32), pltpu.VMEM((N_DEV, DEPTH, nc), F32),
                        pltpu.SemaphoreType.DMA((2 * kp,)), pltpu.SemaphoreType.DMA((2 * kp,))],
        compiler_params=pltpu.CompilerParams(vmem_limit_bytes=VMEM_LIMIT),
    )(c, w16, bias)


def _pack(arrs):
    parts = []
    for a in arrs:
        f = a.reshape(-1).astype(F32)
        parts.append(jnp.pad(f, (0, (-f.shape[0]) % 128)))
    flat = jnp.concatenate(parts)
    flat = jnp.pad(flat, (0, (-flat.shape[0]) % 1024))
    return flat.reshape(-1, 128)


def _unpack(packed, shapes):
    flat = packed.reshape(packed.shape[:-2] + (-1,))
    out, r = [], 0
    for s in shapes:
        n = int(np.prod(s))
        out.append(flat[..., r:r + n].reshape(packed.shape[:-2] + tuple(s)))
        r += -(-n // 128) * 128
    return out


def _pad_rows(w, rows):
    return jnp.pad(w, ((0, 0), (0, rows - w.shape[1]), (0, 0)))


_SMALL = ("b_ada", "norm_mix_g", "norm_ffn_g", "conv_a_w", "conf_dw_w", "conf_dw_b", "conf_ln_g", "conf_ln_b",
          "dn_conv_w", "dn_a_log", "dn_dt_bias", "dn_norm_g", "final_norm_g")
_BIG = ("w_in", "w_out", "w_ffn_in", "w_ffn_out")
_WEIGHTS = ("w_ada", "b_ada", "norm_mix_g", "norm_ffn_g", "w_in", "conv_a_w", "conf_dw_w", "conf_dw_b", "conf_ln_g",
            "conf_ln_b", "dn_conv_w", "dn_a_log", "dn_dt_bias", "dn_norm_g", "w_out", "w_ffn_in", "w_ffn_out",
            "final_norm_g")


def _step(x, c, loss_target, W, M, V):
    T = x.shape[1]
    me = _linear(_coords())
    xs, tgt = x[0], loss_target[0]
    vec = lambda a: a.reshape(1, -1)

    nada = W["w_ada"].shape[2]
    rows, act_all = _ada_mod(c, W["w_ada"].astype(BF16), lax.dynamic_slice(W["b_ada"], (0, me * nada), (DEPTH, nada)))
    mod = rows.transpose(1, 0, 2).reshape(DEPTH, 6, 1, D)

    g_in, g_out, g_fi, g_fo = _exchange("gather_weights", [W[k].astype(BF16) for k in _BIG], scatter=False)
    win = jnp.pad(g_in.transpose(1, 2, 0, 3).reshape(DEPTH, D, IN_COLS), ((0, 0), (0, 0), (0, IN_PAD - IN_COLS)))
    wout = g_out.transpose(1, 0, 2, 3).reshape(DEPTH, D, D)
    wfi = g_fi.transpose(1, 2, 0, 3).reshape(DEPTH, D, 2 * D_FF)
    wfo = g_fo.transpose(1, 0, 2, 3).reshape(DEPTH, D_FF, D)
    conv_names = ("conv_a_w", "conf_dw_w", "dn_conv_w")
    conv_all, _ = _small_allgather("gather_conv_w", _pack([W[k] for k in conv_names]))
    conv_full = [t.transpose(1, 2, 0, 3).reshape(t.shape[1], t.shape[2], -1)
                 for t in _unpack(conv_all, [W[k].shape for k in conv_names])]
    wa, wb, wc = _pad_rows(conv_full[0], 8), _pad_rows(conv_full[1], 32), _pad_rows(conv_full[2], 8)
    lane_pad = lambda a: jnp.pad(a, ((0, 0), (0, 128 - a.shape[1])))
    alog, dtb = lane_pad(W["dn_a_log"]), lane_pad(W["dn_dt_bias"])

    saved = []
    xc = xs
    for l in range(DEPTH):
        sh1, sc1, g1, sh2, sc2, g2 = [mod[l, i] for i in range(6)]
        proj, h = _normproj_fwd("inproj_fwd", xc, sh1, sc1, vec(W["norm_mix_g"][l]), win[l], 512)
        convout = _conv_fwd(proj, wa[l], wb[l], wc[l], 256)
        y_ab, qkv, gb = _stage2_fwd(proj, convout, vec(W["conf_dw_b"][l]), vec(W["conf_ln_g"][l]), vec(W["conf_ln_b"][l]),
                                    alog[l:l + 1], dtb[l:l + 1], 256)
        o, ss = _delta_fwd(qkv, gb, DELTA_NB)
        x1, mix, ycat = _outproj_fwd(xc, o, proj, y_ab, g1, vec(W["dn_norm_g"][l]), wout[l], 512)
        gu, h2 = _normproj_fwd("ffnin_fwd", x1, sh2, sc2, vec(W["norm_ffn_g"][l]), wfi[l], 256)
        x2, f = _ffn_out_fwd(x1, gu, g2, wfo[l], 256)
        saved.append((xc, proj, h, convout, qkv, gb, o, ss, mix, ycat, x1, gu, h2, f))
        xc = x2

    dx, loss_row, d_gfin = _loss_bwd(xc, tgt, vec(W["final_norm_g"]), 512)
    loss = lax.psum(loss_row[0, 0], ("x", "y", "c"))

    big_out = {k: [None] * DEPTH for k in _BIG}
    dmod, small = [None] * DEPTH, [None] * DEPTH
    for l in reversed(range(DEPTH)):
        xc, proj, h, convout, qkv, gb, o, ss, mix, ycat, x1, gu, h2, f = saved[l]
        sh1, sc1, g1, sh2, sc2, g2 = [mod[l, i] for i in range(6)]
        gm, gf = vec(W["norm_mix_g"][l]), vec(W["norm_ffn_g"][l])
        bb, lg, lb = vec(W["conf_dw_b"][l]), vec(W["conf_ln_g"][l]), vec(W["conf_ln_b"][l])
        dng = vec(W["dn_norm_g"][l])

        dgu, s, df, d_g2 = _ffn_out_bwd(dx, gu, f, g2, wfo[l], 256)
        gw_fo = _wgrad("wgrad_ffn_out", s, df)
        dx1, d_sh2, d_sc2, d_gf = _normproj_bwd("ffnin_bwd", x1, dgu, dx, sc2, gf, wfi[l], 256)
        gw_fi = _wgrad("wgrad_ffn_in", h2, dgu)
        dmix, dy_ab, do, dz, d_g1, d_dng = _outproj_bwd(dx1, mix, o, proj, g1, dng, wout[l], 512)
        gw_out = _wgrad("wgrad_out", ycat, dmix)
        dqkv = _delta_bwd(qkv, gb, ss, do, DELTA_NB)
        dco, da_b, dblk, d_bb, d_lg, d_lb, d_alog, d_dtb = _stage2_bwd(
            proj, convout, dy_ab, _cols(dqkv, D_DN, 0), _cols(dqkv, D_DN, 1), _cols(dqkv, D_DN, 2),
            _cols(dqkv, 128, 3 * D_DN // 128), bb, lg, lb, alog[l:l + 1], dtb[l:l + 1], 256)
        dproj, d_wa, d_wb, d_wc = _conv_bwd(proj, dco, da_b, dz, dblk, wa[l], wb[l], wc[l], 256)
        dx, d_sh1, d_sc1, d_gm = _normproj_bwd("inproj_bwd", xc, dproj, dx1, sc1, gm, win[l], 512)
        gw_in = _wgrad("wgrad_in", h, dproj)

        dmod[l] = jnp.concatenate([d_sh1, d_sc1, d_g1, d_sh2, d_sc2, d_g2], axis=1)
        small[l] = dict(norm_mix_g=d_gm, norm_ffn_g=d_gf, conv_a_w=d_wa[:KA], conf_dw_w=d_wb[:KB], conf_dw_b=d_bb,
                        conf_ln_g=d_lg, conf_ln_b=d_lb, dn_conv_w=d_wc[:KC], dn_a_log=d_alog, dn_dt_bias=d_dtb,
                        dn_norm_g=d_dng)

        parts = [gw_in[:, :IN_COLS].reshape(D, N_DEV, -1).transpose(1, 0, 2), gw_out.reshape(N_DEV, -1, D),
                 gw_fi.reshape(D, N_DEV, -1).transpose(1, 0, 2), gw_fo.reshape(N_DEV, -1, D)]
        recv = _exchange("exchange_grads", parts, scatter=True)
        for k, r in zip(_BIG, recv):
            big_out[k][l] = _reduce_adam("reduce_adam_" + k, r, W[k][l], M[k][l], V[k][l],
                                         256 if r.shape[1] % 256 == 0 else r.shape[1])

    names = ("norm_mix_g", "norm_ffn_g", "conv_a_w", "conf_dw_w", "conf_dw_b", "conf_ln_g", "conf_ln_b", "dn_conv_w",
             "dn_a_log", "dn_dt_bias", "dn_norm_g")
    pieces = [jnp.stack(dmod)] + [jnp.stack([small[l][k] for l in range(DEPTH)]) for k in names] + [d_gfin]
    shapes = [p.shape for p in pieces]
    every, total = _small_allgather("gather_small_grads", _pack(pieces))
    tot = dict(zip(("dmod",) + names + ("final_norm_g",), _unpack(total, shapes)))
    dmod_all = _unpack(every, shapes[:1])[0]

    grads = {}
    grads["b_ada"] = tot["dmod"].reshape(DEPTH, 6 * D)
    for k in ("norm_mix_g", "norm_ffn_g", "conf_dw_b", "conf_ln_g", "conf_ln_b", "dn_norm_g"):
        grads[k] = tot[k].reshape(W[k].shape)
    grads["dn_a_log"] = tot["dn_a_log"].reshape(DEPTH, 128)[:, :HEADS]
    grads["dn_dt_bias"] = tot["dn_dt_bias"].reshape(DEPTH, 128)[:, :HEADS]
    grads["final_norm_g"] = tot["final_norm_g"].reshape(D)
    for k in conv_names:
        nloc = W[k].shape[2]
        grads[k] = lax.dynamic_slice_in_dim(tot[k], me * nloc, nloc, axis=2)

    dm = lax.dynamic_slice_in_dim(dmod_all.reshape(N_DEV, DEPTH, 6 * D), me * nada, nada, axis=2)
    pad16 = lambda a: jnp.pad(a, ((0, 16 - N_DEV), (0, 0))).astype(BF16)
    g_ada = _wgrad("wgrad_ada", pad16(act_all), pad16(dm.reshape(N_DEV, DEPTH * nada)))
    grads["w_ada"] = g_ada.reshape(D, DEPTH, nada).transpose(1, 0, 2)

    delta, new_m, new_v = {}, {}, {}
    r2 = lambda a: a.reshape(DEPTH * D, nada)
    d_, m_, v_ = _adam_call("adam_ada", r2(W["w_ada"]), r2(grads["w_ada"]), r2(M["w_ada"]), r2(V["w_ada"]), 512)
    delta["w_ada"], new_m["w_ada"], new_v["w_ada"] = [t.reshape(W["w_ada"].shape) for t in (d_, m_, v_)]
    sshapes = [W[k].shape for k in _SMALL]
    d_, m_, v_ = _adam_call("adam_small", _pack([W[k] for k in _SMALL]), _pack([grads[k] for k in _SMALL]),
                            _pack([M[k] for k in _SMALL]), _pack([V[k] for k in _SMALL]), 4096)
    for dst, packed in ((delta, d_), (new_m, m_), (new_v, v_)):
        dst.update(zip(_SMALL, _unpack(packed, sshapes)))
    for k in _BIG:
        g_, d_, m_, v_ = [jnp.stack([big_out[k][l][i] for l in range(DEPTH)]) for i in range(4)]
        grads[k], delta[k], new_m[k], new_v[k] = g_, d_, m_, v_

    return (loss, dx[None], *[grads[k] for k in _WEIGHTS], *[delta[k] for k in _WEIGHTS],
            *[new_m[k] for k in _WEIGHTS], *[new_v[k] for k in _WEIGHTS])


def kernel(x, c, w_ada, b_ada, norm_mix_g, norm_ffn_g, w_in, conv_a_w, conf_dw_w, conf_dw_b, conf_ln_g, conf_ln_b, dn_conv_w, dn_a_log, dn_dt_bias, dn_norm_g, w_out, w_ffn_in, w_ffn_out, final_norm_g, loss_target, m_w_ada, m_b_ada, m_norm_mix_g, m_norm_ffn_g, m_w_in, m_conv_a_w, m_conf_dw_w, m_conf_dw_b, m_conf_ln_g, m_conf_ln_b, m_dn_conv_w, m_dn_a_log, m_dn_dt_bias, m_dn_norm_g, m_w_out, m_w_ffn_in, m_w_ffn_out, m_final_norm_g, v_w_ada, v_b_ada, v_norm_mix_g, v_norm_ffn_g, v_w_in, v_conv_a_w, v_conf_dw_w, v_conf_dw_b, v_conf_ln_g, v_conf_ln_b, v_dn_conv_w, v_dn_a_log, v_dn_dt_bias, v_dn_norm_g, v_w_out, v_w_ffn_in, v_w_ffn_out, v_final_norm_g):
    a = dict(locals())
    W = {k: a[k] for k in _WEIGHTS}
    M = {k: a["m_" + k] for k in _WEIGHTS}
    V = {k: a["v_" + k] for k in _WEIGHTS}
    return _step(x, c, loss_target, W, M, V)
```

```python
import functools

import jax
import jax.numpy as jnp
import numpy as np
from jax import lax
from jax.experimental import pallas as pl
from jax.experimental.pallas import tpu as pltpu

F32 = jnp.float32
BF16 = jnp.bfloat16

N_DEV = 8
D = 1024
DEPTH = 4
D_CONV = 256
D_CONF = 256
D_DN = 512
HEADS = 4
HD = 128
KA, KB, KC = 3, 31, 4
CHUNK = 64
D_FF = 2816
IN_COLS = 3336
IN_PAD = 3456
N_CONVCOL = 2048
EPS = 1e-6
LN_EPS = 1e-5
HALO = 32
VMEM_LIMIT = 56 * 1024 * 1024
DELTA_NB = 4

C_AB, C_AC, C_AV, C_BA, C_BG, C_Q, C_Z, C_GB = 0, 256, 512, 768, 1024, 1280, 2816, 3328

LR, B1, B2, AEPS, WD, STEP = 0.001, 0.9, 0.999, 1e-08, 0.01, 10


def _dot(a, b, dims, hi):
    if hi:
        return lax.dot_general(a.astype(F32), b.astype(F32), (dims, ((), ())), precision=lax.Precision.HIGHEST,
                               preferred_element_type=F32)
    return lax.dot_general(a.astype(BF16), b.astype(BF16), (dims, ((), ())), preferred_element_type=F32)


@functools.partial(jax.custom_vjp, nondiff_argnums=(2,))
def mm_nn(a, b, hi=False):
    return _dot(a, b, ((1,), (0,)), hi)


@functools.partial(jax.custom_vjp, nondiff_argnums=(2,))
def mm_nt(a, b, hi=False):
    return _dot(a, b, ((1,), (1,)), hi)


@functools.partial(jax.custom_vjp, nondiff_argnums=(2,))
def mm_tn(a, b, hi=False):
    return _dot(a, b, ((0,), (0,)), hi)


mm_nn.defvjp(lambda a, b, hi: (mm_nn(a, b, hi), (a, b)),
             lambda hi, r, g: (mm_nt(g, r[1], hi), mm_tn(r[0], g, hi)))
mm_nt.defvjp(lambda a, b, hi: (mm_nt(a, b, hi), (a, b)),
             lambda hi, r, g: (mm_nn(g, r[1], hi), mm_tn(g, r[0], hi)))
mm_tn.defvjp(lambda a, b, hi: (mm_tn(a, b, hi), (a, b)),
             lambda hi, r, g: (mm_nt(r[1], g, hi), mm_nn(r[0], g, hi)))


def _sigmoid(x):
    return 1.0 / (1.0 + jnp.exp(-x))


def _silu(x):
    return x * _sigmoid(x)


def _softplus(x):
    return jnp.maximum(x, 0.0) + jnp.log(1.0 + jnp.exp(-jnp.abs(x)))


def _iota2(shape, dim):
    return lax.broadcasted_iota(jnp.int32, shape, dim)


def _dot16(a, b):
    return jnp.dot(a.astype(BF16), b.astype(BF16), preferred_element_type=F32)


def _dot_3pass(a, b):
    ah = a.astype(BF16)
    bh = b.astype(BF16)
    al = (a - ah.astype(F32)).astype(BF16)
    bl = (b - bh.astype(F32)).astype(BF16)
    d = lambda x, y: jnp.dot(x, y, preferred_element_type=F32)
    return d(ah, bh) + (d(ah, bl) + d(al, bh))


@jax.custom_vjp
def _unit_lower_inverses(Xs):
    n = Xs[0].shape[0]
    r, c = _iota2((n, n), 0), _iota2((n, n), 1)
    eye = (r == c).astype(F32)

    def joins(b):
        s = b.bit_length() - 1
        return ((r >> (s + 1)) == (c >> (s + 1))) & (((r >> s) & 1) == 1) & (((c >> s) & 1) == 0)

    Ts = [eye + jnp.where(joins(1), x, 0.0) for x in Xs]
    b = 2
    while b < n:
        m = joins(b)
        Ys = [_dot16(jnp.where(m, x, 0.0), t) for x, t in zip(Xs, Ts)]
        Ts = [t + _dot16(t, y) for t, y in zip(Ts, Ys)]
        b *= 2
    Rs = [(eye - t) + _dot_3pass(x, t) for x, t in zip(Xs, Ts)]
    return [t + _dot16(t, r_) for t, r_ in zip(Ts, Rs)]


def _unit_lower_inverses_fwd(Xs):
    Ts = _unit_lower_inverses(Xs)
    return Ts, Ts


def _unit_lower_inverses_bwd(Ts, gs):
    inner = [mm_nt(g, t) for g, t in zip(gs, Ts)]
    return ([mm_tn(t, i) for t, i in zip(Ts, inner)],)


_unit_lower_inverses.defvjp(_unit_lower_inverses_fwd, _unit_lower_inverses_bwd)

def _delta_chunks(qs, ks, vs, gbs, Ss):
    C = CHUNK
    nb = len(gbs)
    pairs = [(c, h) for c in range(nb) for h in range(HEADS)]
    each = lambda fn, *lists: [fn(*a) for a in zip(*lists)]
    row = _iota2((C, C), 0)
    col = _iota2((C, C), 1)
    causal = row >= col
    strict = row > col
    tri = causal.astype(F32)
    eye = (row == col).astype(F32)
    lane = _iota2((C, 128), 1)
    subl = _iota2((128, C), 0)
    last = (_iota2((C, 1), 0) == C - 1).astype(F32)

    gc_all = [mm_nn(tri, gb, True) for gb in gbs]
    gc_t = [g.T for g in gc_all]
    q = [qs[c][h] * (HD ** -0.5) for c, h in pairs]
    k = [ks[c][h] for c, h in pairs]
    v = [vs[c][h] for c, h in pairs]
    gcol = [jnp.sum(jnp.where(lane == h, gc_all[c], 0.0), axis=1, keepdims=True) for c, h in pairs]
    grow = [jnp.sum(jnp.where(subl == h, gc_t[c], 0.0), axis=0, keepdims=True) for c, h in pairs]
    beta = [jnp.sum(jnp.where(lane == HEADS + h, gbs[c], 0.0), axis=1, keepdims=True) for c, h in pairs]
    decay = each(lambda a, b: jnp.where(causal, jnp.exp(jnp.where(causal, a - b, 0.0)), 0.0), gcol, grow)
    kb = each(lambda a, b: a * b, k, beta)
    vb = each(lambda a, b: a * b, v, beta)
    kk = each(lambda a, b: mm_nt(a, b), kb, k)
    X = each(lambda a, d: -jnp.where(strict, a * d, 0.0), kk, decay)
    T = _unit_lower_inverses(X)
    eg = [jnp.exp(g) for g in gcol]
    u = each(lambda t, a: mm_nn(t, a), T, vb)
    w = each(lambda t, a, e: mm_nn(t, a * e), T, kb, eg)
    qk = each(lambda a, b, d: jnp.where(causal, mm_nt(a, b) * d, 0.0), q, k, decay)
    qg = each(lambda a, e: a * e, q, eg)
    g_last = [jnp.sum(g * last, axis=0, keepdims=True) for g in gcol]
    kd = each(lambda a, gl, g: a * jnp.exp(gl - g), k, g_last, gcol)
    eg_last = [jnp.exp(g) for g in g_last]

    outs = []
    for c in range(nb):
        sl = slice(c * HEADS, (c + 1) * HEADS)
        v_new = each(lambda a, b, S: a - mm_nn(b, S), u[sl], w[sl], Ss)
        oS = each(lambda a, S: mm_nn(a, S), qg[sl], Ss)
        outs.append(each(lambda a, b, n: a + mm_nn(b, n), oS, qk[sl], v_new))
        Ss = each(lambda S, e, a, n: S * e + mm_tn(a, n), Ss, eg_last[sl], kd[sl], v_new)
    return outs, Ss


def _split_chunks(ref, nb):
    return [[ref[c * CHUNK:(c + 1) * CHUNK, h * HD:(h + 1) * HD] for h in range(HEADS)] for c in range(nb)]


def _join_chunks(vals):
    return jnp.concatenate([jnp.concatenate(heads, axis=1) for heads in vals], axis=0)


def _delta_fwd(qkv, gb, nb):
    T = qkv.shape[0]
    nb = min(nb, T // CHUNK)
    rows = nb * CHUNK
    n = T // rows

    def body(q_ref, k_ref, v_ref, gb_ref, o_ref, ss_ref, s_scr):
        @pl.when(pl.program_id(0) == 0)
        def _():
            s_scr[...] = jnp.zeros_like(s_scr)

        Ss = [s_scr[h] for h in range(HEADS)]
        for h in range(HEADS):
            ss_ref[0, h] = Ss[h]
        gbs = [gb_ref[c * CHUNK:(c + 1) * CHUNK, :] for c in range(nb)]
        outs, new_S = _delta_chunks(_split_chunks(q_ref, nb), _split_chunks(k_ref, nb), _split_chunks(v_ref, nb), gbs, Ss)
        o_ref[...] = _join_chunks(outs)
        s_scr[...] = jnp.stack(new_S)

    row = lambda w, j=0: pl.BlockSpec((rows, w), lambda i: (i, j))
    return pl.pallas_call(
        body, name="delta_fwd", grid=(n,),
        in_specs=[row(D_DN, 0), row(D_DN, 1), row(D_DN, 2), row(128)],
        out_specs=[row(D_DN), pl.BlockSpec((1, HEADS, HD, HD), lambda i: (i, 0, 0, 0))],
        out_shape=[jax.ShapeDtypeStruct((T, D_DN), F32), jax.ShapeDtypeStruct((n, HEADS, HD, HD), F32)],
        scratch_shapes=[pltpu.VMEM((HEADS, HD, HD), F32)],
        compiler_params=pltpu.CompilerParams(dimension_semantics=("arbitrary",), vmem_limit_bytes=VMEM_LIMIT),
    )(qkv, qkv, qkv, gb)


def _delta_bwd(qkv, gb, ss, do, nb):
    T = qkv.shape[0]
    nb = min(nb, T // CHUNK)
    rows = nb * CHUNK
    n = T // rows

    def body(q_ref, k_ref, v_ref, gb_ref, ss_ref, do_ref, d_ref, ds_scr):
        @pl.when(pl.program_id(0) == 0)
        def _():
            ds_scr[...] = jnp.zeros_like(ds_scr)

        Ss = [ss_ref[0, h] for h in range(HEADS)]
        gbs = [gb_ref[c * CHUNK:(c + 1) * CHUNK, :] for c in range(nb)]
        _, vjp = jax.vjp(_delta_chunks, _split_chunks(q_ref, nb), _split_chunks(k_ref, nb), _split_chunks(v_ref, nb), gbs, Ss)
        dqs, dks, dvs, dgbs, dSs = vjp((_split_chunks(do_ref, nb), [ds_scr[h] for h in range(HEADS)]))
        d_ref[...] = jnp.concatenate([_join_chunks(dqs), _join_chunks(dks), _join_chunks(dvs),
                                      jnp.concatenate(dgbs, axis=0)], axis=1)
        ds_scr[...] = jnp.stack(dSs)

    row = lambda w, j=0: pl.BlockSpec((rows, w), lambda i: (n - 1 - i, j))
    return pl.pallas_call(
        body, name="delta_bwd", grid=(n,),
        in_specs=[row(D_DN, 0), row(D_DN, 1), row(D_DN, 2), row(128),
                  pl.BlockSpec((1, HEADS, HD, HD), lambda i: (n - 1 - i, 0, 0, 0)), row(D_DN)],
        out_specs=row(3 * D_DN + 128),
        out_shape=jax.ShapeDtypeStruct((T, 3 * D_DN + 128), F32),
        scratch_shapes=[pltpu.VMEM((HEADS, HD, HD), F32)],
        compiler_params=pltpu.CompilerParams(dimension_semantics=("arbitrary",), vmem_limit_bytes=VMEM_LIMIT),
    )(qkv, qkv, qkv, gb, ss, do)


def _cols(arr, width, index, first_row=0):
    return (arr, width, index, first_row)


def _rowwise(name, fn, tiled, consts, out_tiled, out_acc, tm, rows=None):
    tiled = [t if isinstance(t, tuple) else (t, t.shape[1], 0, 0) for t in tiled]
    T = tiled[0][0].shape[0] if rows is None else rows
    tm = min(tm, T)
    assert T % tm == 0 and all(r % tm == 0 for (_, _, _, r) in tiled)
    n_t, n_c, n_o, n_a = len(tiled), len(consts), len(out_tiled), len(out_acc)

    def body(*refs):
        ins = [r[...] for r in refs[:n_t + n_c]]
        outs = fn(*ins)
        o_refs = refs[n_t + n_c:n_t + n_c + n_o]
        a_refs = refs[n_t + n_c + n_o:]
        for r, val in zip(o_refs, outs[:n_o]):
            r[...] = val.astype(r.dtype)
        if n_a:
            @pl.when(pl.program_id(0) == 0)
            def _():
                for r in a_refs:
                    r[...] = jnp.zeros_like(r)
            for r, val in zip(a_refs, outs[n_o:]):
                r[...] += val

    def const_spec(a):
        nd = a.ndim
        return pl.BlockSpec(a.shape, lambda i: (0,) * nd, pipeline_mode=pl.Buffered(1))

    in_specs = [pl.BlockSpec((tm, w), functools.partial(lambda i, j, r: (i + r, j), j=j, r=r0 // tm))
                for (_, w, j, r0) in tiled]
    in_specs += [const_spec(a) for a in consts]
    out_specs = [pl.BlockSpec((tm, w), lambda i: (i, 0)) for (w, _) in out_tiled]
    out_specs += [pl.BlockSpec(s, lambda i: (0, 0)) for (s, _) in out_acc]
    out_shape = [jax.ShapeDtypeStruct((T, w), dt) for (w, dt) in out_tiled]
    out_shape += [jax.ShapeDtypeStruct(s, dt) for (s, dt) in out_acc]
    return pl.pallas_call(
        body, name=name, grid=(T // tm,), in_specs=in_specs, out_specs=out_specs, out_shape=out_shape,
        compiler_params=pltpu.CompilerParams(dimension_semantics=("arbitrary",), vmem_limit_bytes=VMEM_LIMIT),
    )(*[t[0] for t in tiled], *consts)


def _colsum(x):
    return jnp.sum(x, axis=0, keepdims=True)


def _rms(x):
    r = lax.rsqrt(jnp.mean(x * x, axis=-1, keepdims=True) + EPS)
    return x * r, r


def _rms_bwd(dxn, xn, r):
    return r * (dxn - xn * jnp.mean(dxn * xn, axis=-1, keepdims=True))


def _normproj_fwd(name, x, sh, sc, g, w, tm):
    def fn(x, sh, sc, g, w):
        xn, _ = _rms(x)
        h = (xn * (g * (1.0 + sc)) + sh).astype(BF16)
        return jnp.dot(h, w, preferred_element_type=F32), h

    return _rowwise(name, fn, [x], [sh, sc, g, w], [(w.shape[1], F32), (D, BF16)], [], tm)


def _normproj_bwd(name, x, dpre, dres, sc, g, w, tm):
    def fn(x, dpre, dres, sc, g, w):
        xn, r = _rms(x)
        dh = lax.dot_general(dpre, w, (((1,), (1,)), ((), ())), preferred_element_type=F32)
        da = _colsum(dh * xn)
        dx = _rms_bwd(dh * (g * (1.0 + sc)), xn, r) + dres
        return dx, _colsum(dh), da * g, da * (1.0 + sc)

    vec = ((1, D), F32)
    return _rowwise(name, fn, [x, dpre, dres], [sc, g, w], [(D, F32)], [vec, vec, vec], tm)


def _stage2(a_b, blk, cp, u1c, qp, kp, vp, bb, lg, lb, alog, dtb):
    y_a = a_b * cp
    u1 = u1c + bb
    mu = jnp.mean(u1, axis=-1, keepdims=True)
    uc = u1 - mu
    var = jnp.mean(uc * uc, axis=-1, keepdims=True)
    y_b = _silu(uc * lax.rsqrt(var + LN_EPS) * lg + lb)

    def l2(t):
        t = _silu(t)
        return t * lax.rsqrt(jnp.sum(t * t, axis=-1, keepdims=True) + EPS)

    q = [l2(t) for t in qp]
    k = [l2(t) for t in kp]
    v = _silu(vp)
    lane = _iota2(blk.shape, 1)
    gdec = -jnp.exp(alog) * _softplus(blk + dtb)
    gb = jnp.where(lane < HEADS, gdec, jnp.where(lane < 2 * HEADS, _sigmoid(blk), 0.0))
    return y_a, y_b, q, k, v, gb


def _heads_of(x, base=0):
    return [x[:, base + h * HD:base + (h + 1) * HD] for h in range(HEADS)]


def _stage2_fwd(proj, convout, bb, lg, lb, alog, dtb, tm):
    def fn(a_b, blk, co, bb, lg, lb, alog, dtb):
        y_a, y_b, q, k, v, gb = _stage2(a_b, blk, co[:, 0:256], co[:, 256:512], _heads_of(co, 512), _heads_of(co, 1024),
                                        co[:, 1536:2048], bb, lg, lb, alog, dtb)
        return jnp.concatenate([y_a, y_b], axis=1), jnp.concatenate(q + k + [v], axis=1), gb

    return _rowwise("stage2_fwd", fn, [_cols(proj, 256, 0), _cols(proj, 128, C_GB // 128), convout],
                    [bb, lg, lb, alog, dtb], [(512, BF16), (1536, F32), (128, F32)], [], tm)


def _stage2_bwd(proj, convout, dy_ab, dq, dk, dv, dgb, bb, lg, lb, alog, dtb, tm):
    def fn(a_b, blk, co, dy_ab, dq, dk, dv, dgb, bb, lg, lb, alog, dtb):
        args = (a_b, blk, co[:, 0:256], co[:, 256:512], _heads_of(co, 512), _heads_of(co, 1024), co[:, 1536:2048],
                bb, lg, lb, alog, dtb)
        _, vjp = jax.vjp(_stage2, *args)
        ct = (dy_ab[:, 0:256], dy_ab[:, 256:512], _heads_of(dq), _heads_of(dk), dv, dgb)
        da_b, dblk, dcp, du1c, dqp, dkp, dvp, dbb, dlg, dlb, dalog, ddtb = vjp(ct)
        dco = jnp.concatenate([dcp, du1c] + dqp + dkp + [dvp], axis=1)
        return dco, da_b, dblk, dbb, dlg, dlb, dalog, ddtb

    v256, v128 = ((1, 256), F32), ((1, 128), F32)
    return _rowwise("stage2_bwd", fn,
                    [_cols(proj, 256, 0), _cols(proj, 128, C_GB // 128), convout, dy_ab, dq, dk, dv, dgb],
                    [bb, lg, lb, alog, dtb], [(N_CONVCOL, F32), (256, F32), (128, F32)],
                    [v256, v256, v256, v128, v128], tm)


def _stage3(o, z, dng):
    ys = []
    for oh, zh in zip(o, z):
        on = oh * lax.rsqrt(jnp.mean(oh * oh, axis=-1, keepdims=True) + EPS)
        ys.append(on * dng * _silu(zh))
    return ys


def _outproj_fwd(x, o, proj, y_ab, g1, dng, wout, tm):
    def fn(x, o, z0, z1, z2, z3, y_ab, g1, dng, wout):
        y_c = _stage3(_heads_of(o), [z0, z1, z2, z3], dng)
        ycat = jnp.concatenate([y_ab] + [t.astype(BF16) for t in y_c], axis=1)
        mix = jnp.dot(ycat, wout, preferred_element_type=F32)
        return x + g1 * mix, mix, ycat

    return _rowwise("outproj_fwd", fn, [x, o] + _z_heads(proj) + [y_ab],
                    [g1, dng, wout], [(D, F32), (D, F32), (D, BF16)], [], tm)


def _z_heads(proj):
    return [_cols(proj, HD, C_Z // HD + h) for h in range(HEADS)]


def _outproj_bwd(dx1, mix, o, proj, g1, dng, wout, tm):
    def fn(dx1, mix, o, z0, z1, z2, z3, g1, dng, wout):
        dmix = (dx1 * g1).astype(BF16)
        dycat = lax.dot_general(dmix, wout, (((1,), (1,)), ((), ())), preferred_element_type=F32)
        _, vjp = jax.vjp(_stage3, _heads_of(o), [z0, z1, z2, z3], dng)
        do, dz, ddng = vjp(_heads_of(dycat, 512))
        return (dmix, dycat[:, 0:512], jnp.concatenate(do, axis=1), jnp.concatenate(dz, axis=1),
                _colsum(dx1 * mix), ddng)

    return _rowwise("outproj_bwd", fn, [dx1, mix, o] + _z_heads(proj), [g1, dng, wout],
                    [(D, BF16), (512, F32), (512, F32), (512, F32)], [((1, D), F32), ((1, HD), F32)], tm)


_CONV_BLOCKS = ((0, 256, KA), (256, 512, KB), (512, 2048, KC))
_CONV_STRIP = 256


def _conv_inputs(proj_ref):
    a_c, a_v = proj_ref[:, C_AC:C_AC + 256], proj_ref[:, C_AV:C_AV + 256]
    b_a, b_g = proj_ref[:, C_BA:C_BA + 256], proj_ref[:, C_BG:C_BG + 256]
    return a_c, a_v, b_a, _sigmoid(b_g)


def _conv_fwd(proj, wa, wb, wc, tm):
    T = proj.shape[0]
    tm = min(tm, T)

    def body(proj_ref, wa_ref, wb_ref, wc_ref, out_ref, ext):
        @pl.when(pl.program_id(0) == 0)
        def _():
            ext[0:HALO, :] = jnp.zeros((HALO, N_CONVCOL), F32)

        a_c, a_v, b_a, sg = _conv_inputs(proj_ref)
        ext[HALO:HALO + tm, 0:256] = a_c * a_v
        ext[HALO:HALO + tm, 256:512] = b_a * sg
        ext[HALO:HALO + tm, 512:2048] = proj_ref[:, C_Q:C_Q + 1536]
        for (c0, c1, kw), w_ref in zip(_CONV_BLOCKS, (wa_ref, wb_ref, wc_ref)):
            for s0 in range(c0, c1, _CONV_STRIP):
                acc = jnp.zeros((tm, _CONV_STRIP), F32)
                for k in range(kw):
                    acc += w_ref[k:k + 1, s0 - c0:s0 - c0 + _CONV_STRIP] * ext[pl.ds(HALO - (kw - 1) + k, tm), s0:s0 + _CONV_STRIP]
                out_ref[:, s0:s0 + _CONV_STRIP] = acc
        ext[0:HALO, :] = ext[tm:tm + HALO, :]

    full = lambda a: pl.BlockSpec(a.shape, lambda i: (0, 0))
    return pl.pallas_call(
        body, name="conv_fwd", grid=(T // tm,),
        in_specs=[pl.BlockSpec((tm, IN_PAD), lambda i: (i, 0)), full(wa), full(wb), full(wc)],
        out_specs=pl.BlockSpec((tm, N_CONVCOL), lambda i: (i, 0)),
        out_shape=jax.ShapeDtypeStruct((T, N_CONVCOL), F32),
        scratch_shapes=[pltpu.VMEM((HALO + tm, N_CONVCOL), F32)],
        compiler_params=pltpu.CompilerParams(dimension_semantics=("arbitrary",), vmem_limit_bytes=VMEM_LIMIT),
    )(proj, wa, wb, wc)


def _conv_bwd(proj, dco, da_b, dz, dblk, wa, wb, wc, tm):
    T = proj.shape[0]
    tm = min(tm, T)
    n = T // tm

    def body(proj_ref, dco_ref, dab_ref, dz_ref, dblk_ref, wa_ref, wb_ref, wc_ref,
             dproj_ref, dwa_ref, dwb_ref, dwc_ref, ext, dxs):
        @pl.when(pl.program_id(0) == 0)
        def _():
            ext[tm:tm + HALO, :] = jnp.zeros((HALO, N_CONVCOL), F32)
            dwa_ref[...] = jnp.zeros_like(dwa_ref)
            dwb_ref[...] = jnp.zeros_like(dwb_ref)
            dwc_ref[...] = jnp.zeros_like(dwc_ref)

        ext[0:tm, :] = dco_ref[...]
        a_c, a_v, b_a, sg = _conv_inputs(proj_ref)

        def conv_in(s0):
            if s0 < 256:
                return a_c * a_v
            if s0 < 512:
                return b_a * sg
            return proj_ref[:, C_Q + s0 - 512:C_Q + s0 - 512 + _CONV_STRIP]

        for (c0, c1, kw), w_ref, dw_ref in zip(_CONV_BLOCKS, (wa_ref, wb_ref, wc_ref), (dwa_ref, dwb_ref, dwc_ref)):
            for s0 in range(c0, c1, _CONV_STRIP):
                xin = conv_in(s0)
                acc = jnp.zeros((tm, _CONV_STRIP), F32)
                for k in range(kw):
                    sh = ext[pl.ds(kw - 1 - k, tm), s0:s0 + _CONV_STRIP]
                    acc += w_ref[k:k + 1, s0 - c0:s0 - c0 + _CONV_STRIP] * sh
                    dw_ref[k:k + 1, s0 - c0:s0 - c0 + _CONV_STRIP] += _colsum(sh * xin)
                dxs[:, s0:s0 + _CONV_STRIP] = acc
        ext[tm:tm + HALO, :] = ext[0:HALO, :]

        dp, du0 = dxs[:, 0:256], dxs[:, 256:512]
        dproj_ref[:, C_AB:C_AB + 256] = dab_ref[...].astype(BF16)
        dproj_ref[:, C_AC:C_AC + 256] = (dp * a_v).astype(BF16)
        dproj_ref[:, C_AV:C_AV + 256] = (dp * a_c).astype(BF16)
        dproj_ref[:, C_BA:C_BA + 256] = (du0 * sg).astype(BF16)
        dproj_ref[:, C_BG:C_BG + 256] = (du0 * b_a * sg * (1.0 - sg)).astype(BF16)
        dproj_ref[:, C_Q:C_Q + 1536] = dxs[:, 512:2048].astype(BF16)
        dproj_ref[:, C_Z:C_Z + 512] = dz_ref[...].astype(BF16)
        dproj_ref[:, C_GB:C_GB + 128] = dblk_ref[...].astype(BF16)

    rev = lambda w: pl.BlockSpec((tm, w), lambda i: (n - 1 - i, 0))
    full = lambda a: pl.BlockSpec(a.shape, lambda i: (0, 0))
    return pl.pallas_call(
        body, name="conv_bwd", grid=(n,),
        in_specs=[rev(IN_PAD), rev(N_CONVCOL), rev(256), rev(512), rev(128), full(wa), full(wb), full(wc)],
        out_specs=[rev(IN_PAD), full(wa), full(wb), full(wc)],
        out_shape=[jax.ShapeDtypeStruct((T, IN_PAD), BF16), jax.ShapeDtypeStruct(wa.shape, F32),
                   jax.ShapeDtypeStruct(wb.shape, F32), jax.ShapeDtypeStruct(wc.shape, F32)],
        scratch_shapes=[pltpu.VMEM((tm + HALO, N_CONVCOL), F32), pltpu.VMEM((tm, N_CONVCOL), F32)],
        compiler_params=pltpu.CompilerParams(dimension_semantics=("arbitrary",), vmem_limit_bytes=VMEM_LIMIT),
    )(proj, dco, da_b, dz, dblk, wa, wb, wc)


def _ffn_out_fwd(x1, gu, g2, wfo, tm):
    def fn(x1, gu, g2, wfo):
        s = (_silu(gu[:, :D_FF]) * gu[:, D_FF:]).astype(BF16)
        f = jnp.dot(s, wfo, preferred_element_type=F32)
        return x1 + g2 * f, f

    return _rowwise("ffnout_fwd", fn, [x1, gu], [g2, wfo], [(D, F32), (D, F32)], [], tm)


def _ffn_out_bwd(dx2, gu, f, g2, wfo, tm):
    def fn(dx2, gu, f, g2, wfo):
        gate, up = gu[:, :D_FF], gu[:, D_FF:]
        sg = _sigmoid(gate)
        sl = gate * sg
        df = (dx2 * g2).astype(BF16)
        ds = lax.dot_general(df, wfo, (((1,), (1,)), ((), ())), preferred_element_type=F32)
        dgate = ds * up * (sg * (1.0 + gate * (1.0 - sg)))
        dgu = jnp.concatenate([dgate.astype(BF16), (ds * sl).astype(BF16)], axis=1)
        return dgu, sl * up, df, _colsum(dx2 * f)

    return _rowwise("ffnout_bwd", fn, [dx2, gu, f], [g2, wfo], [(2 * D_FF, BF16), (D_FF, BF16), (D, BF16)],
                    [((1, D), F32)], tm)


def _loss_bwd(x, tgt, gfin, tm):
    def fn(x, tgt, gfin):
        xn, r = _rms(x)
        e = xn * gfin - tgt
        loss = 0.5 * jnp.sum(jnp.mean(e * e, axis=-1, keepdims=True), axis=0, keepdims=True)
        dy = e * (1.0 / D)
        return _rms_bwd(dy * gfin, xn, r), jnp.broadcast_to(loss, (1, 128)), _colsum(dy * xn)

    return _rowwise("loss_bwd", fn, [x, tgt], [gfin], [(D, F32)], [((1, 128), F32), ((1, D), F32)], tm)


def _wgrad(name, a, b, bm=256, bk=512):
    T, M = a.shape
    N = b.shape[1]
    bk = min(bk, T)

    def body(a_ref, b_ref, o_ref):
        @pl.when(pl.program_id(1) == 0)
        def _():
            o_ref[...] = jnp.zeros_like(o_ref)

        o_ref[...] += lax.dot_general(a_ref[...], b_ref[...], (((0,), (0,)), ((), ())), preferred_element_type=F32)

    return pl.pallas_call(
        body, name=name, grid=(M // bm, T // bk),
        in_specs=[pl.BlockSpec((bk, bm), lambda i, k: (k, i)), pl.BlockSpec((bk, N), lambda i, k: (k, 0))],
        out_specs=pl.BlockSpec((bm, N), lambda i, k: (i, 0)),
        out_shape=jax.ShapeDtypeStruct((M, N), F32),
        compiler_params=pltpu.CompilerParams(dimension_semantics=("arbitrary", "arbitrary"), vmem_limit_bytes=VMEM_LIMIT),
    )(a, b)


def _adamw(w, g, m, v):
    m = B1 * m + (1.0 - B1) * g
    v = B2 * v + (1.0 - B2) * (g * g)
    m_hat = m / (1.0 - B1 ** STEP)
    v_hat = v / (1.0 - B2 ** STEP)
    return -LR * (m_hat / (jnp.sqrt(v_hat) + AEPS) + WD * w), m, v


def _adam_call(name, w, g, m, v, tm):
    C = w.shape[1]
    return _rowwise(name, _adamw, [w, g, m, v], [], [(C, F32)] * 3, [], tm)


def _pair_add(name, mine, theirs_recv, tm):
    n, R, C = mine.shape

    def fn(a, b):
        p = a + b
        return p, p

    return _rowwise(name, fn, [mine.reshape(n * R, C), theirs_recv.reshape(n * R, C)], [], [(C, BF16), (C, F32)], [], tm)


def _reduce_adam(name, own, recv, w, m, v, tm):
    n, R, C = recv.shape
    flat = recv.reshape(n * R, C)

    def fn(own, r0, r1, r2, w, m, v):
        g = ((own + r0.astype(F32)) + r1.astype(F32)) + r2.astype(F32)
        return (g,) + _adamw(w, g, m, v)

    return _rowwise(name, fn, [own] + [_cols(flat, C, 0, j * R) for j in range(n)] + [w, m, v], [], [(C, F32)] * 4, [],
                    tm, rows=R)


_OFFSETS = [(dx, dy, dc) for dx in (0, 1) for dy in (0, 1) for dc in (0, 1)][1:]
_MESH = pl.DeviceIdType.MESH


def _coords():
    return lax.axis_index("x"), lax.axis_index("y"), lax.axis_index("c")


def _flip(me, off):
    return tuple((1 - m) if d else m for m, d in zip(me, off))


def _linear(p):
    return 4 * p[0] + 2 * p[1] + p[2]


_CHIP_FLIPS = ((1, 0), (0, 1), (1, 1))


def _gather_weights(arrs):
    n = len(arrs)

    def body(*refs):
        ins, outs = refs[:n], refs[n:2 * n]
        send, recv, loc = refs[2 * n:]
        x, y, c = _coords()
        me, sib = (x, y, c), (x, y, 1 - c)
        chips = [((1 - x) if dx else x, (1 - y) if dy else y) for dx, dy in _CHIP_FLIPS]

        def copy(a, k, block, to, src=None):
            slot = outs[a].at[_linear(block)]
            return pltpu.make_async_remote_copy(src_ref=slot if src is None else src, dst_ref=slot, send_sem=send.at[a * 7 + k],
                                                recv_sem=recv.at[a * 7 + k], device_id=to, device_id_type=_MESH)

        local = [pltpu.make_async_copy(ins[a], outs[a].at[_linear(me)], loc.at[a]) for a in range(n)]
        for cp in local:
            cp.start()
        first = []
        for a in range(n):
            first.append(copy(a, 0, me, sib, src=ins[a]))
            first += [copy(a, 1 + j, me, (*chip, c), src=ins[a]) for j, chip in enumerate(chips)]
        for cp in first:
            cp.start()
        passed = []
        for j, chip in enumerate(chips):
            for a in range(n):
                copy(a, 1 + j, (*chip, c), me).wait_recv()
                cp = copy(a, 4 + j, (*chip, c), sib)
                cp.start()
                passed.append(cp)
        for a in range(n):
            copy(a, 0, sib, me).wait_recv()
            for j, chip in enumerate(chips):
                copy(a, 4 + j, (*chip, 1 - c), me).wait_recv()
        for cp in first + passed:
            cp.wait_send()
        for cp in local:
            cp.wait()

    anyspec = pl.BlockSpec(memory_space=pl.ANY)
    return pl.pallas_call(
        body, name="gather_weights", in_specs=[anyspec] * n, out_specs=[anyspec] * n,
        out_shape=[jax.ShapeDtypeStruct((N_DEV,) + a.shape, a.dtype) for a in arrs],
        scratch_shapes=[pltpu.SemaphoreType.DMA((n * 7,)), pltpu.SemaphoreType.DMA((n * 7,)), pltpu.SemaphoreType.DMA((n,))],
    )(*arrs)


def _pair_exchange(arrs):
    n = len(arrs)

    def body(*refs):
        ins, outs = refs[:n], refs[n:2 * n]
        send, recv = refs[2 * n:]
        x, y, c = _coords()
        copies = [pltpu.make_async_remote_copy(src_ref=ins[a], dst_ref=outs[a], send_sem=send.at[a], recv_sem=recv.at[a],
                                               device_id=(x, y, 1 - c), device_id_type=_MESH) for a in range(n)]
        for cp in copies:
            cp.start()
        for cp in copies:
            cp.wait()

    anyspec = pl.BlockSpec(memory_space=pl.ANY)
    return pl.pallas_call(
        body, name="pair_exchange", in_specs=[anyspec] * n, out_specs=[anyspec] * n,
        out_shape=[jax.ShapeDtypeStruct(a.shape, a.dtype) for a in arrs],
        scratch_shapes=[pltpu.SemaphoreType.DMA((n,)), pltpu.SemaphoreType.DMA((n,))],
    )(*arrs)


def _chip_exchange(arrs):
    n = len(arrs)
    nf = len(_CHIP_FLIPS)

    def body(*refs):
        ins, outs = refs[:n], refs[n:2 * n]
        send, recv = refs[2 * n:]
        x, y, c = _coords()
        copies = []
        for a in range(n):
            for k, (dx, dy) in enumerate(_CHIP_FLIPS):
                px, py = (1 - x) if dx else x, (1 - y) if dy else y
                copies.append(pltpu.make_async_remote_copy(
                    src_ref=ins[a].at[2 * px + py], dst_ref=outs[a].at[k], send_sem=send.at[a * nf + k],
                    recv_sem=recv.at[a * nf + k], device_id=(px, py, c), device_id_type=_MESH))
        for cp in copies:
            cp.start()
        for cp in copies:
            cp.wait()

    anyspec = pl.BlockSpec(memory_space=pl.ANY)
    return pl.pallas_call(
        body, name="chip_exchange", in_specs=[anyspec] * n, out_specs=[anyspec] * n,
        out_shape=[jax.ShapeDtypeStruct((nf,) + a.shape[1:], a.dtype) for a in arrs],
        scratch_shapes=[pltpu.SemaphoreType.DMA((n * nf,)), pltpu.SemaphoreType.DMA((n * nf,))],
    )(*arrs)


def _small_allgather(name, packed):
    R = packed.shape[0]

    def body(in_ref, all_ref, sum_ref, send, recv):
        me = _coords()
        my = _linear(me)
        all_ref[my] = in_ref[...]
        copies = []
        for k, off in enumerate(_OFFSETS):
            cp = pltpu.make_async_remote_copy(src_ref=in_ref, dst_ref=all_ref.at[my], send_sem=send.at[k], recv_sem=recv.at[k],
                                              device_id=_flip(me, off), device_id_type=_MESH)
            cp.start()
            copies.append(cp)
        for cp in copies:
            cp.wait()
        acc = all_ref[0]
        for j in range(1, N_DEV):
            acc = acc + all_ref[j]
        sum_ref[...] = acc

    vm = pl.BlockSpec(memory_space=pltpu.VMEM)
    return pl.pallas_call(
        body, name=name, in_specs=[vm], out_specs=[vm, vm],
        out_shape=[jax.ShapeDtypeStruct((N_DEV, R, 128), F32), jax.ShapeDtypeStruct((R, 128), F32)],
        scratch_shapes=[pltpu.SemaphoreType.DMA((len(_OFFSETS),)), pltpu.SemaphoreType.DMA((len(_OFFSETS),))],
        compiler_params=pltpu.CompilerParams(vmem_limit_bytes=VMEM_LIMIT),
    )(packed)


def _ada_mod(c, w16, bias):
    nc = w16.shape[2]
    kp = len(_OFFSETS)

    def body(c_ref, w_ref, b_ref, rows_ref, act_ref, cbuf, sbuf, send, recv):
        me = _coords()
        my = _linear(me)
        cbuf[my] = c_ref[...]
        copies = []
        for k, off in enumerate(_OFFSETS):
            cp = pltpu.make_async_remote_copy(src_ref=c_ref, dst_ref=cbuf.at[my], send_sem=send.at[k], recv_sem=recv.at[k],
                                              device_id=_flip(me, off), device_id_type=_MESH)
            cp.start()
            copies.append(cp)
        for cp in copies:
            cp.wait()
        act = _silu(jnp.concatenate([cbuf[j] for j in range(N_DEV)], axis=0))
        act_ref[...] = act
        act16 = act.astype(BF16)
        for l in range(DEPTH):
            ml = jnp.dot(act16, w_ref[l], preferred_element_type=F32) + b_ref[l:l + 1, :]
            for j in range(N_DEV):
                sbuf[j, l:l + 1, :] = ml[j:j + 1, :]
        rows_ref[my] = sbuf[my]
        copies = []
        for k, off in enumerate(_OFFSETS):
            peer = _flip(me, off)
            cp = pltpu.make_async_remote_copy(src_ref=sbuf.at[_linear(peer)], dst_ref=rows_ref.at[my], send_sem=send.at[kp + k],
                                              recv_sem=recv.at[kp + k], device_id=peer, device_id_type=_MESH)
            cp.start()
            copies.append(cp)
        for cp in copies:
            cp.wait()

    vm = pl.BlockSpec(memory_space=pltpu.VMEM)
    return pl.pallas_call(
        body, name="ada_mod", in_specs=[vm, vm, vm], out_specs=[vm, vm],
        out_shape=[jax.ShapeDtypeStruct((N_DEV, DEPTH, nc), F32), jax.ShapeDtypeStruct((N_DEV, D), F32)],
        scratch_shapes=[pltpu.VMEM((N_DEV, 1, D), F32), pltpu.VMEM((N_DEV, DEPTH, nc), F32),
                        pltpu.SemaphoreType.DMA((2 * kp,)), pltpu.SemaphoreType.DMA((2 * kp,))],
        compiler_params=pltpu.CompilerParams(vmem_limit_bytes=VMEM_LIMIT),
    )(c, w16, bias)


def _pack(arrs):
    parts = []
    for a in arrs:
        f = a.reshape(-1).astype(F32)
        parts.append(jnp.pad(f, (0, (-f.shape[0]) % 128)))
    flat = jnp.concatenate(parts)
    flat = jnp.pad(flat, (0, (-flat.shape[0]) % 1024))
    return flat.reshape(-1, 128)


def _unpack(packed, shapes):
    flat = packed.reshape(packed.shape[:-2] + (-1,))
    out, r = [], 0
    for s in shapes:
        n = int(np.prod(s))
        out.append(flat[..., r:r + n].reshape(packed.shape[:-2] + tuple(s)))
        r += -(-n // 128) * 128
    return out


def _pad_rows(w, rows):
    return jnp.pad(w, ((0, 0), (0, rows - w.shape[1]), (0, 0)))


_SMALL = ("b_ada", "norm_mix_g", "norm_ffn_g", "conv_a_w", "conf_dw_w", "conf_dw_b", "conf_ln_g", "conf_ln_b",
          "dn_conv_w", "dn_a_log", "dn_dt_bias", "dn_norm_g", "final_norm_g")
_BIG = ("w_in", "w_out", "w_ffn_in", "w_ffn_out")
_WEIGHTS = ("w_ada", "b_ada", "norm_mix_g", "norm_ffn_g", "w_in", "conv_a_w", "conf_dw_w", "conf_dw_b", "conf_ln_g",
            "conf_ln_b", "dn_conv_w", "dn_a_log", "dn_dt_bias", "dn_norm_g", "w_out", "w_ffn_in", "w_ffn_out",
            "final_norm_g")


def _step(x, c, loss_target, W, M, V):
    T = x.shape[1]
    me = _linear(_coords())
    cc = lax.axis_index("c")
    chip = 2 * lax.axis_index("x") + lax.axis_index("y")
    xs, tgt = x[0], loss_target[0]
    vec = lambda a: a.reshape(1, -1)

    nada = W["w_ada"].shape[2]
    rows, act_all = _ada_mod(c, W["w_ada"].astype(BF16), lax.dynamic_slice(W["b_ada"], (0, me * nada), (DEPTH, nada)))
    mod = rows.transpose(1, 0, 2).reshape(DEPTH, 6, 1, D)

    g_in, g_out, g_fi, g_fo = _gather_weights([W[k].astype(BF16) for k in _BIG])
    win = jnp.pad(g_in.transpose(1, 2, 0, 3).reshape(DEPTH, D, IN_COLS), ((0, 0), (0, 0), (0, IN_PAD - IN_COLS)))
    wout = g_out.transpose(1, 0, 2, 3).reshape(DEPTH, D, D)
    wfi = g_fi.transpose(1, 2, 0, 3).reshape(DEPTH, D, 2 * D_FF)
    wfo = g_fo.transpose(1, 0, 2, 3).reshape(DEPTH, D_FF, D)
    conv_names = ("conv_a_w", "conf_dw_w", "dn_conv_w")
    conv_all, _ = _small_allgather("gather_conv_w", _pack([W[k] for k in conv_names]))
    conv_full = [t.transpose(1, 2, 0, 3).reshape(t.shape[1], t.shape[2], -1)
                 for t in _unpack(conv_all, [W[k].shape for k in conv_names])]
    wa, wb, wc = _pad_rows(conv_full[0], 8), _pad_rows(conv_full[1], 32), _pad_rows(conv_full[2], 8)
    lane_pad = lambda a: jnp.pad(a, ((0, 0), (0, 128 - a.shape[1])))
    alog, dtb = lane_pad(W["dn_a_log"]), lane_pad(W["dn_dt_bias"])

    saved = []
    xc = xs
    for l in range(DEPTH):
        sh1, sc1, g1, sh2, sc2, g2 = [mod[l, i] for i in range(6)]
        proj, h = _normproj_fwd("inproj_fwd", xc, sh1, sc1, vec(W["norm_mix_g"][l]), win[l], 512)
        convout = _conv_fwd(proj, wa[l], wb[l], wc[l], 256)
        y_ab, qkv, gb = _stage2_fwd(proj, convout, vec(W["conf_dw_b"][l]), vec(W["conf_ln_g"][l]), vec(W["conf_ln_b"][l]),
                                    alog[l:l + 1], dtb[l:l + 1], 256)
        o, ss = _delta_fwd(qkv, gb, DELTA_NB)
        x1, mix, ycat = _outproj_fwd(xc, o, proj, y_ab, g1, vec(W["dn_norm_g"][l]), wout[l], 512)
        gu, h2 = _normproj_fwd("ffnin_fwd", x1, sh2, sc2, vec(W["norm_ffn_g"][l]), wfi[l], 256)
        x2, f = _ffn_out_fwd(x1, gu, g2, wfo[l], 256)
        saved.append((xc, proj, h, convout, qkv, gb, o, ss, mix, ycat, x1, gu, h2, f))
        xc = x2

    dx, loss_row, d_gfin = _loss_bwd(xc, tgt, vec(W["final_norm_g"]), 512)
    loss = lax.psum(loss_row[0, 0], ("x", "y", "c"))

    big_out = {k: [None] * DEPTH for k in _BIG}
    dmod, small = [None] * DEPTH, [None] * DEPTH
    for l in reversed(range(DEPTH)):
        xc, proj, h, convout, qkv, gb, o, ss, mix, ycat, x1, gu, h2, f = saved[l]
        sh1, sc1, g1, sh2, sc2, g2 = [mod[l, i] for i in range(6)]
        gm, gf = vec(W["norm_mix_g"][l]), vec(W["norm_ffn_g"][l])
        bb, lg, lb = vec(W["conf_dw_b"][l]), vec(W["conf_ln_g"][l]), vec(W["conf_ln_b"][l])
        dng = vec(W["dn_norm_g"][l])

        dgu, s, df, d_g2 = _ffn_out_bwd(dx, gu, f, g2, wfo[l], 256)
        gw_fo = _wgrad("wgrad_ffn_out", s, df)
        dx1, d_sh2, d_sc2, d_gf = _normproj_bwd("ffnin_bwd", x1, dgu, dx, sc2, gf, wfi[l], 256)
        gw_fi = _wgrad("wgrad_ffn_in", h2, dgu)
        dmix, dy_ab, do, dz, d_g1, d_dng = _outproj_bwd(dx1, mix, o, proj, g1, dng, wout[l], 512)
        gw_out = _wgrad("wgrad_out", ycat, dmix)
        dqkv = _delta_bwd(qkv, gb, ss, do, DELTA_NB)
        dco, da_b, dblk, d_bb, d_lg, d_lb, d_alog, d_dtb = _stage2_bwd(
            proj, convout, dy_ab, _cols(dqkv, D_DN, 0), _cols(dqkv, D_DN, 1), _cols(dqkv, D_DN, 2),
            _cols(dqkv, 128, 3 * D_DN // 128), bb, lg, lb, alog[l:l + 1], dtb[l:l + 1], 256)
        dproj, d_wa, d_wb, d_wc = _conv_bwd(proj, dco, da_b, dz, dblk, wa[l], wb[l], wc[l], 256)
        dx, d_sh1, d_sc1, d_gm = _normproj_bwd("inproj_bwd", xc, dproj, dx1, sc1, gm, win[l], 512)
        gw_in = _wgrad("wgrad_in", h, dproj)

        dmod[l] = jnp.concatenate([d_sh1, d_sc1, d_g1, d_sh2, d_sc2, d_g2], axis=1)
        small[l] = dict(norm_mix_g=d_gm, norm_ffn_g=d_gf, conv_a_w=d_wa[:KA], conf_dw_w=d_wb[:KB], conf_dw_b=d_bb,
                        conf_ln_g=d_lg, conf_ln_b=d_lb, dn_conv_w=d_wc[:KC], dn_a_log=d_alog, dn_dt_bias=d_dtb,
                        dn_norm_g=d_dng)

        by_cols = lambda g: g.reshape(D, 4, 2, -1).transpose(2, 1, 0, 3)
        by_rows = lambda g: g.reshape(4, 2, -1, D).transpose(1, 0, 2, 3)
        parts = [by_cols(gw_in[:, :IN_COLS]), by_rows(gw_out), by_cols(gw_fi), by_rows(gw_fo)]
        mine = [lax.dynamic_index_in_dim(p, cc, 0, keepdims=False) for p in parts]
        theirs = [lax.dynamic_index_in_dim(p, 1 - cc, 0, keepdims=False) for p in parts]
        from_sib = _pair_exchange(theirs)
        sums = [_pair_add("pair_add_" + k, a, b, 512 if a.shape[1] % 128 == 0 else a.shape[1])
                for k, a, b in zip(_BIG, mine, from_sib)]
        from_chips = _chip_exchange([p16.reshape(a.shape) for (p16, _), a in zip(sums, mine)])
        for k, (_, p32), a, r in zip(_BIG, sums, mine, from_chips):
            own = lax.dynamic_index_in_dim(p32.reshape(a.shape), chip, 0, keepdims=False)
            big_out[k][l] = _reduce_adam("reduce_adam_" + k, own, r, W[k][l], M[k][l], V[k][l],
                                         256 if r.shape[1] % 256 == 0 else r.shape[1])

    names = ("norm_mix_g", "norm_ffn_g", "conv_a_w", "conf_dw_w", "conf_dw_b", "conf_ln_g", "conf_ln_b", "dn_conv_w",
             "dn_a_log", "dn_dt_bias", "dn_norm_g")
    pieces = [jnp.stack(dmod)] + [jnp.stack([small[l][k] for l in range(DEPTH)]) for k in names] + [d_gfin]
    shapes = [p.shape for p in pieces]
    every, total = _small_allgather("gather_small_grads", _pack(pieces))
    tot = dict(zip(("dmod",) + names + ("final_norm_g",), _unpack(total, shapes)))
    dmod_all = _unpack(every, shapes[:1])[0]

    grads = {}
    grads["b_ada"] = tot["dmod"].reshape(DEPTH, 6 * D)
    for k in ("norm_mix_g", "norm_ffn_g", "conf_dw_b", "conf_ln_g", "conf_ln_b", "dn_norm_g"):
        grads[k] = tot[k].reshape(W[k].shape)
    grads["dn_a_log"] = tot["dn_a_log"].reshape(DEPTH, 128)[:, :HEADS]
    grads["dn_dt_bias"] = tot["dn_dt_bias"].reshape(DEPTH, 128)[:, :HEADS]
    grads["final_norm_g"] = tot["final_norm_g"].reshape(D)
    for k in conv_names:
        nloc = W[k].shape[2]
        grads[k] = lax.dynamic_slice_in_dim(tot[k], me * nloc, nloc, axis=2)

    dm = lax.dynamic_slice_in_dim(dmod_all.reshape(N_DEV, DEPTH, 6 * D), me * nada, nada, axis=2)
    pad16 = lambda a: jnp.pad(a, ((0, 16 - N_DEV), (0, 0))).astype(BF16)
    g_ada = _wgrad("wgrad_ada", pad16(act_all), pad16(dm.reshape(N_DEV, DEPTH * nada)))
    grads["w_ada"] = g_ada.reshape(D, DEPTH, nada).transpose(1, 0, 2)

    delta, new_m, new_v = {}, {}, {}
    r2 = lambda a: a.reshape(DEPTH * D, nada)
    d_, m_, v_ = _adam_call("adam_ada", r2(W["w_ada"]), r2(grads["w_ada"]), r2(M["w_ada"]), r2(V["w_ada"]), 512)
    delta["w_ada"], new_m["w_ada"], new_v["w_ada"] = [t.reshape(W["w_ada"].shape) for t in (d_, m_, v_)]
    sshapes = [W[k].shape for k in _SMALL]
    d_, m_, v_ = _adam_call("adam_small", _pack([W[k] for k in _SMALL]), _pack([grads[k] for k in _SMALL]),
                            _pack([M[k] for k in _SMALL]), _pack([V[k] for k in _SMALL]), 4096)
    for dst, packed in ((delta, d_), (new_m, m_), (new_v, v_)):
        dst.update(zip(_SMALL, _unpack(packed, sshapes)))
    for k in _BIG:
        g_, d_, m_, v_ = [jnp.stack([big_out[k][l][i] for l in range(DEPTH)]) for i in range(4)]
        grads[k], delta[k], new_m[k], new_v[k] = g_, d_, m_, v_

    return (loss, dx[None], *[grads[k] for k in _WEIGHTS], *[delta[k] for k in _WEIGHTS],
            *[new_m[k] for k in _WEIGHTS], *[new_v[k] for k in _WEIGHTS])


def kernel(x, c, w_ada, b_ada, norm_mix_g, norm_ffn_g, w_in, conv_a_w, conf_dw_w, conf_dw_b, conf_ln_g, conf_ln_b, dn_conv_w, dn_a_log, dn_dt_bias, dn_norm_g, w_out, w_ffn_in, w_ffn_out, final_norm_g, loss_target, m_w_ada, m_b_ada, m_norm_mix_g, m_norm_ffn_g, m_w_in, m_conv_a_w, m_conf_dw_w, m_conf_dw_b, m_conf_ln_g, m_conf_ln_b, m_dn_conv_w, m_dn_a_log, m_dn_dt_bias, m_dn_norm_g, m_w_out, m_w_ffn_in, m_w_ffn_out, m_final_norm_g, v_w_ada, v_b_ada, v_norm_mix_g, v_norm_ffn_g, v_w_in, v_conv_a_w, v_conf_dw_w, v_conf_dw_b, v_conf_ln_g, v_conf_ln_b, v_dn_conv_w, v_dn_a_log, v_dn_dt_bias, v_dn_norm_g, v_w_out, v_w_ffn_in, v_w_ffn_out, v_final_norm_g):
    a = dict(locals())
    W = {k: a[k] for k in _WEIGHTS}
    M = {k: a["m_" + k] for k in _WEIGHTS}
    V = {k: a["v_" + k] for k in _WEIGHTS}
    return _step(x, c, loss_target, W, M, V)
```

```python
import functools

import jax
import jax.numpy as jnp
import numpy as np
from jax import lax
from jax.experimental import pallas as pl
from jax.experimental.pallas import tpu as pltpu

F32 = jnp.float32
BF16 = jnp.bfloat16

N_DEV = 8
D = 1024
DEPTH = 4
D_CONV = 256
D_CONF = 256
D_DN = 512
HEADS = 4
HD = 128
KA, KB, KC = 3, 31, 4
CHUNK = 64
D_FF = 2816
IN_COLS = 3336
IN_PAD = 3456
N_CONVCOL = 2048
EPS = 1e-6
LN_EPS = 1e-5
HALO = 32
VMEM_LIMIT = 56 * 1024 * 1024
DELTA_NB = 4

C_AB, C_AC, C_AV, C_BA, C_BG, C_Q, C_Z, C_GB = 0, 256, 512, 768, 1024, 1280, 2816, 3328

LR, B1, B2, AEPS, WD, STEP = 0.001, 0.9, 0.999, 1e-08, 0.01, 10


def _dot(a, b, dims, hi):
    if hi:
        return lax.dot_general(a.astype(F32), b.astype(F32), (dims, ((), ())), precision=lax.Precision.HIGHEST,
                               preferred_element_type=F32)
    return lax.dot_general(a.astype(BF16), b.astype(BF16), (dims, ((), ())), preferred_element_type=F32)


@functools.partial(jax.custom_vjp, nondiff_argnums=(2,))
def mm_nn(a, b, hi=False):
    return _dot(a, b, ((1,), (0,)), hi)


@functools.partial(jax.custom_vjp, nondiff_argnums=(2,))
def mm_nt(a, b, hi=False):
    return _dot(a, b, ((1,), (1,)), hi)


@functools.partial(jax.custom_vjp, nondiff_argnums=(2,))
def mm_tn(a, b, hi=False):
    return _dot(a, b, ((0,), (0,)), hi)


mm_nn.defvjp(lambda a, b, hi: (mm_nn(a, b, hi), (a, b)),
             lambda hi, r, g: (mm_nt(g, r[1], hi), mm_tn(r[0], g, hi)))
mm_nt.defvjp(lambda a, b, hi: (mm_nt(a, b, hi), (a, b)),
             lambda hi, r, g: (mm_nn(g, r[1], hi), mm_tn(g, r[0], hi)))
mm_tn.defvjp(lambda a, b, hi: (mm_tn(a, b, hi), (a, b)),
             lambda hi, r, g: (mm_nt(r[1], g, hi), mm_nn(r[0], g, hi)))


def _sigmoid(x):
    return 1.0 / (1.0 + jnp.exp(-x))


def _silu(x):
    return x * _sigmoid(x)


def _softplus(x):
    return jnp.maximum(x, 0.0) + jnp.log(1.0 + jnp.exp(-jnp.abs(x)))


def _iota2(shape, dim):
    return lax.broadcasted_iota(jnp.int32, shape, dim)


def _dot16(a, b):
    return jnp.dot(a.astype(BF16), b.astype(BF16), preferred_element_type=F32)


def _dot_3pass(a, b):
    ah = a.astype(BF16)
    bh = b.astype(BF16)
    al = (a - ah.astype(F32)).astype(BF16)
    bl = (b - bh.astype(F32)).astype(BF16)
    d = lambda x, y: jnp.dot(x, y, preferred_element_type=F32)
    return d(ah, bh) + (d(ah, bl) + d(al, bh))


@jax.custom_vjp
def _unit_lower_inverses(Xs):
    n = Xs[0].shape[0]
    r, c = _iota2((n, n), 0), _iota2((n, n), 1)
    eye = (r == c).astype(F32)

    def joins(b):
        s = b.bit_length() - 1
        return ((r >> (s + 1)) == (c >> (s + 1))) & (((r >> s) & 1) == 1) & (((c >> s) & 1) == 0)

    Ts = [eye + jnp.where(joins(1), x, 0.0) for x in Xs]
    b = 2
    while b < n:
        m = joins(b)
        Ys = [_dot16(jnp.where(m, x, 0.0), t) for x, t in zip(Xs, Ts)]
        Ts = [t + _dot16(t, y) for t, y in zip(Ts, Ys)]
        b *= 2
    Rs = [(eye - t) + _dot_3pass(x, t) for x, t in zip(Xs, Ts)]
    return [t + _dot16(t, r_) for t, r_ in zip(Ts, Rs)]


def _unit_lower_inverses_fwd(Xs):
    Ts = _unit_lower_inverses(Xs)
    return Ts, Ts


def _unit_lower_inverses_bwd(Ts, gs):
    inner = [mm_nt(g, t) for g, t in zip(gs, Ts)]
    return ([mm_tn(t, i) for t, i in zip(Ts, inner)],)


_unit_lower_inverses.defvjp(_unit_lower_inverses_fwd, _unit_lower_inverses_bwd)

def _delta_chunks(qs, ks, vs, gbs, Ss):
    C = CHUNK
    nb = len(gbs)
    pairs = [(c, h) for c in range(nb) for h in range(HEADS)]
    each = lambda fn, *lists: [fn(*a) for a in zip(*lists)]
    row = _iota2((C, C), 0)
    col = _iota2((C, C), 1)
    causal = row >= col
    strict = row > col
    tri = causal.astype(F32)
    eye = (row == col).astype(F32)
    lane = _iota2((C, 128), 1)
    subl = _iota2((128, C), 0)
    last = (_iota2((C, 1), 0) == C - 1).astype(F32)

    gc_all = [mm_nn(tri, gb, True) for gb in gbs]
    gc_t = [g.T for g in gc_all]
    q = [qs[c][h] * (HD ** -0.5) for c, h in pairs]
    k = [ks[c][h] for c, h in pairs]
    v = [vs[c][h] for c, h in pairs]
    gcol = [jnp.sum(jnp.where(lane == h, gc_all[c], 0.0), axis=1, keepdims=True) for c, h in pairs]
    grow = [jnp.sum(jnp.where(subl == h, gc_t[c], 0.0), axis=0, keepdims=True) for c, h in pairs]
    beta = [jnp.sum(jnp.where(lane == HEADS + h, gbs[c], 0.0), axis=1, keepdims=True) for c, h in pairs]
    decay = each(lambda a, b: jnp.where(causal, jnp.exp(jnp.where(causal, a - b, 0.0)), 0.0), gcol, grow)
    kb = each(lambda a, b: a * b, k, beta)
    vb = each(lambda a, b: a * b, v, beta)
    kk = each(lambda a, b: mm_nt(a, b), kb, k)
    X = each(lambda a, d: -jnp.where(strict, a * d, 0.0), kk, decay)
    T = _unit_lower_inverses(X)
    eg = [jnp.exp(g) for g in gcol]
    u = each(lambda t, a: mm_nn(t, a), T, vb)
    w = each(lambda t, a, e: mm_nn(t, a * e), T, kb, eg)
    qk = each(lambda a, b, d: jnp.where(causal, mm_nt(a, b) * d, 0.0), q, k, decay)
    qg = each(lambda a, e: a * e, q, eg)
    g_last = [jnp.sum(g * last, axis=0, keepdims=True) for g in gcol]
    kd = each(lambda a, gl, g: a * jnp.exp(gl - g), k, g_last, gcol)
    eg_last = [jnp.exp(g) for g in g_last]

    outs = []
    for c in range(nb):
        sl = slice(c * HEADS, (c + 1) * HEADS)
        v_new = each(lambda a, b, S: a - mm_nn(b, S), u[sl], w[sl], Ss)
        oS = each(lambda a, S: mm_nn(a, S), qg[sl], Ss)
        outs.append(each(lambda a, b, n: a + mm_nn(b, n), oS, qk[sl], v_new))
        Ss = each(lambda S, e, a, n: S * e + mm_tn(a, n), Ss, eg_last[sl], kd[sl], v_new)
    return outs, Ss


def _split_chunks(ref, nb):
    return [[ref[c * CHUNK:(c + 1) * CHUNK, h * HD:(h + 1) * HD] for h in range(HEADS)] for c in range(nb)]


def _join_chunks(vals):
    return jnp.concatenate([jnp.concatenate(heads, axis=1) for heads in vals], axis=0)


def _delta_fwd(qkv, gb, nb):
    T = qkv.shape[0]
    nb = min(nb, T // CHUNK)
    rows = nb * CHUNK
    n = T // rows

    def body(q_ref, k_ref, v_ref, gb_ref, o_ref, ss_ref, s_scr):
        @pl.when(pl.program_id(0) == 0)
        def _():
            s_scr[...] = jnp.zeros_like(s_scr)

        Ss = [s_scr[h] for h in range(HEADS)]
        for h in range(HEADS):
            ss_ref[0, h] = Ss[h]
        gbs = [gb_ref[c * CHUNK:(c + 1) * CHUNK, :] for c in range(nb)]
        outs, new_S = _delta_chunks(_split_chunks(q_ref, nb), _split_chunks(k_ref, nb), _split_chunks(v_ref, nb), gbs, Ss)
        o_ref[...] = _join_chunks(outs)
        s_scr[...] = jnp.stack(new_S)

    row = lambda w, j=0: pl.BlockSpec((rows, w), lambda i: (i, j))
    return pl.pallas_call(
        body, name="delta_fwd", grid=(n,),
        in_specs=[row(D_DN, 0), row(D_DN, 1), row(D_DN, 2), row(128)],
        out_specs=[row(D_DN), pl.BlockSpec((1, HEADS, HD, HD), lambda i: (i, 0, 0, 0))],
        out_shape=[jax.ShapeDtypeStruct((T, D_DN), F32), jax.ShapeDtypeStruct((n, HEADS, HD, HD), F32)],
        scratch_shapes=[pltpu.VMEM((HEADS, HD, HD), F32)],
        compiler_params=pltpu.CompilerParams(dimension_semantics=("arbitrary",), vmem_limit_bytes=VMEM_LIMIT),
    )(qkv, qkv, qkv, gb)


def _delta_bwd(qkv, gb, ss, do, nb):
    T = qkv.shape[0]
    nb = min(nb, T // CHUNK)
    rows = nb * CHUNK
    n = T // rows

    def body(q_ref, k_ref, v_ref, gb_ref, ss_ref, do_ref, d_ref, ds_scr):
        @pl.when(pl.program_id(0) == 0)
        def _():
            ds_scr[...] = jnp.zeros_like(ds_scr)

        Ss = [ss_ref[0, h] for h in range(HEADS)]
        gbs = [gb_ref[c * CHUNK:(c + 1) * CHUNK, :] for c in range(nb)]
        _, vjp = jax.vjp(_delta_chunks, _split_chunks(q_ref, nb), _split_chunks(k_ref, nb), _split_chunks(v_ref, nb), gbs, Ss)
        dqs, dks, dvs, dgbs, dSs = vjp((_split_chunks(do_ref, nb), [ds_scr[h] for h in range(HEADS)]))
        d_ref[...] = jnp.concatenate([_join_chunks(dqs), _join_chunks(dks), _join_chunks(dvs),
                                      jnp.concatenate(dgbs, axis=0)], axis=1)
        ds_scr[...] = jnp.stack(dSs)

    row = lambda w, j=0: pl.BlockSpec((rows, w), lambda i: (n - 1 - i, j))
    return pl.pallas_call(
        body, name="delta_bwd", grid=(n,),
        in_specs=[row(D_DN, 0), row(D_DN, 1), row(D_DN, 2), row(128),
                  pl.BlockSpec((1, HEADS, HD, HD), lambda i: (n - 1 - i, 0, 0, 0)), row(D_DN)],
        out_specs=row(3 * D_DN + 128),
        out_shape=jax.ShapeDtypeStruct((T, 3 * D_DN + 128), F32),
        scratch_shapes=[pltpu.VMEM((HEADS, HD, HD), F32)],
        compiler_params=pltpu.CompilerParams(dimension_semantics=("arbitrary",), vmem_limit_bytes=VMEM_LIMIT),
    )(qkv, qkv, qkv, gb, ss, do)


def _cols(arr, width, index, first_row=0):
    return (arr, width, index, first_row)


def _rowwise(name, fn, tiled, consts, out_tiled, out_acc, tm, rows=None, rider=None):
    tiled = [t if isinstance(t, tuple) else (t, t.shape[1], 0, 0) for t in tiled]
    T = tiled[0][0].shape[0] if rows is None else rows
    tm = min(tm, T)
    assert T % tm == 0 and all(r % tm == 0 for (_, _, _, r) in tiled)
    n_t, n_c, n_o, n_a = len(tiled), len(consts), len(out_tiled), len(out_acc)

    n_ri = len(rider.ins) if rider else 0
    n_ro = len(rider.outs) if rider else 0
    n_steps = T // tm

    def body(*refs):
        r_ins = refs[n_t + n_c:n_t + n_c + n_ri]
        o_refs = refs[n_t + n_c + n_ri:n_t + n_c + n_ri + n_o]
        a_refs = refs[n_t + n_c + n_ri + n_o:n_t + n_c + n_ri + n_o + n_a]
        r_outs = refs[n_t + n_c + n_ri + n_o + n_a:n_t + n_c + n_ri + n_o + n_a + n_ro]
        sems = refs[n_t + n_c + n_ri + n_o + n_a + n_ro:]
        if rider:
            @pl.when(pl.program_id(0) == 0)
            def _():
                rider.start(r_ins, r_outs, *sems)

        ins = [r[...] for r in refs[:n_t + n_c]]
        outs = fn(*ins)
        for r, val in zip(o_refs, outs[:n_o]):
            r[...] = val.astype(r.dtype)
        if n_a:
            @pl.when(pl.program_id(0) == 0)
            def _():
                for r in a_refs:
                    r[...] = jnp.zeros_like(r)
            for r, val in zip(a_refs, outs[n_o:]):
                r[...] += val
        if rider:
            @pl.when(pl.program_id(0) == n_steps - 1)
            def _():
                rider.finish(r_ins, r_outs, *sems)

    def const_spec(a):
        nd = a.ndim
        return pl.BlockSpec(a.shape, lambda i: (0,) * nd, pipeline_mode=pl.Buffered(1))

    in_specs = [pl.BlockSpec((tm, w), functools.partial(lambda i, j, r: (i + r, j), j=j, r=r0 // tm))
                for (_, w, j, r0) in tiled]
    in_specs += [const_spec(a) for a in consts]
    out_specs = [pl.BlockSpec((tm, w), lambda i: (i, 0)) for (w, _) in out_tiled]
    out_specs += [pl.BlockSpec(s, lambda i: (0, 0)) for (s, _) in out_acc]
    out_shape = [jax.ShapeDtypeStruct((T, w), dt) for (w, dt) in out_tiled]
    out_shape += [jax.ShapeDtypeStruct(s, dt) for (s, dt) in out_acc]
    operands = [t[0] for t in tiled] + list(consts)
    scratch = []
    if rider:
        anyspec = pl.BlockSpec(memory_space=pl.ANY)
        in_specs += [anyspec] * n_ri
        out_specs += [anyspec] * n_ro
        out_shape += list(rider.outs)
        operands += list(rider.ins)
        scratch = rider.scratch()
    return pl.pallas_call(
        body, name=name, grid=(n_steps,), in_specs=in_specs, out_specs=out_specs, out_shape=out_shape, scratch_shapes=scratch,
        compiler_params=pltpu.CompilerParams(dimension_semantics=("arbitrary",), vmem_limit_bytes=VMEM_LIMIT),
    )(*operands)


def _colsum(x):
    return jnp.sum(x, axis=0, keepdims=True)


def _rms(x):
    r = lax.rsqrt(jnp.mean(x * x, axis=-1, keepdims=True) + EPS)
    return x * r, r


def _rms_bwd(dxn, xn, r):
    return r * (dxn - xn * jnp.mean(dxn * xn, axis=-1, keepdims=True))


def _normproj_fwd(name, x, sh, sc, g, w, tm, rider=None):
    def fn(x, sh, sc, g, w):
        xn, _ = _rms(x)
        h = (xn * (g * (1.0 + sc)) + sh).astype(BF16)
        return jnp.dot(h, w, preferred_element_type=F32), h

    return _rowwise(name, fn, [x], [sh, sc, g, w], [(w.shape[1], F32), (D, BF16)], [], tm, rider=rider)


def _normproj_bwd(name, x, dpre, dres, sc, g, w, tm, rider=None):
    def fn(x, dpre, dres, sc, g, w):
        xn, r = _rms(x)
        dh = lax.dot_general(dpre, w, (((1,), (1,)), ((), ())), preferred_element_type=F32)
        da = _colsum(dh * xn)
        dx = _rms_bwd(dh * (g * (1.0 + sc)), xn, r) + dres
        return dx, _colsum(dh), da * g, da * (1.0 + sc)

    vec = ((1, D), F32)
    return _rowwise(name, fn, [x, dpre, dres], [sc, g, w], [(D, F32)], [vec, vec, vec], tm, rider=rider)


def _stage2(a_b, blk, cp, u1c, qp, kp, vp, bb, lg, lb, alog, dtb):
    y_a = a_b * cp
    u1 = u1c + bb
    mu = jnp.mean(u1, axis=-1, keepdims=True)
    uc = u1 - mu
    var = jnp.mean(uc * uc, axis=-1, keepdims=True)
    y_b = _silu(uc * lax.rsqrt(var + LN_EPS) * lg + lb)

    def l2(t):
        t = _silu(t)
        return t * lax.rsqrt(jnp.sum(t * t, axis=-1, keepdims=True) + EPS)

    q = [l2(t) for t in qp]
    k = [l2(t) for t in kp]
    v = _silu(vp)
    lane = _iota2(blk.shape, 1)
    gdec = -jnp.exp(alog) * _softplus(blk + dtb)
    gb = jnp.where(lane < HEADS, gdec, jnp.where(lane < 2 * HEADS, _sigmoid(blk), 0.0))
    return y_a, y_b, q, k, v, gb


def _heads_of(x, base=0):
    return [x[:, base + h * HD:base + (h + 1) * HD] for h in range(HEADS)]


def _stage2_fwd(proj, convout, bb, lg, lb, alog, dtb, tm):
    def fn(a_b, blk, co, bb, lg, lb, alog, dtb):
        y_a, y_b, q, k, v, gb = _stage2(a_b, blk, co[:, 0:256], co[:, 256:512], _heads_of(co, 512), _heads_of(co, 1024),
                                        co[:, 1536:2048], bb, lg, lb, alog, dtb)
        return jnp.concatenate([y_a, y_b], axis=1), jnp.concatenate(q + k + [v], axis=1), gb

    return _rowwise("stage2_fwd", fn, [_cols(proj, 256, 0), _cols(proj, 128, C_GB // 128), convout],
                    [bb, lg, lb, alog, dtb], [(512, BF16), (1536, F32), (128, F32)], [], tm)


def _stage2_bwd(proj, convout, dy_ab, dq, dk, dv, dgb, bb, lg, lb, alog, dtb, tm, rider=None):
    def fn(a_b, blk, co, dy_ab, dq, dk, dv, dgb, bb, lg, lb, alog, dtb):
        args = (a_b, blk, co[:, 0:256], co[:, 256:512], _heads_of(co, 512), _heads_of(co, 1024), co[:, 1536:2048],
                bb, lg, lb, alog, dtb)
        _, vjp = jax.vjp(_stage2, *args)
        ct = (dy_ab[:, 0:256], dy_ab[:, 256:512], _heads_of(dq), _heads_of(dk), dv, dgb)
        da_b, dblk, dcp, du1c, dqp, dkp, dvp, dbb, dlg, dlb, dalog, ddtb = vjp(ct)
        dco = jnp.concatenate([dcp, du1c] + dqp + dkp + [dvp], axis=1)
        return dco, da_b, dblk, dbb, dlg, dlb, dalog, ddtb

    v256, v128 = ((1, 256), F32), ((1, 128), F32)
    return _rowwise("stage2_bwd", fn,
                    [_cols(proj, 256, 0), _cols(proj, 128, C_GB // 128), convout, dy_ab, dq, dk, dv, dgb],
                    [bb, lg, lb, alog, dtb], [(N_CONVCOL, F32), (256, F32), (128, F32)],
                    [v256, v256, v256, v128, v128], tm, rider=rider)


def _stage3(o, z, dng):
    ys = []
    for oh, zh in zip(o, z):
        on = oh * lax.rsqrt(jnp.mean(oh * oh, axis=-1, keepdims=True) + EPS)
        ys.append(on * dng * _silu(zh))
    return ys


def _outproj_fwd(x, o, proj, y_ab, g1, dng, wout, tm):
    def fn(x, o, z0, z1, z2, z3, y_ab, g1, dng, wout):
        y_c = _stage3(_heads_of(o), [z0, z1, z2, z3], dng)
        ycat = jnp.concatenate([y_ab] + [t.astype(BF16) for t in y_c], axis=1)
        mix = jnp.dot(ycat, wout, preferred_element_type=F32)
        return x + g1 * mix, mix, ycat

    return _rowwise("outproj_fwd", fn, [x, o] + _z_heads(proj) + [y_ab],
                    [g1, dng, wout], [(D, F32), (D, F32), (D, BF16)], [], tm)


def _z_heads(proj):
    return [_cols(proj, HD, C_Z // HD + h) for h in range(HEADS)]


def _outproj_bwd(dx1, mix, o, proj, g1, dng, wout, tm):
    def fn(dx1, mix, o, z0, z1, z2, z3, g1, dng, wout):
        dmix = (dx1 * g1).astype(BF16)
        dycat = lax.dot_general(dmix, wout, (((1,), (1,)), ((), ())), preferred_element_type=F32)
        _, vjp = jax.vjp(_stage3, _heads_of(o), [z0, z1, z2, z3], dng)
        do, dz, ddng = vjp(_heads_of(dycat, 512))
        return (dmix, dycat[:, 0:512], jnp.concatenate(do, axis=1), jnp.concatenate(dz, axis=1),
                _colsum(dx1 * mix), ddng)

    return _rowwise("outproj_bwd", fn, [dx1, mix, o] + _z_heads(proj), [g1, dng, wout],
                    [(D, BF16), (512, F32), (512, F32), (512, F32)], [((1, D), F32), ((1, HD), F32)], tm)


_CONV_BLOCKS = ((0, 256, KA), (256, 512, KB), (512, 2048, KC))
_CONV_STRIP = 256


def _conv_inputs(proj_ref):
    a_c, a_v = proj_ref[:, C_AC:C_AC + 256], proj_ref[:, C_AV:C_AV + 256]
    b_a, b_g = proj_ref[:, C_BA:C_BA + 256], proj_ref[:, C_BG:C_BG + 256]
    return a_c, a_v, b_a, _sigmoid(b_g)


def _conv_fwd(proj, wa, wb, wc, tm):
    T = proj.shape[0]
    tm = min(tm, T)

    def body(proj_ref, wa_ref, wb_ref, wc_ref, out_ref, ext):
        @pl.when(pl.program_id(0) == 0)
        def _():
            ext[0:HALO, :] = jnp.zeros((HALO, N_CONVCOL), F32)

        a_c, a_v, b_a, sg = _conv_inputs(proj_ref)
        ext[HALO:HALO + tm, 0:256] = a_c * a_v
        ext[HALO:HALO + tm, 256:512] = b_a * sg
        ext[HALO:HALO + tm, 512:2048] = proj_ref[:, C_Q:C_Q + 1536]
        for (c0, c1, kw), w_ref in zip(_CONV_BLOCKS, (wa_ref, wb_ref, wc_ref)):
            for s0 in range(c0, c1, _CONV_STRIP):
                acc = jnp.zeros((tm, _CONV_STRIP), F32)
                for k in range(kw):
                    acc += w_ref[k:k + 1, s0 - c0:s0 - c0 + _CONV_STRIP] * ext[pl.ds(HALO - (kw - 1) + k, tm), s0:s0 + _CONV_STRIP]
                out_ref[:, s0:s0 + _CONV_STRIP] = acc
        ext[0:HALO, :] = ext[tm:tm + HALO, :]

    full = lambda a: pl.BlockSpec(a.shape, lambda i: (0, 0))
    return pl.pallas_call(
        body, name="conv_fwd", grid=(T // tm,),
        in_specs=[pl.BlockSpec((tm, IN_PAD), lambda i: (i, 0)), full(wa), full(wb), full(wc)],
        out_specs=pl.BlockSpec((tm, N_CONVCOL), lambda i: (i, 0)),
        out_shape=jax.ShapeDtypeStruct((T, N_CONVCOL), F32),
        scratch_shapes=[pltpu.VMEM((HALO + tm, N_CONVCOL), F32)],
        compiler_params=pltpu.CompilerParams(dimension_semantics=("arbitrary",), vmem_limit_bytes=VMEM_LIMIT),
    )(proj, wa, wb, wc)


def _conv_bwd(proj, dco, da_b, dz, dblk, wa, wb, wc, tm):
    T = proj.shape[0]
    tm = min(tm, T)
    n = T // tm

    def body(proj_ref, dco_ref, dab_ref, dz_ref, dblk_ref, wa_ref, wb_ref, wc_ref,
             dproj_ref, dwa_ref, dwb_ref, dwc_ref, ext, dxs):
        @pl.when(pl.program_id(0) == 0)
        def _():
            ext[tm:tm + HALO, :] = jnp.zeros((HALO, N_CONVCOL), F32)
            dwa_ref[...] = jnp.zeros_like(dwa_ref)
            dwb_ref[...] = jnp.zeros_like(dwb_ref)
            dwc_ref[...] = jnp.zeros_like(dwc_ref)

        ext[0:tm, :] = dco_ref[...]
        a_c, a_v, b_a, sg = _conv_inputs(proj_ref)

        def conv_in(s0):
            if s0 < 256:
                return a_c * a_v
            if s0 < 512:
                return b_a * sg
            return proj_ref[:, C_Q + s0 - 512:C_Q + s0 - 512 + _CONV_STRIP]

        for (c0, c1, kw), w_ref, dw_ref in zip(_CONV_BLOCKS, (wa_ref, wb_ref, wc_ref), (dwa_ref, dwb_ref, dwc_ref)):
            for s0 in range(c0, c1, _CONV_STRIP):
                xin = conv_in(s0)
                acc = jnp.zeros((tm, _CONV_STRIP), F32)
                for k in range(kw):
                    sh = ext[pl.ds(kw - 1 - k, tm), s0:s0 + _CONV_STRIP]
                    acc += w_ref[k:k + 1, s0 - c0:s0 - c0 + _CONV_STRIP] * sh
                    dw_ref[k:k + 1, s0 - c0:s0 - c0 + _CONV_STRIP] += _colsum(sh * xin)
                dxs[:, s0:s0 + _CONV_STRIP] = acc
        ext[tm:tm + HALO, :] = ext[0:HALO, :]

        dp, du0 = dxs[:, 0:256], dxs[:, 256:512]
        dproj_ref[:, C_AB:C_AB + 256] = dab_ref[...].astype(BF16)
        dproj_ref[:, C_AC:C_AC + 256] = (dp * a_v).astype(BF16)
        dproj_ref[:, C_AV:C_AV + 256] = (dp * a_c).astype(BF16)
        dproj_ref[:, C_BA:C_BA + 256] = (du0 * sg).astype(BF16)
        dproj_ref[:, C_BG:C_BG + 256] = (du0 * b_a * sg * (1.0 - sg)).astype(BF16)
        dproj_ref[:, C_Q:C_Q + 1536] = dxs[:, 512:2048].astype(BF16)
        dproj_ref[:, C_Z:C_Z + 512] = dz_ref[...].astype(BF16)
        dproj_ref[:, C_GB:C_GB + 128] = dblk_ref[...].astype(BF16)

    rev = lambda w: pl.BlockSpec((tm, w), lambda i: (n - 1 - i, 0))
    full = lambda a: pl.BlockSpec(a.shape, lambda i: (0, 0))
    return pl.pallas_call(
        body, name="conv_bwd", grid=(n,),
        in_specs=[rev(IN_PAD), rev(N_CONVCOL), rev(256), rev(512), rev(128), full(wa), full(wb), full(wc)],
        out_specs=[rev(IN_PAD), full(wa), full(wb), full(wc)],
        out_shape=[jax.ShapeDtypeStruct((T, IN_PAD), BF16), jax.ShapeDtypeStruct(wa.shape, F32),
                   jax.ShapeDtypeStruct(wb.shape, F32), jax.ShapeDtypeStruct(wc.shape, F32)],
        scratch_shapes=[pltpu.VMEM((tm + HALO, N_CONVCOL), F32), pltpu.VMEM((tm, N_CONVCOL), F32)],
        compiler_params=pltpu.CompilerParams(dimension_semantics=("arbitrary",), vmem_limit_bytes=VMEM_LIMIT),
    )(proj, dco, da_b, dz, dblk, wa, wb, wc)


def _ffn_out_fwd(x1, gu, g2, wfo, tm, rider=None):
    def fn(x1, gu, g2, wfo):
        s = (_silu(gu[:, :D_FF]) * gu[:, D_FF:]).astype(BF16)
        f = jnp.dot(s, wfo, preferred_element_type=F32)
        return x1 + g2 * f, f

    return _rowwise("ffnout_fwd", fn, [x1, gu], [g2, wfo], [(D, F32), (D, F32)], [], tm, rider=rider)


def _ffn_out_bwd(dx2, gu, f, g2, wfo, tm, rider=None):
    def fn(dx2, gu, f, g2, wfo):
        gate, up = gu[:, :D_FF], gu[:, D_FF:]
        sg = _sigmoid(gate)
        sl = gate * sg
        df = (dx2 * g2).astype(BF16)
        ds = lax.dot_general(df, wfo, (((1,), (1,)), ((), ())), preferred_element_type=F32)
        dgate = ds * up * (sg * (1.0 + gate * (1.0 - sg)))
        dgu = jnp.concatenate([dgate.astype(BF16), (ds * sl).astype(BF16)], axis=1)
        return dgu, sl * up, df, _colsum(dx2 * f)

    return _rowwise("ffnout_bwd", fn, [dx2, gu, f], [g2, wfo], [(2 * D_FF, BF16), (D_FF, BF16), (D, BF16)],
                    [((1, D), F32)], tm, rider=rider)


def _loss_bwd(x, tgt, gfin, tm):
    def fn(x, tgt, gfin):
        xn, r = _rms(x)
        e = xn * gfin - tgt
        loss = 0.5 * jnp.sum(jnp.mean(e * e, axis=-1, keepdims=True), axis=0, keepdims=True)
        dy = e * (1.0 / D)
        return _rms_bwd(dy * gfin, xn, r), jnp.broadcast_to(loss, (1, 128)), _colsum(dy * xn)

    return _rowwise("loss_bwd", fn, [x, tgt], [gfin], [(D, F32)], [((1, 128), F32), ((1, D), F32)], tm)


def _wgrad(name, a, b, bm, bk):
    T, M = a.shape
    N = b.shape[1]
    bk = min(bk, T)

    def body(a_ref, b_ref, o_ref):
        @pl.when(pl.program_id(1) == 0)
        def _():
            o_ref[...] = jnp.zeros_like(o_ref)

        o_ref[...] += lax.dot_general(a_ref[...], b_ref[...], (((0,), (0,)), ((), ())), preferred_element_type=F32)

    return pl.pallas_call(
        body, name=name, grid=(M // bm, T // bk),
        in_specs=[pl.BlockSpec((bk, bm), lambda i, k: (k, i)), pl.BlockSpec((bk, N), lambda i, k: (k, 0))],
        out_specs=pl.BlockSpec((bm, N), lambda i, k: (i, 0)),
        out_shape=jax.ShapeDtypeStruct((M, N), F32),
        compiler_params=pltpu.CompilerParams(dimension_semantics=("arbitrary", "arbitrary"), vmem_limit_bytes=VMEM_LIMIT),
    )(a, b)


def _adamw(w, g, m, v):
    m = B1 * m + (1.0 - B1) * g
    v = B2 * v + (1.0 - B2) * (g * g)
    m_hat = m / (1.0 - B1 ** STEP)
    v_hat = v / (1.0 - B2 ** STEP)
    return -LR * (m_hat / (jnp.sqrt(v_hat) + AEPS) + WD * w), m, v


def _adam_call(name, w, g, m, v, tm):
    C = w.shape[1]
    return _rowwise(name, _adamw, [w, g, m, v], [], [(C, F32)] * 3, [], tm)


def _pair_add(name, mine, theirs_recv, tm):
    n, R, C = mine.shape

    def fn(a, b):
        p = a + b
        return p, p

    return _rowwise(name, fn, [mine.reshape(n * R, C), theirs_recv.reshape(n * R, C)], [], [(C, BF16), (C, F32)], [], tm)


def _reduce_adam(name, own, recv, w, m, v, tm):
    n, R, C = recv.shape
    flat = recv.reshape(n * R, C)

    def fn(own, r0, r1, r2, w, m, v):
        g = ((own + r0.astype(F32)) + r1.astype(F32)) + r2.astype(F32)
        return (g,) + _adamw(w, g, m, v)

    return _rowwise(name, fn, [own] + [_cols(flat, C, 0, j * R) for j in range(n)] + [w, m, v], [], [(C, F32)] * 4, [],
                    tm, rows=R)


_OFFSETS = [(dx, dy, dc) for dx in (0, 1) for dy in (0, 1) for dc in (0, 1)][1:]
_MESH = pl.DeviceIdType.MESH


def _coords():
    return lax.axis_index("x"), lax.axis_index("y"), lax.axis_index("c")


def _flip(me, off):
    return tuple((1 - m) if d else m for m, d in zip(me, off))


def _linear(p):
    return 4 * p[0] + 2 * p[1] + p[2]


_CHIP_FLIPS = ((1, 0), (0, 1), (1, 1))


class _Rider:
    def __init__(self, ins, outs, n_remote, n_local, start, finish):
        self.ins, self.outs, self.n_remote, self.n_local, self.start, self.finish = ins, outs, n_remote, n_local, start, finish

    def scratch(self):
        return [pltpu.SemaphoreType.DMA((self.n_remote,)), pltpu.SemaphoreType.DMA((self.n_remote,)),
                pltpu.SemaphoreType.DMA((max(self.n_local, 1),))]


def _run_rider(name, rider):
    def body(*refs):
        n_i, n_o = len(rider.ins), len(rider.outs)
        rider.start(refs[:n_i], refs[n_i:n_i + n_o], *refs[n_i + n_o:])
        rider.finish(refs[:n_i], refs[n_i:n_i + n_o], *refs[n_i + n_o:])

    anyspec = pl.BlockSpec(memory_space=pl.ANY)
    return pl.pallas_call(
        body, name=name, in_specs=[anyspec] * len(rider.ins), out_specs=[anyspec] * len(rider.outs),
        out_shape=list(rider.outs), scratch_shapes=rider.scratch(),
    )(*rider.ins)


def _gather_rider(arrs, layer):
    n = len(arrs)

    def parts(ins, outs, send, recv, loc):
        x, y, c = _coords()
        me, sib = (x, y, c), (x, y, 1 - c)
        chips = [((1 - x) if dx else x, (1 - y) if dy else y) for dx, dy in _CHIP_FLIPS]

        def copy(a, k, block, to, own=False):
            slot = outs[a].at[_linear(block)]
            return pltpu.make_async_remote_copy(src_ref=ins[a].at[layer] if own else slot, dst_ref=slot,
                                                send_sem=send.at[a * 7 + k], recv_sem=recv.at[a * 7 + k],
                                                device_id=to, device_id_type=_MESH)

        local = [pltpu.make_async_copy(ins[a].at[layer], outs[a].at[_linear(me)], loc.at[a]) for a in range(n)]
        first = []
        for a in range(n):
            first.append(copy(a, 0, me, sib, own=True))
            first += [copy(a, 1 + j, me, (*chip, c), own=True) for j, chip in enumerate(chips)]
        return copy, local, first, me, sib, chips, c

    def start(*refs):
        _, local, first, *_ = parts(*refs)
        for cp in local + first:
            cp.start()

    def finish(*refs):
        copy, local, first, me, sib, chips, c = parts(*refs)
        passed = []
        for j, chip in enumerate(chips):
            for a in range(n):
                copy(a, 1 + j, (*chip, c), me).wait_recv()
                cp = copy(a, 4 + j, (*chip, c), sib)
                cp.start()
                passed.append(cp)
        for a in range(n):
            copy(a, 0, sib, me).wait_recv()
            for j, chip in enumerate(chips):
                copy(a, 4 + j, (*chip, 1 - c), me).wait_recv()
        for cp in first + passed:
            cp.wait_send()
        for cp in local:
            cp.wait()

    outs = [jax.ShapeDtypeStruct((N_DEV,) + a.shape[1:], a.dtype) for a in arrs]
    return _Rider(list(arrs), outs, 7 * n, n, start, finish)


def _simple_rider(ins, outs, n_remote, make):
    def start(*refs):
        for cp in make(*refs):
            cp.start()

    def finish(*refs):
        for cp in make(*refs):
            cp.wait()

    return _Rider(ins, outs, n_remote, 0, start, finish)


def _pair_rider(arrs):
    def make(ins, outs, send, recv, loc):
        x, y, c = _coords()
        return [pltpu.make_async_remote_copy(src_ref=ins[a], dst_ref=outs[a], send_sem=send.at[a], recv_sem=recv.at[a],
                                             device_id=(x, y, 1 - c), device_id_type=_MESH) for a in range(len(arrs))]

    return _simple_rider(list(arrs), [jax.ShapeDtypeStruct(a.shape, a.dtype) for a in arrs], len(arrs), make)


def _chip_rider(arrs):
    nf = len(_CHIP_FLIPS)

    def make(ins, outs, send, recv, loc):
        x, y, c = _coords()
        copies = []
        for a in range(len(arrs)):
            for k, (dx, dy) in enumerate(_CHIP_FLIPS):
                px, py = (1 - x) if dx else x, (1 - y) if dy else y
                copies.append(pltpu.make_async_remote_copy(
                    src_ref=ins[a].at[2 * px + py], dst_ref=outs[a].at[k], send_sem=send.at[a * nf + k],
                    recv_sem=recv.at[a * nf + k], device_id=(px, py, c), device_id_type=_MESH))
        return copies

    return _simple_rider(list(arrs), [jax.ShapeDtypeStruct((nf,) + a.shape[1:], a.dtype) for a in arrs], nf * len(arrs), make)


def _small_allgather(name, packed):
    R = packed.shape[0]

    def body(in_ref, all_ref, sum_ref, send, recv):
        me = _coords()
        my = _linear(me)
        all_ref[my] = in_ref[...]
        copies = []
        for k, off in enumerate(_OFFSETS):
            cp = pltpu.make_async_remote_copy(src_ref=in_ref, dst_ref=all_ref.at[my], send_sem=send.at[k], recv_sem=recv.at[k],
                                              device_id=_flip(me, off), device_id_type=_MESH)
            cp.start()
            copies.append(cp)
        for cp in copies:
            cp.wait()
        acc = all_ref[0]
        for j in range(1, N_DEV):
            acc = acc + all_ref[j]
        sum_ref[...] = acc

    vm = pl.BlockSpec(memory_space=pltpu.VMEM)
    return pl.pallas_call(
        body, name=name, in_specs=[vm], out_specs=[vm, vm],
        out_shape=[jax.ShapeDtypeStruct((N_DEV, R, 128), F32), jax.ShapeDtypeStruct((R, 128), F32)],
        scratch_shapes=[pltpu.SemaphoreType.DMA((len(_OFFSETS),)), pltpu.SemaphoreType.DMA((len(_OFFSETS),))],
        compiler_params=pltpu.CompilerParams(vmem_limit_bytes=VMEM_LIMIT),
    )(packed)


def _ada_mod(c, w16, bias):
    nc = w16.shape[2]
    kp = len(_OFFSETS)

    def body(c_ref, w_ref, b_ref, rows_ref, act_ref, cbuf, sbuf, send, recv):
        me = _coords()
        my = _linear(me)
        cbuf[my] = c_ref[...]
        copies = []
        for k, off in enumerate(_OFFSETS):
            cp = pltpu.make_async_remote_copy(src_ref=c_ref, dst_ref=cbuf.at[my], send_sem=send.at[k], recv_sem=recv.at[k],
                                              device_id=_flip(me, off), device_id_type=_MESH)
            cp.start()
            copies.append(cp)
        for cp in copies:
            cp.wait()
        act = _silu(jnp.concatenate([cbuf[j] for j in range(N_DEV)], axis=0))
        act_ref[...] = act
        act16 = act.astype(BF16)
        for l in range(DEPTH):
            ml = jnp.dot(act16, w_ref[l], preferred_element_type=F32) + b_ref[l:l + 1, :]
            for j in range(N_DEV):
                sbuf[j, l:l + 1, :] = ml[j:j + 1, :]
        rows_ref[my] = sbuf[my]
        copies = []
        for k, off in enumerate(_OFFSETS):
            peer = _flip(me, off)
            cp = pltpu.make_async_remote_copy(src_ref=sbuf.at[_linear(peer)], dst_ref=rows_ref.at[my], send_sem=send.at[kp + k],
                                              recv_sem=recv.at[kp + k], device_id=peer, device_id_type=_MESH)
            cp.start()
            copies.append(cp)
        for cp in copies:
            cp.wait()

    vm = pl.BlockSpec(memory_space=pltpu.VMEM)
    return pl.pallas_call(
        body, name="ada_mod", in_specs=[vm, vm, vm], out_specs=[vm, vm],
        out_shape=[jax.ShapeDtypeStruct((N_DEV, DEPTH, nc), F32), jax.ShapeDtypeStruct((N_DEV, D), F32)],
        scratch_shapes=[pltpu.VMEM((N_DEV, 1, D), F32), pltpu.VMEM((N_DEV, DEPTH, nc), F32),
                        pltpu.SemaphoreType.DMA((2 * kp,)), pltpu.SemaphoreType.DMA((2 * kp,))],
        compiler_params=pltpu.CompilerParams(vmem_limit_bytes=VMEM_LIMIT),
    )(c, w16, bias)


def _pack(arrs):
    parts = []
    for a in arrs:
        f = a.reshape(-1).astype(F32)
        parts.append(jnp.pad(f, (0, (-f.shape[0]) % 128)))
    flat = jnp.concatenate(parts)
    flat = jnp.pad(flat, (0, (-flat.shape[0]) % 1024))
    return flat.reshape(-1, 128)


def _unpack(packed, shapes):
    flat = packed.reshape(packed.shape[:-2] + (-1,))
    out, r = [], 0
    for s in shapes:
        n = int(np.prod(s))
        out.append(flat[..., r:r + n].reshape(packed.shape[:-2] + tuple(s)))
        r += -(-n // 128) * 128
    return out


def _pad_rows(w, rows):
    return jnp.pad(w, ((0, 0), (0, rows - w.shape[1]), (0, 0)))


_SMALL = ("b_ada", "norm_mix_g", "norm_ffn_g", "conv_a_w", "conf_dw_w", "conf_dw_b", "conf_ln_g", "conf_ln_b",
          "dn_conv_w", "dn_a_log", "dn_dt_bias", "dn_norm_g", "final_norm_g")
_BIG = ("w_in", "w_out", "w_ffn_in", "w_ffn_out")
_WEIGHTS = ("w_ada", "b_ada", "norm_mix_g", "norm_ffn_g", "w_in", "conv_a_w", "conf_dw_w", "conf_dw_b", "conf_ln_g",
            "conf_ln_b", "dn_conv_w", "dn_a_log", "dn_dt_bias", "dn_norm_g", "w_out", "w_ffn_in", "w_ffn_out",
            "final_norm_g")


def _step(x, c, loss_target, W, M, V):
    T = x.shape[1]
    me = _linear(_coords())
    cc = lax.axis_index("c")
    chip = 2 * lax.axis_index("x") + lax.axis_index("y")
    xs, tgt = x[0], loss_target[0]
    vec = lambda a: a.reshape(1, -1)

    nada = W["w_ada"].shape[2]
    rows, act_all = _ada_mod(c, W["w_ada"].astype(BF16), lax.dynamic_slice(W["b_ada"], (0, me * nada), (DEPTH, nada)))
    mod = rows.transpose(1, 0, 2).reshape(DEPTH, 6, 1, D)

    w16 = {k: W[k].astype(BF16) for k in _BIG}

    def whole(g_in=None, g_out=None, g_fi=None, g_fo=None):
        out = {}
        if g_in is not None:
            out["w_in"] = jnp.pad(g_in.transpose(1, 0, 2).reshape(D, IN_COLS), ((0, 0), (0, IN_PAD - IN_COLS)))
        if g_out is not None:
            out["w_out"] = g_out.reshape(D, D)
        if g_fi is not None:
            out["w_ffn_in"] = g_fi.transpose(1, 0, 2).reshape(D, 2 * D_FF)
        if g_fo is not None:
            out["w_ffn_out"] = g_fo.reshape(D_FF, D)
        return out

    wts = [dict() for _ in range(DEPTH)]
    wts[0] = whole(*_run_rider("gather_weights", _gather_rider([w16[k] for k in _BIG], 0)))
    conv_names = ("conv_a_w", "conf_dw_w", "dn_conv_w")
    conv_all, _ = _small_allgather("gather_conv_w", _pack([W[k] for k in conv_names]))
    conv_full = [t.transpose(1, 2, 0, 3).reshape(t.shape[1], t.shape[2], -1)
                 for t in _unpack(conv_all, [W[k].shape for k in conv_names])]
    wa, wb, wc = _pad_rows(conv_full[0], 8), _pad_rows(conv_full[1], 32), _pad_rows(conv_full[2], 8)
    lane_pad = lambda a: jnp.pad(a, ((0, 0), (0, 128 - a.shape[1])))
    alog, dtb = lane_pad(W["dn_a_log"]), lane_pad(W["dn_dt_bias"])

    saved = []
    xc = xs
    for l in range(DEPTH):
        more = l + 1 < DEPTH
        ride = lambda names: _gather_rider([w16[k] for k in names], l + 1) if more else None
        sh1, sc1, g1, sh2, sc2, g2 = [mod[l, i] for i in range(6)]
        proj, h, *g_in = _normproj_fwd("inproj_fwd", xc, sh1, sc1, vec(W["norm_mix_g"][l]), wts[l]["w_in"], 512,
                                       rider=ride(["w_in"]))
        convout = _conv_fwd(proj, wa[l], wb[l], wc[l], 256)
        y_ab, qkv, gb = _stage2_fwd(proj, convout, vec(W["conf_dw_b"][l]), vec(W["conf_ln_g"][l]), vec(W["conf_ln_b"][l]),
                                    alog[l:l + 1], dtb[l:l + 1], 256)
        o, ss = _delta_fwd(qkv, gb, DELTA_NB)
        x1, mix, ycat = _outproj_fwd(xc, o, proj, y_ab, g1, vec(W["dn_norm_g"][l]), wts[l]["w_out"], 512)
        gu, h2, *g_fi = _normproj_fwd("ffnin_fwd", x1, sh2, sc2, vec(W["norm_ffn_g"][l]), wts[l]["w_ffn_in"], 256,
                                      rider=ride(["w_ffn_in"]))
        x2, f, *g_fo_out = _ffn_out_fwd(x1, gu, g2, wts[l]["w_ffn_out"], 256, rider=ride(["w_ffn_out", "w_out"]))
        if more:
            wts[l + 1] = whole(g_in=g_in[0], g_fi=g_fi[0], g_fo=g_fo_out[0], g_out=g_fo_out[1])
        saved.append((xc, proj, h, convout, qkv, gb, o, ss, mix, ycat, x1, gu, h2, f))
        xc = x2

    dx, loss_row, d_gfin = _loss_bwd(xc, tgt, vec(W["final_norm_g"]), 512)
    loss = lax.psum(loss_row[0, 0], ("x", "y", "c"))

    big_out = {k: [None] * DEPTH for k in _BIG}
    dmod, small = [None] * DEPTH, [None] * DEPTH
    by_cols = lambda g: g.reshape(D, 4, 2, -1).transpose(2, 1, 0, 3)
    by_rows = lambda g: g.reshape(4, 2, -1, D).transpose(1, 0, 2, 3)
    pair_tm = lambda a: 512 if a.shape[1] % 128 == 0 else a.shape[1]

    def pair_sums(mine, from_sib):
        sums = [_pair_add("pair_add_" + k, a, b, pair_tm(a)) for k, a, b in zip(_BIG, mine, from_sib)]
        p16 = {k: p.reshape(a.shape) for k, (p, _), a in zip(_BIG, sums, mine)}
        own = {k: lax.dynamic_index_in_dim(q.reshape(a.shape), chip, 0, keepdims=False) for k, (_, q), a in zip(_BIG, sums, mine)}
        return p16, own

    def finish(layer, own, recv):
        for k in _BIG:
            r = recv[k]
            big_out[k][layer] = _reduce_adam("reduce_adam_" + k, own[k], r, W[k][layer], M[k][layer], V[k][layer],
                                             256 if r.shape[1] % 256 == 0 else r.shape[1])

    above = None
    for l in reversed(range(DEPTH)):
        xc, proj, h, convout, qkv, gb, o, ss, mix, ycat, x1, gu, h2, f = saved[l]
        sh1, sc1, g1, sh2, sc2, g2 = [mod[l, i] for i in range(6)]
        gm, gf = vec(W["norm_mix_g"][l]), vec(W["norm_ffn_g"][l])
        bb, lg, lb = vec(W["conf_dw_b"][l]), vec(W["conf_ln_g"][l]), vec(W["conf_ln_b"][l])
        dng = vec(W["dn_norm_g"][l])
        wl = wts[l]

        dgu, s, df, d_g2, *from_sib = _ffn_out_bwd(dx, gu, f, g2, wl["w_ffn_out"], 256,
                                                   rider=_pair_rider(above[2]) if above else None)
        if above:
            p16, own = pair_sums(above[1], from_sib)
            recv = {}
        chips = lambda names: _chip_rider([p16[k] for k in names]) if above else None
        gw_fo = _wgrad("wgrad_ffn_out", s, df, 1408, 1024)
        dx1, d_sh2, d_sc2, d_gf, *got = _normproj_bwd("ffnin_bwd", x1, dgu, dx, sc2, gf, wl["w_ffn_in"], 256,
                                                      rider=chips(["w_ffn_in"]))
        if above:
            recv["w_ffn_in"] = got[0]
        gw_fi = _wgrad("wgrad_ffn_in", h2, dgu, 512, 512)
        dmix, dy_ab, do, dz, d_g1, d_dng = _outproj_bwd(dx1, mix, o, proj, g1, dng, wl["w_out"], 512)
        gw_out = _wgrad("wgrad_out", ycat, dmix, 512, 2048)
        dqkv = _delta_bwd(qkv, gb, ss, do, DELTA_NB)
        dco, da_b, dblk, d_bb, d_lg, d_lb, d_alog, d_dtb, *got = _stage2_bwd(
            proj, convout, dy_ab, _cols(dqkv, D_DN, 0), _cols(dqkv, D_DN, 1), _cols(dqkv, D_DN, 2),
            _cols(dqkv, 128, 3 * D_DN // 128), bb, lg, lb, alog[l:l + 1], dtb[l:l + 1], 256,
            rider=chips(["w_ffn_out", "w_out"]))
        if above:
            recv["w_ffn_out"], recv["w_out"] = got
        dproj, d_wa, d_wb, d_wc = _conv_bwd(proj, dco, da_b, dz, dblk, wa[l], wb[l], wc[l], 256)
        dx, d_sh1, d_sc1, d_gm, *got = _normproj_bwd("inproj_bwd", xc, dproj, dx1, sc1, gm, wl["w_in"], 512,
                                                     rider=chips(["w_in"]))
        if above:
            recv["w_in"] = got[0]
            finish(above[0], own, recv)
        gw_in = _wgrad("wgrad_in", h, dproj, 512, 1024)

        dmod[l] = jnp.concatenate([d_sh1, d_sc1, d_g1, d_sh2, d_sc2, d_g2], axis=1)
        small[l] = dict(norm_mix_g=d_gm, norm_ffn_g=d_gf, conv_a_w=d_wa[:KA], conf_dw_w=d_wb[:KB], conf_dw_b=d_bb,
                        conf_ln_g=d_lg, conf_ln_b=d_lb, dn_conv_w=d_wc[:KC], dn_a_log=d_alog, dn_dt_bias=d_dtb,
                        dn_norm_g=d_dng)
        parts = [by_cols(gw_in[:, :IN_COLS]), by_rows(gw_out), by_cols(gw_fi), by_rows(gw_fo)]
        above = (l, [lax.dynamic_index_in_dim(p, cc, 0, keepdims=False) for p in parts],
                 [lax.dynamic_index_in_dim(p, 1 - cc, 0, keepdims=False) for p in parts])

    p16, own = pair_sums(above[1], _run_rider("pair_exchange", _pair_rider(above[2])))
    finish(above[0], own, dict(zip(_BIG, _run_rider("chip_exchange", _chip_rider([p16[k] for k in _BIG])))))

    names = ("norm_mix_g", "norm_ffn_g", "conv_a_w", "conf_dw_w", "conf_dw_b", "conf_ln_g", "conf_ln_b", "dn_conv_w",
             "dn_a_log", "dn_dt_bias", "dn_norm_g")
    pieces = [jnp.stack(dmod)] + [jnp.stack([small[l][k] for l in range(DEPTH)]) for k in names] + [d_gfin]
    shapes = [p.shape for p in pieces]
    every, total = _small_allgather("gather_small_grads", _pack(pieces))
    tot = dict(zip(("dmod",) + names + ("final_norm_g",), _unpack(total, shapes)))
    dmod_all = _unpack(every, shapes[:1])[0]

    grads = {}
    grads["b_ada"] = tot["dmod"].reshape(DEPTH, 6 * D)
    for k in ("norm_mix_g", "norm_ffn_g", "conf_dw_b", "conf_ln_g", "conf_ln_b", "dn_norm_g"):
        grads[k] = tot[k].reshape(W[k].shape)
    grads["dn_a_log"] = tot["dn_a_log"].reshape(DEPTH, 128)[:, :HEADS]
    grads["dn_dt_bias"] = tot["dn_dt_bias"].reshape(DEPTH, 128)[:, :HEADS]
    grads["final_norm_g"] = tot["final_norm_g"].reshape(D)
    for k in conv_names:
        nloc = W[k].shape[2]
        grads[k] = lax.dynamic_slice_in_dim(tot[k], me * nloc, nloc, axis=2)

    dm = lax.dynamic_slice_in_dim(dmod_all.reshape(N_DEV, DEPTH, 6 * D), me * nada, nada, axis=2)
    pad16 = lambda a: jnp.pad(a, ((0, 16 - N_DEV), (0, 0))).astype(BF16)
    g_ada = _wgrad("wgrad_ada", pad16(act_all), pad16(dm.reshape(N_DEV, DEPTH * nada)), 256, 16)
    grads["w_ada"] = g_ada.reshape(D, DEPTH, nada).transpose(1, 0, 2)

    delta, new_m, new_v = {}, {}, {}
    r2 = lambda a: a.reshape(DEPTH * D, nada)
    d_, m_, v_ = _adam_call("adam_ada", r2(W["w_ada"]), r2(grads["w_ada"]), r2(M["w_ada"]), r2(V["w_ada"]), 512)
    delta["w_ada"], new_m["w_ada"], new_v["w_ada"] = [t.reshape(W["w_ada"].shape) for t in (d_, m_, v_)]
    sshapes = [W[k].shape for k in _SMALL]
    d_, m_, v_ = _adam_call("adam_small", _pack([W[k] for k in _SMALL]), _pack([grads[k] for k in _SMALL]),
                            _pack([M[k] for k in _SMALL]), _pack([V[k] for k in _SMALL]), 4096)
    for dst, packed in ((delta, d_), (new_m, m_), (new_v, v_)):
        dst.update(zip(_SMALL, _unpack(packed, sshapes)))
    for k in _BIG:
        g_, d_, m_, v_ = [jnp.stack([big_out[k][l][i] for l in range(DEPTH)]) for i in range(4)]
        grads[k], delta[k], new_m[k], new_v[k] = g_, d_, m_, v_

    return (loss, dx[None], *[grads[k] for k in _WEIGHTS], *[delta[k] for k in _WEIGHTS],
            *[new_m[k] for k in _WEIGHTS], *[new_v[k] for k in _WEIGHTS])


def kernel(x, c, w_ada, b_ada, norm_mix_g, norm_ffn_g, w_in, conv_a_w, conf_dw_w, conf_dw_b, conf_ln_g, conf_ln_b, dn_conv_w, dn_a_log, dn_dt_bias, dn_norm_g, w_out, w_ffn_in, w_ffn_out, final_norm_g, loss_target, m_w_ada, m_b_ada, m_norm_mix_g, m_norm_ffn_g, m_w_in, m_conv_a_w, m_conf_dw_w, m_conf_dw_b, m_conf_ln_g, m_conf_ln_b, m_dn_conv_w, m_dn_a_log, m_dn_dt_bias, m_dn_norm_g, m_w_out, m_w_ffn_in, m_w_ffn_out, m_final_norm_g, v_w_ada, v_b_ada, v_norm_mix_g, v_norm_ffn_g, v_w_in, v_conv_a_w, v_conf_dw_w, v_conf_dw_b, v_conf_ln_g, v_conf_ln_b, v_dn_conv_w, v_dn_a_log, v_dn_dt_bias, v_dn_norm_g, v_w_out, v_w_ffn_in, v_w_ffn_out, v_final_norm_g):
    a = dict(locals())
    W = {k: a[k] for k in _WEIGHTS}
    M = {k: a["m_" + k] for k in _WEIGHTS}
    V = {k: a["v_" + k] for k in _WEIGHTS}
    return _step(x, c, loss_target, W, M, V)
```

```python
import functools

import jax
import jax.numpy as jnp
import numpy as np
from jax import lax
from jax.experimental import pallas as pl
from jax.experimental.pallas import tpu as pltpu

F32 = jnp.float32
BF16 = jnp.bfloat16

N_DEV = 8
D = 1024
DEPTH = 4
D_CONV = 256
D_CONF = 256
D_DN = 512
HEADS = 4
HD = 128
KA, KB, KC = 3, 31, 4
CHUNK = 64
D_FF = 2816
IN_COLS = 3336
IN_PAD = 3456
N_CONVCOL = 2048
EPS = 1e-6
LN_EPS = 1e-5
HALO = 32
VMEM_LIMIT = 56 * 1024 * 1024
DELTA_NB = 4

C_AB, C_AC, C_AV, C_BA, C_BG, C_Q, C_Z, C_GB = 0, 256, 512, 768, 1024, 1280, 2816, 3328

LR, B1, B2, AEPS, WD, STEP = 0.001, 0.9, 0.999, 1e-08, 0.01, 10


def _dot(a, b, dims, hi):
    if hi:
        return lax.dot_general(a.astype(F32), b.astype(F32), (dims, ((), ())), precision=lax.Precision.HIGHEST,
                               preferred_element_type=F32)
    return lax.dot_general(a.astype(BF16), b.astype(BF16), (dims, ((), ())), preferred_element_type=F32)


@functools.partial(jax.custom_vjp, nondiff_argnums=(2,))
def mm_nn(a, b, hi=False):
    return _dot(a, b, ((1,), (0,)), hi)


@functools.partial(jax.custom_vjp, nondiff_argnums=(2,))
def mm_nt(a, b, hi=False):
    return _dot(a, b, ((1,), (1,)), hi)


@functools.partial(jax.custom_vjp, nondiff_argnums=(2,))
def mm_tn(a, b, hi=False):
    return _dot(a, b, ((0,), (0,)), hi)


mm_nn.defvjp(lambda a, b, hi: (mm_nn(a, b, hi), (a, b)),
             lambda hi, r, g: (mm_nt(g, r[1], hi), mm_tn(r[0], g, hi)))
mm_nt.defvjp(lambda a, b, hi: (mm_nt(a, b, hi), (a, b)),
             lambda hi, r, g: (mm_nn(g, r[1], hi), mm_tn(g, r[0], hi)))
mm_tn.defvjp(lambda a, b, hi: (mm_tn(a, b, hi), (a, b)),
             lambda hi, r, g: (mm_nt(r[1], g, hi), mm_nn(r[0], g, hi)))


def _sigmoid(x):
    return 1.0 / (1.0 + jnp.exp(-x))


def _silu(x):
    return x * _sigmoid(x)


def _softplus(x):
    return jnp.maximum(x, 0.0) + jnp.log(1.0 + jnp.exp(-jnp.abs(x)))


def _iota2(shape, dim):
    return lax.broadcasted_iota(jnp.int32, shape, dim)


def _dot16(a, b):
    return jnp.dot(a.astype(BF16), b.astype(BF16), preferred_element_type=F32)


def _dot_3pass(a, b):
    ah = a.astype(BF16)
    bh = b.astype(BF16)
    al = (a - ah.astype(F32)).astype(BF16)
    bl = (b - bh.astype(F32)).astype(BF16)
    d = lambda x, y: jnp.dot(x, y, preferred_element_type=F32)
    return d(ah, bh) + (d(ah, bl) + d(al, bh))


@jax.custom_vjp
def _unit_lower_inverses(Xs):
    n = Xs[0].shape[0]
    r, c = _iota2((n, n), 0), _iota2((n, n), 1)
    eye = (r == c).astype(F32)

    def joins(b):
        s = b.bit_length() - 1
        return ((r >> (s + 1)) == (c >> (s + 1))) & (((r >> s) & 1) == 1) & (((c >> s) & 1) == 0)

    Ts = [eye + jnp.where(joins(1), x, 0.0) for x in Xs]
    b = 2
    while b < n:
        m = joins(b)
        Ys = [_dot16(jnp.where(m, x, 0.0), t) for x, t in zip(Xs, Ts)]
        Ts = [t + _dot16(t, y) for t, y in zip(Ts, Ys)]
        b *= 2
    Rs = [(eye - t) + _dot_3pass(x, t) for x, t in zip(Xs, Ts)]
    return [t + _dot16(t, r_) for t, r_ in zip(Ts, Rs)]


def _unit_lower_inverses_fwd(Xs):
    Ts = _unit_lower_inverses(Xs)
    return Ts, Ts


def _unit_lower_inverses_bwd(Ts, gs):
    inner = [mm_nt(g, t) for g, t in zip(gs, Ts)]
    return ([mm_tn(t, i) for t, i in zip(Ts, inner)],)


_unit_lower_inverses.defvjp(_unit_lower_inverses_fwd, _unit_lower_inverses_bwd)

def _delta_chunks(qs, ks, vs, gbs, Ss):
    C = CHUNK
    nb = len(gbs)
    pairs = [(c, h) for c in range(nb) for h in range(HEADS)]
    each = lambda fn, *lists: [fn(*a) for a in zip(*lists)]
    row = _iota2((C, C), 0)
    col = _iota2((C, C), 1)
    causal = row >= col
    strict = row > col
    tri = causal.astype(F32)
    eye = (row == col).astype(F32)
    lane = _iota2((C, 128), 1)
    subl = _iota2((128, C), 0)
    last = (_iota2((C, 1), 0) == C - 1).astype(F32)

    gc_all = [mm_nn(tri, gb, True) for gb in gbs]
    gc_t = [g.T for g in gc_all]
    q = [qs[c][h] * (HD ** -0.5) for c, h in pairs]
    k = [ks[c][h] for c, h in pairs]
    v = [vs[c][h] for c, h in pairs]
    gcol = [jnp.sum(jnp.where(lane == h, gc_all[c], 0.0), axis=1, keepdims=True) for c, h in pairs]
    grow = [jnp.sum(jnp.where(subl == h, gc_t[c], 0.0), axis=0, keepdims=True) for c, h in pairs]
    beta = [jnp.sum(jnp.where(lane == HEADS + h, gbs[c], 0.0), axis=1, keepdims=True) for c, h in pairs]
    decay = each(lambda a, b: jnp.where(causal, jnp.exp(jnp.where(causal, a - b, 0.0)), 0.0), gcol, grow)
    kb = each(lambda a, b: a * b, k, beta)
    vb = each(lambda a, b: a * b, v, beta)
    kk = each(lambda a, b: mm_nt(a, b), kb, k)
    X = each(lambda a, d: -jnp.where(strict, a * d, 0.0), kk, decay)
    T = _unit_lower_inverses(X)
    eg = [jnp.exp(g) for g in gcol]
    u = each(lambda t, a: mm_nn(t, a), T, vb)
    w = each(lambda t, a, e: mm_nn(t, a * e), T, kb, eg)
    qk = each(lambda a, b, d: jnp.where(causal, mm_nt(a, b) * d, 0.0), q, k, decay)
    qg = each(lambda a, e: a * e, q, eg)
    g_last = [jnp.sum(g * last, axis=0, keepdims=True) for g in gcol]
    kd = each(lambda a, gl, g: a * jnp.exp(gl - g), k, g_last, gcol)
    eg_last = [jnp.exp(g) for g in g_last]

    outs = []
    for c in range(nb):
        sl = slice(c * HEADS, (c + 1) * HEADS)
        v_new = each(lambda a, b, S: a - mm_nn(b, S), u[sl], w[sl], Ss)
        oS = each(lambda a, S: mm_nn(a, S), qg[sl], Ss)
        outs.append(each(lambda a, b, n: a + mm_nn(b, n), oS, qk[sl], v_new))
        Ss = each(lambda S, e, a, n: S * e + mm_tn(a, n), Ss, eg_last[sl], kd[sl], v_new)
    return outs, Ss


def _split_chunks(ref, nb):
    return [[ref[c * CHUNK:(c + 1) * CHUNK, h * HD:(h + 1) * HD] for h in range(HEADS)] for c in range(nb)]


def _join_chunks(vals):
    return jnp.concatenate([jnp.concatenate(heads, axis=1) for heads in vals], axis=0)


def _delta_fwd(qkv, gb, nb):
    T = qkv.shape[0]
    nb = min(nb, T // CHUNK)
    rows = nb * CHUNK
    n = T // rows

    def body(q_ref, k_ref, v_ref, gb_ref, o_ref, ss_ref, s_scr):
        @pl.when(pl.program_id(0) == 0)
        def _():
            s_scr[...] = jnp.zeros_like(s_scr)

        Ss = [s_scr[h] for h in range(HEADS)]
        for h in range(HEADS):
            ss_ref[0, h] = Ss[h]
        gbs = [gb_ref[c * CHUNK:(c + 1) * CHUNK, :] for c in range(nb)]
        outs, new_S = _delta_chunks(_split_chunks(q_ref, nb), _split_chunks(k_ref, nb), _split_chunks(v_ref, nb), gbs, Ss)
        o_ref[...] = _join_chunks(outs)
        s_scr[...] = jnp.stack(new_S)

    row = lambda w, j=0: pl.BlockSpec((rows, w), lambda i: (i, j))
    return pl.pallas_call(
        body, name="delta_fwd", grid=(n,),
        in_specs=[row(D_DN, 0), row(D_DN, 1), row(D_DN, 2), row(128)],
        out_specs=[row(D_DN), pl.BlockSpec((1, HEADS, HD, HD), lambda i: (i, 0, 0, 0))],
        out_shape=[jax.ShapeDtypeStruct((T, D_DN), F32), jax.ShapeDtypeStruct((n, HEADS, HD, HD), F32)],
        scratch_shapes=[pltpu.VMEM((HEADS, HD, HD), F32)],
        compiler_params=pltpu.CompilerParams(dimension_semantics=("arbitrary",), vmem_limit_bytes=VMEM_LIMIT),
    )(qkv, qkv, qkv, gb)


def _delta_bwd(qkv, gb, ss, do, nb):
    T = qkv.shape[0]
    nb = min(nb, T // CHUNK)
    rows = nb * CHUNK
    n = T // rows

    def body(q_ref, k_ref, v_ref, gb_ref, ss_ref, do_ref, d_ref, ds_scr):
        @pl.when(pl.program_id(0) == 0)
        def _():
            ds_scr[...] = jnp.zeros_like(ds_scr)

        Ss = [ss_ref[0, h] for h in range(HEADS)]
        gbs = [gb_ref[c * CHUNK:(c + 1) * CHUNK, :] for c in range(nb)]
        _, vjp = jax.vjp(_delta_chunks, _split_chunks(q_ref, nb), _split_chunks(k_ref, nb), _split_chunks(v_ref, nb), gbs, Ss)
        dqs, dks, dvs, dgbs, dSs = vjp((_split_chunks(do_ref, nb), [ds_scr[h] for h in range(HEADS)]))
        d_ref[...] = jnp.concatenate([_join_chunks(dqs), _join_chunks(dks), _join_chunks(dvs),
                                      jnp.concatenate(dgbs, axis=0)], axis=1)
        ds_scr[...] = jnp.stack(dSs)

    row = lambda w, j=0: pl.BlockSpec((rows, w), lambda i: (n - 1 - i, j))
    return pl.pallas_call(
        body, name="delta_bwd", grid=(n,),
        in_specs=[row(D_DN, 0), row(D_DN, 1), row(D_DN, 2), row(128),
                  pl.BlockSpec((1, HEADS, HD, HD), lambda i: (n - 1 - i, 0, 0, 0)), row(D_DN)],
        out_specs=row(3 * D_DN + 128),
        out_shape=jax.ShapeDtypeStruct((T, 3 * D_DN + 128), F32),
        scratch_shapes=[pltpu.VMEM((HEADS, HD, HD), F32)],
        compiler_params=pltpu.CompilerParams(dimension_semantics=("arbitrary",), vmem_limit_bytes=VMEM_LIMIT),
    )(qkv, qkv, qkv, gb, ss, do)


def _cols(arr, width, index, first_row=0):
    return (arr, width, index, first_row)


def _rowwise(name, fn, tiled, consts, out_tiled, out_acc, tm, rows=None, rider=None, layer=None, prev=None):
    tiled = [t if isinstance(t, tuple) else (t, t.shape[-1], 0, 0) for t in tiled]
    T = tiled[0][0].shape[-2] if rows is None else rows
    tm = min(tm, T)
    assert T % tm == 0 and all(r % tm == 0 for (_, _, _, r) in tiled)
    n_t, n_c, n_o, n_a = len(tiled), len(consts), len(out_tiled), len(out_acc)

    n_ri = len(rider.ins) if rider else 0
    n_ro = len(rider.outs) if rider else 0
    n_steps = T // tm

    def body(*refs):
        n_in = n_t + n_c + n_ri + (n_o if layer is not None else 0)
        r_ins = refs[n_t + n_c:n_t + n_c + n_ri]
        o_refs = refs[n_in:n_in + n_o]
        a_refs = refs[n_in + n_o:n_in + n_o + n_a]
        r_outs = refs[n_in + n_o + n_a:n_in + n_o + n_a + n_ro]
        sems = refs[n_in + n_o + n_a + n_ro:]
        if rider:
            @pl.when(pl.program_id(0) == 0)
            def _():
                rider.start(r_ins, r_outs, *sems)

        ins = [r[...] for r in refs[:n_t + n_c]]
        outs = fn(*ins)
        for r, val in zip(o_refs, outs[:n_o]):
            r[...] = val.astype(r.dtype)
        if n_a:
            @pl.when(pl.program_id(0) == 0)
            def _():
                for r in a_refs:
                    r[...] = jnp.zeros_like(r)
            for r, val in zip(a_refs, outs[n_o:]):
                r[...] += val
        if rider:
            @pl.when(pl.program_id(0) == n_steps - 1)
            def _():
                rider.finish(r_ins, r_outs, *sems)

    def const_spec(a):
        nd = a.ndim
        return pl.BlockSpec(a.shape, lambda i: (0,) * nd, pipeline_mode=pl.Buffered(1))

    def tile_spec(arr, w, j, r0):
        if arr.ndim == 3:
            return pl.BlockSpec((None, tm, w), lambda i: (layer, i + r0 // tm, j))
        return pl.BlockSpec((tm, w), lambda i: (i + r0 // tm, j))

    in_specs = [tile_spec(*t) for t in tiled]
    in_specs += [const_spec(a) for a in consts]
    if layer is None:
        out_specs = [pl.BlockSpec((tm, w), lambda i: (i, 0)) for (w, _) in out_tiled]
        out_shape = [jax.ShapeDtypeStruct((T, w), dt) for (w, dt) in out_tiled]
    else:
        out_specs = [pl.BlockSpec((None, tm, w), lambda i: (layer, i, 0)) for (w, _) in out_tiled]
        out_shape = [jax.ShapeDtypeStruct((DEPTH, T, w), dt) for (w, dt) in out_tiled]
    out_specs += [pl.BlockSpec(s, lambda i: (0, 0)) for (s, _) in out_acc]
    out_shape += [jax.ShapeDtypeStruct(s, dt) for (s, dt) in out_acc]
    operands = [t[0] for t in tiled] + list(consts)
    scratch = []
    aliases = {}
    if rider:
        anyspec = pl.BlockSpec(memory_space=pl.ANY)
        in_specs += [anyspec] * n_ri
        out_specs += [anyspec] * n_ro
        out_shape += list(rider.outs)
        operands += list(rider.ins)
        scratch = rider.scratch()
    n_prev = 0
    if layer is not None:
        assert rider is None and not out_acc
        if prev is None:
            prev = [jnp.zeros(o.shape, o.dtype) for o in out_shape]
        n_prev = len(prev)
        aliases = {len(operands) + i: i for i in range(n_prev)}
        in_specs += [pl.BlockSpec(memory_space=pl.ANY)] * n_prev
        operands += list(prev)
    return pl.pallas_call(
        body, name=name, grid=(n_steps,), in_specs=in_specs, out_specs=out_specs, out_shape=out_shape, scratch_shapes=scratch,
        input_output_aliases=aliases,
        compiler_params=pltpu.CompilerParams(dimension_semantics=("arbitrary",), vmem_limit_bytes=VMEM_LIMIT),
    )(*operands)


def _colsum(x):
    return jnp.sum(x, axis=0, keepdims=True)


def _rms(x):
    r = lax.rsqrt(jnp.mean(x * x, axis=-1, keepdims=True) + EPS)
    return x * r, r


def _rms_bwd(dxn, xn, r):
    return r * (dxn - xn * jnp.mean(dxn * xn, axis=-1, keepdims=True))


def _normproj_fwd(name, x, sh, sc, g, w, tm, rider=None, w_t=False):
    def fn(x, sh, sc, g, w):
        xn, _ = _rms(x)
        h = (xn * (g * (1.0 + sc)) + sh).astype(BF16)
        return lax.dot_general(h, w, (((1,), (1 if w_t else 0,)), ((), ())), preferred_element_type=F32), h

    return _rowwise(name, fn, [x], [sh, sc, g, w], [(w.shape[0 if w_t else 1], F32), (D, BF16)], [], tm, rider=rider)


def _normproj_bwd(name, x, dpre, dres, sc, g, w, tm, rider=None, w_t=False):
    def fn(x, dpre, dres, sc, g, w):
        xn, r = _rms(x)
        dh = lax.dot_general(dpre, w, (((1,), (0 if w_t else 1,)), ((), ())), preferred_element_type=F32)
        da = _colsum(dh * xn)
        dx = _rms_bwd(dh * (g * (1.0 + sc)), xn, r) + dres
        return dx, _colsum(dh), da * g, da * (1.0 + sc)

    vec = ((1, D), F32)
    return _rowwise(name, fn, [x, dpre, dres], [sc, g, w], [(D, F32)], [vec, vec, vec], tm, rider=rider)


def _stage2(a_b, blk, cp, u1c, qp, kp, vp, bb, lg, lb, alog, dtb):
    y_a = a_b * cp
    u1 = u1c + bb
    mu = jnp.mean(u1, axis=-1, keepdims=True)
    uc = u1 - mu
    var = jnp.mean(uc * uc, axis=-1, keepdims=True)
    y_b = _silu(uc * lax.rsqrt(var + LN_EPS) * lg + lb)

    def l2(t):
        t = _silu(t)
        return t * lax.rsqrt(jnp.sum(t * t, axis=-1, keepdims=True) + EPS)

    q = [l2(t) for t in qp]
    k = [l2(t) for t in kp]
    v = _silu(vp)
    lane = _iota2(blk.shape, 1)
    gdec = -jnp.exp(alog) * _softplus(blk + dtb)
    gb = jnp.where(lane < HEADS, gdec, jnp.where(lane < 2 * HEADS, _sigmoid(blk), 0.0))
    return y_a, y_b, q, k, v, gb


def _heads_of(x, base=0):
    return [x[:, base + h * HD:base + (h + 1) * HD] for h in range(HEADS)]


def _stage2_fwd(proj, convout, bb, lg, lb, alog, dtb, tm):
    def fn(a_b, blk, co, bb, lg, lb, alog, dtb):
        y_a, y_b, q, k, v, gb = _stage2(a_b, blk, co[:, 0:256], co[:, 256:512], _heads_of(co, 512), _heads_of(co, 1024),
                                        co[:, 1536:2048], bb, lg, lb, alog, dtb)
        return jnp.concatenate([y_a, y_b], axis=1), jnp.concatenate(q + k + [v], axis=1), gb

    return _rowwise("stage2_fwd", fn, [_cols(proj, 256, 0), _cols(proj, 128, C_GB // 128), convout],
                    [bb, lg, lb, alog, dtb], [(512, BF16), (1536, F32), (128, F32)], [], tm)


def _stage2_bwd(proj, convout, dy_ab, dq, dk, dv, dgb, bb, lg, lb, alog, dtb, tm, rider=None):
    def fn(a_b, blk, co, dy_ab, dq, dk, dv, dgb, bb, lg, lb, alog, dtb):
        args = (a_b, blk, co[:, 0:256], co[:, 256:512], _heads_of(co, 512), _heads_of(co, 1024), co[:, 1536:2048],
                bb, lg, lb, alog, dtb)
        _, vjp = jax.vjp(_stage2, *args)
        ct = (dy_ab[:, 0:256], dy_ab[:, 256:512], _heads_of(dq), _heads_of(dk), dv, dgb)
        da_b, dblk, dcp, du1c, dqp, dkp, dvp, dbb, dlg, dlb, dalog, ddtb = vjp(ct)
        dco = jnp.concatenate([dcp, du1c] + dqp + dkp + [dvp], axis=1)
        return dco, da_b, dblk, dbb, dlg, dlb, dalog, ddtb

    v256, v128 = ((1, 256), F32), ((1, 128), F32)
    return _rowwise("stage2_bwd", fn,
                    [_cols(proj, 256, 0), _cols(proj, 128, C_GB // 128), convout, dy_ab, dq, dk, dv, dgb],
                    [bb, lg, lb, alog, dtb], [(N_CONVCOL, F32), (256, F32), (128, F32)],
                    [v256, v256, v256, v128, v128], tm, rider=rider)


def _stage3(o, z, dng):
    ys = []
    for oh, zh in zip(o, z):
        on = oh * lax.rsqrt(jnp.mean(oh * oh, axis=-1, keepdims=True) + EPS)
        ys.append(on * dng * _silu(zh))
    return ys


def _outproj_fwd(x, o, proj, y_ab, g1, dng, wout, tm):
    def fn(x, o, z0, z1, z2, z3, y_ab, g1, dng, wout):
        y_c = _stage3(_heads_of(o), [z0, z1, z2, z3], dng)
        ycat = jnp.concatenate([y_ab] + [t.astype(BF16) for t in y_c], axis=1)
        mix = jnp.dot(ycat, wout, preferred_element_type=F32)
        return x + g1 * mix, mix, ycat

    return _rowwise("outproj_fwd", fn, [x, o] + _z_heads(proj) + [y_ab],
                    [g1, dng, wout], [(D, F32), (D, F32), (D, BF16)], [], tm)


def _z_heads(proj):
    return [_cols(proj, HD, C_Z // HD + h) for h in range(HEADS)]


def _outproj_bwd(dx1, mix, o, proj, g1, dng, wout, tm):
    def fn(dx1, mix, o, z0, z1, z2, z3, g1, dng, wout):
        dmix = (dx1 * g1).astype(BF16)
        dycat = lax.dot_general(dmix, wout, (((1,), (1,)), ((), ())), preferred_element_type=F32)
        _, vjp = jax.vjp(_stage3, _heads_of(o), [z0, z1, z2, z3], dng)
        do, dz, ddng = vjp(_heads_of(dycat, 512))
        return (dmix, dycat[:, 0:512], jnp.concatenate(do, axis=1), jnp.concatenate(dz, axis=1),
                _colsum(dx1 * mix), ddng)

    return _rowwise("outproj_bwd", fn, [dx1, mix, o] + _z_heads(proj), [g1, dng, wout],
                    [(D, BF16), (512, F32), (512, F32), (512, F32)], [((1, D), F32), ((1, HD), F32)], tm)


_CONV_BLOCKS = ((0, 256, KA), (256, 512, KB), (512, 2048, KC))
_CONV_STRIP = 256


_CONV_ROWS = 32


def _conv_inputs(proj_ref, rows):
    a_c, a_v = proj_ref[rows, C_AC:C_AC + 256], proj_ref[rows, C_AV:C_AV + 256]
    b_a, b_g = proj_ref[rows, C_BA:C_BA + 256], proj_ref[rows, C_BG:C_BG + 256]
    return a_c, a_v, b_a, _sigmoid(b_g)


def _conv_fwd(proj, wa, wb, wc, tm):
    T = proj.shape[0]
    tm = min(tm, T)

    def body(proj_ref, wa_ref, wb_ref, wc_ref, out_ref, ext):
        @pl.when(pl.program_id(0) == 0)
        def _():
            ext[0:HALO, :] = jnp.zeros((HALO, N_CONVCOL), F32)

        a_c, a_v, b_a, sg = _conv_inputs(proj_ref, slice(None))
        ext[HALO:HALO + tm, 0:256] = a_c * a_v
        ext[HALO:HALO + tm, 256:512] = b_a * sg
        ext[HALO:HALO + tm, 512:2048] = proj_ref[:, C_Q:C_Q + 1536]
        for r0 in range(0, tm, _CONV_ROWS):
            for (c0, c1, kw), w_ref in zip(_CONV_BLOCKS, (wa_ref, wb_ref, wc_ref)):
                for s0 in range(c0, c1, _CONV_STRIP):
                    acc = jnp.zeros((_CONV_ROWS, _CONV_STRIP), F32)
                    for k in range(kw):
                        acc += (w_ref[k:k + 1, s0 - c0:s0 - c0 + _CONV_STRIP]
                                * ext[pl.ds(r0 + HALO - (kw - 1) + k, _CONV_ROWS), s0:s0 + _CONV_STRIP])
                    out_ref[r0:r0 + _CONV_ROWS, s0:s0 + _CONV_STRIP] = acc
        ext[0:HALO, :] = ext[tm:tm + HALO, :]

    full = lambda a: pl.BlockSpec(a.shape, lambda i: (0, 0))
    return pl.pallas_call(
        body, name="conv_fwd", grid=(T // tm,),
        in_specs=[pl.BlockSpec((tm, IN_PAD), lambda i: (i, 0)), full(wa), full(wb), full(wc)],
        out_specs=pl.BlockSpec((tm, N_CONVCOL), lambda i: (i, 0)),
        out_shape=jax.ShapeDtypeStruct((T, N_CONVCOL), F32),
        scratch_shapes=[pltpu.VMEM((HALO + tm, N_CONVCOL), F32)],
        compiler_params=pltpu.CompilerParams(dimension_semantics=("arbitrary",), vmem_limit_bytes=VMEM_LIMIT),
    )(proj, wa, wb, wc)


def _conv_bwd(proj, dco, da_b, dz, dblk, wa, wb, wc, tm):
    T = proj.shape[0]
    tm = min(tm, T)
    n = T // tm

    def body(proj_ref, dco_ref, dab_ref, dz_ref, dblk_ref, wa_ref, wb_ref, wc_ref,
             dproj_ref, dwa_ref, dwb_ref, dwc_ref, ext, acc_a, acc_b, acc_c):
        @pl.when(pl.program_id(0) == 0)
        def _():
            ext[tm:tm + HALO, :] = jnp.zeros((HALO, N_CONVCOL), F32)
            acc_a[...] = jnp.zeros_like(acc_a)
            acc_b[...] = jnp.zeros_like(acc_b)
            acc_c[...] = jnp.zeros_like(acc_c)

        ext[0:tm, :] = dco_ref[...]

        def taps(w_ref, acc_ref, kw, c0, wc0, xin, r0):
            dx = jnp.zeros((_CONV_ROWS, _CONV_STRIP), F32)
            for k in range(kw):
                sh = ext[pl.ds(r0 + kw - 1 - k, _CONV_ROWS), c0:c0 + _CONV_STRIP]
                dx += w_ref[k:k + 1, wc0:wc0 + _CONV_STRIP] * sh
                pr = sh * xin
                part = pr[0:8]
                for g in range(8, _CONV_ROWS, 8):
                    part += pr[g:g + 8]
                acc_ref[8 * k:8 * k + 8, wc0:wc0 + _CONV_STRIP] += part
            return dx

        for r0 in range(0, tm, _CONV_ROWS):
            rows = slice(r0, r0 + _CONV_ROWS)
            a_c, a_v, b_a, sg = _conv_inputs(proj_ref, rows)
            dp = taps(wa_ref, acc_a, KA, 0, 0, a_c * a_v, r0)
            dproj_ref[rows, C_AC:C_AC + 256] = (dp * a_v).astype(BF16)
            dproj_ref[rows, C_AV:C_AV + 256] = (dp * a_c).astype(BF16)
            du0 = taps(wb_ref, acc_b, KB, 256, 0, b_a * sg, r0)
            dproj_ref[rows, C_BA:C_BA + 256] = (du0 * sg).astype(BF16)
            dproj_ref[rows, C_BG:C_BG + 256] = (du0 * b_a * sg * (1.0 - sg)).astype(BF16)
            for s0 in range(0, 1536, _CONV_STRIP):
                dq = taps(wc_ref, acc_c, KC, 512 + s0, s0, proj_ref[rows, C_Q + s0:C_Q + s0 + _CONV_STRIP], r0)
                dproj_ref[rows, C_Q + s0:C_Q + s0 + _CONV_STRIP] = dq.astype(BF16)
        ext[tm:tm + HALO, :] = ext[0:HALO, :]

        dproj_ref[:, C_AB:C_AB + 256] = dab_ref[...].astype(BF16)
        dproj_ref[:, C_Z:C_Z + 512] = dz_ref[...].astype(BF16)
        dproj_ref[:, C_GB:C_GB + 128] = dblk_ref[...].astype(BF16)

        @pl.when(pl.program_id(0) == n - 1)
        def _():
            for acc_ref, dw_ref, kw in ((acc_a, dwa_ref, KA), (acc_b, dwb_ref, KB), (acc_c, dwc_ref, KC)):
                dw_ref[...] = jnp.zeros_like(dw_ref)
                for k in range(kw):
                    dw_ref[k:k + 1, :] = _colsum(acc_ref[8 * k:8 * k + 8, :])

    rev = lambda w: pl.BlockSpec((tm, w), lambda i: (n - 1 - i, 0))
    full = lambda a: pl.BlockSpec(a.shape, lambda i: (0, 0))
    return pl.pallas_call(
        body, name="conv_bwd", grid=(n,),
        in_specs=[rev(IN_PAD), rev(N_CONVCOL), rev(256), rev(512), rev(128), full(wa), full(wb), full(wc)],
        out_specs=[rev(IN_PAD), full(wa), full(wb), full(wc)],
        out_shape=[jax.ShapeDtypeStruct((T, IN_PAD), BF16), jax.ShapeDtypeStruct(wa.shape, F32),
                   jax.ShapeDtypeStruct(wb.shape, F32), jax.ShapeDtypeStruct(wc.shape, F32)],
        scratch_shapes=[pltpu.VMEM((tm + HALO, N_CONVCOL), F32), pltpu.VMEM((8 * KA, 256), F32),
                        pltpu.VMEM((8 * KB, 256), F32), pltpu.VMEM((8 * KC, 1536), F32)],
        compiler_params=pltpu.CompilerParams(dimension_semantics=("arbitrary",), vmem_limit_bytes=VMEM_LIMIT),
    )(proj, dco, da_b, dz, dblk, wa, wb, wc)


def _ffn_out_fwd(x1, gu, g2, wfo, tm, rider=None):
    def fn(x1, gu, g2, wfo):
        s = (_silu(gu[:, :D_FF]) * gu[:, D_FF:]).astype(BF16)
        f = jnp.dot(s, wfo, preferred_element_type=F32)
        return x1 + g2 * f, f

    return _rowwise("ffnout_fwd", fn, [x1, gu], [g2, wfo], [(D, F32), (D, F32)], [], tm, rider=rider)


def _ffn_out_bwd(dx2, gu, f, g2, wfo, tm, rider=None):
    def fn(dx2, gu, f, g2, wfo):
        gate, up = gu[:, :D_FF], gu[:, D_FF:]
        sg = _sigmoid(gate)
        sl = gate * sg
        df = (dx2 * g2).astype(BF16)
        ds = lax.dot_general(df, wfo, (((1,), (1,)), ((), ())), preferred_element_type=F32)
        dgate = ds * up * (sg * (1.0 + gate * (1.0 - sg)))
        dgu = jnp.concatenate([dgate.astype(BF16), (ds * sl).astype(BF16)], axis=1)
        return dgu, sl * up, df, _colsum(dx2 * f)

    return _rowwise("ffnout_bwd", fn, [dx2, gu, f], [g2, wfo], [(2 * D_FF, BF16), (D_FF, BF16), (D, BF16)],
                    [((1, D), F32)], tm, rider=rider)


def _loss_bwd(x, tgt, gfin, tm):
    def fn(x, tgt, gfin):
        xn, r = _rms(x)
        e = xn * gfin - tgt
        loss = 0.5 * jnp.sum(jnp.mean(e * e, axis=-1, keepdims=True), axis=0, keepdims=True)
        dy = e * (1.0 / D)
        return _rms_bwd(dy * gfin, xn, r), jnp.broadcast_to(loss, (1, 128)), _colsum(dy * xn)

    return _rowwise("loss_bwd", fn, [x, tgt], [gfin], [(D, F32)], [((1, 128), F32), ((1, D), F32)], tm)


def _wgrad(name, a, b, bm, bk):
    T, M = a.shape
    N = b.shape[1]
    bk = min(bk, T)

    def body(a_ref, b_ref, o_ref):
        @pl.when(pl.program_id(1) == 0)
        def _():
            o_ref[...] = jnp.zeros_like(o_ref)

        o_ref[...] += lax.dot_general(a_ref[...], b_ref[...], (((0,), (0,)), ((), ())), preferred_element_type=F32)

    return pl.pallas_call(
        body, name=name, grid=(M // bm, T // bk),
        in_specs=[pl.BlockSpec((bk, bm), lambda i, k: (k, i)), pl.BlockSpec((bk, N), lambda i, k: (k, 0))],
        out_specs=pl.BlockSpec((bm, N), lambda i, k: (i, 0)),
        out_shape=jax.ShapeDtypeStruct((M, N), F32),
        compiler_params=pltpu.CompilerParams(dimension_semantics=("arbitrary", "arbitrary"), vmem_limit_bytes=VMEM_LIMIT),
    )(a, b)


def _adamw(w, g, m, v):
    m = B1 * m + (1.0 - B1) * g
    v = B2 * v + (1.0 - B2) * (g * g)
    m_hat = m / (1.0 - B1 ** STEP)
    v_hat = v / (1.0 - B2 ** STEP)
    return -LR * (m_hat / (jnp.sqrt(v_hat) + AEPS) + WD * w), m, v


def _adam_call(name, w, g, m, v, tm):
    C = w.shape[1]
    return _rowwise(name, _adamw, [w, g, m, v], [], [(C, F32)] * 3, [], tm)


def _adam_layer(name, g, w, m, v, tm, layer, prev):
    C = g.shape[1]
    return _rowwise(name, lambda g, w, m, v: (g,) + _adamw(w, g, m, v), [g, w, m, v], [], [(C, F32)] * 4, [], tm,
                    layer=layer, prev=prev)


def _reduce_sum(name, own, recv, tm):
    n, R, C = recv.shape
    flat = recv.reshape(n * R, C)
    fn = lambda own, r0, r1, r2: (((own + r0.astype(F32)) + r1.astype(F32)) + r2.astype(F32),)
    return _rowwise(name, fn, [own] + [_cols(flat, C, 0, j * R) for j in range(n)], [], [(C, F32)], [], tm, rows=R)[0]


def _pair_add(name, mine, theirs_recv, tm):
    n, R, C = mine.shape

    def fn(a, b):
        p = a + b
        return p, p

    return _rowwise(name, fn, [mine.reshape(n * R, C), theirs_recv.reshape(n * R, C)], [], [(C, BF16), (C, F32)], [], tm)


def _reduce_adam(name, own, recv, w, m, v, tm, layer, prev):
    n, R, C = recv.shape
    flat = recv.reshape(n * R, C)

    def fn(own, r0, r1, r2, w, m, v):
        g = ((own + r0.astype(F32)) + r1.astype(F32)) + r2.astype(F32)
        return (g,) + _adamw(w, g, m, v)

    return _rowwise(name, fn, [own] + [_cols(flat, C, 0, j * R) for j in range(n)] + [w, m, v], [], [(C, F32)] * 4, [],
                    tm, rows=R, layer=layer, prev=prev)


_OFFSETS = [(dx, dy, dc) for dx in (0, 1) for dy in (0, 1) for dc in (0, 1)][1:]
_MESH = pl.DeviceIdType.MESH


def _coords():
    return lax.axis_index("x"), lax.axis_index("y"), lax.axis_index("c")


def _flip(me, off):
    return tuple((1 - m) if d else m for m, d in zip(me, off))


def _linear(p):
    return 4 * p[0] + 2 * p[1] + p[2]


_CHIP_FLIPS = ((1, 0), (0, 1), (1, 1))


class _Rider:
    def __init__(self, ins, outs, n_remote, n_local, start, finish):
        self.ins, self.outs, self.n_remote, self.n_local, self.start, self.finish = ins, outs, n_remote, n_local, start, finish

    def scratch(self):
        return [pltpu.SemaphoreType.DMA((self.n_remote,)), pltpu.SemaphoreType.DMA((self.n_remote,)),
                pltpu.SemaphoreType.DMA((max(self.n_local, 1),))]


def _run_rider(name, rider):
    def body(*refs):
        n_i, n_o = len(rider.ins), len(rider.outs)
        rider.start(refs[:n_i], refs[n_i:n_i + n_o], *refs[n_i + n_o:])
        rider.finish(refs[:n_i], refs[n_i:n_i + n_o], *refs[n_i + n_o:])

    anyspec = pl.BlockSpec(memory_space=pl.ANY)
    return pl.pallas_call(
        body, name=name, in_specs=[anyspec] * len(rider.ins), out_specs=[anyspec] * len(rider.outs),
        out_shape=list(rider.outs), scratch_shapes=rider.scratch(),
    )(*rider.ins)


def _gather_rider(arrs, layer):
    n = len(arrs)

    def parts(ins, outs, send, recv, loc):
        x, y, c = _coords()
        me, sib = (x, y, c), (x, y, 1 - c)
        chips = [((1 - x) if dx else x, (1 - y) if dy else y) for dx, dy in _CHIP_FLIPS]

        def copy(a, k, block, to, own=False):
            slot = outs[a].at[_linear(block)]
            return pltpu.make_async_remote_copy(src_ref=ins[a].at[layer] if own else slot, dst_ref=slot,
                                                send_sem=send.at[a * 7 + k], recv_sem=recv.at[a * 7 + k],
                                                device_id=to, device_id_type=_MESH)

        local = [pltpu.make_async_copy(ins[a].at[layer], outs[a].at[_linear(me)], loc.at[a]) for a in range(n)]
        first = []
        for a in range(n):
            first.append(copy(a, 0, me, sib, own=True))
            first += [copy(a, 1 + j, me, (*chip, c), own=True) for j, chip in enumerate(chips)]
        return copy, local, first, me, sib, chips, c

    def start(*refs):
        _, local, first, *_ = parts(*refs)
        for cp in local + first:
            cp.start()

    def finish(*refs):
        copy, local, first, me, sib, chips, c = parts(*refs)
        passed = []
        for j, chip in enumerate(chips):
            for a in range(n):
                copy(a, 1 + j, (*chip, c), me).wait_recv()
                cp = copy(a, 4 + j, (*chip, c), sib)
                cp.start()
                passed.append(cp)
        for a in range(n):
            copy(a, 0, sib, me).wait_recv()
            for j, chip in enumerate(chips):
                copy(a, 4 + j, (*chip, 1 - c), me).wait_recv()
        for cp in first + passed:
            cp.wait_send()
        for cp in local:
            cp.wait()

    outs = [jax.ShapeDtypeStruct((N_DEV,) + a.shape[1:], a.dtype) for a in arrs]
    return _Rider(list(arrs), outs, 7 * n, n, start, finish)


def _simple_rider(ins, outs, n_remote, make):
    def start(*refs):
        for cp in make(*refs):
            cp.start()

    def finish(*refs):
        for cp in make(*refs):
            cp.wait()

    return _Rider(ins, outs, n_remote, 0, start, finish)


def _pair_rider(arrs):
    def make(ins, outs, send, recv, loc):
        x, y, c = _coords()
        return [pltpu.make_async_remote_copy(src_ref=ins[a], dst_ref=outs[a], send_sem=send.at[a], recv_sem=recv.at[a],
                                             device_id=(x, y, 1 - c), device_id_type=_MESH) for a in range(len(arrs))]

    return _simple_rider(list(arrs), [jax.ShapeDtypeStruct(a.shape, a.dtype) for a in arrs], len(arrs), make)


def _chip_rider(arrs):
    nf = len(_CHIP_FLIPS)

    def make(ins, outs, send, recv, loc):
        x, y, c = _coords()
        copies = []
        for a in range(len(arrs)):
            for k, (dx, dy) in enumerate(_CHIP_FLIPS):
                px, py = (1 - x) if dx else x, (1 - y) if dy else y
                copies.append(pltpu.make_async_remote_copy(
                    src_ref=ins[a].at[2 * px + py], dst_ref=outs[a].at[k], send_sem=send.at[a * nf + k],
                    recv_sem=recv.at[a * nf + k], device_id=(px, py, c), device_id_type=_MESH))
        return copies

    return _simple_rider(list(arrs), [jax.ShapeDtypeStruct((nf,) + a.shape[1:], a.dtype) for a in arrs], nf * len(arrs), make)


def _small_allgather(name, packed):
    R = packed.shape[0]

    def body(in_ref, all_ref, sum_ref, send, recv):
        me = _coords()
        my = _linear(me)
        all_ref[my] = in_ref[...]
        copies = []
        for k, off in enumerate(_OFFSETS):
            cp = pltpu.make_async_remote_copy(src_ref=in_ref, dst_ref=all_ref.at[my], send_sem=send.at[k], recv_sem=recv.at[k],
                                              device_id=_flip(me, off), device_id_type=_MESH)
            cp.start()
            copies.append(cp)
        for cp in copies:
            cp.wait()
        acc = all_ref[0]
        for j in range(1, N_DEV):
            acc = acc + all_ref[j]
        sum_ref[...] = acc

    vm = pl.BlockSpec(memory_space=pltpu.VMEM)
    return pl.pallas_call(
        body, name=name, in_specs=[vm], out_specs=[vm, vm],
        out_shape=[jax.ShapeDtypeStruct((N_DEV, R, 128), F32), jax.ShapeDtypeStruct((R, 128), F32)],
        scratch_shapes=[pltpu.SemaphoreType.DMA((len(_OFFSETS),)), pltpu.SemaphoreType.DMA((len(_OFFSETS),))],
        compiler_params=pltpu.CompilerParams(vmem_limit_bytes=VMEM_LIMIT),
    )(packed)


def _ada_mod(c, w16, bias):
    nc = w16.shape[2]
    kp = len(_OFFSETS)

    def body(c_ref, w_ref, b_ref, rows_ref, act_ref, cbuf, sbuf, send, recv):
        me = _coords()
        my = _linear(me)
        cbuf[my] = c_ref[...]
        copies = []
        for k, off in enumerate(_OFFSETS):
            cp = pltpu.make_async_remote_copy(src_ref=c_ref, dst_ref=cbuf.at[my], send_sem=send.at[k], recv_sem=recv.at[k],
                                              device_id=_flip(me, off), device_id_type=_MESH)
            cp.start()
            copies.append(cp)
        for cp in copies:
            cp.wait()
        act = _silu(jnp.concatenate([cbuf[j] for j in range(N_DEV)], axis=0))
        act_ref[...] = act
        act16 = act.astype(BF16)
        for l in range(DEPTH):
            ml = jnp.dot(act16, w_ref[l], preferred_element_type=F32) + b_ref[l:l + 1, :]
            for j in range(N_DEV):
                sbuf[j, l:l + 1, :] = ml[j:j + 1, :]
        rows_ref[my] = sbuf[my]
        copies = []
        for k, off in enumerate(_OFFSETS):
            peer = _flip(me, off)
            cp = pltpu.make_async_remote_copy(src_ref=sbuf.at[_linear(peer)], dst_ref=rows_ref.at[my], send_sem=send.at[kp + k],
                                              recv_sem=recv.at[kp + k], device_id=peer, device_id_type=_MESH)
            cp.start()
            copies.append(cp)
        for cp in copies:
            cp.wait()

    vm = pl.BlockSpec(memory_space=pltpu.VMEM)
    return pl.pallas_call(
        body, name="ada_mod", in_specs=[vm, vm, vm], out_specs=[vm, vm],
        out_shape=[jax.ShapeDtypeStruct((N_DEV, DEPTH, nc), F32), jax.ShapeDtypeStruct((N_DEV, D), F32)],
        scratch_shapes=[pltpu.VMEM((N_DEV, 1, D), F32), pltpu.VMEM((N_DEV, DEPTH, nc), F32),
                        pltpu.SemaphoreType.DMA((2 * kp,)), pltpu.SemaphoreType.DMA((2 * kp,))],
        compiler_params=pltpu.CompilerParams(vmem_limit_bytes=VMEM_LIMIT),
    )(c, w16, bias)


def _pack(arrs):
    parts = []
    for a in arrs:
        f = a.reshape(-1).astype(F32)
        parts.append(jnp.pad(f, (0, (-f.shape[0]) % 128)))
    flat = jnp.concatenate(parts)
    flat = jnp.pad(flat, (0, (-flat.shape[0]) % 1024))
    return flat.reshape(-1, 128)


def _unpack(packed, shapes):
    flat = packed.reshape(packed.shape[:-2] + (-1,))
    out, r = [], 0
    for s in shapes:
        n = int(np.prod(s))
        out.append(flat[..., r:r + n].reshape(packed.shape[:-2] + tuple(s)))
        r += -(-n // 128) * 128
    return out


def _pad_rows(w, rows):
    return jnp.pad(w, ((0, 0), (0, rows - w.shape[1]), (0, 0)))


_SMALL = ("b_ada", "norm_mix_g", "norm_ffn_g", "conv_a_w", "conf_dw_w", "conf_dw_b", "conf_ln_g", "conf_ln_b",
          "dn_conv_w", "dn_a_log", "dn_dt_bias", "dn_norm_g", "final_norm_g")
_BIG = ("w_in", "w_out", "w_ffn_in", "w_ffn_out")
_WEIGHTS = ("w_ada", "b_ada", "norm_mix_g", "norm_ffn_g", "w_in", "conv_a_w", "conf_dw_w", "conf_dw_b", "conf_ln_g",
            "conf_ln_b", "dn_conv_w", "dn_a_log", "dn_dt_bias", "dn_norm_g", "w_out", "w_ffn_in", "w_ffn_out",
            "final_norm_g")


def _step(x, c, loss_target, W, M, V):
    T = x.shape[1]
    me = _linear(_coords())
    cc = lax.axis_index("c")
    chip = 2 * lax.axis_index("x") + lax.axis_index("y")
    xs, tgt = x[0], loss_target[0]
    vec = lambda a: a.reshape(1, -1)

    nada = W["w_ada"].shape[2]
    rows, act_all = _ada_mod(c, W["w_ada"].astype(BF16), lax.dynamic_slice(W["b_ada"], (0, me * nada), (DEPTH, nada)))
    mod = rows.transpose(1, 0, 2).reshape(DEPTH, 6, 1, D)

    w16 = {k: W[k].astype(BF16) for k in _BIG}
    w16["w_ffn_in"] = W["w_ffn_in"].transpose(0, 2, 1).astype(BF16)

    def whole(g_in=None, g_out=None, g_fi=None, g_fo=None):
        out = {}
        if g_in is not None:
            out["w_in"] = jnp.pad(g_in.transpose(1, 0, 2).reshape(D, IN_COLS), ((0, 0), (0, IN_PAD - IN_COLS)))
        if g_out is not None:
            out["w_out"] = g_out.reshape(D, D)
        if g_fi is not None:
            out["w_ffn_in"] = g_fi.reshape(2 * D_FF, D)
        if g_fo is not None:
            out["w_ffn_out"] = g_fo.reshape(D_FF, D)
        return out

    wts = [dict() for _ in range(DEPTH)]
    wts[0] = whole(*_run_rider("gather_weights", _gather_rider([w16[k] for k in _BIG], 0)))
    conv_names = ("conv_a_w", "conf_dw_w", "dn_conv_w")
    conv_all, _ = _small_allgather("gather_conv_w", _pack([W[k] for k in conv_names]))
    conv_full = [t.transpose(1, 2, 0, 3).reshape(t.shape[1], t.shape[2], -1)
                 for t in _unpack(conv_all, [W[k].shape for k in conv_names])]
    wa, wb, wc = _pad_rows(conv_full[0], 8), _pad_rows(conv_full[1], 32), _pad_rows(conv_full[2], 8)
    lane_pad = lambda a: jnp.pad(a, ((0, 0), (0, 128 - a.shape[1])))
    alog, dtb = lane_pad(W["dn_a_log"]), lane_pad(W["dn_dt_bias"])

    saved = []
    xc = xs
    for l in range(DEPTH):
        more = l + 1 < DEPTH
        ride = lambda names: _gather_rider([w16[k] for k in names], l + 1) if more else None
        sh1, sc1, g1, sh2, sc2, g2 = [mod[l, i] for i in range(6)]
        proj, h, *g_in = _normproj_fwd("inproj_fwd", xc, sh1, sc1, vec(W["norm_mix_g"][l]), wts[l]["w_in"], 512,
                                       rider=ride(["w_in"]))
        convout = _conv_fwd(proj, wa[l], wb[l], wc[l], 256)
        y_ab, qkv, gb = _stage2_fwd(proj, convout, vec(W["conf_dw_b"][l]), vec(W["conf_ln_g"][l]), vec(W["conf_ln_b"][l]),
                                    alog[l:l + 1], dtb[l:l + 1], 256)
        o, ss = _delta_fwd(qkv, gb, DELTA_NB)
        x1, mix, ycat = _outproj_fwd(xc, o, proj, y_ab, g1, vec(W["dn_norm_g"][l]), wts[l]["w_out"], 512)
        gu, h2, *g_fi = _normproj_fwd("ffnin_fwd", x1, sh2, sc2, vec(W["norm_ffn_g"][l]), wts[l]["w_ffn_in"], 256,
                                      rider=ride(["w_ffn_in"]), w_t=True)
        x2, f, *g_fo_out = _ffn_out_fwd(x1, gu, g2, wts[l]["w_ffn_out"], 256, rider=ride(["w_ffn_out", "w_out"]))
        if more:
            wts[l + 1] = whole(g_in=g_in[0], g_fi=g_fi[0], g_fo=g_fo_out[0], g_out=g_fo_out[1])
        saved.append((xc, proj, h, convout, qkv, gb, o, ss, mix, ycat, x1, gu, h2, f))
        xc = x2

    dx, loss_row, d_gfin = _loss_bwd(xc, tgt, vec(W["final_norm_g"]), 512)
    loss = lax.psum(loss_row[0, 0], ("x", "y", "c"))

    big_out = {k: None for k in _BIG}
    dmod, small = [None] * DEPTH, [None] * DEPTH
    by_cols = lambda g: g.reshape(D, 4, 2, -1).transpose(2, 1, 0, 3)
    by_rows = lambda g: g.reshape(4, 2, -1, D).transpose(1, 0, 2, 3)
    pair_tm = lambda a: 512 if a.shape[1] % 128 == 0 else a.shape[1]

    def pair_sums(mine, from_sib):
        sums = [_pair_add("pair_add_" + k, a, b, pair_tm(a)) for k, a, b in zip(_BIG, mine, from_sib)]
        p16 = {k: p.reshape(a.shape) for k, (p, _), a in zip(_BIG, sums, mine)}
        own = {k: lax.dynamic_index_in_dim(q.reshape(a.shape), chip, 0, keepdims=False) for k, (_, q), a in zip(_BIG, sums, mine)}
        return p16, own

    def finish(layer, own, recv):
        for k in _BIG:
            r = recv[k]
            tm = 256 if r.shape[1] % 256 == 0 else r.shape[1]
            if k == "w_ffn_in":
                g = _reduce_sum("reduce_" + k, own[k], r, tm).T
                big_out[k] = _adam_layer("adam_" + k, g, W[k], M[k], V[k], 256, layer, big_out[k])
            else:
                big_out[k] = _reduce_adam("reduce_adam_" + k, own[k], r, W[k], M[k], V[k], tm, layer, big_out[k])

    above = None
    for l in reversed(range(DEPTH)):
        xc, proj, h, convout, qkv, gb, o, ss, mix, ycat, x1, gu, h2, f = saved[l]
        sh1, sc1, g1, sh2, sc2, g2 = [mod[l, i] for i in range(6)]
        gm, gf = vec(W["norm_mix_g"][l]), vec(W["norm_ffn_g"][l])
        bb, lg, lb = vec(W["conf_dw_b"][l]), vec(W["conf_ln_g"][l]), vec(W["conf_ln_b"][l])
        dng = vec(W["dn_norm_g"][l])
        wl = wts[l]

        dgu, s, df, d_g2, *from_sib = _ffn_out_bwd(dx, gu, f, g2, wl["w_ffn_out"], 256,
                                                   rider=_pair_rider(above[2]) if above else None)
        if above:
            p16, own = pair_sums(above[1], from_sib)
            recv = {}
        chips = lambda names: _chip_rider([p16[k] for k in names]) if above else None
        gw_fo = _wgrad("wgrad_ffn_out", s, df, 1408, 1024)
        dx1, d_sh2, d_sc2, d_gf, *got = _normproj_bwd("ffnin_bwd", x1, dgu, dx, sc2, gf, wl["w_ffn_in"], 256,
                                                      rider=chips(["w_ffn_in"]), w_t=True)
        if above:
            recv["w_ffn_in"] = got[0]
        gw_fi = _wgrad("wgrad_ffn_in", dgu, h2, 1408, 1024)
        dmix, dy_ab, do, dz, d_g1, d_dng = _outproj_bwd(dx1, mix, o, proj, g1, dng, wl["w_out"], 512)
        gw_out = _wgrad("wgrad_out", ycat, dmix, 512, 2048)
        dqkv = _delta_bwd(qkv, gb, ss, do, DELTA_NB)
        dco, da_b, dblk, d_bb, d_lg, d_lb, d_alog, d_dtb, *got = _stage2_bwd(
            proj, convout, dy_ab, _cols(dqkv, D_DN, 0), _cols(dqkv, D_DN, 1), _cols(dqkv, D_DN, 2),
            _cols(dqkv, 128, 3 * D_DN // 128), bb, lg, lb, alog[l:l + 1], dtb[l:l + 1], 256,
            rider=chips(["w_ffn_out", "w_out"]))
        if above:
            recv["w_ffn_out"], recv["w_out"] = got
        dproj, d_wa, d_wb, d_wc = _conv_bwd(proj, dco, da_b, dz, dblk, wa[l], wb[l], wc[l], 256)
        dx, d_sh1, d_sc1, d_gm, *got = _normproj_bwd("inproj_bwd", xc, dproj, dx1, sc1, gm, wl["w_in"], 512,
                                                     rider=chips(["w_in"]))
        if above:
            recv["w_in"] = got[0]
            finish(above[0], own, recv)
        gw_in = _wgrad("wgrad_in", h, dproj, 512, 1024)

        dmod[l] = jnp.concatenate([d_sh1, d_sc1, d_g1, d_sh2, d_sc2, d_g2], axis=1)
        small[l] = dict(norm_mix_g=d_gm, norm_ffn_g=d_gf, conv_a_w=d_wa[:KA], conf_dw_w=d_wb[:KB], conf_dw_b=d_bb,
                        conf_ln_g=d_lg, conf_ln_b=d_lb, dn_conv_w=d_wc[:KC], dn_a_log=d_alog, dn_dt_bias=d_dtb,
                        dn_norm_g=d_dng)
        parts = [by_cols(gw_in[:, :IN_COLS]), by_rows(gw_out), by_rows(gw_fi), by_rows(gw_fo)]
        above = (l, [lax.dynamic_index_in_dim(p, cc, 0, keepdims=False) for p in parts],
                 [lax.dynamic_index_in_dim(p, 1 - cc, 0, keepdims=False) for p in parts])

    p16, own = pair_sums(above[1], _run_rider("pair_exchange", _pair_rider(above[2])))
    finish(above[0], own, dict(zip(_BIG, _run_rider("chip_exchange", _chip_rider([p16[k] for k in _BIG])))))

    names = ("norm_mix_g", "norm_ffn_g", "conv_a_w", "conf_dw_w", "conf_dw_b", "conf_ln_g", "conf_ln_b", "dn_conv_w",
             "dn_a_log", "dn_dt_bias", "dn_norm_g")
    pieces = [jnp.stack(dmod)] + [jnp.stack([small[l][k] for l in range(DEPTH)]) for k in names] + [d_gfin]
    shapes = [p.shape for p in pieces]
    every, total = _small_allgather("gather_small_grads", _pack(pieces))
    tot = dict(zip(("dmod",) + names + ("final_norm_g",), _unpack(total, shapes)))
    dmod_all = _unpack(every, shapes[:1])[0]

    grads = {}
    grads["b_ada"] = tot["dmod"].reshape(DEPTH, 6 * D)
    for k in ("norm_mix_g", "norm_ffn_g", "conf_dw_b", "conf_ln_g", "conf_ln_b", "dn_norm_g"):
        grads[k] = tot[k].reshape(W[k].shape)
    grads["dn_a_log"] = tot["dn_a_log"].reshape(DEPTH, 128)[:, :HEADS]
    grads["dn_dt_bias"] = tot["dn_dt_bias"].reshape(DEPTH, 128)[:, :HEADS]
    grads["final_norm_g"] = tot["final_norm_g"].reshape(D)
    for k in conv_names:
        nloc = W[k].shape[2]
        grads[k] = lax.dynamic_slice_in_dim(tot[k], me * nloc, nloc, axis=2)

    dm = lax.dynamic_slice_in_dim(dmod_all.reshape(N_DEV, DEPTH, 6 * D), me * nada, nada, axis=2)
    pad16 = lambda a: jnp.pad(a, ((0, 16 - N_DEV), (0, 0))).astype(BF16)
    g_ada = _wgrad("wgrad_ada", pad16(act_all), pad16(dm.reshape(N_DEV, DEPTH * nada)), 256, 16)
    grads["w_ada"] = g_ada.reshape(D, DEPTH, nada).transpose(1, 0, 2)

    delta, new_m, new_v = {}, {}, {}
    r2 = lambda a: a.reshape(DEPTH * D, nada)
    d_, m_, v_ = _adam_call("adam_ada", r2(W["w_ada"]), r2(grads["w_ada"]), r2(M["w_ada"]), r2(V["w_ada"]), 512)
    delta["w_ada"], new_m["w_ada"], new_v["w_ada"] = [t.reshape(W["w_ada"].shape) for t in (d_, m_, v_)]
    sshapes = [W[k].shape for k in _SMALL]
    d_, m_, v_ = _adam_call("adam_small", _pack([W[k] for k in _SMALL]), _pack([grads[k] for k in _SMALL]),
                            _pack([M[k] for k in _SMALL]), _pack([V[k] for k in _SMALL]), 4096)
    for dst, packed in ((delta, d_), (new_m, m_), (new_v, v_)):
        dst.update(zip(_SMALL, _unpack(packed, sshapes)))
    for k in _BIG:
        grads[k], delta[k], new_m[k], new_v[k] = big_out[k]

    return (loss, dx[None], *[grads[k] for k in _WEIGHTS], *[delta[k] for k in _WEIGHTS],
            *[new_m[k] for k in _WEIGHTS], *[new_v[k] for k in _WEIGHTS])


def kernel(x, c, w_ada, b_ada, norm_mix_g, norm_ffn_g, w_in, conv_a_w, conf_dw_w, conf_dw_b, conf_ln_g, conf_ln_b, dn_conv_w, dn_a_log, dn_dt_bias, dn_norm_g, w_out, w_ffn_in, w_ffn_out, final_norm_g, loss_target, m_w_ada, m_b_ada, m_norm_mix_g, m_norm_ffn_g, m_w_in, m_conv_a_w, m_conf_dw_w, m_conf_dw_b, m_conf_ln_g, m_conf_ln_b, m_dn_conv_w, m_dn_a_log, m_dn_dt_bias, m_dn_norm_g, m_w_out, m_w_ffn_in, m_w_ffn_out, m_final_norm_g, v_w_ada, v_b_ada, v_norm_mix_g, v_norm_ffn_g, v_w_in, v_conv_a_w, v_conf_dw_w, v_conf_dw_b, v_conf_ln_g, v_conf_ln_b, v_dn_conv_w, v_dn_a_log, v_dn_dt_bias, v_dn_norm_g, v_w_out, v_w_ffn_in, v_w_ffn_out, v_final_norm_g):
    a = dict(locals())
    W = {k: a[k] for k in _WEIGHTS}
    M = {k: a["m_" + k] for k in _WEIGHTS}
    V = {k: a["v_" + k] for k in _WEIGHTS}
    return _step(x, c, loss_target, W, M, V)
```

```python
import functools

import jax
import jax.numpy as jnp
import numpy as np
from jax import lax
from jax.experimental import pallas as pl
from jax.experimental.pallas import tpu as pltpu

F32 = jnp.float32
BF16 = jnp.bfloat16

N_DEV = 8
D = 1024
DEPTH = 4
D_CONV = 256
D_CONF = 256
D_DN = 512
HEADS = 4
HD = 128
KA, KB, KC = 3, 31, 4
CHUNK = 64
D_FF = 2816
IN_COLS = 3336
IN_PAD = 3456
N_CONVCOL = 2048
EPS = 1e-6
LN_EPS = 1e-5
HALO = 32
VMEM_LIMIT = 56 * 1024 * 1024
DELTA_NB = 8

C_AB, C_AC, C_AV, C_BA, C_BG, C_Q, C_Z, C_GB = 0, 256, 512, 768, 1024, 1280, 2816, 3328

LR, B1, B2, AEPS, WD, STEP = 0.001, 0.9, 0.999, 1e-08, 0.01, 10


def _dot(a, b, dims, hi):
    if hi:
        return lax.dot_general(a.astype(F32), b.astype(F32), (dims, ((), ())), precision=lax.Precision.HIGHEST,
                               preferred_element_type=F32)
    return lax.dot_general(a.astype(BF16), b.astype(BF16), (dims, ((), ())), preferred_element_type=F32)


@functools.partial(jax.custom_vjp, nondiff_argnums=(2,))
def mm_nn(a, b, hi=False):
    return _dot(a, b, ((1,), (0,)), hi)


@functools.partial(jax.custom_vjp, nondiff_argnums=(2,))
def mm_nt(a, b, hi=False):
    return _dot(a, b, ((1,), (1,)), hi)


@functools.partial(jax.custom_vjp, nondiff_argnums=(2,))
def mm_tn(a, b, hi=False):
    return _dot(a, b, ((0,), (0,)), hi)


mm_nn.defvjp(lambda a, b, hi: (mm_nn(a, b, hi), (a, b)),
             lambda hi, r, g: (mm_nt(g, r[1], hi), mm_tn(r[0], g, hi)))
mm_nt.defvjp(lambda a, b, hi: (mm_nt(a, b, hi), (a, b)),
             lambda hi, r, g: (mm_nn(g, r[1], hi), mm_tn(g, r[0], hi)))
mm_tn.defvjp(lambda a, b, hi: (mm_tn(a, b, hi), (a, b)),
             lambda hi, r, g: (mm_nt(r[1], g, hi), mm_nn(r[0], g, hi)))


def _sigmoid(x):
    return 1.0 / (1.0 + jnp.exp(-x))


def _silu(x):
    return x * _sigmoid(x)


def _softplus(x):
    return jnp.maximum(x, 0.0) + jnp.log(1.0 + jnp.exp(-jnp.abs(x)))


def _iota2(shape, dim):
    return lax.broadcasted_iota(jnp.int32, shape, dim)


def _dot16(a, b):
    return jnp.dot(a.astype(BF16), b.astype(BF16), preferred_element_type=F32)


def _dot_3pass(a, b):
    ah = a.astype(BF16)
    bh = b.astype(BF16)
    al = (a - ah.astype(F32)).astype(BF16)
    bl = (b - bh.astype(F32)).astype(BF16)
    d = lambda x, y: jnp.dot(x, y, preferred_element_type=F32)
    return d(ah, bh) + (d(ah, bl) + d(al, bh))


@jax.custom_vjp
def _unit_lower_inverses(Xs):
    n = Xs[0].shape[0]
    r, c = _iota2((n, n), 0), _iota2((n, n), 1)
    eye = (r == c).astype(F32)

    def joins(b):
        s = b.bit_length() - 1
        return ((r >> (s + 1)) == (c >> (s + 1))) & (((r >> s) & 1) == 1) & (((c >> s) & 1) == 0)

    Ts = [eye + jnp.where(joins(1), x, 0.0) for x in Xs]
    b = 2
    while b < n:
        m = joins(b)
        Ys = [_dot16(jnp.where(m, x, 0.0), t) for x, t in zip(Xs, Ts)]
        Ts = [t + _dot16(t, y) for t, y in zip(Ts, Ys)]
        b *= 2
    Rs = [(eye - t) + _dot_3pass(x, t) for x, t in zip(Xs, Ts)]
    return [t + _dot16(t, r_) for t, r_ in zip(Ts, Rs)]


def _unit_lower_inverses_fwd(Xs):
    Ts = _unit_lower_inverses(Xs)
    return Ts, Ts


def _unit_lower_inverses_bwd(Ts, gs):
    inner = [mm_nt(g, t) for g, t in zip(gs, Ts)]
    return ([mm_tn(t, i) for t, i in zip(Ts, inner)],)


_unit_lower_inverses.defvjp(_unit_lower_inverses_fwd, _unit_lower_inverses_bwd)

def _delta_chunks(qs, ks, vs, gbs, Ss):
    C = CHUNK
    nb = len(gbs)
    pairs = [(c, h) for c in range(nb) for h in range(HEADS)]
    each = lambda fn, *lists: [fn(*a) for a in zip(*lists)]
    row = _iota2((C, C), 0)
    col = _iota2((C, C), 1)
    causal = row >= col
    strict = row > col
    tri = causal.astype(F32)
    eye = (row == col).astype(F32)
    lane = _iota2((C, 128), 1)
    subl = _iota2((128, C), 0)
    last = (_iota2((C, 1), 0) == C - 1).astype(F32)

    gc_all = [mm_nn(tri, gb, True) for gb in gbs]
    gc_t = [g.T for g in gc_all]
    q = [qs[c][h] * (HD ** -0.5) for c, h in pairs]
    k = [ks[c][h] for c, h in pairs]
    v = [vs[c][h] for c, h in pairs]
    gcol = [jnp.sum(jnp.where(lane == h, gc_all[c], 0.0), axis=1, keepdims=True) for c, h in pairs]
    grow = [jnp.sum(jnp.where(subl == h, gc_t[c], 0.0), axis=0, keepdims=True) for c, h in pairs]
    beta = [jnp.sum(jnp.where(lane == HEADS + h, gbs[c], 0.0), axis=1, keepdims=True) for c, h in pairs]
    decay = each(lambda a, b: jnp.where(causal, jnp.exp(jnp.where(causal, a - b, 0.0)), 0.0), gcol, grow)
    kb = each(lambda a, b: a * b, k, beta)
    vb = each(lambda a, b: a * b, v, beta)
    kk = each(lambda a, b: mm_nt(a, b), kb, k)
    X = each(lambda a, d: -jnp.where(strict, a * d, 0.0), kk, decay)
    T = _unit_lower_inverses(X)
    eg = [jnp.exp(g) for g in gcol]
    u = each(lambda t, a: mm_nn(t, a), T, vb)
    w = each(lambda t, a, e: mm_nn(t, a * e), T, kb, eg)
    qk = each(lambda a, b, d: jnp.where(causal, mm_nt(a, b) * d, 0.0), q, k, decay)
    qg = each(lambda a, e: a * e, q, eg)
    g_last = [jnp.sum(g * last, axis=0, keepdims=True) for g in gcol]
    kd = each(lambda a, gl, g: a * jnp.exp(gl - g), k, g_last, gcol)
    eg_last = [jnp.exp(g) for g in g_last]

    outs = []
    for c in range(nb):
        sl = slice(c * HEADS, (c + 1) * HEADS)
        v_new = each(lambda a, b, S: a - mm_nn(b, S), u[sl], w[sl], Ss)
        oS = each(lambda a, S: mm_nn(a, S), qg[sl], Ss)
        outs.append(each(lambda a, b, n: a + mm_nn(b, n), oS, qk[sl], v_new))
        Ss = each(lambda S, e, a, n: S * e + mm_tn(a, n), Ss, eg_last[sl], kd[sl], v_new)
    return outs, Ss


def _split_chunks(ref, nb):
    return [[ref[c * CHUNK:(c + 1) * CHUNK, h * HD:(h + 1) * HD] for h in range(HEADS)] for c in range(nb)]


def _join_chunks(vals):
    return jnp.concatenate([jnp.concatenate(heads, axis=1) for heads in vals], axis=0)


def _hosted(body, n_in, n_out, rider, n_steps):
    if rider is None:
        return body
    n_ri, n_ro = len(rider.ins), len(rider.outs)

    def wrapped(*refs):
        ins, r_ins = refs[:n_in], refs[n_in:n_in + n_ri]
        outs = refs[n_in + n_ri:n_in + n_ri + n_out]
        r_outs = refs[n_in + n_ri + n_out:n_in + n_ri + n_out + n_ro]
        rest = refs[n_in + n_ri + n_out + n_ro:]
        scr, sems = rest[:len(rest) - 3], rest[len(rest) - 3:]

        @pl.when(pl.program_id(0) == 0)
        def _():
            rider.start(r_ins, r_outs, *sems)

        body(*ins, *outs, *scr)

        @pl.when(pl.program_id(0) == n_steps - 1)
        def _():
            rider.finish(r_ins, r_outs, *sems)

    return wrapped


def _host_call(body, name, grid, in_specs, out_specs, out_shape, scratch, operands, rider):
    out_specs, out_shape = list(out_specs), list(out_shape)
    n_in, n_out = len(in_specs), len(out_specs)
    if rider is not None:
        anyspec = pl.BlockSpec(memory_space=pl.ANY)
        in_specs = list(in_specs) + [anyspec] * len(rider.ins)
        out_specs += [anyspec] * len(rider.outs)
        out_shape += list(rider.outs)
        scratch = list(scratch) + rider.scratch()
        operands = list(operands) + list(rider.ins)
    return pl.pallas_call(
        _hosted(body, n_in, n_out, rider, grid[0]), name=name, grid=grid, in_specs=in_specs, out_specs=out_specs,
        out_shape=out_shape, scratch_shapes=scratch,
        compiler_params=pltpu.CompilerParams(dimension_semantics=("arbitrary",), vmem_limit_bytes=VMEM_LIMIT),
    )(*operands)


def _delta_fwd(qkv, gb, nb, rider=None):
    T = qkv.shape[0]
    nb = min(nb, T // CHUNK)
    rows = nb * CHUNK
    n = T // rows

    def body(q_ref, k_ref, v_ref, gb_ref, o_ref, ss_ref, s_scr):
        @pl.when(pl.program_id(0) == 0)
        def _():
            s_scr[...] = jnp.zeros_like(s_scr)

        Ss = [s_scr[h] for h in range(HEADS)]
        for h in range(HEADS):
            ss_ref[0, h] = Ss[h]
        gbs = [gb_ref[c * CHUNK:(c + 1) * CHUNK, :] for c in range(nb)]
        outs, new_S = _delta_chunks(_split_chunks(q_ref, nb), _split_chunks(k_ref, nb), _split_chunks(v_ref, nb), gbs, Ss)
        o_ref[...] = _join_chunks(outs)
        s_scr[...] = jnp.stack(new_S)

    row = lambda w, j=0: pl.BlockSpec((rows, w), lambda i: (i, j))
    return _host_call(
        body, "delta_fwd", (n,),
        [row(D_DN, 0), row(D_DN, 1), row(D_DN, 2), row(128)],
        [row(D_DN), pl.BlockSpec((1, HEADS, HD, HD), lambda i: (i, 0, 0, 0))],
        [jax.ShapeDtypeStruct((T, D_DN), F32), jax.ShapeDtypeStruct((n, HEADS, HD, HD), F32)],
        [pltpu.VMEM((HEADS, HD, HD), F32)], [qkv, qkv, qkv, gb], rider)


def _delta_bwd(qkv, gb, ss, do, nb, rider=None):
    T = qkv.shape[0]
    nb = min(nb, T // CHUNK)
    rows = nb * CHUNK
    n = T // rows

    def body(q_ref, k_ref, v_ref, gb_ref, ss_ref, do_ref, d_ref, ds_scr):
        @pl.when(pl.program_id(0) == 0)
        def _():
            ds_scr[...] = jnp.zeros_like(ds_scr)

        Ss = [ss_ref[0, h] for h in range(HEADS)]
        gbs = [gb_ref[c * CHUNK:(c + 1) * CHUNK, :] for c in range(nb)]
        _, vjp = jax.vjp(_delta_chunks, _split_chunks(q_ref, nb), _split_chunks(k_ref, nb), _split_chunks(v_ref, nb), gbs, Ss)
        dqs, dks, dvs, dgbs, dSs = vjp((_split_chunks(do_ref, nb), [ds_scr[h] for h in range(HEADS)]))
        d_ref[...] = jnp.concatenate([_join_chunks(dqs), _join_chunks(dks), _join_chunks(dvs),
                                      jnp.concatenate(dgbs, axis=0)], axis=1)
        ds_scr[...] = jnp.stack(dSs)

    row = lambda w, j=0: pl.BlockSpec((rows, w), lambda i: (n - 1 - i, j))
    return _host_call(
        body, "delta_bwd", (n,),
        [row(D_DN, 0), row(D_DN, 1), row(D_DN, 2), row(128),
         pl.BlockSpec((1, HEADS, HD, HD), lambda i: (n - 1 - i, 0, 0, 0)), row(D_DN)],
        [row(3 * D_DN + 128)], [jax.ShapeDtypeStruct((T, 3 * D_DN + 128), F32)],
        [pltpu.VMEM((HEADS, HD, HD), F32)], [qkv, qkv, qkv, gb, ss, do], rider)


def _cols(arr, width, index, first_row=0):
    return (arr, width, index, first_row)


def _rowwise(name, fn, tiled, consts, out_tiled, out_acc, tm, rows=None, rider=None, layer=None, prev=None):
    tiled = [t if isinstance(t, tuple) else (t, t.shape[-1], 0, 0) for t in tiled]
    T = tiled[0][0].shape[-2] if rows is None else rows
    tm = min(tm, T)
    assert T % tm == 0 and all(r % tm == 0 for (_, _, _, r) in tiled)
    n_t, n_c, n_o, n_a = len(tiled), len(consts), len(out_tiled), len(out_acc)

    n_ri = len(rider.ins) if rider else 0
    n_ro = len(rider.outs) if rider else 0
    n_steps = T // tm

    def body(*refs):
        n_in = n_t + n_c + n_ri + (n_o if layer is not None else 0)
        r_ins = refs[n_t + n_c:n_t + n_c + n_ri]
        o_refs = refs[n_in:n_in + n_o]
        a_refs = refs[n_in + n_o:n_in + n_o + n_a]
        r_outs = refs[n_in + n_o + n_a:n_in + n_o + n_a + n_ro]
        sems = refs[n_in + n_o + n_a + n_ro:]
        if rider:
            @pl.when(pl.program_id(0) == 0)
            def _():
                rider.start(r_ins, r_outs, *sems)

        ins = [r[...] for r in refs[:n_t + n_c]]
        outs = fn(*ins)
        for r, val in zip(o_refs, outs[:n_o]):
            r[...] = val.astype(r.dtype)
        if n_a:
            @pl.when(pl.program_id(0) == 0)
            def _():
                for r in a_refs:
                    r[...] = jnp.zeros_like(r)
            for r, val in zip(a_refs, outs[n_o:]):
                r[...] += val
        if rider:
            @pl.when(pl.program_id(0) == n_steps - 1)
            def _():
                rider.finish(r_ins, r_outs, *sems)

    def const_spec(a):
        nd = a.ndim
        return pl.BlockSpec(a.shape, lambda i: (0,) * nd, pipeline_mode=pl.Buffered(1))

    def tile_spec(arr, w, j, r0):
        if arr.ndim == 3:
            return pl.BlockSpec((None, tm, w), lambda i: (layer, i + r0 // tm, j))
        return pl.BlockSpec((tm, w), lambda i: (i + r0 // tm, j))

    in_specs = [tile_spec(*t) for t in tiled]
    in_specs += [const_spec(a) for a in consts]
    if layer is None:
        out_specs = [pl.BlockSpec((tm, w), lambda i: (i, 0)) for (w, _) in out_tiled]
        out_shape = [jax.ShapeDtypeStruct((T, w), dt) for (w, dt) in out_tiled]
    else:
        out_specs = [pl.BlockSpec((None, tm, w), lambda i: (layer, i, 0)) for (w, _) in out_tiled]
        out_shape = [jax.ShapeDtypeStruct((DEPTH, T, w), dt) for (w, dt) in out_tiled]
    out_specs += [pl.BlockSpec(s, lambda i: (0, 0)) for (s, _) in out_acc]
    out_shape += [jax.ShapeDtypeStruct(s, dt) for (s, dt) in out_acc]
    operands = [t[0] for t in tiled] + list(consts)
    scratch = []
    aliases = {}
    if rider:
        anyspec = pl.BlockSpec(memory_space=pl.ANY)
        in_specs += [anyspec] * n_ri
        out_specs += [anyspec] * n_ro
        out_shape += list(rider.outs)
        operands += list(rider.ins)
        scratch = rider.scratch()
    n_prev = 0
    if layer is not None:
        assert rider is None and not out_acc
        if prev is None:
            prev = [jnp.zeros(o.shape, o.dtype) for o in out_shape]
        n_prev = len(prev)
        aliases = {len(operands) + i: i for i in range(n_prev)}
        in_specs += [pl.BlockSpec(memory_space=pl.ANY)] * n_prev
        operands += list(prev)
    return pl.pallas_call(
        body, name=name, grid=(n_steps,), in_specs=in_specs, out_specs=out_specs, out_shape=out_shape, scratch_shapes=scratch,
        input_output_aliases=aliases,
        compiler_params=pltpu.CompilerParams(dimension_semantics=("arbitrary",), vmem_limit_bytes=VMEM_LIMIT),
    )(*operands)


def _colsum(x):
    return jnp.sum(x, axis=0, keepdims=True)


def _rms(x):
    r = lax.rsqrt(jnp.mean(x * x, axis=-1, keepdims=True) + EPS)
    return x * r, r


def _rms_bwd(dxn, xn, r):
    return r * (dxn - xn * jnp.mean(dxn * xn, axis=-1, keepdims=True))


def _normproj_fwd(name, x, sh, sc, g, w, tm, rider=None, w_t=False):
    def fn(x, sh, sc, g, w):
        xn, _ = _rms(x)
        h = (xn * (g * (1.0 + sc)) + sh).astype(BF16)
        return lax.dot_general(h, w, (((1,), (1 if w_t else 0,)), ((), ())), preferred_element_type=F32), h

    return _rowwise(name, fn, [x], [sh, sc, g, w], [(w.shape[0 if w_t else 1], F32), (D, BF16)], [], tm, rider=rider)


def _normproj_bwd(name, x, dpre, dres, sc, g, w, tm, rider=None, w_t=False):
    def fn(x, dpre, dres, sc, g, w):
        xn, r = _rms(x)
        dh = lax.dot_general(dpre, w, (((1,), (0 if w_t else 1,)), ((), ())), preferred_element_type=F32)
        da = _colsum(dh * xn)
        dx = _rms_bwd(dh * (g * (1.0 + sc)), xn, r) + dres
        return dx, _colsum(dh), da * g, da * (1.0 + sc)

    vec = ((1, D), F32)
    return _rowwise(name, fn, [x, dpre, dres], [sc, g, w], [(D, F32)], [vec, vec, vec], tm, rider=rider)


def _stage2(a_b, blk, cp, u1c, qp, kp, vp, bb, lg, lb, alog, dtb):
    y_a = a_b * cp
    u1 = u1c + bb
    mu = jnp.mean(u1, axis=-1, keepdims=True)
    uc = u1 - mu
    var = jnp.mean(uc * uc, axis=-1, keepdims=True)
    y_b = _silu(uc * lax.rsqrt(var + LN_EPS) * lg + lb)

    def l2(t):
        t = _silu(t)
        return t * lax.rsqrt(jnp.sum(t * t, axis=-1, keepdims=True) + EPS)

    q = [l2(t) for t in qp]
    k = [l2(t) for t in kp]
    v = _silu(vp)
    lane = _iota2(blk.shape, 1)
    gdec = -jnp.exp(alog) * _softplus(blk + dtb)
    gb = jnp.where(lane < HEADS, gdec, jnp.where(lane < 2 * HEADS, _sigmoid(blk), 0.0))
    return y_a, y_b, q, k, v, gb


def _heads_of(x, base=0):
    return [x[:, base + h * HD:base + (h + 1) * HD] for h in range(HEADS)]


def _stage2_fwd(proj, convout, bb, lg, lb, alog, dtb, tm):
    def fn(a_b, blk, co, bb, lg, lb, alog, dtb):
        y_a, y_b, q, k, v, gb = _stage2(a_b, blk, co[:, 0:256], co[:, 256:512], _heads_of(co, 512), _heads_of(co, 1024),
                                        co[:, 1536:2048], bb, lg, lb, alog, dtb)
        return jnp.concatenate([y_a, y_b], axis=1), jnp.concatenate(q + k + [v], axis=1), gb

    return _rowwise("stage2_fwd", fn, [_cols(proj, 256, 0), _cols(proj, 128, C_GB // 128), convout],
                    [bb, lg, lb, alog, dtb], [(512, BF16), (1536, F32), (128, F32)], [], tm)


def _stage2_bwd(proj, convout, dy_ab, dq, dk, dv, dgb, bb, lg, lb, alog, dtb, tm, rider=None):
    def fn(a_b, blk, co, dy_ab, dq, dk, dv, dgb, bb, lg, lb, alog, dtb):
        args = (a_b, blk, co[:, 0:256], co[:, 256:512], _heads_of(co, 512), _heads_of(co, 1024), co[:, 1536:2048],
                bb, lg, lb, alog, dtb)
        _, vjp = jax.vjp(_stage2, *args)
        ct = (dy_ab[:, 0:256], dy_ab[:, 256:512], _heads_of(dq), _heads_of(dk), dv, dgb)
        da_b, dblk, dcp, du1c, dqp, dkp, dvp, dbb, dlg, dlb, dalog, ddtb = vjp(ct)
        dco = jnp.concatenate([dcp, du1c] + dqp + dkp + [dvp], axis=1)
        return dco, da_b, dblk, dbb, dlg, dlb, dalog, ddtb

    v256, v128 = ((1, 256), F32), ((1, 128), F32)
    return _rowwise("stage2_bwd", fn,
                    [_cols(proj, 256, 0), _cols(proj, 128, C_GB // 128), convout, dy_ab, dq, dk, dv, dgb],
                    [bb, lg, lb, alog, dtb], [(N_CONVCOL, F32), (256, F32), (128, F32)],
                    [v256, v256, v256, v128, v128], tm, rider=rider)


def _stage3(o, z, dng):
    ys = []
    for oh, zh in zip(o, z):
        on = oh * lax.rsqrt(jnp.mean(oh * oh, axis=-1, keepdims=True) + EPS)
        ys.append(on * dng * _silu(zh))
    return ys


def _outproj_fwd(x, o, proj, y_ab, g1, dng, wout, tm, rider=None):
    def fn(x, o, z0, z1, z2, z3, y_ab, g1, dng, wout):
        y_c = _stage3(_heads_of(o), [z0, z1, z2, z3], dng)
        ycat = jnp.concatenate([y_ab] + [t.astype(BF16) for t in y_c], axis=1)
        mix = jnp.dot(ycat, wout, preferred_element_type=F32)
        return x + g1 * mix, mix, ycat

    return _rowwise("outproj_fwd", fn, [x, o] + _z_heads(proj) + [y_ab],
                    [g1, dng, wout], [(D, F32), (D, F32), (D, BF16)], [], tm, rider=rider)


def _z_heads(proj):
    return [_cols(proj, HD, C_Z // HD + h) for h in range(HEADS)]


def _outproj_bwd(dx1, mix, o, proj, g1, dng, wout, tm, rider=None):
    def fn(dx1, mix, o, z0, z1, z2, z3, g1, dng, wout):
        dmix = (dx1 * g1).astype(BF16)
        dycat = lax.dot_general(dmix, wout, (((1,), (1,)), ((), ())), preferred_element_type=F32)
        _, vjp = jax.vjp(_stage3, _heads_of(o), [z0, z1, z2, z3], dng)
        do, dz, ddng = vjp(_heads_of(dycat, 512))
        return (dmix, dycat[:, 0:512], jnp.concatenate(do, axis=1), jnp.concatenate(dz, axis=1),
                _colsum(dx1 * mix), ddng)

    return _rowwise("outproj_bwd", fn, [dx1, mix, o] + _z_heads(proj), [g1, dng, wout],
                    [(D, BF16), (512, F32), (512, F32), (512, F32)], [((1, D), F32), ((1, HD), F32)], tm, rider=rider)


_CONV_BLOCKS = ((0, 256, KA), (256, 512, KB), (512, 2048, KC))
_CONV_STRIP = 256


_CONV_ROWS = 32


def _conv_inputs(proj_ref, rows):
    a_c, a_v = proj_ref[rows, C_AC:C_AC + 256], proj_ref[rows, C_AV:C_AV + 256]
    b_a, b_g = proj_ref[rows, C_BA:C_BA + 256], proj_ref[rows, C_BG:C_BG + 256]
    return a_c, a_v, b_a, _sigmoid(b_g)


def _conv_fwd(proj, wa, wb, wc, tm):
    T = proj.shape[0]
    tm = min(tm, T)

    def body(proj_ref, wa_ref, wb_ref, wc_ref, out_ref, ext):
        @pl.when(pl.program_id(0) == 0)
        def _():
            ext[0:HALO, :] = jnp.zeros((HALO, N_CONVCOL), F32)

        a_c, a_v, b_a, sg = _conv_inputs(proj_ref, slice(None))
        ext[HALO:HALO + tm, 0:256] = a_c * a_v
        ext[HALO:HALO + tm, 256:512] = b_a * sg
        ext[HALO:HALO + tm, 512:2048] = proj_ref[:, C_Q:C_Q + 1536]
        for r0 in range(0, tm, _CONV_ROWS):
            for (c0, c1, kw), w_ref in zip(_CONV_BLOCKS, (wa_ref, wb_ref, wc_ref)):
                for s0 in range(c0, c1, _CONV_STRIP):
                    acc = jnp.zeros((_CONV_ROWS, _CONV_STRIP), F32)
                    for k in range(kw):
                        acc += (w_ref[k:k + 1, s0 - c0:s0 - c0 + _CONV_STRIP]
                                * ext[pl.ds(r0 + HALO - (kw - 1) + k, _CONV_ROWS), s0:s0 + _CONV_STRIP])
                    out_ref[r0:r0 + _CONV_ROWS, s0:s0 + _CONV_STRIP] = acc
        ext[0:HALO, :] = ext[tm:tm + HALO, :]

    full = lambda a: pl.BlockSpec(a.shape, lambda i: (0, 0))
    return pl.pallas_call(
        body, name="conv_fwd", grid=(T // tm,),
        in_specs=[pl.BlockSpec((tm, IN_PAD), lambda i: (i, 0)), full(wa), full(wb), full(wc)],
        out_specs=pl.BlockSpec((tm, N_CONVCOL), lambda i: (i, 0)),
        out_shape=jax.ShapeDtypeStruct((T, N_CONVCOL), F32),
        scratch_shapes=[pltpu.VMEM((HALO + tm, N_CONVCOL), F32)],
        compiler_params=pltpu.CompilerParams(dimension_semantics=("arbitrary",), vmem_limit_bytes=VMEM_LIMIT),
    )(proj, wa, wb, wc)


def _conv_bwd(proj, dco, da_b, dz, dblk, wa, wb, wc, tm, rider=None):
    T = proj.shape[0]
    tm = min(tm, T)
    n = T // tm

    def body(proj_ref, dco_ref, dab_ref, dz_ref, dblk_ref, wa_ref, wb_ref, wc_ref,
             dproj_ref, dwa_ref, dwb_ref, dwc_ref, ext, acc_a, acc_b, acc_c):
        @pl.when(pl.program_id(0) == 0)
        def _():
            ext[tm:tm + HALO, :] = jnp.zeros((HALO, N_CONVCOL), F32)
            acc_a[...] = jnp.zeros_like(acc_a)
            acc_b[...] = jnp.zeros_like(acc_b)
            acc_c[...] = jnp.zeros_like(acc_c)

        ext[0:tm, :] = dco_ref[...]

        def taps(w_ref, acc_ref, kw, c0, wc0, xin, r0):
            dx = jnp.zeros((_CONV_ROWS, _CONV_STRIP), F32)
            for k in range(kw):
                sh = ext[pl.ds(r0 + kw - 1 - k, _CONV_ROWS), c0:c0 + _CONV_STRIP]
                dx += w_ref[k:k + 1, wc0:wc0 + _CONV_STRIP] * sh
                pr = sh * xin
                part = pr[0:8]
                for g in range(8, _CONV_ROWS, 8):
                    part += pr[g:g + 8]
                acc_ref[8 * k:8 * k + 8, wc0:wc0 + _CONV_STRIP] += part
            return dx

        for r0 in range(0, tm, _CONV_ROWS):
            rows = slice(r0, r0 + _CONV_ROWS)
            a_c, a_v, b_a, sg = _conv_inputs(proj_ref, rows)
            dp = taps(wa_ref, acc_a, KA, 0, 0, a_c * a_v, r0)
            dproj_ref[rows, C_AC:C_AC + 256] = (dp * a_v).astype(BF16)
            dproj_ref[rows, C_AV:C_AV + 256] = (dp * a_c).astype(BF16)
            du0 = taps(wb_ref, acc_b, KB, 256, 0, b_a * sg, r0)
            dproj_ref[rows, C_BA:C_BA + 256] = (du0 * sg).astype(BF16)
            dproj_ref[rows, C_BG:C_BG + 256] = (du0 * b_a * sg * (1.0 - sg)).astype(BF16)
            for s0 in range(0, 1536, _CONV_STRIP):
                dq = taps(wc_ref, acc_c, KC, 512 + s0, s0, proj_ref[rows, C_Q + s0:C_Q + s0 + _CONV_STRIP], r0)
                dproj_ref[rows, C_Q + s0:C_Q + s0 + _CONV_STRIP] = dq.astype(BF16)
        ext[tm:tm + HALO, :] = ext[0:HALO, :]

        dproj_ref[:, C_AB:C_AB + 256] = dab_ref[...].astype(BF16)
        dproj_ref[:, C_Z:C_Z + 512] = dz_ref[...].astype(BF16)
        dproj_ref[:, C_GB:C_GB + 128] = dblk_ref[...].astype(BF16)

        @pl.when(pl.program_id(0) == n - 1)
        def _():
            for acc_ref, dw_ref, kw in ((acc_a, dwa_ref, KA), (acc_b, dwb_ref, KB), (acc_c, dwc_ref, KC)):
                dw_ref[...] = jnp.zeros_like(dw_ref)
                for k in range(kw):
                    dw_ref[k:k + 1, :] = _colsum(acc_ref[8 * k:8 * k + 8, :])

    rev = lambda w: pl.BlockSpec((tm, w), lambda i: (n - 1 - i, 0))
    full = lambda a: pl.BlockSpec(a.shape, lambda i: (0, 0))
    return _host_call(
        body, "conv_bwd", (n,),
        [rev(IN_PAD), rev(N_CONVCOL), rev(256), rev(512), rev(128), full(wa), full(wb), full(wc)],
        [rev(IN_PAD), full(wa), full(wb), full(wc)],
        [jax.ShapeDtypeStruct((T, IN_PAD), BF16), jax.ShapeDtypeStruct(wa.shape, F32),
         jax.ShapeDtypeStruct(wb.shape, F32), jax.ShapeDtypeStruct(wc.shape, F32)],
        [pltpu.VMEM((tm + HALO, N_CONVCOL), F32), pltpu.VMEM((8 * KA, 256), F32),
         pltpu.VMEM((8 * KB, 256), F32), pltpu.VMEM((8 * KC, 1536), F32)],
        [proj, dco, da_b, dz, dblk, wa, wb, wc], rider)


def _ffn_out_fwd(x1, gu, g2, wfo, tm, rider=None):
    def fn(x1, gu, g2, wfo):
        s = (_silu(gu[:, :D_FF]) * gu[:, D_FF:]).astype(BF16)
        f = jnp.dot(s, wfo, preferred_element_type=F32)
        return x1 + g2 * f, f

    return _rowwise("ffnout_fwd", fn, [x1, gu], [g2, wfo], [(D, F32), (D, F32)], [], tm, rider=rider)


def _ffn_out_bwd(dx2, gu, f, g2, wfo, tm, rider=None):
    def fn(dx2, gu, f, g2, wfo):
        gate, up = gu[:, :D_FF], gu[:, D_FF:]
        sg = _sigmoid(gate)
        sl = gate * sg
        df = (dx2 * g2).astype(BF16)
        ds = lax.dot_general(df, wfo, (((1,), (1,)), ((), ())), preferred_element_type=F32)
        dgate = ds * up * (sg * (1.0 + gate * (1.0 - sg)))
        dgu = jnp.concatenate([dgate.astype(BF16), (ds * sl).astype(BF16)], axis=1)
        return dgu, sl * up, df, _colsum(dx2 * f)

    return _rowwise("ffnout_bwd", fn, [dx2, gu, f], [g2, wfo], [(2 * D_FF, BF16), (D_FF, BF16), (D, BF16)],
                    [((1, D), F32)], tm, rider=rider)


def _loss_bwd(x, tgt, gfin, tm):
    def fn(x, tgt, gfin):
        xn, r = _rms(x)
        e = xn * gfin - tgt
        loss = 0.5 * jnp.sum(jnp.mean(e * e, axis=-1, keepdims=True), axis=0, keepdims=True)
        dy = e * (1.0 / D)
        return _rms_bwd(dy * gfin, xn, r), jnp.broadcast_to(loss, (1, 128)), _colsum(dy * xn)

    return _rowwise("loss_bwd", fn, [x, tgt], [gfin], [(D, F32)], [((1, 128), F32), ((1, D), F32)], tm)


def _wgrad(name, a, b, bm, bk):
    T, M = a.shape
    N = b.shape[1]
    bk = min(bk, T)

    def body(a_ref, b_ref, o_ref):
        @pl.when(pl.program_id(1) == 0)
        def _():
            o_ref[...] = jnp.zeros_like(o_ref)

        o_ref[...] += lax.dot_general(a_ref[...], b_ref[...], (((0,), (0,)), ((), ())), preferred_element_type=F32)

    return pl.pallas_call(
        body, name=name, grid=(M // bm, T // bk),
        in_specs=[pl.BlockSpec((bk, bm), lambda i, k: (k, i)), pl.BlockSpec((bk, N), lambda i, k: (k, 0))],
        out_specs=pl.BlockSpec((bm, N), lambda i, k: (i, 0)),
        out_shape=jax.ShapeDtypeStruct((M, N), F32),
        compiler_params=pltpu.CompilerParams(dimension_semantics=("arbitrary", "arbitrary"), vmem_limit_bytes=VMEM_LIMIT),
    )(a, b)


def _adamw(w, g, m, v):
    m = B1 * m + (1.0 - B1) * g
    v = B2 * v + (1.0 - B2) * (g * g)
    m_hat = m / (1.0 - B1 ** STEP)
    v_hat = v / (1.0 - B2 ** STEP)
    return -LR * (m_hat / (jnp.sqrt(v_hat) + AEPS) + WD * w), m, v


def _adam_call(name, w, g, m, v, tm):
    C = w.shape[1]
    return _rowwise(name, _adamw, [w, g, m, v], [], [(C, F32)] * 3, [], tm)


def _adam_layer(name, g, w, m, v, tm, layer, prev):
    C = g.shape[1]
    return _rowwise(name, lambda g, w, m, v: (g,) + _adamw(w, g, m, v), [g, w, m, v], [], [(C, F32)] * 4, [], tm,
                    layer=layer, prev=prev)


def _reduce_sum(name, own, recv, tm):
    n, R, C = recv.shape
    flat = recv.reshape(n * R, C)
    fn = lambda own, r0, r1, r2: (((own + r0.astype(F32)) + r1.astype(F32)) + r2.astype(F32),)
    return _rowwise(name, fn, [own] + [_cols(flat, C, 0, j * R) for j in range(n)], [], [(C, F32)], [], tm, rows=R)[0]


def _pair_add(name, mine, theirs_recv, tm):
    n, R, C = mine.shape

    def fn(a, b):
        p = a + b
        return p, p

    return _rowwise(name, fn, [mine.reshape(n * R, C), theirs_recv.reshape(n * R, C)], [], [(C, BF16), (C, F32)], [], tm)


def _reduce_adam(name, own, recv, w, m, v, tm, layer, prev):
    n, R, C = recv.shape
    flat = recv.reshape(n * R, C)

    def fn(own, r0, r1, r2, w, m, v):
        g = ((own + r0.astype(F32)) + r1.astype(F32)) + r2.astype(F32)
        return (g,) + _adamw(w, g, m, v)

    return _rowwise(name, fn, [own] + [_cols(flat, C, 0, j * R) for j in range(n)] + [w, m, v], [], [(C, F32)] * 4, [],
                    tm, rows=R, layer=layer, prev=prev)


_OFFSETS = [(dx, dy, dc) for dx in (0, 1) for dy in (0, 1) for dc in (0, 1)][1:]
_MESH = pl.DeviceIdType.MESH


def _coords():
    return lax.axis_index("x"), lax.axis_index("y"), lax.axis_index("c")


def _flip(me, off):
    return tuple((1 - m) if d else m for m, d in zip(me, off))


def _linear(p):
    return 4 * p[0] + 2 * p[1] + p[2]


_CHIP_FLIPS = ((1, 0), (0, 1), (1, 1))


class _Rider:
    def __init__(self, ins, outs, n_remote, n_local, start, finish):
        self.ins, self.outs, self.n_remote, self.n_local, self.start, self.finish = ins, outs, n_remote, n_local, start, finish

    def scratch(self):
        return [pltpu.SemaphoreType.DMA((self.n_remote,)), pltpu.SemaphoreType.DMA((self.n_remote,)),
                pltpu.SemaphoreType.DMA((max(self.n_local, 1),))]


def _run_rider(name, rider):
    def body(*refs):
        n_i, n_o = len(rider.ins), len(rider.outs)
        rider.start(refs[:n_i], refs[n_i:n_i + n_o], *refs[n_i + n_o:])
        rider.finish(refs[:n_i], refs[n_i:n_i + n_o], *refs[n_i + n_o:])

    anyspec = pl.BlockSpec(memory_space=pl.ANY)
    return pl.pallas_call(
        body, name=name, in_specs=[anyspec] * len(rider.ins), out_specs=[anyspec] * len(rider.outs),
        out_shape=list(rider.outs), scratch_shapes=rider.scratch(),
    )(*rider.ins)


def _gather_rider(arrs, layer):
    n = len(arrs)

    def parts(ins, outs, send, recv, loc):
        x, y, c = _coords()
        me, sib = (x, y, c), (x, y, 1 - c)
        chips = [((1 - x) if dx else x, (1 - y) if dy else y) for dx, dy in _CHIP_FLIPS]

        def copy(a, k, block, to, own=False):
            slot = outs[a].at[_linear(block)]
            return pltpu.make_async_remote_copy(src_ref=ins[a].at[layer] if own else slot, dst_ref=slot,
                                                send_sem=send.at[a * 7 + k], recv_sem=recv.at[a * 7 + k],
                                                device_id=to, device_id_type=_MESH)

        local = [pltpu.make_async_copy(ins[a].at[layer], outs[a].at[_linear(me)], loc.at[a]) for a in range(n)]
        first = []
        for a in range(n):
            first.append(copy(a, 0, me, sib, own=True))
            first += [copy(a, 1 + j, me, (*chip, c), own=True) for j, chip in enumerate(chips)]
        return copy, local, first, me, sib, chips, c

    def start(*refs):
        _, local, first, *_ = parts(*refs)
        for cp in local + first:
            cp.start()

    def finish(*refs):
        copy, local, first, me, sib, chips, c = parts(*refs)
        passed = []
        for j, chip in enumerate(chips):
            for a in range(n):
                copy(a, 1 + j, (*chip, c), me).wait_recv()
                cp = copy(a, 4 + j, (*chip, c), sib)
                cp.start()
                passed.append(cp)
        for a in range(n):
            copy(a, 0, sib, me).wait_recv()
            for j, chip in enumerate(chips):
                copy(a, 4 + j, (*chip, 1 - c), me).wait_recv()
        for cp in first + passed:
            cp.wait_send()
        for cp in local:
            cp.wait()

    outs = [jax.ShapeDtypeStruct((N_DEV,) + a.shape[1:], a.dtype) for a in arrs]
    return _Rider(list(arrs), outs, 7 * n, n, start, finish)


def _simple_rider(ins, outs, n_remote, make):
    def start(*refs):
        for cp in make(*refs):
            cp.start()

    def finish(*refs):
        for cp in make(*refs):
            cp.wait()

    return _Rider(ins, outs, n_remote, 0, start, finish)


def _pair_rider(arrs):
    def make(ins, outs, send, recv, loc):
        x, y, c = _coords()
        return [pltpu.make_async_remote_copy(src_ref=ins[a], dst_ref=outs[a], send_sem=send.at[a], recv_sem=recv.at[a],
                                             device_id=(x, y, 1 - c), device_id_type=_MESH) for a in range(len(arrs))]

    return _simple_rider(list(arrs), [jax.ShapeDtypeStruct(a.shape, a.dtype) for a in arrs], len(arrs), make)


def _chip_rider(arrs):
    nf = len(_CHIP_FLIPS)

    def make(ins, outs, send, recv, loc):
        x, y, c = _coords()
        copies = []
        for a in range(len(arrs)):
            for k, (dx, dy) in enumerate(_CHIP_FLIPS):
                px, py = (1 - x) if dx else x, (1 - y) if dy else y
                copies.append(pltpu.make_async_remote_copy(
                    src_ref=ins[a].at[2 * px + py], dst_ref=outs[a].at[k], send_sem=send.at[a * nf + k],
                    recv_sem=recv.at[a * nf + k], device_id=(px, py, c), device_id_type=_MESH))
        return copies

    return _simple_rider(list(arrs), [jax.ShapeDtypeStruct((nf,) + a.shape[1:], a.dtype) for a in arrs], nf * len(arrs), make)


def _small_allgather(name, packed):
    R = packed.shape[0]

    def body(in_ref, all_ref, sum_ref, send, recv):
        me = _coords()
        my = _linear(me)
        all_ref[my] = in_ref[...]
        copies = []
        for k, off in enumerate(_OFFSETS):
            cp = pltpu.make_async_remote_copy(src_ref=in_ref, dst_ref=all_ref.at[my], send_sem=send.at[k], recv_sem=recv.at[k],
                                              device_id=_flip(me, off), device_id_type=_MESH)
            cp.start()
            copies.append(cp)
        for cp in copies:
            cp.wait()
        acc = all_ref[0]
        for j in range(1, N_DEV):
            acc = acc + all_ref[j]
        sum_ref[...] = acc

    vm = pl.BlockSpec(memory_space=pltpu.VMEM)
    return pl.pallas_call(
        body, name=name, in_specs=[vm], out_specs=[vm, vm],
        out_shape=[jax.ShapeDtypeStruct((N_DEV, R, 128), F32), jax.ShapeDtypeStruct((R, 128), F32)],
        scratch_shapes=[pltpu.SemaphoreType.DMA((len(_OFFSETS),)), pltpu.SemaphoreType.DMA((len(_OFFSETS),))],
        compiler_params=pltpu.CompilerParams(vmem_limit_bytes=VMEM_LIMIT),
    )(packed)


def _ada_mod(c, w16, bias):
    nc = w16.shape[2]
    kp = len(_OFFSETS)

    def body(c_ref, w_ref, b_ref, rows_ref, act_ref, cbuf, sbuf, send, recv):
        me = _coords()
        my = _linear(me)
        cbuf[my] = c_ref[...]
        copies = []
        for k, off in enumerate(_OFFSETS):
            cp = pltpu.make_async_remote_copy(src_ref=c_ref, dst_ref=cbuf.at[my], send_sem=send.at[k], recv_sem=recv.at[k],
                                              device_id=_flip(me, off), device_id_type=_MESH)
            cp.start()
            copies.append(cp)
        for cp in copies:
            cp.wait()
        act = _silu(jnp.concatenate([cbuf[j] for j in range(N_DEV)], axis=0))
        act_ref[...] = act
        act16 = act.astype(BF16)
        for l in range(DEPTH):
            ml = jnp.dot(act16, w_ref[l], preferred_element_type=F32) + b_ref[l:l + 1, :]
            for j in range(N_DEV):
                sbuf[j, l:l + 1, :] = ml[j:j + 1, :]
        rows_ref[my] = sbuf[my]
        copies = []
        for k, off in enumerate(_OFFSETS):
            peer = _flip(me, off)
            cp = pltpu.make_async_remote_copy(src_ref=sbuf.at[_linear(peer)], dst_ref=rows_ref.at[my], send_sem=send.at[kp + k],
                                              recv_sem=recv.at[kp + k], device_id=peer, device_id_type=_MESH)
            cp.start()
            copies.append(cp)
        for cp in copies:
            cp.wait()

    vm = pl.BlockSpec(memory_space=pltpu.VMEM)
    return pl.pallas_call(
        body, name="ada_mod", in_specs=[vm, vm, vm], out_specs=[vm, vm],
        out_shape=[jax.ShapeDtypeStruct((N_DEV, DEPTH, nc), F32), jax.ShapeDtypeStruct((N_DEV, D), F32)],
        scratch_shapes=[pltpu.VMEM((N_DEV, 1, D), F32), pltpu.VMEM((N_DEV, DEPTH, nc), F32),
                        pltpu.SemaphoreType.DMA((2 * kp,)), pltpu.SemaphoreType.DMA((2 * kp,))],
        compiler_params=pltpu.CompilerParams(vmem_limit_bytes=VMEM_LIMIT),
    )(c, w16, bias)


def _pack(arrs):
    parts = []
    for a in arrs:
        f = a.reshape(-1).astype(F32)
        parts.append(jnp.pad(f, (0, (-f.shape[0]) % 128)))
    flat = jnp.concatenate(parts)
    flat = jnp.pad(flat, (0, (-flat.shape[0]) % 1024))
    return flat.reshape(-1, 128)


def _unpack(packed, shapes):
    flat = packed.reshape(packed.shape[:-2] + (-1,))
    out, r = [], 0
    for s in shapes:
        n = int(np.prod(s))
        out.append(flat[..., r:r + n].reshape(packed.shape[:-2] + tuple(s)))
        r += -(-n // 128) * 128
    return out


def _pad_rows(w, rows):
    return jnp.pad(w, ((0, 0), (0, rows - w.shape[1]), (0, 0)))


_SMALL = ("b_ada", "norm_mix_g", "norm_ffn_g", "conv_a_w", "conf_dw_w", "conf_dw_b", "conf_ln_g", "conf_ln_b",
          "dn_conv_w", "dn_a_log", "dn_dt_bias", "dn_norm_g", "final_norm_g")
_BIG = ("w_in", "w_out", "w_ffn_in", "w_ffn_out")
_WEIGHTS = ("w_ada", "b_ada", "norm_mix_g", "norm_ffn_g", "w_in", "conv_a_w", "conf_dw_w", "conf_dw_b", "conf_ln_g",
            "conf_ln_b", "dn_conv_w", "dn_a_log", "dn_dt_bias", "dn_norm_g", "w_out", "w_ffn_in", "w_ffn_out",
            "final_norm_g")


def _step(x, c, loss_target, W, M, V):
    T = x.shape[1]
    me = _linear(_coords())
    cc = lax.axis_index("c")
    chip = 2 * lax.axis_index("x") + lax.axis_index("y")
    xs, tgt = x[0], loss_target[0]
    vec = lambda a: a.reshape(1, -1)

    nada = W["w_ada"].shape[2]
    rows, act_all = _ada_mod(c, W["w_ada"].astype(BF16), lax.dynamic_slice(W["b_ada"], (0, me * nada), (DEPTH, nada)))
    mod = rows.transpose(1, 0, 2).reshape(DEPTH, 6, 1, D)

    w16 = {k: W[k].astype(BF16) for k in _BIG}
    w16["w_ffn_in"] = W["w_ffn_in"].transpose(0, 2, 1).astype(BF16)

    def whole(g_in=None, g_out=None, g_fi=None, g_fo=None):
        out = {}
        if g_in is not None:
            out["w_in"] = jnp.pad(g_in.transpose(1, 0, 2).reshape(D, IN_COLS), ((0, 0), (0, IN_PAD - IN_COLS)))
        if g_out is not None:
            out["w_out"] = g_out.reshape(D, D)
        if g_fi is not None:
            out["w_ffn_in"] = g_fi.reshape(2 * D_FF, D)
        if g_fo is not None:
            out["w_ffn_out"] = g_fo.reshape(D_FF, D)
        return out

    wts = [dict() for _ in range(DEPTH)]
    gather = lambda names, layer: _gather_rider([w16[k] for k in names], layer)
    wts[0].update(whole(g_in=_run_rider("gather_weights", gather(["w_in"], 0))[0]))
    conv_names = ("conv_a_w", "conf_dw_w", "dn_conv_w")
    conv_all, _ = _small_allgather("gather_conv_w", _pack([W[k] for k in conv_names]))
    conv_full = [t.transpose(1, 2, 0, 3).reshape(t.shape[1], t.shape[2], -1)
                 for t in _unpack(conv_all, [W[k].shape for k in conv_names])]
    wa, wb, wc = _pad_rows(conv_full[0], 8), _pad_rows(conv_full[1], 32), _pad_rows(conv_full[2], 8)
    lane_pad = lambda a: jnp.pad(a, ((0, 0), (0, 128 - a.shape[1])))
    alog, dtb = lane_pad(W["dn_a_log"]), lane_pad(W["dn_dt_bias"])

    saved = []
    xc = xs
    for l in range(DEPTH):
        more = l + 1 < DEPTH
        sh1, sc1, g1, sh2, sc2, g2 = [mod[l, i] for i in range(6)]
        proj, h, *got = _normproj_fwd("inproj_fwd", xc, sh1, sc1, vec(W["norm_mix_g"][l]), wts[l]["w_in"], 512,
                                      rider=gather(["w_out", "w_ffn_out"], 0) if l == 0 else None)
        if l == 0:
            wts[0].update(whole(g_out=got[0], g_fo=got[1]))
        convout = _conv_fwd(proj, wa[l], wb[l], wc[l], 256)
        y_ab, qkv, gb = _stage2_fwd(proj, convout, vec(W["conf_dw_b"][l]), vec(W["conf_ln_g"][l]), vec(W["conf_ln_b"][l]),
                                    alog[l:l + 1], dtb[l:l + 1], 256)
        o, ss, *got = _delta_fwd(qkv, gb, DELTA_NB,
                                 rider=gather(["w_ffn_in"], 0) if l == 0 else gather(["w_in"], l + 1) if more else None)
        if l == 0:
            wts[0].update(whole(g_fi=got[0]))
        elif more:
            wts[l + 1].update(whole(g_in=got[0]))
        x1, mix, ycat, *got = _outproj_fwd(xc, o, proj, y_ab, g1, vec(W["dn_norm_g"][l]), wts[l]["w_out"], 512,
                                           rider=gather(["w_in"], 1) if l == 0 else None)
        if l == 0:
            wts[1].update(whole(g_in=got[0]))
        gu, h2, *got = _normproj_fwd("ffnin_fwd", x1, sh2, sc2, vec(W["norm_ffn_g"][l]), wts[l]["w_ffn_in"], 256,
                                     rider=gather(["w_ffn_in"], l + 1) if more else None, w_t=True)
        if more:
            wts[l + 1].update(whole(g_fi=got[0]))
        x2, f, *got = _ffn_out_fwd(x1, gu, g2, wts[l]["w_ffn_out"], 256,
                                   rider=gather(["w_ffn_out", "w_out"], l + 1) if more else None)
        if more:
            wts[l + 1].update(whole(g_fo=got[0], g_out=got[1]))
        saved.append((xc, proj, h, convout, qkv, gb, o, ss, mix, ycat, x1, gu, h2, f))
        xc = x2

    dx, loss_row, d_gfin = _loss_bwd(xc, tgt, vec(W["final_norm_g"]), 512)
    loss = lax.psum(loss_row[0, 0], ("x", "y", "c"))

    big_out = {k: None for k in _BIG}
    dmod, small = [None] * DEPTH, [None] * DEPTH
    by_cols = lambda g: g.reshape(D, 4, 2, -1).transpose(2, 1, 0, 3)
    by_rows = lambda g: g.reshape(4, 2, -1, D).transpose(1, 0, 2, 3)

    def halves(blocks):
        return (lax.dynamic_index_in_dim(blocks, cc, 0, keepdims=False),
                lax.dynamic_index_in_dim(blocks, 1 - cc, 0, keepdims=False))

    def pair_sum(k, mine, from_sib):
        p16, p32 = _pair_add("pair_add_" + k, mine, from_sib, 512 if mine.shape[1] % 128 == 0 else mine.shape[1])
        return p16.reshape(mine.shape), lax.dynamic_index_in_dim(p32.reshape(mine.shape), chip, 0, keepdims=False)

    def finish(k, layer, own, r):
        tm = 256 if r.shape[1] % 256 == 0 else r.shape[1]
        if k == "w_ffn_in":
            g = _reduce_sum("reduce_" + k, own, r, tm).T
            big_out[k] = _adam_layer("adam_" + k, g, W[k], M[k], V[k], 256, layer, big_out[k])
        else:
            big_out[k] = _reduce_adam("reduce_adam_" + k, own, r, W[k], M[k], V[k], tm, layer, big_out[k])

    above = None
    for l in reversed(range(DEPTH)):
        xc, proj, h, convout, qkv, gb, o, ss, mix, ycat, x1, gu, h2, f = saved[l]
        sh1, sc1, g1, sh2, sc2, g2 = [mod[l, i] for i in range(6)]
        gm, gf = vec(W["norm_mix_g"][l]), vec(W["norm_ffn_g"][l])
        bb, lg, lb = vec(W["conf_dw_b"][l]), vec(W["conf_ln_g"][l]), vec(W["conf_ln_b"][l])
        dng = vec(W["dn_norm_g"][l])
        wl = wts[l]

        dgu, s, df, d_g2, *got = _ffn_out_bwd(dx, gu, f, g2, wl["w_ffn_out"], 256,
                                              rider=_pair_rider([above[2]]) if above else None)
        if above:
            in16, in_own = pair_sum("w_in", above[1], got[0])
        gw_fo = _wgrad("wgrad_ffn_out", s, df, 1408, 1024)
        dx1, d_sh2, d_sc2, d_gf, *got = _normproj_bwd("ffnin_bwd", x1, dgu, dx, sc2, gf, wl["w_ffn_in"], 256,
                                                      rider=_chip_rider([in16]) if above else None, w_t=True)
        if above:
            finish("w_in", above[0], in_own, got[0])
        gw_fi = _wgrad("wgrad_ffn_in", dgu, h2, 1408, 1024)
        fo_mine, fo_theirs = halves(by_rows(gw_fo))
        fi_mine, fi_theirs = halves(by_rows(gw_fi))
        dmix, dy_ab, do, dz, d_g1, d_dng, *got = _outproj_bwd(dx1, mix, o, proj, g1, dng, wl["w_out"], 512,
                                                               rider=_pair_rider([fo_theirs, fi_theirs]))
        fo16, fo_own = pair_sum("w_ffn_out", fo_mine, got[0])
        fi16, fi_own = pair_sum("w_ffn_in", fi_mine, got[1])
        gw_out = _wgrad("wgrad_out", ycat, dmix, 512, 2048)
        out_mine, out_theirs = halves(by_rows(gw_out))
        dqkv, *got = _delta_bwd(qkv, gb, ss, do, DELTA_NB, rider=_chip_rider([fo16, fi16]))
        finish("w_ffn_out", l, fo_own, got[0])
        finish("w_ffn_in", l, fi_own, got[1])
        dco, da_b, dblk, d_bb, d_lg, d_lb, d_alog, d_dtb, *got = _stage2_bwd(
            proj, convout, dy_ab, _cols(dqkv, D_DN, 0), _cols(dqkv, D_DN, 1), _cols(dqkv, D_DN, 2),
            _cols(dqkv, 128, 3 * D_DN // 128), bb, lg, lb, alog[l:l + 1], dtb[l:l + 1], 256,
            rider=_pair_rider([out_theirs]))
        out16, out_own = pair_sum("w_out", out_mine, got[0])
        dproj, d_wa, d_wb, d_wc, *got = _conv_bwd(proj, dco, da_b, dz, dblk, wa[l], wb[l], wc[l], 256,
                                                  rider=_chip_rider([out16]))
        finish("w_out", l, out_own, got[0])
        dx, d_sh1, d_sc1, d_gm = _normproj_bwd("inproj_bwd", xc, dproj, dx1, sc1, gm, wl["w_in"], 512)
        gw_in = _wgrad("wgrad_in", h, dproj, 512, 1024)
        above = (l,) + halves(by_cols(gw_in[:, :IN_COLS]))

        dmod[l] = jnp.concatenate([d_sh1, d_sc1, d_g1, d_sh2, d_sc2, d_g2], axis=1)
        small[l] = dict(norm_mix_g=d_gm, norm_ffn_g=d_gf, conv_a_w=d_wa[:KA], conf_dw_w=d_wb[:KB], conf_dw_b=d_bb,
                        conf_ln_g=d_lg, conf_ln_b=d_lb, dn_conv_w=d_wc[:KC], dn_a_log=d_alog, dn_dt_bias=d_dtb,
                        dn_norm_g=d_dng)

    in16, in_own = pair_sum("w_in", above[1], _run_rider("pair_exchange", _pair_rider([above[2]]))[0])
    finish("w_in", above[0], in_own, _run_rider("chip_exchange", _chip_rider([in16]))[0])

    names = ("norm_mix_g", "norm_ffn_g", "conv_a_w", "conf_dw_w", "conf_dw_b", "conf_ln_g", "conf_ln_b", "dn_conv_w",
             "dn_a_log", "dn_dt_bias", "dn_norm_g")
    pieces = [jnp.stack(dmod)] + [jnp.stack([small[l][k] for l in range(DEPTH)]) for k in names] + [d_gfin]
    shapes = [p.shape for p in pieces]
    every, total = _small_allgather("gather_small_grads", _pack(pieces))
    tot = dict(zip(("dmod",) + names + ("final_norm_g",), _unpack(total, shapes)))
    dmod_all = _unpack(every, shapes[:1])[0]

    grads = {}
    grads["b_ada"] = tot["dmod"].reshape(DEPTH, 6 * D)
    for k in ("norm_mix_g", "norm_ffn_g", "conf_dw_b", "conf_ln_g", "conf_ln_b", "dn_norm_g"):
        grads[k] = tot[k].reshape(W[k].shape)
    grads["dn_a_log"] = tot["dn_a_log"].reshape(DEPTH, 128)[:, :HEADS]
    grads["dn_dt_bias"] = tot["dn_dt_bias"].reshape(DEPTH, 128)[:, :HEADS]
    grads["final_norm_g"] = tot["final_norm_g"].reshape(D)
    for k in conv_names:
        nloc = W[k].shape[2]
        grads[k] = lax.dynamic_slice_in_dim(tot[k], me * nloc, nloc, axis=2)

    dm = lax.dynamic_slice_in_dim(dmod_all.reshape(N_DEV, DEPTH, 6 * D), me * nada, nada, axis=2)
    pad16 = lambda a: jnp.pad(a, ((0, 16 - N_DEV), (0, 0))).astype(BF16)
    g_ada = _wgrad("wgrad_ada", pad16(act_all), pad16(dm.reshape(N_DEV, DEPTH * nada)), 256, 16)
    grads["w_ada"] = g_ada.reshape(D, DEPTH, nada).transpose(1, 0, 2)

    delta, new_m, new_v = {}, {}, {}
    r2 = lambda a: a.reshape(DEPTH * D, nada)
    d_, m_, v_ = _adam_call("adam_ada", r2(W["w_ada"]), r2(grads["w_ada"]), r2(M["w_ada"]), r2(V["w_ada"]), 512)
    delta["w_ada"], new_m["w_ada"], new_v["w_ada"] = [t.reshape(W["w_ada"].shape) for t in (d_, m_, v_)]
    sshapes = [W[k].shape for k in _SMALL]
    d_, m_, v_ = _adam_call("adam_small", _pack([W[k] for k in _SMALL]), _pack([grads[k] for k in _SMALL]),
                            _pack([M[k] for k in _SMALL]), _pack([V[k] for k in _SMALL]), 4096)
    for dst, packed in ((delta, d_), (new_m, m_), (new_v, v_)):
        dst.update(zip(_SMALL, _unpack(packed, sshapes)))
    for k in _BIG:
        grads[k], delta[k], new_m[k], new_v[k] = big_out[k]

    return (loss, dx[None], *[grads[k] for k in _WEIGHTS], *[delta[k] for k in _WEIGHTS],
            *[new_m[k] for k in _WEIGHTS], *[new_v[k] for k in _WEIGHTS])


def kernel(x, c, w_ada, b_ada, norm_mix_g, norm_ffn_g, w_in, conv_a_w, conf_dw_w, conf_dw_b, conf_ln_g, conf_ln_b, dn_conv_w, dn_a_log, dn_dt_bias, dn_norm_g, w_out, w_ffn_in, w_ffn_out, final_norm_g, loss_target, m_w_ada, m_b_ada, m_norm_mix_g, m_norm_ffn_g, m_w_in, m_conv_a_w, m_conf_dw_w, m_conf_dw_b, m_conf_ln_g, m_conf_ln_b, m_dn_conv_w, m_dn_a_log, m_dn_dt_bias, m_dn_norm_g, m_w_out, m_w_ffn_in, m_w_ffn_out, m_final_norm_g, v_w_ada, v_b_ada, v_norm_mix_g, v_norm_ffn_g, v_w_in, v_conv_a_w, v_conf_dw_w, v_conf_dw_b, v_conf_ln_g, v_conf_ln_b, v_dn_conv_w, v_dn_a_log, v_dn_dt_bias, v_dn_norm_g, v_w_out, v_w_ffn_in, v_w_ffn_out, v_final_norm_g):
    a = dict(locals())
    W = {k: a[k] for k in _WEIGHTS}
    M = {k: a["m_" + k] for k in _WEIGHTS}
    V = {k: a["v_" + k] for k in _WEIGHTS}
    return _step(x, c, loss_target, W, M, V)
```

```python
import functools

import jax
import jax.numpy as jnp
import numpy as np
from jax import lax
from jax.experimental import pallas as pl
from jax.experimental.pallas import tpu as pltpu

F32 = jnp.float32
BF16 = jnp.bfloat16

N_DEV = 8
D = 1024
DEPTH = 4
D_CONV = 256
D_CONF = 256
D_DN = 512
HEADS = 4
HD = 128
KA, KB, KC = 3, 31, 4
CHUNK = 64
D_FF = 2816
IN_COLS = 3336
IN_PAD = 3456
N_CONVCOL = 2048
EPS = 1e-6
LN_EPS = 1e-5
HALO = 32
VMEM_LIMIT = 56 * 1024 * 1024
DELTA_NB = 8

C_AB, C_AC, C_AV, C_BA, C_BG, C_Q, C_Z, C_GB = 0, 256, 512, 768, 1024, 1280, 2816, 3328

LR, B1, B2, AEPS, WD, STEP = 0.001, 0.9, 0.999, 1e-08, 0.01, 10


def _dot(a, b, dims, hi):
    if hi:
        return lax.dot_general(a.astype(F32), b.astype(F32), (dims, ((), ())), precision=lax.Precision.HIGHEST,
                               preferred_element_type=F32)
    return lax.dot_general(a.astype(BF16), b.astype(BF16), (dims, ((), ())), preferred_element_type=F32)


@functools.partial(jax.custom_vjp, nondiff_argnums=(2,))
def mm_nn(a, b, hi=False):
    return _dot(a, b, ((1,), (0,)), hi)


@functools.partial(jax.custom_vjp, nondiff_argnums=(2,))
def mm_nt(a, b, hi=False):
    return _dot(a, b, ((1,), (1,)), hi)


@functools.partial(jax.custom_vjp, nondiff_argnums=(2,))
def mm_tn(a, b, hi=False):
    return _dot(a, b, ((0,), (0,)), hi)


mm_nn.defvjp(lambda a, b, hi: (mm_nn(a, b, hi), (a, b)),
             lambda hi, r, g: (mm_nt(g, r[1], hi), mm_tn(r[0], g, hi)))
mm_nt.defvjp(lambda a, b, hi: (mm_nt(a, b, hi), (a, b)),
             lambda hi, r, g: (mm_nn(g, r[1], hi), mm_tn(g, r[0], hi)))
mm_tn.defvjp(lambda a, b, hi: (mm_tn(a, b, hi), (a, b)),
             lambda hi, r, g: (mm_nt(r[1], g, hi), mm_nn(r[0], g, hi)))


def _sigmoid(x):
    return 1.0 / (1.0 + jnp.exp(-x))


def _silu(x):
    return x * _sigmoid(x)


def _softplus(x):
    return jnp.maximum(x, 0.0) + jnp.log(1.0 + jnp.exp(-jnp.abs(x)))


def _iota2(shape, dim):
    return lax.broadcasted_iota(jnp.int32, shape, dim)


def _dot16(a, b):
    return jnp.dot(a.astype(BF16), b.astype(BF16), preferred_element_type=F32)


def _dot_3pass(a, b):
    ah = a.astype(BF16)
    bh = b.astype(BF16)
    al = (a - ah.astype(F32)).astype(BF16)
    bl = (b - bh.astype(F32)).astype(BF16)
    d = lambda x, y: jnp.dot(x, y, preferred_element_type=F32)
    return d(ah, bh) + (d(ah, bl) + d(al, bh))


@jax.custom_vjp
def _unit_lower_inverses(Xs):
    n = Xs[0].shape[0]
    r, c = _iota2((n, n), 0), _iota2((n, n), 1)
    eye = (r == c).astype(F32)

    def joins(b):
        s = b.bit_length() - 1
        return ((r >> (s + 1)) == (c >> (s + 1))) & (((r >> s) & 1) == 1) & (((c >> s) & 1) == 0)

    Ts = [eye + jnp.where(joins(1), x, 0.0) for x in Xs]
    b = 2
    while b < n:
        m = joins(b)
        Ys = [_dot16(jnp.where(m, x, 0.0), t) for x, t in zip(Xs, Ts)]
        Ts = [t + _dot16(t, y) for t, y in zip(Ts, Ys)]
        b *= 2
    Rs = [(eye - t) + _dot_3pass(x, t) for x, t in zip(Xs, Ts)]
    return [t + _dot16(t, r_) for t, r_ in zip(Ts, Rs)]


def _unit_lower_inverses_fwd(Xs):
    Ts = _unit_lower_inverses(Xs)
    return Ts, Ts


def _unit_lower_inverses_bwd(Ts, gs):
    inner = [mm_nt(g, t) for g, t in zip(gs, Ts)]
    return ([mm_tn(t, i) for t, i in zip(Ts, inner)],)


_unit_lower_inverses.defvjp(_unit_lower_inverses_fwd, _unit_lower_inverses_bwd)

def _delta_chunks(qs, ks, vs, gbs, Ss):
    C = CHUNK
    nb = len(gbs)
    pairs = [(c, h) for c in range(nb) for h in range(HEADS)]
    each = lambda fn, *lists: [fn(*a) for a in zip(*lists)]
    row = _iota2((C, C), 0)
    col = _iota2((C, C), 1)
    causal = row >= col
    strict = row > col
    tri = causal.astype(F32)
    eye = (row == col).astype(F32)
    lane = _iota2((C, 128), 1)
    subl = _iota2((128, C), 0)
    last = (_iota2((C, 1), 0) == C - 1).astype(F32)

    gc_all = [mm_nn(tri, gb, True) for gb in gbs]
    gc_t = [g.T for g in gc_all]
    q = [qs[c][h] * (HD ** -0.5) for c, h in pairs]
    k = [ks[c][h] for c, h in pairs]
    v = [vs[c][h] for c, h in pairs]
    gcol = [jnp.sum(jnp.where(lane == h, gc_all[c], 0.0), axis=1, keepdims=True) for c, h in pairs]
    grow = [jnp.sum(jnp.where(subl == h, gc_t[c], 0.0), axis=0, keepdims=True) for c, h in pairs]
    beta = [jnp.sum(jnp.where(lane == HEADS + h, gbs[c], 0.0), axis=1, keepdims=True) for c, h in pairs]
    decay = each(lambda a, b: jnp.where(causal, jnp.exp(jnp.where(causal, a - b, 0.0)), 0.0), gcol, grow)
    kb = each(lambda a, b: a * b, k, beta)
    vb = each(lambda a, b: a * b, v, beta)
    kk = each(lambda a, b: mm_nt(a, b), kb, k)
    X = each(lambda a, d: -jnp.where(strict, a * d, 0.0), kk, decay)
    T = _unit_lower_inverses(X)
    eg = [jnp.exp(g) for g in gcol]
    u = each(lambda t, a: mm_nn(t, a), T, vb)
    w = each(lambda t, a, e: mm_nn(t, a * e), T, kb, eg)
    qk = each(lambda a, b, d: jnp.where(causal, mm_nt(a, b) * d, 0.0), q, k, decay)
    qg = each(lambda a, e: a * e, q, eg)
    g_last = [jnp.sum(g * last, axis=0, keepdims=True) for g in gcol]
    kd = each(lambda a, gl, g: a * jnp.exp(gl - g), k, g_last, gcol)
    eg_last = [jnp.exp(g) for g in g_last]

    outs = []
    for c in range(nb):
        sl = slice(c * HEADS, (c + 1) * HEADS)
        v_new = each(lambda a, b, S: a - mm_nn(b, S), u[sl], w[sl], Ss)
        oS = each(lambda a, S: mm_nn(a, S), qg[sl], Ss)
        outs.append(each(lambda a, b, n: a + mm_nn(b, n), oS, qk[sl], v_new))
        Ss = each(lambda S, e, a, n: S * e + mm_tn(a, n), Ss, eg_last[sl], kd[sl], v_new)
    return outs, Ss


def _split_chunks(ref, nb):
    return [[ref[c * CHUNK:(c + 1) * CHUNK, h * HD:(h + 1) * HD] for h in range(HEADS)] for c in range(nb)]


def _join_chunks(vals):
    return jnp.concatenate([jnp.concatenate(heads, axis=1) for heads in vals], axis=0)


def _hosted(body, n_in, n_out, rider, n_steps):
    if rider is None:
        return body
    n_ri, n_ro = len(rider.ins), len(rider.outs)

    def wrapped(*refs):
        ins, r_ins = refs[:n_in], refs[n_in:n_in + n_ri]
        outs = refs[n_in + n_ri:n_in + n_ri + n_out]
        r_outs = refs[n_in + n_ri + n_out:n_in + n_ri + n_out + n_ro]
        rest = refs[n_in + n_ri + n_out + n_ro:]
        scr, sems = rest[:len(rest) - 3], rest[len(rest) - 3:]

        @pl.when(pl.program_id(0) == 0)
        def _():
            rider.start(r_ins, r_outs, *sems)

        body(*ins, *outs, *scr)

        @pl.when(pl.program_id(0) == n_steps - 1)
        def _():
            rider.finish(r_ins, r_outs, *sems)

    return wrapped


def _host_call(body, name, grid, in_specs, out_specs, out_shape, scratch, operands, rider):
    out_specs, out_shape = list(out_specs), list(out_shape)
    n_in, n_out = len(in_specs), len(out_specs)
    if rider is not None:
        anyspec = pl.BlockSpec(memory_space=pl.ANY)
        in_specs = list(in_specs) + [anyspec] * len(rider.ins)
        out_specs += [anyspec] * len(rider.outs)
        out_shape += list(rider.outs)
        scratch = list(scratch) + rider.scratch()
        operands = list(operands) + list(rider.ins)
    return pl.pallas_call(
        _hosted(body, n_in, n_out, rider, grid[0]), name=name, grid=grid, in_specs=in_specs, out_specs=out_specs,
        out_shape=out_shape, scratch_shapes=scratch,
        compiler_params=pltpu.CompilerParams(dimension_semantics=("arbitrary",), vmem_limit_bytes=VMEM_LIMIT),
    )(*operands)


def _delta_fwd(qkv, gb, nb, rider=None):
    T = qkv.shape[0]
    nb = min(nb, T // CHUNK)
    rows = nb * CHUNK
    n = T // rows

    def body(q_ref, k_ref, v_ref, gb_ref, o_ref, ss_ref, s_scr):
        @pl.when(pl.program_id(0) == 0)
        def _():
            s_scr[...] = jnp.zeros_like(s_scr)

        Ss = [s_scr[h] for h in range(HEADS)]
        for h in range(HEADS):
            ss_ref[0, h] = Ss[h]
        gbs = [gb_ref[c * CHUNK:(c + 1) * CHUNK, :] for c in range(nb)]
        outs, new_S = _delta_chunks(_split_chunks(q_ref, nb), _split_chunks(k_ref, nb), _split_chunks(v_ref, nb), gbs, Ss)
        o_ref[...] = _join_chunks(outs)
        s_scr[...] = jnp.stack(new_S)

    row = lambda w, j=0: pl.BlockSpec((rows, w), lambda i: (i, j))
    return _host_call(
        body, "delta_fwd", (n,),
        [row(D_DN, 0), row(D_DN, 1), row(D_DN, 2), row(128)],
        [row(D_DN), pl.BlockSpec((1, HEADS, HD, HD), lambda i: (i, 0, 0, 0))],
        [jax.ShapeDtypeStruct((T, D_DN), F32), jax.ShapeDtypeStruct((n, HEADS, HD, HD), F32)],
        [pltpu.VMEM((HEADS, HD, HD), F32)], [qkv, qkv, qkv, gb], rider)


def _delta_bwd(qkv, gb, ss, do, nb, rider=None):
    T = qkv.shape[0]
    nb = min(nb, T // CHUNK)
    rows = nb * CHUNK
    n = T // rows

    def body(q_ref, k_ref, v_ref, gb_ref, ss_ref, do_ref, d_ref, ds_scr):
        @pl.when(pl.program_id(0) == 0)
        def _():
            ds_scr[...] = jnp.zeros_like(ds_scr)

        Ss = [ss_ref[0, h] for h in range(HEADS)]
        gbs = [gb_ref[c * CHUNK:(c + 1) * CHUNK, :] for c in range(nb)]
        _, vjp = jax.vjp(_delta_chunks, _split_chunks(q_ref, nb), _split_chunks(k_ref, nb), _split_chunks(v_ref, nb), gbs, Ss)
        dqs, dks, dvs, dgbs, dSs = vjp((_split_chunks(do_ref, nb), [ds_scr[h] for h in range(HEADS)]))
        d_ref[...] = jnp.concatenate([_join_chunks(dqs), _join_chunks(dks), _join_chunks(dvs),
                                      jnp.concatenate(dgbs, axis=0)], axis=1)
        ds_scr[...] = jnp.stack(dSs)

    row = lambda w, j=0: pl.BlockSpec((rows, w), lambda i: (n - 1 - i, j))
    return _host_call(
        body, "delta_bwd", (n,),
        [row(D_DN, 0), row(D_DN, 1), row(D_DN, 2), row(128),
         pl.BlockSpec((1, HEADS, HD, HD), lambda i: (n - 1 - i, 0, 0, 0)), row(D_DN)],
        [row(3 * D_DN + 128)], [jax.ShapeDtypeStruct((T, 3 * D_DN + 128), F32)],
        [pltpu.VMEM((HEADS, HD, HD), F32)], [qkv, qkv, qkv, gb, ss, do], rider)


def _cols(arr, width, index, first_row=0, block_row=None):
    return (arr, width, index, first_row, block_row)


def _rowwise(name, fn, tiled, consts, out_tiled, out_acc, tm, rows=None, rider=None, layer=None, prev=None):
    tiled = [t if isinstance(t, tuple) else (t, t.shape[-1], 0, 0, None) for t in tiled]
    T = tiled[0][0].shape[-2] if rows is None else rows
    tm = min(tm, T)
    assert T % tm == 0 and all(t[3] % tm == 0 for t in tiled)
    n_t, n_c, n_o, n_a = len(tiled), len(consts), len(out_tiled), len(out_acc)

    n_ri = len(rider.ins) if rider else 0
    n_ro = len(rider.outs) if rider else 0
    n_steps = T // tm

    def body(*refs):
        n_in = n_t + n_c + n_ri + (n_o if layer is not None else 0)
        r_ins = refs[n_t + n_c:n_t + n_c + n_ri]
        o_refs = refs[n_in:n_in + n_o]
        a_refs = refs[n_in + n_o:n_in + n_o + n_a]
        r_outs = refs[n_in + n_o + n_a:n_in + n_o + n_a + n_ro]
        sems = refs[n_in + n_o + n_a + n_ro:]
        if rider:
            @pl.when(pl.program_id(0) == 0)
            def _():
                rider.start(r_ins, r_outs, *sems)

        ins = [r[...] for r in refs[:n_t + n_c]]
        outs = fn(*ins)
        for r, val in zip(o_refs, outs[:n_o]):
            r[...] = val.astype(r.dtype)
        if n_a:
            @pl.when(pl.program_id(0) == 0)
            def _():
                for r in a_refs:
                    r[...] = jnp.zeros_like(r)
            for r, val in zip(a_refs, outs[n_o:]):
                r[...] += val
        if rider:
            @pl.when(pl.program_id(0) == n_steps - 1)
            def _():
                rider.finish(r_ins, r_outs, *sems)

    def const_spec(a):
        nd = a.ndim
        return pl.BlockSpec(a.shape, lambda i: (0,) * nd, pipeline_mode=pl.Buffered(1))

    def tile_spec(arr, w, j, r0, block_row):
        row = block_row if block_row is not None else (lambda i: i + r0 // tm)
        if arr.ndim == 3:
            return pl.BlockSpec((None, tm, w), lambda i: (layer, row(i), j))
        return pl.BlockSpec((tm, w), lambda i: (row(i), j))

    in_specs = [tile_spec(*t) for t in tiled]
    in_specs += [const_spec(a) for a in consts]
    if layer is None:
        out_specs = [pl.BlockSpec((tm, w), lambda i: (i, 0)) for (w, _) in out_tiled]
        out_shape = [jax.ShapeDtypeStruct((T, w), dt) for (w, dt) in out_tiled]
    else:
        out_specs = [pl.BlockSpec((None, tm, w), lambda i: (layer, i, 0)) for (w, _) in out_tiled]
        out_shape = [jax.ShapeDtypeStruct((DEPTH, T, w), dt) for (w, dt) in out_tiled]
    out_specs += [pl.BlockSpec(s, lambda i: (0, 0)) for (s, _) in out_acc]
    out_shape += [jax.ShapeDtypeStruct(s, dt) for (s, dt) in out_acc]
    operands = [t[0] for t in tiled] + list(consts)
    scratch = []
    aliases = {}
    if rider:
        anyspec = pl.BlockSpec(memory_space=pl.ANY)
        in_specs += [anyspec] * n_ri
        out_specs += [anyspec] * n_ro
        out_shape += list(rider.outs)
        operands += list(rider.ins)
        scratch = rider.scratch()
    n_prev = 0
    if layer is not None:
        assert rider is None and not out_acc
        if prev is None:
            prev = [jnp.zeros(o.shape, o.dtype) for o in out_shape]
        n_prev = len(prev)
        aliases = {len(operands) + i: i for i in range(n_prev)}
        in_specs += [pl.BlockSpec(memory_space=pl.ANY)] * n_prev
        operands += list(prev)
    return pl.pallas_call(
        body, name=name, grid=(n_steps,), in_specs=in_specs, out_specs=out_specs, out_shape=out_shape, scratch_shapes=scratch,
        input_output_aliases=aliases,
        compiler_params=pltpu.CompilerParams(dimension_semantics=("arbitrary",), vmem_limit_bytes=VMEM_LIMIT),
    )(*operands)


def _colsum(x):
    return jnp.sum(x, axis=0, keepdims=True)


def _rms(x):
    r = lax.rsqrt(jnp.mean(x * x, axis=-1, keepdims=True) + EPS)
    return x * r, r


def _rms_bwd(dxn, xn, r):
    return r * (dxn - xn * jnp.mean(dxn * xn, axis=-1, keepdims=True))


def _normproj_fwd(name, x, sh, sc, g, w, tm, rider=None, w_t=False):
    def fn(x, sh, sc, g, w):
        xn, _ = _rms(x)
        h = (xn * (g * (1.0 + sc)) + sh).astype(BF16)
        return lax.dot_general(h, w, (((1,), (1 if w_t else 0,)), ((), ())), preferred_element_type=F32), h

    return _rowwise(name, fn, [x], [sh, sc, g, w], [(w.shape[0 if w_t else 1], F32), (D, BF16)], [], tm, rider=rider)


def _normproj_bwd(name, x, dpre, dres, sc, g, w, tm, rider=None, w_t=False):
    def fn(x, dpre, dres, sc, g, w):
        xn, r = _rms(x)
        dh = lax.dot_general(dpre, w, (((1,), (0 if w_t else 1,)), ((), ())), preferred_element_type=F32)
        da = _colsum(dh * xn)
        dx = _rms_bwd(dh * (g * (1.0 + sc)), xn, r) + dres
        return dx, _colsum(dh), da * g, da * (1.0 + sc)

    vec = ((1, D), F32)
    return _rowwise(name, fn, [x, dpre, dres], [sc, g, w], [(D, F32)], [vec, vec, vec], tm, rider=rider)


def _stage2(a_b, blk, cp, u1c, qp, kp, vp, bb, lg, lb, alog, dtb):
    y_a = a_b * cp
    u1 = u1c + bb
    mu = jnp.mean(u1, axis=-1, keepdims=True)
    uc = u1 - mu
    var = jnp.mean(uc * uc, axis=-1, keepdims=True)
    y_b = _silu(uc * lax.rsqrt(var + LN_EPS) * lg + lb)

    def l2(t):
        t = _silu(t)
        return t * lax.rsqrt(jnp.sum(t * t, axis=-1, keepdims=True) + EPS)

    q = [l2(t) for t in qp]
    k = [l2(t) for t in kp]
    v = _silu(vp)
    lane = _iota2(blk.shape, 1)
    gdec = -jnp.exp(alog) * _softplus(blk + dtb)
    gb = jnp.where(lane < HEADS, gdec, jnp.where(lane < 2 * HEADS, _sigmoid(blk), 0.0))
    return y_a, y_b, q, k, v, gb


def _heads_of(x, base=0):
    return [x[:, base + h * HD:base + (h + 1) * HD] for h in range(HEADS)]


def _stage2_fwd(proj, convout, bb, lg, lb, alog, dtb, tm):
    def fn(a_b, blk, co, bb, lg, lb, alog, dtb):
        y_a, y_b, q, k, v, gb = _stage2(a_b, blk, co[:, 0:256], co[:, 256:512], _heads_of(co, 512), _heads_of(co, 1024),
                                        co[:, 1536:2048], bb, lg, lb, alog, dtb)
        return jnp.concatenate([y_a, y_b], axis=1), jnp.concatenate(q + k + [v], axis=1), gb

    return _rowwise("stage2_fwd", fn, [_cols(proj, 256, 0), _cols(proj, 128, C_GB // 128), convout],
                    [bb, lg, lb, alog, dtb], [(512, BF16), (1536, F32), (128, F32)], [], tm)


def _stage2_bwd(proj, convout, dy_ab, dq, dk, dv, dgb, bb, lg, lb, alog, dtb, tm, rider=None):
    def fn(a_b, blk, co, dy_ab, dq, dk, dv, dgb, bb, lg, lb, alog, dtb):
        args = (a_b, blk, co[:, 0:256], co[:, 256:512], _heads_of(co, 512), _heads_of(co, 1024), co[:, 1536:2048],
                bb, lg, lb, alog, dtb)
        _, vjp = jax.vjp(_stage2, *args)
        ct = (dy_ab[:, 0:256], dy_ab[:, 256:512], _heads_of(dq), _heads_of(dk), dv, dgb)
        da_b, dblk, dcp, du1c, dqp, dkp, dvp, dbb, dlg, dlb, dalog, ddtb = vjp(ct)
        dco = jnp.concatenate([dcp, du1c] + dqp + dkp + [dvp], axis=1)
        return dco, da_b, dblk, dbb, dlg, dlb, dalog, ddtb

    v256, v128 = ((1, 256), F32), ((1, 128), F32)
    return _rowwise("stage2_bwd", fn,
                    [_cols(proj, 256, 0), _cols(proj, 128, C_GB // 128), convout, dy_ab, dq, dk, dv, dgb],
                    [bb, lg, lb, alog, dtb], [(N_CONVCOL, F32), (256, F32), (128, F32)],
                    [v256, v256, v256, v128, v128], tm, rider=rider)


def _stage3(o, z, dng):
    ys = []
    for oh, zh in zip(o, z):
        on = oh * lax.rsqrt(jnp.mean(oh * oh, axis=-1, keepdims=True) + EPS)
        ys.append(on * dng * _silu(zh))
    return ys


def _outproj_fwd(x, o, proj, y_ab, g1, dng, wout, tm, rider=None):
    def fn(x, o, z0, z1, z2, z3, y_ab, g1, dng, wout):
        y_c = _stage3(_heads_of(o), [z0, z1, z2, z3], dng)
        ycat = jnp.concatenate([y_ab] + [t.astype(BF16) for t in y_c], axis=1)
        mix = jnp.dot(ycat, wout, preferred_element_type=F32)
        return x + g1 * mix, mix, ycat

    return _rowwise("outproj_fwd", fn, [x, o] + _z_heads(proj) + [y_ab],
                    [g1, dng, wout], [(D, F32), (D, F32), (D, BF16)], [], tm, rider=rider)


def _z_heads(proj):
    return [_cols(proj, HD, C_Z // HD + h) for h in range(HEADS)]


def _outproj_bwd(dx1, mix, o, proj, g1, dng, wout, tm, rider=None):
    def fn(dx1, mix, o, z0, z1, z2, z3, g1, dng, wout):
        dmix = (dx1 * g1).astype(BF16)
        dycat = lax.dot_general(dmix, wout, (((1,), (1,)), ((), ())), preferred_element_type=F32)
        _, vjp = jax.vjp(_stage3, _heads_of(o), [z0, z1, z2, z3], dng)
        do, dz, ddng = vjp(_heads_of(dycat, 512))
        return (dmix, dycat[:, 0:512], jnp.concatenate(do, axis=1), jnp.concatenate(dz, axis=1),
                _colsum(dx1 * mix), ddng)

    return _rowwise("outproj_bwd", fn, [dx1, mix, o] + _z_heads(proj), [g1, dng, wout],
                    [(D, BF16), (512, F32), (512, F32), (512, F32)], [((1, D), F32), ((1, HD), F32)], tm, rider=rider)


_CONV_BLOCKS = ((0, 256, KA), (256, 512, KB), (512, 2048, KC))
_CONV_STRIP = 256


_CONV_ROWS = 32


def _conv_inputs(proj_ref, rows):
    a_c, a_v = proj_ref[rows, C_AC:C_AC + 256], proj_ref[rows, C_AV:C_AV + 256]
    b_a, b_g = proj_ref[rows, C_BA:C_BA + 256], proj_ref[rows, C_BG:C_BG + 256]
    return a_c, a_v, b_a, _sigmoid(b_g)


def _conv_fwd(proj, wa, wb, wc, tm):
    T = proj.shape[0]
    tm = min(tm, T)

    def body(proj_ref, wa_ref, wb_ref, wc_ref, out_ref, ext):
        @pl.when(pl.program_id(0) == 0)
        def _():
            ext[0:HALO, :] = jnp.zeros((HALO, N_CONVCOL), F32)

        a_c, a_v, b_a, sg = _conv_inputs(proj_ref, slice(None))
        ext[HALO:HALO + tm, 0:256] = a_c * a_v
        ext[HALO:HALO + tm, 256:512] = b_a * sg
        ext[HALO:HALO + tm, 512:2048] = proj_ref[:, C_Q:C_Q + 1536]
        for r0 in range(0, tm, _CONV_ROWS):
            for (c0, c1, kw), w_ref in zip(_CONV_BLOCKS, (wa_ref, wb_ref, wc_ref)):
                for s0 in range(c0, c1, _CONV_STRIP):
                    acc = jnp.zeros((_CONV_ROWS, _CONV_STRIP), F32)
                    for k in range(kw):
                        acc += (w_ref[k:k + 1, s0 - c0:s0 - c0 + _CONV_STRIP]
                                * ext[pl.ds(r0 + HALO - (kw - 1) + k, _CONV_ROWS), s0:s0 + _CONV_STRIP])
                    out_ref[r0:r0 + _CONV_ROWS, s0:s0 + _CONV_STRIP] = acc
        ext[0:HALO, :] = ext[tm:tm + HALO, :]

    full = lambda a: pl.BlockSpec(a.shape, lambda i: (0, 0))
    return pl.pallas_call(
        body, name="conv_fwd", grid=(T // tm,),
        in_specs=[pl.BlockSpec((tm, IN_PAD), lambda i: (i, 0)), full(wa), full(wb), full(wc)],
        out_specs=pl.BlockSpec((tm, N_CONVCOL), lambda i: (i, 0)),
        out_shape=jax.ShapeDtypeStruct((T, N_CONVCOL), F32),
        scratch_shapes=[pltpu.VMEM((HALO + tm, N_CONVCOL), F32)],
        compiler_params=pltpu.CompilerParams(dimension_semantics=("arbitrary",), vmem_limit_bytes=VMEM_LIMIT),
    )(proj, wa, wb, wc)


def _conv_bwd(proj, dco, da_b, dz, dblk, wa, wb, wc, tm, rider=None):
    T = proj.shape[0]
    tm = min(tm, T)
    n = T // tm

    def body(proj_ref, dco_ref, dab_ref, dz_ref, dblk_ref, wa_ref, wb_ref, wc_ref,
             dproj_ref, dwa_ref, dwb_ref, dwc_ref, ext, acc_a, acc_b, acc_c):
        @pl.when(pl.program_id(0) == 0)
        def _():
            ext[tm:tm + HALO, :] = jnp.zeros((HALO, N_CONVCOL), F32)
            acc_a[...] = jnp.zeros_like(acc_a)
            acc_b[...] = jnp.zeros_like(acc_b)
            acc_c[...] = jnp.zeros_like(acc_c)

        ext[0:tm, :] = dco_ref[...]

        def taps(w_ref, acc_ref, kw, c0, wc0, xin, r0):
            dx = jnp.zeros((_CONV_ROWS, _CONV_STRIP), F32)
            for k in range(kw):
                sh = ext[pl.ds(r0 + kw - 1 - k, _CONV_ROWS), c0:c0 + _CONV_STRIP]
                dx += w_ref[k:k + 1, wc0:wc0 + _CONV_STRIP] * sh
                pr = sh * xin
                part = pr[0:8]
                for g in range(8, _CONV_ROWS, 8):
                    part += pr[g:g + 8]
                acc_ref[8 * k:8 * k + 8, wc0:wc0 + _CONV_STRIP] += part
            return dx

        for r0 in range(0, tm, _CONV_ROWS):
            rows = slice(r0, r0 + _CONV_ROWS)
            a_c, a_v, b_a, sg = _conv_inputs(proj_ref, rows)
            dp = taps(wa_ref, acc_a, KA, 0, 0, a_c * a_v, r0)
            dproj_ref[rows, C_AC:C_AC + 256] = (dp * a_v).astype(BF16)
            dproj_ref[rows, C_AV:C_AV + 256] = (dp * a_c).astype(BF16)
            du0 = taps(wb_ref, acc_b, KB, 256, 0, b_a * sg, r0)
            dproj_ref[rows, C_BA:C_BA + 256] = (du0 * sg).astype(BF16)
            dproj_ref[rows, C_BG:C_BG + 256] = (du0 * b_a * sg * (1.0 - sg)).astype(BF16)
            for s0 in range(0, 1536, _CONV_STRIP):
                dq = taps(wc_ref, acc_c, KC, 512 + s0, s0, proj_ref[rows, C_Q + s0:C_Q + s0 + _CONV_STRIP], r0)
                dproj_ref[rows, C_Q + s0:C_Q + s0 + _CONV_STRIP] = dq.astype(BF16)
        ext[tm:tm + HALO, :] = ext[0:HALO, :]

        dproj_ref[:, C_AB:C_AB + 256] = dab_ref[...].astype(BF16)
        dproj_ref[:, C_Z:C_Z + 512] = dz_ref[...].astype(BF16)
        dproj_ref[:, C_GB:C_GB + 128] = dblk_ref[...].astype(BF16)

        @pl.when(pl.program_id(0) == n - 1)
        def _():
            for acc_ref, dw_ref, kw in ((acc_a, dwa_ref, KA), (acc_b, dwb_ref, KB), (acc_c, dwc_ref, KC)):
                dw_ref[...] = jnp.zeros_like(dw_ref)
                for k in range(kw):
                    dw_ref[k:k + 1, :] = _colsum(acc_ref[8 * k:8 * k + 8, :])

    rev = lambda w: pl.BlockSpec((tm, w), lambda i: (n - 1 - i, 0))
    full = lambda a: pl.BlockSpec(a.shape, lambda i: (0, 0))
    return _host_call(
        body, "conv_bwd", (n,),
        [rev(IN_PAD), rev(N_CONVCOL), rev(256), rev(512), rev(128), full(wa), full(wb), full(wc)],
        [rev(IN_PAD), full(wa), full(wb), full(wc)],
        [jax.ShapeDtypeStruct((T, IN_PAD), BF16), jax.ShapeDtypeStruct(wa.shape, F32),
         jax.ShapeDtypeStruct(wb.shape, F32), jax.ShapeDtypeStruct(wc.shape, F32)],
        [pltpu.VMEM((tm + HALO, N_CONVCOL), F32), pltpu.VMEM((8 * KA, 256), F32),
         pltpu.VMEM((8 * KB, 256), F32), pltpu.VMEM((8 * KC, 1536), F32)],
        [proj, dco, da_b, dz, dblk, wa, wb, wc], rider)


def _ffn_out_fwd(x1, gu, g2, wfo, tm, rider=None):
    def fn(x1, gu, g2, wfo):
        s = (_silu(gu[:, :D_FF]) * gu[:, D_FF:]).astype(BF16)
        f = jnp.dot(s, wfo, preferred_element_type=F32)
        return x1 + g2 * f, f

    return _rowwise("ffnout_fwd", fn, [x1, gu], [g2, wfo], [(D, F32), (D, F32)], [], tm, rider=rider)


def _ffn_out_bwd(dx2, gu, f, g2, wfo, tm, rider=None):
    def fn(dx2, gu, f, g2, wfo):
        gate, up = gu[:, :D_FF], gu[:, D_FF:]
        sg = _sigmoid(gate)
        sl = gate * sg
        df = (dx2 * g2).astype(BF16)
        ds = lax.dot_general(df, wfo, (((1,), (1,)), ((), ())), preferred_element_type=F32)
        dgate = ds * up * (sg * (1.0 + gate * (1.0 - sg)))
        dgu = jnp.concatenate([dgate.astype(BF16), (ds * sl).astype(BF16)], axis=1)
        return dgu, sl * up, df, _colsum(dx2 * f)

    return _rowwise("ffnout_bwd", fn, [dx2, gu, f], [g2, wfo], [(2 * D_FF, BF16), (D_FF, BF16), (D, BF16)],
                    [((1, D), F32)], tm, rider=rider)


def _loss_bwd(x, tgt, gfin, tm):
    def fn(x, tgt, gfin):
        xn, r = _rms(x)
        e = xn * gfin - tgt
        loss = 0.5 * jnp.sum(jnp.mean(e * e, axis=-1, keepdims=True), axis=0, keepdims=True)
        dy = e * (1.0 / D)
        return _rms_bwd(dy * gfin, xn, r), jnp.broadcast_to(loss, (1, 128)), _colsum(dy * xn)

    return _rowwise("loss_bwd", fn, [x, tgt], [gfin], [(D, F32)], [((1, 128), F32), ((1, D), F32)], tm)


def _wgrad(name, a, b, bm, bk):
    T, M = a.shape
    N = b.shape[1]
    bk = min(bk, T)

    def body(a_ref, b_ref, o_ref):
        @pl.when(pl.program_id(1) == 0)
        def _():
            o_ref[...] = jnp.zeros_like(o_ref)

        o_ref[...] += lax.dot_general(a_ref[...], b_ref[...], (((0,), (0,)), ((), ())), preferred_element_type=F32)

    return pl.pallas_call(
        body, name=name, grid=(M // bm, T // bk),
        in_specs=[pl.BlockSpec((bk, bm), lambda i, k: (k, i)), pl.BlockSpec((bk, N), lambda i, k: (k, 0))],
        out_specs=pl.BlockSpec((bm, N), lambda i, k: (i, 0)),
        out_shape=jax.ShapeDtypeStruct((M, N), F32),
        compiler_params=pltpu.CompilerParams(dimension_semantics=("arbitrary", "arbitrary"), vmem_limit_bytes=VMEM_LIMIT),
    )(a, b)


IN_BLOCK = 512


def _wgrad_in_blocks(name, a, b, bm, bk):
    T, M = a.shape
    N = b.shape[1]
    bk = min(bk, T)
    nk = T // bk
    per = IN_COLS // N_DEV
    win = IN_BLOCK + 128

    def body(a_ref, b_ref, o_ref, acc):
        @pl.when(pl.program_id(1) == 0)
        def _():
            acc[...] = jnp.zeros_like(acc)

        acc[...] += lax.dot_general(a_ref[...], b_ref[...], (((0,), (0,)), ((), ())), preferred_element_type=F32)

        @pl.when(pl.program_id(1) == nk - 1)
        def _():
            for j in range(N_DEV):
                q, r = divmod(per * j, 128)
                w = acc[:, 128 * q:128 * q + win]
                if r:
                    w = pltpu.roll(w, win - r, axis=1)
                o_ref[j] = w[:, :IN_BLOCK]

    assert 128 * ((per * (N_DEV - 1)) // 128) + win <= N
    return pl.pallas_call(
        body, name=name, grid=(M // bm, nk),
        in_specs=[pl.BlockSpec((bk, bm), lambda i, k: (k, i)), pl.BlockSpec((bk, N), lambda i, k: (k, 0))],
        out_specs=pl.BlockSpec((N_DEV, bm, IN_BLOCK), lambda i, k: (0, i, 0)),
        out_shape=jax.ShapeDtypeStruct((N_DEV, M, IN_BLOCK), F32),
        scratch_shapes=[pltpu.VMEM((bm, N), F32)],
        compiler_params=pltpu.CompilerParams(dimension_semantics=("arbitrary", "arbitrary"), vmem_limit_bytes=VMEM_LIMIT),
    )(a, b)


def _adamw(w, g, m, v):
    m = B1 * m + (1.0 - B1) * g
    v = B2 * v + (1.0 - B2) * (g * g)
    m_hat = m / (1.0 - B1 ** STEP)
    v_hat = v / (1.0 - B2 ** STEP)
    return -LR * (m_hat / (jnp.sqrt(v_hat) + AEPS) + WD * w), m, v


def _adam_call(name, w, g, m, v, tm):
    C = w.shape[1]
    return _rowwise(name, _adamw, [w, g, m, v], [], [(C, F32)] * 3, [], tm)


def _adam_layer(name, g, w, m, v, tm, layer, prev):
    C = g.shape[1]
    return _rowwise(name, lambda g, w, m, v: (g,) + _adamw(w, g, m, v), [g, w, m, v], [], [(C, F32)] * 4, [], tm,
                    layer=layer, prev=prev)


def _reduce_sum(name, own, recv, tm):
    n, R, C = recv.shape
    flat = recv.reshape(n * R, C)
    fn = lambda own, r0, r1, r2: (((own + r0.astype(F32)) + r1.astype(F32)) + r2.astype(F32),)
    return _rowwise(name, fn, [_own_block(own, R, tm)] + [_cols(flat, C, 0, j * R) for j in range(n)], [], [(C, F32)], [],
                    tm, rows=R)[0]


def _pair_add(name, g, from_sib, tm):
    n, R, C = from_sib.shape
    n_r = R // tm
    mine = _cols(g.reshape(N_DEV * R, C), C, 0,
                 block_row=lambda i: (2 * (i // n_r) + lax.axis_index("c")) * n_r + i % n_r)

    def fn(a, b):
        p = a + b
        return p, p

    return _rowwise(name, fn, [mine, from_sib.reshape(n * R, C)], [], [(C, BF16), (C, F32)], [], tm, rows=n * R)


def _own_block(p32, R, tm):
    n_r = R // tm
    return _cols(p32, p32.shape[1], 0, block_row=lambda i: (2 * lax.axis_index("x") + lax.axis_index("y")) * n_r + i)


def _reduce_adam(name, own, recv, w, m, v, tm, layer, prev):
    n, R, C = recv.shape
    flat = recv.reshape(n * R, C)

    def fn(own, r0, r1, r2, w, m, v):
        g = ((own + r0.astype(F32)) + r1.astype(F32)) + r2.astype(F32)
        return (g,) + _adamw(w, g, m, v)

    return _rowwise(name, fn, [_own_block(own, R, tm)] + [_cols(flat, C, 0, j * R) for j in range(n)] + [w, m, v], [],
                    [(C, F32)] * 4, [], tm, rows=R, layer=layer, prev=prev)


_OFFSETS = [(dx, dy, dc) for dx in (0, 1) for dy in (0, 1) for dc in (0, 1)][1:]
_MESH = pl.DeviceIdType.MESH


def _coords():
    return lax.axis_index("x"), lax.axis_index("y"), lax.axis_index("c")


def _flip(me, off):
    return tuple((1 - m) if d else m for m, d in zip(me, off))


def _linear(p):
    return 4 * p[0] + 2 * p[1] + p[2]


_CHIP_FLIPS = ((1, 0), (0, 1), (1, 1))


class _Rider:
    def __init__(self, ins, outs, n_remote, n_local, start, finish):
        self.ins, self.outs, self.n_remote, self.n_local, self.start, self.finish = ins, outs, n_remote, n_local, start, finish

    def scratch(self):
        return [pltpu.SemaphoreType.DMA((self.n_remote,)), pltpu.SemaphoreType.DMA((self.n_remote,)),
                pltpu.SemaphoreType.DMA((max(self.n_local, 1),))]


def _run_rider(name, rider):
    def body(*refs):
        n_i, n_o = len(rider.ins), len(rider.outs)
        rider.start(refs[:n_i], refs[n_i:n_i + n_o], *refs[n_i + n_o:])
        rider.finish(refs[:n_i], refs[n_i:n_i + n_o], *refs[n_i + n_o:])

    anyspec = pl.BlockSpec(memory_space=pl.ANY)
    return pl.pallas_call(
        body, name=name, in_specs=[anyspec] * len(rider.ins), out_specs=[anyspec] * len(rider.outs),
        out_shape=list(rider.outs), scratch_shapes=rider.scratch(),
    )(*rider.ins)


def _gather_rider(arrs, layer):
    n = len(arrs)

    def parts(ins, outs, send, recv, loc):
        x, y, c = _coords()
        me, sib = (x, y, c), (x, y, 1 - c)
        chips = [((1 - x) if dx else x, (1 - y) if dy else y) for dx, dy in _CHIP_FLIPS]

        def copy(a, k, block, to, own=False):
            slot = outs[a].at[_linear(block)]
            return pltpu.make_async_remote_copy(src_ref=ins[a].at[layer] if own else slot, dst_ref=slot,
                                                send_sem=send.at[a * 7 + k], recv_sem=recv.at[a * 7 + k],
                                                device_id=to, device_id_type=_MESH)

        local = [pltpu.make_async_copy(ins[a].at[layer], outs[a].at[_linear(me)], loc.at[a]) for a in range(n)]
        first = []
        for a in range(n):
            first.append(copy(a, 0, me, sib, own=True))
            first += [copy(a, 1 + j, me, (*chip, c), own=True) for j, chip in enumerate(chips)]
        return copy, local, first, me, sib, chips, c

    def start(*refs):
        _, local, first, *_ = parts(*refs)
        for cp in local + first:
            cp.start()

    def finish(*refs):
        copy, local, first, me, sib, chips, c = parts(*refs)
        passed = []
        for j, chip in enumerate(chips):
            for a in range(n):
                copy(a, 1 + j, (*chip, c), me).wait_recv()
                cp = copy(a, 4 + j, (*chip, c), sib)
                cp.start()
                passed.append(cp)
        for a in range(n):
            copy(a, 0, sib, me).wait_recv()
            for j, chip in enumerate(chips):
                copy(a, 4 + j, (*chip, 1 - c), me).wait_recv()
        for cp in first + passed:
            cp.wait_send()
        for cp in local:
            cp.wait()

    outs = [jax.ShapeDtypeStruct((N_DEV,) + a.shape[1:], a.dtype) for a in arrs]
    return _Rider(list(arrs), outs, 7 * n, n, start, finish)


def _simple_rider(ins, outs, n_remote, make):
    def start(*refs):
        for cp in make(*refs):
            cp.start()

    def finish(*refs):
        for cp in make(*refs):
            cp.wait()

    return _Rider(ins, outs, n_remote, 0, start, finish)


def _pair_rider(arrs):
    def make(ins, outs, send, recv, loc):
        x, y, c = _coords()
        return [pltpu.make_async_remote_copy(src_ref=ins[a].at[2 * s_ + 1 - c], dst_ref=outs[a].at[s_], send_sem=send.at[4 * a + s_],
                                             recv_sem=recv.at[4 * a + s_], device_id=(x, y, 1 - c), device_id_type=_MESH)
                for a in range(len(arrs)) for s_ in range(4)]

    return _simple_rider(list(arrs), [jax.ShapeDtypeStruct((4,) + a.shape[1:], a.dtype) for a in arrs], 4 * len(arrs), make)


def _chip_rider(arrs):
    nf = len(_CHIP_FLIPS)

    def make(ins, outs, send, recv, loc):
        x, y, c = _coords()
        copies = []
        for a in range(len(arrs)):
            for k, (dx, dy) in enumerate(_CHIP_FLIPS):
                px, py = (1 - x) if dx else x, (1 - y) if dy else y
                copies.append(pltpu.make_async_remote_copy(
                    src_ref=ins[a].at[2 * px + py], dst_ref=outs[a].at[k], send_sem=send.at[a * nf + k],
                    recv_sem=recv.at[a * nf + k], device_id=(px, py, c), device_id_type=_MESH))
        return copies

    return _simple_rider(list(arrs), [jax.ShapeDtypeStruct((nf,) + a.shape[1:], a.dtype) for a in arrs], nf * len(arrs), make)


def _small_allgather(name, packed):
    R = packed.shape[0]

    def body(in_ref, all_ref, sum_ref, send, recv):
        me = _coords()
        my = _linear(me)
        all_ref[my] = in_ref[...]
        copies = []
        for k, off in enumerate(_OFFSETS):
            cp = pltpu.make_async_remote_copy(src_ref=in_ref, dst_ref=all_ref.at[my], send_sem=send.at[k], recv_sem=recv.at[k],
                                              device_id=_flip(me, off), device_id_type=_MESH)
            cp.start()
            copies.append(cp)
        for cp in copies:
            cp.wait()
        acc = all_ref[0]
        for j in range(1, N_DEV):
            acc = acc + all_ref[j]
        sum_ref[...] = acc

    vm = pl.BlockSpec(memory_space=pltpu.VMEM)
    return pl.pallas_call(
        body, name=name, in_specs=[vm], out_specs=[vm, vm],
        out_shape=[jax.ShapeDtypeStruct((N_DEV, R, 128), F32), jax.ShapeDtypeStruct((R, 128), F32)],
        scratch_shapes=[pltpu.SemaphoreType.DMA((len(_OFFSETS),)), pltpu.SemaphoreType.DMA((len(_OFFSETS),))],
        compiler_params=pltpu.CompilerParams(vmem_limit_bytes=VMEM_LIMIT),
    )(packed)


def _ada_mod(c, w16, bias):
    nc = w16.shape[2]
    kp = len(_OFFSETS)

    def body(c_ref, w_ref, b_ref, rows_ref, act_ref, cbuf, sbuf, send, recv):
        me = _coords()
        my = _linear(me)
        cbuf[my] = c_ref[...]
        copies = []
        for k, off in enumerate(_OFFSETS):
            cp = pltpu.make_async_remote_copy(src_ref=c_ref, dst_ref=cbuf.at[my], send_sem=send.at[k], recv_sem=recv.at[k],
                                              device_id=_flip(me, off), device_id_type=_MESH)
            cp.start()
            copies.append(cp)
        for cp in copies:
            cp.wait()
        act = _silu(jnp.concatenate([cbuf[j] for j in range(N_DEV)], axis=0))
        act_ref[...] = act
        act16 = act.astype(BF16)
        for l in range(DEPTH):
            ml = jnp.dot(act16, w_ref[l], preferred_element_type=F32) + b_ref[l:l + 1, :]
            for j in range(N_DEV):
                sbuf[j, l:l + 1, :] = ml[j:j + 1, :]
        rows_ref[my] = sbuf[my]
        copies = []
        for k, off in enumerate(_OFFSETS):
            peer = _flip(me, off)
            cp = pltpu.make_async_remote_copy(src_ref=sbuf.at[_linear(peer)], dst_ref=rows_ref.at[my], send_sem=send.at[kp + k],
                                              recv_sem=recv.at[kp + k], device_id=peer, device_id_type=_MESH)
            cp.start()
            copies.append(cp)
        for cp in copies:
            cp.wait()

    vm = pl.BlockSpec(memory_space=pltpu.VMEM)
    return pl.pallas_call(
        body, name="ada_mod", in_specs=[vm, vm, vm], out_specs=[vm, vm],
        out_shape=[jax.ShapeDtypeStruct((N_DEV, DEPTH, nc), F32), jax.ShapeDtypeStruct((N_DEV, D), F32)],
        scratch_shapes=[pltpu.VMEM((N_DEV, 1, D), F32), pltpu.VMEM((N_DEV, DEPTH, nc), F32),
                        pltpu.SemaphoreType.DMA((2 * kp,)), pltpu.SemaphoreType.DMA((2 * kp,))],
        compiler_params=pltpu.CompilerParams(vmem_limit_bytes=VMEM_LIMIT),
    )(c, w16, bias)


def _pack(arrs):
    parts = []
    for a in arrs:
        f = a.reshape(-1).astype(F32)
        parts.append(jnp.pad(f, (0, (-f.shape[0]) % 128)))
    flat = jnp.concatenate(parts)
    flat = jnp.pad(flat, (0, (-flat.shape[0]) % 1024))
    return flat.reshape(-1, 128)


def _unpack(packed, shapes):
    flat = packed.reshape(packed.shape[:-2] + (-1,))
    out, r = [], 0
    for s in shapes:
        n = int(np.prod(s))
        out.append(flat[..., r:r + n].reshape(packed.shape[:-2] + tuple(s)))
        r += -(-n // 128) * 128
    return out


def _pad_rows(w, rows):
    return jnp.pad(w, ((0, 0), (0, rows - w.shape[1]), (0, 0)))


_SMALL = ("b_ada", "norm_mix_g", "norm_ffn_g", "conv_a_w", "conf_dw_w", "conf_dw_b", "conf_ln_g", "conf_ln_b",
          "dn_conv_w", "dn_a_log", "dn_dt_bias", "dn_norm_g", "final_norm_g")
_BIG = ("w_in", "w_out", "w_ffn_in", "w_ffn_out")
_WEIGHTS = ("w_ada", "b_ada", "norm_mix_g", "norm_ffn_g", "w_in", "conv_a_w", "conf_dw_w", "conf_dw_b", "conf_ln_g",
            "conf_ln_b", "dn_conv_w", "dn_a_log", "dn_dt_bias", "dn_norm_g", "w_out", "w_ffn_in", "w_ffn_out",
            "final_norm_g")


def _step(x, c, loss_target, W, M, V):
    T = x.shape[1]
    me = _linear(_coords())
    cc = lax.axis_index("c")
    chip = 2 * lax.axis_index("x") + lax.axis_index("y")
    xs, tgt = x[0], loss_target[0]
    vec = lambda a: a.reshape(1, -1)

    nada = W["w_ada"].shape[2]
    rows, act_all = _ada_mod(c, W["w_ada"].astype(BF16), lax.dynamic_slice(W["b_ada"], (0, me * nada), (DEPTH, nada)))
    mod = rows.transpose(1, 0, 2).reshape(DEPTH, 6, 1, D)

    w16 = {k: W[k].astype(BF16) for k in _BIG}
    w16["w_ffn_in"] = W["w_ffn_in"].transpose(0, 2, 1).astype(BF16)

    def whole(g_in=None, g_out=None, g_fi=None, g_fo=None):
        out = {}
        if g_in is not None:
            out["w_in"] = jnp.pad(g_in.transpose(1, 0, 2).reshape(D, IN_COLS), ((0, 0), (0, IN_PAD - IN_COLS)))
        if g_out is not None:
            out["w_out"] = g_out.reshape(D, D)
        if g_fi is not None:
            out["w_ffn_in"] = g_fi.reshape(2 * D_FF, D)
        if g_fo is not None:
            out["w_ffn_out"] = g_fo.reshape(D_FF, D)
        return out

    wts = [dict() for _ in range(DEPTH)]
    gather = lambda names, layer: _gather_rider([w16[k] for k in names], layer)
    wts[0].update(whole(g_in=_run_rider("gather_weights", gather(["w_in"], 0))[0]))
    conv_names = ("conv_a_w", "conf_dw_w", "dn_conv_w")
    conv_all, _ = _small_allgather("gather_conv_w", _pack([W[k] for k in conv_names]))
    conv_full = [t.transpose(1, 2, 0, 3).reshape(t.shape[1], t.shape[2], -1)
                 for t in _unpack(conv_all, [W[k].shape for k in conv_names])]
    wa, wb, wc = _pad_rows(conv_full[0], 8), _pad_rows(conv_full[1], 32), _pad_rows(conv_full[2], 8)
    lane_pad = lambda a: jnp.pad(a, ((0, 0), (0, 128 - a.shape[1])))
    alog, dtb = lane_pad(W["dn_a_log"]), lane_pad(W["dn_dt_bias"])

    saved = []
    xc = xs
    for l in range(DEPTH):
        more = l + 1 < DEPTH
        sh1, sc1, g1, sh2, sc2, g2 = [mod[l, i] for i in range(6)]
        proj, h, *got = _normproj_fwd("inproj_fwd", xc, sh1, sc1, vec(W["norm_mix_g"][l]), wts[l]["w_in"], 512,
                                      rider=gather(["w_out", "w_ffn_out"], 0) if l == 0 else None)
        if l == 0:
            wts[0].update(whole(g_out=got[0], g_fo=got[1]))
        convout = _conv_fwd(proj, wa[l], wb[l], wc[l], 256)
        y_ab, qkv, gb = _stage2_fwd(proj, convout, vec(W["conf_dw_b"][l]), vec(W["conf_ln_g"][l]), vec(W["conf_ln_b"][l]),
                                    alog[l:l + 1], dtb[l:l + 1], 256)
        o, ss, *got = _delta_fwd(qkv, gb, DELTA_NB,
                                 rider=gather(["w_ffn_in"], 0) if l == 0 else gather(["w_in"], l + 1) if more else None)
        if l == 0:
            wts[0].update(whole(g_fi=got[0]))
        elif more:
            wts[l + 1].update(whole(g_in=got[0]))
        x1, mix, ycat, *got = _outproj_fwd(xc, o, proj, y_ab, g1, vec(W["dn_norm_g"][l]), wts[l]["w_out"], 512,
                                           rider=gather(["w_in"], 1) if l == 0 else None)
        if l == 0:
            wts[1].update(whole(g_in=got[0]))
        gu, h2, *got = _normproj_fwd("ffnin_fwd", x1, sh2, sc2, vec(W["norm_ffn_g"][l]), wts[l]["w_ffn_in"], 256,
                                     rider=gather(["w_ffn_in"], l + 1) if more else None, w_t=True)
        if more:
            wts[l + 1].update(whole(g_fi=got[0]))
        x2, f, *got = _ffn_out_fwd(x1, gu, g2, wts[l]["w_ffn_out"], 256,
                                   rider=gather(["w_ffn_out", "w_out"], l + 1) if more else None)
        if more:
            wts[l + 1].update(whole(g_fo=got[0], g_out=got[1]))
        saved.append((xc, proj, h, convout, qkv, gb, o, ss, mix, ycat, x1, gu, h2, f))
        xc = x2

    dx, loss_row, d_gfin = _loss_bwd(xc, tgt, vec(W["final_norm_g"]), 512)
    loss = lax.psum(loss_row[0, 0], ("x", "y", "c"))

    big_out = {k: None for k in _BIG}
    dmod, small = [None] * DEPTH, [None] * DEPTH
    blocks = lambda g: g.reshape(N_DEV, -1, g.shape[-1])
    pair_tm = {"w_in": 512, "w_out": 128, "w_ffn_in": 704, "w_ffn_out": 352}
    sum_tm = {"w_in": 256, "w_out": 128, "w_ffn_in": 176, "w_ffn_out": 176}
    turned = lambda a: a.transpose(0, 2, 1)
    wmv = {k: (W[k], M[k], V[k]) for k in _BIG}
    wmv["w_ffn_in"] = tuple(turned(a) for a in wmv["w_ffn_in"])

    def pair_sum(k, g, from_sib):
        p16, p32 = _pair_add("pair_add_" + k, g, from_sib, pair_tm[k])
        return p16.reshape(from_sib.shape), p32

    def finish(k, layer, p32, r):
        if k == "w_in":
            g = _reduce_sum("reduce_" + k, p32, r, sum_tm[k])[:, :IN_COLS // N_DEV]
            big_out[k] = _adam_layer("adam_" + k, g, *wmv[k], 256, layer, big_out[k])
        else:
            big_out[k] = _reduce_adam("reduce_adam_" + k, p32, r, *wmv[k], sum_tm[k], layer, big_out[k])

    above = None
    for l in reversed(range(DEPTH)):
        xc, proj, h, convout, qkv, gb, o, ss, mix, ycat, x1, gu, h2, f = saved[l]
        sh1, sc1, g1, sh2, sc2, g2 = [mod[l, i] for i in range(6)]
        gm, gf = vec(W["norm_mix_g"][l]), vec(W["norm_ffn_g"][l])
        bb, lg, lb = vec(W["conf_dw_b"][l]), vec(W["conf_ln_g"][l]), vec(W["conf_ln_b"][l])
        dng = vec(W["dn_norm_g"][l])
        wl = wts[l]

        dgu, s, df, d_g2, *got = _ffn_out_bwd(dx, gu, f, g2, wl["w_ffn_out"], 256,
                                              rider=_pair_rider([above[1]]) if above else None)
        if above:
            in16, in_own = pair_sum("w_in", above[1], got[0])
        gw_fo = _wgrad("wgrad_ffn_out", s, df, 1408, 1024)
        dx1, d_sh2, d_sc2, d_gf, *got = _normproj_bwd("ffnin_bwd", x1, dgu, dx, sc2, gf, wl["w_ffn_in"], 256,
                                                      rider=_chip_rider([in16]) if above else None, w_t=True)
        if above:
            finish("w_in", above[0], in_own, got[0])
        gw_fi = _wgrad("wgrad_ffn_in", dgu, h2, 1408, 1024)
        gw_fo, gw_fi = blocks(gw_fo), blocks(gw_fi)
        dmix, dy_ab, do, dz, d_g1, d_dng, *got = _outproj_bwd(dx1, mix, o, proj, g1, dng, wl["w_out"], 512,
                                                               rider=_pair_rider([gw_fo, gw_fi]))
        fo16, fo_own = pair_sum("w_ffn_out", gw_fo, got[0])
        fi16, fi_own = pair_sum("w_ffn_in", gw_fi, got[1])
        gw_out = blocks(_wgrad("wgrad_out", ycat, dmix, 512, 2048))
        dqkv, *got = _delta_bwd(qkv, gb, ss, do, DELTA_NB, rider=_chip_rider([fo16, fi16]))
        finish("w_ffn_out", l, fo_own, got[0])
        finish("w_ffn_in", l, fi_own, got[1])
        dco, da_b, dblk, d_bb, d_lg, d_lb, d_alog, d_dtb, *got = _stage2_bwd(
            proj, convout, dy_ab, _cols(dqkv, D_DN, 0), _cols(dqkv, D_DN, 1), _cols(dqkv, D_DN, 2),
            _cols(dqkv, 128, 3 * D_DN // 128), bb, lg, lb, alog[l:l + 1], dtb[l:l + 1], 256,
            rider=_pair_rider([gw_out]))
        out16, out_own = pair_sum("w_out", gw_out, got[0])
        dproj, d_wa, d_wb, d_wc, *got = _conv_bwd(proj, dco, da_b, dz, dblk, wa[l], wb[l], wc[l], 256,
                                                  rider=_chip_rider([out16]))
        finish("w_out", l, out_own, got[0])
        dx, d_sh1, d_sc1, d_gm = _normproj_bwd("inproj_bwd", xc, dproj, dx1, sc1, gm, wl["w_in"], 512)
        above = (l, _wgrad_in_blocks("wgrad_in", h, dproj, 512, 1024))

        dmod[l] = jnp.concatenate([d_sh1, d_sc1, d_g1, d_sh2, d_sc2, d_g2], axis=1)
        small[l] = dict(norm_mix_g=d_gm, norm_ffn_g=d_gf, conv_a_w=d_wa[:KA], conf_dw_w=d_wb[:KB], conf_dw_b=d_bb,
                        conf_ln_g=d_lg, conf_ln_b=d_lb, dn_conv_w=d_wc[:KC], dn_a_log=d_alog, dn_dt_bias=d_dtb,
                        dn_norm_g=d_dng)

    in16, in_own = pair_sum("w_in", above[1], _run_rider("pair_exchange", _pair_rider([above[1]]))[0])
    finish("w_in", above[0], in_own, _run_rider("chip_exchange", _chip_rider([in16]))[0])

    names = ("norm_mix_g", "norm_ffn_g", "conv_a_w", "conf_dw_w", "conf_dw_b", "conf_ln_g", "conf_ln_b", "dn_conv_w",
             "dn_a_log", "dn_dt_bias", "dn_norm_g")
    pieces = [jnp.stack(dmod)] + [jnp.stack([small[l][k] for l in range(DEPTH)]) for k in names] + [d_gfin]
    shapes = [p.shape for p in pieces]
    every, total = _small_allgather("gather_small_grads", _pack(pieces))
    tot = dict(zip(("dmod",) + names + ("final_norm_g",), _unpack(total, shapes)))
    dmod_all = _unpack(every, shapes[:1])[0]

    grads = {}
    grads["b_ada"] = tot["dmod"].reshape(DEPTH, 6 * D)
    for k in ("norm_mix_g", "norm_ffn_g", "conf_dw_b", "conf_ln_g", "conf_ln_b", "dn_norm_g"):
        grads[k] = tot[k].reshape(W[k].shape)
    grads["dn_a_log"] = tot["dn_a_log"].reshape(DEPTH, 128)[:, :HEADS]
    grads["dn_dt_bias"] = tot["dn_dt_bias"].reshape(DEPTH, 128)[:, :HEADS]
    grads["final_norm_g"] = tot["final_norm_g"].reshape(D)
    for k in conv_names:
        nloc = W[k].shape[2]
        grads[k] = lax.dynamic_slice_in_dim(tot[k], me * nloc, nloc, axis=2)

    dm = lax.dynamic_slice_in_dim(dmod_all.reshape(N_DEV, DEPTH, 6 * D), me * nada, nada, axis=2)
    pad16 = lambda a: jnp.pad(a, ((0, 16 - N_DEV), (0, 0))).astype(BF16)
    g_ada = _wgrad("wgrad_ada", pad16(act_all), pad16(dm.reshape(N_DEV, DEPTH * nada)), 256, 16)
    grads["w_ada"] = g_ada.reshape(D, DEPTH, nada).transpose(1, 0, 2)

    delta, new_m, new_v = {}, {}, {}
    r2 = lambda a: a.reshape(DEPTH * D, nada)
    d_, m_, v_ = _adam_call("adam_ada", r2(W["w_ada"]), r2(grads["w_ada"]), r2(M["w_ada"]), r2(V["w_ada"]), 512)
    delta["w_ada"], new_m["w_ada"], new_v["w_ada"] = [t.reshape(W["w_ada"].shape) for t in (d_, m_, v_)]
    sshapes = [W[k].shape for k in _SMALL]
    d_, m_, v_ = _adam_call("adam_small", _pack([W[k] for k in _SMALL]), _pack([grads[k] for k in _SMALL]),
                            _pack([M[k] for k in _SMALL]), _pack([V[k] for k in _SMALL]), 4096)
    for dst, packed in ((delta, d_), (new_m, m_), (new_v, v_)):
        dst.update(zip(_SMALL, _unpack(packed, sshapes)))
    for k in _BIG:
        grads[k], delta[k], new_m[k], new_v[k] = [turned(a) for a in big_out[k]] if k == "w_ffn_in" else big_out[k]

    return (loss, dx[None], *[grads[k] for k in _WEIGHTS], *[delta[k] for k in _WEIGHTS],
            *[new_m[k] for k in _WEIGHTS], *[new_v[k] for k in _WEIGHTS])


def kernel(x, c, w_ada, b_ada, norm_mix_g, norm_ffn_g, w_in, conv_a_w, conf_dw_w, conf_dw_b, conf_ln_g, conf_ln_b, dn_conv_w, dn_a_log, dn_dt_bias, dn_norm_g, w_out, w_ffn_in, w_ffn_out, final_norm_g, loss_target, m_w_ada, m_b_ada, m_norm_mix_g, m_norm_ffn_g, m_w_in, m_conv_a_w, m_conf_dw_w, m_conf_dw_b, m_conf_ln_g, m_conf_ln_b, m_dn_conv_w, m_dn_a_log, m_dn_dt_bias, m_dn_norm_g, m_w_out, m_w_ffn_in, m_w_ffn_out, m_final_norm_g, v_w_ada, v_b_ada, v_norm_mix_g, v_norm_ffn_g, v_w_in, v_conv_a_w, v_conf_dw_w, v_conf_dw_b, v_conf_ln_g, v_conf_ln_b, v_dn_conv_w, v_dn_a_log, v_dn_dt_bias, v_dn_norm_g, v_w_out, v_w_ffn_in, v_w_ffn_out, v_final_norm_g):
    a = dict(locals())
    W = {k: a[k] for k in _WEIGHTS}
    M = {k: a["m_" + k] for k in _WEIGHTS}
    V = {k: a["v_" + k] for k in _WEIGHTS}
    return _step(x, c, loss_target, W, M, V)
```

```python
import functools

import jax
import jax.numpy as jnp
import numpy as np
from jax import lax
from jax.experimental import pallas as pl
from jax.experimental.pallas import tpu as pltpu

F32 = jnp.float32
BF16 = jnp.bfloat16

N_DEV = 8
D = 1024
DEPTH = 4
D_CONV = 256
D_CONF = 256
D_DN = 512
HEADS = 4
HD = 128
KA, KB, KC = 3, 31, 4
CHUNK = 64
D_FF = 2816
IN_COLS = 3336
IN_PAD = 3456
N_CONVCOL = 2048
EPS = 1e-6
LN_EPS = 1e-5
HALO = 32
VMEM_LIMIT = 56 * 1024 * 1024
DELTA_NB = 8

C_AB, C_AC, C_AV, C_BA, C_BG, C_Q, C_Z, C_GB = 0, 256, 512, 768, 1024, 1280, 2816, 3328

LR, B1, B2, AEPS, WD, STEP = 0.001, 0.9, 0.999, 1e-08, 0.01, 10


def _dot(a, b, dims, hi):
    if hi:
        return lax.dot_general(a.astype(F32), b.astype(F32), (dims, ((), ())), precision=lax.Precision.HIGHEST,
                               preferred_element_type=F32)
    return lax.dot_general(a.astype(BF16), b.astype(BF16), (dims, ((), ())), preferred_element_type=F32)


@functools.partial(jax.custom_vjp, nondiff_argnums=(2,))
def mm_nn(a, b, hi=False):
    return _dot(a, b, ((1,), (0,)), hi)


@functools.partial(jax.custom_vjp, nondiff_argnums=(2,))
def mm_nt(a, b, hi=False):
    return _dot(a, b, ((1,), (1,)), hi)


@functools.partial(jax.custom_vjp, nondiff_argnums=(2,))
def mm_tn(a, b, hi=False):
    return _dot(a, b, ((0,), (0,)), hi)


mm_nn.defvjp(lambda a, b, hi: (mm_nn(a, b, hi), (a, b)),
             lambda hi, r, g: (mm_nt(g, r[1], hi), mm_tn(r[0], g, hi)))
mm_nt.defvjp(lambda a, b, hi: (mm_nt(a, b, hi), (a, b)),
             lambda hi, r, g: (mm_nn(g, r[1], hi), mm_tn(g, r[0], hi)))
mm_tn.defvjp(lambda a, b, hi: (mm_tn(a, b, hi), (a, b)),
             lambda hi, r, g: (mm_nt(r[1], g, hi), mm_nn(r[0], g, hi)))


def _sigmoid(x):
    return 1.0 / (1.0 + jnp.exp(-x))


def _silu(x):
    return x * _sigmoid(x)


def _softplus(x):
    return jnp.maximum(x, 0.0) + jnp.log(1.0 + jnp.exp(-jnp.abs(x)))


def _iota2(shape, dim):
    return lax.broadcasted_iota(jnp.int32, shape, dim)


def _dot16(a, b):
    return jnp.dot(a.astype(BF16), b.astype(BF16), preferred_element_type=F32)


def _dot_3pass(a, b):
    ah = a.astype(BF16)
    bh = b.astype(BF16)
    al = (a - ah.astype(F32)).astype(BF16)
    bl = (b - bh.astype(F32)).astype(BF16)
    d = lambda x, y: jnp.dot(x, y, preferred_element_type=F32)
    return d(ah, bh) + (d(ah, bl) + d(al, bh))


@jax.custom_vjp
def _unit_lower_inverses(Xs):
    n = Xs[0].shape[0]
    r, c = _iota2((n, n), 0), _iota2((n, n), 1)
    eye = (r == c).astype(F32)

    def joins(b):
        s = b.bit_length() - 1
        return ((r >> (s + 1)) == (c >> (s + 1))) & (((r >> s) & 1) == 1) & (((c >> s) & 1) == 0)

    Ts = [eye + jnp.where(joins(1), x, 0.0) for x in Xs]
    b = 2
    while b < n:
        m = joins(b)
        Ys = [_dot16(jnp.where(m, x, 0.0), t) for x, t in zip(Xs, Ts)]
        Ts = [t + _dot16(t, y) for t, y in zip(Ts, Ys)]
        b *= 2
    Rs = [(eye - t) + _dot_3pass(x, t) for x, t in zip(Xs, Ts)]
    return [t + _dot16(t, r_) for t, r_ in zip(Ts, Rs)]


def _unit_lower_inverses_fwd(Xs):
    Ts = _unit_lower_inverses(Xs)
    return Ts, Ts


def _unit_lower_inverses_bwd(Ts, gs):
    inner = [mm_nt(g, t) for g, t in zip(gs, Ts)]
    return ([mm_tn(t, i) for t, i in zip(Ts, inner)],)


_unit_lower_inverses.defvjp(_unit_lower_inverses_fwd, _unit_lower_inverses_bwd)


@jax.custom_vjp
def _saved_inverses(Xs, Ts):
    return list(Ts)


_saved_inverses.defvjp(lambda Xs, Ts: (list(Ts), Ts),
                       lambda Ts, gs: (_unit_lower_inverses_bwd(Ts, gs)[0], [jnp.zeros_like(t) for t in Ts]))


def _delta_chunks(qs, ks, vs, gbs, Ss, Ts=None, keep=None):
    C = CHUNK
    nb = len(gbs)
    pairs = [(c, h) for c in range(nb) for h in range(HEADS)]
    each = lambda fn, *lists: [fn(*a) for a in zip(*lists)]
    row = _iota2((C, C), 0)
    col = _iota2((C, C), 1)
    causal = row >= col
    strict = row > col
    tri = causal.astype(F32)
    eye = (row == col).astype(F32)
    lane = _iota2((C, 128), 1)
    subl = _iota2((128, C), 0)
    last = (_iota2((C, 1), 0) == C - 1).astype(F32)

    gc_all = [mm_nn(tri, gb, True) for gb in gbs]
    gc_t = [g.T for g in gc_all]
    q = [qs[c][h] * (HD ** -0.5) for c, h in pairs]
    k = [ks[c][h] for c, h in pairs]
    v = [vs[c][h] for c, h in pairs]
    gcol = [jnp.sum(jnp.where(lane == h, gc_all[c], 0.0), axis=1, keepdims=True) for c, h in pairs]
    grow = [jnp.sum(jnp.where(subl == h, gc_t[c], 0.0), axis=0, keepdims=True) for c, h in pairs]
    beta = [jnp.sum(jnp.where(lane == HEADS + h, gbs[c], 0.0), axis=1, keepdims=True) for c, h in pairs]
    decay = each(lambda a, b: jnp.where(causal, jnp.exp(jnp.where(causal, a - b, 0.0)), 0.0), gcol, grow)
    kb = each(lambda a, b: a * b, k, beta)
    vb = each(lambda a, b: a * b, v, beta)
    kk = each(lambda a, b: mm_nt(a, b), kb, k)
    X = each(lambda a, d: -jnp.where(strict, a * d, 0.0), kk, decay)
    T = _unit_lower_inverses(X) if Ts is None else _saved_inverses(X, Ts)
    if keep is not None:
        keep.extend(T)
    eg = [jnp.exp(g) for g in gcol]
    u = each(lambda t, a: mm_nn(t, a), T, vb)
    w = each(lambda t, a, e: mm_nn(t, a * e), T, kb, eg)
    qk = each(lambda a, b, d: jnp.where(causal, mm_nt(a, b) * d, 0.0), q, k, decay)
    qg = each(lambda a, e: a * e, q, eg)
    g_last = [jnp.sum(g * last, axis=0, keepdims=True) for g in gcol]
    kd = each(lambda a, gl, g: a * jnp.exp(gl - g), k, g_last, gcol)
    eg_last = [jnp.exp(g) for g in g_last]

    outs = []
    for c in range(nb):
        sl = slice(c * HEADS, (c + 1) * HEADS)
        v_new = each(lambda a, b, S: a - mm_nn(b, S), u[sl], w[sl], Ss)
        oS = each(lambda a, S: mm_nn(a, S), qg[sl], Ss)
        outs.append(each(lambda a, b, n: a + mm_nn(b, n), oS, qk[sl], v_new))
        Ss = each(lambda S, e, a, n: S * e + mm_tn(a, n), Ss, eg_last[sl], kd[sl], v_new)
    return outs, Ss


def _split_chunks(ref, nb):
    return [[ref[c * CHUNK:(c + 1) * CHUNK, h * HD:(h + 1) * HD] for h in range(HEADS)] for c in range(nb)]


def _join_chunks(vals):
    return jnp.concatenate([jnp.concatenate(heads, axis=1) for heads in vals], axis=0)


def _hosted(body, n_in, n_out, rider, n_steps):
    if rider is None:
        return body
    n_ri, n_ro = len(rider.ins), len(rider.outs)

    def wrapped(*refs):
        ins, r_ins = refs[:n_in], refs[n_in:n_in + n_ri]
        outs = refs[n_in + n_ri:n_in + n_ri + n_out]
        r_outs = refs[n_in + n_ri + n_out:n_in + n_ri + n_out + n_ro]
        rest = refs[n_in + n_ri + n_out + n_ro:]
        scr, sems = rest[:len(rest) - 3], rest[len(rest) - 3:]

        @pl.when(pl.program_id(0) == 0)
        def _():
            rider.start(r_ins, r_outs, *sems)

        body(*ins, *outs, *scr)

        @pl.when(pl.program_id(0) == n_steps - 1)
        def _():
            rider.finish(r_ins, r_outs, *sems)

    return wrapped


def _host_call(body, name, grid, in_specs, out_specs, out_shape, scratch, operands, rider):
    out_specs, out_shape = list(out_specs), list(out_shape)
    n_in, n_out = len(in_specs), len(out_specs)
    if rider is not None:
        anyspec = pl.BlockSpec(memory_space=pl.ANY)
        in_specs = list(in_specs) + [anyspec] * len(rider.ins)
        out_specs += [anyspec] * len(rider.outs)
        out_shape += list(rider.outs)
        scratch = list(scratch) + rider.scratch()
        operands = list(operands) + list(rider.ins)
    return pl.pallas_call(
        _hosted(body, n_in, n_out, rider, grid[0]), name=name, grid=grid, in_specs=in_specs, out_specs=out_specs,
        out_shape=out_shape, scratch_shapes=scratch,
        compiler_params=pltpu.CompilerParams(dimension_semantics=("arbitrary",), vmem_limit_bytes=VMEM_LIMIT),
    )(*operands)


def _delta_fwd(qkv, gb, nb, rider=None):
    T = qkv.shape[0]
    nb = min(nb, T // CHUNK)
    rows = nb * CHUNK
    n = T // rows

    def body(q_ref, k_ref, v_ref, gb_ref, o_ref, ss_ref, t_ref, s_scr):
        @pl.when(pl.program_id(0) == 0)
        def _():
            s_scr[...] = jnp.zeros_like(s_scr)

        Ss = [s_scr[h] for h in range(HEADS)]
        for h in range(HEADS):
            ss_ref[0, h] = Ss[h]
        gbs = [gb_ref[c * CHUNK:(c + 1) * CHUNK, :] for c in range(nb)]
        kept = []
        outs, new_S = _delta_chunks(_split_chunks(q_ref, nb), _split_chunks(k_ref, nb), _split_chunks(v_ref, nb), gbs, Ss,
                                    keep=kept)
        o_ref[...] = _join_chunks(outs)
        s_scr[...] = jnp.stack(new_S)
        for c in range(nb):
            for h in range(HEADS):
                t_ref[c, h] = kept[c * HEADS + h]

    row = lambda w, j=0: pl.BlockSpec((rows, w), lambda i: (i, j))
    return _host_call(
        body, "delta_fwd", (n,),
        [row(D_DN, 0), row(D_DN, 1), row(D_DN, 2), row(128)],
        [row(D_DN), pl.BlockSpec((1, HEADS, HD, HD), lambda i: (i, 0, 0, 0)),
         pl.BlockSpec((nb, HEADS, CHUNK, CHUNK), lambda i: (i, 0, 0, 0))],
        [jax.ShapeDtypeStruct((T, D_DN), F32), jax.ShapeDtypeStruct((n, HEADS, HD, HD), F32),
         jax.ShapeDtypeStruct((T // CHUNK, HEADS, CHUNK, CHUNK), F32)],
        [pltpu.VMEM((HEADS, HD, HD), F32)], [qkv, qkv, qkv, gb], rider)


def _delta_bwd(qkv, gb, ss, inv, do, nb, rider=None):
    T = qkv.shape[0]
    nb = min(nb, T // CHUNK)
    rows = nb * CHUNK
    n = T // rows

    def body(q_ref, k_ref, v_ref, gb_ref, ss_ref, t_ref, do_ref, d_ref, ds_scr):
        @pl.when(pl.program_id(0) == 0)
        def _():
            ds_scr[...] = jnp.zeros_like(ds_scr)

        Ss = [ss_ref[0, h] for h in range(HEADS)]
        gbs = [gb_ref[c * CHUNK:(c + 1) * CHUNK, :] for c in range(nb)]
        Ts = [t_ref[c, h] for c in range(nb) for h in range(HEADS)]
        _, vjp = jax.vjp(functools.partial(_delta_chunks, Ts=Ts), _split_chunks(q_ref, nb), _split_chunks(k_ref, nb),
                         _split_chunks(v_ref, nb), gbs, Ss)
        dqs, dks, dvs, dgbs, dSs = vjp((_split_chunks(do_ref, nb), [ds_scr[h] for h in range(HEADS)]))
        d_ref[...] = jnp.concatenate([_join_chunks(dqs), _join_chunks(dks), _join_chunks(dvs),
                                      jnp.concatenate(dgbs, axis=0)], axis=1)
        ds_scr[...] = jnp.stack(dSs)

    row = lambda w, j=0: pl.BlockSpec((rows, w), lambda i: (n - 1 - i, j))
    return _host_call(
        body, "delta_bwd", (n,),
        [row(D_DN, 0), row(D_DN, 1), row(D_DN, 2), row(128),
         pl.BlockSpec((1, HEADS, HD, HD), lambda i: (n - 1 - i, 0, 0, 0)),
         pl.BlockSpec((nb, HEADS, CHUNK, CHUNK), lambda i: (n - 1 - i, 0, 0, 0)), row(D_DN)],
        [row(3 * D_DN + 128)], [jax.ShapeDtypeStruct((T, 3 * D_DN + 128), F32)],
        [pltpu.VMEM((HEADS, HD, HD), F32)], [qkv, qkv, qkv, gb, ss, inv, do], rider)


def _cols(arr, width, index, first_row=0, block_row=None):
    return (arr, width, index, first_row, block_row)


def _rowwise(name, fn, tiled, consts, out_tiled, out_acc, tm, rows=None, rider=None, layer=None, prev=None):
    tiled = [t if isinstance(t, tuple) else (t, t.shape[-1], 0, 0, None) for t in tiled]
    T = tiled[0][0].shape[-2] if rows is None else rows
    tm = min(tm, T)
    assert T % tm == 0 and all(t[3] % tm == 0 for t in tiled)
    n_t, n_c, n_o, n_a = len(tiled), len(consts), len(out_tiled), len(out_acc)

    n_ri = len(rider.ins) if rider else 0
    n_ro = len(rider.outs) if rider else 0
    n_steps = T // tm

    def body(*refs):
        n_in = n_t + n_c + n_ri + (n_o if layer is not None else 0)
        r_ins = refs[n_t + n_c:n_t + n_c + n_ri]
        o_refs = refs[n_in:n_in + n_o]
        a_refs = refs[n_in + n_o:n_in + n_o + n_a]
        r_outs = refs[n_in + n_o + n_a:n_in + n_o + n_a + n_ro]
        sems = refs[n_in + n_o + n_a + n_ro:]
        if rider:
            @pl.when(pl.program_id(0) == 0)
            def _():
                rider.start(r_ins, r_outs, *sems)

        ins = [r[...] for r in refs[:n_t + n_c]]
        outs = fn(*ins)
        for r, val in zip(o_refs, outs[:n_o]):
            r[...] = val.astype(r.dtype)
        if n_a:
            @pl.when(pl.program_id(0) == 0)
            def _():
                for r in a_refs:
                    r[...] = jnp.zeros_like(r)
            for r, val in zip(a_refs, outs[n_o:]):
                r[...] += val
        if rider:
            @pl.when(pl.program_id(0) == n_steps - 1)
            def _():
                rider.finish(r_ins, r_outs, *sems)

    def const_spec(a):
        nd = a.ndim
        return pl.BlockSpec(a.shape, lambda i: (0,) * nd, pipeline_mode=pl.Buffered(1))

    def tile_spec(arr, w, j, r0, block_row):
        row = block_row if block_row is not None else (lambda i: i + r0 // tm)
        if arr.ndim == 3:
            return pl.BlockSpec((None, tm, w), lambda i: (layer, row(i), j))
        return pl.BlockSpec((tm, w), lambda i: (row(i), j))

    in_specs = [tile_spec(*t) for t in tiled]
    in_specs += [const_spec(a) for a in consts]
    if layer is None:
        out_specs = [pl.BlockSpec((tm, w), lambda i: (i, 0)) for (w, _) in out_tiled]
        out_shape = [jax.ShapeDtypeStruct((T, w), dt) for (w, dt) in out_tiled]
    else:
        out_specs = [pl.BlockSpec((None, tm, w), lambda i: (layer, i, 0)) for (w, _) in out_tiled]
        out_shape = [jax.ShapeDtypeStruct((DEPTH, T, w), dt) for (w, dt) in out_tiled]
    out_specs += [pl.BlockSpec(s, lambda i: (0, 0)) for (s, _) in out_acc]
    out_shape += [jax.ShapeDtypeStruct(s, dt) for (s, dt) in out_acc]
    operands = [t[0] for t in tiled] + list(consts)
    scratch = []
    aliases = {}
    if rider:
        anyspec = pl.BlockSpec(memory_space=pl.ANY)
        in_specs += [anyspec] * n_ri
        out_specs += [anyspec] * n_ro
        out_shape += list(rider.outs)
        operands += list(rider.ins)
        scratch = rider.scratch()
    n_prev = 0
    if layer is not None:
        assert rider is None and not out_acc
        if prev is None:
            prev = [jnp.zeros(o.shape, o.dtype) for o in out_shape]
        n_prev = len(prev)
        aliases = {len(operands) + i: i for i in range(n_prev)}
        in_specs += [pl.BlockSpec(memory_space=pl.ANY)] * n_prev
        operands += list(prev)
    return pl.pallas_call(
        body, name=name, grid=(n_steps,), in_specs=in_specs, out_specs=out_specs, out_shape=out_shape, scratch_shapes=scratch,
        input_output_aliases=aliases,
        compiler_params=pltpu.CompilerParams(dimension_semantics=("arbitrary",), vmem_limit_bytes=VMEM_LIMIT),
    )(*operands)


def _colsum(x):
    return jnp.sum(x, axis=0, keepdims=True)


def _rms(x):
    r = lax.rsqrt(jnp.mean(x * x, axis=-1, keepdims=True) + EPS)
    return x * r, r


def _rms_bwd(dxn, xn, r):
    return r * (dxn - xn * jnp.mean(dxn * xn, axis=-1, keepdims=True))


def _normproj_fwd(name, x, sh, sc, g, w, tm, rider=None, w_t=False, out_dtype=F32):
    def fn(x, sh, sc, g, w):
        xn, _ = _rms(x)
        h = (xn * (g * (1.0 + sc)) + sh).astype(BF16)
        return lax.dot_general(h, w, (((1,), (1 if w_t else 0,)), ((), ())), preferred_element_type=F32), h

    return _rowwise(name, fn, [x], [sh, sc, g, w], [(w.shape[0 if w_t else 1], out_dtype), (D, BF16)], [], tm, rider=rider)


def _normproj_bwd(name, x, dpre, dres, sc, g, w, tm, rider=None, w_t=False):
    def fn(x, dpre, dres, sc, g, w):
        xn, r = _rms(x)
        dh = lax.dot_general(dpre, w, (((1,), (0 if w_t else 1,)), ((), ())), preferred_element_type=F32)
        da = _colsum(dh * xn)
        dx = _rms_bwd(dh * (g * (1.0 + sc)), xn, r) + dres
        return dx, _colsum(dh), da * g, da * (1.0 + sc)

    vec = ((1, D), F32)
    return _rowwise(name, fn, [x, dpre, dres], [sc, g, w], [(D, F32)], [vec, vec, vec], tm, rider=rider)


def _stage2(a_b, blk, cp, u1c, qp, kp, vp, bb, lg, lb, alog, dtb):
    y_a = a_b * cp
    u1 = u1c + bb
    mu = jnp.mean(u1, axis=-1, keepdims=True)
    uc = u1 - mu
    var = jnp.mean(uc * uc, axis=-1, keepdims=True)
    y_b = _silu(uc * lax.rsqrt(var + LN_EPS) * lg + lb)

    def l2(t):
        t = _silu(t)
        return t * lax.rsqrt(jnp.sum(t * t, axis=-1, keepdims=True) + EPS)

    q = [l2(t) for t in qp]
    k = [l2(t) for t in kp]
    v = _silu(vp)
    lane = _iota2(blk.shape, 1)
    gdec = -jnp.exp(alog) * _softplus(blk + dtb)
    gb = jnp.where(lane < HEADS, gdec, jnp.where(lane < 2 * HEADS, _sigmoid(blk), 0.0))
    return y_a, y_b, q, k, v, gb


def _heads_of(x, base=0):
    return [x[:, base + h * HD:base + (h + 1) * HD] for h in range(HEADS)]


def _stage2_fwd(proj, convout, bb, lg, lb, alog, dtb, tm):
    def fn(a_b, blk, co, bb, lg, lb, alog, dtb):
        y_a, y_b, q, k, v, gb = _stage2(a_b, blk, co[:, 0:256], co[:, 256:512], _heads_of(co, 512), _heads_of(co, 1024),
                                        co[:, 1536:2048], bb, lg, lb, alog, dtb)
        return jnp.concatenate([y_a, y_b], axis=1), jnp.concatenate(q + k + [v], axis=1), gb

    return _rowwise("stage2_fwd", fn, [_cols(proj, 256, 0), _cols(proj, 128, C_GB // 128), convout],
                    [bb, lg, lb, alog, dtb], [(512, BF16), (1536, F32), (128, F32)], [], tm)


def _stage2_bwd(proj, convout, dy_ab, dq, dk, dv, dgb, bb, lg, lb, alog, dtb, tm, rider=None):
    def fn(a_b, blk, co, dy_ab, dq, dk, dv, dgb, bb, lg, lb, alog, dtb):
        args = (a_b, blk, co[:, 0:256], co[:, 256:512], _heads_of(co, 512), _heads_of(co, 1024), co[:, 1536:2048],
                bb, lg, lb, alog, dtb)
        _, vjp = jax.vjp(_stage2, *args)
        ct = (dy_ab[:, 0:256], dy_ab[:, 256:512], _heads_of(dq), _heads_of(dk), dv, dgb)
        da_b, dblk, dcp, du1c, dqp, dkp, dvp, dbb, dlg, dlb, dalog, ddtb = vjp(ct)
        dco = jnp.concatenate([dcp, du1c] + dqp + dkp + [dvp], axis=1)
        return dco, da_b, dblk, dbb, dlg, dlb, dalog, ddtb

    v256, v128 = ((1, 256), F32), ((1, 128), F32)
    return _rowwise("stage2_bwd", fn,
                    [_cols(proj, 256, 0), _cols(proj, 128, C_GB // 128), convout, dy_ab, dq, dk, dv, dgb],
                    [bb, lg, lb, alog, dtb], [(N_CONVCOL, F32), (256, F32), (128, F32)],
                    [v256, v256, v256, v128, v128], tm, rider=rider)


def _stage3(o, z, dng):
    ys = []
    for oh, zh in zip(o, z):
        on = oh * lax.rsqrt(jnp.mean(oh * oh, axis=-1, keepdims=True) + EPS)
        ys.append(on * dng * _silu(zh))
    return ys


def _outproj_fwd(x, o, proj, y_ab, g1, dng, wout, tm, rider=None):
    def fn(x, o, z0, z1, z2, z3, y_ab, g1, dng, wout):
        y_c = _stage3(_heads_of(o), [z0, z1, z2, z3], dng)
        ycat = jnp.concatenate([y_ab] + [t.astype(BF16) for t in y_c], axis=1)
        mix = jnp.dot(ycat, wout, preferred_element_type=F32)
        return x + g1 * mix, mix, ycat

    return _rowwise("outproj_fwd", fn, [x, o] + _z_heads(proj) + [y_ab],
                    [g1, dng, wout], [(D, F32), (D, F32), (D, BF16)], [], tm, rider=rider)


def _z_heads(proj):
    return [_cols(proj, HD, C_Z // HD + h) for h in range(HEADS)]


def _outproj_bwd(dx1, mix, o, proj, g1, dng, wout, tm, rider=None):
    def fn(dx1, mix, o, z0, z1, z2, z3, g1, dng, wout):
        dmix = (dx1 * g1).astype(BF16)
        dycat = lax.dot_general(dmix, wout, (((1,), (1,)), ((), ())), preferred_element_type=F32)
        _, vjp = jax.vjp(_stage3, _heads_of(o), [z0, z1, z2, z3], dng)
        do, dz, ddng = vjp(_heads_of(dycat, 512))
        return (dmix, dycat[:, 0:512], jnp.concatenate(do, axis=1), jnp.concatenate(dz, axis=1),
                _colsum(dx1 * mix), ddng)

    return _rowwise("outproj_bwd", fn, [dx1, mix, o] + _z_heads(proj), [g1, dng, wout],
                    [(D, BF16), (512, F32), (512, F32), (512, F32)], [((1, D), F32), ((1, HD), F32)], tm, rider=rider)


_CONV_BLOCKS = ((0, 256, KA), (256, 512, KB), (512, 2048, KC))
_CONV_STRIP = 256


_CONV_ROWS = 32


def _conv_inputs(proj_ref, rows):
    a_c, a_v = proj_ref[rows, C_AC:C_AC + 256], proj_ref[rows, C_AV:C_AV + 256]
    b_a, b_g = proj_ref[rows, C_BA:C_BA + 256], proj_ref[rows, C_BG:C_BG + 256]
    return a_c, a_v, b_a, _sigmoid(b_g)


def _shifted_copies(ext, phases, tm):
    n = tm + HALO - 8
    for b in range(1, 8):
        phases[b - 1] = ext[pl.ds(b, n), 256:512]


def _rows_at(ext, phases, row, col, kw):
    a, b = divmod(row, 8)
    if kw == KB and b:
        return phases[b - 1, pl.ds(8 * a, _CONV_ROWS), :]
    return ext[pl.ds(row, _CONV_ROWS), col:col + _CONV_STRIP]


def _conv_fwd(proj, wa, wb, wc, tm):
    T = proj.shape[0]
    tm = min(tm, T)

    def body(proj_ref, wa_ref, wb_ref, wc_ref, out_ref, ext, phases):
        @pl.when(pl.program_id(0) == 0)
        def _():
            ext[0:HALO, :] = jnp.zeros((HALO, N_CONVCOL), F32)

        a_c, a_v, b_a, sg = _conv_inputs(proj_ref, slice(None))
        ext[HALO:HALO + tm, 0:256] = a_c * a_v
        ext[HALO:HALO + tm, 256:512] = b_a * sg
        ext[HALO:HALO + tm, 512:2048] = proj_ref[:, C_Q:C_Q + 1536]
        _shifted_copies(ext, phases, tm)
        for r0 in range(0, tm, _CONV_ROWS):
            for (c0, c1, kw), w_ref in zip(_CONV_BLOCKS, (wa_ref, wb_ref, wc_ref)):
                for s0 in range(c0, c1, _CONV_STRIP):
                    acc = jnp.zeros((_CONV_ROWS, _CONV_STRIP), F32)
                    for k in range(kw):
                        acc += (w_ref[k:k + 1, s0 - c0:s0 - c0 + _CONV_STRIP]
                                * _rows_at(ext, phases, r0 + HALO - (kw - 1) + k, s0, kw))
                    out_ref[r0:r0 + _CONV_ROWS, s0:s0 + _CONV_STRIP] = acc
        ext[0:HALO, :] = ext[tm:tm + HALO, :]

    full = lambda a: pl.BlockSpec(a.shape, lambda i: (0, 0))
    return pl.pallas_call(
        body, name="conv_fwd", grid=(T // tm,),
        in_specs=[pl.BlockSpec((tm, IN_PAD), lambda i: (i, 0)), full(wa), full(wb), full(wc)],
        out_specs=pl.BlockSpec((tm, N_CONVCOL), lambda i: (i, 0)),
        out_shape=jax.ShapeDtypeStruct((T, N_CONVCOL), F32),
        scratch_shapes=[pltpu.VMEM((HALO + tm, N_CONVCOL), F32), pltpu.VMEM((7, tm + HALO - 8, 256), F32)],
        compiler_params=pltpu.CompilerParams(dimension_semantics=("arbitrary",), vmem_limit_bytes=VMEM_LIMIT),
    )(proj, wa, wb, wc)


def _conv_bwd(proj, dco, da_b, dz, dblk, wa, wb, wc, tm, rider=None):
    T = proj.shape[0]
    tm = min(tm, T)
    n = T // tm

    def body(proj_ref, dco_ref, dab_ref, dz_ref, dblk_ref, wa_ref, wb_ref, wc_ref,
             dproj_ref, dwa_ref, dwb_ref, dwc_ref, ext, acc_a, acc_b, acc_c, phases):
        @pl.when(pl.program_id(0) == 0)
        def _():
            ext[tm:tm + HALO, :] = jnp.zeros((HALO, N_CONVCOL), F32)
            acc_a[...] = jnp.zeros_like(acc_a)
            acc_b[...] = jnp.zeros_like(acc_b)
            acc_c[...] = jnp.zeros_like(acc_c)

        ext[0:tm, :] = dco_ref[...]
        _shifted_copies(ext, phases, tm)

        def taps(w_ref, acc_ref, kw, c0, wc0, xin, r0):
            dx = jnp.zeros((_CONV_ROWS, _CONV_STRIP), F32)
            for k in range(kw):
                sh = _rows_at(ext, phases, r0 + kw - 1 - k, c0, kw)
                dx += w_ref[k:k + 1, wc0:wc0 + _CONV_STRIP] * sh
                pr = sh * xin
                part = pr[0:8]
                for g in range(8, _CONV_ROWS, 8):
                    part += pr[g:g + 8]
                acc_ref[8 * k:8 * k + 8, wc0:wc0 + _CONV_STRIP] += part
            return dx

        for r0 in range(0, tm, _CONV_ROWS):
            rows = slice(r0, r0 + _CONV_ROWS)
            a_c, a_v, b_a, sg = _conv_inputs(proj_ref, rows)
            dp = taps(wa_ref, acc_a, KA, 0, 0, a_c * a_v, r0)
            dproj_ref[rows, C_AC:C_AC + 256] = (dp * a_v).astype(BF16)
            dproj_ref[rows, C_AV:C_AV + 256] = (dp * a_c).astype(BF16)
            du0 = taps(wb_ref, acc_b, KB, 256, 0, b_a * sg, r0)
            dproj_ref[rows, C_BA:C_BA + 256] = (du0 * sg).astype(BF16)
            dproj_ref[rows, C_BG:C_BG + 256] = (du0 * b_a * sg * (1.0 - sg)).astype(BF16)
            for s0 in range(0, 1536, _CONV_STRIP):
                dq = taps(wc_ref, acc_c, KC, 512 + s0, s0, proj_ref[rows, C_Q + s0:C_Q + s0 + _CONV_STRIP], r0)
                dproj_ref[rows, C_Q + s0:C_Q + s0 + _CONV_STRIP] = dq.astype(BF16)
        ext[tm:tm + HALO, :] = ext[0:HALO, :]

        dproj_ref[:, C_AB:C_AB + 256] = dab_ref[...].astype(BF16)
        dproj_ref[:, C_Z:C_Z + 512] = dz_ref[...].astype(BF16)
        dproj_ref[:, C_GB:C_GB + 128] = dblk_ref[...].astype(BF16)

        @pl.when(pl.program_id(0) == n - 1)
        def _():
            for acc_ref, dw_ref, kw in ((acc_a, dwa_ref, KA), (acc_b, dwb_ref, KB), (acc_c, dwc_ref, KC)):
                dw_ref[...] = jnp.zeros_like(dw_ref)
                for k in range(kw):
                    dw_ref[k:k + 1, :] = _colsum(acc_ref[8 * k:8 * k + 8, :])

    rev = lambda w: pl.BlockSpec((tm, w), lambda i: (n - 1 - i, 0))
    full = lambda a: pl.BlockSpec(a.shape, lambda i: (0, 0))
    return _host_call(
        body, "conv_bwd", (n,),
        [rev(IN_PAD), rev(N_CONVCOL), rev(256), rev(512), rev(128), full(wa), full(wb), full(wc)],
        [rev(IN_PAD), full(wa), full(wb), full(wc)],
        [jax.ShapeDtypeStruct((T, IN_PAD), BF16), jax.ShapeDtypeStruct(wa.shape, F32),
         jax.ShapeDtypeStruct(wb.shape, F32), jax.ShapeDtypeStruct(wc.shape, F32)],
        [pltpu.VMEM((tm + HALO, N_CONVCOL), F32), pltpu.VMEM((8 * KA, 256), F32),
         pltpu.VMEM((8 * KB, 256), F32), pltpu.VMEM((8 * KC, 1536), F32), pltpu.VMEM((7, tm + HALO - 8, 256), F32)],
        [proj, dco, da_b, dz, dblk, wa, wb, wc], rider)


def _ffn_out_fwd(x1, gu, g2, wfo, tm, rider=None):
    def fn(x1, gu, g2, wfo):
        s = (_silu(gu[:, :D_FF].astype(F32)) * gu[:, D_FF:].astype(F32)).astype(BF16)
        f = jnp.dot(s, wfo, preferred_element_type=F32)
        return x1 + g2 * f, f

    return _rowwise("ffnout_fwd", fn, [x1, gu], [g2, wfo], [(D, F32), (D, F32)], [], tm, rider=rider)


def _ffn_out_bwd(dx2, gu, f, g2, wfo, tm, rider=None):
    def fn(dx2, gu, f, g2, wfo):
        gate, up = gu[:, :D_FF].astype(F32), gu[:, D_FF:].astype(F32)
        sg = _sigmoid(gate)
        sl = gate * sg
        df = (dx2 * g2).astype(BF16)
        ds = lax.dot_general(df, wfo, (((1,), (1,)), ((), ())), preferred_element_type=F32)
        dgate = ds * up * (sg * (1.0 + gate * (1.0 - sg)))
        dgu = jnp.concatenate([dgate.astype(BF16), (ds * sl).astype(BF16)], axis=1)
        return dgu, sl * up, df, _colsum(dx2 * f)

    return _rowwise("ffnout_bwd", fn, [dx2, gu, f], [g2, wfo], [(2 * D_FF, BF16), (D_FF, BF16), (D, BF16)],
                    [((1, D), F32)], tm, rider=rider)


def _loss_bwd(x, tgt, gfin, tm):
    def fn(x, tgt, gfin):
        xn, r = _rms(x)
        e = xn * gfin - tgt
        loss = 0.5 * jnp.sum(jnp.mean(e * e, axis=-1, keepdims=True), axis=0, keepdims=True)
        dy = e * (1.0 / D)
        return _rms_bwd(dy * gfin, xn, r), jnp.broadcast_to(loss, (1, 128)), _colsum(dy * xn)

    return _rowwise("loss_bwd", fn, [x, tgt], [gfin], [(D, F32)], [((1, 128), F32), ((1, D), F32)], tm)


def _wgrad(name, a, b, bm, bk):
    T, M = a.shape
    N = b.shape[1]
    bk = min(bk, T)

    def body(a_ref, b_ref, o_ref):
        @pl.when(pl.program_id(1) == 0)
        def _():
            o_ref[...] = jnp.zeros_like(o_ref)

        o_ref[...] += lax.dot_general(a_ref[...], b_ref[...], (((0,), (0,)), ((), ())), preferred_element_type=F32)

    return pl.pallas_call(
        body, name=name, grid=(M // bm, T // bk),
        in_specs=[pl.BlockSpec((bk, bm), lambda i, k: (k, i)), pl.BlockSpec((bk, N), lambda i, k: (k, 0))],
        out_specs=pl.BlockSpec((bm, N), lambda i, k: (i, 0)),
        out_shape=jax.ShapeDtypeStruct((M, N), F32),
        compiler_params=pltpu.CompilerParams(dimension_semantics=("arbitrary", "arbitrary"), vmem_limit_bytes=VMEM_LIMIT),
    )(a, b)


IN_BLOCK = 512


def _wgrad_in_blocks(name, a, b, bm, bk):
    T, M = a.shape
    N = b.shape[1]
    bk = min(bk, T)
    nk = T // bk
    per = IN_COLS // N_DEV
    win = IN_BLOCK + 128

    def body(a_ref, b_ref, o_ref, acc):
        @pl.when(pl.program_id(1) == 0)
        def _():
            acc[...] = jnp.zeros_like(acc)

        acc[...] += lax.dot_general(a_ref[...], b_ref[...], (((0,), (0,)), ((), ())), preferred_element_type=F32)

        @pl.when(pl.program_id(1) == nk - 1)
        def _():
            for j in range(N_DEV):
                q, r = divmod(per * j, 128)
                w = acc[:, 128 * q:128 * q + win]
                if r:
                    w = pltpu.roll(w, win - r, axis=1)
                o_ref[j] = w[:, :IN_BLOCK]

    assert 128 * ((per * (N_DEV - 1)) // 128) + win <= N
    return pl.pallas_call(
        body, name=name, grid=(M // bm, nk),
        in_specs=[pl.BlockSpec((bk, bm), lambda i, k: (k, i)), pl.BlockSpec((bk, N), lambda i, k: (k, 0))],
        out_specs=pl.BlockSpec((N_DEV, bm, IN_BLOCK), lambda i, k: (0, i, 0)),
        out_shape=jax.ShapeDtypeStruct((N_DEV, M, IN_BLOCK), F32),
        scratch_shapes=[pltpu.VMEM((bm, N), F32)],
        compiler_params=pltpu.CompilerParams(dimension_semantics=("arbitrary", "arbitrary"), vmem_limit_bytes=VMEM_LIMIT),
    )(a, b)


def _adamw(w, g, m, v):
    m = B1 * m + (1.0 - B1) * g
    v = B2 * v + (1.0 - B2) * (g * g)
    m_hat = m / (1.0 - B1 ** STEP)
    v_hat = v / (1.0 - B2 ** STEP)
    return -LR * (m_hat / (jnp.sqrt(v_hat) + AEPS) + WD * w), m, v


def _adam_call(name, w, g, m, v, tm):
    C = w.shape[1]
    return _rowwise(name, _adamw, [w, g, m, v], [], [(C, F32)] * 3, [], tm)


def _adam_layer(name, g, w, m, v, tm, layer, prev):
    C = g.shape[1]
    return _rowwise(name, lambda g, w, m, v: (g,) + _adamw(w, g, m, v), [g, w, m, v], [], [(C, F32)] * 4, [], tm,
                    layer=layer, prev=prev)


def _reduce_sum(name, own, recv, tm):
    n, R, C = recv.shape
    flat = recv.reshape(n * R, C)
    fn = lambda own, r0, r1, r2: (((own + r0.astype(F32)) + r1.astype(F32)) + r2.astype(F32),)
    return _rowwise(name, fn, [_own_block(own, R, tm)] + [_cols(flat, C, 0, j * R) for j in range(n)], [], [(C, F32)], [],
                    tm, rows=R)[0]


def _pair_add(name, g, from_sib, tm):
    n, R, C = from_sib.shape
    n_r = R // tm
    mine = _cols(g.reshape(N_DEV * R, C), C, 0,
                 block_row=lambda i: (2 * (i // n_r) + lax.axis_index("c")) * n_r + i % n_r)

    def fn(a, b):
        p = a + b
        return p, p

    return _rowwise(name, fn, [mine, from_sib.reshape(n * R, C)], [], [(C, BF16), (C, F32)], [], tm, rows=n * R)


def _own_block(p32, R, tm):
    n_r = R // tm
    return _cols(p32, p32.shape[1], 0, block_row=lambda i: (2 * lax.axis_index("x") + lax.axis_index("y")) * n_r + i)


def _reduce_adam(name, own, recv, w, m, v, tm, layer, prev):
    n, R, C = recv.shape
    flat = recv.reshape(n * R, C)

    def fn(own, r0, r1, r2, w, m, v):
        g = ((own + r0.astype(F32)) + r1.astype(F32)) + r2.astype(F32)
        return (g,) + _adamw(w, g, m, v)

    return _rowwise(name, fn, [_own_block(own, R, tm)] + [_cols(flat, C, 0, j * R) for j in range(n)] + [w, m, v], [],
                    [(C, F32)] * 4, [], tm, rows=R, layer=layer, prev=prev)


_OFFSETS = [(dx, dy, dc) for dx in (0, 1) for dy in (0, 1) for dc in (0, 1)][1:]
_MESH = pl.DeviceIdType.MESH


def _coords():
    return lax.axis_index("x"), lax.axis_index("y"), lax.axis_index("c")


def _flip(me, off):
    return tuple((1 - m) if d else m for m, d in zip(me, off))


def _linear(p):
    return 4 * p[0] + 2 * p[1] + p[2]


_CHIP_FLIPS = ((1, 0), (0, 1), (1, 1))


class _Rider:
    def __init__(self, ins, outs, n_remote, n_local, start, finish):
        self.ins, self.outs, self.n_remote, self.n_local, self.start, self.finish = ins, outs, n_remote, n_local, start, finish

    def scratch(self):
        return [pltpu.SemaphoreType.DMA((self.n_remote,)), pltpu.SemaphoreType.DMA((self.n_remote,)),
                pltpu.SemaphoreType.DMA((max(self.n_local, 1),))]


def _run_rider(name, rider):
    def body(*refs):
        n_i, n_o = len(rider.ins), len(rider.outs)
        rider.start(refs[:n_i], refs[n_i:n_i + n_o], *refs[n_i + n_o:])
        rider.finish(refs[:n_i], refs[n_i:n_i + n_o], *refs[n_i + n_o:])

    anyspec = pl.BlockSpec(memory_space=pl.ANY)
    return pl.pallas_call(
        body, name=name, in_specs=[anyspec] * len(rider.ins), out_specs=[anyspec] * len(rider.outs),
        out_shape=list(rider.outs), scratch_shapes=rider.scratch(),
    )(*rider.ins)


def _gather_rider(arrs, layer):
    n = len(arrs)

    def parts(ins, outs, send, recv, loc):
        x, y, c = _coords()
        me, sib = (x, y, c), (x, y, 1 - c)
        chips = [((1 - x) if dx else x, (1 - y) if dy else y) for dx, dy in _CHIP_FLIPS]

        def copy(a, k, block, to, own=False):
            slot = outs[a].at[_linear(block)]
            return pltpu.make_async_remote_copy(src_ref=ins[a].at[layer] if own else slot, dst_ref=slot,
                                                send_sem=send.at[a * 7 + k], recv_sem=recv.at[a * 7 + k],
                                                device_id=to, device_id_type=_MESH)

        local = [pltpu.make_async_copy(ins[a].at[layer], outs[a].at[_linear(me)], loc.at[a]) for a in range(n)]
        first = []
        for a in range(n):
            first.append(copy(a, 0, me, sib, own=True))
            first += [copy(a, 1 + j, me, (*chip, c), own=True) for j, chip in enumerate(chips)]
        return copy, local, first, me, sib, chips, c

    def start(*refs):
        _, local, first, *_ = parts(*refs)
        for cp in local + first:
            cp.start()

    def finish(*refs):
        copy, local, first, me, sib, chips, c = parts(*refs)
        passed = []
        for j, chip in enumerate(chips):
            for a in range(n):
                copy(a, 1 + j, (*chip, c), me).wait_recv()
                cp = copy(a, 4 + j, (*chip, c), sib)
                cp.start()
                passed.append(cp)
        for a in range(n):
            copy(a, 0, sib, me).wait_recv()
            for j, chip in enumerate(chips):
                copy(a, 4 + j, (*chip, 1 - c), me).wait_recv()
        for cp in first + passed:
            cp.wait_send()
        for cp in local:
            cp.wait()

    outs = [jax.ShapeDtypeStruct((N_DEV,) + a.shape[1:], a.dtype) for a in arrs]
    return _Rider(list(arrs), outs, 7 * n, n, start, finish)


def _simple_rider(ins, outs, n_remote, make):
    def start(*refs):
        for cp in make(*refs):
            cp.start()

    def finish(*refs):
        for cp in make(*refs):
            cp.wait()

    return _Rider(ins, outs, n_remote, 0, start, finish)


def _pair_rider(arrs):
    def make(ins, outs, send, recv, loc):
        x, y, c = _coords()
        return [pltpu.make_async_remote_copy(src_ref=ins[a].at[2 * s_ + 1 - c], dst_ref=outs[a].at[s_], send_sem=send.at[4 * a + s_],
                                             recv_sem=recv.at[4 * a + s_], device_id=(x, y, 1 - c), device_id_type=_MESH)
                for a in range(len(arrs)) for s_ in range(4)]

    return _simple_rider(list(arrs), [jax.ShapeDtypeStruct((4,) + a.shape[1:], a.dtype) for a in arrs], 4 * len(arrs), make)


def _chip_rider(arrs):
    nf = len(_CHIP_FLIPS)

    def make(ins, outs, send, recv, loc):
        x, y, c = _coords()
        copies = []
        for a in range(len(arrs)):
            for k, (dx, dy) in enumerate(_CHIP_FLIPS):
                px, py = (1 - x) if dx else x, (1 - y) if dy else y
                copies.append(pltpu.make_async_remote_copy(
                    src_ref=ins[a].at[2 * px + py], dst_ref=outs[a].at[k], send_sem=send.at[a * nf + k],
                    recv_sem=recv.at[a * nf + k], device_id=(px, py, c), device_id_type=_MESH))
        return copies

    return _simple_rider(list(arrs), [jax.ShapeDtypeStruct((nf,) + a.shape[1:], a.dtype) for a in arrs], nf * len(arrs), make)


def _small_allgather(name, packed):
    R = packed.shape[0]

    def body(in_ref, all_ref, sum_ref, send, recv):
        me = _coords()
        my = _linear(me)
        all_ref[my] = in_ref[...]
        copies = []
        for k, off in enumerate(_OFFSETS):
            cp = pltpu.make_async_remote_copy(src_ref=in_ref, dst_ref=all_ref.at[my], send_sem=send.at[k], recv_sem=recv.at[k],
                                              device_id=_flip(me, off), device_id_type=_MESH)
            cp.start()
            copies.append(cp)
        for cp in copies:
            cp.wait()
        acc = all_ref[0]
        for j in range(1, N_DEV):
            acc = acc + all_ref[j]
        sum_ref[...] = acc

    vm = pl.BlockSpec(memory_space=pltpu.VMEM)
    return pl.pallas_call(
        body, name=name, in_specs=[vm], out_specs=[vm, vm],
        out_shape=[jax.ShapeDtypeStruct((N_DEV, R, 128), F32), jax.ShapeDtypeStruct((R, 128), F32)],
        scratch_shapes=[pltpu.SemaphoreType.DMA((len(_OFFSETS),)), pltpu.SemaphoreType.DMA((len(_OFFSETS),))],
        compiler_params=pltpu.CompilerParams(vmem_limit_bytes=VMEM_LIMIT),
    )(packed)


def _ada_mod(c, w16, bias):
    nc = w16.shape[2]
    kp = len(_OFFSETS)

    def body(c_ref, w_ref, b_ref, rows_ref, act_ref, cbuf, sbuf, send, recv):
        me = _coords()
        my = _linear(me)
        cbuf[my] = c_ref[...]
        copies = []
        for k, off in enumerate(_OFFSETS):
            cp = pltpu.make_async_remote_copy(src_ref=c_ref, dst_ref=cbuf.at[my], send_sem=send.at[k], recv_sem=recv.at[k],
                                              device_id=_flip(me, off), device_id_type=_MESH)
            cp.start()
            copies.append(cp)
        for cp in copies:
            cp.wait()
        act = _silu(jnp.concatenate([cbuf[j] for j in range(N_DEV)], axis=0))
        act_ref[...] = act
        act16 = act.astype(BF16)
        for l in range(DEPTH):
            ml = jnp.dot(act16, w_ref[l], preferred_element_type=F32) + b_ref[l:l + 1, :]
            for j in range(N_DEV):
                sbuf[j, l:l + 1, :] = ml[j:j + 1, :]
        rows_ref[my] = sbuf[my]
        copies = []
        for k, off in enumerate(_OFFSETS):
            peer = _flip(me, off)
            cp = pltpu.make_async_remote_copy(src_ref=sbuf.at[_linear(peer)], dst_ref=rows_ref.at[my], send_sem=send.at[kp + k],
                                              recv_sem=recv.at[kp + k], device_id=peer, device_id_type=_MESH)
            cp.start()
            copies.append(cp)
        for cp in copies:
            cp.wait()

    vm = pl.BlockSpec(memory_space=pltpu.VMEM)
    return pl.pallas_call(
        body, name="ada_mod", in_specs=[vm, vm, vm], out_specs=[vm, vm],
        out_shape=[jax.ShapeDtypeStruct((N_DEV, DEPTH, nc), F32), jax.ShapeDtypeStruct((N_DEV, D), F32)],
        scratch_shapes=[pltpu.VMEM((N_DEV, 1, D), F32), pltpu.VMEM((N_DEV, DEPTH, nc), F32),
                        pltpu.SemaphoreType.DMA((2 * kp,)), pltpu.SemaphoreType.DMA((2 * kp,))],
        compiler_params=pltpu.CompilerParams(vmem_limit_bytes=VMEM_LIMIT),
    )(c, w16, bias)


def _pack(arrs):
    parts = []
    for a in arrs:
        f = a.reshape(-1).astype(F32)
        parts.append(jnp.pad(f, (0, (-f.shape[0]) % 128)))
    flat = jnp.concatenate(parts)
    flat = jnp.pad(flat, (0, (-flat.shape[0]) % 1024))
    return flat.reshape(-1, 128)


def _unpack(packed, shapes):
    flat = packed.reshape(packed.shape[:-2] + (-1,))
    out, r = [], 0
    for s in shapes:
        n = int(np.prod(s))
        out.append(flat[..., r:r + n].reshape(packed.shape[:-2] + tuple(s)))
        r += -(-n // 128) * 128
    return out


def _pad_rows(w, rows):
    return jnp.pad(w, ((0, 0), (0, rows - w.shape[1]), (0, 0)))


_SMALL = ("b_ada", "norm_mix_g", "norm_ffn_g", "conv_a_w", "conf_dw_w", "conf_dw_b", "conf_ln_g", "conf_ln_b",
          "dn_conv_w", "dn_a_log", "dn_dt_bias", "dn_norm_g", "final_norm_g")
_BIG = ("w_in", "w_out", "w_ffn_in", "w_ffn_out")
_WEIGHTS = ("w_ada", "b_ada", "norm_mix_g", "norm_ffn_g", "w_in", "conv_a_w", "conf_dw_w", "conf_dw_b", "conf_ln_g",
            "conf_ln_b", "dn_conv_w", "dn_a_log", "dn_dt_bias", "dn_norm_g", "w_out", "w_ffn_in", "w_ffn_out",
            "final_norm_g")


def _step(x, c, loss_target, W, M, V):
    T = x.shape[1]
    me = _linear(_coords())
    cc = lax.axis_index("c")
    chip = 2 * lax.axis_index("x") + lax.axis_index("y")
    xs, tgt = x[0], loss_target[0]
    vec = lambda a: a.reshape(1, -1)

    nada = W["w_ada"].shape[2]
    rows, act_all = _ada_mod(c, W["w_ada"].astype(BF16), lax.dynamic_slice(W["b_ada"], (0, me * nada), (DEPTH, nada)))
    mod = rows.transpose(1, 0, 2).reshape(DEPTH, 6, 1, D)

    w16 = {k: W[k].astype(BF16) for k in _BIG}
    w16["w_ffn_in"] = W["w_ffn_in"].transpose(0, 2, 1).astype(BF16)

    def whole(g_in=None, g_out=None, g_fi=None, g_fo=None):
        out = {}
        if g_in is not None:
            out["w_in"] = jnp.pad(g_in.transpose(1, 0, 2).reshape(D, IN_COLS), ((0, 0), (0, IN_PAD - IN_COLS)))
        if g_out is not None:
            out["w_out"] = g_out.reshape(D, D)
        if g_fi is not None:
            out["w_ffn_in"] = g_fi.reshape(2 * D_FF, D)
        if g_fo is not None:
            out["w_ffn_out"] = g_fo.reshape(D_FF, D)
        return out

    wts = [dict() for _ in range(DEPTH)]
    gather = lambda names, layer: _gather_rider([w16[k] for k in names], layer)
    wts[0].update(whole(g_in=_run_rider("gather_weights", gather(["w_in"], 0))[0]))
    conv_names = ("conv_a_w", "conf_dw_w", "dn_conv_w")
    conv_all, _ = _small_allgather("gather_conv_w", _pack([W[k] for k in conv_names]))
    conv_full = [t.transpose(1, 2, 0, 3).reshape(t.shape[1], t.shape[2], -1)
                 for t in _unpack(conv_all, [W[k].shape for k in conv_names])]
    wa, wb, wc = _pad_rows(conv_full[0], 8), _pad_rows(conv_full[1], 32), _pad_rows(conv_full[2], 8)
    lane_pad = lambda a: jnp.pad(a, ((0, 0), (0, 128 - a.shape[1])))
    alog, dtb = lane_pad(W["dn_a_log"]), lane_pad(W["dn_dt_bias"])

    saved = []
    xc = xs
    for l in range(DEPTH):
        more = l + 1 < DEPTH
        sh1, sc1, g1, sh2, sc2, g2 = [mod[l, i] for i in range(6)]
        proj, h, *got = _normproj_fwd("inproj_fwd", xc, sh1, sc1, vec(W["norm_mix_g"][l]), wts[l]["w_in"], 512,
                                      rider=gather(["w_out", "w_ffn_out"], 0) if l == 0 else None)
        if l == 0:
            wts[0].update(whole(g_out=got[0], g_fo=got[1]))
        convout = _conv_fwd(proj, wa[l], wb[l], wc[l], 256)
        y_ab, qkv, gb = _stage2_fwd(proj, convout, vec(W["conf_dw_b"][l]), vec(W["conf_ln_g"][l]), vec(W["conf_ln_b"][l]),
                                    alog[l:l + 1], dtb[l:l + 1], 256)
        o, ss, inv, *got = _delta_fwd(qkv, gb, DELTA_NB,
                                 rider=gather(["w_ffn_in"], 0) if l == 0 else gather(["w_in"], l + 1) if more else None)
        if l == 0:
            wts[0].update(whole(g_fi=got[0]))
        elif more:
            wts[l + 1].update(whole(g_in=got[0]))
        x1, mix, ycat, *got = _outproj_fwd(xc, o, proj, y_ab, g1, vec(W["dn_norm_g"][l]), wts[l]["w_out"], 512,
                                           rider=gather(["w_in"], 1) if l == 0 else None)
        if l == 0:
            wts[1].update(whole(g_in=got[0]))
        gu, h2, *got = _normproj_fwd("ffnin_fwd", x1, sh2, sc2, vec(W["norm_ffn_g"][l]), wts[l]["w_ffn_in"], 256,
                                     rider=gather(["w_ffn_in"], l + 1) if more else None, w_t=True, out_dtype=BF16)
        if more:
            wts[l + 1].update(whole(g_fi=got[0]))
        x2, f, *got = _ffn_out_fwd(x1, gu, g2, wts[l]["w_ffn_out"], 256,
                                   rider=gather(["w_ffn_out", "w_out"], l + 1) if more else None)
        if more:
            wts[l + 1].update(whole(g_fo=got[0], g_out=got[1]))
        saved.append((xc, proj, h, convout, qkv, gb, o, ss, inv, mix, ycat, x1, gu, h2, f))
        xc = x2

    dx, loss_row, d_gfin = _loss_bwd(xc, tgt, vec(W["final_norm_g"]), 512)
    loss = lax.psum(loss_row[0, 0], ("x", "y", "c"))

    big_out = {k: None for k in _BIG}
    dmod, small = [None] * DEPTH, [None] * DEPTH
    blocks = lambda g: g.reshape(N_DEV, -1, g.shape[-1])
    pair_tm = {"w_in": 512, "w_out": 128, "w_ffn_in": 704, "w_ffn_out": 352}
    sum_tm = {"w_in": 256, "w_out": 128, "w_ffn_in": 176, "w_ffn_out": 176}
    turned = lambda a: a.transpose(0, 2, 1)
    wmv = {k: (W[k], M[k], V[k]) for k in _BIG}
    wmv["w_ffn_in"] = tuple(turned(a) for a in wmv["w_ffn_in"])

    def pair_sum(k, g, from_sib):
        p16, p32 = _pair_add("pair_add_" + k, g, from_sib, pair_tm[k])
        return p16.reshape(from_sib.shape), p32

    def finish(k, layer, p32, r):
        if k == "w_in":
            g = _reduce_sum("reduce_" + k, p32, r, sum_tm[k])[:, :IN_COLS // N_DEV]
            big_out[k] = _adam_layer("adam_" + k, g, *wmv[k], 256, layer, big_out[k])
        else:
            big_out[k] = _reduce_adam("reduce_adam_" + k, p32, r, *wmv[k], sum_tm[k], layer, big_out[k])

    above = None
    for l in reversed(range(DEPTH)):
        xc, proj, h, convout, qkv, gb, o, ss, inv, mix, ycat, x1, gu, h2, f = saved[l]
        sh1, sc1, g1, sh2, sc2, g2 = [mod[l, i] for i in range(6)]
        gm, gf = vec(W["norm_mix_g"][l]), vec(W["norm_ffn_g"][l])
        bb, lg, lb = vec(W["conf_dw_b"][l]), vec(W["conf_ln_g"][l]), vec(W["conf_ln_b"][l])
        dng = vec(W["dn_norm_g"][l])
        wl = wts[l]

        dgu, s, df, d_g2, *got = _ffn_out_bwd(dx, gu, f, g2, wl["w_ffn_out"], 256,
                                              rider=_pair_rider([above[1]]) if above else None)
        if above:
            in16, in_own = pair_sum("w_in", above[1], got[0])
        gw_fo = _wgrad("wgrad_ffn_out", s, df, 1408, 1024)
        dx1, d_sh2, d_sc2, d_gf, *got = _normproj_bwd("ffnin_bwd", x1, dgu, dx, sc2, gf, wl["w_ffn_in"], 256,
                                                      rider=_chip_rider([in16]) if above else None, w_t=True)
        if above:
            finish("w_in", above[0], in_own, got[0])
        gw_fi = _wgrad("wgrad_ffn_in", dgu, h2, 1408, 1024)
        gw_fo, gw_fi = blocks(gw_fo), blocks(gw_fi)
        dmix, dy_ab, do, dz, d_g1, d_dng, *got = _outproj_bwd(dx1, mix, o, proj, g1, dng, wl["w_out"], 512,
                                                               rider=_pair_rider([gw_fo, gw_fi]))
        fo16, fo_own = pair_sum("w_ffn_out", gw_fo, got[0])
        fi16, fi_own = pair_sum("w_ffn_in", gw_fi, got[1])
        gw_out = blocks(_wgrad("wgrad_out", ycat, dmix, 512, 2048))
        dqkv, *got = _delta_bwd(qkv, gb, ss, inv, do, DELTA_NB, rider=_chip_rider([fo16, fi16]))
        finish("w_ffn_out", l, fo_own, got[0])
        finish("w_ffn_in", l, fi_own, got[1])
        dco, da_b, dblk, d_bb, d_lg, d_lb, d_alog, d_dtb, *got = _stage2_bwd(
            proj, convout, dy_ab, _cols(dqkv, D_DN, 0), _cols(dqkv, D_DN, 1), _cols(dqkv, D_DN, 2),
            _cols(dqkv, 128, 3 * D_DN // 128), bb, lg, lb, alog[l:l + 1], dtb[l:l + 1], 256,
            rider=_pair_rider([gw_out]))
        out16, out_own = pair_sum("w_out", gw_out, got[0])
        dproj, d_wa, d_wb, d_wc, *got = _conv_bwd(proj, dco, da_b, dz, dblk, wa[l], wb[l], wc[l], 256,
                                                  rider=_chip_rider([out16]))
        finish("w_out", l, out_own, got[0])
        dx, d_sh1, d_sc1, d_gm = _normproj_bwd("inproj_bwd", xc, dproj, dx1, sc1, gm, wl["w_in"], 512)
        above = (l, _wgrad_in_blocks("wgrad_in", h, dproj, 512, 1024))

        dmod[l] = jnp.concatenate([d_sh1, d_sc1, d_g1, d_sh2, d_sc2, d_g2], axis=1)
        small[l] = dict(norm_mix_g=d_gm, norm_ffn_g=d_gf, conv_a_w=d_wa[:KA], conf_dw_w=d_wb[:KB], conf_dw_b=d_bb,
                        conf_ln_g=d_lg, conf_ln_b=d_lb, dn_conv_w=d_wc[:KC], dn_a_log=d_alog, dn_dt_bias=d_dtb,
                        dn_norm_g=d_dng)

    in16, in_own = pair_sum("w_in", above[1], _run_rider("pair_exchange", _pair_rider([above[1]]))[0])
    finish("w_in", above[0], in_own, _run_rider("chip_exchange", _chip_rider([in16]))[0])

    names = ("norm_mix_g", "norm_ffn_g", "conv_a_w", "conf_dw_w", "conf_dw_b", "conf_ln_g", "conf_ln_b", "dn_conv_w",
             "dn_a_log", "dn_dt_bias", "dn_norm_g")
    pieces = [jnp.stack(dmod)] + [jnp.stack([small[l][k] for l in range(DEPTH)]) for k in names] + [d_gfin]
    shapes = [p.shape for p in pieces]
    every, total = _small_allgather("gather_small_grads", _pack(pieces))
    tot = dict(zip(("dmod",) + names + ("final_norm_g",), _unpack(total, shapes)))
    dmod_all = _unpack(every, shapes[:1])[0]

    grads = {}
    grads["b_ada"] = tot["dmod"].reshape(DEPTH, 6 * D)
    for k in ("norm_mix_g", "norm_ffn_g", "conf_dw_b", "conf_ln_g", "conf_ln_b", "dn_norm_g"):
        grads[k] = tot[k].reshape(W[k].shape)
    grads["dn_a_log"] = tot["dn_a_log"].reshape(DEPTH, 128)[:, :HEADS]
    grads["dn_dt_bias"] = tot["dn_dt_bias"].reshape(DEPTH, 128)[:, :HEADS]
    grads["final_norm_g"] = tot["final_norm_g"].reshape(D)
    for k in conv_names:
        nloc = W[k].shape[2]
        grads[k] = lax.dynamic_slice_in_dim(tot[k], me * nloc, nloc, axis=2)

    dm = lax.dynamic_slice_in_dim(dmod_all.reshape(N_DEV, DEPTH, 6 * D), me * nada, nada, axis=2)
    pad16 = lambda a: jnp.pad(a, ((0, 16 - N_DEV), (0, 0))).astype(BF16)
    g_ada = _wgrad("wgrad_ada", pad16(act_all), pad16(dm.reshape(N_DEV, DEPTH * nada)), 256, 16)
    grads["w_ada"] = g_ada.reshape(D, DEPTH, nada).transpose(1, 0, 2)

    delta, new_m, new_v = {}, {}, {}
    r2 = lambda a: a.reshape(DEPTH * D, nada)
    d_, m_, v_ = _adam_call("adam_ada", r2(W["w_ada"]), r2(grads["w_ada"]), r2(M["w_ada"]), r2(V["w_ada"]), 512)
    delta["w_ada"], new_m["w_ada"], new_v["w_ada"] = [t.reshape(W["w_ada"].shape) for t in (d_, m_, v_)]
    sshapes = [W[k].shape for k in _SMALL]
    d_, m_, v_ = _adam_call("adam_small", _pack([W[k] for k in _SMALL]), _pack([grads[k] for k in _SMALL]),
                            _pack([M[k] for k in _SMALL]), _pack([V[k] for k in _SMALL]), 4096)
    for dst, packed in ((delta, d_), (new_m, m_), (new_v, v_)):
        dst.update(zip(_SMALL, _unpack(packed, sshapes)))
    for k in _BIG:
        grads[k], delta[k], new_m[k], new_v[k] = [turned(a) for a in big_out[k]] if k == "w_ffn_in" else big_out[k]

    return (loss, dx[None], *[grads[k] for k in _WEIGHTS], *[delta[k] for k in _WEIGHTS],
            *[new_m[k] for k in _WEIGHTS], *[new_v[k] for k in _WEIGHTS])


def kernel(x, c, w_ada, b_ada, norm_mix_g, norm_ffn_g, w_in, conv_a_w, conf_dw_w, conf_dw_b, conf_ln_g, conf_ln_b, dn_conv_w, dn_a_log, dn_dt_bias, dn_norm_g, w_out, w_ffn_in, w_ffn_out, final_norm_g, loss_target, m_w_ada, m_b_ada, m_norm_mix_g, m_norm_ffn_g, m_w_in, m_conv_a_w, m_conf_dw_w, m_conf_dw_b, m_conf_ln_g, m_conf_ln_b, m_dn_conv_w, m_dn_a_log, m_dn_dt_bias, m_dn_norm_g, m_w_out, m_w_ffn_in, m_w_ffn_out, m_final_norm_g, v_w_ada, v_b_ada, v_norm_mix_g, v_norm_ffn_g, v_w_in, v_conv_a_w, v_conf_dw_w, v_conf_dw_b, v_conf_ln_g, v_conf_ln_b, v_dn_conv_w, v_dn_a_log, v_dn_dt_bias, v_dn_norm_g, v_w_out, v_w_ffn_in, v_w_ffn_out, v_final_norm_g):
    a = dict(locals())
    W = {k: a[k] for k in _WEIGHTS}
    M = {k: a["m_" + k] for k in _WEIGHTS}
    V = {k: a["v_" + k] for k in _WEIGHTS}
    return _step(x, c, loss_target, W, M, V)
```

```python
import functools

import jax
import jax.numpy as jnp
import numpy as np
from jax import lax
from jax.experimental import pallas as pl
from jax.experimental.pallas import tpu as pltpu

F32 = jnp.float32
BF16 = jnp.bfloat16

N_DEV = 8
D = 1024
DEPTH = 4
D_CONV = 256
D_CONF = 256
D_DN = 512
HEADS = 4
HD = 128
KA, KB, KC = 3, 31, 4
CHUNK = 64
D_FF = 2816
IN_COLS = 3336
IN_PAD = 3456
N_CONVCOL = 2048
EPS = 1e-6
LN_EPS = 1e-5
HALO = 32
VMEM_LIMIT = 56 * 1024 * 1024
DELTA_NB = 8

C_AB, C_AC, C_AV, C_BA, C_BG, C_Q, C_Z, C_GB = 0, 256, 512, 768, 1024, 1280, 2816, 3328

LR, B1, B2, AEPS, WD, STEP = 0.001, 0.9, 0.999, 1e-08, 0.01, 10


def _dot(a, b, dims, hi):
    if hi:
        return lax.dot_general(a.astype(F32), b.astype(F32), (dims, ((), ())), precision=lax.Precision.HIGHEST,
                               preferred_element_type=F32)
    return lax.dot_general(a.astype(BF16), b.astype(BF16), (dims, ((), ())), preferred_element_type=F32)


@functools.partial(jax.custom_vjp, nondiff_argnums=(2,))
def mm_nn(a, b, hi=False):
    return _dot(a, b, ((1,), (0,)), hi)


@functools.partial(jax.custom_vjp, nondiff_argnums=(2,))
def mm_nt(a, b, hi=False):
    return _dot(a, b, ((1,), (1,)), hi)


@functools.partial(jax.custom_vjp, nondiff_argnums=(2,))
def mm_tn(a, b, hi=False):
    return _dot(a, b, ((0,), (0,)), hi)


mm_nn.defvjp(lambda a, b, hi: (mm_nn(a, b, hi), (a, b)),
             lambda hi, r, g: (mm_nt(g, r[1], hi), mm_tn(r[0], g, hi)))
mm_nt.defvjp(lambda a, b, hi: (mm_nt(a, b, hi), (a, b)),
             lambda hi, r, g: (mm_nn(g, r[1], hi), mm_tn(g, r[0], hi)))
mm_tn.defvjp(lambda a, b, hi: (mm_tn(a, b, hi), (a, b)),
             lambda hi, r, g: (mm_nt(r[1], g, hi), mm_nn(r[0], g, hi)))


def _sigmoid(x):
    return 1.0 / (1.0 + jnp.exp(-x))


def _silu(x):
    return x * _sigmoid(x)


def _softplus(x):
    return jnp.maximum(x, 0.0) + jnp.log(1.0 + jnp.exp(-jnp.abs(x)))


def _iota2(shape, dim):
    return lax.broadcasted_iota(jnp.int32, shape, dim)


def _dot16(a, b):
    return jnp.dot(a.astype(BF16), b.astype(BF16), preferred_element_type=F32)


def _dot_3pass(a, b):
    ah = a.astype(BF16)
    bh = b.astype(BF16)
    al = (a - ah.astype(F32)).astype(BF16)
    bl = (b - bh.astype(F32)).astype(BF16)
    d = lambda x, y: jnp.dot(x, y, preferred_element_type=F32)
    return d(ah, bh) + (d(ah, bl) + d(al, bh))


@jax.custom_vjp
def _unit_lower_inverses(Xs):
    n = Xs[0].shape[0]
    r, c = _iota2((n, n), 0), _iota2((n, n), 1)
    eye = (r == c).astype(F32)

    def joins(b):
        s = b.bit_length() - 1
        return ((r >> (s + 1)) == (c >> (s + 1))) & (((r >> s) & 1) == 1) & (((c >> s) & 1) == 0)

    Ts = [eye + jnp.where(joins(1), x, 0.0) for x in Xs]
    b = 2
    while b < n:
        m = joins(b)
        Ys = [_dot16(jnp.where(m, x, 0.0), t) for x, t in zip(Xs, Ts)]
        Ts = [t + _dot16(t, y) for t, y in zip(Ts, Ys)]
        b *= 2
    Rs = [(eye - t) + _dot_3pass(x, t) for x, t in zip(Xs, Ts)]
    return [t + _dot16(t, r_) for t, r_ in zip(Ts, Rs)]


def _unit_lower_inverses_fwd(Xs):
    Ts = _unit_lower_inverses(Xs)
    return Ts, Ts


def _unit_lower_inverses_bwd(Ts, gs):
    inner = [mm_nt(g, t) for g, t in zip(gs, Ts)]
    return ([mm_tn(t, i) for t, i in zip(Ts, inner)],)


_unit_lower_inverses.defvjp(_unit_lower_inverses_fwd, _unit_lower_inverses_bwd)


@jax.custom_vjp
def _saved_inverses(Xs, Ts):
    return list(Ts)


_saved_inverses.defvjp(lambda Xs, Ts: (list(Ts), Ts),
                       lambda Ts, gs: (_unit_lower_inverses_bwd(Ts, gs)[0], [jnp.zeros_like(t) for t in Ts]))


def _delta_chunks(qs, ks, vs, gbs, Ss, Ts=None, keep=None):
    C = CHUNK
    nb = len(gbs)
    pairs = [(c, h) for c in range(nb) for h in range(HEADS)]
    each = lambda fn, *lists: [fn(*a) for a in zip(*lists)]
    row = _iota2((C, C), 0)
    col = _iota2((C, C), 1)
    causal = row >= col
    strict = row > col
    tri = causal.astype(F32)
    eye = (row == col).astype(F32)
    lane = _iota2((C, 128), 1)
    subl = _iota2((128, C), 0)
    last = (_iota2((C, 1), 0) == C - 1).astype(F32)

    gc_all = [mm_nn(tri, gb, True) for gb in gbs]
    gc_t = [g.T for g in gc_all]
    q = [qs[c][h] * (HD ** -0.5) for c, h in pairs]
    k = [ks[c][h] for c, h in pairs]
    v = [vs[c][h] for c, h in pairs]
    gcol = [jnp.sum(jnp.where(lane == h, gc_all[c], 0.0), axis=1, keepdims=True) for c, h in pairs]
    grow = [jnp.sum(jnp.where(subl == h, gc_t[c], 0.0), axis=0, keepdims=True) for c, h in pairs]
    beta = [jnp.sum(jnp.where(lane == HEADS + h, gbs[c], 0.0), axis=1, keepdims=True) for c, h in pairs]
    decay = each(lambda a, b: jnp.where(causal, jnp.exp(jnp.where(causal, a - b, 0.0)), 0.0), gcol, grow)
    kb = each(lambda a, b: a * b, k, beta)
    vb = each(lambda a, b: a * b, v, beta)
    kk = each(lambda a, b: mm_nt(a, b), kb, k)
    X = each(lambda a, d: -jnp.where(strict, a * d, 0.0), kk, decay)
    T = _unit_lower_inverses(X) if Ts is None else _saved_inverses(X, Ts)
    if keep is not None:
        keep.extend(T)
    eg = [jnp.exp(g) for g in gcol]
    u = each(lambda t, a: mm_nn(t, a), T, vb)
    w = each(lambda t, a, e: mm_nn(t, a * e), T, kb, eg)
    qk = each(lambda a, b, d: jnp.where(causal, mm_nt(a, b) * d, 0.0), q, k, decay)
    qg = each(lambda a, e: a * e, q, eg)
    g_last = [jnp.sum(g * last, axis=0, keepdims=True) for g in gcol]
    kd = each(lambda a, gl, g: a * jnp.exp(gl - g), k, g_last, gcol)
    eg_last = [jnp.exp(g) for g in g_last]

    outs = []
    for c in range(nb):
        sl = slice(c * HEADS, (c + 1) * HEADS)
        v_new = each(lambda a, b, S: a - mm_nn(b, S), u[sl], w[sl], Ss)
        oS = each(lambda a, S: mm_nn(a, S), qg[sl], Ss)
        outs.append(each(lambda a, b, n: a + mm_nn(b, n), oS, qk[sl], v_new))
        Ss = each(lambda S, e, a, n: S * e + mm_tn(a, n), Ss, eg_last[sl], kd[sl], v_new)
    return outs, Ss


def _split_chunks(ref, nb):
    return [[ref[c * CHUNK:(c + 1) * CHUNK, h * HD:(h + 1) * HD] for h in range(HEADS)] for c in range(nb)]


def _join_chunks(vals):
    return jnp.concatenate([jnp.concatenate(heads, axis=1) for heads in vals], axis=0)


def _hosted(body, n_in, n_out, rider, n_steps):
    if rider is None:
        return body
    n_ri, n_ro = len(rider.ins), len(rider.outs)

    def wrapped(*refs):
        ins, r_ins = refs[:n_in], refs[n_in:n_in + n_ri]
        outs = refs[n_in + n_ri:n_in + n_ri + n_out]
        r_outs = refs[n_in + n_ri + n_out:n_in + n_ri + n_out + n_ro]
        rest = refs[n_in + n_ri + n_out + n_ro:]
        scr, sems = rest[:len(rest) - 3], rest[len(rest) - 3:]

        @pl.when(pl.program_id(0) == 0)
        def _():
            rider.start(r_ins, r_outs, *sems)

        body(*ins, *outs, *scr)

        @pl.when(pl.program_id(0) == n_steps - 1)
        def _():
            rider.finish(r_ins, r_outs, *sems)

    return wrapped


def _host_call(body, name, grid, in_specs, out_specs, out_shape, scratch, operands, rider):
    out_specs, out_shape = list(out_specs), list(out_shape)
    n_in, n_out = len(in_specs), len(out_specs)
    if rider is not None:
        anyspec = pl.BlockSpec(memory_space=pl.ANY)
        in_specs = list(in_specs) + [anyspec] * len(rider.ins)
        out_specs += [anyspec] * len(rider.outs)
        out_shape += list(rider.outs)
        scratch = list(scratch) + rider.scratch()
        operands = list(operands) + list(rider.ins)
    return pl.pallas_call(
        _hosted(body, n_in, n_out, rider, grid[0]), name=name, grid=grid, in_specs=in_specs, out_specs=out_specs,
        out_shape=out_shape, scratch_shapes=scratch,
        compiler_params=pltpu.CompilerParams(dimension_semantics=("arbitrary",), vmem_limit_bytes=VMEM_LIMIT),
    )(*operands)


def _delta_fwd(qkv, gb, nb, rider=None):
    T = qkv.shape[0]
    nb = min(nb, T // CHUNK)
    rows = nb * CHUNK
    n = T // rows

    def body(q_ref, k_ref, v_ref, gb_ref, o_ref, ss_ref, t_ref, s_scr):
        @pl.when(pl.program_id(0) == 0)
        def _():
            s_scr[...] = jnp.zeros_like(s_scr)

        Ss = [s_scr[h] for h in range(HEADS)]
        for h in range(HEADS):
            ss_ref[0, h] = Ss[h]
        gbs = [gb_ref[c * CHUNK:(c + 1) * CHUNK, :] for c in range(nb)]
        kept = []
        outs, new_S = _delta_chunks(_split_chunks(q_ref, nb), _split_chunks(k_ref, nb), _split_chunks(v_ref, nb), gbs, Ss,
                                    keep=kept)
        o_ref[...] = _join_chunks(outs)
        s_scr[...] = jnp.stack(new_S)
        for c in range(nb):
            for h in range(HEADS):
                t_ref[c, h] = kept[c * HEADS + h]

    row = lambda w, j=0: pl.BlockSpec((rows, w), lambda i: (i, j))
    return _host_call(
        body, "delta_fwd", (n,),
        [row(D_DN, 0), row(D_DN, 1), row(D_DN, 2), row(128)],
        [row(D_DN), pl.BlockSpec((1, HEADS, HD, HD), lambda i: (i, 0, 0, 0)),
         pl.BlockSpec((nb, HEADS, CHUNK, CHUNK), lambda i: (i, 0, 0, 0))],
        [jax.ShapeDtypeStruct((T, D_DN), F32), jax.ShapeDtypeStruct((n, HEADS, HD, HD), F32),
         jax.ShapeDtypeStruct((T // CHUNK, HEADS, CHUNK, CHUNK), F32)],
        [pltpu.VMEM((HEADS, HD, HD), F32)], [qkv, qkv, qkv, gb], rider)


def _delta_bwd(qkv, gb, ss, inv, do, nb, rider=None):
    T = qkv.shape[0]
    nb = min(nb, T // CHUNK)
    rows = nb * CHUNK
    n = T // rows

    def body(q_ref, k_ref, v_ref, gb_ref, ss_ref, t_ref, do_ref, d_ref, ds_scr):
        @pl.when(pl.program_id(0) == 0)
        def _():
            ds_scr[...] = jnp.zeros_like(ds_scr)

        Ss = [ss_ref[0, h] for h in range(HEADS)]
        gbs = [gb_ref[c * CHUNK:(c + 1) * CHUNK, :] for c in range(nb)]
        Ts = [t_ref[c, h] for c in range(nb) for h in range(HEADS)]
        _, vjp = jax.vjp(functools.partial(_delta_chunks, Ts=Ts), _split_chunks(q_ref, nb), _split_chunks(k_ref, nb),
                         _split_chunks(v_ref, nb), gbs, Ss)
        dqs, dks, dvs, dgbs, dSs = vjp((_split_chunks(do_ref, nb), [ds_scr[h] for h in range(HEADS)]))
        d_ref[...] = jnp.concatenate([_join_chunks(dqs), _join_chunks(dks), _join_chunks(dvs),
                                      jnp.concatenate(dgbs, axis=0)], axis=1)
        ds_scr[...] = jnp.stack(dSs)

    row = lambda w, j=0: pl.BlockSpec((rows, w), lambda i: (n - 1 - i, j))
    return _host_call(
        body, "delta_bwd", (n,),
        [row(D_DN, 0), row(D_DN, 1), row(D_DN, 2), row(128),
         pl.BlockSpec((1, HEADS, HD, HD), lambda i: (n - 1 - i, 0, 0, 0)),
         pl.BlockSpec((nb, HEADS, CHUNK, CHUNK), lambda i: (n - 1 - i, 0, 0, 0)), row(D_DN)],
        [row(3 * D_DN + 128)], [jax.ShapeDtypeStruct((T, 3 * D_DN + 128), F32)],
        [pltpu.VMEM((HEADS, HD, HD), F32)], [qkv, qkv, qkv, gb, ss, inv, do], rider)


def _cols(arr, width, index, first_row=0, block_row=None):
    return (arr, width, index, first_row, block_row)


def _rowwise(name, fn, tiled, consts, out_tiled, out_acc, tm, rows=None, rider=None, layer=None, prev=None):
    tiled = [t if isinstance(t, tuple) else (t, t.shape[-1], 0, 0, None) for t in tiled]
    T = tiled[0][0].shape[-2] if rows is None else rows
    tm = min(tm, T)
    assert T % tm == 0 and all(t[3] % tm == 0 for t in tiled)
    n_t, n_c, n_o, n_a = len(tiled), len(consts), len(out_tiled), len(out_acc)

    n_ri = len(rider.ins) if rider else 0
    n_ro = len(rider.outs) if rider else 0
    n_steps = T // tm

    def body(*refs):
        n_in = n_t + n_c + n_ri + (n_o if layer is not None else 0)
        r_ins = refs[n_t + n_c:n_t + n_c + n_ri]
        o_refs = refs[n_in:n_in + n_o]
        a_refs = refs[n_in + n_o:n_in + n_o + n_a]
        r_outs = refs[n_in + n_o + n_a:n_in + n_o + n_a + n_ro]
        sems = refs[n_in + n_o + n_a + n_ro:]
        if rider:
            @pl.when(pl.program_id(0) == 0)
            def _():
                rider.start(r_ins, r_outs, *sems)

        ins = [r[...] for r in refs[:n_t + n_c]]
        outs = fn(*ins)
        for r, val in zip(o_refs, outs[:n_o]):
            r[...] = val.astype(r.dtype)
        if n_a:
            @pl.when(pl.program_id(0) == 0)
            def _():
                for r in a_refs:
                    r[...] = jnp.zeros_like(r)
            for r, val in zip(a_refs, outs[n_o:]):
                r[...] += val
        if rider:
            @pl.when(pl.program_id(0) == n_steps - 1)
            def _():
                rider.finish(r_ins, r_outs, *sems)

    def const_spec(a):
        nd = a.ndim
        return pl.BlockSpec(a.shape, lambda i: (0,) * nd, pipeline_mode=pl.Buffered(1))

    def tile_spec(arr, w, j, r0, block_row):
        row = block_row if block_row is not None else (lambda i: i + r0 // tm)
        if arr.ndim == 3:
            return pl.BlockSpec((None, tm, w), lambda i: (layer, row(i), j))
        return pl.BlockSpec((tm, w), lambda i: (row(i), j))

    in_specs = [tile_spec(*t) for t in tiled]
    in_specs += [const_spec(a) for a in consts]
    if layer is None:
        out_specs = [pl.BlockSpec((tm, w), lambda i: (i, 0)) for (w, _) in out_tiled]
        out_shape = [jax.ShapeDtypeStruct((T, w), dt) for (w, dt) in out_tiled]
    else:
        out_specs = [pl.BlockSpec((None, tm, w), lambda i: (layer, i, 0)) for (w, _) in out_tiled]
        out_shape = [jax.ShapeDtypeStruct((DEPTH, T, w), dt) for (w, dt) in out_tiled]
    out_specs += [pl.BlockSpec(s, lambda i: (0, 0)) for (s, _) in out_acc]
    out_shape += [jax.ShapeDtypeStruct(s, dt) for (s, dt) in out_acc]
    operands = [t[0] for t in tiled] + list(consts)
    scratch = []
    aliases = {}
    if rider:
        anyspec = pl.BlockSpec(memory_space=pl.ANY)
        in_specs += [anyspec] * n_ri
        out_specs += [anyspec] * n_ro
        out_shape += list(rider.outs)
        operands += list(rider.ins)
        scratch = rider.scratch()
    n_prev = 0
    if layer is not None:
        assert rider is None and not out_acc
        if prev is None:
            prev = [jnp.zeros(o.shape, o.dtype) for o in out_shape]
        n_prev = len(prev)
        aliases = {len(operands) + i: i for i in range(n_prev)}
        in_specs += [pl.BlockSpec(memory_space=pl.ANY)] * n_prev
        operands += list(prev)
    return pl.pallas_call(
        body, name=name, grid=(n_steps,), in_specs=in_specs, out_specs=out_specs, out_shape=out_shape, scratch_shapes=scratch,
        input_output_aliases=aliases,
        compiler_params=pltpu.CompilerParams(dimension_semantics=("arbitrary",), vmem_limit_bytes=VMEM_LIMIT),
    )(*operands)


def _colsum(x):
    return jnp.sum(x, axis=0, keepdims=True)


def _rms(x):
    r = lax.rsqrt(jnp.mean(x * x, axis=-1, keepdims=True) + EPS)
    return x * r, r


def _rms_bwd(dxn, xn, r):
    return r * (dxn - xn * jnp.mean(dxn * xn, axis=-1, keepdims=True))


def _normproj_fwd(name, x, sh, sc, g, w, tm, rider=None, w_t=False, out_dtype=F32):
    def fn(x, sh, sc, g, w):
        xn, _ = _rms(x)
        h = (xn * (g * (1.0 + sc)) + sh).astype(BF16)
        return lax.dot_general(h, w, (((1,), (1 if w_t else 0,)), ((), ())), preferred_element_type=F32), h

    return _rowwise(name, fn, [x], [sh, sc, g, w], [(w.shape[0 if w_t else 1], out_dtype), (D, BF16)], [], tm, rider=rider)


def _normproj_bwd(name, x, dpre, dres, sc, g, w, tm, rider=None, w_t=False):
    def fn(x, dpre, dres, sc, g, w):
        xn, r = _rms(x)
        dh = lax.dot_general(dpre, w, (((1,), (0 if w_t else 1,)), ((), ())), preferred_element_type=F32)
        da = _colsum(dh * xn)
        dx = _rms_bwd(dh * (g * (1.0 + sc)), xn, r) + dres
        return dx, _colsum(dh), da * g, da * (1.0 + sc)

    vec = ((1, D), F32)
    return _rowwise(name, fn, [x, dpre, dres], [sc, g, w], [(D, F32)], [vec, vec, vec], tm, rider=rider)


def _stage2(a_b, blk, cp, u1c, qp, kp, vp, bb, lg, lb, alog, dtb):
    y_a = a_b * cp
    u1 = u1c + bb
    mu = jnp.mean(u1, axis=-1, keepdims=True)
    uc = u1 - mu
    var = jnp.mean(uc * uc, axis=-1, keepdims=True)
    y_b = _silu(uc * lax.rsqrt(var + LN_EPS) * lg + lb)

    def l2(t):
        t = _silu(t)
        return t * lax.rsqrt(jnp.sum(t * t, axis=-1, keepdims=True) + EPS)

    q = [l2(t) for t in qp]
    k = [l2(t) for t in kp]
    v = _silu(vp)
    lane = _iota2(blk.shape, 1)
    gdec = -jnp.exp(alog) * _softplus(blk + dtb)
    gb = jnp.where(lane < HEADS, gdec, jnp.where(lane < 2 * HEADS, _sigmoid(blk), 0.0))
    return y_a, y_b, q, k, v, gb


def _heads_of(x, base=0):
    return [x[:, base + h * HD:base + (h + 1) * HD] for h in range(HEADS)]


def _stage2_fwd(proj, convout, bb, lg, lb, alog, dtb, tm):
    def fn(a_b, blk, co, bb, lg, lb, alog, dtb):
        y_a, y_b, q, k, v, gb = _stage2(a_b, blk, co[:, 0:256], co[:, 256:512], _heads_of(co, 512), _heads_of(co, 1024),
                                        co[:, 1536:2048], bb, lg, lb, alog, dtb)
        return jnp.concatenate([y_a, y_b], axis=1), jnp.concatenate(q + k + [v], axis=1), gb

    return _rowwise("stage2_fwd", fn, [_cols(proj, 256, 0), _cols(proj, 128, C_GB // 128), convout],
                    [bb, lg, lb, alog, dtb], [(512, BF16), (1536, F32), (128, F32)], [], tm)


def _stage2_bwd(proj, convout, dy_ab, dq, dk, dv, dgb, bb, lg, lb, alog, dtb, tm, rider=None):
    def fn(a_b, blk, co, dy_ab, dq, dk, dv, dgb, bb, lg, lb, alog, dtb):
        args = (a_b, blk, co[:, 0:256], co[:, 256:512], _heads_of(co, 512), _heads_of(co, 1024), co[:, 1536:2048],
                bb, lg, lb, alog, dtb)
        _, vjp = jax.vjp(_stage2, *args)
        ct = (dy_ab[:, 0:256], dy_ab[:, 256:512], _heads_of(dq), _heads_of(dk), dv, dgb)
        da_b, dblk, dcp, du1c, dqp, dkp, dvp, dbb, dlg, dlb, dalog, ddtb = vjp(ct)
        dco = jnp.concatenate([dcp, du1c] + dqp + dkp + [dvp], axis=1)
        return dco, da_b, dblk, dbb, dlg, dlb, dalog, ddtb

    v256, v128 = ((1, 256), F32), ((1, 128), F32)
    return _rowwise("stage2_bwd", fn,
                    [_cols(proj, 256, 0), _cols(proj, 128, C_GB // 128), convout, dy_ab, dq, dk, dv, dgb],
                    [bb, lg, lb, alog, dtb], [(N_CONVCOL, F32), (256, F32), (128, F32)],
                    [v256, v256, v256, v128, v128], tm, rider=rider)


def _stage3(o, z, dng):
    ys = []
    for oh, zh in zip(o, z):
        on = oh * lax.rsqrt(jnp.mean(oh * oh, axis=-1, keepdims=True) + EPS)
        ys.append(on * dng * _silu(zh))
    return ys


def _outproj_fwd(x, o, proj, y_ab, g1, dng, wout, tm, rider=None):
    def fn(x, o, z0, z1, z2, z3, y_ab, g1, dng, wout):
        y_c = _stage3(_heads_of(o), [z0, z1, z2, z3], dng)
        ycat = jnp.concatenate([y_ab] + [t.astype(BF16) for t in y_c], axis=1)
        mix = jnp.dot(ycat, wout, preferred_element_type=F32)
        return x + g1 * mix, mix, ycat

    return _rowwise("outproj_fwd", fn, [x, o] + _z_heads(proj) + [y_ab],
                    [g1, dng, wout], [(D, F32), (D, F32), (D, BF16)], [], tm, rider=rider)


def _z_heads(proj):
    return [_cols(proj, HD, C_Z // HD + h) for h in range(HEADS)]


def _outproj_bwd(dx1, mix, o, proj, g1, dng, wout, tm, rider=None):
    def fn(dx1, mix, o, z0, z1, z2, z3, g1, dng, wout):
        dmix = (dx1 * g1).astype(BF16)
        dycat = lax.dot_general(dmix, wout, (((1,), (1,)), ((), ())), preferred_element_type=F32)
        _, vjp = jax.vjp(_stage3, _heads_of(o), [z0, z1, z2, z3], dng)
        do, dz, ddng = vjp(_heads_of(dycat, 512))
        return (dmix, dycat[:, 0:512], jnp.concatenate(do, axis=1), jnp.concatenate(dz, axis=1),
                _colsum(dx1 * mix), ddng)

    return _rowwise("outproj_bwd", fn, [dx1, mix, o] + _z_heads(proj), [g1, dng, wout],
                    [(D, BF16), (512, F32), (512, F32), (512, F32)], [((1, D), F32), ((1, HD), F32)], tm, rider=rider)


_CONV_BLOCKS = ((0, 256, KA), (256, 512, KB), (512, 2048, KC))
_CONV_STRIP = 256


_CONV_ROWS = 32


def _conv_inputs(proj_ref, rows):
    a_c, a_v = proj_ref[rows, C_AC:C_AC + 256], proj_ref[rows, C_AV:C_AV + 256]
    b_a, b_g = proj_ref[rows, C_BA:C_BA + 256], proj_ref[rows, C_BG:C_BG + 256]
    return a_c, a_v, b_a, _sigmoid(b_g)


def _shifted_copies(ext, phases, tm):
    n = tm + HALO - 8
    for b in range(1, 8):
        phases[b - 1] = ext[pl.ds(b, n), 256:512]


def _rows_at(ext, phases, row, col, kw):
    a, b = divmod(row, 8)
    if kw == KB and b:
        return phases[b - 1, pl.ds(8 * a, _CONV_ROWS), :]
    return ext[pl.ds(row, _CONV_ROWS), col:col + _CONV_STRIP]


def _conv_fwd(proj, wa, wb, wc, tm):
    T = proj.shape[0]
    tm = min(tm, T)

    def body(proj_ref, wa_ref, wb_ref, wc_ref, out_ref, ext, phases):
        @pl.when(pl.program_id(0) == 0)
        def _():
            ext[0:HALO, :] = jnp.zeros((HALO, N_CONVCOL), F32)

        a_c, a_v, b_a, sg = _conv_inputs(proj_ref, slice(None))
        ext[HALO:HALO + tm, 0:256] = a_c * a_v
        ext[HALO:HALO + tm, 256:512] = b_a * sg
        ext[HALO:HALO + tm, 512:2048] = proj_ref[:, C_Q:C_Q + 1536]
        _shifted_copies(ext, phases, tm)
        for r0 in range(0, tm, _CONV_ROWS):
            for (c0, c1, kw), w_ref in zip(_CONV_BLOCKS, (wa_ref, wb_ref, wc_ref)):
                for s0 in range(c0, c1, _CONV_STRIP):
                    acc = jnp.zeros((_CONV_ROWS, _CONV_STRIP), F32)
                    for k in range(kw):
                        acc += (w_ref[k:k + 1, s0 - c0:s0 - c0 + _CONV_STRIP]
                                * _rows_at(ext, phases, r0 + HALO - (kw - 1) + k, s0, kw))
                    out_ref[r0:r0 + _CONV_ROWS, s0:s0 + _CONV_STRIP] = acc
        ext[0:HALO, :] = ext[tm:tm + HALO, :]

    full = lambda a: pl.BlockSpec(a.shape, lambda i: (0, 0))
    return pl.pallas_call(
        body, name="conv_fwd", grid=(T // tm,),
        in_specs=[pl.BlockSpec((tm, IN_PAD), lambda i: (i, 0)), full(wa), full(wb), full(wc)],
        out_specs=pl.BlockSpec((tm, N_CONVCOL), lambda i: (i, 0)),
        out_shape=jax.ShapeDtypeStruct((T, N_CONVCOL), F32),
        scratch_shapes=[pltpu.VMEM((HALO + tm, N_CONVCOL), F32), pltpu.VMEM((7, tm + HALO - 8, 256), F32)],
        compiler_params=pltpu.CompilerParams(dimension_semantics=("arbitrary",), vmem_limit_bytes=VMEM_LIMIT),
    )(proj, wa, wb, wc)


def _conv_bwd(proj, dco, da_b, dz, dblk, wa, wb, wc, tm, rider=None):
    T = proj.shape[0]
    tm = min(tm, T)
    n = T // tm

    def body(proj_ref, dco_ref, dab_ref, dz_ref, dblk_ref, wa_ref, wb_ref, wc_ref,
             dproj_ref, dwa_ref, dwb_ref, dwc_ref, ext, acc_a, acc_b, acc_c, phases):
        @pl.when(pl.program_id(0) == 0)
        def _():
            ext[tm:tm + HALO, :] = jnp.zeros((HALO, N_CONVCOL), F32)
            acc_a[...] = jnp.zeros_like(acc_a)
            acc_b[...] = jnp.zeros_like(acc_b)
            acc_c[...] = jnp.zeros_like(acc_c)

        ext[0:tm, :] = dco_ref[...]
        _shifted_copies(ext, phases, tm)

        def taps(w_ref, acc_ref, kw, c0, wc0, xin, r0):
            dx = jnp.zeros((_CONV_ROWS, _CONV_STRIP), F32)
            for k in range(kw):
                sh = _rows_at(ext, phases, r0 + kw - 1 - k, c0, kw)
                dx += w_ref[k:k + 1, wc0:wc0 + _CONV_STRIP] * sh
                pr = sh * xin
                part = pr[0:8]
                for g in range(8, _CONV_ROWS, 8):
                    part += pr[g:g + 8]
                acc_ref[8 * k:8 * k + 8, wc0:wc0 + _CONV_STRIP] += part
            return dx

        for r0 in range(0, tm, _CONV_ROWS):
            rows = slice(r0, r0 + _CONV_ROWS)
            a_c, a_v, b_a, sg = _conv_inputs(proj_ref, rows)
            dp = taps(wa_ref, acc_a, KA, 0, 0, a_c * a_v, r0)
            dproj_ref[rows, C_AC:C_AC + 256] = (dp * a_v).astype(BF16)
            dproj_ref[rows, C_AV:C_AV + 256] = (dp * a_c).astype(BF16)
            du0 = taps(wb_ref, acc_b, KB, 256, 0, b_a * sg, r0)
            dproj_ref[rows, C_BA:C_BA + 256] = (du0 * sg).astype(BF16)
            dproj_ref[rows, C_BG:C_BG + 256] = (du0 * b_a * sg * (1.0 - sg)).astype(BF16)
            for s0 in range(0, 1536, _CONV_STRIP):
                dq = taps(wc_ref, acc_c, KC, 512 + s0, s0, proj_ref[rows, C_Q + s0:C_Q + s0 + _CONV_STRIP], r0)
                dproj_ref[rows, C_Q + s0:C_Q + s0 + _CONV_STRIP] = dq.astype(BF16)
        ext[tm:tm + HALO, :] = ext[0:HALO, :]

        dproj_ref[:, C_AB:C_AB + 256] = dab_ref[...].astype(BF16)
        dproj_ref[:, C_Z:C_Z + 512] = dz_ref[...].astype(BF16)
        dproj_ref[:, C_GB:C_GB + 128] = dblk_ref[...].astype(BF16)

        @pl.when(pl.program_id(0) == n - 1)
        def _():
            for acc_ref, dw_ref, kw in ((acc_a, dwa_ref, KA), (acc_b, dwb_ref, KB), (acc_c, dwc_ref, KC)):
                dw_ref[...] = jnp.zeros_like(dw_ref)
                for k in range(kw):
                    dw_ref[k:k + 1, :] = _colsum(acc_ref[8 * k:8 * k + 8, :])

    rev = lambda w: pl.BlockSpec((tm, w), lambda i: (n - 1 - i, 0))
    full = lambda a: pl.BlockSpec(a.shape, lambda i: (0, 0))
    return _host_call(
        body, "conv_bwd", (n,),
        [rev(IN_PAD), rev(N_CONVCOL), rev(256), rev(512), rev(128), full(wa), full(wb), full(wc)],
        [rev(IN_PAD), full(wa), full(wb), full(wc)],
        [jax.ShapeDtypeStruct((T, IN_PAD), BF16), jax.ShapeDtypeStruct(wa.shape, F32),
         jax.ShapeDtypeStruct(wb.shape, F32), jax.ShapeDtypeStruct(wc.shape, F32)],
        [pltpu.VMEM((tm + HALO, N_CONVCOL), F32), pltpu.VMEM((8 * KA, 256), F32),
         pltpu.VMEM((8 * KB, 256), F32), pltpu.VMEM((8 * KC, 1536), F32), pltpu.VMEM((7, tm + HALO - 8, 256), F32)],
        [proj, dco, da_b, dz, dblk, wa, wb, wc], rider)


def _ffn_out_fwd(x1, gu, g2, wfo, tm, rider=None):
    def fn(x1, gu, g2, wfo):
        s = (_silu(gu[:, :D_FF].astype(F32)) * gu[:, D_FF:].astype(F32)).astype(BF16)
        f = jnp.dot(s, wfo, preferred_element_type=F32)
        return x1 + g2 * f, f

    return _rowwise("ffnout_fwd", fn, [x1, gu], [g2, wfo], [(D, F32), (D, F32)], [], tm, rider=rider)


def _ffn_out_bwd(dx2, gu, f, g2, wfo, tm, rider=None):
    def fn(dx2, gu, f, g2, wfo):
        gate, up = gu[:, :D_FF].astype(F32), gu[:, D_FF:].astype(F32)
        sg = _sigmoid(gate)
        sl = gate * sg
        df = (dx2 * g2).astype(BF16)
        ds = lax.dot_general(df, wfo, (((1,), (1,)), ((), ())), preferred_element_type=F32)
        dgate = ds * up * (sg * (1.0 + gate * (1.0 - sg)))
        dgu = jnp.concatenate([dgate.astype(BF16), (ds * sl).astype(BF16)], axis=1)
        return dgu, sl * up, df, _colsum(dx2 * f)

    return _rowwise("ffnout_bwd", fn, [dx2, gu, f], [g2, wfo], [(2 * D_FF, BF16), (D_FF, BF16), (D, BF16)],
                    [((1, D), F32)], tm, rider=rider)


def _loss_bwd(x, tgt, gfin, tm):
    def fn(x, tgt, gfin):
        xn, r = _rms(x)
        e = xn * gfin - tgt
        loss = 0.5 * jnp.sum(jnp.mean(e * e, axis=-1, keepdims=True), axis=0, keepdims=True)
        dy = e * (1.0 / D)
        return _rms_bwd(dy * gfin, xn, r), jnp.broadcast_to(loss, (1, 128)), _colsum(dy * xn)

    return _rowwise("loss_bwd", fn, [x, tgt], [gfin], [(D, F32)], [((1, 128), F32), ((1, D), F32)], tm)


def _wgrad(name, a, b, bm, bk):
    T, M = a.shape
    N = b.shape[1]
    bk = min(bk, T)

    def body(a_ref, b_ref, o_ref):
        @pl.when(pl.program_id(1) == 0)
        def _():
            o_ref[...] = jnp.zeros_like(o_ref)

        o_ref[...] += lax.dot_general(a_ref[...], b_ref[...], (((0,), (0,)), ((), ())), preferred_element_type=F32)

    return pl.pallas_call(
        body, name=name, grid=(M // bm, T // bk),
        in_specs=[pl.BlockSpec((bk, bm), lambda i, k: (k, i)), pl.BlockSpec((bk, N), lambda i, k: (k, 0))],
        out_specs=pl.BlockSpec((bm, N), lambda i, k: (i, 0)),
        out_shape=jax.ShapeDtypeStruct((M, N), F32),
        compiler_params=pltpu.CompilerParams(dimension_semantics=("arbitrary", "arbitrary"), vmem_limit_bytes=VMEM_LIMIT),
    )(a, b)


IN_BLOCK = 512


def _wgrad_in_blocks(name, a, b, bm, bk):
    T, M = a.shape
    N = b.shape[1]
    bk = min(bk, T)
    nk = T // bk
    per = IN_COLS // N_DEV
    win = IN_BLOCK + 128

    def body(a_ref, b_ref, o_ref, acc):
        @pl.when(pl.program_id(1) == 0)
        def _():
            acc[...] = jnp.zeros_like(acc)

        acc[...] += lax.dot_general(a_ref[...], b_ref[...], (((0,), (0,)), ((), ())), preferred_element_type=F32)

        @pl.when(pl.program_id(1) == nk - 1)
        def _():
            for j in range(N_DEV):
                q, r = divmod(per * j, 128)
                w = acc[:, 128 * q:128 * q + win]
                if r:
                    w = pltpu.roll(w, win - r, axis=1)
                o_ref[j] = w[:, :IN_BLOCK]

    assert 128 * ((per * (N_DEV - 1)) // 128) + win <= N
    return pl.pallas_call(
        body, name=name, grid=(M // bm, nk),
        in_specs=[pl.BlockSpec((bk, bm), lambda i, k: (k, i)), pl.BlockSpec((bk, N), lambda i, k: (k, 0))],
        out_specs=pl.BlockSpec((N_DEV, bm, IN_BLOCK), lambda i, k: (0, i, 0)),
        out_shape=jax.ShapeDtypeStruct((N_DEV, M, IN_BLOCK), F32),
        scratch_shapes=[pltpu.VMEM((bm, N), F32)],
        compiler_params=pltpu.CompilerParams(dimension_semantics=("arbitrary", "arbitrary"), vmem_limit_bytes=VMEM_LIMIT),
    )(a, b)


def _adamw(w, g, m, v):
    m = B1 * m + (1.0 - B1) * g
    v = B2 * v + (1.0 - B2) * (g * g)
    m_hat = m / (1.0 - B1 ** STEP)
    v_hat = v / (1.0 - B2 ** STEP)
    return -LR * (m_hat / (jnp.sqrt(v_hat) + AEPS) + WD * w), m, v


def _adam_call(name, w, g, m, v, tm):
    C = w.shape[1]
    return _rowwise(name, _adamw, [w, g, m, v], [], [(C, F32)] * 3, [], tm)


def _adam_layer(name, g, w, m, v, tm, layer, prev):
    C = g.shape[1]
    return _rowwise(name, lambda g, w, m, v: (g,) + _adamw(w, g, m, v), [g, w, m, v], [], [(C, F32)] * 4, [], tm,
                    layer=layer, prev=prev)


def _reduce_sum(name, own, recv, tm):
    n, R, C = recv.shape
    flat = recv.reshape(n * R, C)
    fn = lambda a, b, r0, r1, r2: ((((a + b) + r0.astype(F32)) + r1.astype(F32)) + r2.astype(F32),)
    return _rowwise(name, fn, _own_blocks(own, R, tm) + [_cols(flat, C, 0, j * R) for j in range(n)], [], [(C, F32)], [],
                    tm, rows=R)[0]


def _pair_add(name, g, from_sib, tm):
    n, R, C = from_sib.shape
    n_r = R // tm
    mine = _cols(g.reshape(N_DEV * R, C), C, 0,
                 block_row=lambda i: (2 * (i // n_r) + lax.axis_index("c")) * n_r + i % n_r)

    return _rowwise(name, lambda a, b: (a + b,), [mine, from_sib.reshape(n * R, C)], [], [(C, BF16)], [], tm, rows=n * R)[0]


def _own_blocks(own, R, tm):
    g, from_sib = own
    C = g.shape[-1]
    n_r = R // tm
    chip = lambda: 2 * lax.axis_index("x") + lax.axis_index("y")
    return [_cols(g.reshape(N_DEV * R, C), C, 0, block_row=lambda i: (2 * chip() + lax.axis_index("c")) * n_r + i),
            _cols(from_sib.reshape(4 * R, C), C, 0, block_row=lambda i: chip() * n_r + i)]


def _reduce_adam(name, own, recv, w, m, v, tm, layer, prev):
    n, R, C = recv.shape
    flat = recv.reshape(n * R, C)

    def fn(a, b, r0, r1, r2, w, m, v):
        g = (((a + b) + r0.astype(F32)) + r1.astype(F32)) + r2.astype(F32)
        return (g,) + _adamw(w, g, m, v)

    return _rowwise(name, fn, _own_blocks(own, R, tm) + [_cols(flat, C, 0, j * R) for j in range(n)] + [w, m, v], [],
                    [(C, F32)] * 4, [], tm, rows=R, layer=layer, prev=prev)


_OFFSETS = [(dx, dy, dc) for dx in (0, 1) for dy in (0, 1) for dc in (0, 1)][1:]
_MESH = pl.DeviceIdType.MESH


def _coords():
    return lax.axis_index("x"), lax.axis_index("y"), lax.axis_index("c")


def _flip(me, off):
    return tuple((1 - m) if d else m for m, d in zip(me, off))


def _linear(p):
    return 4 * p[0] + 2 * p[1] + p[2]


_CHIP_FLIPS = ((1, 0), (0, 1), (1, 1))


class _Rider:
    def __init__(self, ins, outs, n_remote, n_local, start, finish):
        self.ins, self.outs, self.n_remote, self.n_local, self.start, self.finish = ins, outs, n_remote, n_local, start, finish

    def scratch(self):
        return [pltpu.SemaphoreType.DMA((self.n_remote,)), pltpu.SemaphoreType.DMA((self.n_remote,)),
                pltpu.SemaphoreType.DMA((max(self.n_local, 1),))]


def _run_rider(name, rider):
    def body(*refs):
        n_i, n_o = len(rider.ins), len(rider.outs)
        rider.start(refs[:n_i], refs[n_i:n_i + n_o], *refs[n_i + n_o:])
        rider.finish(refs[:n_i], refs[n_i:n_i + n_o], *refs[n_i + n_o:])

    anyspec = pl.BlockSpec(memory_space=pl.ANY)
    return pl.pallas_call(
        body, name=name, in_specs=[anyspec] * len(rider.ins), out_specs=[anyspec] * len(rider.outs),
        out_shape=list(rider.outs), scratch_shapes=rider.scratch(),
    )(*rider.ins)


def _gather_rider(arrs, layer):
    n = len(arrs)

    def parts(ins, outs, send, recv, loc):
        x, y, c = _coords()
        me, sib = (x, y, c), (x, y, 1 - c)
        chips = [((1 - x) if dx else x, (1 - y) if dy else y) for dx, dy in _CHIP_FLIPS]

        def copy(a, k, block, to, own=False):
            slot = outs[a].at[_linear(block)]
            return pltpu.make_async_remote_copy(src_ref=ins[a].at[layer] if own else slot, dst_ref=slot,
                                                send_sem=send.at[a * 7 + k], recv_sem=recv.at[a * 7 + k],
                                                device_id=to, device_id_type=_MESH)

        local = [pltpu.make_async_copy(ins[a].at[layer], outs[a].at[_linear(me)], loc.at[a]) for a in range(n)]
        first = []
        for a in range(n):
            first.append(copy(a, 0, me, sib, own=True))
            first += [copy(a, 1 + j, me, (*chip, c), own=True) for j, chip in enumerate(chips)]
        return copy, local, first, me, sib, chips, c

    def start(*refs):
        _, local, first, *_ = parts(*refs)
        for cp in local + first:
            cp.start()

    def finish(*refs):
        copy, local, first, me, sib, chips, c = parts(*refs)
        passed = []
        for j, chip in enumerate(chips):
            for a in range(n):
                copy(a, 1 + j, (*chip, c), me).wait_recv()
                cp = copy(a, 4 + j, (*chip, c), sib)
                cp.start()
                passed.append(cp)
        for a in range(n):
            copy(a, 0, sib, me).wait_recv()
            for j, chip in enumerate(chips):
                copy(a, 4 + j, (*chip, 1 - c), me).wait_recv()
        for cp in first + passed:
            cp.wait_send()
        for cp in local:
            cp.wait()

    outs = [jax.ShapeDtypeStruct((N_DEV,) + a.shape[1:], a.dtype) for a in arrs]
    return _Rider(list(arrs), outs, 7 * n, n, start, finish)


def _simple_rider(ins, outs, n_remote, make):
    def start(*refs):
        for cp in make(*refs):
            cp.start()

    def finish(*refs):
        for cp in make(*refs):
            cp.wait()

    return _Rider(ins, outs, n_remote, 0, start, finish)


def _pair_rider(arrs):
    def make(ins, outs, send, recv, loc):
        x, y, c = _coords()
        return [pltpu.make_async_remote_copy(src_ref=ins[a].at[2 * s_ + 1 - c], dst_ref=outs[a].at[s_], send_sem=send.at[4 * a + s_],
                                             recv_sem=recv.at[4 * a + s_], device_id=(x, y, 1 - c), device_id_type=_MESH)
                for a in range(len(arrs)) for s_ in range(4)]

    return _simple_rider(list(arrs), [jax.ShapeDtypeStruct((4,) + a.shape[1:], a.dtype) for a in arrs], 4 * len(arrs), make)


def _chip_rider(arrs):
    nf = len(_CHIP_FLIPS)

    def make(ins, outs, send, recv, loc):
        x, y, c = _coords()
        copies = []
        for a in range(len(arrs)):
            for k, (dx, dy) in enumerate(_CHIP_FLIPS):
                px, py = (1 - x) if dx else x, (1 - y) if dy else y
                copies.append(pltpu.make_async_remote_copy(
                    src_ref=ins[a].at[2 * px + py], dst_ref=outs[a].at[k], send_sem=send.at[a * nf + k],
                    recv_sem=recv.at[a * nf + k], device_id=(px, py, c), device_id_type=_MESH))
        return copies

    return _simple_rider(list(arrs), [jax.ShapeDtypeStruct((nf,) + a.shape[1:], a.dtype) for a in arrs], nf * len(arrs), make)


def _small_allgather(name, packed):
    R = packed.shape[0]

    def body(in_ref, all_ref, sum_ref, send, recv):
        me = _coords()
        my = _linear(me)
        all_ref[my] = in_ref[...]
        copies = []
        for k, off in enumerate(_OFFSETS):
            cp = pltpu.make_async_remote_copy(src_ref=in_ref, dst_ref=all_ref.at[my], send_sem=send.at[k], recv_sem=recv.at[k],
                                              device_id=_flip(me, off), device_id_type=_MESH)
            cp.start()
            copies.append(cp)
        for cp in copies:
            cp.wait()
        acc = all_ref[0]
        for j in range(1, N_DEV):
            acc = acc + all_ref[j]
        sum_ref[...] = acc

    vm = pl.BlockSpec(memory_space=pltpu.VMEM)
    return pl.pallas_call(
        body, name=name, in_specs=[vm], out_specs=[vm, vm],
        out_shape=[jax.ShapeDtypeStruct((N_DEV, R, 128), F32), jax.ShapeDtypeStruct((R, 128), F32)],
        scratch_shapes=[pltpu.SemaphoreType.DMA((len(_OFFSETS),)), pltpu.SemaphoreType.DMA((len(_OFFSETS),))],
        compiler_params=pltpu.CompilerParams(vmem_limit_bytes=VMEM_LIMIT),
    )(packed)


def _ada_mod(c, w16, bias):
    nc = w16.shape[2]
    kp = len(_OFFSETS)

    def body(c_ref, w_ref, b_ref, rows_ref, act_ref, cbuf, sbuf, send, recv):
        me = _coords()
        my = _linear(me)
        cbuf[my] = c_ref[...]
        copies = []
        for k, off in enumerate(_OFFSETS):
            cp = pltpu.make_async_remote_copy(src_ref=c_ref, dst_ref=cbuf.at[my], send_sem=send.at[k], recv_sem=recv.at[k],
                                              device_id=_flip(me, off), device_id_type=_MESH)
            cp.start()
            copies.append(cp)
        for cp in copies:
            cp.wait()
        act = _silu(jnp.concatenate([cbuf[j] for j in range(N_DEV)], axis=0))
        act_ref[...] = act
        act16 = act.astype(BF16)
        for l in range(DEPTH):
            ml = jnp.dot(act16, w_ref[l], preferred_element_type=F32) + b_ref[l:l + 1, :]
            for j in range(N_DEV):
                sbuf[j, l:l + 1, :] = ml[j:j + 1, :]
        rows_ref[my] = sbuf[my]
        copies = []
        for k, off in enumerate(_OFFSETS):
            peer = _flip(me, off)
            cp = pltpu.make_async_remote_copy(src_ref=sbuf.at[_linear(peer)], dst_ref=rows_ref.at[my], send_sem=send.at[kp + k],
                                              recv_sem=recv.at[kp + k], device_id=peer, device_id_type=_MESH)
            cp.start()
            copies.append(cp)
        for cp in copies:
            cp.wait()

    vm = pl.BlockSpec(memory_space=pltpu.VMEM)
    return pl.pallas_call(
        body, name="ada_mod", in_specs=[vm, vm, vm], out_specs=[vm, vm],
        out_shape=[jax.ShapeDtypeStruct((N_DEV, DEPTH, nc), F32), jax.ShapeDtypeStruct((N_DEV, D), F32)],
        scratch_shapes=[pltpu.VMEM((N_DEV, 1, D), F32), pltpu.VMEM((N_DEV, DEPTH, nc), F32),
                        pltpu.SemaphoreType.DMA((2 * kp,)), pltpu.SemaphoreType.DMA((2 * kp,))],
        compiler_params=pltpu.CompilerParams(vmem_limit_bytes=VMEM_LIMIT),
    )(c, w16, bias)


def _pack(arrs):
    parts = []
    for a in arrs:
        f = a.reshape(-1).astype(F32)
        parts.append(jnp.pad(f, (0, (-f.shape[0]) % 128)))
    flat = jnp.concatenate(parts)
    flat = jnp.pad(flat, (0, (-flat.shape[0]) % 1024))
    return flat.reshape(-1, 128)


def _unpack(packed, shapes):
    flat = packed.reshape(packed.shape[:-2] + (-1,))
    out, r = [], 0
    for s in shapes:
        n = int(np.prod(s))
        out.append(flat[..., r:r + n].reshape(packed.shape[:-2] + tuple(s)))
        r += -(-n // 128) * 128
    return out


def _pad_rows(w, rows):
    return jnp.pad(w, ((0, 0), (0, rows - w.shape[1]), (0, 0)))


_SMALL = ("b_ada", "norm_mix_g", "norm_ffn_g", "conv_a_w", "conf_dw_w", "conf_dw_b", "conf_ln_g", "conf_ln_b",
          "dn_conv_w", "dn_a_log", "dn_dt_bias", "dn_norm_g", "final_norm_g")
_BIG = ("w_in", "w_out", "w_ffn_in", "w_ffn_out")
_WEIGHTS = ("w_ada", "b_ada", "norm_mix_g", "norm_ffn_g", "w_in", "conv_a_w", "conf_dw_w", "conf_dw_b", "conf_ln_g",
            "conf_ln_b", "dn_conv_w", "dn_a_log", "dn_dt_bias", "dn_norm_g", "w_out", "w_ffn_in", "w_ffn_out",
            "final_norm_g")


def _step(x, c, loss_target, W, M, V):
    T = x.shape[1]
    me = _linear(_coords())
    xs, tgt = x[0], loss_target[0]
    vec = lambda a: a.reshape(1, -1)

    nada = W["w_ada"].shape[2]
    rows, act_all = _ada_mod(c, W["w_ada"].astype(BF16), lax.dynamic_slice(W["b_ada"], (0, me * nada), (DEPTH, nada)))
    mod = rows.transpose(1, 0, 2).reshape(DEPTH, 6, 1, D)

    w16 = {k: W[k].astype(BF16) for k in _BIG}
    w16["w_ffn_in"] = W["w_ffn_in"].transpose(0, 2, 1).astype(BF16)

    def whole(g_in=None, g_out=None, g_fi=None, g_fo=None):
        out = {}
        if g_in is not None:
            out["w_in"] = jnp.pad(g_in.transpose(1, 0, 2).reshape(D, IN_COLS), ((0, 0), (0, IN_PAD - IN_COLS)))
        if g_out is not None:
            out["w_out"] = g_out.reshape(D, D)
        if g_fi is not None:
            out["w_ffn_in"] = g_fi.reshape(2 * D_FF, D)
        if g_fo is not None:
            out["w_ffn_out"] = g_fo.reshape(D_FF, D)
        return out

    wts = [dict() for _ in range(DEPTH)]
    gather = lambda names, layer: _gather_rider([w16[k] for k in names], layer)
    wts[0].update(whole(g_in=_run_rider("gather_weights", gather(["w_in"], 0))[0]))
    conv_names = ("conv_a_w", "conf_dw_w", "dn_conv_w")
    conv_all, _ = _small_allgather("gather_conv_w", _pack([W[k] for k in conv_names]))
    conv_full = [t.transpose(1, 2, 0, 3).reshape(t.shape[1], t.shape[2], -1)
                 for t in _unpack(conv_all, [W[k].shape for k in conv_names])]
    wa, wb, wc = _pad_rows(conv_full[0], 8), _pad_rows(conv_full[1], 32), _pad_rows(conv_full[2], 8)
    lane_pad = lambda a: jnp.pad(a, ((0, 0), (0, 128 - a.shape[1])))
    alog, dtb = lane_pad(W["dn_a_log"]), lane_pad(W["dn_dt_bias"])

    saved = []
    xc = xs
    for l in range(DEPTH):
        more = l + 1 < DEPTH
        sh1, sc1, g1, sh2, sc2, g2 = [mod[l, i] for i in range(6)]
        proj, h, *got = _normproj_fwd("inproj_fwd", xc, sh1, sc1, vec(W["norm_mix_g"][l]), wts[l]["w_in"], 512,
                                      rider=gather(["w_out", "w_ffn_out"], 0) if l == 0 else None)
        if l == 0:
            wts[0].update(whole(g_out=got[0], g_fo=got[1]))
        convout = _conv_fwd(proj, wa[l], wb[l], wc[l], 256)
        y_ab, qkv, gb = _stage2_fwd(proj, convout, vec(W["conf_dw_b"][l]), vec(W["conf_ln_g"][l]), vec(W["conf_ln_b"][l]),
                                    alog[l:l + 1], dtb[l:l + 1], 256)
        o, ss, inv, *got = _delta_fwd(qkv, gb, DELTA_NB,
                                 rider=gather(["w_ffn_in"], 0) if l == 0 else gather(["w_in"], l + 1) if more else None)
        if l == 0:
            wts[0].update(whole(g_fi=got[0]))
        elif more:
            wts[l + 1].update(whole(g_in=got[0]))
        x1, mix, ycat, *got = _outproj_fwd(xc, o, proj, y_ab, g1, vec(W["dn_norm_g"][l]), wts[l]["w_out"], 512,
                                           rider=gather(["w_in"], 1) if l == 0 else None)
        if l == 0:
            wts[1].update(whole(g_in=got[0]))
        gu, h2, *got = _normproj_fwd("ffnin_fwd", x1, sh2, sc2, vec(W["norm_ffn_g"][l]), wts[l]["w_ffn_in"], 256,
                                     rider=gather(["w_ffn_in"], l + 1) if more else None, w_t=True, out_dtype=BF16)
        if more:
            wts[l + 1].update(whole(g_fi=got[0]))
        x2, f, *got = _ffn_out_fwd(x1, gu, g2, wts[l]["w_ffn_out"], 256,
                                   rider=gather(["w_ffn_out", "w_out"], l + 1) if more else None)
        if more:
            wts[l + 1].update(whole(g_fo=got[0], g_out=got[1]))
        saved.append((xc, proj, h, convout, qkv, gb, o, ss, inv, mix, ycat, x1, gu, h2, f))
        xc = x2

    dx, loss_row, d_gfin = _loss_bwd(xc, tgt, vec(W["final_norm_g"]), 512)
    loss = lax.psum(loss_row[0, 0], ("x", "y", "c"))

    big_out = {k: None for k in _BIG}
    dmod, small = [None] * DEPTH, [None] * DEPTH
    blocks = lambda g: g.reshape(N_DEV, -1, g.shape[-1])
    pair_tm = {"w_in": 512, "w_out": 128, "w_ffn_in": 704, "w_ffn_out": 352}
    sum_tm = {"w_in": 256, "w_out": 128, "w_ffn_in": 176, "w_ffn_out": 176}
    turned = lambda a: a.transpose(0, 2, 1)
    wmv = {k: (W[k], M[k], V[k]) for k in _BIG}
    wmv["w_ffn_in"] = tuple(turned(a) for a in wmv["w_ffn_in"])

    def pair_sum(k, g, from_sib):
        return _pair_add("pair_add_" + k, g, from_sib, pair_tm[k]).reshape(from_sib.shape), (g, from_sib)

    def finish(k, layer, own, r):
        if k == "w_in":
            g = _reduce_sum("reduce_" + k, own, r, sum_tm[k])[:, :IN_COLS // N_DEV]
            big_out[k] = _adam_layer("adam_" + k, g, *wmv[k], 256, layer, big_out[k])
        else:
            big_out[k] = _reduce_adam("reduce_adam_" + k, own, r, *wmv[k], sum_tm[k], layer, big_out[k])

    above = None
    for l in reversed(range(DEPTH)):
        xc, proj, h, convout, qkv, gb, o, ss, inv, mix, ycat, x1, gu, h2, f = saved[l]
        sh1, sc1, g1, sh2, sc2, g2 = [mod[l, i] for i in range(6)]
        gm, gf = vec(W["norm_mix_g"][l]), vec(W["norm_ffn_g"][l])
        bb, lg, lb = vec(W["conf_dw_b"][l]), vec(W["conf_ln_g"][l]), vec(W["conf_ln_b"][l])
        dng = vec(W["dn_norm_g"][l])
        wl = wts[l]

        dgu, s, df, d_g2, *got = _ffn_out_bwd(dx, gu, f, g2, wl["w_ffn_out"], 256,
                                              rider=_pair_rider([above[1]]) if above else None)
        if above:
            in16, in_own = pair_sum("w_in", above[1], got[0])
        gw_fo = _wgrad("wgrad_ffn_out", s, df, 1408, 1024)
        dx1, d_sh2, d_sc2, d_gf, *got = _normproj_bwd("ffnin_bwd", x1, dgu, dx, sc2, gf, wl["w_ffn_in"], 256,
                                                      rider=_chip_rider([in16]) if above else None, w_t=True)
        if above:
            finish("w_in", above[0], in_own, got[0])
        gw_fi = _wgrad("wgrad_ffn_in", dgu, h2, 1408, 1024)
        gw_fo, gw_fi = blocks(gw_fo), blocks(gw_fi)
        dmix, dy_ab, do, dz, d_g1, d_dng, *got = _outproj_bwd(dx1, mix, o, proj, g1, dng, wl["w_out"], 512,
                                                               rider=_pair_rider([gw_fo, gw_fi]))
        fo16, fo_own = pair_sum("w_ffn_out", gw_fo, got[0])
        fi16, fi_own = pair_sum("w_ffn_in", gw_fi, got[1])
        gw_out = blocks(_wgrad("wgrad_out", ycat, dmix, 512, 2048))
        dqkv, *got = _delta_bwd(qkv, gb, ss, inv, do, DELTA_NB, rider=_chip_rider([fi16]))
        finish("w_ffn_in", l, fi_own, got[0])
        dco, da_b, dblk, d_bb, d_lg, d_lb, d_alog, d_dtb, *got = _stage2_bwd(
            proj, convout, dy_ab, _cols(dqkv, D_DN, 0), _cols(dqkv, D_DN, 1), _cols(dqkv, D_DN, 2),
            _cols(dqkv, 128, 3 * D_DN // 128), bb, lg, lb, alog[l:l + 1], dtb[l:l + 1], 256,
            rider=_pair_rider([gw_out]))
        out16, out_own = pair_sum("w_out", gw_out, got[0])
        dproj, d_wa, d_wb, d_wc, *got = _conv_bwd(proj, dco, da_b, dz, dblk, wa[l], wb[l], wc[l], 256,
                                                  rider=_chip_rider([out16, fo16]))
        finish("w_out", l, out_own, got[0])
        finish("w_ffn_out", l, fo_own, got[1])
        dx, d_sh1, d_sc1, d_gm = _normproj_bwd("inproj_bwd", xc, dproj, dx1, sc1, gm, wl["w_in"], 512)
        above = (l, _wgrad_in_blocks("wgrad_in", h, dproj, 512, 1024))

        dmod[l] = jnp.concatenate([d_sh1, d_sc1, d_g1, d_sh2, d_sc2, d_g2], axis=1)
        small[l] = dict(norm_mix_g=d_gm, norm_ffn_g=d_gf, conv_a_w=d_wa[:KA], conf_dw_w=d_wb[:KB], conf_dw_b=d_bb,
                        conf_ln_g=d_lg, conf_ln_b=d_lb, dn_conv_w=d_wc[:KC], dn_a_log=d_alog, dn_dt_bias=d_dtb,
                        dn_norm_g=d_dng)

    in16, in_own = pair_sum("w_in", above[1], _run_rider("pair_exchange", _pair_rider([above[1]]))[0])
    finish("w_in", above[0], in_own, _run_rider("chip_exchange", _chip_rider([in16]))[0])

    names = ("norm_mix_g", "norm_ffn_g", "conv_a_w", "conf_dw_w", "conf_dw_b", "conf_ln_g", "conf_ln_b", "dn_conv_w",
             "dn_a_log", "dn_dt_bias", "dn_norm_g")
    pieces = [jnp.stack(dmod)] + [jnp.stack([small[l][k] for l in range(DEPTH)]) for k in names] + [d_gfin]
    shapes = [p.shape for p in pieces]
    every, total = _small_allgather("gather_small_grads", _pack(pieces))
    tot = dict(zip(("dmod",) + names + ("final_norm_g",), _unpack(total, shapes)))
    dmod_all = _unpack(every, shapes[:1])[0]

    grads = {}
    grads["b_ada"] = tot["dmod"].reshape(DEPTH, 6 * D)
    for k in ("norm_mix_g", "norm_ffn_g", "conf_dw_b", "conf_ln_g", "conf_ln_b", "dn_norm_g"):
        grads[k] = tot[k].reshape(W[k].shape)
    grads["dn_a_log"] = tot["dn_a_log"].reshape(DEPTH, 128)[:, :HEADS]
    grads["dn_dt_bias"] = tot["dn_dt_bias"].reshape(DEPTH, 128)[:, :HEADS]
    grads["final_norm_g"] = tot["final_norm_g"].reshape(D)
    for k in conv_names:
        nloc = W[k].shape[2]
        grads[k] = lax.dynamic_slice_in_dim(tot[k], me * nloc, nloc, axis=2)

    dm = lax.dynamic_slice_in_dim(dmod_all.reshape(N_DEV, DEPTH, 6 * D), me * nada, nada, axis=2)
    pad16 = lambda a: jnp.pad(a, ((0, 16 - N_DEV), (0, 0))).astype(BF16)
    g_ada = _wgrad("wgrad_ada", pad16(act_all), pad16(dm.reshape(N_DEV, DEPTH * nada)), 256, 16)
    grads["w_ada"] = g_ada.reshape(D, DEPTH, nada).transpose(1, 0, 2)

    delta, new_m, new_v = {}, {}, {}
    r2 = lambda a: a.reshape(DEPTH * D, nada)
    d_, m_, v_ = _adam_call("adam_ada", r2(W["w_ada"]), r2(grads["w_ada"]), r2(M["w_ada"]), r2(V["w_ada"]), 512)
    delta["w_ada"], new_m["w_ada"], new_v["w_ada"] = [t.reshape(W["w_ada"].shape) for t in (d_, m_, v_)]
    sshapes = [W[k].shape for k in _SMALL]
    d_, m_, v_ = _adam_call("adam_small", _pack([W[k] for k in _SMALL]), _pack([grads[k] for k in _SMALL]),
                            _pack([M[k] for k in _SMALL]), _pack([V[k] for k in _SMALL]), 4096)
    for dst, packed in ((delta, d_), (new_m, m_), (new_v, v_)):
        dst.update(zip(_SMALL, _unpack(packed, sshapes)))
    for k in _BIG:
        grads[k], delta[k], new_m[k], new_v[k] = [turned(a) for a in big_out[k]] if k == "w_ffn_in" else big_out[k]

    return (loss, dx[None], *[grads[k] for k in _WEIGHTS], *[delta[k] for k in _WEIGHTS],
            *[new_m[k] for k in _WEIGHTS], *[new_v[k] for k in _WEIGHTS])


def kernel(x, c, w_ada, b_ada, norm_mix_g, norm_ffn_g, w_in, conv_a_w, conf_dw_w, conf_dw_b, conf_ln_g, conf_ln_b, dn_conv_w, dn_a_log, dn_dt_bias, dn_norm_g, w_out, w_ffn_in, w_ffn_out, final_norm_g, loss_target, m_w_ada, m_b_ada, m_norm_mix_g, m_norm_ffn_g, m_w_in, m_conv_a_w, m_conf_dw_w, m_conf_dw_b, m_conf_ln_g, m_conf_ln_b, m_dn_conv_w, m_dn_a_log, m_dn_dt_bias, m_dn_norm_g, m_w_out, m_w_ffn_in, m_w_ffn_out, m_final_norm_g, v_w_ada, v_b_ada, v_norm_mix_g, v_norm_ffn_g, v_w_in, v_conv_a_w, v_conf_dw_w, v_conf_dw_b, v_conf_ln_g, v_conf_ln_b, v_dn_conv_w, v_dn_a_log, v_dn_dt_bias, v_dn_norm_g, v_w_out, v_w_ffn_in, v_w_ffn_out, v_final_norm_g):
    a = dict(locals())
    W = {k: a[k] for k in _WEIGHTS}
    M = {k: a["m_" + k] for k in _WEIGHTS}
    V = {k: a["v_" + k] for k in _WEIGHTS}
    return _step(x, c, loss_target, W, M, V)
```

```python
import functools

import jax
import jax.numpy as jnp
import numpy as np
from jax import lax
from jax.experimental import pallas as pl
from jax.experimental.pallas import tpu as pltpu

F32 = jnp.float32
BF16 = jnp.bfloat16

N_DEV = 8
D = 1024
DEPTH = 4
D_CONV = 256
D_CONF = 256
D_DN = 512
HEADS = 4
HD = 128
KA, KB, KC = 3, 31, 4
CHUNK = 64
D_FF = 2816
IN_COLS = 3336
IN_PAD = 3456
N_CONVCOL = 2048
EPS = 1e-6
LN_EPS = 1e-5
HALO = 32
VMEM_LIMIT = 56 * 1024 * 1024
DELTA_NB = 8

C_AB, C_AC, C_AV, C_BA, C_BG, C_Q, C_Z, C_GB = 0, 256, 512, 768, 1024, 1280, 2816, 3328

LR, B1, B2, AEPS, WD, STEP = 0.001, 0.9, 0.999, 1e-08, 0.01, 10


def _dot(a, b, dims, hi):
    if hi:
        return lax.dot_general(a.astype(F32), b.astype(F32), (dims, ((), ())), precision=lax.Precision.HIGHEST,
                               preferred_element_type=F32)
    return lax.dot_general(a.astype(BF16), b.astype(BF16), (dims, ((), ())), preferred_element_type=F32)


@functools.partial(jax.custom_vjp, nondiff_argnums=(2,))
def mm_nn(a, b, hi=False):
    return _dot(a, b, ((1,), (0,)), hi)


@functools.partial(jax.custom_vjp, nondiff_argnums=(2,))
def mm_nt(a, b, hi=False):
    return _dot(a, b, ((1,), (1,)), hi)


@functools.partial(jax.custom_vjp, nondiff_argnums=(2,))
def mm_tn(a, b, hi=False):
    return _dot(a, b, ((0,), (0,)), hi)


mm_nn.defvjp(lambda a, b, hi: (mm_nn(a, b, hi), (a, b)),
             lambda hi, r, g: (mm_nt(g, r[1], hi), mm_tn(r[0], g, hi)))
mm_nt.defvjp(lambda a, b, hi: (mm_nt(a, b, hi), (a, b)),
             lambda hi, r, g: (mm_nn(g, r[1], hi), mm_tn(g, r[0], hi)))
mm_tn.defvjp(lambda a, b, hi: (mm_tn(a, b, hi), (a, b)),
             lambda hi, r, g: (mm_nt(r[1], g, hi), mm_nn(r[0], g, hi)))


def _sigmoid(x):
    return 1.0 / (1.0 + jnp.exp(-x))


def _silu(x):
    return x * _sigmoid(x)


def _softplus(x):
    return jnp.maximum(x, 0.0) + jnp.log(1.0 + jnp.exp(-jnp.abs(x)))


def _iota2(shape, dim):
    return lax.broadcasted_iota(jnp.int32, shape, dim)


def _dot16(a, b):
    return jnp.dot(a.astype(BF16), b.astype(BF16), preferred_element_type=F32)


def _dot_3pass(a, b):
    ah = a.astype(BF16)
    bh = b.astype(BF16)
    al = (a - ah.astype(F32)).astype(BF16)
    bl = (b - bh.astype(F32)).astype(BF16)
    d = lambda x, y: jnp.dot(x, y, preferred_element_type=F32)
    return d(ah, bh) + (d(ah, bl) + d(al, bh))


@jax.custom_vjp
def _unit_lower_inverses(Xs):
    n = Xs[0].shape[0]
    r, c = _iota2((n, n), 0), _iota2((n, n), 1)
    eye = (r == c).astype(F32)

    def joins(b):
        s = b.bit_length() - 1
        return ((r >> (s + 1)) == (c >> (s + 1))) & (((r >> s) & 1) == 1) & (((c >> s) & 1) == 0)

    Ts = [eye + jnp.where(joins(1), x, 0.0) for x in Xs]
    b = 2
    while b < n:
        m = joins(b)
        Ys = [_dot16(jnp.where(m, x, 0.0), t) for x, t in zip(Xs, Ts)]
        Ts = [t + _dot16(t, y) for t, y in zip(Ts, Ys)]
        b *= 2
    Rs = [(eye - t) + _dot_3pass(x, t) for x, t in zip(Xs, Ts)]
    return [t + _dot16(t, r_) for t, r_ in zip(Ts, Rs)]


def _unit_lower_inverses_fwd(Xs):
    Ts = _unit_lower_inverses(Xs)
    return Ts, Ts


def _unit_lower_inverses_bwd(Ts, gs):
    inner = [mm_nt(g, t) for g, t in zip(gs, Ts)]
    return ([mm_tn(t, i) for t, i in zip(Ts, inner)],)


_unit_lower_inverses.defvjp(_unit_lower_inverses_fwd, _unit_lower_inverses_bwd)


@jax.custom_vjp
def _saved_inverses(Xs, Ts):
    return list(Ts)


_saved_inverses.defvjp(lambda Xs, Ts: (list(Ts), Ts),
                       lambda Ts, gs: (_unit_lower_inverses_bwd(Ts, gs)[0], [jnp.zeros_like(t) for t in Ts]))


def _delta_chunks(qs, ks, vs, gbs, Ss, Ts=None, keep=None):
    C = CHUNK
    nb = len(gbs)
    pairs = [(c, h) for c in range(nb) for h in range(HEADS)]
    each = lambda fn, *lists: [fn(*a) for a in zip(*lists)]
    row = _iota2((C, C), 0)
    col = _iota2((C, C), 1)
    causal = row >= col
    strict = row > col
    tri = causal.astype(F32)
    eye = (row == col).astype(F32)
    lane = _iota2((C, 128), 1)
    subl = _iota2((128, C), 0)
    last = (_iota2((C, 1), 0) == C - 1).astype(F32)

    gc_all = [mm_nn(tri, gb, True) for gb in gbs]
    gc_t = [g.T for g in gc_all]
    q = [qs[c][h] * (HD ** -0.5) for c, h in pairs]
    k = [ks[c][h] for c, h in pairs]
    v = [vs[c][h] for c, h in pairs]
    gcol = [jnp.sum(jnp.where(lane == h, gc_all[c], 0.0), axis=1, keepdims=True) for c, h in pairs]
    grow = [jnp.sum(jnp.where(subl == h, gc_t[c], 0.0), axis=0, keepdims=True) for c, h in pairs]
    beta = [jnp.sum(jnp.where(lane == HEADS + h, gbs[c], 0.0), axis=1, keepdims=True) for c, h in pairs]
    decay = each(lambda a, b: jnp.where(causal, jnp.exp(jnp.where(causal, a - b, 0.0)), 0.0), gcol, grow)
    kb = each(lambda a, b: a * b, k, beta)
    vb = each(lambda a, b: a * b, v, beta)
    kk = each(lambda a, b: mm_nt(a, b), kb, k)
    X = each(lambda a, d: -jnp.where(strict, a * d, 0.0), kk, decay)
    T = _unit_lower_inverses(X) if Ts is None else _saved_inverses(X, Ts)
    if keep is not None:
        keep.extend(T)
    eg = [jnp.exp(g) for g in gcol]
    u = each(lambda t, a: mm_nn(t, a), T, vb)
    w = each(lambda t, a, e: mm_nn(t, a * e), T, kb, eg)
    qk = each(lambda a, b, d: jnp.where(causal, mm_nt(a, b) * d, 0.0), q, k, decay)
    qg = each(lambda a, e: a * e, q, eg)
    g_last = [jnp.sum(g * last, axis=0, keepdims=True) for g in gcol]
    kd = each(lambda a, gl, g: a * jnp.exp(gl - g), k, g_last, gcol)
    eg_last = [jnp.exp(g) for g in g_last]

    outs = []
    for c in range(nb):
        sl = slice(c * HEADS, (c + 1) * HEADS)
        v_new = each(lambda a, b, S: a - mm_nn(b, S), u[sl], w[sl], Ss)
        oS = each(lambda a, S: mm_nn(a, S), qg[sl], Ss)
        outs.append(each(lambda a, b, n: a + mm_nn(b, n), oS, qk[sl], v_new))
        Ss = each(lambda S, e, a, n: S * e + mm_tn(a, n), Ss, eg_last[sl], kd[sl], v_new)
    return outs, Ss


def _split_chunks(ref, nb):
    return [[ref[c * CHUNK:(c + 1) * CHUNK, h * HD:(h + 1) * HD] for h in range(HEADS)] for c in range(nb)]


def _join_chunks(vals):
    return jnp.concatenate([jnp.concatenate(heads, axis=1) for heads in vals], axis=0)


def _hosted(body, n_in, n_out, rider, n_steps):
    if rider is None:
        return body
    n_ri, n_ro = len(rider.ins), len(rider.outs)

    def wrapped(*refs):
        ins, r_ins = refs[:n_in], refs[n_in:n_in + n_ri]
        outs = refs[n_in + n_ri:n_in + n_ri + n_out]
        r_outs = refs[n_in + n_ri + n_out:n_in + n_ri + n_out + n_ro]
        rest = refs[n_in + n_ri + n_out + n_ro:]
        scr, sems = rest[:len(rest) - 3], rest[len(rest) - 3:]

        @pl.when(pl.program_id(0) == 0)
        def _():
            rider.start(r_ins, r_outs, *sems)

        body(*ins, *outs, *scr)

        @pl.when(pl.program_id(0) == n_steps - 1)
        def _():
            rider.finish(r_ins, r_outs, *sems)

    return wrapped


def _host_call(body, name, grid, in_specs, out_specs, out_shape, scratch, operands, rider):
    out_specs, out_shape = list(out_specs), list(out_shape)
    n_in, n_out = len(in_specs), len(out_specs)
    if rider is not None:
        anyspec = pl.BlockSpec(memory_space=pl.ANY)
        in_specs = list(in_specs) + [anyspec] * len(rider.ins)
        out_specs += [anyspec] * len(rider.outs)
        out_shape += list(rider.outs)
        scratch = list(scratch) + rider.scratch()
        operands = list(operands) + list(rider.ins)
    return pl.pallas_call(
        _hosted(body, n_in, n_out, rider, grid[0]), name=name, grid=grid, in_specs=in_specs, out_specs=out_specs,
        out_shape=out_shape, scratch_shapes=scratch,
        compiler_params=pltpu.CompilerParams(dimension_semantics=("arbitrary",), vmem_limit_bytes=VMEM_LIMIT),
    )(*operands)


def _delta_fwd(qkv, gb, nb, rider=None):
    T = qkv.shape[0]
    nb = min(nb, T // CHUNK)
    rows = nb * CHUNK
    n = T // rows

    def body(q_ref, k_ref, v_ref, gb_ref, o_ref, ss_ref, t_ref, s_scr):
        @pl.when(pl.program_id(0) == 0)
        def _():
            s_scr[...] = jnp.zeros_like(s_scr)

        Ss = [s_scr[h] for h in range(HEADS)]
        for h in range(HEADS):
            ss_ref[0, h] = Ss[h]
        gbs = [gb_ref[c * CHUNK:(c + 1) * CHUNK, :] for c in range(nb)]
        kept = []
        outs, new_S = _delta_chunks(_split_chunks(q_ref, nb), _split_chunks(k_ref, nb), _split_chunks(v_ref, nb), gbs, Ss,
                                    keep=kept)
        o_ref[...] = _join_chunks(outs)
        s_scr[...] = jnp.stack(new_S)
        for c in range(nb):
            for h in range(HEADS):
                t_ref[c, h] = kept[c * HEADS + h]

    row = lambda w, j=0: pl.BlockSpec((rows, w), lambda i: (i, j))
    return _host_call(
        body, "delta_fwd", (n,),
        [row(D_DN, 0), row(D_DN, 1), row(D_DN, 2), row(128)],
        [row(D_DN), pl.BlockSpec((1, HEADS, HD, HD), lambda i: (i, 0, 0, 0)),
         pl.BlockSpec((nb, HEADS, CHUNK, CHUNK), lambda i: (i, 0, 0, 0))],
        [jax.ShapeDtypeStruct((T, D_DN), F32), jax.ShapeDtypeStruct((n, HEADS, HD, HD), F32),
         jax.ShapeDtypeStruct((T // CHUNK, HEADS, CHUNK, CHUNK), F32)],
        [pltpu.VMEM((HEADS, HD, HD), F32)], [qkv, qkv, qkv, gb], rider)


def _delta_bwd(qkv, gb, ss, inv, do, nb, rider=None):
    T = qkv.shape[0]
    nb = min(nb, T // CHUNK)
    rows = nb * CHUNK
    n = T // rows

    def body(q_ref, k_ref, v_ref, gb_ref, ss_ref, t_ref, do_ref, d_ref, ds_scr):
        @pl.when(pl.program_id(0) == 0)
        def _():
            ds_scr[...] = jnp.zeros_like(ds_scr)

        Ss = [ss_ref[0, h] for h in range(HEADS)]
        gbs = [gb_ref[c * CHUNK:(c + 1) * CHUNK, :] for c in range(nb)]
        Ts = [t_ref[c, h] for c in range(nb) for h in range(HEADS)]
        _, vjp = jax.vjp(functools.partial(_delta_chunks, Ts=Ts), _split_chunks(q_ref, nb), _split_chunks(k_ref, nb),
                         _split_chunks(v_ref, nb), gbs, Ss)
        dqs, dks, dvs, dgbs, dSs = vjp((_split_chunks(do_ref, nb), [ds_scr[h] for h in range(HEADS)]))
        d_ref[...] = jnp.concatenate([_join_chunks(dqs), _join_chunks(dks), _join_chunks(dvs),
                                      jnp.concatenate(dgbs, axis=0)], axis=1)
        ds_scr[...] = jnp.stack(dSs)

    row = lambda w, j=0: pl.BlockSpec((rows, w), lambda i: (n - 1 - i, j))
    return _host_call(
        body, "delta_bwd", (n,),
        [row(D_DN, 0), row(D_DN, 1), row(D_DN, 2), row(128),
         pl.BlockSpec((1, HEADS, HD, HD), lambda i: (n - 1 - i, 0, 0, 0)),
         pl.BlockSpec((nb, HEADS, CHUNK, CHUNK), lambda i: (n - 1 - i, 0, 0, 0)), row(D_DN)],
        [row(3 * D_DN + 128)], [jax.ShapeDtypeStruct((T, 3 * D_DN + 128), F32)],
        [pltpu.VMEM((HEADS, HD, HD), F32)], [qkv, qkv, qkv, gb, ss, inv, do], rider)


def _cols(arr, width, index, first_row=0, block_row=None):
    return (arr, width, index, first_row, block_row)


def _rowwise(name, fn, tiled, consts, out_tiled, out_acc, tm, rows=None, rider=None, layer=None, prev=None):
    tiled = [t if isinstance(t, tuple) else (t, t.shape[-1], 0, 0, None) for t in tiled]
    T = tiled[0][0].shape[-2] if rows is None else rows
    tm = min(tm, T)
    assert T % tm == 0 and all(t[3] % tm == 0 for t in tiled)
    n_t, n_c, n_o, n_a = len(tiled), len(consts), len(out_tiled), len(out_acc)

    n_ri = len(rider.ins) if rider else 0
    n_ro = len(rider.outs) if rider else 0
    n_steps = T // tm

    def body(*refs):
        n_in = n_t + n_c + n_ri + (n_o if layer is not None else 0)
        r_ins = refs[n_t + n_c:n_t + n_c + n_ri]
        o_refs = refs[n_in:n_in + n_o]
        a_refs = refs[n_in + n_o:n_in + n_o + n_a]
        r_outs = refs[n_in + n_o + n_a:n_in + n_o + n_a + n_ro]
        sems = refs[n_in + n_o + n_a + n_ro:]
        if rider:
            @pl.when(pl.program_id(0) == 0)
            def _():
                rider.start(r_ins, r_outs, *sems)

        ins = [r[...] for r in refs[:n_t + n_c]]
        outs = fn(*ins)
        for r, val in zip(o_refs, outs[:n_o]):
            r[...] = val.astype(r.dtype)
        if n_a:
            @pl.when(pl.program_id(0) == 0)
            def _():
                for r in a_refs:
                    r[...] = jnp.zeros_like(r)
            for r, val in zip(a_refs, outs[n_o:]):
                r[...] += val
        if rider:
            @pl.when(pl.program_id(0) == n_steps - 1)
            def _():
                rider.finish(r_ins, r_outs, *sems)

    def const_spec(a):
        nd = a.ndim
        return pl.BlockSpec(a.shape, lambda i: (0,) * nd, pipeline_mode=pl.Buffered(1))

    def tile_spec(arr, w, j, r0, block_row):
        row = block_row if block_row is not None else (lambda i: i + r0 // tm)
        if arr.ndim == 3:
            return pl.BlockSpec((None, tm, w), lambda i: (layer, row(i), j))
        return pl.BlockSpec((tm, w), lambda i: (row(i), j))

    in_specs = [tile_spec(*t) for t in tiled]
    in_specs += [const_spec(a) for a in consts]
    if layer is None:
        out_specs = [pl.BlockSpec((tm, w), lambda i: (i, 0)) for (w, _) in out_tiled]
        out_shape = [jax.ShapeDtypeStruct((T, w), dt) for (w, dt) in out_tiled]
    else:
        out_specs = [pl.BlockSpec((None, tm, w), lambda i: (layer, i, 0)) for (w, _) in out_tiled]
        out_shape = [jax.ShapeDtypeStruct((DEPTH, T, w), dt) for (w, dt) in out_tiled]
    out_specs += [pl.BlockSpec(s, lambda i: (0, 0)) for (s, _) in out_acc]
    out_shape += [jax.ShapeDtypeStruct(s, dt) for (s, dt) in out_acc]
    operands = [t[0] for t in tiled] + list(consts)
    scratch = []
    aliases = {}
    if rider:
        anyspec = pl.BlockSpec(memory_space=pl.ANY)
        in_specs += [anyspec] * n_ri
        out_specs += [anyspec] * n_ro
        out_shape += list(rider.outs)
        operands += list(rider.ins)
        scratch = rider.scratch()
    n_prev = 0
    if layer is not None:
        assert rider is None and not out_acc
        if prev is None:
            prev = [jnp.zeros(o.shape, o.dtype) for o in out_shape]
        n_prev = len(prev)
        aliases = {len(operands) + i: i for i in range(n_prev)}
        in_specs += [pl.BlockSpec(memory_space=pl.ANY)] * n_prev
        operands += list(prev)
    return pl.pallas_call(
        body, name=name, grid=(n_steps,), in_specs=in_specs, out_specs=out_specs, out_shape=out_shape, scratch_shapes=scratch,
        input_output_aliases=aliases,
        compiler_params=pltpu.CompilerParams(dimension_semantics=("arbitrary",), vmem_limit_bytes=VMEM_LIMIT),
    )(*operands)


def _colsum(x):
    return jnp.sum(x, axis=0, keepdims=True)


def _rms(x):
    r = lax.rsqrt(jnp.mean(x * x, axis=-1, keepdims=True) + EPS)
    return x * r, r


def _rms_bwd(dxn, xn, r):
    return r * (dxn - xn * jnp.mean(dxn * xn, axis=-1, keepdims=True))


def _normproj_fwd(name, x, sh, sc, g, w, tm, rider=None, w_t=False, out_dtype=F32):
    def fn(x, sh, sc, g, w):
        xn, _ = _rms(x)
        h = (xn * (g * (1.0 + sc)) + sh).astype(BF16)
        return lax.dot_general(h, w, (((1,), (1 if w_t else 0,)), ((), ())), preferred_element_type=F32), h

    return _rowwise(name, fn, [x], [sh, sc, g, w], [(w.shape[0 if w_t else 1], out_dtype), (D, BF16)], [], tm, rider=rider)


def _normproj_bwd(name, x, dpre, dres, sc, g, w, tm, rider=None, w_t=False):
    def fn(x, dpre, dres, sc, g, w):
        xn, r = _rms(x)
        dh = lax.dot_general(dpre, w, (((1,), (0 if w_t else 1,)), ((), ())), preferred_element_type=F32)
        da = _colsum(dh * xn)
        dx = _rms_bwd(dh * (g * (1.0 + sc)), xn, r) + dres
        return dx, _colsum(dh), da * g, da * (1.0 + sc)

    vec = ((1, D), F32)
    return _rowwise(name, fn, [x, dpre, dres], [sc, g, w], [(D, F32)], [vec, vec, vec], tm, rider=rider)


def _stage2(a_b, blk, cp, u1c, qp, kp, vp, bb, lg, lb, alog, dtb):
    y_a = a_b * cp
    u1 = u1c + bb
    mu = jnp.mean(u1, axis=-1, keepdims=True)
    uc = u1 - mu
    var = jnp.mean(uc * uc, axis=-1, keepdims=True)
    y_b = _silu(uc * lax.rsqrt(var + LN_EPS) * lg + lb)

    def l2(t):
        t = _silu(t)
        return t * lax.rsqrt(jnp.sum(t * t, axis=-1, keepdims=True) + EPS)

    q = [l2(t) for t in qp]
    k = [l2(t) for t in kp]
    v = _silu(vp)
    lane = _iota2(blk.shape, 1)
    gdec = -jnp.exp(alog) * _softplus(blk + dtb)
    gb = jnp.where(lane < HEADS, gdec, jnp.where(lane < 2 * HEADS, _sigmoid(blk), 0.0))
    return y_a, y_b, q, k, v, gb


def _heads_of(x, base=0):
    return [x[:, base + h * HD:base + (h + 1) * HD] for h in range(HEADS)]


def _stage2_fwd(proj, convout, bb, lg, lb, alog, dtb, tm):
    def fn(a_b, blk, co, bb, lg, lb, alog, dtb):
        y_a, y_b, q, k, v, gb = _stage2(a_b, blk, co[:, 0:256], co[:, 256:512], _heads_of(co, 512), _heads_of(co, 1024),
                                        co[:, 1536:2048], bb, lg, lb, alog, dtb)
        return jnp.concatenate([y_a, y_b], axis=1), jnp.concatenate(q + k + [v], axis=1), gb

    return _rowwise("stage2_fwd", fn, [_cols(proj, 256, 0), _cols(proj, 128, C_GB // 128), convout],
                    [bb, lg, lb, alog, dtb], [(512, BF16), (1536, F32), (128, F32)], [], tm)


def _stage2_bwd(proj, convout, dy_ab, dq, dk, dv, dgb, bb, lg, lb, alog, dtb, tm, rider=None):
    def fn(a_b, blk, co, dy_ab, dq, dk, dv, dgb, bb, lg, lb, alog, dtb):
        args = (a_b, blk, co[:, 0:256], co[:, 256:512], _heads_of(co, 512), _heads_of(co, 1024), co[:, 1536:2048],
                bb, lg, lb, alog, dtb)
        _, vjp = jax.vjp(_stage2, *args)
        ct = (dy_ab[:, 0:256], dy_ab[:, 256:512], _heads_of(dq), _heads_of(dk), dv, dgb)
        da_b, dblk, dcp, du1c, dqp, dkp, dvp, dbb, dlg, dlb, dalog, ddtb = vjp(ct)
        dco = jnp.concatenate([dcp, du1c] + dqp + dkp + [dvp], axis=1)
        return dco, da_b, dblk, dbb, dlg, dlb, dalog, ddtb

    v256, v128 = ((1, 256), F32), ((1, 128), F32)
    return _rowwise("stage2_bwd", fn,
                    [_cols(proj, 256, 0), _cols(proj, 128, C_GB // 128), convout, dy_ab, dq, dk, dv, dgb],
                    [bb, lg, lb, alog, dtb], [(N_CONVCOL, F32), (256, F32), (128, F32)],
                    [v256, v256, v256, v128, v128], tm, rider=rider)


def _stage3(o, z, dng):
    ys = []
    for oh, zh in zip(o, z):
        on = oh * lax.rsqrt(jnp.mean(oh * oh, axis=-1, keepdims=True) + EPS)
        ys.append(on * dng * _silu(zh))
    return ys


def _outproj_fwd(x, o, proj, y_ab, g1, dng, wout, tm, rider=None):
    def fn(x, o, z0, z1, z2, z3, y_ab, g1, dng, wout):
        y_c = _stage3(_heads_of(o), [z0, z1, z2, z3], dng)
        ycat = jnp.concatenate([y_ab] + [t.astype(BF16) for t in y_c], axis=1)
        mix = jnp.dot(ycat, wout, preferred_element_type=F32)
        return x + g1 * mix, mix, ycat

    return _rowwise("outproj_fwd", fn, [x, o] + _z_heads(proj) + [y_ab],
                    [g1, dng, wout], [(D, F32), (D, F32), (D, BF16)], [], tm, rider=rider)


def _z_heads(proj):
    return [_cols(proj, HD, C_Z // HD + h) for h in range(HEADS)]


def _outproj_bwd(dx1, mix, o, proj, g1, dng, wout, tm, rider=None):
    def fn(dx1, mix, o, z0, z1, z2, z3, g1, dng, wout):
        dmix = (dx1 * g1).astype(BF16)
        dycat = lax.dot_general(dmix, wout, (((1,), (1,)), ((), ())), preferred_element_type=F32)
        _, vjp = jax.vjp(_stage3, _heads_of(o), [z0, z1, z2, z3], dng)
        do, dz, ddng = vjp(_heads_of(dycat, 512))
        return (dmix, dycat[:, 0:512], jnp.concatenate(do, axis=1), jnp.concatenate(dz, axis=1),
                _colsum(dx1 * mix), ddng)

    return _rowwise("outproj_bwd", fn, [dx1, mix, o] + _z_heads(proj), [g1, dng, wout],
                    [(D, BF16), (512, F32), (512, F32), (512, F32)], [((1, D), F32), ((1, HD), F32)], tm, rider=rider)


_CONV_BLOCKS = ((0, 256, KA), (256, 512, KB), (512, 2048, KC))
_CONV_STRIP = 256


_CONV_ROWS = 32


def _conv_inputs(proj_ref, rows):
    a_c, a_v = proj_ref[rows, C_AC:C_AC + 256], proj_ref[rows, C_AV:C_AV + 256]
    b_a, b_g = proj_ref[rows, C_BA:C_BA + 256], proj_ref[rows, C_BG:C_BG + 256]
    return a_c, a_v, b_a, _sigmoid(b_g)


def _shifted_copies(ext, phases, tm):
    n = tm + HALO - 8
    for b in range(1, 8):
        phases[b - 1] = ext[pl.ds(b, n), 256:512]


def _rows_at(ext, phases, row, col, kw):
    a, b = divmod(row, 8)
    if kw == KB and b:
        return phases[b - 1, pl.ds(8 * a, _CONV_ROWS), :]
    return ext[pl.ds(row, _CONV_ROWS), col:col + _CONV_STRIP]


def _conv_fwd(proj, wa, wb, wc, tm):
    T = proj.shape[0]
    tm = min(tm, T)

    def body(proj_ref, wa_ref, wb_ref, wc_ref, out_ref, ext, phases):
        @pl.when(pl.program_id(0) == 0)
        def _():
            ext[0:HALO, :] = jnp.zeros((HALO, N_CONVCOL), F32)

        a_c, a_v, b_a, sg = _conv_inputs(proj_ref, slice(None))
        ext[HALO:HALO + tm, 0:256] = a_c * a_v
        ext[HALO:HALO + tm, 256:512] = b_a * sg
        ext[HALO:HALO + tm, 512:2048] = proj_ref[:, C_Q:C_Q + 1536]
        _shifted_copies(ext, phases, tm)
        for r0 in range(0, tm, _CONV_ROWS):
            for (c0, c1, kw), w_ref in zip(_CONV_BLOCKS, (wa_ref, wb_ref, wc_ref)):
                for s0 in range(c0, c1, _CONV_STRIP):
                    acc = jnp.zeros((_CONV_ROWS, _CONV_STRIP), F32)
                    for k in range(kw):
                        acc += (w_ref[k:k + 1, s0 - c0:s0 - c0 + _CONV_STRIP]
                                * _rows_at(ext, phases, r0 + HALO - (kw - 1) + k, s0, kw))
                    out_ref[r0:r0 + _CONV_ROWS, s0:s0 + _CONV_STRIP] = acc
        ext[0:HALO, :] = ext[tm:tm + HALO, :]

    full = lambda a: pl.BlockSpec(a.shape, lambda i: (0, 0))
    return pl.pallas_call(
        body, name="conv_fwd", grid=(T // tm,),
        in_specs=[pl.BlockSpec((tm, IN_PAD), lambda i: (i, 0)), full(wa), full(wb), full(wc)],
        out_specs=pl.BlockSpec((tm, N_CONVCOL), lambda i: (i, 0)),
        out_shape=jax.ShapeDtypeStruct((T, N_CONVCOL), F32),
        scratch_shapes=[pltpu.VMEM((HALO + tm, N_CONVCOL), F32), pltpu.VMEM((7, tm + HALO - 8, 256), F32)],
        compiler_params=pltpu.CompilerParams(dimension_semantics=("arbitrary",), vmem_limit_bytes=VMEM_LIMIT),
    )(proj, wa, wb, wc)


def _conv_bwd(proj, dco, da_b, dz, dblk, wa, wb, wc, tm, rider=None):
    T = proj.shape[0]
    tm = min(tm, T)
    n = T // tm

    def body(proj_ref, dco_ref, dab_ref, dz_ref, dblk_ref, wa_ref, wb_ref, wc_ref,
             dproj_ref, dwa_ref, dwb_ref, dwc_ref, ext, acc_a, acc_b, acc_c, phases):
        @pl.when(pl.program_id(0) == 0)
        def _():
            ext[tm:tm + HALO, :] = jnp.zeros((HALO, N_CONVCOL), F32)
            acc_a[...] = jnp.zeros_like(acc_a)
            acc_b[...] = jnp.zeros_like(acc_b)
            acc_c[...] = jnp.zeros_like(acc_c)

        ext[0:tm, :] = dco_ref[...]
        _shifted_copies(ext, phases, tm)

        def taps(w_ref, acc_ref, kw, c0, wc0, xin, r0):
            dx = jnp.zeros((_CONV_ROWS, _CONV_STRIP), F32)
            for k in range(kw):
                sh = _rows_at(ext, phases, r0 + kw - 1 - k, c0, kw)
                dx += w_ref[k:k + 1, wc0:wc0 + _CONV_STRIP] * sh
                pr = sh * xin
                part = pr[0:8]
                for g in range(8, _CONV_ROWS, 8):
                    part += pr[g:g + 8]
                acc_ref[8 * k:8 * k + 8, wc0:wc0 + _CONV_STRIP] += part
            return dx

        for r0 in range(0, tm, _CONV_ROWS):
            rows = slice(r0, r0 + _CONV_ROWS)
            a_c, a_v, b_a, sg = _conv_inputs(proj_ref, rows)
            dp = taps(wa_ref, acc_a, KA, 0, 0, a_c * a_v, r0)
            dproj_ref[rows, C_AC:C_AC + 256] = (dp * a_v).astype(BF16)
            dproj_ref[rows, C_AV:C_AV + 256] = (dp * a_c).astype(BF16)
            du0 = taps(wb_ref, acc_b, KB, 256, 0, b_a * sg, r0)
            dproj_ref[rows, C_BA:C_BA + 256] = (du0 * sg).astype(BF16)
            dproj_ref[rows, C_BG:C_BG + 256] = (du0 * b_a * sg * (1.0 - sg)).astype(BF16)
            for s0 in range(0, 1536, _CONV_STRIP):
                dq = taps(wc_ref, acc_c, KC, 512 + s0, s0, proj_ref[rows, C_Q + s0:C_Q + s0 + _CONV_STRIP], r0)
                dproj_ref[rows, C_Q + s0:C_Q + s0 + _CONV_STRIP] = dq.astype(BF16)
        ext[tm:tm + HALO, :] = ext[0:HALO, :]

        dproj_ref[:, C_AB:C_AB + 256] = dab_ref[...].astype(BF16)
        dproj_ref[:, C_Z:C_Z + 512] = dz_ref[...].astype(BF16)
        dproj_ref[:, C_GB:C_GB + 128] = dblk_ref[...].astype(BF16)

        @pl.when(pl.program_id(0) == n - 1)
        def _():
            for acc_ref, dw_ref, kw in ((acc_a, dwa_ref, KA), (acc_b, dwb_ref, KB), (acc_c, dwc_ref, KC)):
                dw_ref[...] = jnp.zeros_like(dw_ref)
                for k in range(kw):
                    dw_ref[k:k + 1, :] = _colsum(acc_ref[8 * k:8 * k + 8, :])

    rev = lambda w: pl.BlockSpec((tm, w), lambda i: (n - 1 - i, 0))
    full = lambda a: pl.BlockSpec(a.shape, lambda i: (0, 0))
    return _host_call(
        body, "conv_bwd", (n,),
        [rev(IN_PAD), rev(N_CONVCOL), rev(256), rev(512), rev(128), full(wa), full(wb), full(wc)],
        [rev(IN_PAD), full(wa), full(wb), full(wc)],
        [jax.ShapeDtypeStruct((T, IN_PAD), BF16), jax.ShapeDtypeStruct(wa.shape, F32),
         jax.ShapeDtypeStruct(wb.shape, F32), jax.ShapeDtypeStruct(wc.shape, F32)],
        [pltpu.VMEM((tm + HALO, N_CONVCOL), F32), pltpu.VMEM((8 * KA, 256), F32),
         pltpu.VMEM((8 * KB, 256), F32), pltpu.VMEM((8 * KC, 1536), F32), pltpu.VMEM((7, tm + HALO - 8, 256), F32)],
        [proj, dco, da_b, dz, dblk, wa, wb, wc], rider)


def _ffn_out_fwd(x1, gu, g2, wfo, tm, rider=None):
    def fn(x1, gu, g2, wfo):
        s = (_silu(gu[:, :D_FF].astype(F32)) * gu[:, D_FF:].astype(F32)).astype(BF16)
        f = jnp.dot(s, wfo, preferred_element_type=F32)
        return x1 + g2 * f, f

    return _rowwise("ffnout_fwd", fn, [x1, gu], [g2, wfo], [(D, F32), (D, F32)], [], tm, rider=rider)


def _ffn_out_bwd(dx2, gu, f, g2, wfo, tm, rider=None):
    def fn(dx2, gu, f, g2, wfo):
        gate, up = gu[:, :D_FF].astype(F32), gu[:, D_FF:].astype(F32)
        sg = _sigmoid(gate)
        sl = gate * sg
        df = (dx2 * g2).astype(BF16)
        ds = lax.dot_general(df, wfo, (((1,), (1,)), ((), ())), preferred_element_type=F32)
        dgate = ds * up * (sg * (1.0 + gate * (1.0 - sg)))
        dgu = jnp.concatenate([dgate.astype(BF16), (ds * sl).astype(BF16)], axis=1)
        return dgu, sl * up, df, _colsum(dx2 * f)

    return _rowwise("ffnout_bwd", fn, [dx2, gu, f], [g2, wfo], [(2 * D_FF, BF16), (D_FF, BF16), (D, BF16)],
                    [((1, D), F32)], tm, rider=rider)


def _loss_bwd(x, tgt, gfin, tm):
    def fn(x, tgt, gfin):
        xn, r = _rms(x)
        e = xn * gfin - tgt
        loss = 0.5 * jnp.sum(jnp.mean(e * e, axis=-1, keepdims=True), axis=0, keepdims=True)
        dy = e * (1.0 / D)
        return _rms_bwd(dy * gfin, xn, r), jnp.broadcast_to(loss, (1, 128)), _colsum(dy * xn)

    return _rowwise("loss_bwd", fn, [x, tgt], [gfin], [(D, F32)], [((1, 128), F32), ((1, D), F32)], tm)


def _wgrad(name, a, b, bm, bk):
    T, M = a.shape
    N = b.shape[1]
    bk = min(bk, T)

    def body(a_ref, b_ref, o_ref):
        @pl.when(pl.program_id(1) == 0)
        def _():
            o_ref[...] = jnp.zeros_like(o_ref)

        o_ref[...] += lax.dot_general(a_ref[...], b_ref[...], (((0,), (0,)), ((), ())), preferred_element_type=F32)

    return pl.pallas_call(
        body, name=name, grid=(M // bm, T // bk),
        in_specs=[pl.BlockSpec((bk, bm), lambda i, k: (k, i)), pl.BlockSpec((bk, N), lambda i, k: (k, 0))],
        out_specs=pl.BlockSpec((bm, N), lambda i, k: (i, 0)),
        out_shape=jax.ShapeDtypeStruct((M, N), F32),
        compiler_params=pltpu.CompilerParams(dimension_semantics=("arbitrary", "arbitrary"), vmem_limit_bytes=VMEM_LIMIT),
    )(a, b)


IN_BLOCK = 512


def _wgrad_in_blocks(name, a, b, bm, bk):
    T, M = a.shape
    N = b.shape[1]
    bk = min(bk, T)
    nk = T // bk
    per = IN_COLS // N_DEV
    win = IN_BLOCK + 128

    def body(a_ref, b_ref, o_ref, acc):
        @pl.when(pl.program_id(1) == 0)
        def _():
            acc[...] = jnp.zeros_like(acc)

        acc[...] += lax.dot_general(a_ref[...], b_ref[...], (((0,), (0,)), ((), ())), preferred_element_type=F32)

        @pl.when(pl.program_id(1) == nk - 1)
        def _():
            for j in range(N_DEV):
                q, r = divmod(per * j, 128)
                w = acc[:, 128 * q:128 * q + win]
                if r:
                    w = pltpu.roll(w, win - r, axis=1)
                o_ref[j] = w[:, :IN_BLOCK]

    assert 128 * ((per * (N_DEV - 1)) // 128) + win <= N
    return pl.pallas_call(
        body, name=name, grid=(M // bm, nk),
        in_specs=[pl.BlockSpec((bk, bm), lambda i, k: (k, i)), pl.BlockSpec((bk, N), lambda i, k: (k, 0))],
        out_specs=pl.BlockSpec((N_DEV, bm, IN_BLOCK), lambda i, k: (0, i, 0)),
        out_shape=jax.ShapeDtypeStruct((N_DEV, M, IN_BLOCK), F32),
        scratch_shapes=[pltpu.VMEM((bm, N), F32)],
        compiler_params=pltpu.CompilerParams(dimension_semantics=("arbitrary", "arbitrary"), vmem_limit_bytes=VMEM_LIMIT),
    )(a, b)


def _adamw(w, g, m, v):
    m = B1 * m + (1.0 - B1) * g
    v = B2 * v + (1.0 - B2) * (g * g)
    m_hat = m / (1.0 - B1 ** STEP)
    v_hat = v / (1.0 - B2 ** STEP)
    return -LR * (m_hat / (jnp.sqrt(v_hat) + AEPS) + WD * w), m, v


def _adam_call(name, w, g, m, v, tm):
    C = w.shape[1]
    return _rowwise(name, _adamw, [w, g, m, v], [], [(C, F32)] * 3, [], tm)


def _adam_layer(name, g, w, m, v, tm, layer, prev):
    C = g.shape[1]
    return _rowwise(name, lambda g, w, m, v: (g,) + _adamw(w, g, m, v), [g, w, m, v], [], [(C, F32)] * 4, [], tm,
                    layer=layer, prev=prev)


def _reduce_sum(name, own, recv, tm):
    n, R, C = recv.shape
    flat = recv.reshape(n * R, C)
    fn = lambda a, b, r0, r1, r2: ((((a + b) + r0.astype(F32)) + r1.astype(F32)) + r2.astype(F32),)
    return _rowwise(name, fn, _own_blocks(own, R, tm) + [_cols(flat, C, 0, j * R) for j in range(n)], [], [(C, F32)], [],
                    tm, rows=R)[0]


def _pair_add(name, g, from_sib, tm):
    n, R, C = from_sib.shape
    n_r = R // tm
    mine = _cols(g.reshape(N_DEV * R, C), C, 0,
                 block_row=lambda i: (2 * (i // n_r) + lax.axis_index("c")) * n_r + i % n_r)

    return _rowwise(name, lambda a, b: (a + b,), [mine, from_sib.reshape(n * R, C)], [], [(C, BF16)], [], tm, rows=n * R)[0]


def _own_blocks(own, R, tm):
    g, from_sib = own
    C = g.shape[-1]
    n_r = R // tm
    chip = lambda: 2 * lax.axis_index("x") + lax.axis_index("y")
    return [_cols(g.reshape(N_DEV * R, C), C, 0, block_row=lambda i: (2 * chip() + lax.axis_index("c")) * n_r + i),
            _cols(from_sib.reshape(4 * R, C), C, 0, block_row=lambda i: chip() * n_r + i)]


def _reduce_adam(name, own, recv, w, m, v, tm, layer, prev):
    n, R, C = recv.shape
    flat = recv.reshape(n * R, C)

    def fn(a, b, r0, r1, r2, w, m, v):
        g = (((a + b) + r0.astype(F32)) + r1.astype(F32)) + r2.astype(F32)
        return (g,) + _adamw(w, g, m, v)

    return _rowwise(name, fn, _own_blocks(own, R, tm) + [_cols(flat, C, 0, j * R) for j in range(n)] + [w, m, v], [],
                    [(C, F32)] * 4, [], tm, rows=R, layer=layer, prev=prev)


_OFFSETS = [(dx, dy, dc) for dx in (0, 1) for dy in (0, 1) for dc in (0, 1)][1:]
_MESH = pl.DeviceIdType.MESH


def _coords():
    return lax.axis_index("x"), lax.axis_index("y"), lax.axis_index("c")


def _flip(me, off):
    return tuple((1 - m) if d else m for m, d in zip(me, off))


def _linear(p):
    return 4 * p[0] + 2 * p[1] + p[2]


_CHIP_FLIPS = ((1, 0), (0, 1), (1, 1))


class _Rider:
    def __init__(self, ins, outs, n_remote, n_local, start, finish):
        self.ins, self.outs, self.n_remote, self.n_local, self.start, self.finish = ins, outs, n_remote, n_local, start, finish

    def scratch(self):
        return [pltpu.SemaphoreType.DMA((self.n_remote,)), pltpu.SemaphoreType.DMA((self.n_remote,)),
                pltpu.SemaphoreType.DMA((max(self.n_local, 1),))]


def _run_rider(name, rider):
    def body(*refs):
        n_i, n_o = len(rider.ins), len(rider.outs)
        rider.start(refs[:n_i], refs[n_i:n_i + n_o], *refs[n_i + n_o:])
        rider.finish(refs[:n_i], refs[n_i:n_i + n_o], *refs[n_i + n_o:])

    anyspec = pl.BlockSpec(memory_space=pl.ANY)
    return pl.pallas_call(
        body, name=name, in_specs=[anyspec] * len(rider.ins), out_specs=[anyspec] * len(rider.outs),
        out_shape=list(rider.outs), scratch_shapes=rider.scratch(),
    )(*rider.ins)


def _gather_rider(arrs, layer):
    n = len(arrs)

    def parts(ins, outs, send, recv, loc):
        x, y, c = _coords()
        me, sib = (x, y, c), (x, y, 1 - c)
        chips = [((1 - x) if dx else x, (1 - y) if dy else y) for dx, dy in _CHIP_FLIPS]

        def copy(a, k, block, to, own=False):
            slot = outs[a].at[_linear(block)]
            return pltpu.make_async_remote_copy(src_ref=ins[a].at[layer] if own else slot, dst_ref=slot,
                                                send_sem=send.at[a * 7 + k], recv_sem=recv.at[a * 7 + k],
                                                device_id=to, device_id_type=_MESH)

        local = [pltpu.make_async_copy(ins[a].at[layer], outs[a].at[_linear(me)], loc.at[a]) for a in range(n)]
        first = []
        for a in range(n):
            first.append(copy(a, 0, me, sib, own=True))
            first += [copy(a, 1 + j, me, (*chip, c), own=True) for j, chip in enumerate(chips)]
        return copy, local, first, me, sib, chips, c

    def start(*refs):
        _, local, first, *_ = parts(*refs)
        for cp in local + first:
            cp.start()

    def finish(*refs):
        copy, local, first, me, sib, chips, c = parts(*refs)
        passed = []
        for j, chip in enumerate(chips):
            for a in range(n):
                copy(a, 1 + j, (*chip, c), me).wait_recv()
                cp = copy(a, 4 + j, (*chip, c), sib)
                cp.start()
                passed.append(cp)
        for a in range(n):
            copy(a, 0, sib, me).wait_recv()
            for j, chip in enumerate(chips):
                copy(a, 4 + j, (*chip, 1 - c), me).wait_recv()
        for cp in first + passed:
            cp.wait_send()
        for cp in local:
            cp.wait()

    outs = [jax.ShapeDtypeStruct((N_DEV,) + a.shape[1:], a.dtype) for a in arrs]
    return _Rider(list(arrs), outs, 7 * n, n, start, finish)


def _simple_rider(ins, outs, n_remote, make):
    def start(*refs):
        for cp in make(*refs):
            cp.start()

    def finish(*refs):
        for cp in make(*refs):
            cp.wait()

    return _Rider(ins, outs, n_remote, 0, start, finish)


def _join_riders(a, b):
    assert b.sem0 == a.n_remote and a.n_local == 0 and b.n_local == 0
    n_i, n_o = len(a.ins), len(a.outs)

    def both(fa, fb):
        def run(ins, outs, send, recv, loc):
            fa(ins[:n_i], outs[:n_o], send, recv, loc)
            fb(ins[n_i:], outs[n_o:], send, recv, loc)
        return run

    return _Rider(a.ins + b.ins, a.outs + b.outs, a.n_remote + b.n_remote, 0, both(a.start, b.start), both(a.finish, b.finish))


def _pair_rider(arrs, sem0=0):
    def make(ins, outs, send, recv, loc):
        x, y, c = _coords()
        return [pltpu.make_async_remote_copy(src_ref=ins[a].at[2 * s_ + 1 - c], dst_ref=outs[a].at[s_],
                                             send_sem=send.at[sem0 + 4 * a + s_], recv_sem=recv.at[sem0 + 4 * a + s_],
                                             device_id=(x, y, 1 - c), device_id_type=_MESH)
                for a in range(len(arrs)) for s_ in range(4)]

    r = _simple_rider(list(arrs), [jax.ShapeDtypeStruct((4,) + a.shape[1:], a.dtype) for a in arrs], 4 * len(arrs), make)
    r.sem0 = sem0
    return r


def _chip_rider(arrs, sem0=0):
    nf = len(_CHIP_FLIPS)

    def make(ins, outs, send, recv, loc):
        x, y, c = _coords()
        copies = []
        for a in range(len(arrs)):
            for k, (dx, dy) in enumerate(_CHIP_FLIPS):
                px, py = (1 - x) if dx else x, (1 - y) if dy else y
                copies.append(pltpu.make_async_remote_copy(
                    src_ref=ins[a].at[2 * px + py], dst_ref=outs[a].at[k], send_sem=send.at[sem0 + a * nf + k],
                    recv_sem=recv.at[sem0 + a * nf + k], device_id=(px, py, c), device_id_type=_MESH))
        return copies

    r = _simple_rider(list(arrs), [jax.ShapeDtypeStruct((nf,) + a.shape[1:], a.dtype) for a in arrs], nf * len(arrs), make)
    r.sem0 = sem0
    return r


def _small_allgather(name, packed, rider=None):
    R = packed.shape[0]
    n_ri = len(rider.ins) if rider else 0
    n_ro = len(rider.outs) if rider else 0

    def body(*refs):
        in_ref, r_ins = refs[0], refs[1:1 + n_ri]
        all_ref, sum_ref = refs[1 + n_ri:3 + n_ri]
        r_outs = refs[3 + n_ri:3 + n_ri + n_ro]
        send, recv = refs[3 + n_ri + n_ro:5 + n_ri + n_ro]
        r_sems = refs[5 + n_ri + n_ro:]
        if rider:
            rider.start(r_ins, r_outs, *r_sems)
        me = _coords()
        my = _linear(me)
        all_ref[my] = in_ref[...]
        copies = []
        for k, off in enumerate(_OFFSETS):
            cp = pltpu.make_async_remote_copy(src_ref=in_ref, dst_ref=all_ref.at[my], send_sem=send.at[k], recv_sem=recv.at[k],
                                              device_id=_flip(me, off), device_id_type=_MESH)
            cp.start()
            copies.append(cp)
        for cp in copies:
            cp.wait()
        acc = all_ref[0]
        for j in range(1, N_DEV):
            acc = acc + all_ref[j]
        sum_ref[...] = acc
        if rider:
            rider.finish(r_ins, r_outs, *r_sems)

    vm = pl.BlockSpec(memory_space=pltpu.VMEM)
    anyspec = pl.BlockSpec(memory_space=pl.ANY)
    return pl.pallas_call(
        body, name=name, in_specs=[vm] + [anyspec] * n_ri, out_specs=[vm, vm] + [anyspec] * n_ro,
        out_shape=[jax.ShapeDtypeStruct((N_DEV, R, 128), F32), jax.ShapeDtypeStruct((R, 128), F32)] + (list(rider.outs) if rider else []),
        scratch_shapes=[pltpu.SemaphoreType.DMA((len(_OFFSETS),)), pltpu.SemaphoreType.DMA((len(_OFFSETS),))]
        + (rider.scratch() if rider else []),
        compiler_params=pltpu.CompilerParams(vmem_limit_bytes=VMEM_LIMIT),
    )(packed, *(rider.ins if rider else []))


def _ada_mod(c, w16, bias):
    nc = w16.shape[2]
    kp = len(_OFFSETS)

    def body(c_ref, w_ref, b_ref, rows_ref, act_ref, cbuf, sbuf, send, recv):
        me = _coords()
        my = _linear(me)
        cbuf[my] = c_ref[...]
        copies = []
        for k, off in enumerate(_OFFSETS):
            cp = pltpu.make_async_remote_copy(src_ref=c_ref, dst_ref=cbuf.at[my], send_sem=send.at[k], recv_sem=recv.at[k],
                                              device_id=_flip(me, off), device_id_type=_MESH)
            cp.start()
            copies.append(cp)
        for cp in copies:
            cp.wait()
        act = _silu(jnp.concatenate([cbuf[j] for j in range(N_DEV)], axis=0))
        act_ref[...] = act
        act16 = act.astype(BF16)
        for l in range(DEPTH):
            ml = jnp.dot(act16, w_ref[l], preferred_element_type=F32) + b_ref[l:l + 1, :]
            for j in range(N_DEV):
                sbuf[j, l:l + 1, :] = ml[j:j + 1, :]
        rows_ref[my] = sbuf[my]
        copies = []
        for k, off in enumerate(_OFFSETS):
            peer = _flip(me, off)
            cp = pltpu.make_async_remote_copy(src_ref=sbuf.at[_linear(peer)], dst_ref=rows_ref.at[my], send_sem=send.at[kp + k],
                                              recv_sem=recv.at[kp + k], device_id=peer, device_id_type=_MESH)
            cp.start()
            copies.append(cp)
        for cp in copies:
            cp.wait()

    vm = pl.BlockSpec(memory_space=pltpu.VMEM)
    return pl.pallas_call(
        body, name="ada_mod", in_specs=[vm, vm, vm], out_specs=[vm, vm],
        out_shape=[jax.ShapeDtypeStruct((N_DEV, DEPTH, nc), F32), jax.ShapeDtypeStruct((N_DEV, D), F32)],
        scratch_shapes=[pltpu.VMEM((N_DEV, 1, D), F32), pltpu.VMEM((N_DEV, DEPTH, nc), F32),
                        pltpu.SemaphoreType.DMA((2 * kp,)), pltpu.SemaphoreType.DMA((2 * kp,))],
        compiler_params=pltpu.CompilerParams(vmem_limit_bytes=VMEM_LIMIT),
    )(c, w16, bias)


def _pack(arrs):
    parts = []
    for a in arrs:
        f = a.reshape(-1).astype(F32)
        parts.append(jnp.pad(f, (0, (-f.shape[0]) % 128)))
    flat = jnp.concatenate(parts)
    flat = jnp.pad(flat, (0, (-flat.shape[0]) % 1024))
    return flat.reshape(-1, 128)


def _unpack(packed, shapes):
    flat = packed.reshape(packed.shape[:-2] + (-1,))
    out, r = [], 0
    for s in shapes:
        n = int(np.prod(s))
        out.append(flat[..., r:r + n].reshape(packed.shape[:-2] + tuple(s)))
        r += -(-n // 128) * 128
    return out


def _pad_rows(w, rows):
    return jnp.pad(w, ((0, 0), (0, rows - w.shape[1]), (0, 0)))


_SMALL = ("b_ada", "norm_mix_g", "norm_ffn_g", "conv_a_w", "conf_dw_w", "conf_dw_b", "conf_ln_g", "conf_ln_b",
          "dn_conv_w", "dn_a_log", "dn_dt_bias", "dn_norm_g", "final_norm_g")
_BIG = ("w_in", "w_out", "w_ffn_in", "w_ffn_out")
_WEIGHTS = ("w_ada", "b_ada", "norm_mix_g", "norm_ffn_g", "w_in", "conv_a_w", "conf_dw_w", "conf_dw_b", "conf_ln_g",
            "conf_ln_b", "dn_conv_w", "dn_a_log", "dn_dt_bias", "dn_norm_g", "w_out", "w_ffn_in", "w_ffn_out",
            "final_norm_g")


def _step(x, c, loss_target, W, M, V):
    T = x.shape[1]
    me = _linear(_coords())
    xs, tgt = x[0], loss_target[0]
    vec = lambda a: a.reshape(1, -1)

    nada = W["w_ada"].shape[2]
    rows, act_all = _ada_mod(c, W["w_ada"].astype(BF16), lax.dynamic_slice(W["b_ada"], (0, me * nada), (DEPTH, nada)))
    mod = rows.transpose(1, 0, 2).reshape(DEPTH, 6, 1, D)

    w16 = {k: W[k].astype(BF16) for k in _BIG}
    w16["w_ffn_in"] = W["w_ffn_in"].transpose(0, 2, 1).astype(BF16)

    def whole(g_in=None, g_out=None, g_fi=None, g_fo=None):
        out = {}
        if g_in is not None:
            out["w_in"] = jnp.pad(g_in.transpose(1, 0, 2).reshape(D, IN_COLS), ((0, 0), (0, IN_PAD - IN_COLS)))
        if g_out is not None:
            out["w_out"] = g_out.reshape(D, D)
        if g_fi is not None:
            out["w_ffn_in"] = g_fi.reshape(2 * D_FF, D)
        if g_fo is not None:
            out["w_ffn_out"] = g_fo.reshape(D_FF, D)
        return out

    wts = [dict() for _ in range(DEPTH)]
    gather = lambda names, layer: _gather_rider([w16[k] for k in names], layer)
    wts[0].update(whole(g_in=_run_rider("gather_weights", gather(["w_in"], 0))[0]))
    conv_names = ("conv_a_w", "conf_dw_w", "dn_conv_w")
    conv_all, _ = _small_allgather("gather_conv_w", _pack([W[k] for k in conv_names]))[:2]
    conv_full = [t.transpose(1, 2, 0, 3).reshape(t.shape[1], t.shape[2], -1)
                 for t in _unpack(conv_all, [W[k].shape for k in conv_names])]
    wa, wb, wc = _pad_rows(conv_full[0], 8), _pad_rows(conv_full[1], 32), _pad_rows(conv_full[2], 8)
    lane_pad = lambda a: jnp.pad(a, ((0, 0), (0, 128 - a.shape[1])))
    alog, dtb = lane_pad(W["dn_a_log"]), lane_pad(W["dn_dt_bias"])

    saved = []
    xc = xs
    for l in range(DEPTH):
        more = l + 1 < DEPTH
        sh1, sc1, g1, sh2, sc2, g2 = [mod[l, i] for i in range(6)]
        proj, h, *got = _normproj_fwd("inproj_fwd", xc, sh1, sc1, vec(W["norm_mix_g"][l]), wts[l]["w_in"], 512,
                                      rider=gather(["w_out", "w_ffn_out"], 0) if l == 0 else None)
        if l == 0:
            wts[0].update(whole(g_out=got[0], g_fo=got[1]))
        convout = _conv_fwd(proj, wa[l], wb[l], wc[l], 256)
        y_ab, qkv, gb = _stage2_fwd(proj, convout, vec(W["conf_dw_b"][l]), vec(W["conf_ln_g"][l]), vec(W["conf_ln_b"][l]),
                                    alog[l:l + 1], dtb[l:l + 1], 256)
        o, ss, inv, *got = _delta_fwd(qkv, gb, DELTA_NB,
                                 rider=gather(["w_ffn_in"], 0) if l == 0 else gather(["w_in"], l + 1) if more else None)
        if l == 0:
            wts[0].update(whole(g_fi=got[0]))
        elif more:
            wts[l + 1].update(whole(g_in=got[0]))
        x1, mix, ycat, *got = _outproj_fwd(xc, o, proj, y_ab, g1, vec(W["dn_norm_g"][l]), wts[l]["w_out"], 512,
                                           rider=gather(["w_in"], 1) if l == 0 else None)
        if l == 0:
            wts[1].update(whole(g_in=got[0]))
        gu, h2, *got = _normproj_fwd("ffnin_fwd", x1, sh2, sc2, vec(W["norm_ffn_g"][l]), wts[l]["w_ffn_in"], 256,
                                     rider=gather(["w_ffn_in"], l + 1) if more else None, w_t=True, out_dtype=BF16)
        if more:
            wts[l + 1].update(whole(g_fi=got[0]))
        x2, f, *got = _ffn_out_fwd(x1, gu, g2, wts[l]["w_ffn_out"], 256,
                                   rider=gather(["w_ffn_out", "w_out"], l + 1) if more else None)
        if more:
            wts[l + 1].update(whole(g_fo=got[0], g_out=got[1]))
        saved.append((xc, proj, h, convout, qkv, gb, o, ss, inv, mix, ycat, x1, gu, h2, f))
        xc = x2

    dx, loss_row, d_gfin = _loss_bwd(xc, tgt, vec(W["final_norm_g"]), 512)
    loss = lax.psum(loss_row[0, 0], ("x", "y", "c"))

    big_out = {k: None for k in _BIG}
    dmod, small = [None] * DEPTH, [None] * DEPTH
    blocks = lambda g: g.reshape(N_DEV, -1, g.shape[-1])
    pair_tm = {"w_in": 512, "w_out": 128, "w_ffn_in": 704, "w_ffn_out": 352}
    sum_tm = {"w_in": 256, "w_out": 128, "w_ffn_in": 176, "w_ffn_out": 176}
    turned = lambda a: a.transpose(0, 2, 1)
    wmv = {k: (W[k], M[k], V[k]) for k in _BIG}
    wmv["w_ffn_in"] = tuple(turned(a) for a in wmv["w_ffn_in"])

    def pair_sum(k, g, from_sib):
        return _pair_add("pair_add_" + k, g, from_sib, pair_tm[k]).reshape(from_sib.shape), (g, from_sib)

    def finish(k, layer, own, r):
        if k == "w_in":
            g = _reduce_sum("reduce_" + k, own, r, sum_tm[k])[:, :IN_COLS // N_DEV]
            big_out[k] = _adam_layer("adam_" + k, g, *wmv[k], 256, layer, big_out[k])
        else:
            big_out[k] = _reduce_adam("reduce_adam_" + k, own, r, *wmv[k], sum_tm[k], layer, big_out[k])

    above = None
    for l in reversed(range(DEPTH)):
        xc, proj, h, convout, qkv, gb, o, ss, inv, mix, ycat, x1, gu, h2, f = saved[l]
        sh1, sc1, g1, sh2, sc2, g2 = [mod[l, i] for i in range(6)]
        gm, gf = vec(W["norm_mix_g"][l]), vec(W["norm_ffn_g"][l])
        bb, lg, lb = vec(W["conf_dw_b"][l]), vec(W["conf_ln_g"][l]), vec(W["conf_ln_b"][l])
        dng = vec(W["dn_norm_g"][l])
        wl = wts[l]

        dgu, s, df, d_g2, *got = _ffn_out_bwd(dx, gu, f, g2, wl["w_ffn_out"], 256,
                                              rider=_pair_rider([above[1]]) if above else None)
        if above:
            in16, in_own = pair_sum("w_in", above[1], got[0])
        gw_fo = blocks(_wgrad("wgrad_ffn_out", s, df, 1408, 1024))
        ride = _join_riders(_chip_rider([in16]), _pair_rider([gw_fo], sem0=3)) if above else _pair_rider([gw_fo])
        dx1, d_sh2, d_sc2, d_gf, *got = _normproj_bwd("ffnin_bwd", x1, dgu, dx, sc2, gf, wl["w_ffn_in"], 256, rider=ride, w_t=True)
        if above:
            finish("w_in", above[0], in_own, got[0])
        fo16, fo_own = pair_sum("w_ffn_out", gw_fo, got[-1])
        gw_fi = blocks(_wgrad("wgrad_ffn_in", dgu, h2, 1408, 1024))
        dmix, dy_ab, do, dz, d_g1, d_dng, *got = _outproj_bwd(dx1, mix, o, proj, g1, dng, wl["w_out"], 512,
                                                               rider=_pair_rider([gw_fi]))
        fi16, fi_own = pair_sum("w_ffn_in", gw_fi, got[0])
        gw_out = blocks(_wgrad("wgrad_out", ycat, dmix, 512, 2048))
        dqkv, *got = _delta_bwd(qkv, gb, ss, inv, do, DELTA_NB,
                                rider=_join_riders(_chip_rider([fi16]), _pair_rider([gw_out], sem0=3)))
        finish("w_ffn_in", l, fi_own, got[0])
        out16, out_own = pair_sum("w_out", gw_out, got[1])
        dco, da_b, dblk, d_bb, d_lg, d_lb, d_alog, d_dtb = _stage2_bwd(
            proj, convout, dy_ab, _cols(dqkv, D_DN, 0), _cols(dqkv, D_DN, 1), _cols(dqkv, D_DN, 2),
            _cols(dqkv, 128, 3 * D_DN // 128), bb, lg, lb, alog[l:l + 1], dtb[l:l + 1], 256)
        dproj, d_wa, d_wb, d_wc, *got = _conv_bwd(proj, dco, da_b, dz, dblk, wa[l], wb[l], wc[l], 256,
                                                  rider=_chip_rider([out16, fo16]))
        finish("w_out", l, out_own, got[0])
        finish("w_ffn_out", l, fo_own, got[1])
        dx, d_sh1, d_sc1, d_gm = _normproj_bwd("inproj_bwd", xc, dproj, dx1, sc1, gm, wl["w_in"], 512)
        above = (l, _wgrad_in_blocks("wgrad_in", h, dproj, 512, 1024))

        dmod[l] = jnp.concatenate([d_sh1, d_sc1, d_g1, d_sh2, d_sc2, d_g2], axis=1)
        small[l] = dict(norm_mix_g=d_gm, norm_ffn_g=d_gf, conv_a_w=d_wa[:KA], conf_dw_w=d_wb[:KB], conf_dw_b=d_bb,
                        conf_ln_g=d_lg, conf_ln_b=d_lb, dn_conv_w=d_wc[:KC], dn_a_log=d_alog, dn_dt_bias=d_dtb,
                        dn_norm_g=d_dng)

    in16, in_own = pair_sum("w_in", above[1], _run_rider("pair_exchange", _pair_rider([above[1]]))[0])

    names = ("norm_mix_g", "norm_ffn_g", "conv_a_w", "conf_dw_w", "conf_dw_b", "conf_ln_g", "conf_ln_b", "dn_conv_w",
             "dn_a_log", "dn_dt_bias", "dn_norm_g")
    pieces = [jnp.stack(dmod)] + [jnp.stack([small[l][k] for l in range(DEPTH)]) for k in names] + [d_gfin]
    shapes = [p.shape for p in pieces]
    every, total, in_recv = _small_allgather("gather_small_grads", _pack(pieces), rider=_chip_rider([in16]))
    finish("w_in", above[0], in_own, in_recv)
    tot = dict(zip(("dmod",) + names + ("final_norm_g",), _unpack(total, shapes)))
    dmod_all = _unpack(every, shapes[:1])[0]

    grads = {}
    grads["b_ada"] = tot["dmod"].reshape(DEPTH, 6 * D)
    for k in ("norm_mix_g", "norm_ffn_g", "conf_dw_b", "conf_ln_g", "conf_ln_b", "dn_norm_g"):
        grads[k] = tot[k].reshape(W[k].shape)
    grads["dn_a_log"] = tot["dn_a_log"].reshape(DEPTH, 128)[:, :HEADS]
    grads["dn_dt_bias"] = tot["dn_dt_bias"].reshape(DEPTH, 128)[:, :HEADS]
    grads["final_norm_g"] = tot["final_norm_g"].reshape(D)
    for k in conv_names:
        nloc = W[k].shape[2]
        grads[k] = lax.dynamic_slice_in_dim(tot[k], me * nloc, nloc, axis=2)

    dm = lax.dynamic_slice_in_dim(dmod_all.reshape(N_DEV, DEPTH, 6 * D), me * nada, nada, axis=2)
    pad16 = lambda a: jnp.pad(a, ((0, 16 - N_DEV), (0, 0))).astype(BF16)
    g_ada = _wgrad("wgrad_ada", pad16(act_all), pad16(dm.reshape(N_DEV, DEPTH * nada)), 256, 16)
    grads["w_ada"] = g_ada.reshape(D, DEPTH, nada).transpose(1, 0, 2)

    delta, new_m, new_v = {}, {}, {}
    r2 = lambda a: a.reshape(DEPTH * D, nada)
    d_, m_, v_ = _adam_call("adam_ada", r2(W["w_ada"]), r2(grads["w_ada"]), r2(M["w_ada"]), r2(V["w_ada"]), 512)
    delta["w_ada"], new_m["w_ada"], new_v["w_ada"] = [t.reshape(W["w_ada"].shape) for t in (d_, m_, v_)]
    sshapes = [W[k].shape for k in _SMALL]
    d_, m_, v_ = _adam_call("adam_small", _pack([W[k] for k in _SMALL]), _pack([grads[k] for k in _SMALL]),
                            _pack([M[k] for k in _SMALL]), _pack([V[k] for k in _SMALL]), 4096)
    for dst, packed in ((delta, d_), (new_m, m_), (new_v, v_)):
        dst.update(zip(_SMALL, _unpack(packed, sshapes)))
    for k in _BIG:
        grads[k], delta[k], new_m[k], new_v[k] = [turned(a) for a in big_out[k]] if k == "w_ffn_in" else big_out[k]

    return (loss, dx[None], *[grads[k] for k in _WEIGHTS], *[delta[k] for k in _WEIGHTS],
            *[new_m[k] for k in _WEIGHTS], *[new_v[k] for k in _WEIGHTS])


def kernel(x, c, w_ada, b_ada, norm_mix_g, norm_ffn_g, w_in, conv_a_w, conf_dw_w, conf_dw_b, conf_ln_g, conf_ln_b, dn_conv_w, dn_a_log, dn_dt_bias, dn_norm_g, w_out, w_ffn_in, w_ffn_out, final_norm_g, loss_target, m_w_ada, m_b_ada, m_norm_mix_g, m_norm_ffn_g, m_w_in, m_conv_a_w, m_conf_dw_w, m_conf_dw_b, m_conf_ln_g, m_conf_ln_b, m_dn_conv_w, m_dn_a_log, m_dn_dt_bias, m_dn_norm_g, m_w_out, m_w_ffn_in, m_w_ffn_out, m_final_norm_g, v_w_ada, v_b_ada, v_norm_mix_g, v_norm_ffn_g, v_w_in, v_conv_a_w, v_conf_dw_w, v_conf_dw_b, v_conf_ln_g, v_conf_ln_b, v_dn_conv_w, v_dn_a_log, v_dn_dt_bias, v_dn_norm_g, v_w_out, v_w_ffn_in, v_w_ffn_out, v_final_norm_g):
    a = dict(locals())
    W = {k: a[k] for k in _WEIGHTS}
    M = {k: a["m_" + k] for k in _WEIGHTS}
    V = {k: a["v_" + k] for k in _WEIGHTS}
    return _step(x, c, loss_target, W, M, V)
```

```python
import functools

import jax
import jax.numpy as jnp
import numpy as np
from jax import lax
from jax.experimental import pallas as pl
from jax.experimental.pallas import tpu as pltpu

F32 = jnp.float32
BF16 = jnp.bfloat16

N_DEV = 8
D = 1024
DEPTH = 4
D_CONV = 256
D_CONF = 256
D_DN = 512
HEADS = 4
HD = 128
KA, KB, KC = 3, 31, 4
CHUNK = 64
D_FF = 2816
IN_COLS = 3336
IN_PAD = 3456
N_CONVCOL = 2048
EPS = 1e-6
LN_EPS = 1e-5
HALO = 32
VMEM_LIMIT = 56 * 1024 * 1024
DELTA_NB = 8

C_AB, C_AC, C_AV, C_BA, C_BG, C_Q, C_Z, C_GB = 0, 256, 512, 768, 1024, 1280, 2816, 3328

LR, B1, B2, AEPS, WD, STEP = 0.001, 0.9, 0.999, 1e-08, 0.01, 10


def _dot(a, b, dims, hi):
    if hi:
        return lax.dot_general(a.astype(F32), b.astype(F32), (dims, ((), ())), precision=lax.Precision.HIGHEST,
                               preferred_element_type=F32)
    return lax.dot_general(a.astype(BF16), b.astype(BF16), (dims, ((), ())), preferred_element_type=F32)


@functools.partial(jax.custom_vjp, nondiff_argnums=(2,))
def mm_nn(a, b, hi=False):
    return _dot(a, b, ((1,), (0,)), hi)


@functools.partial(jax.custom_vjp, nondiff_argnums=(2,))
def mm_nt(a, b, hi=False):
    return _dot(a, b, ((1,), (1,)), hi)


@functools.partial(jax.custom_vjp, nondiff_argnums=(2,))
def mm_tn(a, b, hi=False):
    return _dot(a, b, ((0,), (0,)), hi)


mm_nn.defvjp(lambda a, b, hi: (mm_nn(a, b, hi), (a, b)),
             lambda hi, r, g: (mm_nt(g, r[1], hi), mm_tn(r[0], g, hi)))
mm_nt.defvjp(lambda a, b, hi: (mm_nt(a, b, hi), (a, b)),
             lambda hi, r, g: (mm_nn(g, r[1], hi), mm_tn(g, r[0], hi)))
mm_tn.defvjp(lambda a, b, hi: (mm_tn(a, b, hi), (a, b)),
             lambda hi, r, g: (mm_nt(r[1], g, hi), mm_nn(r[0], g, hi)))


def _sigmoid(x):
    return 1.0 / (1.0 + jnp.exp(-x))


def _silu(x):
    return x * _sigmoid(x)


def _softplus(x):
    return jnp.maximum(x, 0.0) + jnp.log(1.0 + jnp.exp(-jnp.abs(x)))


def _iota2(shape, dim):
    return lax.broadcasted_iota(jnp.int32, shape, dim)


def _dot16(a, b):
    return jnp.dot(a.astype(BF16), b.astype(BF16), preferred_element_type=F32)


def _dot_3pass(a, b):
    ah = a.astype(BF16)
    bh = b.astype(BF16)
    al = (a - ah.astype(F32)).astype(BF16)
    bl = (b - bh.astype(F32)).astype(BF16)
    d = lambda x, y: jnp.dot(x, y, preferred_element_type=F32)
    return d(ah, bh) + (d(ah, bl) + d(al, bh))


@jax.custom_vjp
def _unit_lower_inverses(Xs):
    n = Xs[0].shape[0]
    r, c = _iota2((n, n), 0), _iota2((n, n), 1)
    eye = (r == c).astype(F32)

    def joins(b):
        s = b.bit_length() - 1
        return ((r >> (s + 1)) == (c >> (s + 1))) & (((r >> s) & 1) == 1) & (((c >> s) & 1) == 0)

    Ts = [eye + jnp.where(joins(1), x, 0.0) for x in Xs]
    b = 2
    while b < n:
        m = joins(b)
        Ys = [_dot16(jnp.where(m, x, 0.0), t) for x, t in zip(Xs, Ts)]
        Ts = [t + _dot16(t, y) for t, y in zip(Ts, Ys)]
        b *= 2
    Rs = [(eye - t) + _dot_3pass(x, t) for x, t in zip(Xs, Ts)]
    return [t + _dot16(t, r_) for t, r_ in zip(Ts, Rs)]


def _unit_lower_inverses_fwd(Xs):
    Ts = _unit_lower_inverses(Xs)
    return Ts, Ts


def _unit_lower_inverses_bwd(Ts, gs):
    inner = [mm_nt(g, t) for g, t in zip(gs, Ts)]
    return ([mm_tn(t, i) for t, i in zip(Ts, inner)],)


_unit_lower_inverses.defvjp(_unit_lower_inverses_fwd, _unit_lower_inverses_bwd)


@jax.custom_vjp
def _saved_inverses(Xs, Ts):
    return list(Ts)


_saved_inverses.defvjp(lambda Xs, Ts: (list(Ts), Ts),
                       lambda Ts, gs: (_unit_lower_inverses_bwd(Ts, gs)[0], [jnp.zeros_like(t) for t in Ts]))


def _delta_chunks(qs, ks, vs, gbs, Ss, Ts=None, keep=None):
    C = CHUNK
    nb = len(gbs)
    pairs = [(c, h) for c in range(nb) for h in range(HEADS)]
    each = lambda fn, *lists: [fn(*a) for a in zip(*lists)]
    row = _iota2((C, C), 0)
    col = _iota2((C, C), 1)
    causal = row >= col
    strict = row > col
    tri = causal.astype(F32)
    eye = (row == col).astype(F32)
    lane = _iota2((C, 128), 1)
    subl = _iota2((128, C), 0)
    last = (_iota2((C, 1), 0) == C - 1).astype(F32)

    gc_all = [mm_nn(tri, gb, True) for gb in gbs]
    gc_t = [g.T for g in gc_all]
    q = [qs[c][h] * (HD ** -0.5) for c, h in pairs]
    k = [ks[c][h] for c, h in pairs]
    v = [vs[c][h] for c, h in pairs]
    gcol = [jnp.sum(jnp.where(lane == h, gc_all[c], 0.0), axis=1, keepdims=True) for c, h in pairs]
    grow = [jnp.sum(jnp.where(subl == h, gc_t[c], 0.0), axis=0, keepdims=True) for c, h in pairs]
    beta = [jnp.sum(jnp.where(lane == HEADS + h, gbs[c], 0.0), axis=1, keepdims=True) for c, h in pairs]
    decay = each(lambda a, b: jnp.where(causal, jnp.exp(jnp.where(causal, a - b, 0.0)), 0.0), gcol, grow)
    kb = each(lambda a, b: a * b, k, beta)
    vb = each(lambda a, b: a * b, v, beta)
    kk = each(lambda a, b: mm_nt(a, b), kb, k)
    X = each(lambda a, d: -jnp.where(strict, a * d, 0.0), kk, decay)
    T = _unit_lower_inverses(X) if Ts is None else _saved_inverses(X, Ts)
    if keep is not None:
        keep.extend(T)
    eg = [jnp.exp(g) for g in gcol]
    u = each(lambda t, a: mm_nn(t, a), T, vb)
    w = each(lambda t, a, e: mm_nn(t, a * e), T, kb, eg)
    qk = each(lambda a, b, d: jnp.where(causal, mm_nt(a, b) * d, 0.0), q, k, decay)
    qg = each(lambda a, e: a * e, q, eg)
    g_last = [jnp.sum(g * last, axis=0, keepdims=True) for g in gcol]
    kd = each(lambda a, gl, g: a * jnp.exp(gl - g), k, g_last, gcol)
    eg_last = [jnp.exp(g) for g in g_last]

    outs = []
    for c in range(nb):
        sl = slice(c * HEADS, (c + 1) * HEADS)
        v_new = each(lambda a, b, S: a - mm_nn(b, S), u[sl], w[sl], Ss)
        oS = each(lambda a, S: mm_nn(a, S), qg[sl], Ss)
        outs.append(each(lambda a, b, n: a + mm_nn(b, n), oS, qk[sl], v_new))
        Ss = each(lambda S, e, a, n: S * e + mm_tn(a, n), Ss, eg_last[sl], kd[sl], v_new)
    return outs, Ss


def _split_chunks(ref, nb):
    return [[ref[c * CHUNK:(c + 1) * CHUNK, h * HD:(h + 1) * HD] for h in range(HEADS)] for c in range(nb)]


def _join_chunks(vals):
    return jnp.concatenate([jnp.concatenate(heads, axis=1) for heads in vals], axis=0)


def _hosted(body, n_in, n_out, rider, n_steps):
    if rider is None:
        return body
    n_ri, n_ro = len(rider.ins), len(rider.outs)

    def wrapped(*refs):
        ins, r_ins = refs[:n_in], refs[n_in:n_in + n_ri]
        outs = refs[n_in + n_ri:n_in + n_ri + n_out]
        r_outs = refs[n_in + n_ri + n_out:n_in + n_ri + n_out + n_ro]
        rest = refs[n_in + n_ri + n_out + n_ro:]
        scr, sems = rest[:len(rest) - 3], rest[len(rest) - 3:]

        @pl.when(pl.program_id(0) == 0)
        def _():
            rider.start(r_ins, r_outs, *sems)

        body(*ins, *outs, *scr)

        @pl.when(pl.program_id(0) == n_steps - 1)
        def _():
            rider.finish(r_ins, r_outs, *sems)

    return wrapped


def _host_call(body, name, grid, in_specs, out_specs, out_shape, scratch, operands, rider):
    out_specs, out_shape = list(out_specs), list(out_shape)
    n_in, n_out = len(in_specs), len(out_specs)
    if rider is not None:
        anyspec = pl.BlockSpec(memory_space=pl.ANY)
        in_specs = list(in_specs) + [anyspec] * len(rider.ins)
        out_specs += [anyspec] * len(rider.outs)
        out_shape += list(rider.outs)
        scratch = list(scratch) + rider.scratch()
        operands = list(operands) + list(rider.ins)
    return pl.pallas_call(
        _hosted(body, n_in, n_out, rider, grid[0]), name=name, grid=grid, in_specs=in_specs, out_specs=out_specs,
        out_shape=out_shape, scratch_shapes=scratch,
        compiler_params=pltpu.CompilerParams(dimension_semantics=("arbitrary",), vmem_limit_bytes=VMEM_LIMIT),
    )(*operands)


def _delta_fwd(qkv, gb, nb, rider=None):
    T = qkv.shape[0]
    nb = min(nb, T // CHUNK)
    rows = nb * CHUNK
    n = T // rows

    def body(q_ref, k_ref, v_ref, gb_ref, o_ref, ss_ref, t_ref, s_scr):
        @pl.when(pl.program_id(0) == 0)
        def _():
            s_scr[...] = jnp.zeros_like(s_scr)

        Ss = [s_scr[h] for h in range(HEADS)]
        for h in range(HEADS):
            ss_ref[0, h] = Ss[h]
        gbs = [gb_ref[c * CHUNK:(c + 1) * CHUNK, :] for c in range(nb)]
        kept = []
        outs, new_S = _delta_chunks(_split_chunks(q_ref, nb), _split_chunks(k_ref, nb), _split_chunks(v_ref, nb), gbs, Ss,
                                    keep=kept)
        o_ref[...] = _join_chunks(outs)
        s_scr[...] = jnp.stack(new_S)
        for c in range(nb):
            for h in range(HEADS):
                t_ref[c, h] = kept[c * HEADS + h]

    row = lambda w, j=0: pl.BlockSpec((rows, w), lambda i: (i, j))
    return _host_call(
        body, "delta_fwd", (n,),
        [row(D_DN, 0), row(D_DN, 1), row(D_DN, 2), row(128)],
        [row(D_DN), pl.BlockSpec((1, HEADS, HD, HD), lambda i: (i, 0, 0, 0)),
         pl.BlockSpec((nb, HEADS, CHUNK, CHUNK), lambda i: (i, 0, 0, 0))],
        [jax.ShapeDtypeStruct((T, D_DN), F32), jax.ShapeDtypeStruct((n, HEADS, HD, HD), F32),
         jax.ShapeDtypeStruct((T // CHUNK, HEADS, CHUNK, CHUNK), F32)],
        [pltpu.VMEM((HEADS, HD, HD), F32)], [qkv, qkv, qkv, gb], rider)


def _delta_bwd(qkv, gb, ss, inv, do, nb, rider=None):
    T = qkv.shape[0]
    nb = min(nb, T // CHUNK)
    rows = nb * CHUNK
    n = T // rows

    def body(q_ref, k_ref, v_ref, gb_ref, ss_ref, t_ref, do_ref, d_ref, ds_scr):
        @pl.when(pl.program_id(0) == 0)
        def _():
            ds_scr[...] = jnp.zeros_like(ds_scr)

        Ss = [ss_ref[0, h] for h in range(HEADS)]
        gbs = [gb_ref[c * CHUNK:(c + 1) * CHUNK, :] for c in range(nb)]
        Ts = [t_ref[c, h] for c in range(nb) for h in range(HEADS)]
        _, vjp = jax.vjp(functools.partial(_delta_chunks, Ts=Ts), _split_chunks(q_ref, nb), _split_chunks(k_ref, nb),
                         _split_chunks(v_ref, nb), gbs, Ss)
        dqs, dks, dvs, dgbs, dSs = vjp((_split_chunks(do_ref, nb), [ds_scr[h] for h in range(HEADS)]))
        d_ref[...] = jnp.concatenate([_join_chunks(dqs), _join_chunks(dks), _join_chunks(dvs),
                                      jnp.concatenate(dgbs, axis=0)], axis=1)
        ds_scr[...] = jnp.stack(dSs)

    row = lambda w, j=0: pl.BlockSpec((rows, w), lambda i: (n - 1 - i, j))
    return _host_call(
        body, "delta_bwd", (n,),
        [row(D_DN, 0), row(D_DN, 1), row(D_DN, 2), row(128),
         pl.BlockSpec((1, HEADS, HD, HD), lambda i: (n - 1 - i, 0, 0, 0)),
         pl.BlockSpec((nb, HEADS, CHUNK, CHUNK), lambda i: (n - 1 - i, 0, 0, 0)), row(D_DN)],
        [row(3 * D_DN + 128)], [jax.ShapeDtypeStruct((T, 3 * D_DN + 128), F32)],
        [pltpu.VMEM((HEADS, HD, HD), F32)], [qkv, qkv, qkv, gb, ss, inv, do], rider)


def _cols(arr, width, index, first_row=0, block_row=None):
    return (arr, width, index, first_row, block_row)


def _rowwise(name, fn, tiled, consts, out_tiled, out_acc, tm, rows=None, rider=None, layer=None, prev=None):
    tiled = [t if isinstance(t, tuple) else (t, t.shape[-1], 0, 0, None) for t in tiled]
    T = tiled[0][0].shape[-2] if rows is None else rows
    tm = min(tm, T)
    assert T % tm == 0 and all(t[3] % tm == 0 for t in tiled)
    n_t, n_c, n_o, n_a = len(tiled), len(consts), len(out_tiled), len(out_acc)

    n_ri = len(rider.ins) if rider else 0
    n_ro = len(rider.outs) if rider else 0
    n_steps = T // tm

    def body(*refs):
        n_in = n_t + n_c + n_ri + (n_o if layer is not None else 0)
        r_ins = refs[n_t + n_c:n_t + n_c + n_ri]
        o_refs = refs[n_in:n_in + n_o]
        a_refs = refs[n_in + n_o:n_in + n_o + n_a]
        r_outs = refs[n_in + n_o + n_a:n_in + n_o + n_a + n_ro]
        sems = refs[n_in + n_o + n_a + n_ro:]
        if rider:
            @pl.when(pl.program_id(0) == 0)
            def _():
                rider.start(r_ins, r_outs, *sems)

        ins = [r[...] for r in refs[:n_t + n_c]]
        outs = fn(*ins)
        for r, val in zip(o_refs, outs[:n_o]):
            r[...] = val.astype(r.dtype)
        if n_a:
            @pl.when(pl.program_id(0) == 0)
            def _():
                for r in a_refs:
                    r[...] = jnp.zeros_like(r)
            for r, val in zip(a_refs, outs[n_o:]):
                r[...] += val
        if rider:
            @pl.when(pl.program_id(0) == n_steps - 1)
            def _():
                rider.finish(r_ins, r_outs, *sems)

    def const_spec(a):
        nd = a.ndim
        return pl.BlockSpec(a.shape, lambda i: (0,) * nd, pipeline_mode=pl.Buffered(1))

    def tile_spec(arr, w, j, r0, block_row):
        row = block_row if block_row is not None else (lambda i: i + r0 // tm)
        if arr.ndim == 3:
            return pl.BlockSpec((None, tm, w), lambda i: (layer, row(i), j))
        return pl.BlockSpec((tm, w), lambda i: (row(i), j))

    in_specs = [tile_spec(*t) for t in tiled]
    in_specs += [const_spec(a) for a in consts]
    if layer is None:
        out_specs = [pl.BlockSpec((tm, w), lambda i: (i, 0)) for (w, _) in out_tiled]
        out_shape = [jax.ShapeDtypeStruct((T, w), dt) for (w, dt) in out_tiled]
    else:
        out_specs = [pl.BlockSpec((None, tm, w), lambda i: (layer, i, 0)) for (w, _) in out_tiled]
        out_shape = [jax.ShapeDtypeStruct((DEPTH, T, w), dt) for (w, dt) in out_tiled]
    out_specs += [pl.BlockSpec(s, lambda i: (0, 0)) for (s, _) in out_acc]
    out_shape += [jax.ShapeDtypeStruct(s, dt) for (s, dt) in out_acc]
    operands = [t[0] for t in tiled] + list(consts)
    scratch = []
    aliases = {}
    if rider:
        anyspec = pl.BlockSpec(memory_space=pl.ANY)
        in_specs += [anyspec] * n_ri
        out_specs += [anyspec] * n_ro
        out_shape += list(rider.outs)
        operands += list(rider.ins)
        scratch = rider.scratch()
    n_prev = 0
    if layer is not None:
        assert rider is None and not out_acc
        if prev is None:
            prev = [jnp.zeros(o.shape, o.dtype) for o in out_shape]
        n_prev = len(prev)
        aliases = {len(operands) + i: i for i in range(n_prev)}
        in_specs += [pl.BlockSpec(memory_space=pl.ANY)] * n_prev
        operands += list(prev)
    return pl.pallas_call(
        body, name=name, grid=(n_steps,), in_specs=in_specs, out_specs=out_specs, out_shape=out_shape, scratch_shapes=scratch,
        input_output_aliases=aliases,
        compiler_params=pltpu.CompilerParams(dimension_semantics=("arbitrary",), vmem_limit_bytes=VMEM_LIMIT),
    )(*operands)


def _colsum(x):
    return jnp.sum(x, axis=0, keepdims=True)


def _rms(x):
    r = lax.rsqrt(jnp.mean(x * x, axis=-1, keepdims=True) + EPS)
    return x * r, r


def _rms_bwd(dxn, xn, r):
    return r * (dxn - xn * jnp.mean(dxn * xn, axis=-1, keepdims=True))


def _normproj_fwd(name, x, sh, sc, g, w, tm, rider=None, w_t=False, out_dtype=F32):
    def fn(x, sh, sc, g, w):
        xn, _ = _rms(x)
        h = (xn * (g * (1.0 + sc)) + sh).astype(BF16)
        return lax.dot_general(h, w, (((1,), (1 if w_t else 0,)), ((), ())), preferred_element_type=F32), h

    return _rowwise(name, fn, [x], [sh, sc, g, w], [(w.shape[0 if w_t else 1], out_dtype), (D, BF16)], [], tm, rider=rider)


def _normproj_bwd(name, x, dpre, dres, sc, g, w, tm, rider=None, w_t=False):
    def fn(x, dpre, dres, sc, g, w):
        xn, r = _rms(x)
        dh = lax.dot_general(dpre, w, (((1,), (0 if w_t else 1,)), ((), ())), preferred_element_type=F32)
        da = _colsum(dh * xn)
        dx = _rms_bwd(dh * (g * (1.0 + sc)), xn, r) + dres
        return dx, _colsum(dh), da * g, da * (1.0 + sc)

    vec = ((1, D), F32)
    return _rowwise(name, fn, [x, dpre, dres], [sc, g, w], [(D, F32)], [vec, vec, vec], tm, rider=rider)


def _stage2(a_b, blk, cp, u1c, qp, kp, vp, bb, lg, lb, alog, dtb):
    y_a = a_b * cp
    u1 = u1c + bb
    mu = jnp.mean(u1, axis=-1, keepdims=True)
    uc = u1 - mu
    var = jnp.mean(uc * uc, axis=-1, keepdims=True)
    y_b = _silu(uc * lax.rsqrt(var + LN_EPS) * lg + lb)

    def l2(t):
        t = _silu(t)
        return t * lax.rsqrt(jnp.sum(t * t, axis=-1, keepdims=True) + EPS)

    q = [l2(t) for t in qp]
    k = [l2(t) for t in kp]
    v = _silu(vp)
    lane = _iota2(blk.shape, 1)
    gdec = -jnp.exp(alog) * _softplus(blk + dtb)
    gb = jnp.where(lane < HEADS, gdec, jnp.where(lane < 2 * HEADS, _sigmoid(blk), 0.0))
    return y_a, y_b, q, k, v, gb


def _heads_of(x, base=0):
    return [x[:, base + h * HD:base + (h + 1) * HD] for h in range(HEADS)]


def _stage2_fwd(proj, convout, bb, lg, lb, alog, dtb, tm):
    def fn(a_b, blk, co, bb, lg, lb, alog, dtb):
        y_a, y_b, q, k, v, gb = _stage2(a_b, blk, co[:, 0:256], co[:, 256:512], _heads_of(co, 512), _heads_of(co, 1024),
                                        co[:, 1536:2048], bb, lg, lb, alog, dtb)
        return jnp.concatenate([y_a, y_b], axis=1), jnp.concatenate(q + k + [v], axis=1), gb

    return _rowwise("stage2_fwd", fn, [_cols(proj, 256, 0), _cols(proj, 128, C_GB // 128), convout],
                    [bb, lg, lb, alog, dtb], [(512, BF16), (1536, F32), (128, F32)], [], tm)


def _stage2_bwd(proj, convout, dy_ab, dq, dk, dv, dgb, bb, lg, lb, alog, dtb, tm, rider=None):
    def fn(a_b, blk, co, dy_ab, dq, dk, dv, dgb, bb, lg, lb, alog, dtb):
        args = (a_b, blk, co[:, 0:256], co[:, 256:512], _heads_of(co, 512), _heads_of(co, 1024), co[:, 1536:2048],
                bb, lg, lb, alog, dtb)
        _, vjp = jax.vjp(_stage2, *args)
        ct = (dy_ab[:, 0:256], dy_ab[:, 256:512], _heads_of(dq), _heads_of(dk), dv, dgb)
        da_b, dblk, dcp, du1c, dqp, dkp, dvp, dbb, dlg, dlb, dalog, ddtb = vjp(ct)
        dco = jnp.concatenate([dcp, du1c] + dqp + dkp + [dvp], axis=1)
        return dco, da_b, dblk, dbb, dlg, dlb, dalog, ddtb

    v256, v128 = ((1, 256), F32), ((1, 128), F32)
    return _rowwise("stage2_bwd", fn,
                    [_cols(proj, 256, 0), _cols(proj, 128, C_GB // 128), convout, dy_ab, dq, dk, dv, dgb],
                    [bb, lg, lb, alog, dtb], [(N_CONVCOL, F32), (256, F32), (128, F32)],
                    [v256, v256, v256, v128, v128], tm, rider=rider)


def _stage3(o, z, dng):
    ys = []
    for oh, zh in zip(o, z):
        on = oh * lax.rsqrt(jnp.mean(oh * oh, axis=-1, keepdims=True) + EPS)
        ys.append(on * dng * _silu(zh))
    return ys


def _outproj_fwd(x, o, proj, y_ab, g1, dng, wout, tm, rider=None):
    def fn(x, o, z0, z1, z2, z3, y_ab, g1, dng, wout):
        y_c = _stage3(_heads_of(o), [z0, z1, z2, z3], dng)
        ycat = jnp.concatenate([y_ab] + [t.astype(BF16) for t in y_c], axis=1)
        mix = jnp.dot(ycat, wout, preferred_element_type=F32)
        return x + g1 * mix, mix, ycat

    return _rowwise("outproj_fwd", fn, [x, o] + _z_heads(proj) + [y_ab],
                    [g1, dng, wout], [(D, F32), (D, F32), (D, BF16)], [], tm, rider=rider)


def _z_heads(proj):
    return [_cols(proj, HD, C_Z // HD + h) for h in range(HEADS)]


def _outproj_bwd(dx1, mix, o, proj, g1, dng, wout, tm, rider=None):
    def fn(dx1, mix, o, z0, z1, z2, z3, g1, dng, wout):
        dmix = (dx1 * g1).astype(BF16)
        dycat = lax.dot_general(dmix, wout, (((1,), (1,)), ((), ())), preferred_element_type=F32)
        _, vjp = jax.vjp(_stage3, _heads_of(o), [z0, z1, z2, z3], dng)
        do, dz, ddng = vjp(_heads_of(dycat, 512))
        return (dmix, dycat[:, 0:512], jnp.concatenate(do, axis=1), jnp.concatenate(dz, axis=1),
                _colsum(dx1 * mix), ddng)

    return _rowwise("outproj_bwd", fn, [dx1, mix, o] + _z_heads(proj), [g1, dng, wout],
                    [(D, BF16), (512, F32), (512, F32), (512, F32)], [((1, D), F32), ((1, HD), F32)], tm, rider=rider)


_CONV_BLOCKS = ((0, 256, KA), (256, 512, KB), (512, 2048, KC))
_CONV_STRIP = 256


_CONV_ROWS = 32


def _conv_inputs(proj_ref, rows):
    a_c, a_v = proj_ref[rows, C_AC:C_AC + 256], proj_ref[rows, C_AV:C_AV + 256]
    b_a, b_g = proj_ref[rows, C_BA:C_BA + 256], proj_ref[rows, C_BG:C_BG + 256]
    return a_c, a_v, b_a, _sigmoid(b_g)


def _shifted_copies(ext, phases, tm):
    n = tm + HALO - 8
    for b in range(1, 8):
        phases[b - 1] = ext[pl.ds(b, n), 256:512]


def _rows_at(ext, phases, row, col, kw):
    a, b = divmod(row, 8)
    if kw == KB and b:
        return phases[b - 1, pl.ds(8 * a, _CONV_ROWS), :]
    return ext[pl.ds(row, _CONV_ROWS), col:col + _CONV_STRIP]


def _conv_fwd(proj, wa, wb, wc, tm):
    T = proj.shape[0]
    tm = min(tm, T)

    def body(proj_ref, wa_ref, wb_ref, wc_ref, out_ref, ext, phases):
        @pl.when(pl.program_id(0) == 0)
        def _():
            ext[0:HALO, :] = jnp.zeros((HALO, N_CONVCOL), F32)

        a_c, a_v, b_a, sg = _conv_inputs(proj_ref, slice(None))
        ext[HALO:HALO + tm, 0:256] = a_c * a_v
        ext[HALO:HALO + tm, 256:512] = b_a * sg
        ext[HALO:HALO + tm, 512:2048] = proj_ref[:, C_Q:C_Q + 1536]
        _shifted_copies(ext, phases, tm)
        for r0 in range(0, tm, _CONV_ROWS):
            for (c0, c1, kw), w_ref in zip(_CONV_BLOCKS, (wa_ref, wb_ref, wc_ref)):
                for s0 in range(c0, c1, _CONV_STRIP):
                    acc = jnp.zeros((_CONV_ROWS, _CONV_STRIP), F32)
                    for k in range(kw):
                        acc += (w_ref[k:k + 1, s0 - c0:s0 - c0 + _CONV_STRIP]
                                * _rows_at(ext, phases, r0 + HALO - (kw - 1) + k, s0, kw))
                    out_ref[r0:r0 + _CONV_ROWS, s0:s0 + _CONV_STRIP] = acc
        ext[0:HALO, :] = ext[tm:tm + HALO, :]

    full = lambda a: pl.BlockSpec(a.shape, lambda i: (0, 0))
    return pl.pallas_call(
        body, name="conv_fwd", grid=(T // tm,),
        in_specs=[pl.BlockSpec((tm, IN_PAD), lambda i: (i, 0)), full(wa), full(wb), full(wc)],
        out_specs=pl.BlockSpec((tm, N_CONVCOL), lambda i: (i, 0)),
        out_shape=jax.ShapeDtypeStruct((T, N_CONVCOL), F32),
        scratch_shapes=[pltpu.VMEM((HALO + tm, N_CONVCOL), F32), pltpu.VMEM((7, tm + HALO - 8, 256), F32)],
        compiler_params=pltpu.CompilerParams(dimension_semantics=("arbitrary",), vmem_limit_bytes=VMEM_LIMIT),
    )(proj, wa, wb, wc)


def _conv_bwd(proj, dco, da_b, dz, dblk, wa, wb, wc, tm, rider=None):
    T = proj.shape[0]
    tm = min(tm, T)
    n = T // tm

    def body(proj_ref, dco_ref, dab_ref, dz_ref, dblk_ref, wa_ref, wb_ref, wc_ref,
             dproj_ref, dwa_ref, dwb_ref, dwc_ref, ext, acc_a, acc_b, acc_c, phases):
        @pl.when(pl.program_id(0) == 0)
        def _():
            ext[tm:tm + HALO, :] = jnp.zeros((HALO, N_CONVCOL), F32)
            acc_a[...] = jnp.zeros_like(acc_a)
            acc_b[...] = jnp.zeros_like(acc_b)
            acc_c[...] = jnp.zeros_like(acc_c)

        ext[0:tm, :] = dco_ref[...]
        _shifted_copies(ext, phases, tm)

        def taps(w_ref, acc_ref, kw, c0, wc0, xin, r0):
            dx = jnp.zeros((_CONV_ROWS, _CONV_STRIP), F32)
            for k in range(kw):
                sh = _rows_at(ext, phases, r0 + kw - 1 - k, c0, kw)
                dx += w_ref[k:k + 1, wc0:wc0 + _CONV_STRIP] * sh
                pr = sh * xin
                part = pr[0:8]
                for g in range(8, _CONV_ROWS, 8):
                    part += pr[g:g + 8]
                acc_ref[8 * k:8 * k + 8, wc0:wc0 + _CONV_STRIP] += part
            return dx

        for r0 in range(0, tm, _CONV_ROWS):
            rows = slice(r0, r0 + _CONV_ROWS)
            a_c, a_v, b_a, sg = _conv_inputs(proj_ref, rows)
            dp = taps(wa_ref, acc_a, KA, 0, 0, a_c * a_v, r0)
            dproj_ref[rows, C_AC:C_AC + 256] = (dp * a_v).astype(BF16)
            dproj_ref[rows, C_AV:C_AV + 256] = (dp * a_c).astype(BF16)
            du0 = taps(wb_ref, acc_b, KB, 256, 0, b_a * sg, r0)
            dproj_ref[rows, C_BA:C_BA + 256] = (du0 * sg).astype(BF16)
            dproj_ref[rows, C_BG:C_BG + 256] = (du0 * b_a * sg * (1.0 - sg)).astype(BF16)
            for s0 in range(0, 1536, _CONV_STRIP):
                dq = taps(wc_ref, acc_c, KC, 512 + s0, s0, proj_ref[rows, C_Q + s0:C_Q + s0 + _CONV_STRIP], r0)
                dproj_ref[rows, C_Q + s0:C_Q + s0 + _CONV_STRIP] = dq.astype(BF16)
        ext[tm:tm + HALO, :] = ext[0:HALO, :]

        dproj_ref[:, C_AB:C_AB + 256] = dab_ref[...].astype(BF16)
        dproj_ref[:, C_Z:C_Z + 512] = dz_ref[...].astype(BF16)
        dproj_ref[:, C_GB:C_GB + 128] = dblk_ref[...].astype(BF16)

        @pl.when(pl.program_id(0) == n - 1)
        def _():
            for acc_ref, dw_ref, kw in ((acc_a, dwa_ref, KA), (acc_b, dwb_ref, KB), (acc_c, dwc_ref, KC)):
                dw_ref[...] = jnp.zeros_like(dw_ref)
                for k in range(kw):
                    dw_ref[k:k + 1, :] = _colsum(acc_ref[8 * k:8 * k + 8, :])

    rev = lambda w: pl.BlockSpec((tm, w), lambda i: (n - 1 - i, 0))
    full = lambda a: pl.BlockSpec(a.shape, lambda i: (0, 0))
    return _host_call(
        body, "conv_bwd", (n,),
        [rev(IN_PAD), rev(N_CONVCOL), rev(256), rev(512), rev(128), full(wa), full(wb), full(wc)],
        [rev(IN_PAD), full(wa), full(wb), full(wc)],
        [jax.ShapeDtypeStruct((T, IN_PAD), BF16), jax.ShapeDtypeStruct(wa.shape, F32),
         jax.ShapeDtypeStruct(wb.shape, F32), jax.ShapeDtypeStruct(wc.shape, F32)],
        [pltpu.VMEM((tm + HALO, N_CONVCOL), F32), pltpu.VMEM((8 * KA, 256), F32),
         pltpu.VMEM((8 * KB, 256), F32), pltpu.VMEM((8 * KC, 1536), F32), pltpu.VMEM((7, tm + HALO - 8, 256), F32)],
        [proj, dco, da_b, dz, dblk, wa, wb, wc], rider)


def _ffn_out_fwd(x1, gu, g2, wfo, tm, rider=None):
    def fn(x1, gu, g2, wfo):
        s = (_silu(gu[:, :D_FF].astype(F32)) * gu[:, D_FF:].astype(F32)).astype(BF16)
        f = jnp.dot(s, wfo, preferred_element_type=F32)
        return x1 + g2 * f, f

    return _rowwise("ffnout_fwd", fn, [x1, gu], [g2, wfo], [(D, F32), (D, F32)], [], tm, rider=rider)


def _ffn_out_bwd(dx2, gu, f, g2, wfo, tm, rider=None):
    def fn(dx2, gu, f, g2, wfo):
        gate, up = gu[:, :D_FF].astype(F32), gu[:, D_FF:].astype(F32)
        sg = _sigmoid(gate)
        sl = gate * sg
        df = (dx2 * g2).astype(BF16)
        ds = lax.dot_general(df, wfo, (((1,), (1,)), ((), ())), preferred_element_type=F32)
        dgate = ds * up * (sg * (1.0 + gate * (1.0 - sg)))
        dgu = jnp.concatenate([dgate.astype(BF16), (ds * sl).astype(BF16)], axis=1)
        return dgu, sl * up, df, _colsum(dx2 * f)

    return _rowwise("ffnout_bwd", fn, [dx2, gu, f], [g2, wfo], [(2 * D_FF, BF16), (D_FF, BF16), (D, BF16)],
                    [((1, D), F32)], tm, rider=rider)


def _loss_bwd(x, tgt, gfin, tm):
    def fn(x, tgt, gfin):
        xn, r = _rms(x)
        e = xn * gfin - tgt
        loss = 0.5 * jnp.sum(jnp.mean(e * e, axis=-1, keepdims=True), axis=0, keepdims=True)
        dy = e * (1.0 / D)
        return _rms_bwd(dy * gfin, xn, r), jnp.broadcast_to(loss, (1, 128)), _colsum(dy * xn)

    return _rowwise("loss_bwd", fn, [x, tgt], [gfin], [(D, F32)], [((1, 128), F32), ((1, D), F32)], tm)


def _wgrad(name, a, b, bm, bk):
    T, M = a.shape
    N = b.shape[1]
    bk = min(bk, T)

    def body(a_ref, b_ref, o_ref):
        @pl.when(pl.program_id(1) == 0)
        def _():
            o_ref[...] = jnp.zeros_like(o_ref)

        o_ref[...] += lax.dot_general(a_ref[...], b_ref[...], (((0,), (0,)), ((), ())), preferred_element_type=F32)

    return pl.pallas_call(
        body, name=name, grid=(M // bm, T // bk),
        in_specs=[pl.BlockSpec((bk, bm), lambda i, k: (k, i)), pl.BlockSpec((bk, N), lambda i, k: (k, 0))],
        out_specs=pl.BlockSpec((bm, N), lambda i, k: (i, 0)),
        out_shape=jax.ShapeDtypeStruct((M, N), F32),
        compiler_params=pltpu.CompilerParams(dimension_semantics=("arbitrary", "arbitrary"), vmem_limit_bytes=VMEM_LIMIT),
    )(a, b)


IN_BLOCK = 512


def _wgrad_in_blocks(name, a, b, bm, bk):
    T, M = a.shape
    N = b.shape[1]
    bk = min(bk, T)
    nk = T // bk
    per = IN_COLS // N_DEV
    win = IN_BLOCK + 128

    def body(a_ref, b_ref, o_ref, acc):
        @pl.when(pl.program_id(1) == 0)
        def _():
            acc[...] = jnp.zeros_like(acc)

        acc[...] += lax.dot_general(a_ref[...], b_ref[...], (((0,), (0,)), ((), ())), preferred_element_type=F32)

        @pl.when(pl.program_id(1) == nk - 1)
        def _():
            for j in range(N_DEV):
                q, r = divmod(per * j, 128)
                w = acc[:, 128 * q:128 * q + win]
                if r:
                    w = pltpu.roll(w, win - r, axis=1)
                o_ref[j] = w[:, :IN_BLOCK]

    assert 128 * ((per * (N_DEV - 1)) // 128) + win <= N
    return pl.pallas_call(
        body, name=name, grid=(M // bm, nk),
        in_specs=[pl.BlockSpec((bk, bm), lambda i, k: (k, i)), pl.BlockSpec((bk, N), lambda i, k: (k, 0))],
        out_specs=pl.BlockSpec((N_DEV, bm, IN_BLOCK), lambda i, k: (0, i, 0)),
        out_shape=jax.ShapeDtypeStruct((N_DEV, M, IN_BLOCK), F32),
        scratch_shapes=[pltpu.VMEM((bm, N), F32)],
        compiler_params=pltpu.CompilerParams(dimension_semantics=("arbitrary", "arbitrary"), vmem_limit_bytes=VMEM_LIMIT),
    )(a, b)


def _adamw(w, g, m, v):
    m = B1 * m + (1.0 - B1) * g
    v = B2 * v + (1.0 - B2) * (g * g)
    m_hat = m / (1.0 - B1 ** STEP)
    v_hat = v / (1.0 - B2 ** STEP)
    return -LR * (m_hat / (jnp.sqrt(v_hat) + AEPS) + WD * w), m, v


def _adam_call(name, w, g, m, v, tm):
    C = w.shape[1]
    return _rowwise(name, _adamw, [w, g, m, v], [], [(C, F32)] * 3, [], tm)


def _adam_layer(name, g, w, m, v, tm, layer, prev):
    C = g.shape[1]
    return _rowwise(name, lambda g, w, m, v: (g,) + _adamw(w, g, m, v), [g, w, m, v], [], [(C, F32)] * 4, [], tm,
                    layer=layer, prev=prev)


def _reduce_sum(name, own, recv, tm):
    n, R, C = recv.shape
    flat = recv.reshape(n * R, C)
    fn = lambda a, b, r0, r1, r2: ((((a + b) + r0.astype(F32)) + r1.astype(F32)) + r2.astype(F32),)
    return _rowwise(name, fn, _own_blocks(own, R, tm) + [_cols(flat, C, 0, j * R) for j in range(n)], [], [(C, F32)], [],
                    tm, rows=R)[0]


def _pair_add(name, g, from_sib, tm):
    n, R, C = from_sib.shape
    n_r = R // tm
    mine = _cols(g.reshape(N_DEV * R, C), C, 0,
                 block_row=lambda i: (2 * (i // n_r) + lax.axis_index("c")) * n_r + i % n_r)

    return _rowwise(name, lambda a, b: (a + b,), [mine, from_sib.reshape(n * R, C)], [], [(C, BF16)], [], tm, rows=n * R)[0]


def _own_blocks(own, R, tm):
    g, from_sib = own
    C = g.shape[-1]
    n_r = R // tm
    chip = lambda: 2 * lax.axis_index("x") + lax.axis_index("y")
    return [_cols(g.reshape(N_DEV * R, C), C, 0, block_row=lambda i: (2 * chip() + lax.axis_index("c")) * n_r + i),
            _cols(from_sib.reshape(4 * R, C), C, 0, block_row=lambda i: chip() * n_r + i)]


def _reduce_adam(name, own, recv, w, m, v, tm, layer, prev):
    n, R, C = recv.shape
    flat = recv.reshape(n * R, C)

    def fn(a, b, r0, r1, r2, w, m, v):
        g = (((a + b) + r0.astype(F32)) + r1.astype(F32)) + r2.astype(F32)
        return (g,) + _adamw(w, g, m, v)

    return _rowwise(name, fn, _own_blocks(own, R, tm) + [_cols(flat, C, 0, j * R) for j in range(n)] + [w, m, v], [],
                    [(C, F32)] * 4, [], tm, rows=R, layer=layer, prev=prev)


_OFFSETS = [(dx, dy, dc) for dx in (0, 1) for dy in (0, 1) for dc in (0, 1)][1:]
_MESH = pl.DeviceIdType.MESH


def _coords():
    return lax.axis_index("x"), lax.axis_index("y"), lax.axis_index("c")


def _flip(me, off):
    return tuple((1 - m) if d else m for m, d in zip(me, off))


def _linear(p):
    return 4 * p[0] + 2 * p[1] + p[2]


_CHIP_FLIPS = ((1, 0), (0, 1), (1, 1))


class _Rider:
    def __init__(self, ins, outs, n_remote, n_local, start, finish):
        self.ins, self.outs, self.n_remote, self.n_local, self.start, self.finish = ins, outs, n_remote, n_local, start, finish

    def scratch(self):
        return [pltpu.SemaphoreType.DMA((self.n_remote,)), pltpu.SemaphoreType.DMA((self.n_remote,)),
                pltpu.SemaphoreType.DMA((max(self.n_local, 1),))]


def _run_rider(name, rider):
    def body(*refs):
        n_i, n_o = len(rider.ins), len(rider.outs)
        rider.start(refs[:n_i], refs[n_i:n_i + n_o], *refs[n_i + n_o:])
        rider.finish(refs[:n_i], refs[n_i:n_i + n_o], *refs[n_i + n_o:])

    anyspec = pl.BlockSpec(memory_space=pl.ANY)
    return pl.pallas_call(
        body, name=name, in_specs=[anyspec] * len(rider.ins), out_specs=[anyspec] * len(rider.outs),
        out_shape=list(rider.outs), scratch_shapes=rider.scratch(),
    )(*rider.ins)


def _gather_rider(arrs, layer):
    n = len(arrs)

    def parts(ins, outs, send, recv, loc):
        x, y, c = _coords()
        me, sib = (x, y, c), (x, y, 1 - c)
        chips = [((1 - x) if dx else x, (1 - y) if dy else y) for dx, dy in _CHIP_FLIPS]

        def copy(a, k, block, to, own=False):
            slot = outs[a].at[_linear(block)]
            return pltpu.make_async_remote_copy(src_ref=ins[a].at[layer] if own else slot, dst_ref=slot,
                                                send_sem=send.at[a * 7 + k], recv_sem=recv.at[a * 7 + k],
                                                device_id=to, device_id_type=_MESH)

        local = [pltpu.make_async_copy(ins[a].at[layer], outs[a].at[_linear(me)], loc.at[a]) for a in range(n)]
        first = []
        for a in range(n):
            first.append(copy(a, 0, me, sib, own=True))
            first += [copy(a, 1 + j, me, (*chip, c), own=True) for j, chip in enumerate(chips)]
        return copy, local, first, me, sib, chips, c

    def start(*refs):
        _, local, first, *_ = parts(*refs)
        for cp in local + first:
            cp.start()

    def finish(*refs):
        copy, local, first, me, sib, chips, c = parts(*refs)
        passed = []
        for j, chip in enumerate(chips):
            for a in range(n):
                copy(a, 1 + j, (*chip, c), me).wait_recv()
                cp = copy(a, 4 + j, (*chip, c), sib)
                cp.start()
                passed.append(cp)
        for a in range(n):
            copy(a, 0, sib, me).wait_recv()
            for j, chip in enumerate(chips):
                copy(a, 4 + j, (*chip, 1 - c), me).wait_recv()
        for cp in first + passed:
            cp.wait_send()
        for cp in local:
            cp.wait()

    outs = [jax.ShapeDtypeStruct((N_DEV,) + a.shape[1:], a.dtype) for a in arrs]
    return _Rider(list(arrs), outs, 7 * n, n, start, finish)


def _simple_rider(ins, outs, n_remote, make):
    def start(*refs):
        for cp in make(*refs):
            cp.start()

    def finish(*refs):
        for cp in make(*refs):
            cp.wait()

    return _Rider(ins, outs, n_remote, 0, start, finish)


def _join_riders(a, b):
    assert b.sem0 == a.n_remote and a.n_local == 0 and b.n_local == 0
    n_i, n_o = len(a.ins), len(a.outs)

    def both(fa, fb):
        def run(ins, outs, send, recv, loc):
            fa(ins[:n_i], outs[:n_o], send, recv, loc)
            fb(ins[n_i:], outs[n_o:], send, recv, loc)
        return run

    return _Rider(a.ins + b.ins, a.outs + b.outs, a.n_remote + b.n_remote, 0, both(a.start, b.start), both(a.finish, b.finish))


def _pair_rider(arrs, sem0=0):
    def make(ins, outs, send, recv, loc):
        x, y, c = _coords()
        return [pltpu.make_async_remote_copy(src_ref=ins[a].at[2 * s_ + 1 - c], dst_ref=outs[a].at[s_],
                                             send_sem=send.at[sem0 + 4 * a + s_], recv_sem=recv.at[sem0 + 4 * a + s_],
                                             device_id=(x, y, 1 - c), device_id_type=_MESH)
                for a in range(len(arrs)) for s_ in range(4)]

    r = _simple_rider(list(arrs), [jax.ShapeDtypeStruct((4,) + a.shape[1:], a.dtype) for a in arrs], 4 * len(arrs), make)
    r.sem0 = sem0
    return r


def _chip_rider(arrs, sem0=0):
    nf = len(_CHIP_FLIPS)

    def make(ins, outs, send, recv, loc):
        x, y, c = _coords()
        copies = []
        for a in range(len(arrs)):
            for k, (dx, dy) in enumerate(_CHIP_FLIPS):
                px, py = (1 - x) if dx else x, (1 - y) if dy else y
                copies.append(pltpu.make_async_remote_copy(
                    src_ref=ins[a].at[2 * px + py], dst_ref=outs[a].at[k], send_sem=send.at[sem0 + a * nf + k],
                    recv_sem=recv.at[sem0 + a * nf + k], device_id=(px, py, c), device_id_type=_MESH))
        return copies

    r = _simple_rider(list(arrs), [jax.ShapeDtypeStruct((nf,) + a.shape[1:], a.dtype) for a in arrs], nf * len(arrs), make)
    r.sem0 = sem0
    return r


def _small_allgather(name, packed, rider=None):
    R = packed.shape[0]
    n_ri = len(rider.ins) if rider else 0
    n_ro = len(rider.outs) if rider else 0

    def body(*refs):
        in_ref, r_ins = refs[0], refs[1:1 + n_ri]
        all_ref, sum_ref = refs[1 + n_ri:3 + n_ri]
        r_outs = refs[3 + n_ri:3 + n_ri + n_ro]
        send, recv = refs[3 + n_ri + n_ro:5 + n_ri + n_ro]
        r_sems = refs[5 + n_ri + n_ro:]
        if rider:
            rider.start(r_ins, r_outs, *r_sems)
        me = _coords()
        my = _linear(me)
        all_ref[my] = in_ref[...]
        copies = []
        for k, off in enumerate(_OFFSETS):
            cp = pltpu.make_async_remote_copy(src_ref=in_ref, dst_ref=all_ref.at[my], send_sem=send.at[k], recv_sem=recv.at[k],
                                              device_id=_flip(me, off), device_id_type=_MESH)
            cp.start()
            copies.append(cp)
        for cp in copies:
            cp.wait()
        acc = all_ref[0]
        for j in range(1, N_DEV):
            acc = acc + all_ref[j]
        sum_ref[...] = acc
        if rider:
            rider.finish(r_ins, r_outs, *r_sems)

    vm = pl.BlockSpec(memory_space=pltpu.VMEM)
    anyspec = pl.BlockSpec(memory_space=pl.ANY)
    return pl.pallas_call(
        body, name=name, in_specs=[vm] + [anyspec] * n_ri, out_specs=[vm, vm] + [anyspec] * n_ro,
        out_shape=[jax.ShapeDtypeStruct((N_DEV, R, 128), F32), jax.ShapeDtypeStruct((R, 128), F32)] + (list(rider.outs) if rider else []),
        scratch_shapes=[pltpu.SemaphoreType.DMA((len(_OFFSETS),)), pltpu.SemaphoreType.DMA((len(_OFFSETS),))]
        + (rider.scratch() if rider else []),
        compiler_params=pltpu.CompilerParams(vmem_limit_bytes=VMEM_LIMIT),
    )(packed, *(rider.ins if rider else []))


def _ada_mod(c, w16, bias):
    nc = w16.shape[2]
    kp = len(_OFFSETS)

    def body(c_ref, w_ref, b_ref, rows_ref, act_ref, cbuf, sbuf, send, recv):
        me = _coords()
        my = _linear(me)
        cbuf[my] = c_ref[...]
        copies = []
        for k, off in enumerate(_OFFSETS):
            cp = pltpu.make_async_remote_copy(src_ref=c_ref, dst_ref=cbuf.at[my], send_sem=send.at[k], recv_sem=recv.at[k],
                                              device_id=_flip(me, off), device_id_type=_MESH)
            cp.start()
            copies.append(cp)
        for cp in copies:
            cp.wait()
        act = _silu(jnp.concatenate([cbuf[j] for j in range(N_DEV)], axis=0))
        act_ref[...] = act
        act16 = act.astype(BF16)
        for l in range(DEPTH):
            ml = jnp.dot(act16, w_ref[l], preferred_element_type=F32) + b_ref[l:l + 1, :]
            for j in range(N_DEV):
                sbuf[j, l:l + 1, :] = ml[j:j + 1, :]
        rows_ref[my] = sbuf[my]
        copies = []
        for k, off in enumerate(_OFFSETS):
            peer = _flip(me, off)
            cp = pltpu.make_async_remote_copy(src_ref=sbuf.at[_linear(peer)], dst_ref=rows_ref.at[my], send_sem=send.at[kp + k],
                                              recv_sem=recv.at[kp + k], device_id=peer, device_id_type=_MESH)
            cp.start()
            copies.append(cp)
        for cp in copies:
            cp.wait()

    vm = pl.BlockSpec(memory_space=pltpu.VMEM)
    return pl.pallas_call(
        body, name="ada_mod", in_specs=[vm, vm, vm], out_specs=[vm, vm],
        out_shape=[jax.ShapeDtypeStruct((N_DEV, DEPTH, nc), F32), jax.ShapeDtypeStruct((N_DEV, D), F32)],
        scratch_shapes=[pltpu.VMEM((N_DEV, 1, D), F32), pltpu.VMEM((N_DEV, DEPTH, nc), F32),
                        pltpu.SemaphoreType.DMA((2 * kp,)), pltpu.SemaphoreType.DMA((2 * kp,))],
        compiler_params=pltpu.CompilerParams(vmem_limit_bytes=VMEM_LIMIT),
    )(c, w16, bias)


def _pack(arrs):
    parts = []
    for a in arrs:
        f = a.reshape(-1).astype(F32)
        parts.append(jnp.pad(f, (0, (-f.shape[0]) % 128)))
    flat = jnp.concatenate(parts)
    flat = jnp.pad(flat, (0, (-flat.shape[0]) % 1024))
    return flat.reshape(-1, 128)


def _unpack(packed, shapes):
    flat = packed.reshape(packed.shape[:-2] + (-1,))
    out, r = [], 0
    for s in shapes:
        n = int(np.prod(s))
        out.append(flat[..., r:r + n].reshape(packed.shape[:-2] + tuple(s)))
        r += -(-n // 128) * 128
    return out


def _pad_rows(w, rows):
    return jnp.pad(w, ((0, 0), (0, rows - w.shape[1]), (0, 0)))


_SMALL = ("b_ada", "norm_mix_g", "norm_ffn_g", "conv_a_w", "conf_dw_w", "conf_dw_b", "conf_ln_g", "conf_ln_b",
          "dn_conv_w", "dn_a_log", "dn_dt_bias", "dn_norm_g", "final_norm_g")
_BIG = ("w_in", "w_out", "w_ffn_in", "w_ffn_out")
_WEIGHTS = ("w_ada", "b_ada", "norm_mix_g", "norm_ffn_g", "w_in", "conv_a_w", "conf_dw_w", "conf_dw_b", "conf_ln_g",
            "conf_ln_b", "dn_conv_w", "dn_a_log", "dn_dt_bias", "dn_norm_g", "w_out", "w_ffn_in", "w_ffn_out",
            "final_norm_g")


def _step(x, c, loss_target, W, M, V):
    T = x.shape[1]
    me = _linear(_coords())
    xs, tgt = x[0], loss_target[0]
    vec = lambda a: a.reshape(1, -1)

    nada = W["w_ada"].shape[2]
    rows, act_all = _ada_mod(c, W["w_ada"].astype(BF16), lax.dynamic_slice(W["b_ada"], (0, me * nada), (DEPTH, nada)))
    mod = rows.transpose(1, 0, 2).reshape(DEPTH, 6, 1, D)

    w16 = {k: W[k].astype(BF16) for k in _BIG}
    w16["w_ffn_in"] = W["w_ffn_in"].transpose(0, 2, 1).astype(BF16)

    def whole(g_in=None, g_out=None, g_fi=None, g_fo=None):
        out = {}
        if g_in is not None:
            out["w_in"] = jnp.pad(g_in.transpose(1, 0, 2).reshape(D, IN_COLS), ((0, 0), (0, IN_PAD - IN_COLS)))
        if g_out is not None:
            out["w_out"] = g_out.reshape(D, D)
        if g_fi is not None:
            out["w_ffn_in"] = g_fi.reshape(2 * D_FF, D)
        if g_fo is not None:
            out["w_ffn_out"] = g_fo.reshape(D_FF, D)
        return out

    wts = [dict() for _ in range(DEPTH)]
    gather = lambda names, layer: _gather_rider([w16[k] for k in names], layer)
    wts[0].update(whole(g_in=_run_rider("gather_weights", gather(["w_in"], 0))[0]))
    conv_names = ("conv_a_w", "conf_dw_w", "dn_conv_w")
    conv_all, _ = _small_allgather("gather_conv_w", _pack([W[k] for k in conv_names]))[:2]
    conv_full = [t.transpose(1, 2, 0, 3).reshape(t.shape[1], t.shape[2], -1)
                 for t in _unpack(conv_all, [W[k].shape for k in conv_names])]
    wa, wb, wc = _pad_rows(conv_full[0], 8), _pad_rows(conv_full[1], 32), _pad_rows(conv_full[2], 8)
    lane_pad = lambda a: jnp.pad(a, ((0, 0), (0, 128 - a.shape[1])))
    alog, dtb = lane_pad(W["dn_a_log"]), lane_pad(W["dn_dt_bias"])

    saved = []
    xc = xs
    for l in range(DEPTH):
        more = l + 1 < DEPTH
        sh1, sc1, g1, sh2, sc2, g2 = [mod[l, i] for i in range(6)]
        proj, h, *got = _normproj_fwd("inproj_fwd", xc, sh1, sc1, vec(W["norm_mix_g"][l]), wts[l]["w_in"], 512,
                                      rider=gather(["w_out", "w_ffn_out"], 0) if l == 0 else None)
        if l == 0:
            wts[0].update(whole(g_out=got[0], g_fo=got[1]))
        convout = _conv_fwd(proj, wa[l], wb[l], wc[l], 512)
        y_ab, qkv, gb = _stage2_fwd(proj, convout, vec(W["conf_dw_b"][l]), vec(W["conf_ln_g"][l]), vec(W["conf_ln_b"][l]),
                                    alog[l:l + 1], dtb[l:l + 1], 512)
        o, ss, inv, *got = _delta_fwd(qkv, gb, DELTA_NB,
                                 rider=gather(["w_ffn_in"], 0) if l == 0 else gather(["w_in"], l + 1) if more else None)
        if l == 0:
            wts[0].update(whole(g_fi=got[0]))
        elif more:
            wts[l + 1].update(whole(g_in=got[0]))
        x1, mix, ycat, *got = _outproj_fwd(xc, o, proj, y_ab, g1, vec(W["dn_norm_g"][l]), wts[l]["w_out"], 512,
                                           rider=gather(["w_in"], 1) if l == 0 else None)
        if l == 0:
            wts[1].update(whole(g_in=got[0]))
        gu, h2, *got = _normproj_fwd("ffnin_fwd", x1, sh2, sc2, vec(W["norm_ffn_g"][l]), wts[l]["w_ffn_in"], 512,
                                     rider=gather(["w_ffn_in"], l + 1) if more else None, w_t=True, out_dtype=BF16)
        if more:
            wts[l + 1].update(whole(g_fi=got[0]))
        x2, f, *got = _ffn_out_fwd(x1, gu, g2, wts[l]["w_ffn_out"], 256,
                                   rider=gather(["w_ffn_out", "w_out"], l + 1) if more else None)
        if more:
            wts[l + 1].update(whole(g_fo=got[0], g_out=got[1]))
        saved.append((xc, proj, h, convout, qkv, gb, o, ss, inv, mix, ycat, x1, gu, h2, f))
        xc = x2

    dx, loss_row, d_gfin = _loss_bwd(xc, tgt, vec(W["final_norm_g"]), 512)
    loss = lax.psum(loss_row[0, 0], ("x", "y", "c"))

    big_out = {k: None for k in _BIG}
    dmod, small = [None] * DEPTH, [None] * DEPTH
    blocks = lambda g: g.reshape(N_DEV, -1, g.shape[-1])
    pair_tm = {"w_in": 512, "w_out": 128, "w_ffn_in": 704, "w_ffn_out": 352}
    sum_tm = {"w_in": 256, "w_out": 128, "w_ffn_in": 176, "w_ffn_out": 176}
    turned = lambda a: a.transpose(0, 2, 1)
    wmv = {k: (W[k], M[k], V[k]) for k in _BIG}
    wmv["w_ffn_in"] = tuple(turned(a) for a in wmv["w_ffn_in"])

    def pair_sum(k, g, from_sib):
        return _pair_add("pair_add_" + k, g, from_sib, pair_tm[k]).reshape(from_sib.shape), (g, from_sib)

    def finish(k, layer, own, r):
        if k == "w_in":
            g = _reduce_sum("reduce_" + k, own, r, sum_tm[k])[:, :IN_COLS // N_DEV]
            big_out[k] = _adam_layer("adam_" + k, g, *wmv[k], 256, layer, big_out[k])
        else:
            big_out[k] = _reduce_adam("reduce_adam_" + k, own, r, *wmv[k], sum_tm[k], layer, big_out[k])

    above = None
    for l in reversed(range(DEPTH)):
        xc, proj, h, convout, qkv, gb, o, ss, inv, mix, ycat, x1, gu, h2, f = saved[l]
        sh1, sc1, g1, sh2, sc2, g2 = [mod[l, i] for i in range(6)]
        gm, gf = vec(W["norm_mix_g"][l]), vec(W["norm_ffn_g"][l])
        bb, lg, lb = vec(W["conf_dw_b"][l]), vec(W["conf_ln_g"][l]), vec(W["conf_ln_b"][l])
        dng = vec(W["dn_norm_g"][l])
        wl = wts[l]

        dgu, s, df, d_g2, *got = _ffn_out_bwd(dx, gu, f, g2, wl["w_ffn_out"], 256,
                                              rider=_pair_rider([above[1]]) if above else None)
        if above:
            in16, in_own = pair_sum("w_in", above[1], got[0])
        gw_fo = blocks(_wgrad("wgrad_ffn_out", s, df, 1408, 1024))
        ride = _join_riders(_chip_rider([in16]), _pair_rider([gw_fo], sem0=3)) if above else _pair_rider([gw_fo])
        dx1, d_sh2, d_sc2, d_gf, *got = _normproj_bwd("ffnin_bwd", x1, dgu, dx, sc2, gf, wl["w_ffn_in"], 512, rider=ride, w_t=True)
        if above:
            finish("w_in", above[0], in_own, got[0])
        fo16, fo_own = pair_sum("w_ffn_out", gw_fo, got[-1])
        gw_fi = blocks(_wgrad("wgrad_ffn_in", dgu, h2, 1408, 1024))
        dmix, dy_ab, do, dz, d_g1, d_dng, *got = _outproj_bwd(dx1, mix, o, proj, g1, dng, wl["w_out"], 512,
                                                               rider=_pair_rider([gw_fi]))
        fi16, fi_own = pair_sum("w_ffn_in", gw_fi, got[0])
        gw_out = blocks(_wgrad("wgrad_out", ycat, dmix, 512, 2048))
        dqkv, *got = _delta_bwd(qkv, gb, ss, inv, do, DELTA_NB,
                                rider=_join_riders(_chip_rider([fi16]), _pair_rider([gw_out], sem0=3)))
        finish("w_ffn_in", l, fi_own, got[0])
        out16, out_own = pair_sum("w_out", gw_out, got[1])
        dco, da_b, dblk, d_bb, d_lg, d_lb, d_alog, d_dtb = _stage2_bwd(
            proj, convout, dy_ab, _cols(dqkv, D_DN, 0), _cols(dqkv, D_DN, 1), _cols(dqkv, D_DN, 2),
            _cols(dqkv, 128, 3 * D_DN // 128), bb, lg, lb, alog[l:l + 1], dtb[l:l + 1], 512)
        dproj, d_wa, d_wb, d_wc, *got = _conv_bwd(proj, dco, da_b, dz, dblk, wa[l], wb[l], wc[l], 512,
                                                  rider=_chip_rider([out16, fo16]))
        finish("w_out", l, out_own, got[0])
        finish("w_ffn_out", l, fo_own, got[1])
        dx, d_sh1, d_sc1, d_gm = _normproj_bwd("inproj_bwd", xc, dproj, dx1, sc1, gm, wl["w_in"], 512)
        above = (l, _wgrad_in_blocks("wgrad_in", h, dproj, 512, 1024))

        dmod[l] = jnp.concatenate([d_sh1, d_sc1, d_g1, d_sh2, d_sc2, d_g2], axis=1)
        small[l] = dict(norm_mix_g=d_gm, norm_ffn_g=d_gf, conv_a_w=d_wa[:KA], conf_dw_w=d_wb[:KB], conf_dw_b=d_bb,
                        conf_ln_g=d_lg, conf_ln_b=d_lb, dn_conv_w=d_wc[:KC], dn_a_log=d_alog, dn_dt_bias=d_dtb,
                        dn_norm_g=d_dng)

    in16, in_own = pair_sum("w_in", above[1], _run_rider("pair_exchange", _pair_rider([above[1]]))[0])

    names = ("norm_mix_g", "norm_ffn_g", "conv_a_w", "conf_dw_w", "conf_dw_b", "conf_ln_g", "conf_ln_b", "dn_conv_w",
             "dn_a_log", "dn_dt_bias", "dn_norm_g")
    pieces = [jnp.stack(dmod)] + [jnp.stack([small[l][k] for l in range(DEPTH)]) for k in names] + [d_gfin]
    shapes = [p.shape for p in pieces]
    every, total, in_recv = _small_allgather("gather_small_grads", _pack(pieces), rider=_chip_rider([in16]))
    finish("w_in", above[0], in_own, in_recv)
    tot = dict(zip(("dmod",) + names + ("final_norm_g",), _unpack(total, shapes)))
    dmod_all = _unpack(every, shapes[:1])[0]

    grads = {}
    grads["b_ada"] = tot["dmod"].reshape(DEPTH, 6 * D)
    for k in ("norm_mix_g", "norm_ffn_g", "conf_dw_b", "conf_ln_g", "conf_ln_b", "dn_norm_g"):
        grads[k] = tot[k].reshape(W[k].shape)
    grads["dn_a_log"] = tot["dn_a_log"].reshape(DEPTH, 128)[:, :HEADS]
    grads["dn_dt_bias"] = tot["dn_dt_bias"].reshape(DEPTH, 128)[:, :HEADS]
    grads["final_norm_g"] = tot["final_norm_g"].reshape(D)
    for k in conv_names:
        nloc = W[k].shape[2]
        grads[k] = lax.dynamic_slice_in_dim(tot[k], me * nloc, nloc, axis=2)

    dm = lax.dynamic_slice_in_dim(dmod_all.reshape(N_DEV, DEPTH, 6 * D), me * nada, nada, axis=2)
    pad16 = lambda a: jnp.pad(a, ((0, 16 - N_DEV), (0, 0))).astype(BF16)
    g_ada = _wgrad("wgrad_ada", pad16(act_all), pad16(dm.reshape(N_DEV, DEPTH * nada)), 256, 16)
    grads["w_ada"] = g_ada.reshape(D, DEPTH, nada).transpose(1, 0, 2)

    delta, new_m, new_v = {}, {}, {}
    r2 = lambda a: a.reshape(DEPTH * D, nada)
    d_, m_, v_ = _adam_call("adam_ada", r2(W["w_ada"]), r2(grads["w_ada"]), r2(M["w_ada"]), r2(V["w_ada"]), 512)
    delta["w_ada"], new_m["w_ada"], new_v["w_ada"] = [t.reshape(W["w_ada"].shape) for t in (d_, m_, v_)]
    sshapes = [W[k].shape for k in _SMALL]
    d_, m_, v_ = _adam_call("adam_small", _pack([W[k] for k in _SMALL]), _pack([grads[k] for k in _SMALL]),
                            _pack([M[k] for k in _SMALL]), _pack([V[k] for k in _SMALL]), 4096)
    for dst, packed in ((delta, d_), (new_m, m_), (new_v, v_)):
        dst.update(zip(_SMALL, _unpack(packed, sshapes)))
    for k in _BIG:
        grads[k], delta[k], new_m[k], new_v[k] = [turned(a) for a in big_out[k]] if k == "w_ffn_in" else big_out[k]

    return (loss, dx[None], *[grads[k] for k in _WEIGHTS], *[delta[k] for k in _WEIGHTS],
            *[new_m[k] for k in _WEIGHTS], *[new_v[k] for k in _WEIGHTS])


def kernel(x, c, w_ada, b_ada, norm_mix_g, norm_ffn_g, w_in, conv_a_w, conf_dw_w, conf_dw_b, conf_ln_g, conf_ln_b, dn_conv_w, dn_a_log, dn_dt_bias, dn_norm_g, w_out, w_ffn_in, w_ffn_out, final_norm_g, loss_target, m_w_ada, m_b_ada, m_norm_mix_g, m_norm_ffn_g, m_w_in, m_conv_a_w, m_conf_dw_w, m_conf_dw_b, m_conf_ln_g, m_conf_ln_b, m_dn_conv_w, m_dn_a_log, m_dn_dt_bias, m_dn_norm_g, m_w_out, m_w_ffn_in, m_w_ffn_out, m_final_norm_g, v_w_ada, v_b_ada, v_norm_mix_g, v_norm_ffn_g, v_w_in, v_conv_a_w, v_conf_dw_w, v_conf_dw_b, v_conf_ln_g, v_conf_ln_b, v_dn_conv_w, v_dn_a_log, v_dn_dt_bias, v_dn_norm_g, v_w_out, v_w_ffn_in, v_w_ffn_out, v_final_norm_g):
    a = dict(locals())
    W = {k: a[k] for k in _WEIGHTS}
    M = {k: a["m_" + k] for k in _WEIGHTS}
    V = {k: a["v_" + k] for k in _WEIGHTS}
    return _step(x, c, loss_target, W, M, V)
```

```python
import functools

import jax
import jax.numpy as jnp
import numpy as np
from jax import lax
from jax.experimental import pallas as pl
from jax.experimental.pallas import tpu as pltpu

F32 = jnp.float32
BF16 = jnp.bfloat16

N_DEV = 8
D = 1024
DEPTH = 4
D_CONV = 256
D_CONF = 256
D_DN = 512
HEADS = 4
HD = 128
KA, KB, KC = 3, 31, 4
CHUNK = 64
D_FF = 2816
IN_COLS = 3336
IN_PAD = 3456
N_CONVCOL = 2048
EPS = 1e-6
LN_EPS = 1e-5
HALO = 32
VMEM_LIMIT = 56 * 1024 * 1024
DELTA_NB = 8

C_AB, C_AC, C_AV, C_BA, C_BG, C_Q, C_Z, C_GB = 0, 256, 512, 768, 1024, 1280, 2816, 3328

LR, B1, B2, AEPS, WD, STEP = 0.001, 0.9, 0.999, 1e-08, 0.01, 10


def _dot(a, b, dims, hi):
    if hi:
        return lax.dot_general(a.astype(F32), b.astype(F32), (dims, ((), ())), precision=lax.Precision.HIGHEST,
                               preferred_element_type=F32)
    return lax.dot_general(a.astype(BF16), b.astype(BF16), (dims, ((), ())), preferred_element_type=F32)


@functools.partial(jax.custom_vjp, nondiff_argnums=(2,))
def mm_nn(a, b, hi=False):
    return _dot(a, b, ((1,), (0,)), hi)


@functools.partial(jax.custom_vjp, nondiff_argnums=(2,))
def mm_nt(a, b, hi=False):
    return _dot(a, b, ((1,), (1,)), hi)


@functools.partial(jax.custom_vjp, nondiff_argnums=(2,))
def mm_tn(a, b, hi=False):
    return _dot(a, b, ((0,), (0,)), hi)


mm_nn.defvjp(lambda a, b, hi: (mm_nn(a, b, hi), (a, b)),
             lambda hi, r, g: (mm_nt(g, r[1], hi), mm_tn(r[0], g, hi)))
mm_nt.defvjp(lambda a, b, hi: (mm_nt(a, b, hi), (a, b)),
             lambda hi, r, g: (mm_nn(g, r[1], hi), mm_tn(g, r[0], hi)))
mm_tn.defvjp(lambda a, b, hi: (mm_tn(a, b, hi), (a, b)),
             lambda hi, r, g: (mm_nt(r[1], g, hi), mm_nn(r[0], g, hi)))


def _sigmoid(x):
    return 1.0 / (1.0 + jnp.exp(-x))


def _silu(x):
    return x * _sigmoid(x)


def _softplus(x):
    return jnp.maximum(x, 0.0) + jnp.log(1.0 + jnp.exp(-jnp.abs(x)))


def _iota2(shape, dim):
    return lax.broadcasted_iota(jnp.int32, shape, dim)


def _dot16(a, b):
    return jnp.dot(a.astype(BF16), b.astype(BF16), preferred_element_type=F32)


def _dot_3pass(a, b):
    ah = a.astype(BF16)
    bh = b.astype(BF16)
    al = (a - ah.astype(F32)).astype(BF16)
    bl = (b - bh.astype(F32)).astype(BF16)
    d = lambda x, y: jnp.dot(x, y, preferred_element_type=F32)
    return d(ah, bh) + (d(ah, bl) + d(al, bh))


@jax.custom_vjp
def _unit_lower_inverses(Xs):
    n = Xs[0].shape[0]
    r, c = _iota2((n, n), 0), _iota2((n, n), 1)
    eye = (r == c).astype(F32)

    def joins(b):
        s = b.bit_length() - 1
        return ((r >> (s + 1)) == (c >> (s + 1))) & (((r >> s) & 1) == 1) & (((c >> s) & 1) == 0)

    Ts = [eye + jnp.where(joins(1), x, 0.0) for x in Xs]
    b = 2
    while b < n:
        m = joins(b)
        Ys = [_dot16(jnp.where(m, x, 0.0), t) for x, t in zip(Xs, Ts)]
        Ts = [t + _dot16(t, y) for t, y in zip(Ts, Ys)]
        b *= 2
    Rs = [(eye - t) + _dot_3pass(x, t) for x, t in zip(Xs, Ts)]
    return [t + _dot16(t, r_) for t, r_ in zip(Ts, Rs)]


def _unit_lower_inverses_fwd(Xs):
    Ts = _unit_lower_inverses(Xs)
    return Ts, Ts


def _unit_lower_inverses_bwd(Ts, gs):
    inner = [mm_nt(g, t) for g, t in zip(gs, Ts)]
    return ([mm_tn(t, i) for t, i in zip(Ts, inner)],)


_unit_lower_inverses.defvjp(_unit_lower_inverses_fwd, _unit_lower_inverses_bwd)


@jax.custom_vjp
def _saved_inverses(Xs, Ts):
    return list(Ts)


_saved_inverses.defvjp(lambda Xs, Ts: (list(Ts), Ts),
                       lambda Ts, gs: (_unit_lower_inverses_bwd(Ts, gs)[0], [jnp.zeros_like(t) for t in Ts]))


def _delta_chunks(qs, ks, vs, gbs, Ss, Ts=None, keep=None):
    C = CHUNK
    nb = len(gbs)
    pairs = [(c, h) for c in range(nb) for h in range(HEADS)]
    each = lambda fn, *lists: [fn(*a) for a in zip(*lists)]
    row = _iota2((C, C), 0)
    col = _iota2((C, C), 1)
    causal = row >= col
    strict = row > col
    tri = causal.astype(F32)
    eye = (row == col).astype(F32)
    lane = _iota2((C, 128), 1)
    subl = _iota2((128, C), 0)
    last = (_iota2((C, 1), 0) == C - 1).astype(F32)

    gc_all = [mm_nn(tri, gb, True) for gb in gbs]
    gc_t = [g.T for g in gc_all]
    q = [qs[c][h] * (HD ** -0.5) for c, h in pairs]
    k = [ks[c][h] for c, h in pairs]
    v = [vs[c][h] for c, h in pairs]
    gcol = [jnp.sum(jnp.where(lane == h, gc_all[c], 0.0), axis=1, keepdims=True) for c, h in pairs]
    grow = [jnp.sum(jnp.where(subl == h, gc_t[c], 0.0), axis=0, keepdims=True) for c, h in pairs]
    beta = [jnp.sum(jnp.where(lane == HEADS + h, gbs[c], 0.0), axis=1, keepdims=True) for c, h in pairs]
    decay = each(lambda a, b: jnp.where(causal, jnp.exp(jnp.where(causal, a - b, 0.0)), 0.0), gcol, grow)
    kb = each(lambda a, b: a * b, k, beta)
    vb = each(lambda a, b: a * b, v, beta)
    kk = each(lambda a, b: mm_nt(a, b), kb, k)
    X = each(lambda a, d: -jnp.where(strict, a * d, 0.0), kk, decay)
    T = _unit_lower_inverses(X) if Ts is None else _saved_inverses(X, Ts)
    if keep is not None:
        keep.extend(T)
    eg = [jnp.exp(g) for g in gcol]
    u = each(lambda t, a: mm_nn(t, a), T, vb)
    w = each(lambda t, a, e: mm_nn(t, a * e), T, kb, eg)
    qk = each(lambda a, b, d: jnp.where(causal, mm_nt(a, b) * d, 0.0), q, k, decay)
    qg = each(lambda a, e: a * e, q, eg)
    g_last = [jnp.sum(g * last, axis=0, keepdims=True) for g in gcol]
    kd = each(lambda a, gl, g: a * jnp.exp(gl - g), k, g_last, gcol)
    eg_last = [jnp.exp(g) for g in g_last]

    outs = []
    for c in range(nb):
        sl = slice(c * HEADS, (c + 1) * HEADS)
        v_new = each(lambda a, b, S: a - mm_nn(b, S), u[sl], w[sl], Ss)
        oS = each(lambda a, S: mm_nn(a, S), qg[sl], Ss)
        outs.append(each(lambda a, b, n: a + mm_nn(b, n), oS, qk[sl], v_new))
        Ss = each(lambda S, e, a, n: S * e + mm_tn(a, n), Ss, eg_last[sl], kd[sl], v_new)
    return outs, Ss


def _split_chunks(ref, nb):
    return [[ref[c * CHUNK:(c + 1) * CHUNK, h * HD:(h + 1) * HD] for h in range(HEADS)] for c in range(nb)]


def _join_chunks(vals):
    return jnp.concatenate([jnp.concatenate(heads, axis=1) for heads in vals], axis=0)


def _hosted(body, n_in, n_out, rider, n_steps):
    if rider is None:
        return body
    n_ri, n_ro = len(rider.ins), len(rider.outs)

    def wrapped(*refs):
        ins, r_ins = refs[:n_in], refs[n_in:n_in + n_ri]
        outs = refs[n_in + n_ri:n_in + n_ri + n_out]
        r_outs = refs[n_in + n_ri + n_out:n_in + n_ri + n_out + n_ro]
        rest = refs[n_in + n_ri + n_out + n_ro:]
        scr, sems = rest[:len(rest) - 3], rest[len(rest) - 3:]

        @pl.when(pl.program_id(0) == 0)
        def _():
            rider.start(r_ins, r_outs, *sems)

        body(*ins, *outs, *scr)

        @pl.when(pl.program_id(0) == n_steps - 1)
        def _():
            rider.finish(r_ins, r_outs, *sems)

    return wrapped


def _host_call(body, name, grid, in_specs, out_specs, out_shape, scratch, operands, rider):
    out_specs, out_shape = list(out_specs), list(out_shape)
    n_in, n_out = len(in_specs), len(out_specs)
    if rider is not None:
        anyspec = pl.BlockSpec(memory_space=pl.ANY)
        in_specs = list(in_specs) + [anyspec] * len(rider.ins)
        out_specs += [anyspec] * len(rider.outs)
        out_shape += list(rider.outs)
        scratch = list(scratch) + rider.scratch()
        operands = list(operands) + list(rider.ins)
    return pl.pallas_call(
        _hosted(body, n_in, n_out, rider, grid[0]), name=name, grid=grid, in_specs=in_specs, out_specs=out_specs,
        out_shape=out_shape, scratch_shapes=scratch,
        compiler_params=pltpu.CompilerParams(dimension_semantics=("arbitrary",), vmem_limit_bytes=VMEM_LIMIT),
    )(*operands)


def _delta_fwd(qkv, gb, nb, rider=None):
    T = qkv.shape[0]
    nb = min(nb, T // CHUNK)
    rows = nb * CHUNK
    n = T // rows

    def body(q_ref, k_ref, v_ref, gb_ref, o_ref, ss_ref, t_ref, s_scr):
        @pl.when(pl.program_id(0) == 0)
        def _():
            s_scr[...] = jnp.zeros_like(s_scr)

        Ss = [s_scr[h] for h in range(HEADS)]
        for h in range(HEADS):
            ss_ref[0, h] = Ss[h]
        gbs = [gb_ref[c * CHUNK:(c + 1) * CHUNK, :] for c in range(nb)]
        kept = []
        outs, new_S = _delta_chunks(_split_chunks(q_ref, nb), _split_chunks(k_ref, nb), _split_chunks(v_ref, nb), gbs, Ss,
                                    keep=kept)
        o_ref[...] = _join_chunks(outs)
        s_scr[...] = jnp.stack(new_S)
        for c in range(nb):
            for h in range(HEADS):
                t_ref[c, h] = kept[c * HEADS + h]

    row = lambda w, j=0: pl.BlockSpec((rows, w), lambda i: (i, j))
    return _host_call(
        body, "delta_fwd", (n,),
        [row(D_DN, 0), row(D_DN, 1), row(D_DN, 2), row(128)],
        [row(D_DN), pl.BlockSpec((1, HEADS, HD, HD), lambda i: (i, 0, 0, 0)),
         pl.BlockSpec((nb, HEADS, CHUNK, CHUNK), lambda i: (i, 0, 0, 0))],
        [jax.ShapeDtypeStruct((T, D_DN), F32), jax.ShapeDtypeStruct((n, HEADS, HD, HD), F32),
         jax.ShapeDtypeStruct((T // CHUNK, HEADS, CHUNK, CHUNK), F32)],
        [pltpu.VMEM((HEADS, HD, HD), F32)], [qkv, qkv, qkv, gb], rider)


def _delta_bwd(qkv, gb, ss, inv, do, nb, rider=None):
    T = qkv.shape[0]
    nb = min(nb, T // CHUNK)
    rows = nb * CHUNK
    n = T // rows

    def body(q_ref, k_ref, v_ref, gb_ref, ss_ref, t_ref, do_ref, d_ref, ds_scr):
        @pl.when(pl.program_id(0) == 0)
        def _():
            ds_scr[...] = jnp.zeros_like(ds_scr)

        Ss = [ss_ref[0, h] for h in range(HEADS)]
        gbs = [gb_ref[c * CHUNK:(c + 1) * CHUNK, :] for c in range(nb)]
        Ts = [t_ref[c, h] for c in range(nb) for h in range(HEADS)]
        _, vjp = jax.vjp(functools.partial(_delta_chunks, Ts=Ts), _split_chunks(q_ref, nb), _split_chunks(k_ref, nb),
                         _split_chunks(v_ref, nb), gbs, Ss)
        dqs, dks, dvs, dgbs, dSs = vjp((_split_chunks(do_ref, nb), [ds_scr[h] for h in range(HEADS)]))
        d_ref[...] = jnp.concatenate([_join_chunks(dqs), _join_chunks(dks), _join_chunks(dvs),
                                      jnp.concatenate(dgbs, axis=0)], axis=1)
        ds_scr[...] = jnp.stack(dSs)

    row = lambda w, j=0: pl.BlockSpec((rows, w), lambda i: (n - 1 - i, j))
    return _host_call(
        body, "delta_bwd", (n,),
        [row(D_DN, 0), row(D_DN, 1), row(D_DN, 2), row(128),
         pl.BlockSpec((1, HEADS, HD, HD), lambda i: (n - 1 - i, 0, 0, 0)),
         pl.BlockSpec((nb, HEADS, CHUNK, CHUNK), lambda i: (n - 1 - i, 0, 0, 0)), row(D_DN)],
        [row(3 * D_DN + 128)], [jax.ShapeDtypeStruct((T, 3 * D_DN + 128), F32)],
        [pltpu.VMEM((HEADS, HD, HD), F32)], [qkv, qkv, qkv, gb, ss, inv, do], rider)


def _cols(arr, width, index, first_row=0, block_row=None):
    return (arr, width, index, first_row, block_row)


def _rowwise(name, fn, tiled, consts, out_tiled, out_acc, tm, rows=None, rider=None, layer=None, prev=None):
    tiled = [t if isinstance(t, tuple) else (t, t.shape[-1], 0, 0, None) for t in tiled]
    T = tiled[0][0].shape[-2] if rows is None else rows
    tm = min(tm, T)
    assert T % tm == 0 and all(t[3] % tm == 0 for t in tiled)
    n_t, n_c, n_o, n_a = len(tiled), len(consts), len(out_tiled), len(out_acc)

    n_ri = len(rider.ins) if rider else 0
    n_ro = len(rider.outs) if rider else 0
    n_steps = T // tm

    def body(*refs):
        n_in = n_t + n_c + n_ri + (n_o if layer is not None else 0)
        r_ins = refs[n_t + n_c:n_t + n_c + n_ri]
        o_refs = refs[n_in:n_in + n_o]
        a_refs = refs[n_in + n_o:n_in + n_o + n_a]
        r_outs = refs[n_in + n_o + n_a:n_in + n_o + n_a + n_ro]
        sems = refs[n_in + n_o + n_a + n_ro:]
        if rider:
            @pl.when(pl.program_id(0) == 0)
            def _():
                rider.start(r_ins, r_outs, *sems)

        ins = [r[...] for r in refs[:n_t + n_c]]
        outs = fn(*ins)
        for r, val in zip(o_refs, outs[:n_o]):
            r[...] = val.astype(r.dtype)
        if n_a:
            @pl.when(pl.program_id(0) == 0)
            def _():
                for r in a_refs:
                    r[...] = jnp.zeros_like(r)
            for r, val in zip(a_refs, outs[n_o:]):
                r[...] += val
        if rider:
            @pl.when(pl.program_id(0) == n_steps - 1)
            def _():
                rider.finish(r_ins, r_outs, *sems)

    def const_spec(a):
        nd = a.ndim
        return pl.BlockSpec(a.shape, lambda i: (0,) * nd, pipeline_mode=pl.Buffered(1))

    def tile_spec(arr, w, j, r0, block_row):
        row = block_row if block_row is not None else (lambda i: i + r0 // tm)
        if arr.ndim == 3:
            return pl.BlockSpec((None, tm, w), lambda i: (layer, row(i), j))
        return pl.BlockSpec((tm, w), lambda i: (row(i), j))

    in_specs = [tile_spec(*t) for t in tiled]
    in_specs += [const_spec(a) for a in consts]
    if layer is None:
        out_specs = [pl.BlockSpec((tm, w), lambda i: (i, 0)) for (w, _) in out_tiled]
        out_shape = [jax.ShapeDtypeStruct((T, w), dt) for (w, dt) in out_tiled]
    else:
        out_specs = [pl.BlockSpec((None, tm, w), lambda i: (layer, i, 0)) for (w, _) in out_tiled]
        out_shape = [jax.ShapeDtypeStruct((DEPTH, T, w), dt) for (w, dt) in out_tiled]
    out_specs += [pl.BlockSpec(s, lambda i: (0, 0)) for (s, _) in out_acc]
    out_shape += [jax.ShapeDtypeStruct(s, dt) for (s, dt) in out_acc]
    operands = [t[0] for t in tiled] + list(consts)
    scratch = []
    aliases = {}
    if rider:
        anyspec = pl.BlockSpec(memory_space=pl.ANY)
        in_specs += [anyspec] * n_ri
        out_specs += [anyspec] * n_ro
        out_shape += list(rider.outs)
        operands += list(rider.ins)
        scratch = rider.scratch()
    n_prev = 0
    if layer is not None:
        assert rider is None and not out_acc
        if prev is None:
            prev = [jnp.zeros(o.shape, o.dtype) for o in out_shape]
        n_prev = len(prev)
        aliases = {len(operands) + i: i for i in range(n_prev)}
        in_specs += [pl.BlockSpec(memory_space=pl.ANY)] * n_prev
        operands += list(prev)
    return pl.pallas_call(
        body, name=name, grid=(n_steps,), in_specs=in_specs, out_specs=out_specs, out_shape=out_shape, scratch_shapes=scratch,
        input_output_aliases=aliases,
        compiler_params=pltpu.CompilerParams(dimension_semantics=("arbitrary",), vmem_limit_bytes=VMEM_LIMIT),
    )(*operands)


def _colsum(x):
    return jnp.sum(x, axis=0, keepdims=True)


def _rms(x):
    r = lax.rsqrt(jnp.mean(x * x, axis=-1, keepdims=True) + EPS)
    return x * r, r


def _rms_bwd(dxn, xn, r):
    return r * (dxn - xn * jnp.mean(dxn * xn, axis=-1, keepdims=True))


def _normproj_fwd(name, x, sh, sc, g, w, tm, rider=None, w_t=False, out_dtype=F32):
    def fn(x, sh, sc, g, w):
        xn, _ = _rms(x)
        h = (xn * (g * (1.0 + sc)) + sh).astype(BF16)
        return lax.dot_general(h, w, (((1,), (1 if w_t else 0,)), ((), ())), preferred_element_type=F32), h

    return _rowwise(name, fn, [x], [sh, sc, g, w], [(w.shape[0 if w_t else 1], out_dtype), (D, BF16)], [], tm, rider=rider)


def _normproj_bwd(name, x, dpre, dres, sc, g, w, tm, rider=None, w_t=False):
    def fn(x, dpre, dres, sc, g, w):
        xn, r = _rms(x)
        dh = lax.dot_general(dpre, w, (((1,), (0 if w_t else 1,)), ((), ())), preferred_element_type=F32)
        da = _colsum(dh * xn)
        dx = _rms_bwd(dh * (g * (1.0 + sc)), xn, r) + dres
        return dx, _colsum(dh), da * g, da * (1.0 + sc)

    vec = ((1, D), F32)
    return _rowwise(name, fn, [x, dpre, dres], [sc, g, w], [(D, F32)], [vec, vec, vec], tm, rider=rider)


def _stage2(a_b, blk, cp, u1c, qp, kp, vp, bb, lg, lb, alog, dtb):
    y_a = a_b * cp
    u1 = u1c + bb
    mu = jnp.mean(u1, axis=-1, keepdims=True)
    uc = u1 - mu
    var = jnp.mean(uc * uc, axis=-1, keepdims=True)
    y_b = _silu(uc * lax.rsqrt(var + LN_EPS) * lg + lb)

    def l2(t):
        t = _silu(t)
        return t * lax.rsqrt(jnp.sum(t * t, axis=-1, keepdims=True) + EPS)

    q = [l2(t) for t in qp]
    k = [l2(t) for t in kp]
    v = _silu(vp)
    lane = _iota2(blk.shape, 1)
    gdec = -jnp.exp(alog) * _softplus(blk + dtb)
    gb = jnp.where(lane < HEADS, gdec, jnp.where(lane < 2 * HEADS, _sigmoid(blk), 0.0))
    return y_a, y_b, q, k, v, gb


def _heads_of(x, base=0):
    return [x[:, base + h * HD:base + (h + 1) * HD] for h in range(HEADS)]


def _stage2_fwd(proj, convout, bb, lg, lb, alog, dtb, tm):
    def fn(a_b, blk, co, bb, lg, lb, alog, dtb):
        y_a, y_b, q, k, v, gb = _stage2(a_b, blk, co[:, 0:256], co[:, 256:512], _heads_of(co, 512), _heads_of(co, 1024),
                                        co[:, 1536:2048], bb, lg, lb, alog, dtb)
        return jnp.concatenate([y_a, y_b], axis=1), jnp.concatenate(q + k + [v], axis=1), gb

    return _rowwise("stage2_fwd", fn, [_cols(proj, 256, 0), _cols(proj, 128, C_GB // 128), convout],
                    [bb, lg, lb, alog, dtb], [(512, BF16), (1536, F32), (128, F32)], [], tm)


def _stage2_bwd(proj, convout, dy_ab, dq, dk, dv, dgb, bb, lg, lb, alog, dtb, tm, rider=None):
    def fn(a_b, blk, co, dy_ab, dq, dk, dv, dgb, bb, lg, lb, alog, dtb):
        args = (a_b, blk, co[:, 0:256], co[:, 256:512], _heads_of(co, 512), _heads_of(co, 1024), co[:, 1536:2048],
                bb, lg, lb, alog, dtb)
        _, vjp = jax.vjp(_stage2, *args)
        ct = (dy_ab[:, 0:256], dy_ab[:, 256:512], _heads_of(dq), _heads_of(dk), dv, dgb)
        da_b, dblk, dcp, du1c, dqp, dkp, dvp, dbb, dlg, dlb, dalog, ddtb = vjp(ct)
        dco = jnp.concatenate([dcp, du1c] + dqp + dkp + [dvp], axis=1)
        return dco, da_b, dblk, dbb, dlg, dlb, dalog, ddtb

    v256, v128 = ((1, 256), F32), ((1, 128), F32)
    return _rowwise("stage2_bwd", fn,
                    [_cols(proj, 256, 0), _cols(proj, 128, C_GB // 128), convout, dy_ab, dq, dk, dv, dgb],
                    [bb, lg, lb, alog, dtb], [(N_CONVCOL, F32), (256, F32), (128, F32)],
                    [v256, v256, v256, v128, v128], tm, rider=rider)


def _stage3(o, z, dng):
    ys = []
    for oh, zh in zip(o, z):
        on = oh * lax.rsqrt(jnp.mean(oh * oh, axis=-1, keepdims=True) + EPS)
        ys.append(on * dng * _silu(zh))
    return ys


def _outproj_fwd(x, o, proj, y_ab, g1, dng, wout, tm, rider=None):
    def fn(x, o, z0, z1, z2, z3, y_ab, g1, dng, wout):
        y_c = _stage3(_heads_of(o), [z0, z1, z2, z3], dng)
        ycat = jnp.concatenate([y_ab] + [t.astype(BF16) for t in y_c], axis=1)
        mix = jnp.dot(ycat, wout, preferred_element_type=F32)
        return x + g1 * mix, mix, ycat

    return _rowwise("outproj_fwd", fn, [x, o] + _z_heads(proj) + [y_ab],
                    [g1, dng, wout], [(D, F32), (D, F32), (D, BF16)], [], tm, rider=rider)


def _z_heads(proj):
    return [_cols(proj, HD, C_Z // HD + h) for h in range(HEADS)]


def _outproj_bwd(dx1, mix, o, proj, g1, dng, wout, tm, rider=None):
    def fn(dx1, mix, o, z0, z1, z2, z3, g1, dng, wout):
        dmix = (dx1 * g1).astype(BF16)
        dycat = lax.dot_general(dmix, wout, (((1,), (1,)), ((), ())), preferred_element_type=F32)
        _, vjp = jax.vjp(_stage3, _heads_of(o), [z0, z1, z2, z3], dng)
        do, dz, ddng = vjp(_heads_of(dycat, 512))
        return (dmix, dycat[:, 0:512], jnp.concatenate(do, axis=1), jnp.concatenate(dz, axis=1),
                _colsum(dx1 * mix), ddng)

    return _rowwise("outproj_bwd", fn, [dx1, mix, o] + _z_heads(proj), [g1, dng, wout],
                    [(D, BF16), (512, F32), (512, F32), (512, F32)], [((1, D), F32), ((1, HD), F32)], tm, rider=rider)


_CONV_BLOCKS = ((0, 256, KA), (256, 512, KB), (512, 2048, KC))
_CONV_STRIP = 256


_CONV_ROWS = 32


def _conv_inputs(proj_ref, rows):
    a_c, a_v = proj_ref[rows, C_AC:C_AC + 256], proj_ref[rows, C_AV:C_AV + 256]
    b_a, b_g = proj_ref[rows, C_BA:C_BA + 256], proj_ref[rows, C_BG:C_BG + 256]
    return a_c, a_v, b_a, _sigmoid(b_g)


def _shifted_copies(ext, phases, tm):
    n = tm + HALO - 8
    for b in range(1, 8):
        phases[b - 1] = ext[pl.ds(b, n), 256:512]


def _rows_at(ext, phases, row, col, kw):
    a, b = divmod(row, 8)
    if kw == KB and b:
        return phases[b - 1, pl.ds(8 * a, _CONV_ROWS), :]
    return ext[pl.ds(row, _CONV_ROWS), col:col + _CONV_STRIP]


def _conv_fwd(proj, wa, wb, wc, tm):
    T = proj.shape[0]
    tm = min(tm, T)

    def body(proj_ref, wa_ref, wb_ref, wc_ref, out_ref, ext, phases):
        @pl.when(pl.program_id(0) == 0)
        def _():
            ext[0:HALO, :] = jnp.zeros((HALO, N_CONVCOL), F32)

        a_c, a_v, b_a, sg = _conv_inputs(proj_ref, slice(None))
        ext[HALO:HALO + tm, 0:256] = a_c * a_v
        ext[HALO:HALO + tm, 256:512] = b_a * sg
        ext[HALO:HALO + tm, 512:2048] = proj_ref[:, C_Q:C_Q + 1536]
        _shifted_copies(ext, phases, tm)
        for r0 in range(0, tm, _CONV_ROWS):
            for (c0, c1, kw), w_ref in zip(_CONV_BLOCKS, (wa_ref, wb_ref, wc_ref)):
                for s0 in range(c0, c1, _CONV_STRIP):
                    acc = jnp.zeros((_CONV_ROWS, _CONV_STRIP), F32)
                    for k in range(kw):
                        acc += (w_ref[k:k + 1, s0 - c0:s0 - c0 + _CONV_STRIP]
                                * _rows_at(ext, phases, r0 + HALO - (kw - 1) + k, s0, kw))
                    out_ref[r0:r0 + _CONV_ROWS, s0:s0 + _CONV_STRIP] = acc
        ext[0:HALO, :] = ext[tm:tm + HALO, :]

    full = lambda a: pl.BlockSpec(a.shape, lambda i: (0, 0))
    return pl.pallas_call(
        body, name="conv_fwd", grid=(T // tm,),
        in_specs=[pl.BlockSpec((tm, IN_PAD), lambda i: (i, 0)), full(wa), full(wb), full(wc)],
        out_specs=pl.BlockSpec((tm, N_CONVCOL), lambda i: (i, 0)),
        out_shape=jax.ShapeDtypeStruct((T, N_CONVCOL), F32),
        scratch_shapes=[pltpu.VMEM((HALO + tm, N_CONVCOL), F32), pltpu.VMEM((7, tm + HALO - 8, 256), F32)],
        compiler_params=pltpu.CompilerParams(dimension_semantics=("arbitrary",), vmem_limit_bytes=VMEM_LIMIT),
    )(proj, wa, wb, wc)


def _conv_bwd(proj, dco, da_b, dz, dblk, wa, wb, wc, tm, rider=None):
    T = proj.shape[0]
    tm = min(tm, T)
    n = T // tm

    def body(proj_ref, dco_ref, dab_ref, dz_ref, dblk_ref, wa_ref, wb_ref, wc_ref,
             dproj_ref, dwa_ref, dwb_ref, dwc_ref, ext, acc_a, acc_b, acc_c, phases):
        @pl.when(pl.program_id(0) == 0)
        def _():
            ext[tm:tm + HALO, :] = jnp.zeros((HALO, N_CONVCOL), F32)
            acc_a[...] = jnp.zeros_like(acc_a)
            acc_b[...] = jnp.zeros_like(acc_b)
            acc_c[...] = jnp.zeros_like(acc_c)

        ext[0:tm, :] = dco_ref[...]
        _shifted_copies(ext, phases, tm)

        def taps(w_ref, acc_ref, kw, c0, wc0, xin, r0):
            dx = jnp.zeros((_CONV_ROWS, _CONV_STRIP), F32)
            for k in range(kw):
                sh = _rows_at(ext, phases, r0 + kw - 1 - k, c0, kw)
                dx += w_ref[k:k + 1, wc0:wc0 + _CONV_STRIP] * sh
                pr = sh * xin
                part = pr[0:8]
                for g in range(8, _CONV_ROWS, 8):
                    part += pr[g:g + 8]
                acc_ref[8 * k:8 * k + 8, wc0:wc0 + _CONV_STRIP] += part
            return dx

        for r0 in range(0, tm, _CONV_ROWS):
            rows = slice(r0, r0 + _CONV_ROWS)
            a_c, a_v, b_a, sg = _conv_inputs(proj_ref, rows)
            dp = taps(wa_ref, acc_a, KA, 0, 0, a_c * a_v, r0)
            dproj_ref[rows, C_AC:C_AC + 256] = (dp * a_v).astype(BF16)
            dproj_ref[rows, C_AV:C_AV + 256] = (dp * a_c).astype(BF16)
            du0 = taps(wb_ref, acc_b, KB, 256, 0, b_a * sg, r0)
            dproj_ref[rows, C_BA:C_BA + 256] = (du0 * sg).astype(BF16)
            dproj_ref[rows, C_BG:C_BG + 256] = (du0 * b_a * sg * (1.0 - sg)).astype(BF16)
            for s0 in range(0, 1536, _CONV_STRIP):
                dq = taps(wc_ref, acc_c, KC, 512 + s0, s0, proj_ref[rows, C_Q + s0:C_Q + s0 + _CONV_STRIP], r0)
                dproj_ref[rows, C_Q + s0:C_Q + s0 + _CONV_STRIP] = dq.astype(BF16)
        ext[tm:tm + HALO, :] = ext[0:HALO, :]

        dproj_ref[:, C_AB:C_AB + 256] = dab_ref[...].astype(BF16)
        dproj_ref[:, C_Z:C_Z + 512] = dz_ref[...].astype(BF16)
        dproj_ref[:, C_GB:C_GB + 128] = dblk_ref[...].astype(BF16)

        @pl.when(pl.program_id(0) == n - 1)
        def _():
            for acc_ref, dw_ref, kw in ((acc_a, dwa_ref, KA), (acc_b, dwb_ref, KB), (acc_c, dwc_ref, KC)):
                dw_ref[...] = jnp.zeros_like(dw_ref)
                for k in range(kw):
                    dw_ref[k:k + 1, :] = _colsum(acc_ref[8 * k:8 * k + 8, :])

    rev = lambda w: pl.BlockSpec((tm, w), lambda i: (n - 1 - i, 0))
    full = lambda a: pl.BlockSpec(a.shape, lambda i: (0, 0))
    return _host_call(
        body, "conv_bwd", (n,),
        [rev(IN_PAD), rev(N_CONVCOL), rev(256), rev(512), rev(128), full(wa), full(wb), full(wc)],
        [rev(IN_PAD), full(wa), full(wb), full(wc)],
        [jax.ShapeDtypeStruct((T, IN_PAD), BF16), jax.ShapeDtypeStruct(wa.shape, F32),
         jax.ShapeDtypeStruct(wb.shape, F32), jax.ShapeDtypeStruct(wc.shape, F32)],
        [pltpu.VMEM((tm + HALO, N_CONVCOL), F32), pltpu.VMEM((8 * KA, 256), F32),
         pltpu.VMEM((8 * KB, 256), F32), pltpu.VMEM((8 * KC, 1536), F32), pltpu.VMEM((7, tm + HALO - 8, 256), F32)],
        [proj, dco, da_b, dz, dblk, wa, wb, wc], rider)


def _ffn_out_fwd(x1, gu, g2, wfo, tm, rider=None):
    def fn(x1, gu, g2, wfo):
        s = (_silu(gu[:, :D_FF].astype(F32)) * gu[:, D_FF:].astype(F32)).astype(BF16)
        f = jnp.dot(s, wfo, preferred_element_type=F32)
        return x1 + g2 * f, f

    return _rowwise("ffnout_fwd", fn, [x1, gu], [g2, wfo], [(D, F32), (D, F32)], [], tm, rider=rider)


def _ffn_out_bwd(dx2, gu, f, g2, wfo, tm, rider=None):
    def fn(dx2, gu, f, g2, wfo):
        gate, up = gu[:, :D_FF].astype(F32), gu[:, D_FF:].astype(F32)
        sg = _sigmoid(gate)
        sl = gate * sg
        df = (dx2 * g2).astype(BF16)
        ds = lax.dot_general(df, wfo, (((1,), (1,)), ((), ())), preferred_element_type=F32)
        dgate = ds * up * (sg * (1.0 + gate * (1.0 - sg)))
        dgu = jnp.concatenate([dgate.astype(BF16), (ds * sl).astype(BF16)], axis=1)
        return dgu, sl * up, df, _colsum(dx2 * f)

    return _rowwise("ffnout_bwd", fn, [dx2, gu, f], [g2, wfo], [(2 * D_FF, BF16), (D_FF, BF16), (D, BF16)],
                    [((1, D), F32)], tm, rider=rider)


def _loss_bwd(x, tgt, gfin, tm):
    def fn(x, tgt, gfin):
        xn, r = _rms(x)
        e = xn * gfin - tgt
        loss = 0.5 * jnp.sum(jnp.mean(e * e, axis=-1, keepdims=True), axis=0, keepdims=True)
        dy = e * (1.0 / D)
        return _rms_bwd(dy * gfin, xn, r), jnp.broadcast_to(loss, (1, 128)), _colsum(dy * xn)

    return _rowwise("loss_bwd", fn, [x, tgt], [gfin], [(D, F32)], [((1, 128), F32), ((1, D), F32)], tm)


def _wgrad(name, a, b, bm, bk):
    T, M = a.shape
    N = b.shape[1]
    bk = min(bk, T)

    def body(a_ref, b_ref, o_ref):
        @pl.when(pl.program_id(1) == 0)
        def _():
            o_ref[...] = jnp.zeros_like(o_ref)

        o_ref[...] += lax.dot_general(a_ref[...], b_ref[...], (((0,), (0,)), ((), ())), preferred_element_type=F32)

    return pl.pallas_call(
        body, name=name, grid=(M // bm, T // bk),
        in_specs=[pl.BlockSpec((bk, bm), lambda i, k: (k, i)), pl.BlockSpec((bk, N), lambda i, k: (k, 0))],
        out_specs=pl.BlockSpec((bm, N), lambda i, k: (i, 0)),
        out_shape=jax.ShapeDtypeStruct((M, N), F32),
        compiler_params=pltpu.CompilerParams(dimension_semantics=("arbitrary", "arbitrary"), vmem_limit_bytes=VMEM_LIMIT),
    )(a, b)


IN_BLOCK = 512


def _wgrad_in_blocks(name, a, b, bm, bk):
    T, M = a.shape
    N = b.shape[1]
    bk = min(bk, T)
    nk = T // bk
    per = IN_COLS // N_DEV
    win = IN_BLOCK + 128

    def body(a_ref, b_ref, o_ref, acc):
        @pl.when(pl.program_id(1) == 0)
        def _():
            acc[...] = jnp.zeros_like(acc)

        acc[...] += lax.dot_general(a_ref[...], b_ref[...], (((0,), (0,)), ((), ())), preferred_element_type=F32)

        @pl.when(pl.program_id(1) == nk - 1)
        def _():
            for j in range(N_DEV):
                q, r = divmod(per * j, 128)
                w = acc[:, 128 * q:128 * q + win]
                if r:
                    w = pltpu.roll(w, win - r, axis=1)
                o_ref[j] = w[:, :IN_BLOCK]

    assert 128 * ((per * (N_DEV - 1)) // 128) + win <= N
    return pl.pallas_call(
        body, name=name, grid=(M // bm, nk),
        in_specs=[pl.BlockSpec((bk, bm), lambda i, k: (k, i)), pl.BlockSpec((bk, N), lambda i, k: (k, 0))],
        out_specs=pl.BlockSpec((N_DEV, bm, IN_BLOCK), lambda i, k: (0, i, 0)),
        out_shape=jax.ShapeDtypeStruct((N_DEV, M, IN_BLOCK), F32),
        scratch_shapes=[pltpu.VMEM((bm, N), F32)],
        compiler_params=pltpu.CompilerParams(dimension_semantics=("arbitrary", "arbitrary"), vmem_limit_bytes=VMEM_LIMIT),
    )(a, b)


def _adamw(w, g, m, v):
    m = B1 * m + (1.0 - B1) * g
    v = B2 * v + (1.0 - B2) * (g * g)
    m_hat = m / (1.0 - B1 ** STEP)
    v_hat = v / (1.0 - B2 ** STEP)
    return -LR * (m_hat / (jnp.sqrt(v_hat) + AEPS) + WD * w), m, v


def _adam_call(name, w, g, m, v, tm):
    C = w.shape[1]
    return _rowwise(name, _adamw, [w, g, m, v], [], [(C, F32)] * 3, [], tm)


def _adam_layer(name, g, w, m, v, tm, layer, prev):
    C = g.shape[1]
    return _rowwise(name, lambda g, w, m, v: (g,) + _adamw(w, g, m, v), [g, w, m, v], [], [(C, F32)] * 4, [], tm,
                    layer=layer, prev=prev)


def _reduce_sum(name, own, recv, tm):
    n, R, C = recv.shape
    flat = recv.reshape(n * R, C)
    fn = lambda a, b, r0, r1, r2: ((((a + b) + r0.astype(F32)) + r1.astype(F32)) + r2.astype(F32),)
    return _rowwise(name, fn, _own_blocks(own, R, tm) + [_cols(flat, C, 0, j * R) for j in range(n)], [], [(C, F32)], [],
                    tm, rows=R)[0]


def _pair_add(name, g, from_sib, tm):
    n, R, C = from_sib.shape
    n_r = R // tm
    mine = _cols(g.reshape(N_DEV * R, C), C, 0,
                 block_row=lambda i: (2 * (i // n_r) + lax.axis_index("c")) * n_r + i % n_r)

    return _rowwise(name, lambda a, b: (a + b,), [mine, from_sib.reshape(n * R, C)], [], [(C, BF16)], [], tm, rows=n * R)[0]


def _own_blocks(own, R, tm):
    g, from_sib = own
    C = g.shape[-1]
    n_r = R // tm
    chip = lambda: 2 * lax.axis_index("x") + lax.axis_index("y")
    return [_cols(g.reshape(N_DEV * R, C), C, 0, block_row=lambda i: (2 * chip() + lax.axis_index("c")) * n_r + i),
            _cols(from_sib.reshape(4 * R, C), C, 0, block_row=lambda i: chip() * n_r + i)]


def _reduce_adam(name, own, recv, w, m, v, tm, layer, prev):
    n, R, C = recv.shape
    flat = recv.reshape(n * R, C)

    def fn(a, b, r0, r1, r2, w, m, v):
        g = (((a + b) + r0.astype(F32)) + r1.astype(F32)) + r2.astype(F32)
        return (g,) + _adamw(w, g, m, v)

    return _rowwise(name, fn, _own_blocks(own, R, tm) + [_cols(flat, C, 0, j * R) for j in range(n)] + [w, m, v], [],
                    [(C, F32)] * 4, [], tm, rows=R, layer=layer, prev=prev)


_OFFSETS = [(dx, dy, dc) for dx in (0, 1) for dy in (0, 1) for dc in (0, 1)][1:]
_MESH = pl.DeviceIdType.MESH


def _coords():
    return lax.axis_index("x"), lax.axis_index("y"), lax.axis_index("c")


def _flip(me, off):
    return tuple((1 - m) if d else m for m, d in zip(me, off))


def _linear(p):
    return 4 * p[0] + 2 * p[1] + p[2]


_CHIP_FLIPS = ((1, 0), (0, 1), (1, 1))


class _Rider:
    def __init__(self, ins, outs, n_remote, n_local, start, finish):
        self.ins, self.outs, self.n_remote, self.n_local, self.start, self.finish = ins, outs, n_remote, n_local, start, finish

    def scratch(self):
        return [pltpu.SemaphoreType.DMA((self.n_remote,)), pltpu.SemaphoreType.DMA((self.n_remote,)),
                pltpu.SemaphoreType.DMA((max(self.n_local, 1),))]


def _run_rider(name, rider):
    def body(*refs):
        n_i, n_o = len(rider.ins), len(rider.outs)
        rider.start(refs[:n_i], refs[n_i:n_i + n_o], *refs[n_i + n_o:])
        rider.finish(refs[:n_i], refs[n_i:n_i + n_o], *refs[n_i + n_o:])

    anyspec = pl.BlockSpec(memory_space=pl.ANY)
    return pl.pallas_call(
        body, name=name, in_specs=[anyspec] * len(rider.ins), out_specs=[anyspec] * len(rider.outs),
        out_shape=list(rider.outs), scratch_shapes=rider.scratch(),
    )(*rider.ins)


def _gather_rider(arrs, layer):
    n = len(arrs)

    def parts(ins, outs, send, recv, loc):
        x, y, c = _coords()
        me, sib = (x, y, c), (x, y, 1 - c)
        chips = [((1 - x) if dx else x, (1 - y) if dy else y) for dx, dy in _CHIP_FLIPS]

        def copy(a, k, block, to, own=False):
            slot = outs[a].at[_linear(block)]
            return pltpu.make_async_remote_copy(src_ref=ins[a].at[layer] if own else slot, dst_ref=slot,
                                                send_sem=send.at[a * 7 + k], recv_sem=recv.at[a * 7 + k],
                                                device_id=to, device_id_type=_MESH)

        local = [pltpu.make_async_copy(ins[a].at[layer], outs[a].at[_linear(me)], loc.at[a]) for a in range(n)]
        first = []
        for a in range(n):
            first.append(copy(a, 0, me, sib, own=True))
            first += [copy(a, 1 + j, me, (*chip, c), own=True) for j, chip in enumerate(chips)]
        return copy, local, first, me, sib, chips, c

    def start(*refs):
        _, local, first, *_ = parts(*refs)
        for cp in local + first:
            cp.start()

    def finish(*refs):
        copy, local, first, me, sib, chips, c = parts(*refs)
        passed = []
        for j, chip in enumerate(chips):
            for a in range(n):
                copy(a, 1 + j, (*chip, c), me).wait_recv()
                cp = copy(a, 4 + j, (*chip, c), sib)
                cp.start()
                passed.append(cp)
        for a in range(n):
            copy(a, 0, sib, me).wait_recv()
            for j, chip in enumerate(chips):
                copy(a, 4 + j, (*chip, 1 - c), me).wait_recv()
        for cp in first + passed:
            cp.wait_send()
        for cp in local:
            cp.wait()

    outs = [jax.ShapeDtypeStruct((N_DEV,) + a.shape[1:], a.dtype) for a in arrs]
    return _Rider(list(arrs), outs, 7 * n, n, start, finish)


def _simple_rider(ins, outs, n_remote, make):
    def start(*refs):
        for cp in make(*refs):
            cp.start()

    def finish(*refs):
        for cp in make(*refs):
            cp.wait()

    return _Rider(ins, outs, n_remote, 0, start, finish)


def _join_riders(a, b):
    assert b.sem0 == a.n_remote and a.n_local == 0 and b.n_local == 0
    n_i, n_o = len(a.ins), len(a.outs)

    def both(fa, fb):
        def run(ins, outs, send, recv, loc):
            fa(ins[:n_i], outs[:n_o], send, recv, loc)
            fb(ins[n_i:], outs[n_o:], send, recv, loc)
        return run

    return _Rider(a.ins + b.ins, a.outs + b.outs, a.n_remote + b.n_remote, 0, both(a.start, b.start), both(a.finish, b.finish))


def _pair_rider(arrs, sem0=0):
    def make(ins, outs, send, recv, loc):
        x, y, c = _coords()
        return [pltpu.make_async_remote_copy(src_ref=ins[a].at[2 * s_ + 1 - c], dst_ref=outs[a].at[s_],
                                             send_sem=send.at[sem0 + 4 * a + s_], recv_sem=recv.at[sem0 + 4 * a + s_],
                                             device_id=(x, y, 1 - c), device_id_type=_MESH)
                for a in range(len(arrs)) for s_ in range(4)]

    r = _simple_rider(list(arrs), [jax.ShapeDtypeStruct((4,) + a.shape[1:], a.dtype) for a in arrs], 4 * len(arrs), make)
    r.sem0 = sem0
    return r


def _chip_rider(arrs, sem0=0):
    nf = len(_CHIP_FLIPS)

    def make(ins, outs, send, recv, loc):
        x, y, c = _coords()
        copies = []
        for a in range(len(arrs)):
            for k, (dx, dy) in enumerate(_CHIP_FLIPS):
                px, py = (1 - x) if dx else x, (1 - y) if dy else y
                copies.append(pltpu.make_async_remote_copy(
                    src_ref=ins[a].at[2 * px + py], dst_ref=outs[a].at[k], send_sem=send.at[sem0 + a * nf + k],
                    recv_sem=recv.at[sem0 + a * nf + k], device_id=(px, py, c), device_id_type=_MESH))
        return copies

    r = _simple_rider(list(arrs), [jax.ShapeDtypeStruct((nf,) + a.shape[1:], a.dtype) for a in arrs], nf * len(arrs), make)
    r.sem0 = sem0
    return r


def _small_allgather(name, packed, rider=None):
    R = packed.shape[0]
    n_ri = len(rider.ins) if rider else 0
    n_ro = len(rider.outs) if rider else 0

    def body(*refs):
        in_ref, r_ins = refs[0], refs[1:1 + n_ri]
        all_ref, sum_ref = refs[1 + n_ri:3 + n_ri]
        r_outs = refs[3 + n_ri:3 + n_ri + n_ro]
        send, recv = refs[3 + n_ri + n_ro:5 + n_ri + n_ro]
        r_sems = refs[5 + n_ri + n_ro:]
        if rider:
            rider.start(r_ins, r_outs, *r_sems)
        me = _coords()
        my = _linear(me)
        all_ref[my] = in_ref[...]
        copies = []
        for k, off in enumerate(_OFFSETS):
            cp = pltpu.make_async_remote_copy(src_ref=in_ref, dst_ref=all_ref.at[my], send_sem=send.at[k], recv_sem=recv.at[k],
                                              device_id=_flip(me, off), device_id_type=_MESH)
            cp.start()
            copies.append(cp)
        for cp in copies:
            cp.wait()
        acc = all_ref[0]
        for j in range(1, N_DEV):
            acc = acc + all_ref[j]
        sum_ref[...] = acc
        if rider:
            rider.finish(r_ins, r_outs, *r_sems)

    vm = pl.BlockSpec(memory_space=pltpu.VMEM)
    anyspec = pl.BlockSpec(memory_space=pl.ANY)
    return pl.pallas_call(
        body, name=name, in_specs=[vm] + [anyspec] * n_ri, out_specs=[vm, vm] + [anyspec] * n_ro,
        out_shape=[jax.ShapeDtypeStruct((N_DEV, R, 128), F32), jax.ShapeDtypeStruct((R, 128), F32)] + (list(rider.outs) if rider else []),
        scratch_shapes=[pltpu.SemaphoreType.DMA((len(_OFFSETS),)), pltpu.SemaphoreType.DMA((len(_OFFSETS),))]
        + (rider.scratch() if rider else []),
        compiler_params=pltpu.CompilerParams(vmem_limit_bytes=VMEM_LIMIT),
    )(packed, *(rider.ins if rider else []))


def _ada_mod(c, w16, bias):
    nc = w16.shape[2]
    kp = len(_OFFSETS)

    def body(c_ref, w_ref, b_ref, rows_ref, act_ref, cbuf, sbuf, send, recv):
        me = _coords()
        my = _linear(me)
        cbuf[my] = c_ref[...]
        copies = []
        for k, off in enumerate(_OFFSETS):
            cp = pltpu.make_async_remote_copy(src_ref=c_ref, dst_ref=cbuf.at[my], send_sem=send.at[k], recv_sem=recv.at[k],
                                              device_id=_flip(me, off), device_id_type=_MESH)
            cp.start()
            copies.append(cp)
        for cp in copies:
            cp.wait()
        act = _silu(jnp.concatenate([cbuf[j] for j in range(N_DEV)], axis=0))
        act_ref[...] = act
        act16 = act.astype(BF16)
        for l in range(DEPTH):
            ml = jnp.dot(act16, w_ref[l], preferred_element_type=F32) + b_ref[l:l + 1, :]
            for j in range(N_DEV):
                sbuf[j, l:l + 1, :] = ml[j:j + 1, :]
        rows_ref[my] = sbuf[my]
        copies = []
        for k, off in enumerate(_OFFSETS):
            peer = _flip(me, off)
            cp = pltpu.make_async_remote_copy(src_ref=sbuf.at[_linear(peer)], dst_ref=rows_ref.at[my], send_sem=send.at[kp + k],
                                              recv_sem=recv.at[kp + k], device_id=peer, device_id_type=_MESH)
            cp.start()
            copies.append(cp)
        for cp in copies:
            cp.wait()

    vm = pl.BlockSpec(memory_space=pltpu.VMEM)
    return pl.pallas_call(
        body, name="ada_mod", in_specs=[vm, vm, vm], out_specs=[vm, vm],
        out_shape=[jax.ShapeDtypeStruct((N_DEV, DEPTH, nc), F32), jax.ShapeDtypeStruct((N_DEV, D), F32)],
        scratch_shapes=[pltpu.VMEM((N_DEV, 1, D), F32), pltpu.VMEM((N_DEV, DEPTH, nc), F32),
                        pltpu.SemaphoreType.DMA((2 * kp,)), pltpu.SemaphoreType.DMA((2 * kp,))],
        compiler_params=pltpu.CompilerParams(vmem_limit_bytes=VMEM_LIMIT),
    )(c, w16, bias)


def _pack(arrs):
    parts = []
    for a in arrs:
        f = a.reshape(-1).astype(F32)
        parts.append(jnp.pad(f, (0, (-f.shape[0]) % 128)))
    flat = jnp.concatenate(parts)
    flat = jnp.pad(flat, (0, (-flat.shape[0]) % 1024))
    return flat.reshape(-1, 128)


def _unpack(packed, shapes):
    flat = packed.reshape(packed.shape[:-2] + (-1,))
    out, r = [], 0
    for s in shapes:
        n = int(np.prod(s))
        out.append(flat[..., r:r + n].reshape(packed.shape[:-2] + tuple(s)))
        r += -(-n // 128) * 128
    return out


def _pad_rows(w, rows):
    return jnp.pad(w, ((0, 0), (0, rows - w.shape[1]), (0, 0)))


_SMALL = ("b_ada", "norm_mix_g", "norm_ffn_g", "conv_a_w", "conf_dw_w", "conf_dw_b", "conf_ln_g", "conf_ln_b",
          "dn_conv_w", "dn_a_log", "dn_dt_bias", "dn_norm_g", "final_norm_g")
_BIG = ("w_in", "w_out", "w_ffn_in", "w_ffn_out")
_WEIGHTS = ("w_ada", "b_ada", "norm_mix_g", "norm_ffn_g", "w_in", "conv_a_w", "conf_dw_w", "conf_dw_b", "conf_ln_g",
            "conf_ln_b", "dn_conv_w", "dn_a_log", "dn_dt_bias", "dn_norm_g", "w_out", "w_ffn_in", "w_ffn_out",
            "final_norm_g")


def _step(x, c, loss_target, W, M, V):
    T = x.shape[1]
    me = _linear(_coords())
    xs, tgt = x[0], loss_target[0]
    vec = lambda a: a.reshape(1, -1)

    nada = W["w_ada"].shape[2]
    rows, act_all = _ada_mod(c, W["w_ada"].astype(BF16), lax.dynamic_slice(W["b_ada"], (0, me * nada), (DEPTH, nada)))
    mod = rows.transpose(1, 0, 2).reshape(DEPTH, 6, 1, D)

    w16 = {k: W[k].astype(BF16) for k in _BIG}
    w16["w_ffn_in"] = W["w_ffn_in"].transpose(0, 2, 1).astype(BF16)

    def whole(g_in=None, g_out=None, g_fi=None, g_fo=None):
        out = {}
        if g_in is not None:
            out["w_in"] = jnp.pad(g_in.transpose(1, 0, 2).reshape(D, IN_COLS), ((0, 0), (0, IN_PAD - IN_COLS)))
        if g_out is not None:
            out["w_out"] = g_out.reshape(D, D)
        if g_fi is not None:
            out["w_ffn_in"] = g_fi.reshape(2 * D_FF, D)
        if g_fo is not None:
            out["w_ffn_out"] = g_fo.reshape(D_FF, D)
        return out

    wts = [dict() for _ in range(DEPTH)]
    gather = lambda names, layer: _gather_rider([w16[k] for k in names], layer)
    wts[0].update(whole(g_in=_run_rider("gather_weights", gather(["w_in"], 0))[0]))
    conv_names = ("conv_a_w", "conf_dw_w", "dn_conv_w")
    conv_all, _ = _small_allgather("gather_conv_w", _pack([W[k] for k in conv_names]))[:2]
    conv_full = [t.transpose(1, 2, 0, 3).reshape(t.shape[1], t.shape[2], -1)
                 for t in _unpack(conv_all, [W[k].shape for k in conv_names])]
    wa, wb, wc = _pad_rows(conv_full[0], 8), _pad_rows(conv_full[1], 32), _pad_rows(conv_full[2], 8)
    lane_pad = lambda a: jnp.pad(a, ((0, 0), (0, 128 - a.shape[1])))
    alog, dtb = lane_pad(W["dn_a_log"]), lane_pad(W["dn_dt_bias"])

    saved = []
    xc = xs
    for l in range(DEPTH):
        more = l + 1 < DEPTH
        sh1, sc1, g1, sh2, sc2, g2 = [mod[l, i] for i in range(6)]
        proj, h, *got = _normproj_fwd("inproj_fwd", xc, sh1, sc1, vec(W["norm_mix_g"][l]), wts[l]["w_in"], 512,
                                      rider=gather(["w_out", "w_ffn_out"], 0) if l == 0 else None)
        if l == 0:
            wts[0].update(whole(g_out=got[0], g_fo=got[1]))
        convout = _conv_fwd(proj, wa[l], wb[l], wc[l], 512)
        y_ab, qkv, gb = _stage2_fwd(proj, convout, vec(W["conf_dw_b"][l]), vec(W["conf_ln_g"][l]), vec(W["conf_ln_b"][l]),
                                    alog[l:l + 1], dtb[l:l + 1], 512)
        o, ss, inv, *got = _delta_fwd(qkv, gb, DELTA_NB,
                                 rider=gather(["w_ffn_in"], 0) if l == 0 else gather(["w_in"], l + 1) if more else None)
        if l == 0:
            wts[0].update(whole(g_fi=got[0]))
        elif more:
            wts[l + 1].update(whole(g_in=got[0]))
        x1, mix, ycat, *got = _outproj_fwd(xc, o, proj, y_ab, g1, vec(W["dn_norm_g"][l]), wts[l]["w_out"], 512,
                                           rider=gather(["w_in"], 1) if l == 0 else None)
        if l == 0:
            wts[1].update(whole(g_in=got[0]))
        gu, h2, *got = _normproj_fwd("ffnin_fwd", x1, sh2, sc2, vec(W["norm_ffn_g"][l]), wts[l]["w_ffn_in"], 512,
                                     rider=gather(["w_ffn_in"], l + 1) if more else None, w_t=True, out_dtype=BF16)
        if more:
            wts[l + 1].update(whole(g_fi=got[0]))
        x2, f, *got = _ffn_out_fwd(x1, gu, g2, wts[l]["w_ffn_out"], 256,
                                   rider=gather(["w_ffn_out", "w_out"], l + 1) if more else None)
        if more:
            wts[l + 1].update(whole(g_fo=got[0], g_out=got[1]))
        saved.append((xc, proj, h, convout, qkv, gb, o, ss, inv, mix, ycat, x1, gu, h2, f))
        xc = x2

    dx, loss_row, d_gfin = _loss_bwd(xc, tgt, vec(W["final_norm_g"]), 512)
    loss = lax.psum(loss_row[0, 0], ("x", "y", "c"))

    big_out = {k: None for k in _BIG}
    dmod, small = [None] * DEPTH, [None] * DEPTH
    blocks = lambda g: g.reshape(N_DEV, -1, g.shape[-1])
    pair_tm = {"w_in": 512, "w_out": 128, "w_ffn_in": 704, "w_ffn_out": 352}
    sum_tm = {"w_in": 256, "w_out": 128, "w_ffn_in": 176, "w_ffn_out": 176}
    turned = lambda a: a.transpose(0, 2, 1)
    wmv = {k: (W[k], M[k], V[k]) for k in _BIG}
    wmv["w_ffn_in"] = tuple(turned(a) for a in wmv["w_ffn_in"])

    def pair_sum(k, g, from_sib):
        return _pair_add("pair_add_" + k, g, from_sib, pair_tm[k]).reshape(from_sib.shape), (g, from_sib)

    def finish(k, layer, own, r):
        if k == "w_in":
            g = _reduce_sum("reduce_" + k, own, r, sum_tm[k])[:, :IN_COLS // N_DEV]
            big_out[k] = _adam_layer("adam_" + k, g, *wmv[k], 256, layer, big_out[k])
        else:
            big_out[k] = _reduce_adam("reduce_adam_" + k, own, r, *wmv[k], sum_tm[k], layer, big_out[k])

    above = None
    for l in reversed(range(DEPTH)):
        xc, proj, h, convout, qkv, gb, o, ss, inv, mix, ycat, x1, gu, h2, f = saved[l]
        sh1, sc1, g1, sh2, sc2, g2 = [mod[l, i] for i in range(6)]
        gm, gf = vec(W["norm_mix_g"][l]), vec(W["norm_ffn_g"][l])
        bb, lg, lb = vec(W["conf_dw_b"][l]), vec(W["conf_ln_g"][l]), vec(W["conf_ln_b"][l])
        dng = vec(W["dn_norm_g"][l])
        wl = wts[l]

        dgu, s, df, d_g2, *got = _ffn_out_bwd(dx, gu, f, g2, wl["w_ffn_out"], 256,
                                              rider=_chip_rider([above[1]]) if above else None)
        if above:
            finish("w_in", above[0], above[2], got[0])
        gw_fo = blocks(_wgrad("wgrad_ffn_out", s, df, 1408, 1024))
        dx1, d_sh2, d_sc2, d_gf, *got = _normproj_bwd("ffnin_bwd", x1, dgu, dx, sc2, gf, wl["w_ffn_in"], 512,
                                                      rider=_pair_rider([gw_fo]), w_t=True)
        fo16, fo_own = pair_sum("w_ffn_out", gw_fo, got[0])
        gw_fi = blocks(_wgrad("wgrad_ffn_in", dgu, h2, 1408, 1024))
        dmix, dy_ab, do, dz, d_g1, d_dng, *got = _outproj_bwd(dx1, mix, o, proj, g1, dng, wl["w_out"], 512,
                                                               rider=_pair_rider([gw_fi]))
        fi16, fi_own = pair_sum("w_ffn_in", gw_fi, got[0])
        gw_out = blocks(_wgrad("wgrad_out", ycat, dmix, 512, 2048))
        dqkv, *got = _delta_bwd(qkv, gb, ss, inv, do, DELTA_NB,
                                rider=_join_riders(_chip_rider([fi16]), _pair_rider([gw_out], sem0=3)))
        finish("w_ffn_in", l, fi_own, got[0])
        out16, out_own = pair_sum("w_out", gw_out, got[1])
        dco, da_b, dblk, d_bb, d_lg, d_lb, d_alog, d_dtb = _stage2_bwd(
            proj, convout, dy_ab, _cols(dqkv, D_DN, 0), _cols(dqkv, D_DN, 1), _cols(dqkv, D_DN, 2),
            _cols(dqkv, 128, 3 * D_DN // 128), bb, lg, lb, alog[l:l + 1], dtb[l:l + 1], 512)
        dproj, d_wa, d_wb, d_wc, *got = _conv_bwd(proj, dco, da_b, dz, dblk, wa[l], wb[l], wc[l], 512,
                                                  rider=_chip_rider([out16, fo16]))
        finish("w_out", l, out_own, got[0])
        finish("w_ffn_out", l, fo_own, got[1])
        gw_in = _wgrad_in_blocks("wgrad_in", h, dproj, 512, 1024)
        dx, d_sh1, d_sc1, d_gm, *got = _normproj_bwd("inproj_bwd", xc, dproj, dx1, sc1, gm, wl["w_in"], 512,
                                                     rider=_pair_rider([gw_in]) if l else None)
        above = (l,) + pair_sum("w_in", gw_in, got[0] if l else _run_rider("pair_exchange", _pair_rider([gw_in]))[0])

        dmod[l] = jnp.concatenate([d_sh1, d_sc1, d_g1, d_sh2, d_sc2, d_g2], axis=1)
        small[l] = dict(norm_mix_g=d_gm, norm_ffn_g=d_gf, conv_a_w=d_wa[:KA], conf_dw_w=d_wb[:KB], conf_dw_b=d_bb,
                        conf_ln_g=d_lg, conf_ln_b=d_lb, dn_conv_w=d_wc[:KC], dn_a_log=d_alog, dn_dt_bias=d_dtb,
                        dn_norm_g=d_dng)


    names = ("norm_mix_g", "norm_ffn_g", "conv_a_w", "conf_dw_w", "conf_dw_b", "conf_ln_g", "conf_ln_b", "dn_conv_w",
             "dn_a_log", "dn_dt_bias", "dn_norm_g")
    pieces = [jnp.stack(dmod)] + [jnp.stack([small[l][k] for l in range(DEPTH)]) for k in names] + [d_gfin]
    shapes = [p.shape for p in pieces]
    every, total, in_recv = _small_allgather("gather_small_grads", _pack(pieces), rider=_chip_rider([above[1]]))
    finish("w_in", above[0], above[2], in_recv)
    tot = dict(zip(("dmod",) + names + ("final_norm_g",), _unpack(total, shapes)))
    dmod_all = _unpack(every, shapes[:1])[0]

    grads = {}
    grads["b_ada"] = tot["dmod"].reshape(DEPTH, 6 * D)
    for k in ("norm_mix_g", "norm_ffn_g", "conf_dw_b", "conf_ln_g", "conf_ln_b", "dn_norm_g"):
        grads[k] = tot[k].reshape(W[k].shape)
    grads["dn_a_log"] = tot["dn_a_log"].reshape(DEPTH, 128)[:, :HEADS]
    grads["dn_dt_bias"] = tot["dn_dt_bias"].reshape(DEPTH, 128)[:, :HEADS]
    grads["final_norm_g"] = tot["final_norm_g"].reshape(D)
    for k in conv_names:
        nloc = W[k].shape[2]
        grads[k] = lax.dynamic_slice_in_dim(tot[k], me * nloc, nloc, axis=2)

    dm = lax.dynamic_slice_in_dim(dmod_all.reshape(N_DEV, DEPTH, 6 * D), me * nada, nada, axis=2)
    pad16 = lambda a: jnp.pad(a, ((0, 16 - N_DEV), (0, 0))).astype(BF16)
    g_ada = _wgrad("wgrad_ada", pad16(act_all), pad16(dm.reshape(N_DEV, DEPTH * nada)), 256, 16)
    grads["w_ada"] = g_ada.reshape(D, DEPTH, nada).transpose(1, 0, 2)

    delta, new_m, new_v = {}, {}, {}
    r2 = lambda a: a.reshape(DEPTH * D, nada)
    d_, m_, v_ = _adam_call("adam_ada", r2(W["w_ada"]), r2(grads["w_ada"]), r2(M["w_ada"]), r2(V["w_ada"]), 512)
    delta["w_ada"], new_m["w_ada"], new_v["w_ada"] = [t.reshape(W["w_ada"].shape) for t in (d_, m_, v_)]
    sshapes = [W[k].shape for k in _SMALL]
    d_, m_, v_ = _adam_call("adam_small", _pack([W[k] for k in _SMALL]), _pack([grads[k] for k in _SMALL]),
                            _pack([M[k] for k in _SMALL]), _pack([V[k] for k in _SMALL]), 4096)
    for dst, packed in ((delta, d_), (new_m, m_), (new_v, v_)):
        dst.update(zip(_SMALL, _unpack(packed, sshapes)))
    for k in _BIG:
        grads[k], delta[k], new_m[k], new_v[k] = [turned(a) for a in big_out[k]] if k == "w_ffn_in" else big_out[k]

    return (loss, dx[None], *[grads[k] for k in _WEIGHTS], *[delta[k] for k in _WEIGHTS],
            *[new_m[k] for k in _WEIGHTS], *[new_v[k] for k in _WEIGHTS])


def kernel(x, c, w_ada, b_ada, norm_mix_g, norm_ffn_g, w_in, conv_a_w, conf_dw_w, conf_dw_b, conf_ln_g, conf_ln_b, dn_conv_w, dn_a_log, dn_dt_bias, dn_norm_g, w_out, w_ffn_in, w_ffn_out, final_norm_g, loss_target, m_w_ada, m_b_ada, m_norm_mix_g, m_norm_ffn_g, m_w_in, m_conv_a_w, m_conf_dw_w, m_conf_dw_b, m_conf_ln_g, m_conf_ln_b, m_dn_conv_w, m_dn_a_log, m_dn_dt_bias, m_dn_norm_g, m_w_out, m_w_ffn_in, m_w_ffn_out, m_final_norm_g, v_w_ada, v_b_ada, v_norm_mix_g, v_norm_ffn_g, v_w_in, v_conv_a_w, v_conf_dw_w, v_conf_dw_b, v_conf_ln_g, v_conf_ln_b, v_dn_conv_w, v_dn_a_log, v_dn_dt_bias, v_dn_norm_g, v_w_out, v_w_ffn_in, v_w_ffn_out, v_final_norm_g):
    a = dict(locals())
    W = {k: a[k] for k in _WEIGHTS}
    M = {k: a["m_" + k] for k in _WEIGHTS}
    V = {k: a["v_" + k] for k in _WEIGHTS}
    return _step(x, c, loss_target, W, M, V)
```

```python
import functools

import jax
import jax.numpy as jnp
import numpy as np
from jax import lax
from jax.experimental import pallas as pl
from jax.experimental.pallas import tpu as pltpu

F32 = jnp.float32
BF16 = jnp.bfloat16

N_DEV = 8
D = 1024
DEPTH = 4
D_CONV = 256
D_CONF = 256
D_DN = 512
HEADS = 4
HD = 128
KA, KB, KC = 3, 31, 4
CHUNK = 64
D_FF = 2816
IN_COLS = 3336
IN_PAD = 3456
N_CONVCOL = 2048
EPS = 1e-6
LN_EPS = 1e-5
HALO = 32
VMEM_LIMIT = 56 * 1024 * 1024
DELTA_NB = 8

C_AB, C_AC, C_AV, C_BA, C_BG, C_Q, C_Z, C_GB = 0, 256, 512, 768, 1024, 1280, 2816, 3328

LR, B1, B2, AEPS, WD, STEP = 0.001, 0.9, 0.999, 1e-08, 0.01, 10


def _dot(a, b, dims, hi):
    if hi:
        return lax.dot_general(a.astype(F32), b.astype(F32), (dims, ((), ())), precision=lax.Precision.HIGHEST,
                               preferred_element_type=F32)
    return lax.dot_general(a.astype(BF16), b.astype(BF16), (dims, ((), ())), preferred_element_type=F32)


@functools.partial(jax.custom_vjp, nondiff_argnums=(2,))
def mm_nn(a, b, hi=False):
    return _dot(a, b, ((1,), (0,)), hi)


@functools.partial(jax.custom_vjp, nondiff_argnums=(2,))
def mm_nt(a, b, hi=False):
    return _dot(a, b, ((1,), (1,)), hi)


@functools.partial(jax.custom_vjp, nondiff_argnums=(2,))
def mm_tn(a, b, hi=False):
    return _dot(a, b, ((0,), (0,)), hi)


mm_nn.defvjp(lambda a, b, hi: (mm_nn(a, b, hi), (a, b)),
             lambda hi, r, g: (mm_nt(g, r[1], hi), mm_tn(r[0], g, hi)))
mm_nt.defvjp(lambda a, b, hi: (mm_nt(a, b, hi), (a, b)),
             lambda hi, r, g: (mm_nn(g, r[1], hi), mm_tn(g, r[0], hi)))
mm_tn.defvjp(lambda a, b, hi: (mm_tn(a, b, hi), (a, b)),
             lambda hi, r, g: (mm_nt(r[1], g, hi), mm_nn(r[0], g, hi)))


def _sigmoid(x):
    return 1.0 / (1.0 + jnp.exp(-x))


def _silu(x):
    return x * _sigmoid(x)


def _softplus(x):
    return jnp.maximum(x, 0.0) + jnp.log(1.0 + jnp.exp(-jnp.abs(x)))


def _iota2(shape, dim):
    return lax.broadcasted_iota(jnp.int32, shape, dim)


def _dot16(a, b):
    return jnp.dot(a.astype(BF16), b.astype(BF16), preferred_element_type=F32)


def _dot_3pass(a, b):
    ah = a.astype(BF16)
    bh = b.astype(BF16)
    al = (a - ah.astype(F32)).astype(BF16)
    bl = (b - bh.astype(F32)).astype(BF16)
    d = lambda x, y: jnp.dot(x, y, preferred_element_type=F32)
    return d(ah, bh) + (d(ah, bl) + d(al, bh))


@jax.custom_vjp
def _unit_lower_inverses(Xs):
    n = Xs[0].shape[0]
    r, c = _iota2((n, n), 0), _iota2((n, n), 1)
    eye = (r == c).astype(F32)

    def joins(b):
        s = b.bit_length() - 1
        return ((r >> (s + 1)) == (c >> (s + 1))) & (((r >> s) & 1) == 1) & (((c >> s) & 1) == 0)

    Ts = [eye + jnp.where(joins(1), x, 0.0) for x in Xs]
    b = 2
    while b < n:
        m = joins(b)
        Ys = [_dot16(jnp.where(m, x, 0.0), t) for x, t in zip(Xs, Ts)]
        Ts = [t + _dot16(t, y) for t, y in zip(Ts, Ys)]
        b *= 2
    Rs = [(eye - t) + _dot_3pass(x, t) for x, t in zip(Xs, Ts)]
    return [t + _dot16(t, r_) for t, r_ in zip(Ts, Rs)]


def _unit_lower_inverses_fwd(Xs):
    Ts = _unit_lower_inverses(Xs)
    return Ts, Ts


def _unit_lower_inverses_bwd(Ts, gs):
    inner = [mm_nt(g, t) for g, t in zip(gs, Ts)]
    return ([mm_tn(t, i) for t, i in zip(Ts, inner)],)


_unit_lower_inverses.defvjp(_unit_lower_inverses_fwd, _unit_lower_inverses_bwd)


@jax.custom_vjp
def _saved_inverses(Xs, Ts):
    return list(Ts)


_saved_inverses.defvjp(lambda Xs, Ts: (list(Ts), Ts),
                       lambda Ts, gs: (_unit_lower_inverses_bwd(Ts, gs)[0], [jnp.zeros_like(t) for t in Ts]))


def _delta_chunks(qs, ks, vs, gbs, Ss, Ts=None, keep=None):
    C = CHUNK
    nb = len(gbs)
    pairs = [(c, h) for c in range(nb) for h in range(HEADS)]
    each = lambda fn, *lists: [fn(*a) for a in zip(*lists)]
    row = _iota2((C, C), 0)
    col = _iota2((C, C), 1)
    causal = row >= col
    strict = row > col
    tri = causal.astype(F32)
    eye = (row == col).astype(F32)
    lane = _iota2((C, 128), 1)
    subl = _iota2((128, C), 0)
    last = (_iota2((C, 1), 0) == C - 1).astype(F32)

    gc_all = [mm_nn(tri, gb, True) for gb in gbs]
    gc_t = [g.T for g in gc_all]
    q = [qs[c][h] * (HD ** -0.5) for c, h in pairs]
    k = [ks[c][h] for c, h in pairs]
    v = [vs[c][h] for c, h in pairs]
    gcol = [jnp.sum(jnp.where(lane == h, gc_all[c], 0.0), axis=1, keepdims=True) for c, h in pairs]
    grow = [jnp.sum(jnp.where(subl == h, gc_t[c], 0.0), axis=0, keepdims=True) for c, h in pairs]
    beta = [jnp.sum(jnp.where(lane == HEADS + h, gbs[c], 0.0), axis=1, keepdims=True) for c, h in pairs]
    decay = each(lambda a, b: jnp.where(causal, jnp.exp(jnp.where(causal, a - b, 0.0)), 0.0), gcol, grow)
    kb = each(lambda a, b: a * b, k, beta)
    vb = each(lambda a, b: a * b, v, beta)
    kk = each(lambda a, b: mm_nt(a, b), kb, k)
    X = each(lambda a, d: -jnp.where(strict, a * d, 0.0), kk, decay)
    T = _unit_lower_inverses(X) if Ts is None else _saved_inverses(X, Ts)
    if keep is not None:
        keep.extend(T)
    eg = [jnp.exp(g) for g in gcol]
    u = each(lambda t, a: mm_nn(t, a), T, vb)
    w = each(lambda t, a, e: mm_nn(t, a * e), T, kb, eg)
    qk = each(lambda a, b, d: jnp.where(causal, mm_nt(a, b) * d, 0.0), q, k, decay)
    qg = each(lambda a, e: a * e, q, eg)
    g_last = [jnp.sum(g * last, axis=0, keepdims=True) for g in gcol]
    kd = each(lambda a, gl, g: a * jnp.exp(gl - g), k, g_last, gcol)
    eg_last = [jnp.exp(g) for g in g_last]

    outs = []
    for c in range(nb):
        sl = slice(c * HEADS, (c + 1) * HEADS)
        v_new = each(lambda a, b, S: a - mm_nn(b, S), u[sl], w[sl], Ss)
        oS = each(lambda a, S: mm_nn(a, S), qg[sl], Ss)
        outs.append(each(lambda a, b, n: a + mm_nn(b, n), oS, qk[sl], v_new))
        Ss = each(lambda S, e, a, n: S * e + mm_tn(a, n), Ss, eg_last[sl], kd[sl], v_new)
    return outs, Ss


def _split_chunks(ref, nb):
    return [[ref[c * CHUNK:(c + 1) * CHUNK, h * HD:(h + 1) * HD] for h in range(HEADS)] for c in range(nb)]


def _join_chunks(vals):
    return jnp.concatenate([jnp.concatenate(heads, axis=1) for heads in vals], axis=0)


def _hosted(body, n_in, n_out, rider, n_steps):
    if rider is None:
        return body
    n_ri, n_ro = len(rider.ins), len(rider.outs)

    def wrapped(*refs):
        ins, r_ins = refs[:n_in], refs[n_in:n_in + n_ri]
        outs = refs[n_in + n_ri:n_in + n_ri + n_out]
        r_outs = refs[n_in + n_ri + n_out:n_in + n_ri + n_out + n_ro]
        rest = refs[n_in + n_ri + n_out + n_ro:]
        scr, sems = rest[:len(rest) - 3], rest[len(rest) - 3:]

        @pl.when(pl.program_id(0) == 0)
        def _():
            rider.start(r_ins, r_outs, *sems)

        body(*ins, *outs, *scr)

        @pl.when(pl.program_id(0) == n_steps - 1)
        def _():
            rider.finish(r_ins, r_outs, *sems)

    return wrapped


def _host_call(body, name, grid, in_specs, out_specs, out_shape, scratch, operands, rider):
    out_specs, out_shape = list(out_specs), list(out_shape)
    n_in, n_out = len(in_specs), len(out_specs)
    if rider is not None:
        anyspec = pl.BlockSpec(memory_space=pl.ANY)
        in_specs = list(in_specs) + [anyspec] * len(rider.ins)
        out_specs += [anyspec] * len(rider.outs)
        out_shape += list(rider.outs)
        scratch = list(scratch) + rider.scratch()
        operands = list(operands) + list(rider.ins)
    return pl.pallas_call(
        _hosted(body, n_in, n_out, rider, grid[0]), name=name, grid=grid, in_specs=in_specs, out_specs=out_specs,
        out_shape=out_shape, scratch_shapes=scratch,
        compiler_params=pltpu.CompilerParams(dimension_semantics=("arbitrary",), vmem_limit_bytes=VMEM_LIMIT),
    )(*operands)


def _delta_fwd(qkv, gb, nb, rider=None):
    T = qkv.shape[0]
    nb = min(nb, T // CHUNK)
    rows = nb * CHUNK
    n = T // rows

    def body(q_ref, k_ref, v_ref, gb_ref, o_ref, ss_ref, t_ref, s_scr):
        @pl.when(pl.program_id(0) == 0)
        def _():
            s_scr[...] = jnp.zeros_like(s_scr)

        Ss = [s_scr[h] for h in range(HEADS)]
        for h in range(HEADS):
            ss_ref[0, h] = Ss[h]
        gbs = [gb_ref[c * CHUNK:(c + 1) * CHUNK, :] for c in range(nb)]
        kept = []
        outs, new_S = _delta_chunks(_split_chunks(q_ref, nb), _split_chunks(k_ref, nb), _split_chunks(v_ref, nb), gbs, Ss,
                                    keep=kept)
        o_ref[...] = _join_chunks(outs)
        s_scr[...] = jnp.stack(new_S)
        for c in range(nb):
            for h in range(HEADS):
                t_ref[c, h] = kept[c * HEADS + h]

    row = lambda w, j=0: pl.BlockSpec((rows, w), lambda i: (i, j))
    return _host_call(
        body, "delta_fwd", (n,),
        [row(D_DN, 0), row(D_DN, 1), row(D_DN, 2), row(128)],
        [row(D_DN), pl.BlockSpec((1, HEADS, HD, HD), lambda i: (i, 0, 0, 0)),
         pl.BlockSpec((nb, HEADS, CHUNK, CHUNK), lambda i: (i, 0, 0, 0))],
        [jax.ShapeDtypeStruct((T, D_DN), F32), jax.ShapeDtypeStruct((n, HEADS, HD, HD), F32),
         jax.ShapeDtypeStruct((T // CHUNK, HEADS, CHUNK, CHUNK), F32)],
        [pltpu.VMEM((HEADS, HD, HD), F32)], [qkv, qkv, qkv, gb], rider)


def _delta_bwd(qkv, gb, ss, inv, do, nb, rider=None):
    T = qkv.shape[0]
    nb = min(nb, T // CHUNK)
    rows = nb * CHUNK
    n = T // rows

    def body(q_ref, k_ref, v_ref, gb_ref, ss_ref, t_ref, do_ref, d_ref, ds_scr):
        @pl.when(pl.program_id(0) == 0)
        def _():
            ds_scr[...] = jnp.zeros_like(ds_scr)

        Ss = [ss_ref[0, h] for h in range(HEADS)]
        gbs = [gb_ref[c * CHUNK:(c + 1) * CHUNK, :] for c in range(nb)]
        Ts = [t_ref[c, h] for c in range(nb) for h in range(HEADS)]
        _, vjp = jax.vjp(functools.partial(_delta_chunks, Ts=Ts), _split_chunks(q_ref, nb), _split_chunks(k_ref, nb),
                         _split_chunks(v_ref, nb), gbs, Ss)
        dqs, dks, dvs, dgbs, dSs = vjp((_split_chunks(do_ref, nb), [ds_scr[h] for h in range(HEADS)]))
        d_ref[...] = jnp.concatenate([_join_chunks(dqs), _join_chunks(dks), _join_chunks(dvs),
                                      jnp.concatenate(dgbs, axis=0)], axis=1)
        ds_scr[...] = jnp.stack(dSs)

    row = lambda w, j=0: pl.BlockSpec((rows, w), lambda i: (n - 1 - i, j))
    return _host_call(
        body, "delta_bwd", (n,),
        [row(D_DN, 0), row(D_DN, 1), row(D_DN, 2), row(128),
         pl.BlockSpec((1, HEADS, HD, HD), lambda i: (n - 1 - i, 0, 0, 0)),
         pl.BlockSpec((nb, HEADS, CHUNK, CHUNK), lambda i: (n - 1 - i, 0, 0, 0)), row(D_DN)],
        [row(3 * D_DN + 128)], [jax.ShapeDtypeStruct((T, 3 * D_DN + 128), F32)],
        [pltpu.VMEM((HEADS, HD, HD), F32)], [qkv, qkv, qkv, gb, ss, inv, do], rider)


def _cols(arr, width, index, first_row=0, block_row=None):
    return (arr, width, index, first_row, block_row)


def _rowwise(name, fn, tiled, consts, out_tiled, out_acc, tm, rows=None, rider=None, layer=None, prev=None):
    tiled = [t if isinstance(t, tuple) else (t, t.shape[-1], 0, 0, None) for t in tiled]
    T = tiled[0][0].shape[-2] if rows is None else rows
    tm = min(tm, T)
    assert T % tm == 0 and all(t[3] % tm == 0 for t in tiled)
    n_t, n_c, n_o, n_a = len(tiled), len(consts), len(out_tiled), len(out_acc)

    n_ri = len(rider.ins) if rider else 0
    n_ro = len(rider.outs) if rider else 0
    n_steps = T // tm

    def body(*refs):
        n_in = n_t + n_c + n_ri + (n_o if layer is not None else 0)
        r_ins = refs[n_t + n_c:n_t + n_c + n_ri]
        o_refs = refs[n_in:n_in + n_o]
        a_refs = refs[n_in + n_o:n_in + n_o + n_a]
        r_outs = refs[n_in + n_o + n_a:n_in + n_o + n_a + n_ro]
        sems = refs[n_in + n_o + n_a + n_ro:]
        if rider:
            @pl.when(pl.program_id(0) == 0)
            def _():
                rider.start(r_ins, r_outs, *sems)

        ins = [r[...] for r in refs[:n_t + n_c]]
        outs = fn(*ins)
        for r, val in zip(o_refs, outs[:n_o]):
            r[...] = val.astype(r.dtype)
        if n_a:
            @pl.when(pl.program_id(0) == 0)
            def _():
                for r in a_refs:
                    r[...] = jnp.zeros_like(r)
            for r, val in zip(a_refs, outs[n_o:]):
                r[...] += val
        if rider:
            @pl.when(pl.program_id(0) == n_steps - 1)
            def _():
                rider.finish(r_ins, r_outs, *sems)

    def const_spec(a):
        nd = a.ndim
        return pl.BlockSpec(a.shape, lambda i: (0,) * nd, pipeline_mode=pl.Buffered(1))

    def tile_spec(arr, w, j, r0, block_row):
        row = block_row if block_row is not None else (lambda i: i + r0 // tm)
        if arr.ndim == 3:
            return pl.BlockSpec((None, tm, w), lambda i: (layer, row(i), j))
        return pl.BlockSpec((tm, w), lambda i: (row(i), j))

    in_specs = [tile_spec(*t) for t in tiled]
    in_specs += [const_spec(a) for a in consts]
    if layer is None:
        out_specs = [pl.BlockSpec((tm, w), lambda i: (i, 0)) for (w, _) in out_tiled]
        out_shape = [jax.ShapeDtypeStruct((T, w), dt) for (w, dt) in out_tiled]
    else:
        out_specs = [pl.BlockSpec((None, tm, w), lambda i: (layer, i, 0)) for (w, _) in out_tiled]
        out_shape = [jax.ShapeDtypeStruct((DEPTH, T, w), dt) for (w, dt) in out_tiled]
    out_specs += [pl.BlockSpec(s, lambda i: (0, 0)) for (s, _) in out_acc]
    out_shape += [jax.ShapeDtypeStruct(s, dt) for (s, dt) in out_acc]
    operands = [t[0] for t in tiled] + list(consts)
    scratch = []
    aliases = {}
    if rider:
        anyspec = pl.BlockSpec(memory_space=pl.ANY)
        in_specs += [anyspec] * n_ri
        out_specs += [anyspec] * n_ro
        out_shape += list(rider.outs)
        operands += list(rider.ins)
        scratch = rider.scratch()
    n_prev = 0
    if layer is not None:
        assert rider is None and not out_acc
        if prev is None:
            prev = [jnp.zeros(o.shape, o.dtype) for o in out_shape]
        n_prev = len(prev)
        aliases = {len(operands) + i: i for i in range(n_prev)}
        in_specs += [pl.BlockSpec(memory_space=pl.ANY)] * n_prev
        operands += list(prev)
    return pl.pallas_call(
        body, name=name, grid=(n_steps,), in_specs=in_specs, out_specs=out_specs, out_shape=out_shape, scratch_shapes=scratch,
        input_output_aliases=aliases,
        compiler_params=pltpu.CompilerParams(dimension_semantics=("arbitrary",), vmem_limit_bytes=VMEM_LIMIT),
    )(*operands)


def _colsum(x):
    return jnp.sum(x, axis=0, keepdims=True)


def _rms(x):
    r = lax.rsqrt(jnp.mean(x * x, axis=-1, keepdims=True) + EPS)
    return x * r, r


def _rms_bwd(dxn, xn, r):
    return r * (dxn - xn * jnp.mean(dxn * xn, axis=-1, keepdims=True))


def _normproj_fwd(name, x, sh, sc, g, w, tm, rider=None, w_t=False, out_dtype=F32):
    def fn(x, sh, sc, g, w):
        xn, _ = _rms(x)
        h = (xn * (g * (1.0 + sc)) + sh).astype(BF16)
        return lax.dot_general(h, w, (((1,), (1 if w_t else 0,)), ((), ())), preferred_element_type=F32), h

    return _rowwise(name, fn, [x], [sh, sc, g, w], [(w.shape[0 if w_t else 1], out_dtype), (D, BF16)], [], tm, rider=rider)


def _normproj_bwd(name, x, dpre, dres, sc, g, w, tm, rider=None, w_t=False):
    def fn(x, dpre, dres, sc, g, w):
        xn, r = _rms(x)
        dh = lax.dot_general(dpre, w, (((1,), (0 if w_t else 1,)), ((), ())), preferred_element_type=F32)
        da = _colsum(dh * xn)
        dx = _rms_bwd(dh * (g * (1.0 + sc)), xn, r) + dres
        return dx, _colsum(dh), da * g, da * (1.0 + sc)

    vec = ((1, D), F32)
    return _rowwise(name, fn, [x, dpre, dres], [sc, g, w], [(D, F32)], [vec, vec, vec], tm, rider=rider)


def _stage2(a_b, blk, cp, u1c, qp, kp, vp, bb, lg, lb, alog, dtb):
    y_a = a_b * cp
    u1 = u1c + bb
    mu = jnp.mean(u1, axis=-1, keepdims=True)
    uc = u1 - mu
    var = jnp.mean(uc * uc, axis=-1, keepdims=True)
    y_b = _silu(uc * lax.rsqrt(var + LN_EPS) * lg + lb)

    def l2(t):
        t = _silu(t)
        return t * lax.rsqrt(jnp.sum(t * t, axis=-1, keepdims=True) + EPS)

    q = [l2(t) for t in qp]
    k = [l2(t) for t in kp]
    v = _silu(vp)
    lane = _iota2(blk.shape, 1)
    gdec = -jnp.exp(alog) * _softplus(blk + dtb)
    gb = jnp.where(lane < HEADS, gdec, jnp.where(lane < 2 * HEADS, _sigmoid(blk), 0.0))
    return y_a, y_b, q, k, v, gb


def _heads_of(x, base=0):
    return [x[:, base + h * HD:base + (h + 1) * HD] for h in range(HEADS)]


def _stage2_fwd(proj, convout, bb, lg, lb, alog, dtb, tm):
    def fn(a_b, blk, co, bb, lg, lb, alog, dtb):
        y_a, y_b, q, k, v, gb = _stage2(a_b, blk, co[:, 0:256], co[:, 256:512], _heads_of(co, 512), _heads_of(co, 1024),
                                        co[:, 1536:2048], bb, lg, lb, alog, dtb)
        return jnp.concatenate([y_a, y_b], axis=1), jnp.concatenate(q + k + [v], axis=1), gb

    return _rowwise("stage2_fwd", fn, [_cols(proj, 256, 0), _cols(proj, 128, C_GB // 128), convout],
                    [bb, lg, lb, alog, dtb], [(512, BF16), (1536, F32), (128, F32)], [], tm)


def _stage2_bwd(proj, convout, dy_ab, dq, dk, dv, dgb, bb, lg, lb, alog, dtb, tm, rider=None):
    def fn(a_b, blk, co, dy_ab, dq, dk, dv, dgb, bb, lg, lb, alog, dtb):
        args = (a_b, blk, co[:, 0:256], co[:, 256:512], _heads_of(co, 512), _heads_of(co, 1024), co[:, 1536:2048],
                bb, lg, lb, alog, dtb)
        _, vjp = jax.vjp(_stage2, *args)
        ct = (dy_ab[:, 0:256], dy_ab[:, 256:512], _heads_of(dq), _heads_of(dk), dv, dgb)
        da_b, dblk, dcp, du1c, dqp, dkp, dvp, dbb, dlg, dlb, dalog, ddtb = vjp(ct)
        dco = jnp.concatenate([dcp, du1c] + dqp + dkp + [dvp], axis=1)
        return dco, da_b, dblk, dbb, dlg, dlb, dalog, ddtb

    v256, v128 = ((1, 256), F32), ((1, 128), F32)
    return _rowwise("stage2_bwd", fn,
                    [_cols(proj, 256, 0), _cols(proj, 128, C_GB // 128), convout, dy_ab, dq, dk, dv, dgb],
                    [bb, lg, lb, alog, dtb], [(N_CONVCOL, F32), (256, F32), (128, F32)],
                    [v256, v256, v256, v128, v128], tm, rider=rider)


def _stage3(o, z, dng):
    ys = []
    for oh, zh in zip(o, z):
        on = oh * lax.rsqrt(jnp.mean(oh * oh, axis=-1, keepdims=True) + EPS)
        ys.append(on * dng * _silu(zh))
    return ys


def _outproj_fwd(x, o, proj, y_ab, g1, dng, wout, tm, rider=None):
    def fn(x, o, z0, z1, z2, z3, y_ab, g1, dng, wout):
        y_c = _stage3(_heads_of(o), [z0, z1, z2, z3], dng)
        ycat = jnp.concatenate([y_ab] + [t.astype(BF16) for t in y_c], axis=1)
        mix = jnp.dot(ycat, wout, preferred_element_type=F32)
        return x + g1 * mix, mix, ycat

    return _rowwise("outproj_fwd", fn, [x, o] + _z_heads(proj) + [y_ab],
                    [g1, dng, wout], [(D, F32), (D, F32), (D, BF16)], [], tm, rider=rider)


def _z_heads(proj):
    return [_cols(proj, HD, C_Z // HD + h) for h in range(HEADS)]


def _outproj_bwd(dx1, mix, o, proj, g1, dng, wout, tm, rider=None):
    def fn(dx1, mix, o, z0, z1, z2, z3, g1, dng, wout):
        dmix = (dx1 * g1).astype(BF16)
        dycat = lax.dot_general(dmix, wout, (((1,), (1,)), ((), ())), preferred_element_type=F32)
        _, vjp = jax.vjp(_stage3, _heads_of(o), [z0, z1, z2, z3], dng)
        do, dz, ddng = vjp(_heads_of(dycat, 512))
        return (dmix, dycat[:, 0:512], jnp.concatenate(do, axis=1), jnp.concatenate(dz, axis=1),
                _colsum(dx1 * mix), ddng)

    return _rowwise("outproj_bwd", fn, [dx1, mix, o] + _z_heads(proj), [g1, dng, wout],
                    [(D, BF16), (512, F32), (512, F32), (512, F32)], [((1, D), F32), ((1, HD), F32)], tm, rider=rider)


_CONV_BLOCKS = ((0, 256, KA), (256, 512, KB), (512, 2048, KC))
_CONV_STRIP = 256


_CONV_ROWS = 32


def _conv_inputs(proj_ref, rows):
    a_c, a_v = proj_ref[rows, C_AC:C_AC + 256], proj_ref[rows, C_AV:C_AV + 256]
    b_a, b_g = proj_ref[rows, C_BA:C_BA + 256], proj_ref[rows, C_BG:C_BG + 256]
    return a_c, a_v, b_a, _sigmoid(b_g)


def _shifted_copies(ext, phases, tm):
    n = tm + HALO - 8
    for b in range(1, 8):
        phases[b - 1] = ext[pl.ds(b, n), 256:512]


def _rows_at(ext, phases, row, col, kw):
    a, b = divmod(row, 8)
    if kw == KB and b:
        return phases[b - 1, pl.ds(8 * a, _CONV_ROWS), :]
    return ext[pl.ds(row, _CONV_ROWS), col:col + _CONV_STRIP]


def _conv_fwd(proj, wa, wb, wc, tm):
    T = proj.shape[0]
    tm = min(tm, T)

    def body(proj_ref, wa_ref, wb_ref, wc_ref, out_ref, ext, phases):
        @pl.when(pl.program_id(0) == 0)
        def _():
            ext[0:HALO, :] = jnp.zeros((HALO, N_CONVCOL), F32)

        a_c, a_v, b_a, sg = _conv_inputs(proj_ref, slice(None))
        ext[HALO:HALO + tm, 0:256] = a_c * a_v
        ext[HALO:HALO + tm, 256:512] = b_a * sg
        ext[HALO:HALO + tm, 512:2048] = proj_ref[:, C_Q:C_Q + 1536]
        _shifted_copies(ext, phases, tm)
        for r0 in range(0, tm, _CONV_ROWS):
            for (c0, c1, kw), w_ref in zip(_CONV_BLOCKS, (wa_ref, wb_ref, wc_ref)):
                for s0 in range(c0, c1, _CONV_STRIP):
                    acc = jnp.zeros((_CONV_ROWS, _CONV_STRIP), F32)
                    for k in range(kw):
                        acc += (w_ref[k:k + 1, s0 - c0:s0 - c0 + _CONV_STRIP]
                                * _rows_at(ext, phases, r0 + HALO - (kw - 1) + k, s0, kw))
                    out_ref[r0:r0 + _CONV_ROWS, s0:s0 + _CONV_STRIP] = acc
        ext[0:HALO, :] = ext[tm:tm + HALO, :]

    full = lambda a: pl.BlockSpec(a.shape, lambda i: (0, 0))
    return pl.pallas_call(
        body, name="conv_fwd", grid=(T // tm,),
        in_specs=[pl.BlockSpec((tm, IN_PAD), lambda i: (i, 0)), full(wa), full(wb), full(wc)],
        out_specs=pl.BlockSpec((tm, N_CONVCOL), lambda i: (i, 0)),
        out_shape=jax.ShapeDtypeStruct((T, N_CONVCOL), F32),
        scratch_shapes=[pltpu.VMEM((HALO + tm, N_CONVCOL), F32), pltpu.VMEM((7, tm + HALO - 8, 256), F32)],
        compiler_params=pltpu.CompilerParams(dimension_semantics=("arbitrary",), vmem_limit_bytes=VMEM_LIMIT),
    )(proj, wa, wb, wc)


def _conv_bwd(proj, dco, da_b, dz, dblk, wa, wb, wc, tm, rider=None):
    T = proj.shape[0]
    tm = min(tm, T)
    n = T // tm

    def body(proj_ref, dco_ref, dab_ref, dz_ref, dblk_ref, wa_ref, wb_ref, wc_ref,
             dproj_ref, dwa_ref, dwb_ref, dwc_ref, ext, acc_a, acc_b, acc_c, phases):
        @pl.when(pl.program_id(0) == 0)
        def _():
            ext[tm:tm + HALO, :] = jnp.zeros((HALO, N_CONVCOL), F32)
            acc_a[...] = jnp.zeros_like(acc_a)
            acc_b[...] = jnp.zeros_like(acc_b)
            acc_c[...] = jnp.zeros_like(acc_c)

        ext[0:tm, :] = dco_ref[...]
        _shifted_copies(ext, phases, tm)

        def taps(w_ref, acc_ref, kw, c0, wc0, xin, r0):
            dx = jnp.zeros((_CONV_ROWS, _CONV_STRIP), F32)
            for k in range(kw):
                sh = _rows_at(ext, phases, r0 + kw - 1 - k, c0, kw)
                dx += w_ref[k:k + 1, wc0:wc0 + _CONV_STRIP] * sh
                pr = sh * xin
                part = pr[0:8]
                for g in range(8, _CONV_ROWS, 8):
                    part += pr[g:g + 8]
                acc_ref[8 * k:8 * k + 8, wc0:wc0 + _CONV_STRIP] += part
            return dx

        for r0 in range(0, tm, _CONV_ROWS):
            rows = slice(r0, r0 + _CONV_ROWS)
            a_c, a_v, b_a, sg = _conv_inputs(proj_ref, rows)
            dp = taps(wa_ref, acc_a, KA, 0, 0, a_c * a_v, r0)
            dproj_ref[rows, C_AC:C_AC + 256] = (dp * a_v).astype(BF16)
            dproj_ref[rows, C_AV:C_AV + 256] = (dp * a_c).astype(BF16)
            du0 = taps(wb_ref, acc_b, KB, 256, 0, b_a * sg, r0)
            dproj_ref[rows, C_BA:C_BA + 256] = (du0 * sg).astype(BF16)
            dproj_ref[rows, C_BG:C_BG + 256] = (du0 * b_a * sg * (1.0 - sg)).astype(BF16)
            for s0 in range(0, 1536, _CONV_STRIP):
                dq = taps(wc_ref, acc_c, KC, 512 + s0, s0, proj_ref[rows, C_Q + s0:C_Q + s0 + _CONV_STRIP], r0)
                dproj_ref[rows, C_Q + s0:C_Q + s0 + _CONV_STRIP] = dq.astype(BF16)
        ext[tm:tm + HALO, :] = ext[0:HALO, :]

        dproj_ref[:, C_AB:C_AB + 256] = dab_ref[...].astype(BF16)
        dproj_ref[:, C_Z:C_Z + 512] = dz_ref[...].astype(BF16)
        dproj_ref[:, C_GB:C_GB + 128] = dblk_ref[...].astype(BF16)

        @pl.when(pl.program_id(0) == n - 1)
        def _():
            for acc_ref, dw_ref, kw in ((acc_a, dwa_ref, KA), (acc_b, dwb_ref, KB), (acc_c, dwc_ref, KC)):
                dw_ref[...] = jnp.zeros_like(dw_ref)
                for k in range(kw):
                    dw_ref[k:k + 1, :] = _colsum(acc_ref[8 * k:8 * k + 8, :])

    rev = lambda w: pl.BlockSpec((tm, w), lambda i: (n - 1 - i, 0))
    full = lambda a: pl.BlockSpec(a.shape, lambda i: (0, 0))
    return _host_call(
        body, "conv_bwd", (n,),
        [rev(IN_PAD), rev(N_CONVCOL), rev(256), rev(512), rev(128), full(wa), full(wb), full(wc)],
        [rev(IN_PAD), full(wa), full(wb), full(wc)],
        [jax.ShapeDtypeStruct((T, IN_PAD), BF16), jax.ShapeDtypeStruct(wa.shape, F32),
         jax.ShapeDtypeStruct(wb.shape, F32), jax.ShapeDtypeStruct(wc.shape, F32)],
        [pltpu.VMEM((tm + HALO, N_CONVCOL), F32), pltpu.VMEM((8 * KA, 256), F32),
         pltpu.VMEM((8 * KB, 256), F32), pltpu.VMEM((8 * KC, 1536), F32), pltpu.VMEM((7, tm + HALO - 8, 256), F32)],
        [proj, dco, da_b, dz, dblk, wa, wb, wc], rider)


def _ffn_out_fwd(x1, gu, g2, wfo, tm, rider=None):
    def fn(x1, gu, g2, wfo):
        s = (_silu(gu[:, :D_FF].astype(F32)) * gu[:, D_FF:].astype(F32)).astype(BF16)
        f = jnp.dot(s, wfo, preferred_element_type=F32)
        return x1 + g2 * f, f

    return _rowwise("ffnout_fwd", fn, [x1, gu], [g2, wfo], [(D, F32), (D, F32)], [], tm, rider=rider)


def _ffn_out_bwd(dx2, gu, f, g2, wfo, tm, rider=None):
    def fn(dx2, gu, f, g2, wfo):
        gate, up = gu[:, :D_FF].astype(F32), gu[:, D_FF:].astype(F32)
        sg = _sigmoid(gate)
        sl = gate * sg
        df = (dx2 * g2).astype(BF16)
        ds = lax.dot_general(df, wfo, (((1,), (1,)), ((), ())), preferred_element_type=F32)
        dgate = ds * up * (sg * (1.0 + gate * (1.0 - sg)))
        dgu = jnp.concatenate([dgate.astype(BF16), (ds * sl).astype(BF16)], axis=1)
        return dgu, sl * up, df, _colsum(dx2 * f)

    return _rowwise("ffnout_bwd", fn, [dx2, gu, f], [g2, wfo], [(2 * D_FF, BF16), (D_FF, BF16), (D, BF16)],
                    [((1, D), F32)], tm, rider=rider)


def _loss_bwd(x, tgt, gfin, tm):
    def fn(x, tgt, gfin):
        xn, r = _rms(x)
        e = xn * gfin - tgt
        loss = 0.5 * jnp.sum(jnp.mean(e * e, axis=-1, keepdims=True), axis=0, keepdims=True)
        dy = e * (1.0 / D)
        return _rms_bwd(dy * gfin, xn, r), jnp.broadcast_to(loss, (1, 128)), _colsum(dy * xn)

    return _rowwise("loss_bwd", fn, [x, tgt], [gfin], [(D, F32)], [((1, 128), F32), ((1, D), F32)], tm)


def _wgrad(name, a, b, bm, bk):
    T, M = a.shape
    N = b.shape[1]
    bk = min(bk, T)

    def body(a_ref, b_ref, o_ref):
        @pl.when(pl.program_id(1) == 0)
        def _():
            o_ref[...] = jnp.zeros_like(o_ref)

        o_ref[...] += lax.dot_general(a_ref[...], b_ref[...], (((0,), (0,)), ((), ())), preferred_element_type=F32)

    return pl.pallas_call(
        body, name=name, grid=(M // bm, T // bk),
        in_specs=[pl.BlockSpec((bk, bm), lambda i, k: (k, i)), pl.BlockSpec((bk, N), lambda i, k: (k, 0))],
        out_specs=pl.BlockSpec((bm, N), lambda i, k: (i, 0)),
        out_shape=jax.ShapeDtypeStruct((M, N), F32),
        compiler_params=pltpu.CompilerParams(dimension_semantics=("arbitrary", "arbitrary"), vmem_limit_bytes=VMEM_LIMIT),
    )(a, b)


IN_BLOCK = 512


def _wgrad_in_blocks(name, a, b, bm, bk):
    T, M = a.shape
    N = b.shape[1]
    bk = min(bk, T)
    nk = T // bk
    per = IN_COLS // N_DEV
    win = IN_BLOCK + 128

    def body(a_ref, b_ref, o_ref, acc):
        @pl.when(pl.program_id(1) == 0)
        def _():
            acc[...] = jnp.zeros_like(acc)

        acc[...] += lax.dot_general(a_ref[...], b_ref[...], (((0,), (0,)), ((), ())), preferred_element_type=F32)

        @pl.when(pl.program_id(1) == nk - 1)
        def _():
            for j in range(N_DEV):
                q, r = divmod(per * j, 128)
                w = acc[:, 128 * q:128 * q + win]
                if r:
                    w = pltpu.roll(w, win - r, axis=1)
                o_ref[j] = w[:, :IN_BLOCK]

    assert 128 * ((per * (N_DEV - 1)) // 128) + win <= N
    return pl.pallas_call(
        body, name=name, grid=(M // bm, nk),
        in_specs=[pl.BlockSpec((bk, bm), lambda i, k: (k, i)), pl.BlockSpec((bk, N), lambda i, k: (k, 0))],
        out_specs=pl.BlockSpec((N_DEV, bm, IN_BLOCK), lambda i, k: (0, i, 0)),
        out_shape=jax.ShapeDtypeStruct((N_DEV, M, IN_BLOCK), F32),
        scratch_shapes=[pltpu.VMEM((bm, N), F32)],
        compiler_params=pltpu.CompilerParams(dimension_semantics=("arbitrary", "arbitrary"), vmem_limit_bytes=VMEM_LIMIT),
    )(a, b)


def _adamw(w, g, m, v):
    m = B1 * m + (1.0 - B1) * g
    v = B2 * v + (1.0 - B2) * (g * g)
    m_hat = m / (1.0 - B1 ** STEP)
    v_hat = v / (1.0 - B2 ** STEP)
    return -LR * (m_hat / (jnp.sqrt(v_hat) + AEPS) + WD * w), m, v


def _adam_call(name, w, g, m, v, tm):
    C = w.shape[1]
    return _rowwise(name, _adamw, [w, g, m, v], [], [(C, F32)] * 3, [], tm)


def _adam_layer(name, g, w, m, v, tm, layer, prev):
    C = g.shape[1]
    return _rowwise(name, lambda g, w, m, v: (g,) + _adamw(w, g, m, v), [g, w, m, v], [], [(C, F32)] * 4, [], tm,
                    layer=layer, prev=prev)


def _reduce_sum(name, own, recv, tm):
    n, R, C = recv.shape
    flat = recv.reshape(n * R, C)
    fn = lambda a, b, r0, r1, r2: ((((a + b) + r0.astype(F32)) + r1.astype(F32)) + r2.astype(F32),)
    return _rowwise(name, fn, _own_blocks(own, R, tm) + [_cols(flat, C, 0, j * R) for j in range(n)], [], [(C, F32)], [],
                    tm, rows=R)[0]


def _pair_add(name, g, from_sib, tm):
    n, R, C = from_sib.shape
    n_r = R // tm
    mine = _cols(g.reshape(N_DEV * R, C), C, 0,
                 block_row=lambda i: (2 * (i // n_r) + lax.axis_index("c")) * n_r + i % n_r)

    return _rowwise(name, lambda a, b: (a + b,), [mine, from_sib.reshape(n * R, C)], [], [(C, BF16)], [], tm, rows=n * R)[0]


def _own_blocks(own, R, tm):
    g, from_sib = own
    C = g.shape[-1]
    n_r = R // tm
    chip = lambda: 2 * lax.axis_index("x") + lax.axis_index("y")
    return [_cols(g.reshape(N_DEV * R, C), C, 0, block_row=lambda i: (2 * chip() + lax.axis_index("c")) * n_r + i),
            _cols(from_sib.reshape(4 * R, C), C, 0, block_row=lambda i: chip() * n_r + i)]


def _reduce_adam(name, own, recv, w, m, v, tm, layer, prev):
    n, R, C = recv.shape
    flat = recv.reshape(n * R, C)

    def fn(a, b, r0, r1, r2, w, m, v):
        g = (((a + b) + r0.astype(F32)) + r1.astype(F32)) + r2.astype(F32)
        return (g,) + _adamw(w, g, m, v)

    return _rowwise(name, fn, _own_blocks(own, R, tm) + [_cols(flat, C, 0, j * R) for j in range(n)] + [w, m, v], [],
                    [(C, F32)] * 4, [], tm, rows=R, layer=layer, prev=prev)


_OFFSETS = [(dx, dy, dc) for dx in (0, 1) for dy in (0, 1) for dc in (0, 1)][1:]
_MESH = pl.DeviceIdType.MESH


def _coords():
    return lax.axis_index("x"), lax.axis_index("y"), lax.axis_index("c")


def _flip(me, off):
    return tuple((1 - m) if d else m for m, d in zip(me, off))


def _linear(p):
    return 4 * p[0] + 2 * p[1] + p[2]


_CHIP_FLIPS = ((1, 0), (0, 1), (1, 1))


class _Rider:
    def __init__(self, ins, outs, n_remote, n_local, start, finish):
        self.ins, self.outs, self.n_remote, self.n_local, self.start, self.finish = ins, outs, n_remote, n_local, start, finish

    def scratch(self):
        return [pltpu.SemaphoreType.DMA((self.n_remote,)), pltpu.SemaphoreType.DMA((self.n_remote,)),
                pltpu.SemaphoreType.DMA((max(self.n_local, 1),))]


def _run_rider(name, rider):
    def body(*refs):
        n_i, n_o = len(rider.ins), len(rider.outs)
        rider.start(refs[:n_i], refs[n_i:n_i + n_o], *refs[n_i + n_o:])
        rider.finish(refs[:n_i], refs[n_i:n_i + n_o], *refs[n_i + n_o:])

    anyspec = pl.BlockSpec(memory_space=pl.ANY)
    return pl.pallas_call(
        body, name=name, in_specs=[anyspec] * len(rider.ins), out_specs=[anyspec] * len(rider.outs),
        out_shape=list(rider.outs), scratch_shapes=rider.scratch(),
    )(*rider.ins)


def _gather_rider(arrs, layer):
    n = len(arrs)

    def parts(ins, outs, send, recv, loc):
        x, y, c = _coords()
        me, sib = (x, y, c), (x, y, 1 - c)
        chips = [((1 - x) if dx else x, (1 - y) if dy else y) for dx, dy in _CHIP_FLIPS]

        def copy(a, k, block, to, own=False):
            slot = outs[a].at[_linear(block)]
            return pltpu.make_async_remote_copy(src_ref=ins[a].at[layer] if own else slot, dst_ref=slot,
                                                send_sem=send.at[a * 7 + k], recv_sem=recv.at[a * 7 + k],
                                                device_id=to, device_id_type=_MESH)

        local = [pltpu.make_async_copy(ins[a].at[layer], outs[a].at[_linear(me)], loc.at[a]) for a in range(n)]
        first = []
        for a in range(n):
            first.append(copy(a, 0, me, sib, own=True))
            first += [copy(a, 1 + j, me, (*chip, c), own=True) for j, chip in enumerate(chips)]
        return copy, local, first, me, sib, chips, c

    def start(*refs):
        _, local, first, *_ = parts(*refs)
        for cp in local + first:
            cp.start()

    def finish(*refs):
        copy, local, first, me, sib, chips, c = parts(*refs)
        passed = []
        for j, chip in enumerate(chips):
            for a in range(n):
                copy(a, 1 + j, (*chip, c), me).wait_recv()
                cp = copy(a, 4 + j, (*chip, c), sib)
                cp.start()
                passed.append(cp)
        for a in range(n):
            copy(a, 0, sib, me).wait_recv()
            for j, chip in enumerate(chips):
                copy(a, 4 + j, (*chip, 1 - c), me).wait_recv()
        for cp in first + passed:
            cp.wait_send()
        for cp in local:
            cp.wait()

    outs = [jax.ShapeDtypeStruct((N_DEV,) + a.shape[1:], a.dtype) for a in arrs]
    return _Rider(list(arrs), outs, 7 * n, n, start, finish)


def _simple_rider(ins, outs, n_remote, make):
    def start(*refs):
        for cp in make(*refs):
            cp.start()

    def finish(*refs):
        for cp in make(*refs):
            cp.wait()

    return _Rider(ins, outs, n_remote, 0, start, finish)


def _join_riders(a, b):
    assert b.sem0 == a.n_remote and a.n_local == 0 and b.n_local == 0
    n_i, n_o = len(a.ins), len(a.outs)

    def both(fa, fb):
        def run(ins, outs, send, recv, loc):
            fa(ins[:n_i], outs[:n_o], send, recv, loc)
            fb(ins[n_i:], outs[n_o:], send, recv, loc)
        return run

    return _Rider(a.ins + b.ins, a.outs + b.outs, a.n_remote + b.n_remote, 0, both(a.start, b.start), both(a.finish, b.finish))


def _pair_rider(arrs, sem0=0):
    def make(ins, outs, send, recv, loc):
        x, y, c = _coords()
        return [pltpu.make_async_remote_copy(src_ref=ins[a].at[2 * s_ + 1 - c], dst_ref=outs[a].at[s_],
                                             send_sem=send.at[sem0 + 4 * a + s_], recv_sem=recv.at[sem0 + 4 * a + s_],
                                             device_id=(x, y, 1 - c), device_id_type=_MESH)
                for a in range(len(arrs)) for s_ in range(4)]

    r = _simple_rider(list(arrs), [jax.ShapeDtypeStruct((4,) + a.shape[1:], a.dtype) for a in arrs], 4 * len(arrs), make)
    r.sem0 = sem0
    return r


def _chip_rider(arrs, sem0=0):
    nf = len(_CHIP_FLIPS)

    def make(ins, outs, send, recv, loc):
        x, y, c = _coords()
        copies = []
        for a in range(len(arrs)):
            for k, (dx, dy) in enumerate(_CHIP_FLIPS):
                px, py = (1 - x) if dx else x, (1 - y) if dy else y
                copies.append(pltpu.make_async_remote_copy(
                    src_ref=ins[a].at[2 * px + py], dst_ref=outs[a].at[k], send_sem=send.at[sem0 + a * nf + k],
                    recv_sem=recv.at[sem0 + a * nf + k], device_id=(px, py, c), device_id_type=_MESH))
        return copies

    r = _simple_rider(list(arrs), [jax.ShapeDtypeStruct((nf,) + a.shape[1:], a.dtype) for a in arrs], nf * len(arrs), make)
    r.sem0 = sem0
    return r


def _small_allgather(name, packed, rider=None):
    R = packed.shape[0]
    n_ri = len(rider.ins) if rider else 0
    n_ro = len(rider.outs) if rider else 0

    def body(*refs):
        in_ref, r_ins = refs[0], refs[1:1 + n_ri]
        all_ref, sum_ref = refs[1 + n_ri:3 + n_ri]
        r_outs = refs[3 + n_ri:3 + n_ri + n_ro]
        send, recv = refs[3 + n_ri + n_ro:5 + n_ri + n_ro]
        r_sems = refs[5 + n_ri + n_ro:]
        if rider:
            rider.start(r_ins, r_outs, *r_sems)
        me = _coords()
        my = _linear(me)
        all_ref[my] = in_ref[...]
        copies = []
        for k, off in enumerate(_OFFSETS):
            cp = pltpu.make_async_remote_copy(src_ref=in_ref, dst_ref=all_ref.at[my], send_sem=send.at[k], recv_sem=recv.at[k],
                                              device_id=_flip(me, off), device_id_type=_MESH)
            cp.start()
            copies.append(cp)
        for cp in copies:
            cp.wait()
        acc = all_ref[0]
        for j in range(1, N_DEV):
            acc = acc + all_ref[j]
        sum_ref[...] = acc
        if rider:
            rider.finish(r_ins, r_outs, *r_sems)

    vm = pl.BlockSpec(memory_space=pltpu.VMEM)
    anyspec = pl.BlockSpec(memory_space=pl.ANY)
    return pl.pallas_call(
        body, name=name, in_specs=[vm] + [anyspec] * n_ri, out_specs=[vm, vm] + [anyspec] * n_ro,
        out_shape=[jax.ShapeDtypeStruct((N_DEV, R, 128), F32), jax.ShapeDtypeStruct((R, 128), F32)] + (list(rider.outs) if rider else []),
        scratch_shapes=[pltpu.SemaphoreType.DMA((len(_OFFSETS),)), pltpu.SemaphoreType.DMA((len(_OFFSETS),))]
        + (rider.scratch() if rider else []),
        compiler_params=pltpu.CompilerParams(vmem_limit_bytes=VMEM_LIMIT),
    )(packed, *(rider.ins if rider else []))


def _ada_mod(c, w16, bias):
    nc = w16.shape[2]
    kp = len(_OFFSETS)

    def body(c_ref, w_ref, b_ref, rows_ref, act_ref, cbuf, sbuf, send, recv):
        me = _coords()
        my = _linear(me)
        cbuf[my] = c_ref[...]
        copies = []
        for k, off in enumerate(_OFFSETS):
            cp = pltpu.make_async_remote_copy(src_ref=c_ref, dst_ref=cbuf.at[my], send_sem=send.at[k], recv_sem=recv.at[k],
                                              device_id=_flip(me, off), device_id_type=_MESH)
            cp.start()
            copies.append(cp)
        for cp in copies:
            cp.wait()
        act = _silu(jnp.concatenate([cbuf[j] for j in range(N_DEV)], axis=0))
        act_ref[...] = act
        act16 = act.astype(BF16)
        for l in range(DEPTH):
            ml = jnp.dot(act16, w_ref[l], preferred_element_type=F32) + b_ref[l:l + 1, :]
            for j in range(N_DEV):
                sbuf[j, l:l + 1, :] = ml[j:j + 1, :]
        rows_ref[my] = sbuf[my]
        copies = []
        for k, off in enumerate(_OFFSETS):
            peer = _flip(me, off)
            cp = pltpu.make_async_remote_copy(src_ref=sbuf.at[_linear(peer)], dst_ref=rows_ref.at[my], send_sem=send.at[kp + k],
                                              recv_sem=recv.at[kp + k], device_id=peer, device_id_type=_MESH)
            cp.start()
            copies.append(cp)
        for cp in copies:
            cp.wait()

    vm = pl.BlockSpec(memory_space=pltpu.VMEM)
    return pl.pallas_call(
        body, name="ada_mod", in_specs=[vm, vm, vm], out_specs=[vm, vm],
        out_shape=[jax.ShapeDtypeStruct((N_DEV, DEPTH, nc), F32), jax.ShapeDtypeStruct((N_DEV, D), F32)],
        scratch_shapes=[pltpu.VMEM((N_DEV, 1, D), F32), pltpu.VMEM((N_DEV, DEPTH, nc), F32),
                        pltpu.SemaphoreType.DMA((2 * kp,)), pltpu.SemaphoreType.DMA((2 * kp,))],
        compiler_params=pltpu.CompilerParams(vmem_limit_bytes=VMEM_LIMIT),
    )(c, w16, bias)


def _pack(arrs):
    parts = []
    for a in arrs:
        f = a.reshape(-1).astype(F32)
        parts.append(jnp.pad(f, (0, (-f.shape[0]) % 128)))
    flat = jnp.concatenate(parts)
    flat = jnp.pad(flat, (0, (-flat.shape[0]) % 1024))
    return flat.reshape(-1, 128)


def _unpack(packed, shapes):
    flat = packed.reshape(packed.shape[:-2] + (-1,))
    out, r = [], 0
    for s in shapes:
        n = int(np.prod(s))
        out.append(flat[..., r:r + n].reshape(packed.shape[:-2] + tuple(s)))
        r += -(-n // 128) * 128
    return out


def _pad_rows(w, rows):
    return jnp.pad(w, ((0, 0), (0, rows - w.shape[1]), (0, 0)))


_SMALL = ("b_ada", "norm_mix_g", "norm_ffn_g", "conv_a_w", "conf_dw_w", "conf_dw_b", "conf_ln_g", "conf_ln_b",
          "dn_conv_w", "dn_a_log", "dn_dt_bias", "dn_norm_g", "final_norm_g")
_BIG = ("w_in", "w_out", "w_ffn_in", "w_ffn_out")
_WEIGHTS = ("w_ada", "b_ada", "norm_mix_g", "norm_ffn_g", "w_in", "conv_a_w", "conf_dw_w", "conf_dw_b", "conf_ln_g",
            "conf_ln_b", "dn_conv_w", "dn_a_log", "dn_dt_bias", "dn_norm_g", "w_out", "w_ffn_in", "w_ffn_out",
            "final_norm_g")


def _step(x, c, loss_target, W, M, V):
    T = x.shape[1]
    me = _linear(_coords())
    xs, tgt = x[0], loss_target[0]
    vec = lambda a: a.reshape(1, -1)

    nada = W["w_ada"].shape[2]
    rows, act_all = _ada_mod(c, W["w_ada"].astype(BF16), lax.dynamic_slice(W["b_ada"], (0, me * nada), (DEPTH, nada)))
    mod = rows.transpose(1, 0, 2).reshape(DEPTH, 6, 1, D)

    w16 = {k: W[k].astype(BF16) for k in _BIG}
    w16["w_ffn_in"] = W["w_ffn_in"].transpose(0, 2, 1).astype(BF16)

    def whole(g_in=None, g_out=None, g_fi=None, g_fo=None):
        out = {}
        if g_in is not None:
            out["w_in"] = jnp.pad(g_in.transpose(1, 0, 2).reshape(D, IN_COLS), ((0, 0), (0, IN_PAD - IN_COLS)))
        if g_out is not None:
            out["w_out"] = g_out.reshape(D, D)
        if g_fi is not None:
            out["w_ffn_in"] = g_fi.reshape(2 * D_FF, D)
        if g_fo is not None:
            out["w_ffn_out"] = g_fo.reshape(D_FF, D)
        return out

    wts = [dict() for _ in range(DEPTH)]
    gather = lambda names, layer: _gather_rider([w16[k] for k in names], layer)
    conv_names = ("conv_a_w", "conf_dw_w", "dn_conv_w")
    conv_all, _, g_in0 = _small_allgather("gather_conv_w", _pack([W[k] for k in conv_names]), rider=gather(["w_in"], 0))
    wts[0].update(whole(g_in=g_in0))
    conv_full = [t.transpose(1, 2, 0, 3).reshape(t.shape[1], t.shape[2], -1)
                 for t in _unpack(conv_all, [W[k].shape for k in conv_names])]
    wa, wb, wc = _pad_rows(conv_full[0], 8), _pad_rows(conv_full[1], 32), _pad_rows(conv_full[2], 8)
    lane_pad = lambda a: jnp.pad(a, ((0, 0), (0, 128 - a.shape[1])))
    alog, dtb = lane_pad(W["dn_a_log"]), lane_pad(W["dn_dt_bias"])

    saved = []
    xc = xs
    for l in range(DEPTH):
        more = l + 1 < DEPTH
        sh1, sc1, g1, sh2, sc2, g2 = [mod[l, i] for i in range(6)]
        proj, h, *got = _normproj_fwd("inproj_fwd", xc, sh1, sc1, vec(W["norm_mix_g"][l]), wts[l]["w_in"], 512,
                                      rider=gather(["w_out", "w_ffn_out"], 0) if l == 0 else None)
        if l == 0:
            wts[0].update(whole(g_out=got[0], g_fo=got[1]))
        convout = _conv_fwd(proj, wa[l], wb[l], wc[l], 512)
        y_ab, qkv, gb = _stage2_fwd(proj, convout, vec(W["conf_dw_b"][l]), vec(W["conf_ln_g"][l]), vec(W["conf_ln_b"][l]),
                                    alog[l:l + 1], dtb[l:l + 1], 512)
        o, ss, inv, *got = _delta_fwd(qkv, gb, DELTA_NB,
                                 rider=gather(["w_ffn_in"], 0) if l == 0 else gather(["w_in"], l + 1) if more else None)
        if l == 0:
            wts[0].update(whole(g_fi=got[0]))
        elif more:
            wts[l + 1].update(whole(g_in=got[0]))
        x1, mix, ycat, *got = _outproj_fwd(xc, o, proj, y_ab, g1, vec(W["dn_norm_g"][l]), wts[l]["w_out"], 512,
                                           rider=gather(["w_in"], 1) if l == 0 else None)
        if l == 0:
            wts[1].update(whole(g_in=got[0]))
        gu, h2, *got = _normproj_fwd("ffnin_fwd", x1, sh2, sc2, vec(W["norm_ffn_g"][l]), wts[l]["w_ffn_in"], 512,
                                     rider=gather(["w_ffn_in"], l + 1) if more else None, w_t=True, out_dtype=BF16)
        if more:
            wts[l + 1].update(whole(g_fi=got[0]))
        x2, f, *got = _ffn_out_fwd(x1, gu, g2, wts[l]["w_ffn_out"], 256,
                                   rider=gather(["w_ffn_out", "w_out"], l + 1) if more else None)
        if more:
            wts[l + 1].update(whole(g_fo=got[0], g_out=got[1]))
        saved.append((xc, proj, h, convout, qkv, gb, o, ss, inv, mix, ycat, x1, gu, h2, f))
        xc = x2

    dx, loss_row, d_gfin = _loss_bwd(xc, tgt, vec(W["final_norm_g"]), 512)
    loss = lax.psum(loss_row[0, 0], ("x", "y", "c"))

    big_out = {k: None for k in _BIG}
    dmod, small = [None] * DEPTH, [None] * DEPTH
    blocks = lambda g: g.reshape(N_DEV, -1, g.shape[-1])
    pair_tm = {"w_in": 512, "w_out": 128, "w_ffn_in": 704, "w_ffn_out": 352}
    sum_tm = {"w_in": 256, "w_out": 128, "w_ffn_in": 176, "w_ffn_out": 176}
    turned = lambda a: a.transpose(0, 2, 1)
    wmv = {k: (W[k], M[k], V[k]) for k in _BIG}
    wmv["w_ffn_in"] = tuple(turned(a) for a in wmv["w_ffn_in"])

    def pair_sum(k, g, from_sib):
        return _pair_add("pair_add_" + k, g, from_sib, pair_tm[k]).reshape(from_sib.shape), (g, from_sib)

    def finish(k, layer, own, r):
        if k == "w_in":
            g = _reduce_sum("reduce_" + k, own, r, sum_tm[k])[:, :IN_COLS // N_DEV]
            big_out[k] = _adam_layer("adam_" + k, g, *wmv[k], 256, layer, big_out[k])
        else:
            big_out[k] = _reduce_adam("reduce_adam_" + k, own, r, *wmv[k], sum_tm[k], layer, big_out[k])

    above = None
    for l in reversed(range(DEPTH)):
        xc, proj, h, convout, qkv, gb, o, ss, inv, mix, ycat, x1, gu, h2, f = saved[l]
        sh1, sc1, g1, sh2, sc2, g2 = [mod[l, i] for i in range(6)]
        gm, gf = vec(W["norm_mix_g"][l]), vec(W["norm_ffn_g"][l])
        bb, lg, lb = vec(W["conf_dw_b"][l]), vec(W["conf_ln_g"][l]), vec(W["conf_ln_b"][l])
        dng = vec(W["dn_norm_g"][l])
        wl = wts[l]

        dgu, s, df, d_g2, *got = _ffn_out_bwd(dx, gu, f, g2, wl["w_ffn_out"], 256,
                                              rider=_chip_rider([above[1]]) if above else None)
        if above:
            finish("w_in", above[0], above[2], got[0])
        gw_fo = blocks(_wgrad("wgrad_ffn_out", s, df, 1408, 1024))
        dx1, d_sh2, d_sc2, d_gf, *got = _normproj_bwd("ffnin_bwd", x1, dgu, dx, sc2, gf, wl["w_ffn_in"], 512,
                                                      rider=_pair_rider([gw_fo]), w_t=True)
        fo16, fo_own = pair_sum("w_ffn_out", gw_fo, got[0])
        gw_fi = blocks(_wgrad("wgrad_ffn_in", dgu, h2, 1408, 1024))
        dmix, dy_ab, do, dz, d_g1, d_dng, *got = _outproj_bwd(dx1, mix, o, proj, g1, dng, wl["w_out"], 512,
                                                               rider=_pair_rider([gw_fi]))
        fi16, fi_own = pair_sum("w_ffn_in", gw_fi, got[0])
        gw_out = blocks(_wgrad("wgrad_out", ycat, dmix, 512, 2048))
        dqkv, *got = _delta_bwd(qkv, gb, ss, inv, do, DELTA_NB,
                                rider=_join_riders(_chip_rider([fi16]), _pair_rider([gw_out], sem0=3)))
        finish("w_ffn_in", l, fi_own, got[0])
        out16, out_own = pair_sum("w_out", gw_out, got[1])
        dco, da_b, dblk, d_bb, d_lg, d_lb, d_alog, d_dtb = _stage2_bwd(
            proj, convout, dy_ab, _cols(dqkv, D_DN, 0), _cols(dqkv, D_DN, 1), _cols(dqkv, D_DN, 2),
            _cols(dqkv, 128, 3 * D_DN // 128), bb, lg, lb, alog[l:l + 1], dtb[l:l + 1], 512)
        dproj, d_wa, d_wb, d_wc, *got = _conv_bwd(proj, dco, da_b, dz, dblk, wa[l], wb[l], wc[l], 512,
                                                  rider=_chip_rider([out16, fo16]))
        finish("w_out", l, out_own, got[0])
        finish("w_ffn_out", l, fo_own, got[1])
        gw_in = _wgrad_in_blocks("wgrad_in", h, dproj, 512, 1024)
        dx, d_sh1, d_sc1, d_gm, *got = _normproj_bwd("inproj_bwd", xc, dproj, dx1, sc1, gm, wl["w_in"], 512,
                                                     rider=_pair_rider([gw_in]) if l else None)
        above = (l,) + pair_sum("w_in", gw_in, got[0] if l else _run_rider("pair_exchange", _pair_rider([gw_in]))[0])

        dmod[l] = jnp.concatenate([d_sh1, d_sc1, d_g1, d_sh2, d_sc2, d_g2], axis=1)
        small[l] = dict(norm_mix_g=d_gm, norm_ffn_g=d_gf, conv_a_w=d_wa[:KA], conf_dw_w=d_wb[:KB], conf_dw_b=d_bb,
                        conf_ln_g=d_lg, conf_ln_b=d_lb, dn_conv_w=d_wc[:KC], dn_a_log=d_alog, dn_dt_bias=d_dtb,
                        dn_norm_g=d_dng)


    names = ("norm_mix_g", "norm_ffn_g", "conv_a_w", "conf_dw_w", "conf_dw_b", "conf_ln_g", "conf_ln_b", "dn_conv_w",
             "dn_a_log", "dn_dt_bias", "dn_norm_g")
    pieces = [jnp.stack(dmod)] + [jnp.stack([small[l][k] for l in range(DEPTH)]) for k in names] + [d_gfin]
    shapes = [p.shape for p in pieces]
    every, total, in_recv = _small_allgather("gather_small_grads", _pack(pieces), rider=_chip_rider([above[1]]))
    finish("w_in", above[0], above[2], in_recv)
    tot = dict(zip(("dmod",) + names + ("final_norm_g",), _unpack(total, shapes)))
    dmod_all = _unpack(every, shapes[:1])[0]

    grads = {}
    grads["b_ada"] = tot["dmod"].reshape(DEPTH, 6 * D)
    for k in ("norm_mix_g", "norm_ffn_g", "conf_dw_b", "conf_ln_g", "conf_ln_b", "dn_norm_g"):
        grads[k] = tot[k].reshape(W[k].shape)
    grads["dn_a_log"] = tot["dn_a_log"].reshape(DEPTH, 128)[:, :HEADS]
    grads["dn_dt_bias"] = tot["dn_dt_bias"].reshape(DEPTH, 128)[:, :HEADS]
    grads["final_norm_g"] = tot["final_norm_g"].reshape(D)
    for k in conv_names:
        nloc = W[k].shape[2]
        grads[k] = lax.dynamic_slice_in_dim(tot[k], me * nloc, nloc, axis=2)

    dm = lax.dynamic_slice_in_dim(dmod_all.reshape(N_DEV, DEPTH, 6 * D), me * nada, nada, axis=2)
    pad16 = lambda a: jnp.pad(a, ((0, 16 - N_DEV), (0, 0))).astype(BF16)
    g_ada = _wgrad("wgrad_ada", pad16(act_all), pad16(dm.reshape(N_DEV, DEPTH * nada)), 256, 16)
    grads["w_ada"] = g_ada.reshape(D, DEPTH, nada).transpose(1, 0, 2)

    delta, new_m, new_v = {}, {}, {}
    r2 = lambda a: a.reshape(DEPTH * D, nada)
    d_, m_, v_ = _adam_call("adam_ada", r2(W["w_ada"]), r2(grads["w_ada"]), r2(M["w_ada"]), r2(V["w_ada"]), 512)
    delta["w_ada"], new_m["w_ada"], new_v["w_ada"] = [t.reshape(W["w_ada"].shape) for t in (d_, m_, v_)]
    sshapes = [W[k].shape for k in _SMALL]
    d_, m_, v_ = _adam_call("adam_small", _pack([W[k] for k in _SMALL]), _pack([grads[k] for k in _SMALL]),
                            _pack([M[k] for k in _SMALL]), _pack([V[k] for k in _SMALL]), 4096)
    for dst, packed in ((delta, d_), (new_m, m_), (new_v, v_)):
        dst.update(zip(_SMALL, _unpack(packed, sshapes)))
    for k in _BIG:
        grads[k], delta[k], new_m[k], new_v[k] = [turned(a) for a in big_out[k]] if k == "w_ffn_in" else big_out[k]

    return (loss, dx[None], *[grads[k] for k in _WEIGHTS], *[delta[k] for k in _WEIGHTS],
            *[new_m[k] for k in _WEIGHTS], *[new_v[k] for k in _WEIGHTS])


def kernel(x, c, w_ada, b_ada, norm_mix_g, norm_ffn_g, w_in, conv_a_w, conf_dw_w, conf_dw_b, conf_ln_g, conf_ln_b, dn_conv_w, dn_a_log, dn_dt_bias, dn_norm_g, w_out, w_ffn_in, w_ffn_out, final_norm_g, loss_target, m_w_ada, m_b_ada, m_norm_mix_g, m_norm_ffn_g, m_w_in, m_conv_a_w, m_conf_dw_w, m_conf_dw_b, m_conf_ln_g, m_conf_ln_b, m_dn_conv_w, m_dn_a_log, m_dn_dt_bias, m_dn_norm_g, m_w_out, m_w_ffn_in, m_w_ffn_out, m_final_norm_g, v_w_ada, v_b_ada, v_norm_mix_g, v_norm_ffn_g, v_w_in, v_conv_a_w, v_conf_dw_w, v_conf_dw_b, v_conf_ln_g, v_conf_ln_b, v_dn_conv_w, v_dn_a_log, v_dn_dt_bias, v_dn_norm_g, v_w_out, v_w_ffn_in, v_w_ffn_out, v_final_norm_g):
    a = dict(locals())
    W = {k: a[k] for k in _WEIGHTS}
    M = {k: a["m_" + k] for k in _WEIGHTS}
    V = {k: a["v_" + k] for k in _WEIGHTS}
    return _step(x, c, loss_target, W, M, V)
```

```python
import functools

import jax
import jax.numpy as jnp
import numpy as np
from jax import lax
from jax.experimental import pallas as pl
from jax.experimental.pallas import tpu as pltpu

F32 = jnp.float32
BF16 = jnp.bfloat16

N_DEV = 8
D = 1024
DEPTH = 4
D_CONV = 256
D_CONF = 256
D_DN = 512
HEADS = 4
HD = 128
KA, KB, KC = 3, 31, 4
CHUNK = 64
D_FF = 2816
IN_COLS = 3336
IN_PAD = 3456
N_CONVCOL = 2048
EPS = 1e-6
LN_EPS = 1e-5
HALO = 32
VMEM_LIMIT = 56 * 1024 * 1024
DELTA_NB = 8

C_AB, C_AC, C_AV, C_BA, C_BG, C_Q, C_Z, C_GB = 0, 256, 512, 768, 1024, 1280, 2816, 3328

LR, B1, B2, AEPS, WD, STEP = 0.001, 0.9, 0.999, 1e-08, 0.01, 10


def _dot(a, b, dims, hi):
    if hi:
        return lax.dot_general(a.astype(F32), b.astype(F32), (dims, ((), ())), precision=lax.Precision.HIGHEST,
                               preferred_element_type=F32)
    return lax.dot_general(a.astype(BF16), b.astype(BF16), (dims, ((), ())), preferred_element_type=F32)


@functools.partial(jax.custom_vjp, nondiff_argnums=(2,))
def mm_nn(a, b, hi=False):
    return _dot(a, b, ((1,), (0,)), hi)


@functools.partial(jax.custom_vjp, nondiff_argnums=(2,))
def mm_nt(a, b, hi=False):
    return _dot(a, b, ((1,), (1,)), hi)


@functools.partial(jax.custom_vjp, nondiff_argnums=(2,))
def mm_tn(a, b, hi=False):
    return _dot(a, b, ((0,), (0,)), hi)


mm_nn.defvjp(lambda a, b, hi: (mm_nn(a, b, hi), (a, b)),
             lambda hi, r, g: (mm_nt(g, r[1], hi), mm_tn(r[0], g, hi)))
mm_nt.defvjp(lambda a, b, hi: (mm_nt(a, b, hi), (a, b)),
             lambda hi, r, g: (mm_nn(g, r[1], hi), mm_tn(g, r[0], hi)))
mm_tn.defvjp(lambda a, b, hi: (mm_tn(a, b, hi), (a, b)),
             lambda hi, r, g: (mm_nt(r[1], g, hi), mm_nn(r[0], g, hi)))


def _sigmoid(x):
    return 1.0 / (1.0 + jnp.exp(-x))


def _silu(x):
    return x * _sigmoid(x)


def _softplus(x):
    return jnp.maximum(x, 0.0) + jnp.log(1.0 + jnp.exp(-jnp.abs(x)))


def _iota2(shape, dim):
    return lax.broadcasted_iota(jnp.int32, shape, dim)


def _dot16(a, b):
    return jnp.dot(a.astype(BF16), b.astype(BF16), preferred_element_type=F32)


def _dot_3pass(a, b):
    ah = a.astype(BF16)
    bh = b.astype(BF16)
    al = (a - ah.astype(F32)).astype(BF16)
    bl = (b - bh.astype(F32)).astype(BF16)
    d = lambda x, y: jnp.dot(x, y, preferred_element_type=F32)
    return d(ah, bh) + (d(ah, bl) + d(al, bh))


@jax.custom_vjp
def _unit_lower_inverses(Xs):
    n = Xs[0].shape[0]
    r, c = _iota2((n, n), 0), _iota2((n, n), 1)
    eye = (r == c).astype(F32)

    def joins(b):
        s = b.bit_length() - 1
        return ((r >> (s + 1)) == (c >> (s + 1))) & (((r >> s) & 1) == 1) & (((c >> s) & 1) == 0)

    Ts = [eye + jnp.where(joins(1), x, 0.0) for x in Xs]
    b = 2
    while b < n:
        m = joins(b)
        Ys = [_dot16(jnp.where(m, x, 0.0), t) for x, t in zip(Xs, Ts)]
        Ts = [t + _dot16(t, y) for t, y in zip(Ts, Ys)]
        b *= 2
    Rs = [(eye - t) + _dot_3pass(x, t) for x, t in zip(Xs, Ts)]
    return [t + _dot16(t, r_) for t, r_ in zip(Ts, Rs)]


def _unit_lower_inverses_fwd(Xs):
    Ts = _unit_lower_inverses(Xs)
    return Ts, Ts


def _unit_lower_inverses_bwd(Ts, gs):
    inner = [mm_nt(g, t) for g, t in zip(gs, Ts)]
    return ([mm_tn(t, i) for t, i in zip(Ts, inner)],)


_unit_lower_inverses.defvjp(_unit_lower_inverses_fwd, _unit_lower_inverses_bwd)


@jax.custom_vjp
def _saved_inverses(Xs, Ts):
    return list(Ts)


_saved_inverses.defvjp(lambda Xs, Ts: (list(Ts), Ts),
                       lambda Ts, gs: (_unit_lower_inverses_bwd(Ts, gs)[0], [jnp.zeros_like(t) for t in Ts]))


def _delta_chunks(qs, ks, vs, gbs, Ss, Ts=None, keep=None):
    C = CHUNK
    nb = len(gbs)
    pairs = [(c, h) for c in range(nb) for h in range(HEADS)]
    each = lambda fn, *lists: [fn(*a) for a in zip(*lists)]
    row = _iota2((C, C), 0)
    col = _iota2((C, C), 1)
    causal = row >= col
    strict = row > col
    tri = causal.astype(F32)
    eye = (row == col).astype(F32)
    lane = _iota2((C, 128), 1)
    subl = _iota2((128, C), 0)
    last = (_iota2((C, 1), 0) == C - 1).astype(F32)

    gc_all = [mm_nn(tri, gb, True) for gb in gbs]
    gc_t = [g.T for g in gc_all]
    q = [qs[c][h] * (HD ** -0.5) for c, h in pairs]
    k = [ks[c][h] for c, h in pairs]
    v = [vs[c][h] for c, h in pairs]
    gcol = [jnp.sum(jnp.where(lane == h, gc_all[c], 0.0), axis=1, keepdims=True) for c, h in pairs]
    grow = [jnp.sum(jnp.where(subl == h, gc_t[c], 0.0), axis=0, keepdims=True) for c, h in pairs]
    beta = [jnp.sum(jnp.where(lane == HEADS + h, gbs[c], 0.0), axis=1, keepdims=True) for c, h in pairs]
    decay = each(lambda a, b: jnp.where(causal, jnp.exp(jnp.where(causal, a - b, 0.0)), 0.0), gcol, grow)
    kb = each(lambda a, b: a * b, k, beta)
    vb = each(lambda a, b: a * b, v, beta)
    kk = each(lambda a, b: mm_nt(a, b), kb, k)
    X = each(lambda a, d: -jnp.where(strict, a * d, 0.0), kk, decay)
    T = _unit_lower_inverses(X) if Ts is None else _saved_inverses(X, Ts)
    if keep is not None:
        keep.extend(T)
    eg = [jnp.exp(g) for g in gcol]
    u = each(lambda t, a: mm_nn(t, a), T, vb)
    w = each(lambda t, a, e: mm_nn(t, a * e), T, kb, eg)
    qk = each(lambda a, b, d: jnp.where(causal, mm_nt(a, b) * d, 0.0), q, k, decay)
    qg = each(lambda a, e: a * e, q, eg)
    g_last = [jnp.sum(g * last, axis=0, keepdims=True) for g in gcol]
    kd = each(lambda a, gl, g: a * jnp.exp(gl - g), k, g_last, gcol)
    eg_last = [jnp.exp(g) for g in g_last]

    outs = []
    for c in range(nb):
        sl = slice(c * HEADS, (c + 1) * HEADS)
        v_new = each(lambda a, b, S: a - mm_nn(b, S), u[sl], w[sl], Ss)
        oS = each(lambda a, S: mm_nn(a, S), qg[sl], Ss)
        outs.append(each(lambda a, b, n: a + mm_nn(b, n), oS, qk[sl], v_new))
        Ss = each(lambda S, e, a, n: S * e + mm_tn(a, n), Ss, eg_last[sl], kd[sl], v_new)
    return outs, Ss


def _split_chunks(ref, nb):
    return [[ref[c * CHUNK:(c + 1) * CHUNK, h * HD:(h + 1) * HD] for h in range(HEADS)] for c in range(nb)]


def _join_chunks(vals):
    return jnp.concatenate([jnp.concatenate(heads, axis=1) for heads in vals], axis=0)


def _hosted(body, n_in, n_out, rider, n_steps):
    if rider is None:
        return body
    n_ri, n_ro = len(rider.ins), len(rider.outs)

    def wrapped(*refs):
        ins, r_ins = refs[:n_in], refs[n_in:n_in + n_ri]
        outs = refs[n_in + n_ri:n_in + n_ri + n_out]
        r_outs = refs[n_in + n_ri + n_out:n_in + n_ri + n_out + n_ro]
        rest = refs[n_in + n_ri + n_out + n_ro:]
        scr, sems = rest[:len(rest) - 3], rest[len(rest) - 3:]

        @pl.when(pl.program_id(0) == 0)
        def _():
            rider.start(r_ins, r_outs, *sems)

        body(*ins, *outs, *scr)

        @pl.when(pl.program_id(0) == n_steps - 1)
        def _():
            rider.finish(r_ins, r_outs, *sems)

    return wrapped


def _host_call(body, name, grid, in_specs, out_specs, out_shape, scratch, operands, rider):
    out_specs, out_shape = list(out_specs), list(out_shape)
    n_in, n_out = len(in_specs), len(out_specs)
    if rider is not None:
        anyspec = pl.BlockSpec(memory_space=pl.ANY)
        in_specs = list(in_specs) + [anyspec] * len(rider.ins)
        out_specs += [anyspec] * len(rider.outs)
        out_shape += list(rider.outs)
        scratch = list(scratch) + rider.scratch()
        operands = list(operands) + list(rider.ins)
    return pl.pallas_call(
        _hosted(body, n_in, n_out, rider, grid[0]), name=name, grid=grid, in_specs=in_specs, out_specs=out_specs,
        out_shape=out_shape, scratch_shapes=scratch,
        compiler_params=pltpu.CompilerParams(dimension_semantics=("arbitrary",), vmem_limit_bytes=VMEM_LIMIT),
    )(*operands)


def _delta_fwd(qkv, gb, nb, rider=None):
    T = qkv.shape[0]
    nb = min(nb, T // CHUNK)
    rows = nb * CHUNK
    n = T // rows

    def body(q_ref, k_ref, v_ref, gb_ref, o_ref, ss_ref, t_ref, s_scr):
        @pl.when(pl.program_id(0) == 0)
        def _():
            s_scr[...] = jnp.zeros_like(s_scr)

        Ss = [s_scr[h] for h in range(HEADS)]
        for h in range(HEADS):
            ss_ref[0, h] = Ss[h]
        gbs = [gb_ref[c * CHUNK:(c + 1) * CHUNK, :] for c in range(nb)]
        kept = []
        outs, new_S = _delta_chunks(_split_chunks(q_ref, nb), _split_chunks(k_ref, nb), _split_chunks(v_ref, nb), gbs, Ss,
                                    keep=kept)
        o_ref[...] = _join_chunks(outs)
        s_scr[...] = jnp.stack(new_S)
        for c in range(nb):
            for h in range(HEADS):
                t_ref[c, h] = kept[c * HEADS + h]

    row = lambda w, j=0: pl.BlockSpec((rows, w), lambda i: (i, j))
    return _host_call(
        body, "delta_fwd", (n,),
        [row(D_DN, 0), row(D_DN, 1), row(D_DN, 2), row(128)],
        [row(D_DN), pl.BlockSpec((1, HEADS, HD, HD), lambda i: (i, 0, 0, 0)),
         pl.BlockSpec((nb, HEADS, CHUNK, CHUNK), lambda i: (i, 0, 0, 0))],
        [jax.ShapeDtypeStruct((T, D_DN), F32), jax.ShapeDtypeStruct((n, HEADS, HD, HD), F32),
         jax.ShapeDtypeStruct((T // CHUNK, HEADS, CHUNK, CHUNK), F32)],
        [pltpu.VMEM((HEADS, HD, HD), F32)], [qkv, qkv, qkv, gb], rider)


def _delta_bwd(qkv, gb, ss, inv, do, nb, rider=None):
    T = qkv.shape[0]
    nb = min(nb, T // CHUNK)
    rows = nb * CHUNK
    n = T // rows

    def body(q_ref, k_ref, v_ref, gb_ref, ss_ref, t_ref, do_ref, d_ref, ds_scr):
        @pl.when(pl.program_id(0) == 0)
        def _():
            ds_scr[...] = jnp.zeros_like(ds_scr)

        Ss = [ss_ref[0, h] for h in range(HEADS)]
        gbs = [gb_ref[c * CHUNK:(c + 1) * CHUNK, :] for c in range(nb)]
        Ts = [t_ref[c, h] for c in range(nb) for h in range(HEADS)]
        _, vjp = jax.vjp(functools.partial(_delta_chunks, Ts=Ts), _split_chunks(q_ref, nb), _split_chunks(k_ref, nb),
                         _split_chunks(v_ref, nb), gbs, Ss)
        dqs, dks, dvs, dgbs, dSs = vjp((_split_chunks(do_ref, nb), [ds_scr[h] for h in range(HEADS)]))
        d_ref[...] = jnp.concatenate([_join_chunks(dqs), _join_chunks(dks), _join_chunks(dvs),
                                      jnp.concatenate(dgbs, axis=0)], axis=1)
        ds_scr[...] = jnp.stack(dSs)

    row = lambda w, j=0: pl.BlockSpec((rows, w), lambda i: (n - 1 - i, j))
    return _host_call(
        body, "delta_bwd", (n,),
        [row(D_DN, 0), row(D_DN, 1), row(D_DN, 2), row(128),
         pl.BlockSpec((1, HEADS, HD, HD), lambda i: (n - 1 - i, 0, 0, 0)),
         pl.BlockSpec((nb, HEADS, CHUNK, CHUNK), lambda i: (n - 1 - i, 0, 0, 0)), row(D_DN)],
        [row(3 * D_DN + 128)], [jax.ShapeDtypeStruct((T, 3 * D_DN + 128), F32)],
        [pltpu.VMEM((HEADS, HD, HD), F32)], [qkv, qkv, qkv, gb, ss, inv, do], rider)


def _cols(arr, width, index, first_row=0, block_row=None):
    return (arr, width, index, first_row, block_row)


def _rowwise(name, fn, tiled, consts, out_tiled, out_acc, tm, rows=None, rider=None, layer=None, prev=None):
    tiled = [t if isinstance(t, tuple) else (t, t.shape[-1], 0, 0, None) for t in tiled]
    T = tiled[0][0].shape[-2] if rows is None else rows
    tm = min(tm, T)
    assert T % tm == 0 and all(t[3] % tm == 0 for t in tiled)
    n_t, n_c, n_o, n_a = len(tiled), len(consts), len(out_tiled), len(out_acc)

    n_ri = len(rider.ins) if rider else 0
    n_ro = len(rider.outs) if rider else 0
    n_steps = T // tm

    def body(*refs):
        n_in = n_t + n_c + n_ri + (n_o if layer is not None else 0)
        r_ins = refs[n_t + n_c:n_t + n_c + n_ri]
        o_refs = refs[n_in:n_in + n_o]
        a_refs = refs[n_in + n_o:n_in + n_o + n_a]
        r_outs = refs[n_in + n_o + n_a:n_in + n_o + n_a + n_ro]
        sems = refs[n_in + n_o + n_a + n_ro:]
        if rider:
            @pl.when(pl.program_id(0) == 0)
            def _():
                rider.start(r_ins, r_outs, *sems)

        ins = [r[...] for r in refs[:n_t + n_c]]
        outs = fn(*ins)
        for r, val in zip(o_refs, outs[:n_o]):
            r[...] = val.astype(r.dtype)
        if n_a:
            @pl.when(pl.program_id(0) == 0)
            def _():
                for r in a_refs:
                    r[...] = jnp.zeros_like(r)
            for r, val in zip(a_refs, outs[n_o:]):
                r[...] += val
        if rider:
            @pl.when(pl.program_id(0) == n_steps - 1)
            def _():
                rider.finish(r_ins, r_outs, *sems)

    def const_spec(a):
        nd = a.ndim
        return pl.BlockSpec(a.shape, lambda i: (0,) * nd, pipeline_mode=pl.Buffered(1))

    def tile_spec(arr, w, j, r0, block_row):
        row = block_row if block_row is not None else (lambda i: i + r0 // tm)
        if arr.ndim == 3:
            return pl.BlockSpec((None, tm, w), lambda i: (layer, row(i), j))
        return pl.BlockSpec((tm, w), lambda i: (row(i), j))

    in_specs = [tile_spec(*t) for t in tiled]
    in_specs += [const_spec(a) for a in consts]
    if layer is None:
        out_specs = [pl.BlockSpec((tm, w), lambda i: (i, 0)) for (w, _) in out_tiled]
        out_shape = [jax.ShapeDtypeStruct((T, w), dt) for (w, dt) in out_tiled]
    else:
        out_specs = [pl.BlockSpec((None, tm, w), lambda i: (layer, i, 0)) for (w, _) in out_tiled]
        out_shape = [jax.ShapeDtypeStruct((DEPTH, T, w), dt) for (w, dt) in out_tiled]
    out_specs += [pl.BlockSpec(s, lambda i: (0, 0)) for (s, _) in out_acc]
    out_shape += [jax.ShapeDtypeStruct(s, dt) for (s, dt) in out_acc]
    operands = [t[0] for t in tiled] + list(consts)
    scratch = []
    aliases = {}
    if rider:
        anyspec = pl.BlockSpec(memory_space=pl.ANY)
        in_specs += [anyspec] * n_ri
        out_specs += [anyspec] * n_ro
        out_shape += list(rider.outs)
        operands += list(rider.ins)
        scratch = rider.scratch()
    n_prev = 0
    if layer is not None:
        assert rider is None and not out_acc
        if prev is None:
            prev = [jnp.zeros(o.shape, o.dtype) for o in out_shape]
        n_prev = len(prev)
        aliases = {len(operands) + i: i for i in range(n_prev)}
        in_specs += [pl.BlockSpec(memory_space=pl.ANY)] * n_prev
        operands += list(prev)
    return pl.pallas_call(
        body, name=name, grid=(n_steps,), in_specs=in_specs, out_specs=out_specs, out_shape=out_shape, scratch_shapes=scratch,
        input_output_aliases=aliases,
        compiler_params=pltpu.CompilerParams(dimension_semantics=("arbitrary",), vmem_limit_bytes=VMEM_LIMIT),
    )(*operands)


def _colsum(x):
    return jnp.sum(x, axis=0, keepdims=True)


def _rms(x):
    r = lax.rsqrt(jnp.mean(x * x, axis=-1, keepdims=True) + EPS)
    return x * r, r


def _rms_bwd(dxn, xn, r):
    return r * (dxn - xn * jnp.mean(dxn * xn, axis=-1, keepdims=True))


def _normproj_fwd(name, x, sh, sc, g, w, tm, rider=None, w_t=False, out_dtype=F32):
    def fn(x, sh, sc, g, w):
        xn, _ = _rms(x)
        h = (xn * (g * (1.0 + sc)) + sh).astype(BF16)
        return lax.dot_general(h, w, (((1,), (1 if w_t else 0,)), ((), ())), preferred_element_type=F32), h

    return _rowwise(name, fn, [x], [sh, sc, g, w], [(w.shape[0 if w_t else 1], out_dtype), (D, BF16)], [], tm, rider=rider)


def _normproj_bwd(name, x, dpre, dres, sc, g, w, tm, rider=None, w_t=False):
    def fn(x, dpre, dres, sc, g, w):
        xn, r = _rms(x)
        dh = lax.dot_general(dpre, w, (((1,), (0 if w_t else 1,)), ((), ())), preferred_element_type=F32)
        da = _colsum(dh * xn)
        dx = _rms_bwd(dh * (g * (1.0 + sc)), xn, r) + dres
        return dx, _colsum(dh), da * g, da * (1.0 + sc)

    vec = ((1, D), F32)
    return _rowwise(name, fn, [x, dpre, dres], [sc, g, w], [(D, F32)], [vec, vec, vec], tm, rider=rider)


def _stage2(a_b, blk, cp, u1c, qp, kp, vp, bb, lg, lb, alog, dtb):
    y_a = a_b * cp
    u1 = u1c + bb
    mu = jnp.mean(u1, axis=-1, keepdims=True)
    uc = u1 - mu
    var = jnp.mean(uc * uc, axis=-1, keepdims=True)
    y_b = _silu(uc * lax.rsqrt(var + LN_EPS) * lg + lb)

    def l2(t):
        t = _silu(t)
        return t * lax.rsqrt(jnp.sum(t * t, axis=-1, keepdims=True) + EPS)

    q = [l2(t) for t in qp]
    k = [l2(t) for t in kp]
    v = _silu(vp)
    lane = _iota2(blk.shape, 1)
    gdec = -jnp.exp(alog) * _softplus(blk + dtb)
    gb = jnp.where(lane < HEADS, gdec, jnp.where(lane < 2 * HEADS, _sigmoid(blk), 0.0))
    return y_a, y_b, q, k, v, gb


def _heads_of(x, base=0):
    return [x[:, base + h * HD:base + (h + 1) * HD] for h in range(HEADS)]


def _stage2_fwd(proj, convout, bb, lg, lb, alog, dtb, tm):
    def fn(a_b, blk, co, bb, lg, lb, alog, dtb):
        y_a, y_b, q, k, v, gb = _stage2(a_b, blk, co[:, 0:256], co[:, 256:512], _heads_of(co, 512), _heads_of(co, 1024),
                                        co[:, 1536:2048], bb, lg, lb, alog, dtb)
        return jnp.concatenate([y_a, y_b], axis=1), jnp.concatenate(q + k + [v], axis=1), gb

    return _rowwise("stage2_fwd", fn, [_cols(proj, 256, 0), _cols(proj, 128, C_GB // 128), convout],
                    [bb, lg, lb, alog, dtb], [(512, BF16), (1536, F32), (128, F32)], [], tm)


def _stage2_bwd(proj, convout, dy_ab, dq, dk, dv, dgb, bb, lg, lb, alog, dtb, tm, rider=None):
    def fn(a_b, blk, co, dy_ab, dq, dk, dv, dgb, bb, lg, lb, alog, dtb):
        args = (a_b, blk, co[:, 0:256], co[:, 256:512], _heads_of(co, 512), _heads_of(co, 1024), co[:, 1536:2048],
                bb, lg, lb, alog, dtb)
        _, vjp = jax.vjp(_stage2, *args)
        ct = (dy_ab[:, 0:256], dy_ab[:, 256:512], _heads_of(dq), _heads_of(dk), dv, dgb)
        da_b, dblk, dcp, du1c, dqp, dkp, dvp, dbb, dlg, dlb, dalog, ddtb = vjp(ct)
        dco = jnp.concatenate([dcp, du1c] + dqp + dkp + [dvp], axis=1)
        return dco, da_b, dblk, dbb, dlg, dlb, dalog, ddtb

    v256, v128 = ((1, 256), F32), ((1, 128), F32)
    return _rowwise("stage2_bwd", fn,
                    [_cols(proj, 256, 0), _cols(proj, 128, C_GB // 128), convout, dy_ab, dq, dk, dv, dgb],
                    [bb, lg, lb, alog, dtb], [(N_CONVCOL, F32), (256, F32), (128, F32)],
                    [v256, v256, v256, v128, v128], tm, rider=rider)


def _stage3(o, z, dng):
    ys = []
    for oh, zh in zip(o, z):
        on = oh * lax.rsqrt(jnp.mean(oh * oh, axis=-1, keepdims=True) + EPS)
        ys.append(on * dng * _silu(zh))
    return ys


def _outproj_fwd(x, o, proj, y_ab, g1, dng, wout, tm, rider=None):
    def fn(x, o, z0, z1, z2, z3, y_ab, g1, dng, wout):
        y_c = _stage3(_heads_of(o), [z0, z1, z2, z3], dng)
        ycat = jnp.concatenate([y_ab] + [t.astype(BF16) for t in y_c], axis=1)
        mix = jnp.dot(ycat, wout, preferred_element_type=F32)
        return x + g1 * mix, mix, ycat

    return _rowwise("outproj_fwd", fn, [x, o] + _z_heads(proj) + [y_ab],
                    [g1, dng, wout], [(D, F32), (D, F32), (D, BF16)], [], tm, rider=rider)


def _z_heads(proj):
    return [_cols(proj, HD, C_Z // HD + h) for h in range(HEADS)]


def _outproj_bwd(dx1, mix, o, proj, g1, dng, wout, tm, rider=None):
    def fn(dx1, mix, o, z0, z1, z2, z3, g1, dng, wout):
        dmix = (dx1 * g1).astype(BF16)
        dycat = lax.dot_general(dmix, wout, (((1,), (1,)), ((), ())), preferred_element_type=F32)
        _, vjp = jax.vjp(_stage3, _heads_of(o), [z0, z1, z2, z3], dng)
        do, dz, ddng = vjp(_heads_of(dycat, 512))
        return (dmix, dycat[:, 0:512], jnp.concatenate(do, axis=1), jnp.concatenate(dz, axis=1),
                _colsum(dx1 * mix), ddng)

    return _rowwise("outproj_bwd", fn, [dx1, mix, o] + _z_heads(proj), [g1, dng, wout],
                    [(D, BF16), (512, F32), (512, F32), (512, F32)], [((1, D), F32), ((1, HD), F32)], tm, rider=rider)


_CONV_BLOCKS = ((0, 256, KA), (256, 512, KB), (512, 2048, KC))
_CONV_STRIP = 256


_CONV_ROWS = 32


def _conv_inputs(proj_ref, rows):
    a_c, a_v = proj_ref[rows, C_AC:C_AC + 256], proj_ref[rows, C_AV:C_AV + 256]
    b_a, b_g = proj_ref[rows, C_BA:C_BA + 256], proj_ref[rows, C_BG:C_BG + 256]
    return a_c, a_v, b_a, _sigmoid(b_g)


def _shifted_copies(ext, phases, tm):
    n = tm + HALO - 8
    for b in range(1, 8):
        phases[b - 1] = ext[pl.ds(b, n), 256:512]


def _rows_at(ext, phases, row, col, kw):
    a, b = divmod(row, 8)
    if kw == KB and b:
        return phases[b - 1, pl.ds(8 * a, _CONV_ROWS), :]
    return ext[pl.ds(row, _CONV_ROWS), col:col + _CONV_STRIP]


def _conv_fwd(proj, wa, wb, wc, tm):
    T = proj.shape[0]
    tm = min(tm, T)

    def body(proj_ref, wa_ref, wb_ref, wc_ref, out_ref, ext, phases):
        @pl.when(pl.program_id(0) == 0)
        def _():
            ext[0:HALO, :] = jnp.zeros((HALO, N_CONVCOL), F32)

        a_c, a_v, b_a, sg = _conv_inputs(proj_ref, slice(None))
        ext[HALO:HALO + tm, 0:256] = a_c * a_v
        ext[HALO:HALO + tm, 256:512] = b_a * sg
        ext[HALO:HALO + tm, 512:2048] = proj_ref[:, C_Q:C_Q + 1536]
        _shifted_copies(ext, phases, tm)
        for r0 in range(0, tm, _CONV_ROWS):
            for (c0, c1, kw), w_ref in zip(_CONV_BLOCKS, (wa_ref, wb_ref, wc_ref)):
                for s0 in range(c0, c1, _CONV_STRIP):
                    acc = jnp.zeros((_CONV_ROWS, _CONV_STRIP), F32)
                    for k in range(kw):
                        acc += (w_ref[k:k + 1, s0 - c0:s0 - c0 + _CONV_STRIP]
                                * _rows_at(ext, phases, r0 + HALO - (kw - 1) + k, s0, kw))
                    out_ref[r0:r0 + _CONV_ROWS, s0:s0 + _CONV_STRIP] = acc
        ext[0:HALO, :] = ext[tm:tm + HALO, :]

    full = lambda a: pl.BlockSpec(a.shape, lambda i: (0, 0))
    return pl.pallas_call(
        body, name="conv_fwd", grid=(T // tm,),
        in_specs=[pl.BlockSpec((tm, IN_PAD), lambda i: (i, 0)), full(wa), full(wb), full(wc)],
        out_specs=pl.BlockSpec((tm, N_CONVCOL), lambda i: (i, 0)),
        out_shape=jax.ShapeDtypeStruct((T, N_CONVCOL), F32),
        scratch_shapes=[pltpu.VMEM((HALO + tm, N_CONVCOL), F32), pltpu.VMEM((7, tm + HALO - 8, 256), F32)],
        compiler_params=pltpu.CompilerParams(dimension_semantics=("arbitrary",), vmem_limit_bytes=VMEM_LIMIT),
    )(proj, wa, wb, wc)


def _conv_bwd(proj, dco, da_b, dz, dblk, wa, wb, wc, tm, rider=None):
    T = proj.shape[0]
    tm = min(tm, T)
    n = T // tm

    def body(proj_ref, dco_ref, dab_ref, dz_ref, dblk_ref, wa_ref, wb_ref, wc_ref,
             dproj_ref, dwa_ref, dwb_ref, dwc_ref, ext, acc_a, acc_b, acc_c, phases):
        @pl.when(pl.program_id(0) == 0)
        def _():
            ext[tm:tm + HALO, :] = jnp.zeros((HALO, N_CONVCOL), F32)
            acc_a[...] = jnp.zeros_like(acc_a)
            acc_b[...] = jnp.zeros_like(acc_b)
            acc_c[...] = jnp.zeros_like(acc_c)

        ext[0:tm, :] = dco_ref[...]
        _shifted_copies(ext, phases, tm)

        def taps(w_ref, acc_ref, kw, c0, wc0, xin, r0):
            dx = jnp.zeros((_CONV_ROWS, _CONV_STRIP), F32)
            for k in range(kw):
                sh = _rows_at(ext, phases, r0 + kw - 1 - k, c0, kw)
                dx += w_ref[k:k + 1, wc0:wc0 + _CONV_STRIP] * sh
                pr = sh * xin
                part = pr[0:8]
                for g in range(8, _CONV_ROWS, 8):
                    part += pr[g:g + 8]
                acc_ref[8 * k:8 * k + 8, wc0:wc0 + _CONV_STRIP] += part
            return dx

        for r0 in range(0, tm, _CONV_ROWS):
            rows = slice(r0, r0 + _CONV_ROWS)
            a_c, a_v, b_a, sg = _conv_inputs(proj_ref, rows)
            dp = taps(wa_ref, acc_a, KA, 0, 0, a_c * a_v, r0)
            dproj_ref[rows, C_AC:C_AC + 256] = (dp * a_v).astype(BF16)
            dproj_ref[rows, C_AV:C_AV + 256] = (dp * a_c).astype(BF16)
            du0 = taps(wb_ref, acc_b, KB, 256, 0, b_a * sg, r0)
            dproj_ref[rows, C_BA:C_BA + 256] = (du0 * sg).astype(BF16)
            dproj_ref[rows, C_BG:C_BG + 256] = (du0 * b_a * sg * (1.0 - sg)).astype(BF16)
            for s0 in range(0, 1536, _CONV_STRIP):
                dq = taps(wc_ref, acc_c, KC, 512 + s0, s0, proj_ref[rows, C_Q + s0:C_Q + s0 + _CONV_STRIP], r0)
                dproj_ref[rows, C_Q + s0:C_Q + s0 + _CONV_STRIP] = dq.astype(BF16)
        ext[tm:tm + HALO, :] = ext[0:HALO, :]

        dproj_ref[:, C_AB:C_AB + 256] = dab_ref[...].astype(BF16)
        dproj_ref[:, C_Z:C_Z + 512] = dz_ref[...].astype(BF16)
        dproj_ref[:, C_GB:C_GB + 128] = dblk_ref[...].astype(BF16)

        @pl.when(pl.program_id(0) == n - 1)
        def _():
            for acc_ref, dw_ref, kw in ((acc_a, dwa_ref, KA), (acc_b, dwb_ref, KB), (acc_c, dwc_ref, KC)):
                dw_ref[...] = jnp.zeros_like(dw_ref)
                for k in range(kw):
                    dw_ref[k:k + 1, :] = _colsum(acc_ref[8 * k:8 * k + 8, :])

    rev = lambda w: pl.BlockSpec((tm, w), lambda i: (n - 1 - i, 0))
    full = lambda a: pl.BlockSpec(a.shape, lambda i: (0, 0))
    return _host_call(
        body, "conv_bwd", (n,),
        [rev(IN_PAD), rev(N_CONVCOL), rev(256), rev(512), rev(128), full(wa), full(wb), full(wc)],
        [rev(IN_PAD), full(wa), full(wb), full(wc)],
        [jax.ShapeDtypeStruct((T, IN_PAD), BF16), jax.ShapeDtypeStruct(wa.shape, F32),
         jax.ShapeDtypeStruct(wb.shape, F32), jax.ShapeDtypeStruct(wc.shape, F32)],
        [pltpu.VMEM((tm + HALO, N_CONVCOL), F32), pltpu.VMEM((8 * KA, 256), F32),
         pltpu.VMEM((8 * KB, 256), F32), pltpu.VMEM((8 * KC, 1536), F32), pltpu.VMEM((7, tm + HALO - 8, 256), F32)],
        [proj, dco, da_b, dz, dblk, wa, wb, wc], rider)


def _ffn_out_fwd(x1, gu, g2, wfo, tm, rider=None):
    def fn(x1, gu, g2, wfo):
        s = (_silu(gu[:, :D_FF].astype(F32)) * gu[:, D_FF:].astype(F32)).astype(BF16)
        f = jnp.dot(s, wfo, preferred_element_type=F32)
        return x1 + g2 * f, f

    return _rowwise("ffnout_fwd", fn, [x1, gu], [g2, wfo], [(D, F32), (D, F32)], [], tm, rider=rider)


def _ffn_out_bwd(dx2, gu, f, g2, wfo, tm, rider=None):
    def fn(dx2, gu, f, g2, wfo):
        df = (dx2 * g2).astype(BF16)
        dgates, dups, ss = [], [], []
        for c0 in range(0, D_FF, D_FF // 2):
            c1 = c0 + D_FF // 2
            gate, up = gu[:, c0:c1].astype(F32), gu[:, D_FF + c0:D_FF + c1].astype(F32)
            sg = _sigmoid(gate)
            sl = gate * sg
            ds = lax.dot_general(df, wfo[c0:c1], (((1,), (1,)), ((), ())), preferred_element_type=F32)
            dgates.append((ds * up * (sg * (1.0 + gate * (1.0 - sg)))).astype(BF16))
            dups.append((ds * sl).astype(BF16))
            ss.append((sl * up).astype(BF16))
        return jnp.concatenate(dgates + dups, axis=1), jnp.concatenate(ss, axis=1), df, _colsum(dx2 * f)

    return _rowwise("ffnout_bwd", fn, [dx2, gu, f], [g2, wfo], [(2 * D_FF, BF16), (D_FF, BF16), (D, BF16)],
                    [((1, D), F32)], tm, rider=rider)


def _loss_bwd(x, tgt, gfin, tm):
    def fn(x, tgt, gfin):
        xn, r = _rms(x)
        e = xn * gfin - tgt
        loss = 0.5 * jnp.sum(jnp.mean(e * e, axis=-1, keepdims=True), axis=0, keepdims=True)
        dy = e * (1.0 / D)
        return _rms_bwd(dy * gfin, xn, r), jnp.broadcast_to(loss, (1, 128)), _colsum(dy * xn)

    return _rowwise("loss_bwd", fn, [x, tgt], [gfin], [(D, F32)], [((1, 128), F32), ((1, D), F32)], tm)


def _wgrad(name, a, b, bm, bk):
    T, M = a.shape
    N = b.shape[1]
    bk = min(bk, T)

    def body(a_ref, b_ref, o_ref):
        @pl.when(pl.program_id(1) == 0)
        def _():
            o_ref[...] = jnp.zeros_like(o_ref)

        o_ref[...] += lax.dot_general(a_ref[...], b_ref[...], (((0,), (0,)), ((), ())), preferred_element_type=F32)

    return pl.pallas_call(
        body, name=name, grid=(M // bm, T // bk),
        in_specs=[pl.BlockSpec((bk, bm), lambda i, k: (k, i)), pl.BlockSpec((bk, N), lambda i, k: (k, 0))],
        out_specs=pl.BlockSpec((bm, N), lambda i, k: (i, 0)),
        out_shape=jax.ShapeDtypeStruct((M, N), F32),
        compiler_params=pltpu.CompilerParams(dimension_semantics=("arbitrary", "arbitrary"), vmem_limit_bytes=VMEM_LIMIT),
    )(a, b)


IN_BLOCK = 512


def _wgrad_in_blocks(name, a, b, bm, bk):
    T, M = a.shape
    N = b.shape[1]
    bk = min(bk, T)
    nk = T // bk
    per = IN_COLS // N_DEV
    win = IN_BLOCK + 128

    def body(a_ref, b_ref, o_ref, acc):
        @pl.when(pl.program_id(1) == 0)
        def _():
            acc[...] = jnp.zeros_like(acc)

        acc[...] += lax.dot_general(a_ref[...], b_ref[...], (((0,), (0,)), ((), ())), preferred_element_type=F32)

        @pl.when(pl.program_id(1) == nk - 1)
        def _():
            for j in range(N_DEV):
                q, r = divmod(per * j, 128)
                w = acc[:, 128 * q:128 * q + win]
                if r:
                    w = pltpu.roll(w, win - r, axis=1)
                o_ref[j] = w[:, :IN_BLOCK]

    assert 128 * ((per * (N_DEV - 1)) // 128) + win <= N
    return pl.pallas_call(
        body, name=name, grid=(M // bm, nk),
        in_specs=[pl.BlockSpec((bk, bm), lambda i, k: (k, i)), pl.BlockSpec((bk, N), lambda i, k: (k, 0))],
        out_specs=pl.BlockSpec((N_DEV, bm, IN_BLOCK), lambda i, k: (0, i, 0)),
        out_shape=jax.ShapeDtypeStruct((N_DEV, M, IN_BLOCK), F32),
        scratch_shapes=[pltpu.VMEM((bm, N), F32)],
        compiler_params=pltpu.CompilerParams(dimension_semantics=("arbitrary", "arbitrary"), vmem_limit_bytes=VMEM_LIMIT),
    )(a, b)


def _adamw(w, g, m, v):
    m = B1 * m + (1.0 - B1) * g
    v = B2 * v + (1.0 - B2) * (g * g)
    m_hat = m / (1.0 - B1 ** STEP)
    v_hat = v / (1.0 - B2 ** STEP)
    return -LR * (m_hat / (jnp.sqrt(v_hat) + AEPS) + WD * w), m, v


def _adam_call(name, w, g, m, v, tm):
    C = w.shape[1]
    return _rowwise(name, _adamw, [w, g, m, v], [], [(C, F32)] * 3, [], tm)


def _adam_layer(name, g, w, m, v, tm, layer, prev):
    C = g.shape[1]
    return _rowwise(name, lambda g, w, m, v: (g,) + _adamw(w, g, m, v), [g, w, m, v], [], [(C, F32)] * 4, [], tm,
                    layer=layer, prev=prev)


def _reduce_sum(name, own, recv, tm):
    n, R, C = recv.shape
    flat = recv.reshape(n * R, C)
    fn = lambda a, b, r0, r1, r2: ((((a + b) + r0.astype(F32)) + r1.astype(F32)) + r2.astype(F32),)
    return _rowwise(name, fn, _own_blocks(own, R, tm) + [_cols(flat, C, 0, j * R) for j in range(n)], [], [(C, F32)], [],
                    tm, rows=R)[0]


def _pair_add(name, g, from_sib, tm):
    n, R, C = from_sib.shape
    n_r = R // tm
    mine = _cols(g.reshape(N_DEV * R, C), C, 0,
                 block_row=lambda i: (2 * (i // n_r) + lax.axis_index("c")) * n_r + i % n_r)

    return _rowwise(name, lambda a, b: (a + b,), [mine, from_sib.reshape(n * R, C)], [], [(C, BF16)], [], tm, rows=n * R)[0]


def _own_blocks(own, R, tm):
    g, from_sib = own
    C = g.shape[-1]
    n_r = R // tm
    chip = lambda: 2 * lax.axis_index("x") + lax.axis_index("y")
    return [_cols(g.reshape(N_DEV * R, C), C, 0, block_row=lambda i: (2 * chip() + lax.axis_index("c")) * n_r + i),
            _cols(from_sib.reshape(4 * R, C), C, 0, block_row=lambda i: chip() * n_r + i)]


def _reduce_adam(name, own, recv, w, m, v, tm, layer, prev):
    n, R, C = recv.shape
    flat = recv.reshape(n * R, C)

    def fn(a, b, r0, r1, r2, w, m, v):
        g = (((a + b) + r0.astype(F32)) + r1.astype(F32)) + r2.astype(F32)
        return (g,) + _adamw(w, g, m, v)

    return _rowwise(name, fn, _own_blocks(own, R, tm) + [_cols(flat, C, 0, j * R) for j in range(n)] + [w, m, v], [],
                    [(C, F32)] * 4, [], tm, rows=R, layer=layer, prev=prev)


_OFFSETS = [(dx, dy, dc) for dx in (0, 1) for dy in (0, 1) for dc in (0, 1)][1:]
_MESH = pl.DeviceIdType.MESH


def _coords():
    return lax.axis_index("x"), lax.axis_index("y"), lax.axis_index("c")


def _flip(me, off):
    return tuple((1 - m) if d else m for m, d in zip(me, off))


def _linear(p):
    return 4 * p[0] + 2 * p[1] + p[2]


_CHIP_FLIPS = ((1, 0), (0, 1), (1, 1))


class _Rider:
    def __init__(self, ins, outs, n_remote, n_local, start, finish):
        self.ins, self.outs, self.n_remote, self.n_local, self.start, self.finish = ins, outs, n_remote, n_local, start, finish

    def scratch(self):
        return [pltpu.SemaphoreType.DMA((self.n_remote,)), pltpu.SemaphoreType.DMA((self.n_remote,)),
                pltpu.SemaphoreType.DMA((max(self.n_local, 1),))]


def _run_rider(name, rider):
    def body(*refs):
        n_i, n_o = len(rider.ins), len(rider.outs)
        rider.start(refs[:n_i], refs[n_i:n_i + n_o], *refs[n_i + n_o:])
        rider.finish(refs[:n_i], refs[n_i:n_i + n_o], *refs[n_i + n_o:])

    anyspec = pl.BlockSpec(memory_space=pl.ANY)
    return pl.pallas_call(
        body, name=name, in_specs=[anyspec] * len(rider.ins), out_specs=[anyspec] * len(rider.outs),
        out_shape=list(rider.outs), scratch_shapes=rider.scratch(),
    )(*rider.ins)


def _gather_rider(arrs, layer):
    n = len(arrs)

    def parts(ins, outs, send, recv, loc):
        x, y, c = _coords()
        me, sib = (x, y, c), (x, y, 1 - c)
        chips = [((1 - x) if dx else x, (1 - y) if dy else y) for dx, dy in _CHIP_FLIPS]

        def copy(a, k, block, to, own=False):
            slot = outs[a].at[_linear(block)]
            return pltpu.make_async_remote_copy(src_ref=ins[a].at[layer] if own else slot, dst_ref=slot,
                                                send_sem=send.at[a * 7 + k], recv_sem=recv.at[a * 7 + k],
                                                device_id=to, device_id_type=_MESH)

        local = [pltpu.make_async_copy(ins[a].at[layer], outs[a].at[_linear(me)], loc.at[a]) for a in range(n)]
        first = []
        for a in range(n):
            first.append(copy(a, 0, me, sib, own=True))
            first += [copy(a, 1 + j, me, (*chip, c), own=True) for j, chip in enumerate(chips)]
        return copy, local, first, me, sib, chips, c

    def start(*refs):
        _, local, first, *_ = parts(*refs)
        for cp in local + first:
            cp.start()

    def finish(*refs):
        copy, local, first, me, sib, chips, c = parts(*refs)
        passed = []
        for j, chip in enumerate(chips):
            for a in range(n):
                copy(a, 1 + j, (*chip, c), me).wait_recv()
                cp = copy(a, 4 + j, (*chip, c), sib)
                cp.start()
                passed.append(cp)
        for a in range(n):
            copy(a, 0, sib, me).wait_recv()
            for j, chip in enumerate(chips):
                copy(a, 4 + j, (*chip, 1 - c), me).wait_recv()
        for cp in first + passed:
            cp.wait_send()
        for cp in local:
            cp.wait()

    outs = [jax.ShapeDtypeStruct((N_DEV,) + a.shape[1:], a.dtype) for a in arrs]
    return _Rider(list(arrs), outs, 7 * n, n, start, finish)


def _simple_rider(ins, outs, n_remote, make):
    def start(*refs):
        for cp in make(*refs):
            cp.start()

    def finish(*refs):
        for cp in make(*refs):
            cp.wait()

    return _Rider(ins, outs, n_remote, 0, start, finish)


def _join_riders(a, b):
    assert b.sem0 == a.n_remote and a.n_local == 0 and b.n_local == 0
    n_i, n_o = len(a.ins), len(a.outs)

    def both(fa, fb):
        def run(ins, outs, send, recv, loc):
            fa(ins[:n_i], outs[:n_o], send, recv, loc)
            fb(ins[n_i:], outs[n_o:], send, recv, loc)
        return run

    return _Rider(a.ins + b.ins, a.outs + b.outs, a.n_remote + b.n_remote, 0, both(a.start, b.start), both(a.finish, b.finish))


def _pair_rider(arrs, sem0=0):
    def make(ins, outs, send, recv, loc):
        x, y, c = _coords()
        return [pltpu.make_async_remote_copy(src_ref=ins[a].at[2 * s_ + 1 - c], dst_ref=outs[a].at[s_],
                                             send_sem=send.at[sem0 + 4 * a + s_], recv_sem=recv.at[sem0 + 4 * a + s_],
                                             device_id=(x, y, 1 - c), device_id_type=_MESH)
                for a in range(len(arrs)) for s_ in range(4)]

    r = _simple_rider(list(arrs), [jax.ShapeDtypeStruct((4,) + a.shape[1:], a.dtype) for a in arrs], 4 * len(arrs), make)
    r.sem0 = sem0
    return r


def _chip_rider(arrs, sem0=0):
    nf = len(_CHIP_FLIPS)

    def make(ins, outs, send, recv, loc):
        x, y, c = _coords()
        copies = []
        for a in range(len(arrs)):
            for k, (dx, dy) in enumerate(_CHIP_FLIPS):
                px, py = (1 - x) if dx else x, (1 - y) if dy else y
                copies.append(pltpu.make_async_remote_copy(
                    src_ref=ins[a].at[2 * px + py], dst_ref=outs[a].at[k], send_sem=send.at[sem0 + a * nf + k],
                    recv_sem=recv.at[sem0 + a * nf + k], device_id=(px, py, c), device_id_type=_MESH))
        return copies

    r = _simple_rider(list(arrs), [jax.ShapeDtypeStruct((nf,) + a.shape[1:], a.dtype) for a in arrs], nf * len(arrs), make)
    r.sem0 = sem0
    return r


def _small_allgather(name, packed, rider=None):
    R = packed.shape[0]
    n_ri = len(rider.ins) if rider else 0
    n_ro = len(rider.outs) if rider else 0

    def body(*refs):
        in_ref, r_ins = refs[0], refs[1:1 + n_ri]
        all_ref, sum_ref = refs[1 + n_ri:3 + n_ri]
        r_outs = refs[3 + n_ri:3 + n_ri + n_ro]
        send, recv = refs[3 + n_ri + n_ro:5 + n_ri + n_ro]
        r_sems = refs[5 + n_ri + n_ro:]
        if rider:
            rider.start(r_ins, r_outs, *r_sems)
        me = _coords()
        my = _linear(me)
        all_ref[my] = in_ref[...]
        copies = []
        for k, off in enumerate(_OFFSETS):
            cp = pltpu.make_async_remote_copy(src_ref=in_ref, dst_ref=all_ref.at[my], send_sem=send.at[k], recv_sem=recv.at[k],
                                              device_id=_flip(me, off), device_id_type=_MESH)
            cp.start()
            copies.append(cp)
        for cp in copies:
            cp.wait()
        acc = all_ref[0]
        for j in range(1, N_DEV):
            acc = acc + all_ref[j]
        sum_ref[...] = acc
        if rider:
            rider.finish(r_ins, r_outs, *r_sems)

    vm = pl.BlockSpec(memory_space=pltpu.VMEM)
    anyspec = pl.BlockSpec(memory_space=pl.ANY)
    return pl.pallas_call(
        body, name=name, in_specs=[vm] + [anyspec] * n_ri, out_specs=[vm, vm] + [anyspec] * n_ro,
        out_shape=[jax.ShapeDtypeStruct((N_DEV, R, 128), F32), jax.ShapeDtypeStruct((R, 128), F32)] + (list(rider.outs) if rider else []),
        scratch_shapes=[pltpu.SemaphoreType.DMA((len(_OFFSETS),)), pltpu.SemaphoreType.DMA((len(_OFFSETS),))]
        + (rider.scratch() if rider else []),
        compiler_params=pltpu.CompilerParams(vmem_limit_bytes=VMEM_LIMIT),
    )(packed, *(rider.ins if rider else []))


def _ada_mod(c, w16, bias):
    nc = w16.shape[2]
    kp = len(_OFFSETS)

    def body(c_ref, w_ref, b_ref, rows_ref, act_ref, cbuf, sbuf, send, recv):
        me = _coords()
        my = _linear(me)
        cbuf[my] = c_ref[...]
        copies = []
        for k, off in enumerate(_OFFSETS):
            cp = pltpu.make_async_remote_copy(src_ref=c_ref, dst_ref=cbuf.at[my], send_sem=send.at[k], recv_sem=recv.at[k],
                                              device_id=_flip(me, off), device_id_type=_MESH)
            cp.start()
            copies.append(cp)
        for cp in copies:
            cp.wait()
        act = _silu(jnp.concatenate([cbuf[j] for j in range(N_DEV)], axis=0))
        act_ref[...] = act
        act16 = act.astype(BF16)
        for l in range(DEPTH):
            ml = jnp.dot(act16, w_ref[l], preferred_element_type=F32) + b_ref[l:l + 1, :]
            for j in range(N_DEV):
                sbuf[j, l:l + 1, :] = ml[j:j + 1, :]
        rows_ref[my] = sbuf[my]
        copies = []
        for k, off in enumerate(_OFFSETS):
            peer = _flip(me, off)
            cp = pltpu.make_async_remote_copy(src_ref=sbuf.at[_linear(peer)], dst_ref=rows_ref.at[my], send_sem=send.at[kp + k],
                                              recv_sem=recv.at[kp + k], device_id=peer, device_id_type=_MESH)
            cp.start()
            copies.append(cp)
        for cp in copies:
            cp.wait()

    vm = pl.BlockSpec(memory_space=pltpu.VMEM)
    return pl.pallas_call(
        body, name="ada_mod", in_specs=[vm, vm, vm], out_specs=[vm, vm],
        out_shape=[jax.ShapeDtypeStruct((N_DEV, DEPTH, nc), F32), jax.ShapeDtypeStruct((N_DEV, D), F32)],
        scratch_shapes=[pltpu.VMEM((N_DEV, 1, D), F32), pltpu.VMEM((N_DEV, DEPTH, nc), F32),
                        pltpu.SemaphoreType.DMA((2 * kp,)), pltpu.SemaphoreType.DMA((2 * kp,))],
        compiler_params=pltpu.CompilerParams(vmem_limit_bytes=VMEM_LIMIT),
    )(c, w16, bias)


def _pack(arrs):
    parts = []
    for a in arrs:
        f = a.reshape(-1).astype(F32)
        parts.append(jnp.pad(f, (0, (-f.shape[0]) % 128)))
    flat = jnp.concatenate(parts)
    flat = jnp.pad(flat, (0, (-flat.shape[0]) % 1024))
    return flat.reshape(-1, 128)


def _unpack(packed, shapes):
    flat = packed.reshape(packed.shape[:-2] + (-1,))
    out, r = [], 0
    for s in shapes:
        n = int(np.prod(s))
        out.append(flat[..., r:r + n].reshape(packed.shape[:-2] + tuple(s)))
        r += -(-n // 128) * 128
    return out


def _pad_rows(w, rows):
    return jnp.pad(w, ((0, 0), (0, rows - w.shape[1]), (0, 0)))


_SMALL = ("b_ada", "norm_mix_g", "norm_ffn_g", "conv_a_w", "conf_dw_w", "conf_dw_b", "conf_ln_g", "conf_ln_b",
          "dn_conv_w", "dn_a_log", "dn_dt_bias", "dn_norm_g", "final_norm_g")
_BIG = ("w_in", "w_out", "w_ffn_in", "w_ffn_out")
_WEIGHTS = ("w_ada", "b_ada", "norm_mix_g", "norm_ffn_g", "w_in", "conv_a_w", "conf_dw_w", "conf_dw_b", "conf_ln_g",
            "conf_ln_b", "dn_conv_w", "dn_a_log", "dn_dt_bias", "dn_norm_g", "w_out", "w_ffn_in", "w_ffn_out",
            "final_norm_g")


def _step(x, c, loss_target, W, M, V):
    T = x.shape[1]
    me = _linear(_coords())
    xs, tgt = x[0], loss_target[0]
    vec = lambda a: a.reshape(1, -1)

    nada = W["w_ada"].shape[2]
    rows, act_all = _ada_mod(c, W["w_ada"].astype(BF16), lax.dynamic_slice(W["b_ada"], (0, me * nada), (DEPTH, nada)))
    mod = rows.transpose(1, 0, 2).reshape(DEPTH, 6, 1, D)

    w16 = {k: W[k].astype(BF16) for k in _BIG}
    w16["w_ffn_in"] = W["w_ffn_in"].transpose(0, 2, 1).astype(BF16)

    def whole(g_in=None, g_out=None, g_fi=None, g_fo=None):
        out = {}
        if g_in is not None:
            out["w_in"] = jnp.pad(g_in.transpose(1, 0, 2).reshape(D, IN_COLS), ((0, 0), (0, IN_PAD - IN_COLS)))
        if g_out is not None:
            out["w_out"] = g_out.reshape(D, D)
        if g_fi is not None:
            out["w_ffn_in"] = g_fi.reshape(2 * D_FF, D)
        if g_fo is not None:
            out["w_ffn_out"] = g_fo.reshape(D_FF, D)
        return out

    wts = [dict() for _ in range(DEPTH)]
    gather = lambda names, layer: _gather_rider([w16[k] for k in names], layer)
    conv_names = ("conv_a_w", "conf_dw_w", "dn_conv_w")
    conv_all, _, g_in0 = _small_allgather("gather_conv_w", _pack([W[k] for k in conv_names]), rider=gather(["w_in"], 0))
    wts[0].update(whole(g_in=g_in0))
    conv_full = [t.transpose(1, 2, 0, 3).reshape(t.shape[1], t.shape[2], -1)
                 for t in _unpack(conv_all, [W[k].shape for k in conv_names])]
    wa, wb, wc = _pad_rows(conv_full[0], 8), _pad_rows(conv_full[1], 32), _pad_rows(conv_full[2], 8)
    lane_pad = lambda a: jnp.pad(a, ((0, 0), (0, 128 - a.shape[1])))
    alog, dtb = lane_pad(W["dn_a_log"]), lane_pad(W["dn_dt_bias"])

    saved = []
    xc = xs
    for l in range(DEPTH):
        more = l + 1 < DEPTH
        sh1, sc1, g1, sh2, sc2, g2 = [mod[l, i] for i in range(6)]
        proj, h, *got = _normproj_fwd("inproj_fwd", xc, sh1, sc1, vec(W["norm_mix_g"][l]), wts[l]["w_in"], 512,
                                      rider=gather(["w_out", "w_ffn_out"], 0) if l == 0 else None)
        if l == 0:
            wts[0].update(whole(g_out=got[0], g_fo=got[1]))
        convout = _conv_fwd(proj, wa[l], wb[l], wc[l], 512)
        y_ab, qkv, gb = _stage2_fwd(proj, convout, vec(W["conf_dw_b"][l]), vec(W["conf_ln_g"][l]), vec(W["conf_ln_b"][l]),
                                    alog[l:l + 1], dtb[l:l + 1], 512)
        o, ss, inv, *got = _delta_fwd(qkv, gb, DELTA_NB,
                                 rider=gather(["w_ffn_in"], 0) if l == 0 else gather(["w_in"], l + 1) if more else None)
        if l == 0:
            wts[0].update(whole(g_fi=got[0]))
        elif more:
            wts[l + 1].update(whole(g_in=got[0]))
        x1, mix, ycat, *got = _outproj_fwd(xc, o, proj, y_ab, g1, vec(W["dn_norm_g"][l]), wts[l]["w_out"], 512,
                                           rider=gather(["w_in"], 1) if l == 0 else None)
        if l == 0:
            wts[1].update(whole(g_in=got[0]))
        gu, h2, *got = _normproj_fwd("ffnin_fwd", x1, sh2, sc2, vec(W["norm_ffn_g"][l]), wts[l]["w_ffn_in"], 512,
                                     rider=gather(["w_ffn_in"], l + 1) if more else None, w_t=True, out_dtype=BF16)
        if more:
            wts[l + 1].update(whole(g_fi=got[0]))
        x2, f, *got = _ffn_out_fwd(x1, gu, g2, wts[l]["w_ffn_out"], 256,
                                   rider=gather(["w_ffn_out", "w_out"], l + 1) if more else None)
        if more:
            wts[l + 1].update(whole(g_fo=got[0], g_out=got[1]))
        saved.append((xc, proj, h, convout, qkv, gb, o, ss, inv, mix, ycat, x1, gu, h2, f))
        xc = x2

    dx, loss_row, d_gfin = _loss_bwd(xc, tgt, vec(W["final_norm_g"]), 512)
    loss = lax.psum(loss_row[0, 0], ("x", "y", "c"))

    big_out = {k: None for k in _BIG}
    dmod, small = [None] * DEPTH, [None] * DEPTH
    blocks = lambda g: g.reshape(N_DEV, -1, g.shape[-1])
    pair_tm = {"w_in": 512, "w_out": 128, "w_ffn_in": 704, "w_ffn_out": 352}
    sum_tm = {"w_in": 256, "w_out": 128, "w_ffn_in": 176, "w_ffn_out": 176}
    turned = lambda a: a.transpose(0, 2, 1)
    wmv = {k: (W[k], M[k], V[k]) for k in _BIG}
    wmv["w_ffn_in"] = tuple(turned(a) for a in wmv["w_ffn_in"])

    def pair_sum(k, g, from_sib):
        return _pair_add("pair_add_" + k, g, from_sib, pair_tm[k]).reshape(from_sib.shape), (g, from_sib)

    def finish(k, layer, own, r):
        if k == "w_in":
            g = _reduce_sum("reduce_" + k, own, r, sum_tm[k])[:, :IN_COLS // N_DEV]
            big_out[k] = _adam_layer("adam_" + k, g, *wmv[k], 256, layer, big_out[k])
        else:
            big_out[k] = _reduce_adam("reduce_adam_" + k, own, r, *wmv[k], sum_tm[k], layer, big_out[k])

    above = None
    for l in reversed(range(DEPTH)):
        xc, proj, h, convout, qkv, gb, o, ss, inv, mix, ycat, x1, gu, h2, f = saved[l]
        sh1, sc1, g1, sh2, sc2, g2 = [mod[l, i] for i in range(6)]
        gm, gf = vec(W["norm_mix_g"][l]), vec(W["norm_ffn_g"][l])
        bb, lg, lb = vec(W["conf_dw_b"][l]), vec(W["conf_ln_g"][l]), vec(W["conf_ln_b"][l])
        dng = vec(W["dn_norm_g"][l])
        wl = wts[l]

        dgu, s, df, d_g2, *got = _ffn_out_bwd(dx, gu, f, g2, wl["w_ffn_out"], 256,
                                              rider=_chip_rider([above[1]]) if above else None)
        if above:
            finish("w_in", above[0], above[2], got[0])
        gw_fo = blocks(_wgrad("wgrad_ffn_out", s, df, 1408, 1024))
        dx1, d_sh2, d_sc2, d_gf, *got = _normproj_bwd("ffnin_bwd", x1, dgu, dx, sc2, gf, wl["w_ffn_in"], 512,
                                                      rider=_pair_rider([gw_fo]), w_t=True)
        fo16, fo_own = pair_sum("w_ffn_out", gw_fo, got[0])
        gw_fi = blocks(_wgrad("wgrad_ffn_in", dgu, h2, 1408, 1024))
        dmix, dy_ab, do, dz, d_g1, d_dng, *got = _outproj_bwd(dx1, mix, o, proj, g1, dng, wl["w_out"], 512,
                                                               rider=_pair_rider([gw_fi]))
        fi16, fi_own = pair_sum("w_ffn_in", gw_fi, got[0])
        gw_out = blocks(_wgrad("wgrad_out", ycat, dmix, 512, 2048))
        dqkv, *got = _delta_bwd(qkv, gb, ss, inv, do, DELTA_NB,
                                rider=_join_riders(_chip_rider([fi16]), _pair_rider([gw_out], sem0=3)))
        finish("w_ffn_in", l, fi_own, got[0])
        out16, out_own = pair_sum("w_out", gw_out, got[1])
        dco, da_b, dblk, d_bb, d_lg, d_lb, d_alog, d_dtb = _stage2_bwd(
            proj, convout, dy_ab, _cols(dqkv, D_DN, 0), _cols(dqkv, D_DN, 1), _cols(dqkv, D_DN, 2),
            _cols(dqkv, 128, 3 * D_DN // 128), bb, lg, lb, alog[l:l + 1], dtb[l:l + 1], 512)
        dproj, d_wa, d_wb, d_wc, *got = _conv_bwd(proj, dco, da_b, dz, dblk, wa[l], wb[l], wc[l], 512,
                                                  rider=_chip_rider([out16, fo16]))
        finish("w_out", l, out_own, got[0])
        finish("w_ffn_out", l, fo_own, got[1])
        gw_in = _wgrad_in_blocks("wgrad_in", h, dproj, 512, 1024)
        dx, d_sh1, d_sc1, d_gm, *got = _normproj_bwd("inproj_bwd", xc, dproj, dx1, sc1, gm, wl["w_in"], 512,
                                                     rider=_pair_rider([gw_in]) if l else None)
        above = (l,) + pair_sum("w_in", gw_in, got[0] if l else _run_rider("pair_exchange", _pair_rider([gw_in]))[0])

        dmod[l] = jnp.concatenate([d_sh1, d_sc1, d_g1, d_sh2, d_sc2, d_g2], axis=1)
        small[l] = dict(norm_mix_g=d_gm, norm_ffn_g=d_gf, conv_a_w=d_wa[:KA], conf_dw_w=d_wb[:KB], conf_dw_b=d_bb,
                        conf_ln_g=d_lg, conf_ln_b=d_lb, dn_conv_w=d_wc[:KC], dn_a_log=d_alog, dn_dt_bias=d_dtb,
                        dn_norm_g=d_dng)


    names = ("norm_mix_g", "norm_ffn_g", "conv_a_w", "conf_dw_w", "conf_dw_b", "conf_ln_g", "conf_ln_b", "dn_conv_w",
             "dn_a_log", "dn_dt_bias", "dn_norm_g")
    pieces = [jnp.stack(dmod)] + [jnp.stack([small[l][k] for l in range(DEPTH)]) for k in names] + [d_gfin]
    shapes = [p.shape for p in pieces]
    every, total, in_recv = _small_allgather("gather_small_grads", _pack(pieces), rider=_chip_rider([above[1]]))
    finish("w_in", above[0], above[2], in_recv)
    tot = dict(zip(("dmod",) + names + ("final_norm_g",), _unpack(total, shapes)))
    dmod_all = _unpack(every, shapes[:1])[0]

    grads = {}
    grads["b_ada"] = tot["dmod"].reshape(DEPTH, 6 * D)
    for k in ("norm_mix_g", "norm_ffn_g", "conf_dw_b", "conf_ln_g", "conf_ln_b", "dn_norm_g"):
        grads[k] = tot[k].reshape(W[k].shape)
    grads["dn_a_log"] = tot["dn_a_log"].reshape(DEPTH, 128)[:, :HEADS]
    grads["dn_dt_bias"] = tot["dn_dt_bias"].reshape(DEPTH, 128)[:, :HEADS]
    grads["final_norm_g"] = tot["final_norm_g"].reshape(D)
    for k in conv_names:
        nloc = W[k].shape[2]
        grads[k] = lax.dynamic_slice_in_dim(tot[k], me * nloc, nloc, axis=2)

    dm = lax.dynamic_slice_in_dim(dmod_all.reshape(N_DEV, DEPTH, 6 * D), me * nada, nada, axis=2)
    pad16 = lambda a: jnp.pad(a, ((0, 16 - N_DEV), (0, 0))).astype(BF16)
    g_ada = _wgrad("wgrad_ada", pad16(act_all), pad16(dm.reshape(N_DEV, DEPTH * nada)), 256, 16)
    grads["w_ada"] = g_ada.reshape(D, DEPTH, nada).transpose(1, 0, 2)

    delta, new_m, new_v = {}, {}, {}
    r2 = lambda a: a.reshape(DEPTH * D, nada)
    d_, m_, v_ = _adam_call("adam_ada", r2(W["w_ada"]), r2(grads["w_ada"]), r2(M["w_ada"]), r2(V["w_ada"]), 512)
    delta["w_ada"], new_m["w_ada"], new_v["w_ada"] = [t.reshape(W["w_ada"].shape) for t in (d_, m_, v_)]
    sshapes = [W[k].shape for k in _SMALL]
    d_, m_, v_ = _adam_call("adam_small", _pack([W[k] for k in _SMALL]), _pack([grads[k] for k in _SMALL]),
                            _pack([M[k] for k in _SMALL]), _pack([V[k] for k in _SMALL]), 4096)
    for dst, packed in ((delta, d_), (new_m, m_), (new_v, v_)):
        dst.update(zip(_SMALL, _unpack(packed, sshapes)))
    for k in _BIG:
        grads[k], delta[k], new_m[k], new_v[k] = [turned(a) for a in big_out[k]] if k == "w_ffn_in" else big_out[k]

    return (loss, dx[None], *[grads[k] for k in _WEIGHTS], *[delta[k] for k in _WEIGHTS],
            *[new_m[k] for k in _WEIGHTS], *[new_v[k] for k in _WEIGHTS])


def kernel(x, c, w_ada, b_ada, norm_mix_g, norm_ffn_g, w_in, conv_a_w, conf_dw_w, conf_dw_b, conf_ln_g, conf_ln_b, dn_conv_w, dn_a_log, dn_dt_bias, dn_norm_g, w_out, w_ffn_in, w_ffn_out, final_norm_g, loss_target, m_w_ada, m_b_ada, m_norm_mix_g, m_norm_ffn_g, m_w_in, m_conv_a_w, m_conf_dw_w, m_conf_dw_b, m_conf_ln_g, m_conf_ln_b, m_dn_conv_w, m_dn_a_log, m_dn_dt_bias, m_dn_norm_g, m_w_out, m_w_ffn_in, m_w_ffn_out, m_final_norm_g, v_w_ada, v_b_ada, v_norm_mix_g, v_norm_ffn_g, v_w_in, v_conv_a_w, v_conf_dw_w, v_conf_dw_b, v_conf_ln_g, v_conf_ln_b, v_dn_conv_w, v_dn_a_log, v_dn_dt_bias, v_dn_norm_g, v_w_out, v_w_ffn_in, v_w_ffn_out, v_final_norm_g):
    a = dict(locals())
    W = {k: a[k] for k in _WEIGHTS}
    M = {k: a["m_" + k] for k in _WEIGHTS}
    V = {k: a["v_" + k] for k in _WEIGHTS}
    return _step(x, c, loss_target, W, M, V)
```
